```python
import math
import jax
import jax.numpy as jnp
from jax import lax
import numpy as np

D_MODEL = 1024
BATCH = 16
SEQ = 256
DEPTH = 4
DEC_BATCH = 8
DEC_SEQ = 1024
PAST_LEN = 256

GRID_W = 64
N_MIXERS = 3
N_CONV_LAYERS = (DEPTH + 2) // 3
N_ATTN_LAYERS = (DEPTH + 1) // 3
N_MLSTM_LAYERS = DEPTH // 3
EPS = 1e-6

CONV_WIDTH = 31
CONV_PAD = CONV_WIDTH // 2

HEAD_DIM = 128
N_HEADS = D_MODEL // HEAD_DIM
N_KV_HEADS = 2
GQA_GROUP = N_HEADS // N_KV_HEADS
Q_DIM = N_HEADS * HEAD_DIM
KV_DIM = N_KV_HEADS * HEAD_DIM
QKV_DIM = Q_DIM + 2 * KV_DIM
AXIS_DIM = HEAD_DIM // 2
ROPE_THETA = 10000.0
Q_BLOCK = 128

M_HEADS = 4
M_HEAD_DIM = D_MODEL // M_HEADS
M_CHUNK = 64
M_IN_DIM = 4 * D_MODEL + 4 * M_HEADS

N_GROUPS = 4
EXPERTS_PER_GROUP = 8
N_EXPERTS = N_GROUPS * EXPERTS_PER_GROUP
TOP_K = 2
D_EXPERT = 512
EXPERT_BLOCK = 128

kernel_name = "hybrid_conv_gqa_mlstm_hmoe_dit_step"

F32 = jnp.float32


def rms_norm(x, g):
    xf = x.astype(F32)
    y = xf * lax.rsqrt(jnp.mean(xf * xf, axis=-1, keepdims=True) + EPS)
    return (y * g.astype(F32)).astype(x.dtype)


def layer_norm(x, g, b):
    xf = x.astype(F32)
    mu = jnp.mean(xf, axis=-1, keepdims=True)
    var = jnp.mean(jnp.square(xf - mu), axis=-1, keepdims=True)
    return ((xf - mu) * lax.rsqrt(var + EPS) * g.astype(F32) + b.astype(F32)).astype(x.dtype)


def ada_modulation(cvec, w, b):
    m = jax.nn.silu(cvec) @ w + b
    return m.reshape(cvec.shape[0], 6, 1, D_MODEL)


def conformer_conv(h, w_in, w_dw, b_dw, ln_g, ln_b, w_out):
    a, g = jnp.split(h @ w_in, 2, axis=-1)
    u = a * jax.nn.sigmoid(g)
    u = lax.conv_general_dilated(
        u, w_dw[:, None, :].astype(u.dtype), (1,), [(CONV_PAD, CONV_PAD)],
        dimension_numbers=("NWC", "WIO", "NWC"), feature_group_count=D_MODEL) + b_dw
    u = jax.nn.silu(layer_norm(u, ln_g, ln_b))
    return u @ w_out


def axial_rope_tables(n_tok):
    rows = n_tok // GRID_W
    row = jnp.repeat(jnp.arange(rows, dtype=F32), GRID_W)
    col = jnp.tile(jnp.arange(GRID_W, dtype=F32), rows)
    freqs = jnp.power(ROPE_THETA, -jnp.arange(AXIS_DIM // 2, dtype=F32) * 2.0 / AXIS_DIM)
    ang_r = row[:, None] * freqs[None, :]
    ang_c = col[:, None] * freqs[None, :]
    ang = jnp.concatenate([ang_r, ang_r, ang_c, ang_c], axis=-1)
    return jnp.cos(ang), jnp.sin(ang)


def apply_axial_rope(x, cos, sin):
    xf = x.astype(F32)
    r1, r2, c1, c2 = jnp.split(xf, 4, axis=-1)
    rot = jnp.concatenate([-r2, r1, -c2, c1], axis=-1)
    return (xf * cos[None, :, None, :] + rot * sin[None, :, None, :]).astype(x.dtype)


def attn_project(h, w_qkv, q_g, k_g):
    B, L, _ = h.shape
    qkv = h @ w_qkv
    q = qkv[..., :Q_DIM].reshape(B, L, N_HEADS, HEAD_DIM)
    k = qkv[..., Q_DIM:Q_DIM + KV_DIM].reshape(B, L, N_KV_HEADS, HEAD_DIM)
    v = qkv[..., Q_DIM + KV_DIM:].reshape(B, L, N_KV_HEADS, HEAD_DIM)
    return rms_norm(q, q_g), rms_norm(k, k_g), v


def block_attention(q, k, v):
    B, Lq, _, _ = q.shape
    nb = Lq // Q_BLOCK
    qb = q.reshape(B, nb, Q_BLOCK, N_KV_HEADS, GQA_GROUP, HEAD_DIM).transpose(1, 0, 2, 3, 4, 5)
    scale = HEAD_DIM ** -0.5

    def one(qi):
        s = jnp.einsum("bqkgd,bskd->bkgqs", qi, k).astype(F32) * scale
        p = jax.nn.softmax(s, axis=-1).astype(v.dtype)
        return jnp.einsum("bkgqs,bskd->bqkgd", p, v)

    o = lax.map(one, qb)
    return o.transpose(1, 0, 2, 3, 4, 5).reshape(B, Lq, Q_DIM)


def to_chunks(a):
    B, H, L = a.shape[:3]
    a = a.reshape(B, H, L // M_CHUNK, M_CHUNK, *a.shape[3:])
    return jnp.moveaxis(a, 2, 0)


def mlstm_chunk_scan(q, k, v, ig, lf, C0, n0, m0):
    B, H, L, _ = q.shape
    tri = jnp.tril(jnp.ones((M_CHUNK, M_CHUNK), dtype=bool))

    def step(carry, xs):
        C, n, m = carry
        qc, kc, vc, ic, fc = xs
        b = jnp.cumsum(fc, axis=-1)
        log_d = jnp.where(tri, b[..., :, None] - b[..., None, :] + ic[..., None, :], -jnp.inf)
        log_inter = b + m[..., None]
        m_row = jnp.maximum(log_inter, jnp.max(log_d, axis=-1))
        a_inter = jnp.exp(log_inter - m_row)
        s = jnp.einsum("bhqd,bhsd->bhqs", qc, kc) * jnp.exp(log_d - m_row[..., None])
        num = a_inter[..., None] * jnp.einsum("bhqd,bhde->bhqe", qc, C) + jnp.einsum("bhqs,bhse->bhqe", s, vc)
        den = a_inter * jnp.einsum("bhqd,bhd->bhq", qc, n) + jnp.sum(s, axis=-1)
        h = num / jnp.maximum(jnp.abs(den), jnp.exp(-m_row))[..., None]
        b_last = b[..., -1]
        log_w = b_last[..., None] - b + ic
        m_new = jnp.maximum(b_last + m, jnp.max(log_w, axis=-1))
        w = jnp.exp(log_w - m_new[..., None])
        decay = jnp.exp(b_last + m - m_new)
        C_new = decay[..., None, None] * C + jnp.einsum("bhsd,bhse->bhde", kc * w[..., None], vc)
        n_new = decay[..., None] * n + jnp.einsum("bhs,bhsd->bhd", w, kc)
        return (C_new, n_new, m_new), h

    xs = tuple(to_chunks(a) for a in (q, k, v, ig, lf))
    (C, n, m), hs = lax.scan(step, (C0, n0, m0), xs)
    h = jnp.moveaxis(hs, 0, 2).reshape(B, H, L, v.shape[-1])
    return h, (C, n, m)


def mlstm_mixer(h, w_in, b_gate, norm_g, w_out, state_fwd, state_bwd):
    B, L, D = h.shape
    u = h @ w_in

    def heads(a):
        return a.reshape(B, L, M_HEADS, M_HEAD_DIM).transpose(0, 2, 1, 3).astype(F32)

    q = heads(u[..., :D])
    k = heads(u[..., D:2 * D]) * (M_HEAD_DIM ** -0.5)
    v = heads(u[..., 2 * D:3 * D])
    o = jax.nn.sigmoid(u[..., 3 * D:4 * D].astype(F32)).reshape(B, L, M_HEADS, M_HEAD_DIM)
    gates = (u[..., 4 * D:].astype(F32) + b_gate.astype(F32)).reshape(B, L, 4, M_HEADS).transpose(2, 0, 3, 1)
    i_f, f_f, i_b, f_b = gates[0], gates[1], gates[2], gates[3]
    h_f, st_f = mlstm_chunk_scan(q, k, v, i_f, jax.nn.log_sigmoid(f_f), *state_fwd)
    rev = lambda a: jnp.flip(a, axis=2)
    h_b, st_b = mlstm_chunk_scan(rev(q), rev(k), rev(v), rev(i_b), rev(jax.nn.log_sigmoid(f_b)), *state_bwd)
    hc = (h_f + rev(h_b)).transpose(0, 2, 1, 3)
    hc = rms_norm(o * hc, norm_g.reshape(M_HEADS, M_HEAD_DIM)).reshape(B, L, D).astype(h.dtype)
    return hc @ w_out, st_f, st_b


def routed_expert_mlps(x, expert_id, weights, w_gate, w_up, w_down):
    N, D = x.shape
    A = N * TOP_K
    n_blocks = -(-A // EXPERT_BLOCK) + N_EXPERTS
    P = n_blocks * EXPERT_BLOCK
    flat_e = expert_id.reshape(-1).astype(jnp.int32)
    flat_tok = jnp.repeat(jnp.arange(N, dtype=jnp.int32), TOP_K)
    flat_w = weights.reshape(-1)
    order = jnp.argsort(flat_e)
    e_sorted = flat_e[order]
    counts = jnp.bincount(flat_e, length=N_EXPERTS).astype(jnp.int32)
    padded = ((counts + EXPERT_BLOCK - 1) // EXPERT_BLOCK) * EXPERT_BLOCK
    pad_end = jnp.cumsum(padded)
    pad_start = pad_end - padded
    start = jnp.cumsum(counts) - counts
    dest = pad_start[e_sorted] + (jnp.arange(A, dtype=jnp.int32) - start[e_sorted])
    tok_buf = jnp.full((P,), N, jnp.int32).at[dest].set(flat_tok[order])
    w_buf = jnp.zeros((P,), F32).at[dest].set(flat_w[order])
    block_e = jnp.clip(jnp.searchsorted(pad_end, jnp.arange(n_blocks, dtype=jnp.int32) * EXPERT_BLOCK, side="right"),
                       0, N_EXPERTS - 1)
    x_pad = jnp.concatenate([x, jnp.zeros((1, D), x.dtype)], axis=0)

    def one(args):
        toks, e = args
        xb = x_pad[toks]
        hb = jax.nn.silu(xb @ w_gate[e]) * (xb @ w_up[e])
        return hb @ w_down[e]

    yb = lax.map(one, (tok_buf.reshape(n_blocks, EXPERT_BLOCK), block_e)).reshape(P, D)
    y = jnp.zeros((N + 1, D), x.dtype).at[tok_buf].add(yb * w_buf[:, None].astype(x.dtype))
    return y[:N]


def hier_moe(h, w_group, b_group, w_router, b_router, w_gate, w_up, w_down):
    B, L, D = h.shape
    N = B * L
    x = h.reshape(N, D)
    g_logits = (x @ w_group).astype(F32) + b_group.astype(F32)
    g_prob = jax.nn.softmax(g_logits, axis=-1)
    g_idx = jnp.argmax(g_logits, axis=-1)
    g_w = jnp.take_along_axis(g_prob, g_idx[:, None], axis=1)[:, 0]
    e_logits = ((x @ w_router).astype(F32) + b_router.astype(F32)).reshape(N, N_GROUPS, EXPERTS_PER_GROUP)
    e_logits = jnp.take_along_axis(e_logits, g_idx[:, None, None], axis=1)[:, 0]
    top_p, top_i = lax.top_k(jax.nn.softmax(e_logits, axis=-1), TOP_K)
    top_p = top_p / jnp.sum(top_p, axis=-1, keepdims=True)
    expert_id = g_idx[:, None] * EXPERTS_PER_GROUP + top_i
    weights = g_w[:, None] * top_p
    return routed_expert_mlps(x, expert_id, weights, w_gate, w_up, w_down).reshape(B, L, D)


def setup_inputs(seed: int = 0) -> dict:
    key = jax.random.key(seed)
    ks = iter(jax.random.split(key, 64))

    def nrm(shape, scale=1.0):
        return jax.random.normal(next(ks), shape, F32) * scale

    D = D_MODEL
    inv = D ** -0.5
    f_bias = jnp.linspace(3.0, 6.0, M_HEADS, dtype=F32)
    zero_h = jnp.zeros((M_HEADS,), F32)
    gate_offset = jnp.stack([zero_h, f_bias, zero_h, f_bias], axis=0)
    return {
        "x_prompt": nrm((BATCH, SEQ, D)),
        "x_sample": nrm((DEC_BATCH, DEC_SEQ, D)),
        "cache_attn_k": nrm((DEC_BATCH, N_ATTN_LAYERS, PAST_LEN, N_KV_HEADS, HEAD_DIM)),
        "cache_attn_v": nrm((DEC_BATCH, N_ATTN_LAYERS, PAST_LEN, N_KV_HEADS, HEAD_DIM)),
        "state_mlstm_C": nrm((DEC_BATCH, N_MLSTM_LAYERS, 2, M_HEADS, M_HEAD_DIM, M_HEAD_DIM), 0.1),
        "state_mlstm_n": nrm((DEC_BATCH, N_MLSTM_LAYERS, 2, M_HEADS, M_HEAD_DIM), 0.1),
        "state_mlstm_m": nrm((DEC_BATCH, N_MLSTM_LAYERS, 2, M_HEADS)),
        "c": nrm((DEC_BATCH, D)),
        "c_ctx": nrm((D,)),
        "ada_w": nrm((DEPTH, D, 6 * D), 0.5 * inv),
        "ada_b": nrm((DEPTH, 6 * D), 0.02),
        "norm1_g": 1.0 + nrm((DEPTH, D), 0.05),
        "norm2_g": 1.0 + nrm((DEPTH, D), 0.05),
        "conv_w_in": nrm((N_CONV_LAYERS, D, 2 * D), inv),
        "conv_w_dw": nrm((N_CONV_LAYERS, CONV_WIDTH, D), CONV_WIDTH ** -0.5),
        "conv_b_dw": nrm((N_CONV_LAYERS, D), 0.02),
        "conv_ln_g": 1.0 + nrm((N_CONV_LAYERS, D), 0.05),
        "conv_ln_b": nrm((N_CONV_LAYERS, D), 0.02),
        "conv_w_out": nrm((N_CONV_LAYERS, D, D), inv),
        "attn_w_qkv": nrm((N_ATTN_LAYERS, D, QKV_DIM), inv),
        "attn_q_norm": 1.0 + nrm((N_ATTN_LAYERS, HEAD_DIM), 0.05),
        "attn_k_norm": 1.0 + nrm((N_ATTN_LAYERS, HEAD_DIM), 0.05),
        "attn_w_o": nrm((N_ATTN_LAYERS, Q_DIM, D), Q_DIM ** -0.5),
        "mlstm_w_in": nrm((N_MLSTM_LAYERS, D, M_IN_DIM), inv),
        "mlstm_b_gate": (nrm((N_MLSTM_LAYERS, 4, M_HEADS), 0.1) + gate_offset[None]).reshape(N_MLSTM_LAYERS, 4 * M_HEADS),
        "mlstm_norm_g": 1.0 + nrm((N_MLSTM_LAYERS, D), 0.05),
        "mlstm_w_out": nrm((N_MLSTM_LAYERS, D, D), inv),
        "moe_w_group": nrm((DEPTH, D, N_GROUPS), inv),
        "moe_b_group": nrm((DEPTH, N_GROUPS), 0.01),
        "moe_w_router": nrm((DEPTH, D, N_EXPERTS), inv),
        "moe_b_router": nrm((DEPTH, N_EXPERTS), 0.01),
        "moe_w_gate": nrm((DEPTH, N_EXPERTS, D, D_EXPERT), inv),
        "moe_w_up": nrm((DEPTH, N_EXPERTS, D, D_EXPERT), inv),
        "moe_w_down": nrm((DEPTH, N_EXPERTS, D_EXPERT, D), D_EXPERT ** -0.5),
        "final_norm_g": 1.0 + nrm((D,), 0.05),
    }


def reference(x_prompt, x_sample, cache_attn_k, cache_attn_v, state_mlstm_C, state_mlstm_n, state_mlstm_m,
              c, c_ctx, ada_w, ada_b, norm1_g, norm2_g,
              conv_w_in, conv_w_dw, conv_b_dw, conv_ln_g, conv_ln_b, conv_w_out,
              attn_w_qkv, attn_q_norm, attn_k_norm, attn_w_o,
              mlstm_w_in, mlstm_b_gate, mlstm_norm_g, mlstm_w_out,
              moe_w_group, moe_b_group, moe_w_router, moe_b_router, moe_w_gate, moe_w_up, moe_w_down,
              final_norm_g):
    yp, ys = x_prompt, x_sample
    Bp = x_prompt.shape[0]
    new_k, new_v, new_C, new_n, new_m = [], [], [], [], []
    cos_s, sin_s = axial_rope_tables(x_sample.shape[1])
    for i in range(DEPTH):
        kind, slot = i % N_MIXERS, i // N_MIXERS
        mp = ada_modulation(c_ctx[None, :], ada_w[i], ada_b[i])
        ms = ada_modulation(c, ada_w[i], ada_b[i])
        hp = rms_norm(yp, norm1_g[i]) * (1.0 + mp[:, 1]) + mp[:, 0]
        hs = rms_norm(ys, norm1_g[i]) * (1.0 + ms[:, 1]) + ms[:, 0]
        if kind == 0:
            cw = (conv_w_in[slot], conv_w_dw[slot], conv_b_dw[slot], conv_ln_g[slot], conv_ln_b[slot], conv_w_out[slot])
            op = conformer_conv(hp, *cw)
            os_ = conformer_conv(hs, *cw)
        elif kind == 1:
            qp, kp, vp = attn_project(hp, attn_w_qkv[slot], attn_q_norm[slot], attn_k_norm[slot])
            op = block_attention(qp, kp, vp) @ attn_w_o[slot]
            new_k.append(kp)
            new_v.append(vp)
            qs, ks_, vs = attn_project(hs, attn_w_qkv[slot], attn_q_norm[slot], attn_k_norm[slot])
            qs = apply_axial_rope(qs, cos_s, sin_s)
            ks_ = apply_axial_rope(ks_, cos_s, sin_s)
            k_all = jnp.concatenate([ks_, cache_attn_k[:, slot].astype(ks_.dtype)], axis=1)
            v_all = jnp.concatenate([vs, cache_attn_v[:, slot].astype(vs.dtype)], axis=1)
            os_ = block_attention(qs, k_all, v_all) @ attn_w_o[slot]
        else:
            mw = (mlstm_w_in[slot], mlstm_b_gate[slot], mlstm_norm_g[slot], mlstm_w_out[slot])
            zero = (jnp.zeros((Bp, M_HEADS, M_HEAD_DIM, M_HEAD_DIM), F32),
                    jnp.zeros((Bp, M_HEADS, M_HEAD_DIM), F32),
                    jnp.zeros((Bp, M_HEADS), F32))
            op, st_f, st_b = mlstm_mixer(hp, *mw, zero, zero)
            new_C.append(jnp.stack([st_f[0], st_b[0]], axis=1))
            new_n.append(jnp.stack([st_f[1], st_b[1]], axis=1))
            new_m.append(jnp.stack([st_f[2], st_b[2]], axis=1))
            cached_f = (state_mlstm_C[:, slot, 0].astype(F32), state_mlstm_n[:, slot, 0].astype(F32),
                        state_mlstm_m[:, slot, 0].astype(F32))
            cached_b = (state_mlstm_C[:, slot, 1].astype(F32), state_mlstm_n[:, slot, 1].astype(F32),
                        state_mlstm_m[:, slot, 1].astype(F32))
            os_, _, _ = mlstm_mixer(hs, *mw, cached_f, cached_b)
        yp = yp + mp[:, 2] * op
        ys = ys + ms[:, 2] * os_
        moe_w = (moe_w_group[i], moe_b_group[i], moe_w_router[i], moe_b_router[i],
                 moe_w_gate[i], moe_w_up[i], moe_w_down[i])
        hp2 = rms_norm(yp, norm2_g[i]) * (1.0 + mp[:, 4]) + mp[:, 3]
        hs2 = rms_norm(ys, norm2_g[i]) * (1.0 + ms[:, 4]) + ms[:, 3]
        yp = yp + mp[:, 5] * hier_moe(hp2, *moe_w)
        ys = ys + ms[:, 5] * hier_moe(hs2, *moe_w)
    y_prompt = rms_norm(yp, final_norm_g)
    y_sample = rms_norm(ys, final_norm_g)
    new_attn_k = jnp.stack(new_k, axis=1)
    new_attn_v = jnp.stack(new_v, axis=1)
    new_mlstm_C = jnp.stack(new_C, axis=1)
    new_mlstm_n = jnp.stack(new_n, axis=1)
    new_mlstm_m = jnp.stack(new_m, axis=1)
    return (y_prompt, y_sample, new_attn_k, new_attn_v, new_mlstm_C, new_mlstm_n, new_mlstm_m)
```

```python
import functools

import jax
import jax.numpy as jnp
import numpy as np
from jax import lax
from jax.experimental import pallas as pl
from jax.experimental.pallas import tpu as pltpu

F32 = jnp.float32
BF16 = jnp.bfloat16
I32 = jnp.int32

D = 1024
BATCH, SEQ = 16, 256
DEC_BATCH, DEC_SEQ = 8, 1024
PAST_LEN = 256
DEPTH = 4
GRID_W = 64
EPS = 1e-6
CONV_WIDTH = 31
CONV_PAD = CONV_WIDTH // 2
HEAD_DIM = 128
N_HEADS = 8
N_KV_HEADS = 2
GQA_GROUP = N_HEADS // N_KV_HEADS
Q_DIM = N_HEADS * HEAD_DIM
KV_DIM = N_KV_HEADS * HEAD_DIM
QKV_DIM = Q_DIM + 2 * KV_DIM
ROPE_THETA = 10000.0
M_HEADS = 4
M_HEAD_DIM = D // M_HEADS
M_CHUNK = 64
N_GROUPS = 4
EXPERTS_PER_GROUP = 8
N_EXPERTS = N_GROUPS * EXPERTS_PER_GROUP
TOP_K = 2
D_EXPERT = 512

NP_TOK = BATCH * SEQ
NS_TOK = DEC_BATCH * DEC_SEQ
N_TOK = NP_TOK + NS_TOK
TM = 256
NB = N_TOK // TM
NBP = NP_TOK // TM
BLK_PER_DEC = DEC_SEQ // TM
MOD_ROWS = 16
HALO = 16
LANE = 128
SUBLANE = 8

N_ASSIGN = N_TOK * TOP_K
EBLK = 128
N_EBLK = N_ASSIGN // EBLK + N_EXPERTS
P_SLOTS = N_EBLK * EBLK
N_PAD_SLOTS = P_SLOTS - N_ASSIGN

VMEM_LIMIT = 56 * 1024 * 1024


def _block_tables():
    j = np.arange(NB)
    is_p = j < NBP
    mod_row = np.where(is_p, 0, 1 + (j - NBP) // BLK_PER_DEC)
    rope_idx = np.where(is_p, 0, 1 + (j - NBP) % BLK_PER_DEC)
    first = np.where(is_p, 1, ((j - NBP) % BLK_PER_DEC == 0).astype(np.int64))
    last = np.where(is_p, 1, ((j - NBP) % BLK_PER_DEC == BLK_PER_DEC - 1).astype(np.int64))
    return (mod_row.astype(np.int32), rope_idx.astype(np.int32), first.astype(np.int32), last.astype(np.int32))


_MOD_ROW, _ROPE_IDX, _SEQ_FIRST, _SEQ_LAST = _block_tables()


def _cparams(n_axes=1):
    return pltpu.CompilerParams(dimension_semantics=("arbitrary",) * n_axes, vmem_limit_bytes=VMEM_LIMIT)


def _sigmoid(x):
    return 1.0 / (1.0 + jnp.exp(-x))


def _rms(x, g):
    return x * lax.rsqrt(jnp.mean(x * x, axis=-1, keepdims=True) + EPS) * g


def _norm_mod(y, g, mod, which):
    shift = mod[3 * which:3 * which + 1]
    scale = mod[3 * which + 1:3 * which + 2]
    return _rms(y, g) * (1.0 + scale) + shift


def _ada_kernel(c_ref, w_ref, b_ref, o_ref):
    c = c_ref[...]
    s = c * _sigmoid(c)
    o_ref[0] = jnp.dot(s.astype(BF16), w_ref[0].astype(BF16), preferred_element_type=F32) + b_ref[0]


def _ada_all(cvec, ada_w, ada_b):
    tn = 1536
    out = pl.pallas_call(
        _ada_kernel,
        out_shape=jax.ShapeDtypeStruct((DEPTH, MOD_ROWS, 6 * D), F32),
        grid=(DEPTH, 6 * D // tn),
        in_specs=[
            pl.BlockSpec((MOD_ROWS, D), lambda l, n: (0, 0)),
            pl.BlockSpec((1, D, tn), lambda l, n: (l, 0, n)),
            pl.BlockSpec((1, 1, tn), lambda l, n: (l, 0, n)),
        ],
        out_specs=pl.BlockSpec((1, MOD_ROWS, tn), lambda l, n: (l, 0, n)),
        compiler_params=_cparams(2),
        name="ada_mod",
    )(cvec, ada_w, ada_b.reshape(DEPTH, 1, 6 * D))
    return out.reshape(DEPTH * MOD_ROWS, 6, D)


def _tok_spec(width):
    return pl.BlockSpec((TM, width), lambda j, *_: (j, 0))


def _mod_spec():
    return pl.BlockSpec((1, 6, D), lambda j, mr, *_: (mr[j], 0, 0))


def _full_spec(shape):
    nd = len(shape)
    return pl.BlockSpec(shape, lambda j, *_: (0,) * nd)


def _conv_in_kernel(mr_ref, y_ref, mod_ref, g_ref, w_ref, u_ref):
    h = _norm_mod(y_ref[...], g_ref[...], mod_ref[0], 0)
    ag = jnp.dot(h.astype(BF16), w_ref[...], preferred_element_type=F32)
    u_ref[...] = ag[:, :D] * _sigmoid(ag[:, D:])


def _conv_in(y, mods, mrow, g1, w_in):
    return pl.pallas_call(
        _conv_in_kernel,
        out_shape=jax.ShapeDtypeStruct((N_TOK, D), F32),
        grid_spec=pltpu.PrefetchScalarGridSpec(
            num_scalar_prefetch=1, grid=(NB,),
            in_specs=[_tok_spec(D), _mod_spec(), _full_spec((1, D)), _full_spec((D, 2 * D))],
            out_specs=_tok_spec(D)),
        compiler_params=_cparams(),
        name="conv_in",
    )(mrow, y, mods, g1, w_in)


def _conv_main_kernel(mr_ref, first_ref, last_ref, u_ref, up_ref, un_ref, wdw_ref, bdw_ref, lg_ref, lb_ref,
                      wout_ref, y_ref, mod_ref, o_ref, ext_ref, acc_ref):
    j = pl.program_id(0)
    zero = jnp.zeros((HALO, D), F32)
    ext_ref[0:HALO, :] = jnp.where(first_ref[j] == 1, zero, up_ref[...])
    ext_ref[HALO:HALO + TM, :] = u_ref[...]
    ext_ref[HALO + TM:2 * HALO + TM, :] = jnp.where(last_ref[j] == 1, zero, un_ref[...])

    off0 = HALO - CONV_PAD
    def strip(ci, carry):
        cs = pl.ds(pl.multiple_of(ci * LANE, LANE), LANE)
        wk = [jnp.broadcast_to(wdw_ref[k:k + 1, cs], (SUBLANE, LANE)) for k in range(CONV_WIDTH)]
        bias = jnp.broadcast_to(bdw_ref[:, cs], (SUBLANE, LANE))
        for base in range(0, TM, SUBLANE):
            acc = bias
            for k in range(CONV_WIDTH):
                acc = acc + ext_ref[base + off0 + k:base + off0 + k + SUBLANE, cs] * wk[k]
            acc_ref[base:base + SUBLANE, cs] = acc
        return carry

    lax.fori_loop(0, D // LANE, strip, 0)

    c = acc_ref[...]
    mu = jnp.mean(c, axis=-1, keepdims=True)
    cc = c - mu
    var = jnp.mean(cc * cc, axis=-1, keepdims=True)
    z = cc * lax.rsqrt(var + EPS) * lg_ref[...] + lb_ref[...]
    z = z * _sigmoid(z)
    out = jnp.dot(z.astype(BF16), wout_ref[...], preferred_element_type=F32)
    o_ref[...] = y_ref[...] + mod_ref[0][2:3] * out


def _conv_main(u, y, mods, mrow, w_dw, b_dw, ln_g, ln_b, w_out):
    nh = N_TOK // HALO
    per = TM // HALO
    return pl.pallas_call(
        _conv_main_kernel,
        out_shape=jax.ShapeDtypeStruct((N_TOK, D), F32),
        grid_spec=pltpu.PrefetchScalarGridSpec(
            num_scalar_prefetch=3, grid=(NB,),
            in_specs=[
                _tok_spec(D),
                pl.BlockSpec((HALO, D), lambda j, *_: (jnp.maximum(j * per - 1, 0), 0)),
                pl.BlockSpec((HALO, D), lambda j, *_: (jnp.minimum((j + 1) * per, nh - 1), 0)),
                _full_spec((CONV_WIDTH + 1, D)), _full_spec((1, D)), _full_spec((1, D)), _full_spec((1, D)),
                _full_spec((D, D)), _tok_spec(D), _mod_spec(),
            ],
            out_specs=_tok_spec(D),
            scratch_shapes=[pltpu.VMEM((TM + 2 * HALO, D), F32), pltpu.VMEM((TM, D), F32)]),
        compiler_params=_cparams(),
        name="conv_main",
    )(mrow, jnp.asarray(_SEQ_FIRST), jnp.asarray(_SEQ_LAST), u, u, u, w_dw, b_dw, ln_g, ln_b, w_out, y, mods)


def _rope_angles():
    rows = DEC_SEQ // GRID_W
    row = jnp.repeat(jnp.arange(rows, dtype=F32), GRID_W)
    col = jnp.tile(jnp.arange(GRID_W, dtype=F32), rows)
    axis_dim = HEAD_DIM // 2
    freqs = jnp.power(ROPE_THETA, -jnp.arange(axis_dim // 2, dtype=F32) * 2.0 / axis_dim)
    ang_r = row[:, None] * freqs[None, :]
    ang_c = col[:, None] * freqs[None, :]
    return jnp.concatenate([ang_r, ang_r, ang_c, ang_c], axis=-1)


def _rope_blocks():
    ang = _rope_angles()
    cos, sin = jnp.cos(ang), jnp.sin(ang)
    lane = np.arange(HEAD_DIM)
    lo = jnp.asarray(((lane % (HEAD_DIM // 2)) < HEAD_DIM // 4).astype(np.float32))
    sin_a = -sin * lo[None, :]
    sin_b = sin * (1.0 - lo)[None, :]
    nblk = DEC_SEQ // TM
    ident = jnp.ones((1, TM, HEAD_DIM), F32)
    zeros = jnp.zeros((1, TM, HEAD_DIM), F32)
    cos_t = jnp.concatenate([ident, cos.reshape(nblk, TM, HEAD_DIM)], axis=0)
    sa_t = jnp.concatenate([zeros, sin_a.reshape(nblk, TM, HEAD_DIM)], axis=0)
    sb_t = jnp.concatenate([zeros, sin_b.reshape(nblk, TM, HEAD_DIM)], axis=0)
    return cos_t, sa_t, sb_t


def _attn_qkv_kernel(mr_ref, ri_ref, y_ref, mod_ref, g_ref, w_ref, qg_ref, kg_ref, cos_ref, sa_ref, sb_ref,
                     q_ref, kb_ref, vb_ref, kf_ref, vf_ref):
    h = _norm_mod(y_ref[...], g_ref[...], mod_ref[0], 0)
    qkv = jnp.dot(h.astype(BF16), w_ref[...], preferred_element_type=F32)
    cos, sa, sb = cos_ref[0], sa_ref[0], sb_ref[0]
    quarter = HEAD_DIM // 4

    def head(x, g):
        xn = _rms(x, g)
        return xn * cos + pltpu.roll(xn, HEAD_DIM - quarter, 1) * sa + pltpu.roll(xn, quarter, 1) * sb

    scale = HEAD_DIM ** -0.5
    for hd in range(N_HEADS):
        sl = slice(hd * HEAD_DIM, (hd + 1) * HEAD_DIM)
        q_ref[:, sl] = (head(qkv[:, sl], qg_ref[...]) * scale).astype(BF16)
    for kv in range(N_KV_HEADS):
        sl = slice(kv * HEAD_DIM, (kv + 1) * HEAD_DIM)
        kr = head(qkv[:, Q_DIM + kv * HEAD_DIM:Q_DIM + (kv + 1) * HEAD_DIM], kg_ref[...])
        kf_ref[:, sl] = kr
        kb_ref[:, sl] = kr.astype(BF16)
    v = qkv[:, Q_DIM + KV_DIM:]
    vf_ref[...] = v
    vb_ref[...] = v.astype(BF16)


def _attn_qkv(y, mods, mrow, g1, w_qkv, q_g, k_g, rope):
    cos_t, sa_t, sb_t = rope
    rspec = pl.BlockSpec((1, TM, HEAD_DIM), lambda j, mr, ri: (ri[j], 0, 0))
    return pl.pallas_call(
        _attn_qkv_kernel,
        out_shape=(jax.ShapeDtypeStruct((N_TOK, Q_DIM), BF16), jax.ShapeDtypeStruct((N_TOK, KV_DIM), BF16),
                   jax.ShapeDtypeStruct((N_TOK, KV_DIM), BF16), jax.ShapeDtypeStruct((N_TOK, KV_DIM), F32),
                   jax.ShapeDtypeStruct((N_TOK, KV_DIM), F32)),
        grid_spec=pltpu.PrefetchScalarGridSpec(
            num_scalar_prefetch=2, grid=(NB,),
            in_specs=[_tok_spec(D), _mod_spec(), _full_spec((1, D)), _full_spec((D, QKV_DIM)),
                      _full_spec((1, HEAD_DIM)), _full_spec((1, HEAD_DIM)), rspec, rspec, rspec],
            out_specs=(_tok_spec(Q_DIM), _tok_spec(KV_DIM), _tok_spec(KV_DIM), _tok_spec(KV_DIM), _tok_spec(KV_DIM))),
        compiler_params=_cparams(),
        name="attn_qkv",
    )(mrow, jnp.asarray(_ROPE_IDX), y, mods, g1, w_qkv, q_g, k_g, cos_t, sa_t, sb_t)


def _attn_heads(q, ks, vs, o_scr):
    nt = (((1,), (1,)), ((), ()))
    for hd in range(N_HEADS):
        g = hd // GQA_GROUP
        qh = q[:, hd * HEAD_DIM:(hd + 1) * HEAD_DIM]
        gs = slice(g * HEAD_DIM, (g + 1) * HEAD_DIM)
        ss = [lax.dot_general(qh, k[:, gs], nt, preferred_element_type=F32) for k in ks]
        m = functools.reduce(jnp.maximum, [jnp.max(s, axis=-1, keepdims=True) for s in ss])
        ps = [jnp.exp(s - m) for s in ss]
        l = functools.reduce(lambda a, b: a + b, [jnp.sum(p, axis=-1, keepdims=True) for p in ps])
        o = functools.reduce(lambda a, b: a + b,
                             [jnp.dot(p.astype(BF16), v[:, gs], preferred_element_type=F32) for p, v in zip(ps, vs)])
        o_scr[:, hd * HEAD_DIM:(hd + 1) * HEAD_DIM] = (o / l).astype(BF16)


def _attn_ctx_kernel(q_ref, k_ref, v_ref, wo_ref, y_ref, mod_ref, o_ref, o_scr):
    _attn_heads(q_ref[...], [k_ref[...]], [v_ref[...]], o_scr)
    out = jnp.dot(o_scr[...], wo_ref[...], preferred_element_type=F32)
    o_ref[...] = y_ref[...] + mod_ref[0][2:3] * out


def _attn_lat_kernel(q_ref, k_ref, v_ref, ck_ref, cv_ref, wo_ref, y_ref, mod_ref, o_ref, o_scr):
    _attn_heads(q_ref[...], [k_ref[...], ck_ref[0].astype(BF16)], [v_ref[...], cv_ref[0].astype(BF16)], o_scr)
    out = jnp.dot(o_scr[...], wo_ref[...], preferred_element_type=F32)
    o_ref[...] = y_ref[...] + mod_ref[0][2:3] * out


def _attention(q, kb, vb, cache_k, cache_v, w_o, y, mods, layer):
    y_ctx = pl.pallas_call(
        _attn_ctx_kernel,
        out_shape=jax.ShapeDtypeStruct((NP_TOK, D), F32),
        grid=(BATCH,),
        in_specs=[
            pl.BlockSpec((SEQ, Q_DIM), lambda s: (s, 0)),
            pl.BlockSpec((SEQ, KV_DIM), lambda s: (s, 0)),
            pl.BlockSpec((SEQ, KV_DIM), lambda s: (s, 0)),
            pl.BlockSpec((Q_DIM, D), lambda s: (0, 0)),
            pl.BlockSpec((SEQ, D), lambda s: (s, 0)),
            pl.BlockSpec((1, 6, D), lambda s: (layer * MOD_ROWS, 0, 0)),
        ],
        out_specs=pl.BlockSpec((SEQ, D), lambda s: (s, 0)),
        scratch_shapes=[pltpu.VMEM((SEQ, Q_DIM), BF16)],
        compiler_params=_cparams(),
        name="attn_ctx",
    )(q, kb, vb, w_o, y, mods)
    pb = NP_TOK // DEC_SEQ
    y_lat = pl.pallas_call(
        _attn_lat_kernel,
        out_shape=jax.ShapeDtypeStruct((NS_TOK, D), F32),
        grid=(DEC_BATCH, BLK_PER_DEC),
        in_specs=[
            pl.BlockSpec((TM, Q_DIM), lambda b, t: (NBP + b * BLK_PER_DEC + t, 0)),
            pl.BlockSpec((DEC_SEQ, KV_DIM), lambda b, t: (pb + b, 0)),
            pl.BlockSpec((DEC_SEQ, KV_DIM), lambda b, t: (pb + b, 0)),
            pl.BlockSpec((1, PAST_LEN, KV_DIM), lambda b, t: (b, 0, 0)),
            pl.BlockSpec((1, PAST_LEN, KV_DIM), lambda b, t: (b, 0, 0)),
            pl.BlockSpec((Q_DIM, D), lambda b, t: (0, 0)),
            pl.BlockSpec((TM, D), lambda b, t: (NBP + b * BLK_PER_DEC + t, 0)),
            pl.BlockSpec((1, 6, D), lambda b, t: (layer * MOD_ROWS + 1 + b, 0, 0)),
        ],
        out_specs=pl.BlockSpec((TM, D), lambda b, t: (b * BLK_PER_DEC + t, 0)),
        scratch_shapes=[pltpu.VMEM((TM, Q_DIM), BF16)],
        compiler_params=_cparams(2),
        name="attn_lat",
    )(q, kb, vb, cache_k, cache_v, w_o, y, mods)
    return jnp.concatenate([y_ctx, y_lat], axis=0)


def _log_sigmoid(x):
    return jnp.minimum(x, 0.0) - jnp.log(1.0 + jnp.exp(-jnp.abs(x)))


def _mlstm_in_kernel(mr_ref, y_ref, mod_ref, g_ref, w_ref, wg_ref, bg_ref, q_ref, k_ref, v_ref, o_ref, gt_ref):
    h = _norm_mod(y_ref[...], g_ref[...], mod_ref[0], 0)
    hb = h.astype(BF16)
    q_ref[...] = jnp.dot(hb, w_ref[:, 0:D], preferred_element_type=F32).astype(BF16)
    k_ref[...] = (jnp.dot(hb, w_ref[:, D:2 * D], preferred_element_type=F32) * (M_HEAD_DIM ** -0.5)).astype(BF16)
    v_ref[...] = jnp.dot(hb, w_ref[:, 2 * D:3 * D], preferred_element_type=F32).astype(BF16)
    o_ref[...] = _sigmoid(jnp.dot(hb, w_ref[:, 3 * D:4 * D], preferred_element_type=F32))
    gt = jnp.dot(h, wg_ref[...], preferred_element_type=F32, precision=lax.Precision.HIGHEST) + bg_ref[...]
    lane = lax.broadcasted_iota(I32, gt.shape, 1)
    is_f = ((lane >= M_HEADS) & (lane < 2 * M_HEADS)) | ((lane >= 3 * M_HEADS) & (lane < 4 * M_HEADS))
    gt_ref[...] = jnp.where(is_f, _log_sigmoid(gt), gt)


def _mlstm_in(y, mods, mrow, g1, w_main, w_gate, b_gate):
    return pl.pallas_call(
        _mlstm_in_kernel,
        out_shape=(jax.ShapeDtypeStruct((N_TOK, D), BF16), jax.ShapeDtypeStruct((N_TOK, D), BF16),
                   jax.ShapeDtypeStruct((N_TOK, D), BF16), jax.ShapeDtypeStruct((N_TOK, D), F32),
                   jax.ShapeDtypeStruct((N_TOK, LANE), F32)),
        grid_spec=pltpu.PrefetchScalarGridSpec(
            num_scalar_prefetch=1, grid=(NB,),
            in_specs=[_tok_spec(D), _mod_spec(), _full_spec((1, D)), _full_spec((D, 4 * D)),
                      _full_spec((D, LANE)), _full_spec((1, LANE))],
            out_specs=(_tok_spec(D), _tok_spec(D), _tok_spec(D), _tok_spec(D), _tok_spec(LANE))),
        compiler_params=_cparams(),
        name="mlstm_in",
    )(mrow, y, mods, g1, w_main, w_gate, b_gate)


def _mlstm_chunk(d, c, m, q_ref, k_ref, v_ref, gc_ref, gr_ref, c_scr, n_scr):
    L = M_CHUNK
    r0 = pl.multiple_of(c * L, L)
    q = q_ref[pl.ds(r0, L), :]
    k = k_ref[pl.ds(r0, L), :]
    v = v_ref[pl.ds(r0, L), :]
    col = gc_ref[0, pl.ds(r0, L), :]
    row = gr_ref[0, c]
    i_col, lf_col = col[:, 2 * d:2 * d + 1], col[:, 2 * d + 1:2 * d + 2]
    i_row, lf_row = row[2 * d:2 * d + 1, :], row[2 * d + 1:2 * d + 2, :]
    t_idx = lax.broadcasted_iota(I32, (L, L), 0)
    s_idx = lax.broadcasted_iota(I32, (L, L), 1)
    if d == 0:
        mask, mask_t = s_idx <= t_idx, t_idx <= s_idx
    else:
        mask, mask_t = s_idx >= t_idx, t_idx >= s_idx
    b_col = jnp.sum(jnp.where(mask, lf_row, 0.0), axis=1, keepdims=True)
    b_row = jnp.sum(jnp.where(mask_t, lf_col, 0.0), axis=0, keepdims=True)
    log_d = jnp.where(mask, b_col - b_row + i_row, -jnp.inf)
    li = b_col + m
    m_r = jnp.maximum(li, jnp.max(log_d, axis=1, keepdims=True))
    a_int = jnp.exp(li - m_r)
    s = lax.dot_general(q, k, (((1,), (1,)), ((), ())), preferred_element_type=F32) * jnp.exp(log_d - m_r)
    cmat = c_scr[d]
    num = a_int * jnp.dot(q, cmat.astype(BF16), preferred_element_type=F32) \
        + jnp.dot(s.astype(BF16), v, preferred_element_type=F32)
    qn = jnp.sum(q.astype(F32) * n_scr[d], axis=1, keepdims=True)
    den = a_int * qn + jnp.sum(s, axis=1, keepdims=True)
    hh = num / jnp.maximum(jnp.abs(den), jnp.exp(-m_r))
    b_last = b_row[:, L - 1:L] if d == 0 else b_row[:, 0:1]
    log_w = b_last - b_col + i_col
    m_new = jnp.maximum(b_last + m, jnp.max(log_w, axis=0, keepdims=True))
    w = jnp.exp(log_w - m_new)
    decay = jnp.exp(b_last + m - m_new)
    kw = k.astype(F32) * w
    c_scr[d] = decay * cmat + lax.dot_general(kw.astype(BF16), v, (((0,), (0,)), ((), ())),
                                              preferred_element_type=F32)
    n_scr[d] = decay * n_scr[d] + jnp.sum(kw, axis=0, keepdims=True)
    return r0, hh, m_new


def _mlstm_scan_body(n_chunks, q_ref, k_ref, v_ref, gc_ref, gr_ref, h_ref, c_scr, n_scr, m0):
    h_ref[...] = jnp.zeros(h_ref.shape, F32)

    def body(c, carry):
        mf, mb = carry
        r0, hf, mf = _mlstm_chunk(0, c, mf, q_ref, k_ref, v_ref, gc_ref, gr_ref, c_scr, n_scr)
        h_ref[pl.ds(r0, M_CHUNK), :] += hf
        r1, hb, mb = _mlstm_chunk(1, n_chunks - 1 - c, mb, q_ref, k_ref, v_ref, gc_ref, gr_ref, c_scr, n_scr)
        h_ref[pl.ds(r1, M_CHUNK), :] += hb
        return mf, mb

    return lax.fori_loop(0, n_chunks, body, m0)


def _mlstm_scan_ctx_kernel(q_ref, k_ref, v_ref, gc_ref, gr_ref, h_ref, cn_ref, nn_ref, mn_ref, c_scr, n_scr):
    c_scr[...] = jnp.zeros(c_scr.shape, F32)
    n_scr[...] = jnp.zeros(n_scr.shape, F32)
    zero = jnp.zeros((1, 1), F32)
    mf, mb = _mlstm_scan_body(SEQ // M_CHUNK, q_ref, k_ref, v_ref, gc_ref, gr_ref, h_ref, c_scr, n_scr, (zero, zero))
    for d, m in ((0, mf), (1, mb)):
        cn_ref[0, d, 0] = c_scr[d]
        nn_ref[0, d, 0] = n_scr[d]
        mn_ref[0, d, 0] = jnp.broadcast_to(m, (1, LANE))


def _mlstm_scan_lat_kernel(q_ref, k_ref, v_ref, gc_ref, gr_ref, c0_ref, n0_ref, m0_ref, h_ref, c_scr, n_scr):
    for d in range(2):
        c_scr[d] = c0_ref[0, d, 0]
        n_scr[d] = n0_ref[0, d, 0]
    m0 = (m0_ref[0, 0, 0], m0_ref[0, 1, 0])
    _mlstm_scan_body(DEC_SEQ // M_CHUNK, q_ref, k_ref, v_ref, gc_ref, gr_ref, h_ref, c_scr, n_scr, m0)


def _mlstm_scan(q, k, v, gates, state_c, state_n, state_m):
    g16 = gates[:, :4 * M_HEADS].reshape(N_TOK, 4, M_HEADS)
    gcol = jnp.transpose(g16, (2, 0, 1))
    grow = jnp.transpose(g16.reshape(N_TOK // M_CHUNK, M_CHUNK, 4, M_HEADS), (3, 0, 2, 1))
    hd = M_HEAD_DIM
    scratch = [pltpu.VMEM((2, hd, hd), F32), pltpu.VMEM((2, 1, hd), F32)]
    ncp = SEQ // M_CHUNK
    h_ctx, new_c, new_n, new_m = pl.pallas_call(
        _mlstm_scan_ctx_kernel,
        out_shape=(jax.ShapeDtypeStruct((NP_TOK, D), F32),
                   jax.ShapeDtypeStruct((BATCH, 2, M_HEADS, hd, hd), F32),
                   jax.ShapeDtypeStruct((BATCH, 2, M_HEADS, 1, hd), F32),
                   jax.ShapeDtypeStruct((BATCH, 2, M_HEADS, 1, LANE), F32)),
        grid=(BATCH, M_HEADS),
        in_specs=[
            pl.BlockSpec((SEQ, hd), lambda s, h: (s, h)),
            pl.BlockSpec((SEQ, hd), lambda s, h: (s, h)),
            pl.BlockSpec((SEQ, hd), lambda s, h: (s, h)),
            pl.BlockSpec((1, SEQ, 4), lambda s, h: (h, s, 0)),
            pl.BlockSpec((1, ncp, 4, M_CHUNK), lambda s, h: (h, s, 0, 0)),
        ],
        out_specs=(
            pl.BlockSpec((SEQ, hd), lambda s, h: (s, h)),
            pl.BlockSpec((1, 2, 1, hd, hd), lambda s, h: (s, 0, h, 0, 0)),
            pl.BlockSpec((1, 2, 1, 1, hd), lambda s, h: (s, 0, h, 0, 0)),
            pl.BlockSpec((1, 2, 1, 1, LANE), lambda s, h: (s, 0, h, 0, 0)),
        ),
        scratch_shapes=scratch,
        compiler_params=_cparams(2),
        name="mlstm_scan_ctx",
    )(q, k, v, gcol, grow)
    ncl = DEC_SEQ // M_CHUNK
    pb = NP_TOK // DEC_SEQ
    h_lat = pl.pallas_call(
        _mlstm_scan_lat_kernel,
        out_shape=jax.ShapeDtypeStruct((NS_TOK, D), F32),
        grid=(DEC_BATCH, M_HEADS),
        in_specs=[
            pl.BlockSpec((DEC_SEQ, hd), lambda b, h: (pb + b, h)),
            pl.BlockSpec((DEC_SEQ, hd), lambda b, h: (pb + b, h)),
            pl.BlockSpec((DEC_SEQ, hd), lambda b, h: (pb + b, h)),
            pl.BlockSpec((1, DEC_SEQ, 4), lambda b, h: (h, pb + b, 0)),
            pl.BlockSpec((1, ncl, 4, M_CHUNK), lambda b, h: (h, pb + b, 0, 0)),
            pl.BlockSpec((1, 2, 1, hd, hd), lambda b, h: (b, 0, h, 0, 0)),
            pl.BlockSpec((1, 2, 1, 1, hd), lambda b, h: (b, 0, h, 0, 0)),
            pl.BlockSpec((1, 2, 1, 1, 1), lambda b, h: (b, 0, h, 0, 0)),
        ],
        out_specs=pl.BlockSpec((DEC_SEQ, hd), lambda b, h: (b, h)),
        scratch_shapes=scratch,
        compiler_params=_cparams(2),
        name="mlstm_scan_lat",
    )(q, k, v, gcol, grow, state_c, state_n, state_m)
    return jnp.concatenate([h_ctx, h_lat], axis=0), new_c, new_n, new_m


def _mlstm_out_kernel(mr_ref, h_ref, o_ref, ng_ref, w_ref, y_ref, mod_ref, out_ref, x_scr):
    hc = o_ref[...] * h_ref[...]
    for hd in range(M_HEADS):
        sl = slice(hd * M_HEAD_DIM, (hd + 1) * M_HEAD_DIM)
        x_scr[:, sl] = _rms(hc[:, sl], ng_ref[:, sl]).astype(BF16)
    out = jnp.dot(x_scr[...], w_ref[...], preferred_element_type=F32)
    out_ref[...] = y_ref[...] + mod_ref[0][2:3] * out


def _mlstm_out(hsum, o, norm_g, w_out, y, mods, mrow):
    return pl.pallas_call(
        _mlstm_out_kernel,
        out_shape=jax.ShapeDtypeStruct((N_TOK, D), F32),
        grid_spec=pltpu.PrefetchScalarGridSpec(
            num_scalar_prefetch=1, grid=(NB,),
            in_specs=[_tok_spec(D), _tok_spec(D), _full_spec((1, D)), _full_spec((D, D)), _tok_spec(D), _mod_spec()],
            out_specs=_tok_spec(D),
            scratch_shapes=[pltpu.VMEM((TM, D), BF16)]),
        compiler_params=_cparams(),
        name="mlstm_out",
    )(mrow, hsum, o, norm_g, w_out, y, mods)


ROUTE_OFF = N_GROUPS


def _route_kernel(mr_ref, y_ref, mod_ref, g_ref, wr_ref, br_ref, x_ref, id_ref, wt_ref):
    x = _norm_mod(y_ref[...], g_ref[...], mod_ref[0], 1)
    x_ref[...] = x
    lg = jnp.dot(x, wr_ref[...], preferred_element_type=F32, precision=lax.Precision.HIGHEST) + br_ref[...]
    lane = lax.broadcasted_iota(I32, lg.shape, 1)
    ninf = -jnp.inf
    big = jnp.int32(LANE)
    lgg = jnp.where(lane < N_GROUPS, lg, ninf)
    gmax = jnp.max(lgg, axis=-1, keepdims=True)
    g_idx = jnp.min(jnp.where(lgg == gmax, lane, big), axis=-1, keepdims=True)
    g_w = 1.0 / jnp.sum(jnp.exp(lgg - gmax), axis=-1, keepdims=True)
    lo = ROUTE_OFF + g_idx * EXPERTS_PER_GROUP
    le = jnp.where((lane >= lo) & (lane < lo + EXPERTS_PER_GROUP), lg, ninf)
    m1 = jnp.max(le, axis=-1, keepdims=True)
    i1 = jnp.min(jnp.where(le == m1, lane, big), axis=-1, keepdims=True)
    le2 = jnp.where(lane == i1, ninf, le)
    m2 = jnp.max(le2, axis=-1, keepdims=True)
    i2 = jnp.min(jnp.where(le2 == m2, lane, big), axis=-1, keepdims=True)
    r = jnp.exp(m2 - m1)
    p1 = 1.0 / (1.0 + r)
    p2 = r / (1.0 + r)
    two = lax.broadcasted_iota(I32, (x.shape[0], TOP_K), 1)
    id_ref[...] = jnp.where(two == 0, i1 - ROUTE_OFF, i2 - ROUTE_OFF)
    wt_ref[...] = jnp.where(two == 0, g_w * p1, g_w * p2)


def _route(y, mods, mrow, g2, w_route, b_route):
    return pl.pallas_call(
        _route_kernel,
        out_shape=(jax.ShapeDtypeStruct((N_TOK, D), F32), jax.ShapeDtypeStruct((N_TOK, TOP_K), I32),
                   jax.ShapeDtypeStruct((N_TOK, TOP_K), F32)),
        grid_spec=pltpu.PrefetchScalarGridSpec(
            num_scalar_prefetch=1, grid=(NB,),
            in_specs=[_tok_spec(D), _mod_spec(), _full_spec((1, D)), _full_spec((D, LANE)), _full_spec((1, LANE))],
            out_specs=(_tok_spec(D), _tok_spec(TOP_K), _tok_spec(TOP_K))),
        compiler_params=_cparams(),
        name="moe_route",
    )(mrow, y, mods, g2, w_route, b_route)


def _dispatch_tables(expert_id, weights):
    flat_e = expert_id.reshape(-1)
    flat_w = weights.reshape(-1)
    order = jnp.argsort(flat_e, stable=True).astype(I32)
    e_sorted = flat_e[order]
    counts = jnp.bincount(flat_e, length=N_EXPERTS).astype(I32)
    padded = ((counts + EBLK - 1) // EBLK) * EBLK
    pad_end = jnp.cumsum(padded)
    pad_start = pad_end - padded
    start = jnp.cumsum(counts) - counts
    dest = pad_start[e_sorted] + (jnp.arange(N_ASSIGN, dtype=I32) - start[e_sorted])
    is_pad = jnp.ones((P_SLOTS,), I32).at[dest].set(0)
    pad_ord = jnp.cumsum(is_pad) - 1
    asg = jnp.zeros((P_SLOTS,), I32).at[dest].set(order)
    src_tok = jnp.where(is_pad == 1, 0, asg // TOP_K)
    dst_row = jnp.where(is_pad == 1, N_ASSIGN + pad_ord, asg)
    w_slot = jnp.zeros((P_SLOTS,), F32).at[dest].set(flat_w[order])
    blk = jnp.arange(N_EBLK, dtype=I32) * EBLK
    block_e = jnp.clip(jnp.searchsorted(pad_end, blk, side="right"), 0, N_EXPERTS - 1).astype(I32)
    first = jnp.concatenate([jnp.ones((1,), I32), (block_e[1:] != block_e[:-1]).astype(I32)])
    return (src_tok.reshape(N_EBLK, 1, EBLK), dst_row.reshape(N_EBLK, 1, EBLK), w_slot.reshape(P_SLOTS, 1),
            block_e, first)


def _expert_kernel(layer, be_ref, first_ref, src0_ref, srcn_ref, dst_ref, w_ref, wg_ref, wu_ref, wd_ref,
                   x_hbm, ym_hbm, xbuf, ybuf, wg_bf, wu_bf, wd_bf, gsem, ssem):
    i = pl.program_id(0)
    nb = pl.num_programs(0)
    slot = lax.rem(i, 2)

    def gather_start(src_ref, s):
        for r in range(EBLK):
            pltpu.make_async_copy(x_hbm.at[pl.ds(src_ref[0, 0, r], 1)], xbuf.at[s, pl.ds(r, 1)], gsem.at[s]).start()

    def gather_wait(s):
        pltpu.make_async_copy(x_hbm.at[pl.ds(0, EBLK)], xbuf.at[s], gsem.at[s]).wait()

    def scatter_wait(s):
        pltpu.make_async_copy(ybuf.at[s], ym_hbm.at[pl.ds(0, EBLK)], ssem.at[s]).wait()

    @pl.when(i == 0)
    def _():
        gather_start(src0_ref, 0)

    gather_wait(slot)

    @pl.when(i + 1 < nb)
    def _():
        gather_start(srcn_ref, 1 - slot)

    @pl.when(first_ref[i] == 1)
    def _():
        wg_bf[...] = wg_ref[0, 0].astype(BF16)
        wu_bf[...] = wu_ref[0, 0].astype(BF16)
        wd_bf[...] = wd_ref[0, 0].astype(BF16)

    xb = xbuf[slot].astype(BF16)
    g = jnp.dot(xb, wg_bf[...], preferred_element_type=F32)
    u = jnp.dot(xb, wu_bf[...], preferred_element_type=F32)
    hmid = (g * _sigmoid(g) * u).astype(BF16)
    res = jnp.dot(hmid, wd_bf[...], preferred_element_type=F32) * w_ref[...]

    @pl.when(i >= 2)
    def _():
        scatter_wait(slot)

    ybuf[slot] = res
    for r in range(EBLK):
        pltpu.make_async_copy(ybuf.at[slot, pl.ds(r, 1)], ym_hbm.at[pl.ds(dst_ref[0, 0, r], 1)], ssem.at[slot]).start()

    @pl.when(i == nb - 1)
    def _():
        scatter_wait(1 - slot)
        scatter_wait(slot)


def _experts(x2, tables, w_gate, w_up, w_down, layer):
    src_tok, dst_row, w_slot, block_e, first = tables
    smem_blk = lambda f: pl.BlockSpec((1, 1, EBLK), f, memory_space=pltpu.SMEM)
    wspec = lambda r, c: pl.BlockSpec((1, 1, r, c), lambda i, be, fi: (layer, be[i], 0, 0))
    return pl.pallas_call(
        functools.partial(_expert_kernel, layer),
        out_shape=jax.ShapeDtypeStruct((N_ASSIGN + N_PAD_SLOTS, D), F32),
        grid_spec=pltpu.PrefetchScalarGridSpec(
            num_scalar_prefetch=2, grid=(N_EBLK,),
            in_specs=[
                smem_blk(lambda i, be, fi: (0, 0, 0)),
                smem_blk(lambda i, be, fi: (jnp.minimum(i + 1, N_EBLK - 1), 0, 0)),
                smem_blk(lambda i, be, fi: (i, 0, 0)),
                pl.BlockSpec((EBLK, 1), lambda i, be, fi: (i, 0)),
                wspec(D, D_EXPERT), wspec(D, D_EXPERT), wspec(D_EXPERT, D),
                pl.BlockSpec(memory_space=pl.ANY),
            ],
            out_specs=pl.BlockSpec(memory_space=pl.ANY),
            scratch_shapes=[
                pltpu.VMEM((2, EBLK, D), F32), pltpu.VMEM((2, EBLK, D), F32),
                pltpu.VMEM((D, D_EXPERT), BF16), pltpu.VMEM((D, D_EXPERT), BF16), pltpu.VMEM((D_EXPERT, D), BF16),
                pltpu.SemaphoreType.DMA((2,)), pltpu.SemaphoreType.DMA((2,)),
            ]),
        compiler_params=_cparams(),
        name="moe_experts",
    )(block_e, first, src_tok, src_tok, dst_row, w_slot, w_gate, w_up, w_down, x2)


def _combine_kernel(mr_ref, ym_ref, y_ref, mod_ref, o_ref):
    ym = ym_ref[...]
    o_ref[...] = y_ref[...] + mod_ref[0][5:6] * (ym[:, :D] + ym[:, D:])


def _combine(ym, y, mods, mrow):
    ym2 = ym.reshape((N_ASSIGN + N_PAD_SLOTS) // TOP_K, TOP_K * D)
    return pl.pallas_call(
        _combine_kernel,
        out_shape=jax.ShapeDtypeStruct((N_TOK, D), F32),
        grid_spec=pltpu.PrefetchScalarGridSpec(
            num_scalar_prefetch=1, grid=(NB,),
            in_specs=[_tok_spec(TOP_K * D), _tok_spec(D), _mod_spec()],
            out_specs=_tok_spec(D)),
        compiler_params=_cparams(),
        name="moe_combine",
    )(mrow, ym2, y, mods)


def _final_norm_kernel(y_ref, g_ref, o_ref):
    o_ref[...] = _rms(y_ref[...], g_ref[...])


def _final_norm(y, g):
    return pl.pallas_call(
        _final_norm_kernel,
        out_shape=jax.ShapeDtypeStruct((N_TOK, D), F32),
        grid=(NB,),
        in_specs=[pl.BlockSpec((TM, D), lambda j: (j, 0)), pl.BlockSpec((1, D), lambda j: (0, 0))],
        out_specs=pl.BlockSpec((TM, D), lambda j: (j, 0)),
        compiler_params=_cparams(),
        name="final_norm",
    )(y, g)


def kernel(x_prompt, x_sample, cache_attn_k, cache_attn_v, state_mlstm_C, state_mlstm_n, state_mlstm_m, c, c_ctx, ada_w, ada_b, norm1_g, norm2_g, conv_w_in, conv_w_dw, conv_b_dw, conv_ln_g, conv_ln_b, conv_w_out, attn_w_qkv, attn_q_norm, attn_k_norm, attn_w_o, mlstm_w_in, mlstm_b_gate, mlstm_norm_g, mlstm_w_out, moe_w_group, moe_b_group, moe_w_router, moe_b_router, moe_w_gate, moe_w_up, moe_w_down, final_norm_g):
    y = jnp.concatenate([x_prompt.reshape(NP_TOK, D), x_sample.reshape(NS_TOK, D)], axis=0)
    cvec = jnp.concatenate([c_ctx[None, :], c, jnp.zeros((MOD_ROWS - 1 - DEC_BATCH, D), F32)], axis=0)
    mods = _ada_all(cvec, ada_w, ada_b)
    rope = _rope_blocks()
    new_k = new_v = new_c = new_n = new_m = None
    for i in range(DEPTH):
        kind, slot = i % 3, i // 3
        mrow = jnp.asarray(_MOD_ROW + i * MOD_ROWS)
        g1 = norm1_g[i].reshape(1, D)
        if kind == 0:
            u = _conv_in(y, mods, mrow, g1, conv_w_in[slot].astype(BF16))
            w_dw = jnp.concatenate([conv_w_dw[slot], jnp.zeros((1, D), F32)], axis=0)
            y = _conv_main(u, y, mods, mrow, w_dw, conv_b_dw[slot].reshape(1, D), conv_ln_g[slot].reshape(1, D),
                           conv_ln_b[slot].reshape(1, D), conv_w_out[slot].astype(BF16))
        elif kind == 1:
            q, kb, vb, kf, vf = _attn_qkv(y, mods, mrow, g1, attn_w_qkv[slot].astype(BF16),
                                          attn_q_norm[slot].reshape(1, HEAD_DIM), attn_k_norm[slot].reshape(1, HEAD_DIM),
                                          rope)
            new_k = kf[:NP_TOK].reshape(BATCH, 1, SEQ, N_KV_HEADS, HEAD_DIM)
            new_v = vf[:NP_TOK].reshape(BATCH, 1, SEQ, N_KV_HEADS, HEAD_DIM)
            ck = cache_attn_k[:, slot].reshape(DEC_BATCH, PAST_LEN, KV_DIM)
            cv = cache_attn_v[:, slot].reshape(DEC_BATCH, PAST_LEN, KV_DIM)
            y = _attention(q, kb, vb, ck, cv, attn_w_o[slot].astype(BF16), y, mods, i)
        else:
            w_in = mlstm_w_in[slot]
            w_gate = jnp.concatenate([w_in[:, 4 * D:], jnp.zeros((D, LANE - 4 * M_HEADS), F32)], axis=1)
            b_gate = jnp.concatenate([mlstm_b_gate[slot], jnp.zeros((LANE - 4 * M_HEADS,), F32)]).reshape(1, LANE)
            q, k, v, o, gates = _mlstm_in(y, mods, mrow, g1, w_in[:, :4 * D].astype(BF16), w_gate, b_gate)
            sc = state_mlstm_C[:, slot]
            sn = state_mlstm_n[:, slot].reshape(DEC_BATCH, 2, M_HEADS, 1, M_HEAD_DIM)
            sm = state_mlstm_m[:, slot].reshape(DEC_BATCH, 2, M_HEADS, 1, 1)
            hsum, nc_, nn_, nm_ = _mlstm_scan(q, k, v, gates, sc, sn, sm)
            new_c = nc_[:, None]
            new_n = nn_.reshape(BATCH, 1, 2, M_HEADS, M_HEAD_DIM)
            new_m = nm_[..., 0, 0].reshape(BATCH, 1, 2, M_HEADS)
            y = _mlstm_out(hsum, o, mlstm_norm_g[slot].reshape(1, D), mlstm_w_out[slot].astype(BF16), y, mods, mrow)
        w_route = jnp.concatenate([moe_w_group[i], moe_w_router[i],
                                   jnp.zeros((D, LANE - N_GROUPS - N_EXPERTS), F32)], axis=1)
        b_route = jnp.concatenate([moe_b_group[i], moe_b_router[i],
                                   jnp.zeros((LANE - N_GROUPS - N_EXPERTS,), F32)]).reshape(1, LANE)
        x2, eid, ewt = _route(y, mods, mrow, norm2_g[i].reshape(1, D), w_route, b_route)
        tables = _dispatch_tables(eid, ewt)
        ym = _experts(x2, tables, moe_w_gate, moe_w_up, moe_w_down, i)
        y = _combine(ym, y, mods, mrow)
    yn = _final_norm(y, final_norm_g.reshape(1, D))
    y_prompt = yn[:NP_TOK].reshape(BATCH, SEQ, D)
    y_sample = yn[NP_TOK:].reshape(DEC_BATCH, DEC_SEQ, D)
    return (y_prompt, y_sample, new_k, new_v, new_c, new_n, new_m)
```

```python
import functools

import jax
import jax.numpy as jnp
import numpy as np
from jax import lax
from jax.experimental import pallas as pl
from jax.experimental.pallas import tpu as pltpu
from jax.experimental.pallas import tpu_sc as plsc

F32 = jnp.float32
BF16 = jnp.bfloat16
I32 = jnp.int32

D = 1024
BATCH, SEQ = 16, 256
DEC_BATCH, DEC_SEQ = 8, 1024
PAST_LEN = 256
DEPTH = 4
GRID_W = 64
EPS = 1e-6
CONV_WIDTH = 31
CONV_PAD = CONV_WIDTH // 2
HEAD_DIM = 128
N_HEADS = 8
N_KV_HEADS = 2
GQA_GROUP = N_HEADS // N_KV_HEADS
Q_DIM = N_HEADS * HEAD_DIM
KV_DIM = N_KV_HEADS * HEAD_DIM
QKV_DIM = Q_DIM + 2 * KV_DIM
ROPE_THETA = 10000.0
M_HEADS = 4
M_HEAD_DIM = D // M_HEADS
M_CHUNK = 64
N_GROUPS = 4
EXPERTS_PER_GROUP = 8
N_EXPERTS = N_GROUPS * EXPERTS_PER_GROUP
TOP_K = 2
D_EXPERT = 512

NP_TOK = BATCH * SEQ
NS_TOK = DEC_BATCH * DEC_SEQ
N_TOK = NP_TOK + NS_TOK
TM = 256
NB = N_TOK // TM
NBP = NP_TOK // TM
BLK_PER_DEC = DEC_SEQ // TM
MOD_ROWS = 16
HALO = 16
LANE = 128
SUBLANE = 8

N_ASSIGN = N_TOK * TOP_K
EBLK = 128
N_EBLK = N_ASSIGN // EBLK + N_EXPERTS
P_SLOTS = N_EBLK * EBLK
N_PAD_SLOTS = P_SLOTS - N_ASSIGN

VMEM_LIMIT = 56 * 1024 * 1024


def _block_tables():
    j = np.arange(NB)
    is_p = j < NBP
    mod_row = np.where(is_p, 0, 1 + (j - NBP) // BLK_PER_DEC)
    rope_idx = np.where(is_p, 0, 1 + (j - NBP) % BLK_PER_DEC)
    first = np.where(is_p, 1, ((j - NBP) % BLK_PER_DEC == 0).astype(np.int64))
    last = np.where(is_p, 1, ((j - NBP) % BLK_PER_DEC == BLK_PER_DEC - 1).astype(np.int64))
    return (mod_row.astype(np.int32), rope_idx.astype(np.int32), first.astype(np.int32), last.astype(np.int32))


_MOD_ROW, _ROPE_IDX, _SEQ_FIRST, _SEQ_LAST = _block_tables()


def _cparams(n_axes=1):
    return pltpu.CompilerParams(dimension_semantics=("arbitrary",) * n_axes, vmem_limit_bytes=VMEM_LIMIT)


def _sigmoid(x):
    return 1.0 / (1.0 + jnp.exp(-x))


def _rms(x, g):
    return x * lax.rsqrt(jnp.mean(x * x, axis=-1, keepdims=True) + EPS) * g


def _norm_mod(y, g, mod, which):
    shift = mod[3 * which:3 * which + 1]
    scale = mod[3 * which + 1:3 * which + 2]
    return _rms(y, g) * (1.0 + scale) + shift


def _ada_kernel(c_ref, w_ref, b_ref, o_ref):
    c = c_ref[...]
    s = c * _sigmoid(c)
    o_ref[0] = jnp.dot(s.astype(BF16), w_ref[0].astype(BF16), preferred_element_type=F32) + b_ref[0]


def _ada_all(cvec, ada_w, ada_b):
    tn = 1536
    out = pl.pallas_call(
        _ada_kernel,
        out_shape=jax.ShapeDtypeStruct((DEPTH, MOD_ROWS, 6 * D), F32),
        grid=(DEPTH, 6 * D // tn),
        in_specs=[
            pl.BlockSpec((MOD_ROWS, D), lambda l, n: (0, 0)),
            pl.BlockSpec((1, D, tn), lambda l, n: (l, 0, n)),
            pl.BlockSpec((1, 1, tn), lambda l, n: (l, 0, n)),
        ],
        out_specs=pl.BlockSpec((1, MOD_ROWS, tn), lambda l, n: (l, 0, n)),
        compiler_params=_cparams(2),
        name="ada_mod",
    )(cvec, ada_w, ada_b.reshape(DEPTH, 1, 6 * D))
    return out.reshape(DEPTH * MOD_ROWS, 6, D)


def _tok_spec(width):
    return pl.BlockSpec((TM, width), lambda j, *_: (j, 0))


def _mod_spec():
    return pl.BlockSpec((1, 6, D), lambda j, mr, *_: (mr[j], 0, 0))


def _full_spec(shape):
    nd = len(shape)
    return pl.BlockSpec(shape, lambda j, *_: (0,) * nd)


def _conv_in_kernel(mr_ref, y_ref, mod_ref, g_ref, w_ref, u_ref):
    h = _norm_mod(y_ref[...], g_ref[...], mod_ref[0], 0)
    ag = jnp.dot(h.astype(BF16), w_ref[...], preferred_element_type=F32)
    u_ref[...] = ag[:, :D] * _sigmoid(ag[:, D:])


def _conv_in(y, mods, mrow, g1, w_in):
    return pl.pallas_call(
        _conv_in_kernel,
        out_shape=jax.ShapeDtypeStruct((N_TOK, D), F32),
        grid_spec=pltpu.PrefetchScalarGridSpec(
            num_scalar_prefetch=1, grid=(NB,),
            in_specs=[_tok_spec(D), _mod_spec(), _full_spec((1, D)), _full_spec((D, 2 * D))],
            out_specs=_tok_spec(D)),
        compiler_params=_cparams(),
        name="conv_in",
    )(mrow, y, mods, g1, w_in)


def _conv_main_kernel(mr_ref, first_ref, last_ref, u_ref, up_ref, un_ref, wdw_ref, bdw_ref, lg_ref, lb_ref,
                      wout_ref, y_ref, mod_ref, o_ref, ext_ref, acc_ref):
    j = pl.program_id(0)
    zero = jnp.zeros((HALO, D), F32)
    ext_ref[0:HALO, :] = jnp.where(first_ref[j] == 1, zero, up_ref[...])
    ext_ref[HALO:HALO + TM, :] = u_ref[...]
    ext_ref[HALO + TM:2 * HALO + TM, :] = jnp.where(last_ref[j] == 1, zero, un_ref[...])

    off0 = HALO - CONV_PAD
    def strip(ci, carry):
        cs = pl.ds(pl.multiple_of(ci * LANE, LANE), LANE)
        wk = [jnp.broadcast_to(wdw_ref[k:k + 1, cs], (SUBLANE, LANE)) for k in range(CONV_WIDTH)]
        bias = jnp.broadcast_to(bdw_ref[:, cs], (SUBLANE, LANE))
        for base in range(0, TM, SUBLANE):
            acc = bias
            for k in range(CONV_WIDTH):
                acc = acc + ext_ref[base + off0 + k:base + off0 + k + SUBLANE, cs] * wk[k]
            acc_ref[base:base + SUBLANE, cs] = acc
        return carry

    lax.fori_loop(0, D // LANE, strip, 0)

    c = acc_ref[...]
    mu = jnp.mean(c, axis=-1, keepdims=True)
    cc = c - mu
    var = jnp.mean(cc * cc, axis=-1, keepdims=True)
    z = cc * lax.rsqrt(var + EPS) * lg_ref[...] + lb_ref[...]
    z = z * _sigmoid(z)
    out = jnp.dot(z.astype(BF16), wout_ref[...], preferred_element_type=F32)
    o_ref[...] = y_ref[...] + mod_ref[0][2:3] * out


def _conv_main(u, y, mods, mrow, w_dw, b_dw, ln_g, ln_b, w_out):
    nh = N_TOK // HALO
    per = TM // HALO
    return pl.pallas_call(
        _conv_main_kernel,
        out_shape=jax.ShapeDtypeStruct((N_TOK, D), F32),
        grid_spec=pltpu.PrefetchScalarGridSpec(
            num_scalar_prefetch=3, grid=(NB,),
            in_specs=[
                _tok_spec(D),
                pl.BlockSpec((HALO, D), lambda j, *_: (jnp.maximum(j * per - 1, 0), 0)),
                pl.BlockSpec((HALO, D), lambda j, *_: (jnp.minimum((j + 1) * per, nh - 1), 0)),
                _full_spec((CONV_WIDTH + 1, D)), _full_spec((1, D)), _full_spec((1, D)), _full_spec((1, D)),
                _full_spec((D, D)), _tok_spec(D), _mod_spec(),
            ],
            out_specs=_tok_spec(D),
            scratch_shapes=[pltpu.VMEM((TM + 2 * HALO, D), F32), pltpu.VMEM((TM, D), F32)]),
        compiler_params=_cparams(),
        name="conv_main",
    )(mrow, jnp.asarray(_SEQ_FIRST), jnp.asarray(_SEQ_LAST), u, u, u, w_dw, b_dw, ln_g, ln_b, w_out, y, mods)


def _rope_angles():
    rows = DEC_SEQ // GRID_W
    row = jnp.repeat(jnp.arange(rows, dtype=F32), GRID_W)
    col = jnp.tile(jnp.arange(GRID_W, dtype=F32), rows)
    axis_dim = HEAD_DIM // 2
    freqs = jnp.power(ROPE_THETA, -jnp.arange(axis_dim // 2, dtype=F32) * 2.0 / axis_dim)
    ang_r = row[:, None] * freqs[None, :]
    ang_c = col[:, None] * freqs[None, :]
    return jnp.concatenate([ang_r, ang_r, ang_c, ang_c], axis=-1)


def _rope_blocks():
    ang = _rope_angles()
    cos, sin = jnp.cos(ang), jnp.sin(ang)
    lane = np.arange(HEAD_DIM)
    lo = jnp.asarray(((lane % (HEAD_DIM // 2)) < HEAD_DIM // 4).astype(np.float32))
    sin_a = -sin * lo[None, :]
    sin_b = sin * (1.0 - lo)[None, :]
    nblk = DEC_SEQ // TM
    ident = jnp.ones((1, TM, HEAD_DIM), F32)
    zeros = jnp.zeros((1, TM, HEAD_DIM), F32)
    cos_t = jnp.concatenate([ident, cos.reshape(nblk, TM, HEAD_DIM)], axis=0)
    sa_t = jnp.concatenate([zeros, sin_a.reshape(nblk, TM, HEAD_DIM)], axis=0)
    sb_t = jnp.concatenate([zeros, sin_b.reshape(nblk, TM, HEAD_DIM)], axis=0)
    return cos_t, sa_t, sb_t


def _attn_qkv_kernel(mr_ref, ri_ref, y_ref, mod_ref, g_ref, w_ref, qg_ref, kg_ref, cos_ref, sa_ref, sb_ref,
                     q_ref, kb_ref, vb_ref, kf_ref, vf_ref):
    h = _norm_mod(y_ref[...], g_ref[...], mod_ref[0], 0)
    qkv = jnp.dot(h.astype(BF16), w_ref[...], preferred_element_type=F32)
    cos, sa, sb = cos_ref[0], sa_ref[0], sb_ref[0]
    quarter = HEAD_DIM // 4

    def head(x, g):
        xn = _rms(x, g)
        return xn * cos + pltpu.roll(xn, HEAD_DIM - quarter, 1) * sa + pltpu.roll(xn, quarter, 1) * sb

    scale = HEAD_DIM ** -0.5
    for hd in range(N_HEADS):
        sl = slice(hd * HEAD_DIM, (hd + 1) * HEAD_DIM)
        q_ref[:, sl] = (head(qkv[:, sl], qg_ref[...]) * scale).astype(BF16)
    for kv in range(N_KV_HEADS):
        sl = slice(kv * HEAD_DIM, (kv + 1) * HEAD_DIM)
        kr = head(qkv[:, Q_DIM + kv * HEAD_DIM:Q_DIM + (kv + 1) * HEAD_DIM], kg_ref[...])
        kf_ref[:, sl] = kr
        kb_ref[:, sl] = kr.astype(BF16)
    v = qkv[:, Q_DIM + KV_DIM:]
    vf_ref[...] = v
    vb_ref[...] = v.astype(BF16)


def _attn_qkv(y, mods, mrow, g1, w_qkv, q_g, k_g, rope):
    cos_t, sa_t, sb_t = rope
    rspec = pl.BlockSpec((1, TM, HEAD_DIM), lambda j, mr, ri: (ri[j], 0, 0))
    return pl.pallas_call(
        _attn_qkv_kernel,
        out_shape=(jax.ShapeDtypeStruct((N_TOK, Q_DIM), BF16), jax.ShapeDtypeStruct((N_TOK, KV_DIM), BF16),
                   jax.ShapeDtypeStruct((N_TOK, KV_DIM), BF16), jax.ShapeDtypeStruct((N_TOK, KV_DIM), F32),
                   jax.ShapeDtypeStruct((N_TOK, KV_DIM), F32)),
        grid_spec=pltpu.PrefetchScalarGridSpec(
            num_scalar_prefetch=2, grid=(NB,),
            in_specs=[_tok_spec(D), _mod_spec(), _full_spec((1, D)), _full_spec((D, QKV_DIM)),
                      _full_spec((1, HEAD_DIM)), _full_spec((1, HEAD_DIM)), rspec, rspec, rspec],
            out_specs=(_tok_spec(Q_DIM), _tok_spec(KV_DIM), _tok_spec(KV_DIM), _tok_spec(KV_DIM), _tok_spec(KV_DIM))),
        compiler_params=_cparams(),
        name="attn_qkv",
    )(mrow, jnp.asarray(_ROPE_IDX), y, mods, g1, w_qkv, q_g, k_g, cos_t, sa_t, sb_t)


def _attn_heads(q, ks, vs, o_scr):
    nt = (((1,), (1,)), ((), ()))
    for hd in range(N_HEADS):
        g = hd // GQA_GROUP
        qh = q[:, hd * HEAD_DIM:(hd + 1) * HEAD_DIM]
        gs = slice(g * HEAD_DIM, (g + 1) * HEAD_DIM)
        ss = [lax.dot_general(qh, k[:, gs], nt, preferred_element_type=F32) for k in ks]
        m = functools.reduce(jnp.maximum, [jnp.max(s, axis=-1, keepdims=True) for s in ss])
        ps = [jnp.exp(s - m) for s in ss]
        l = functools.reduce(lambda a, b: a + b, [jnp.sum(p, axis=-1, keepdims=True) for p in ps])
        o = functools.reduce(lambda a, b: a + b,
                             [jnp.dot(p.astype(BF16), v[:, gs], preferred_element_type=F32) for p, v in zip(ps, vs)])
        o_scr[:, hd * HEAD_DIM:(hd + 1) * HEAD_DIM] = (o / l).astype(BF16)


def _attn_ctx_kernel(q_ref, k_ref, v_ref, wo_ref, y_ref, mod_ref, o_ref, o_scr):
    _attn_heads(q_ref[...], [k_ref[...]], [v_ref[...]], o_scr)
    out = jnp.dot(o_scr[...], wo_ref[...], preferred_element_type=F32)
    o_ref[...] = y_ref[...] + mod_ref[0][2:3] * out


def _attn_lat_kernel(q_ref, k_ref, v_ref, ck_ref, cv_ref, wo_ref, y_ref, mod_ref, o_ref, o_scr):
    _attn_heads(q_ref[...], [k_ref[...], ck_ref[0].astype(BF16)], [v_ref[...], cv_ref[0].astype(BF16)], o_scr)
    out = jnp.dot(o_scr[...], wo_ref[...], preferred_element_type=F32)
    o_ref[...] = y_ref[...] + mod_ref[0][2:3] * out


def _attention(q, kb, vb, cache_k, cache_v, w_o, y, mods, layer):
    y_ctx = pl.pallas_call(
        _attn_ctx_kernel,
        out_shape=jax.ShapeDtypeStruct((NP_TOK, D), F32),
        grid=(BATCH,),
        in_specs=[
            pl.BlockSpec((SEQ, Q_DIM), lambda s: (s, 0)),
            pl.BlockSpec((SEQ, KV_DIM), lambda s: (s, 0)),
            pl.BlockSpec((SEQ, KV_DIM), lambda s: (s, 0)),
            pl.BlockSpec((Q_DIM, D), lambda s: (0, 0)),
            pl.BlockSpec((SEQ, D), lambda s: (s, 0)),
            pl.BlockSpec((1, 6, D), lambda s: (layer * MOD_ROWS, 0, 0)),
        ],
        out_specs=pl.BlockSpec((SEQ, D), lambda s: (s, 0)),
        scratch_shapes=[pltpu.VMEM((SEQ, Q_DIM), BF16)],
        compiler_params=_cparams(),
        name="attn_ctx",
    )(q, kb, vb, w_o, y, mods)
    pb = NP_TOK // DEC_SEQ
    y_lat = pl.pallas_call(
        _attn_lat_kernel,
        out_shape=jax.ShapeDtypeStruct((NS_TOK, D), F32),
        grid=(DEC_BATCH, BLK_PER_DEC),
        in_specs=[
            pl.BlockSpec((TM, Q_DIM), lambda b, t: (NBP + b * BLK_PER_DEC + t, 0)),
            pl.BlockSpec((DEC_SEQ, KV_DIM), lambda b, t: (pb + b, 0)),
            pl.BlockSpec((DEC_SEQ, KV_DIM), lambda b, t: (pb + b, 0)),
            pl.BlockSpec((1, PAST_LEN, KV_DIM), lambda b, t: (b, 0, 0)),
            pl.BlockSpec((1, PAST_LEN, KV_DIM), lambda b, t: (b, 0, 0)),
            pl.BlockSpec((Q_DIM, D), lambda b, t: (0, 0)),
            pl.BlockSpec((TM, D), lambda b, t: (NBP + b * BLK_PER_DEC + t, 0)),
            pl.BlockSpec((1, 6, D), lambda b, t: (layer * MOD_ROWS + 1 + b, 0, 0)),
        ],
        out_specs=pl.BlockSpec((TM, D), lambda b, t: (b * BLK_PER_DEC + t, 0)),
        scratch_shapes=[pltpu.VMEM((TM, Q_DIM), BF16)],
        compiler_params=_cparams(2),
        name="attn_lat",
    )(q, kb, vb, cache_k, cache_v, w_o, y, mods)
    return jnp.concatenate([y_ctx, y_lat], axis=0)


def _log_sigmoid(x):
    return jnp.minimum(x, 0.0) - jnp.log(1.0 + jnp.exp(-jnp.abs(x)))


def _mlstm_in_kernel(mr_ref, y_ref, mod_ref, g_ref, w_ref, wg_ref, bg_ref, q_ref, k_ref, v_ref, o_ref, gt_ref):
    h = _norm_mod(y_ref[...], g_ref[...], mod_ref[0], 0)
    hb = h.astype(BF16)
    q_ref[...] = jnp.dot(hb, w_ref[:, 0:D], preferred_element_type=F32).astype(BF16)
    k_ref[...] = (jnp.dot(hb, w_ref[:, D:2 * D], preferred_element_type=F32) * (M_HEAD_DIM ** -0.5)).astype(BF16)
    v_ref[...] = jnp.dot(hb, w_ref[:, 2 * D:3 * D], preferred_element_type=F32).astype(BF16)
    o_ref[...] = _sigmoid(jnp.dot(hb, w_ref[:, 3 * D:4 * D], preferred_element_type=F32))
    gt = jnp.dot(h, wg_ref[...], preferred_element_type=F32, precision=lax.Precision.HIGHEST) + bg_ref[...]
    lane = lax.broadcasted_iota(I32, gt.shape, 1)
    is_f = ((lane >= M_HEADS) & (lane < 2 * M_HEADS)) | ((lane >= 3 * M_HEADS) & (lane < 4 * M_HEADS))
    gt_ref[...] = jnp.where(is_f, _log_sigmoid(gt), gt)


def _mlstm_in(y, mods, mrow, g1, w_main, w_gate, b_gate):
    return pl.pallas_call(
        _mlstm_in_kernel,
        out_shape=(jax.ShapeDtypeStruct((N_TOK, D), BF16), jax.ShapeDtypeStruct((N_TOK, D), BF16),
                   jax.ShapeDtypeStruct((N_TOK, D), BF16), jax.ShapeDtypeStruct((N_TOK, D), F32),
                   jax.ShapeDtypeStruct((N_TOK, LANE), F32)),
        grid_spec=pltpu.PrefetchScalarGridSpec(
            num_scalar_prefetch=1, grid=(NB,),
            in_specs=[_tok_spec(D), _mod_spec(), _full_spec((1, D)), _full_spec((D, 4 * D)),
                      _full_spec((D, LANE)), _full_spec((1, LANE))],
            out_specs=(_tok_spec(D), _tok_spec(D), _tok_spec(D), _tok_spec(D), _tok_spec(LANE))),
        compiler_params=_cparams(),
        name="mlstm_in",
    )(mrow, y, mods, g1, w_main, w_gate, b_gate)


def _mlstm_chunk(d, c, m, q_ref, k_ref, v_ref, gc_ref, gr_ref, c_scr, n_scr):
    L = M_CHUNK
    r0 = pl.multiple_of(c * L, L)
    q = q_ref[pl.ds(r0, L), :]
    k = k_ref[pl.ds(r0, L), :]
    v = v_ref[pl.ds(r0, L), :]
    col = gc_ref[0, pl.ds(r0, L), :]
    row = gr_ref[0, c]
    i_col, lf_col = col[:, 2 * d:2 * d + 1], col[:, 2 * d + 1:2 * d + 2]
    i_row, lf_row = row[2 * d:2 * d + 1, :], row[2 * d + 1:2 * d + 2, :]
    t_idx = lax.broadcasted_iota(I32, (L, L), 0)
    s_idx = lax.broadcasted_iota(I32, (L, L), 1)
    if d == 0:
        mask, mask_t = s_idx <= t_idx, t_idx <= s_idx
    else:
        mask, mask_t = s_idx >= t_idx, t_idx >= s_idx
    b_col = jnp.sum(jnp.where(mask, lf_row, 0.0), axis=1, keepdims=True)
    b_row = jnp.sum(jnp.where(mask_t, lf_col, 0.0), axis=0, keepdims=True)
    log_d = jnp.where(mask, b_col - b_row + i_row, -jnp.inf)
    li = b_col + m
    m_r = jnp.maximum(li, jnp.max(log_d, axis=1, keepdims=True))
    a_int = jnp.exp(li - m_r)
    s = lax.dot_general(q, k, (((1,), (1,)), ((), ())), preferred_element_type=F32) * jnp.exp(log_d - m_r)
    cmat = c_scr[d]
    num = a_int * jnp.dot(q, cmat.astype(BF16), preferred_element_type=F32) \
        + jnp.dot(s.astype(BF16), v, preferred_element_type=F32)
    qn = jnp.sum(q.astype(F32) * n_scr[d], axis=1, keepdims=True)
    den = a_int * qn + jnp.sum(s, axis=1, keepdims=True)
    hh = num / jnp.maximum(jnp.abs(den), jnp.exp(-m_r))
    b_last = b_row[:, L - 1:L] if d == 0 else b_row[:, 0:1]
    log_w = b_last - b_col + i_col
    m_new = jnp.maximum(b_last + m, jnp.max(log_w, axis=0, keepdims=True))
    w = jnp.exp(log_w - m_new)
    decay = jnp.exp(b_last + m - m_new)
    kw = k.astype(F32) * w
    c_scr[d] = decay * cmat + lax.dot_general(kw.astype(BF16), v, (((0,), (0,)), ((), ())),
                                              preferred_element_type=F32)
    n_scr[d] = decay * n_scr[d] + jnp.sum(kw, axis=0, keepdims=True)
    return r0, hh, m_new


def _mlstm_scan_body(n_chunks, q_ref, k_ref, v_ref, gc_ref, gr_ref, h_ref, c_scr, n_scr, m0):
    h_ref[...] = jnp.zeros(h_ref.shape, F32)

    def body(c, carry):
        mf, mb = carry
        r0, hf, mf = _mlstm_chunk(0, c, mf, q_ref, k_ref, v_ref, gc_ref, gr_ref, c_scr, n_scr)
        h_ref[pl.ds(r0, M_CHUNK), :] += hf
        r1, hb, mb = _mlstm_chunk(1, n_chunks - 1 - c, mb, q_ref, k_ref, v_ref, gc_ref, gr_ref, c_scr, n_scr)
        h_ref[pl.ds(r1, M_CHUNK), :] += hb
        return mf, mb

    return lax.fori_loop(0, n_chunks, body, m0)


def _mlstm_scan_ctx_kernel(q_ref, k_ref, v_ref, gc_ref, gr_ref, h_ref, cn_ref, nn_ref, mn_ref, c_scr, n_scr):
    c_scr[...] = jnp.zeros(c_scr.shape, F32)
    n_scr[...] = jnp.zeros(n_scr.shape, F32)
    zero = jnp.zeros((1, 1), F32)
    mf, mb = _mlstm_scan_body(SEQ // M_CHUNK, q_ref, k_ref, v_ref, gc_ref, gr_ref, h_ref, c_scr, n_scr, (zero, zero))
    for d, m in ((0, mf), (1, mb)):
        cn_ref[0, d, 0] = c_scr[d]
        nn_ref[0, d, 0] = n_scr[d]
        mn_ref[0, d, 0] = jnp.broadcast_to(m, (1, LANE))


def _mlstm_scan_lat_kernel(q_ref, k_ref, v_ref, gc_ref, gr_ref, c0_ref, n0_ref, m0_ref, h_ref, c_scr, n_scr):
    for d in range(2):
        c_scr[d] = c0_ref[0, d, 0]
        n_scr[d] = n0_ref[0, d, 0]
    m0 = (m0_ref[0, 0, 0], m0_ref[0, 1, 0])
    _mlstm_scan_body(DEC_SEQ // M_CHUNK, q_ref, k_ref, v_ref, gc_ref, gr_ref, h_ref, c_scr, n_scr, m0)


def _mlstm_scan(q, k, v, gates, state_c, state_n, state_m):
    g16 = gates[:, :4 * M_HEADS].reshape(N_TOK, 4, M_HEADS)
    gcol = jnp.transpose(g16, (2, 0, 1))
    grow = jnp.transpose(g16.reshape(N_TOK // M_CHUNK, M_CHUNK, 4, M_HEADS), (3, 0, 2, 1))
    hd = M_HEAD_DIM
    scratch = [pltpu.VMEM((2, hd, hd), F32), pltpu.VMEM((2, 1, hd), F32)]
    ncp = SEQ // M_CHUNK
    h_ctx, new_c, new_n, new_m = pl.pallas_call(
        _mlstm_scan_ctx_kernel,
        out_shape=(jax.ShapeDtypeStruct((NP_TOK, D), F32),
                   jax.ShapeDtypeStruct((BATCH, 2, M_HEADS, hd, hd), F32),
                   jax.ShapeDtypeStruct((BATCH, 2, M_HEADS, 1, hd), F32),
                   jax.ShapeDtypeStruct((BATCH, 2, M_HEADS, 1, LANE), F32)),
        grid=(BATCH, M_HEADS),
        in_specs=[
            pl.BlockSpec((SEQ, hd), lambda s, h: (s, h)),
            pl.BlockSpec((SEQ, hd), lambda s, h: (s, h)),
            pl.BlockSpec((SEQ, hd), lambda s, h: (s, h)),
            pl.BlockSpec((1, SEQ, 4), lambda s, h: (h, s, 0)),
            pl.BlockSpec((1, ncp, 4, M_CHUNK), lambda s, h: (h, s, 0, 0)),
        ],
        out_specs=(
            pl.BlockSpec((SEQ, hd), lambda s, h: (s, h)),
            pl.BlockSpec((1, 2, 1, hd, hd), lambda s, h: (s, 0, h, 0, 0)),
            pl.BlockSpec((1, 2, 1, 1, hd), lambda s, h: (s, 0, h, 0, 0)),
            pl.BlockSpec((1, 2, 1, 1, LANE), lambda s, h: (s, 0, h, 0, 0)),
        ),
        scratch_shapes=scratch,
        compiler_params=_cparams(2),
        name="mlstm_scan_ctx",
    )(q, k, v, gcol, grow)
    ncl = DEC_SEQ // M_CHUNK
    pb = NP_TOK // DEC_SEQ
    h_lat = pl.pallas_call(
        _mlstm_scan_lat_kernel,
        out_shape=jax.ShapeDtypeStruct((NS_TOK, D), F32),
        grid=(DEC_BATCH, M_HEADS),
        in_specs=[
            pl.BlockSpec((DEC_SEQ, hd), lambda b, h: (pb + b, h)),
            pl.BlockSpec((DEC_SEQ, hd), lambda b, h: (pb + b, h)),
            pl.BlockSpec((DEC_SEQ, hd), lambda b, h: (pb + b, h)),
            pl.BlockSpec((1, DEC_SEQ, 4), lambda b, h: (h, pb + b, 0)),
            pl.BlockSpec((1, ncl, 4, M_CHUNK), lambda b, h: (h, pb + b, 0, 0)),
            pl.BlockSpec((1, 2, 1, hd, hd), lambda b, h: (b, 0, h, 0, 0)),
            pl.BlockSpec((1, 2, 1, 1, hd), lambda b, h: (b, 0, h, 0, 0)),
            pl.BlockSpec((1, 2, 1, 1, 1), lambda b, h: (b, 0, h, 0, 0)),
        ],
        out_specs=pl.BlockSpec((DEC_SEQ, hd), lambda b, h: (b, h)),
        scratch_shapes=scratch,
        compiler_params=_cparams(2),
        name="mlstm_scan_lat",
    )(q, k, v, gcol, grow, state_c, state_n, state_m)
    return jnp.concatenate([h_ctx, h_lat], axis=0), new_c, new_n, new_m


def _mlstm_out_kernel(mr_ref, h_ref, o_ref, ng_ref, w_ref, y_ref, mod_ref, out_ref, x_scr):
    hc = o_ref[...] * h_ref[...]
    for hd in range(M_HEADS):
        sl = slice(hd * M_HEAD_DIM, (hd + 1) * M_HEAD_DIM)
        x_scr[:, sl] = _rms(hc[:, sl], ng_ref[:, sl]).astype(BF16)
    out = jnp.dot(x_scr[...], w_ref[...], preferred_element_type=F32)
    out_ref[...] = y_ref[...] + mod_ref[0][2:3] * out


def _mlstm_out(hsum, o, norm_g, w_out, y, mods, mrow):
    return pl.pallas_call(
        _mlstm_out_kernel,
        out_shape=jax.ShapeDtypeStruct((N_TOK, D), F32),
        grid_spec=pltpu.PrefetchScalarGridSpec(
            num_scalar_prefetch=1, grid=(NB,),
            in_specs=[_tok_spec(D), _tok_spec(D), _full_spec((1, D)), _full_spec((D, D)), _tok_spec(D), _mod_spec()],
            out_specs=_tok_spec(D),
            scratch_shapes=[pltpu.VMEM((TM, D), BF16)]),
        compiler_params=_cparams(),
        name="mlstm_out",
    )(mrow, hsum, o, norm_g, w_out, y, mods)


ROUTE_OFF = N_GROUPS
SLAB = D // LANE
V7X_SC_CORES = 2
V7X_SC_SUBCORES = 16
SC_WORKERS = V7X_SC_CORES * V7X_SC_SUBCORES
SC_WINDOW = 64


def _store_slabs(ref, x):
    rows = x.shape[0]
    for c in range(SLAB):
        ref[pl.ds(c, rows, stride=SLAB), :] = x[:, c * LANE:(c + 1) * LANE]


def _load_slabs(ref, dst, rows, dtype):
    for c in range(SLAB):
        dst[:, c * LANE:(c + 1) * LANE] = ref[pl.ds(c, rows, stride=SLAB), :].astype(dtype)


def _route_kernel(mr_ref, y_ref, mod_ref, g_ref, wr_ref, br_ref, x_ref, id_ref, wt_ref):
    x = _norm_mod(y_ref[...], g_ref[...], mod_ref[0], 1)
    _store_slabs(x_ref, x)
    lg = jnp.dot(x, wr_ref[...], preferred_element_type=F32, precision=lax.Precision.HIGHEST) + br_ref[...]
    lane = lax.broadcasted_iota(I32, lg.shape, 1)
    ninf = -jnp.inf
    big = jnp.int32(LANE)
    lgg = jnp.where(lane < N_GROUPS, lg, ninf)
    gmax = jnp.max(lgg, axis=-1, keepdims=True)
    g_idx = jnp.min(jnp.where(lgg == gmax, lane, big), axis=-1, keepdims=True)
    g_w = 1.0 / jnp.sum(jnp.exp(lgg - gmax), axis=-1, keepdims=True)
    lo = ROUTE_OFF + g_idx * EXPERTS_PER_GROUP
    le = jnp.where((lane >= lo) & (lane < lo + EXPERTS_PER_GROUP), lg, ninf)
    m1 = jnp.max(le, axis=-1, keepdims=True)
    i1 = jnp.min(jnp.where(le == m1, lane, big), axis=-1, keepdims=True)
    le2 = jnp.where(lane == i1, ninf, le)
    m2 = jnp.max(le2, axis=-1, keepdims=True)
    i2 = jnp.min(jnp.where(le2 == m2, lane, big), axis=-1, keepdims=True)
    r = jnp.exp(m2 - m1)
    p1 = 1.0 / (1.0 + r)
    p2 = r / (1.0 + r)
    two = lax.broadcasted_iota(I32, (x.shape[0], TOP_K), 1)
    id_ref[...] = jnp.where(two == 0, i1 - ROUTE_OFF, i2 - ROUTE_OFF)
    wt_ref[...] = jnp.where(two == 0, g_w * p1, g_w * p2)


def _route(y, mods, mrow, g2, w_route, b_route):
    return pl.pallas_call(
        _route_kernel,
        out_shape=(jax.ShapeDtypeStruct((N_TOK * SLAB, LANE), F32), jax.ShapeDtypeStruct((N_TOK, TOP_K), I32),
                   jax.ShapeDtypeStruct((N_TOK, TOP_K), F32)),
        grid_spec=pltpu.PrefetchScalarGridSpec(
            num_scalar_prefetch=1, grid=(NB,),
            in_specs=[_tok_spec(D), _mod_spec(), _full_spec((1, D)), _full_spec((D, LANE)), _full_spec((1, LANE))],
            out_specs=(pl.BlockSpec((TM * SLAB, LANE), lambda j, *_: (j, 0)), _tok_spec(TOP_K), _tok_spec(TOP_K))),
        compiler_params=_cparams(),
        name="moe_route",
    )(mrow, y, mods, g2, w_route, b_route)


def _dispatch_tables(expert_id):
    flat_e = expert_id.reshape(-1)
    onehot = (flat_e[:, None] == jnp.arange(N_EXPERTS, dtype=I32)[None, :]).astype(I32)
    csum = jnp.cumsum(onehot, axis=0)
    counts = csum[-1]
    padded = ((counts + EBLK - 1) // EBLK) * EBLK
    pad_end = jnp.cumsum(padded)
    pad_start = pad_end - padded
    dest = jnp.sum((csum - 1 + pad_start[None, :]) * onehot, axis=1).astype(I32)
    blk = jnp.arange(N_EBLK, dtype=I32) * EBLK
    block_e = jnp.minimum(jnp.sum((blk[:, None] >= pad_end[None, :]).astype(I32), axis=1), N_EXPERTS - 1)
    first = jnp.concatenate([jnp.ones((1,), I32), (block_e[1:] != block_e[:-1]).astype(I32)])
    dest2 = dest.reshape(N_TOK, TOP_K)
    return dest2[:, 0].reshape(1, N_TOK), dest2[:, 1].reshape(1, N_TOK), block_e.astype(I32), first


def _sc_mesh():
    return plsc.VectorSubcoreMesh(core_axis_name="core", subcore_axis_name="subcore",
                                  num_cores=V7X_SC_CORES, num_subcores=V7X_SC_SUBCORES)


def _sc_worker():
    return lax.axis_index("core") * V7X_SC_SUBCORES + lax.axis_index("subcore")


def _sc_dispatch(x_slabs, d0, d1):
    per = N_TOK // SC_WORKERS

    @functools.partial(
        pl.kernel, out_type=jax.ShapeDtypeStruct((P_SLOTS, SLAB, LANE), F32), mesh=_sc_mesh(), name="moe_dispatch",
        scratch_types=[pltpu.VMEM((1, per), I32), pltpu.VMEM((1, per), I32), pltpu.VMEM((SC_WINDOW, SLAB, LANE), F32)])
    def run(x_hbm, d0_hbm, d1_hbm, o_hbm, i0_v, i1_v, buf):
        base = _sc_worker() * per
        pltpu.sync_copy(d0_hbm.at[:, pl.ds(base, per)], i0_v)
        pltpu.sync_copy(d1_hbm.at[:, pl.ds(base, per)], i1_v)

        @pl.loop(0, per // SC_WINDOW)
        def _(s):
            off = s * SC_WINDOW
            pltpu.sync_copy(x_hbm.at[pl.ds(base + off, SC_WINDOW)], buf)
            pltpu.sync_copy(buf, o_hbm.at[i0_v.at[0, pl.ds(off, SC_WINDOW)]])
            pltpu.sync_copy(buf, o_hbm.at[i1_v.at[0, pl.ds(off, SC_WINDOW)]])

    return run(x_slabs.reshape(N_TOK, SLAB, LANE), d0, d1)


def _sc_collect(y_slabs, dcat):
    per = N_ASSIGN // SC_WORKERS

    @functools.partial(
        pl.kernel, out_type=jax.ShapeDtypeStruct((N_ASSIGN, SLAB, LANE), F32), mesh=_sc_mesh(), name="moe_collect",
        scratch_types=[pltpu.VMEM((1, per), I32), pltpu.VMEM((SC_WINDOW, SLAB, LANE), F32)])
    def run(y_hbm, i_hbm, o_hbm, i_v, buf):
        base = _sc_worker() * per
        pltpu.sync_copy(i_hbm.at[:, pl.ds(base, per)], i_v)

        @pl.loop(0, per // SC_WINDOW)
        def _(s):
            off = s * SC_WINDOW
            pltpu.sync_copy(y_hbm.at[i_v.at[0, pl.ds(off, SC_WINDOW)]], buf)
            pltpu.sync_copy(buf, o_hbm.at[pl.ds(base + off, SC_WINDOW)])

    return run(y_slabs.reshape(P_SLOTS, SLAB, LANE), dcat)


def _expert_kernel(be_ref, first_ref, x_ref, wg_ref, wu_ref, wd_ref, y_ref, xs, wg_bf, wu_bf, wd_bf):
    i = pl.program_id(0)

    @pl.when(first_ref[i] == 1)
    def _():
        wg_bf[...] = wg_ref[0, 0].astype(BF16)
        wu_bf[...] = wu_ref[0, 0].astype(BF16)
        wd_bf[...] = wd_ref[0, 0].astype(BF16)

    _load_slabs(x_ref, xs, EBLK, BF16)
    xb = xs[...]
    g = jnp.dot(xb, wg_bf[...], preferred_element_type=F32)
    u = jnp.dot(xb, wu_bf[...], preferred_element_type=F32)
    hmid = (g * _sigmoid(g) * u).astype(BF16)
    _store_slabs(y_ref, jnp.dot(hmid, wd_bf[...], preferred_element_type=F32))


def _experts(x_sorted, block_e, first, w_gate, w_up, w_down, layer):
    wspec = lambda r, c: pl.BlockSpec((1, 1, r, c), lambda i, be, fi: (layer, be[i], 0, 0))
    slab_spec = pl.BlockSpec((EBLK * SLAB, LANE), lambda i, be, fi: (i, 0))
    return pl.pallas_call(
        _expert_kernel,
        out_shape=jax.ShapeDtypeStruct((P_SLOTS * SLAB, LANE), F32),
        grid_spec=pltpu.PrefetchScalarGridSpec(
            num_scalar_prefetch=2, grid=(N_EBLK,),
            in_specs=[slab_spec, wspec(D, D_EXPERT), wspec(D, D_EXPERT), wspec(D_EXPERT, D)],
            out_specs=slab_spec,
            scratch_shapes=[
                pltpu.VMEM((EBLK, D), BF16),
                pltpu.VMEM((D, D_EXPERT), BF16), pltpu.VMEM((D, D_EXPERT), BF16), pltpu.VMEM((D_EXPERT, D), BF16),
            ]),
        compiler_params=_cparams(),
        name="moe_experts",
    )(block_e, first, x_sorted.reshape(P_SLOTS * SLAB, LANE), w_gate, w_up, w_down)


def _combine_kernel(mr_ref, e0_ref, e1_ref, wt_ref, y_ref, mod_ref, o_ref, a_scr, b_scr):
    _load_slabs(e0_ref, a_scr, TM, F32)
    _load_slabs(e1_ref, b_scr, TM, F32)
    wt = wt_ref[...]
    moe = wt[:, 0:1] * a_scr[...] + wt[:, 1:2] * b_scr[...]
    o_ref[...] = y_ref[...] + mod_ref[0][5:6] * moe


def _combine(ym, wts, y, mods, mrow):
    slab0 = pl.BlockSpec((TM * SLAB, LANE), lambda j, *_: (j, 0))
    slab1 = pl.BlockSpec((TM * SLAB, LANE), lambda j, *_: (NB + j, 0))
    return pl.pallas_call(
        _combine_kernel,
        out_shape=jax.ShapeDtypeStruct((N_TOK, D), F32),
        grid_spec=pltpu.PrefetchScalarGridSpec(
            num_scalar_prefetch=1, grid=(NB,),
            in_specs=[slab0, slab1, _tok_spec(TOP_K), _tok_spec(D), _mod_spec()],
            out_specs=_tok_spec(D),
            scratch_shapes=[pltpu.VMEM((TM, D), F32), pltpu.VMEM((TM, D), F32)]),
        compiler_params=_cparams(),
        name="moe_combine",
    )(mrow, ym, ym, wts, y, mods)


def _final_norm_kernel(y_ref, g_ref, o_ref):
    o_ref[...] = _rms(y_ref[...], g_ref[...])


def _final_norm(y, g):
    return pl.pallas_call(
        _final_norm_kernel,
        out_shape=jax.ShapeDtypeStruct((N_TOK, D), F32),
        grid=(NB,),
        in_specs=[pl.BlockSpec((TM, D), lambda j: (j, 0)), pl.BlockSpec((1, D), lambda j: (0, 0))],
        out_specs=pl.BlockSpec((TM, D), lambda j: (j, 0)),
        compiler_params=_cparams(),
        name="final_norm",
    )(y, g)


def kernel(x_prompt, x_sample, cache_attn_k, cache_attn_v, state_mlstm_C, state_mlstm_n, state_mlstm_m, c, c_ctx, ada_w, ada_b, norm1_g, norm2_g, conv_w_in, conv_w_dw, conv_b_dw, conv_ln_g, conv_ln_b, conv_w_out, attn_w_qkv, attn_q_norm, attn_k_norm, attn_w_o, mlstm_w_in, mlstm_b_gate, mlstm_norm_g, mlstm_w_out, moe_w_group, moe_b_group, moe_w_router, moe_b_router, moe_w_gate, moe_w_up, moe_w_down, final_norm_g):
    y = jnp.concatenate([x_prompt.reshape(NP_TOK, D), x_sample.reshape(NS_TOK, D)], axis=0)
    cvec = jnp.concatenate([c_ctx[None, :], c, jnp.zeros((MOD_ROWS - 1 - DEC_BATCH, D), F32)], axis=0)
    mods = _ada_all(cvec, ada_w, ada_b)
    rope = _rope_blocks()
    new_k = new_v = new_c = new_n = new_m = None
    for i in range(DEPTH):
        kind, slot = i % 3, i // 3
        mrow = jnp.asarray(_MOD_ROW + i * MOD_ROWS)
        g1 = norm1_g[i].reshape(1, D)
        if kind == 0:
            u = _conv_in(y, mods, mrow, g1, conv_w_in[slot].astype(BF16))
            w_dw = jnp.concatenate([conv_w_dw[slot], jnp.zeros((1, D), F32)], axis=0)
            y = _conv_main(u, y, mods, mrow, w_dw, conv_b_dw[slot].reshape(1, D), conv_ln_g[slot].reshape(1, D),
                           conv_ln_b[slot].reshape(1, D), conv_w_out[slot].astype(BF16))
        elif kind == 1:
            q, kb, vb, kf, vf = _attn_qkv(y, mods, mrow, g1, attn_w_qkv[slot].astype(BF16),
                                          attn_q_norm[slot].reshape(1, HEAD_DIM), attn_k_norm[slot].reshape(1, HEAD_DIM),
                                          rope)
            new_k = kf[:NP_TOK].reshape(BATCH, 1, SEQ, N_KV_HEADS, HEAD_DIM)
            new_v = vf[:NP_TOK].reshape(BATCH, 1, SEQ, N_KV_HEADS, HEAD_DIM)
            ck = cache_attn_k[:, slot].reshape(DEC_BATCH, PAST_LEN, KV_DIM)
            cv = cache_attn_v[:, slot].reshape(DEC_BATCH, PAST_LEN, KV_DIM)
            y = _attention(q, kb, vb, ck, cv, attn_w_o[slot].astype(BF16), y, mods, i)
        else:
            w_in = mlstm_w_in[slot]
            w_gate = jnp.concatenate([w_in[:, 4 * D:], jnp.zeros((D, LANE - 4 * M_HEADS), F32)], axis=1)
            b_gate = jnp.concatenate([mlstm_b_gate[slot], jnp.zeros((LANE - 4 * M_HEADS,), F32)]).reshape(1, LANE)
            q, k, v, o, gates = _mlstm_in(y, mods, mrow, g1, w_in[:, :4 * D].astype(BF16), w_gate, b_gate)
            sc = state_mlstm_C[:, slot]
            sn = state_mlstm_n[:, slot].reshape(DEC_BATCH, 2, M_HEADS, 1, M_HEAD_DIM)
            sm = state_mlstm_m[:, slot].reshape(DEC_BATCH, 2, M_HEADS, 1, 1)
            hsum, nc_, nn_, nm_ = _mlstm_scan(q, k, v, gates, sc, sn, sm)
            new_c = nc_[:, None]
            new_n = nn_.reshape(BATCH, 1, 2, M_HEADS, M_HEAD_DIM)
            new_m = nm_[..., 0, 0].reshape(BATCH, 1, 2, M_HEADS)
            y = _mlstm_out(hsum, o, mlstm_norm_g[slot].reshape(1, D), mlstm_w_out[slot].astype(BF16), y, mods, mrow)
        w_route = jnp.concatenate([moe_w_group[i], moe_w_router[i],
                                   jnp.zeros((D, LANE - N_GROUPS - N_EXPERTS), F32)], axis=1)
        b_route = jnp.concatenate([moe_b_group[i], moe_b_router[i],
                                   jnp.zeros((LANE - N_GROUPS - N_EXPERTS,), F32)]).reshape(1, LANE)
        x2, eid, ewt = _route(y, mods, mrow, norm2_g[i].reshape(1, D), w_route, b_route)
        d0, d1, block_e, first = _dispatch_tables(eid)
        x_sorted = _sc_dispatch(x2, d0, d1)
        y_sorted = _experts(x_sorted, block_e, first, moe_w_gate, moe_w_up, moe_w_down, i)
        ym = _sc_collect(y_sorted, jnp.concatenate([d0, d1], axis=1))
        y = _combine(ym.reshape(N_ASSIGN * SLAB, LANE), ewt, y, mods, mrow)
    yn = _final_norm(y, final_norm_g.reshape(1, D))
    y_prompt = yn[:NP_TOK].reshape(BATCH, SEQ, D)
    y_sample = yn[NP_TOK:].reshape(DEC_BATCH, DEC_SEQ, D)
    return (y_prompt, y_sample, new_k, new_v, new_c, new_n, new_m)
```

```python
import functools

import jax
import jax.numpy as jnp
import numpy as np
from jax import lax
from jax.experimental import pallas as pl
from jax.experimental.pallas import tpu as pltpu
from jax.experimental.pallas import tpu_sc as plsc

F32 = jnp.float32
BF16 = jnp.bfloat16
I32 = jnp.int32

D = 1024
BATCH, SEQ = 16, 256
DEC_BATCH, DEC_SEQ = 8, 1024
PAST_LEN = 256
DEPTH = 4
GRID_W = 64
EPS = 1e-6
CONV_WIDTH = 31
CONV_PAD = CONV_WIDTH // 2
HEAD_DIM = 128
N_HEADS = 8
N_KV_HEADS = 2
GQA_GROUP = N_HEADS // N_KV_HEADS
Q_DIM = N_HEADS * HEAD_DIM
KV_DIM = N_KV_HEADS * HEAD_DIM
QKV_DIM = Q_DIM + 2 * KV_DIM
ROPE_THETA = 10000.0
M_HEADS = 4
M_HEAD_DIM = D // M_HEADS
M_CHUNK = 64
N_GROUPS = 4
EXPERTS_PER_GROUP = 8
N_EXPERTS = N_GROUPS * EXPERTS_PER_GROUP
TOP_K = 2
D_EXPERT = 512

NP_TOK = BATCH * SEQ
NS_TOK = DEC_BATCH * DEC_SEQ
N_TOK = NP_TOK + NS_TOK
TM = 256
NB = N_TOK // TM
NBP = NP_TOK // TM
BLK_PER_DEC = DEC_SEQ // TM
MOD_ROWS = 16
HALO = 16
LANE = 128
SUBLANE = 8

N_ASSIGN = N_TOK * TOP_K
EBLK = 256
N_EBLK = N_ASSIGN // EBLK + N_EXPERTS
P_SLOTS = N_EBLK * EBLK
N_PAD_SLOTS = P_SLOTS - N_ASSIGN

VMEM_LIMIT = 56 * 1024 * 1024


def _block_tables():
    j = np.arange(NB)
    is_p = j < NBP
    mod_row = np.where(is_p, 0, 1 + (j - NBP) // BLK_PER_DEC)
    rope_idx = np.where(is_p, 0, 1 + (j - NBP) % BLK_PER_DEC)
    first = np.where(is_p, 1, ((j - NBP) % BLK_PER_DEC == 0).astype(np.int64))
    last = np.where(is_p, 1, ((j - NBP) % BLK_PER_DEC == BLK_PER_DEC - 1).astype(np.int64))
    return (mod_row.astype(np.int32), rope_idx.astype(np.int32), first.astype(np.int32), last.astype(np.int32))


_MOD_ROW, _ROPE_IDX, _SEQ_FIRST, _SEQ_LAST = _block_tables()


def _cparams(n_axes=1):
    return pltpu.CompilerParams(dimension_semantics=("arbitrary",) * n_axes, vmem_limit_bytes=VMEM_LIMIT)


def _sigmoid(x):
    return 1.0 / (1.0 + jnp.exp(-x))


def _rms(x, g):
    return x * lax.rsqrt(jnp.mean(x * x, axis=-1, keepdims=True) + EPS) * g


def _split_hi_lo(w):
    hi = w.astype(BF16)
    return jnp.stack([hi, (w - hi.astype(F32)).astype(BF16)], axis=0)


def _dot_hi_lo(x, w_ref):
    xh = x.astype(BF16)
    xl = (x - xh.astype(F32)).astype(BF16)
    wh, wl = w_ref[0], w_ref[1]
    return (jnp.dot(xh, wh, preferred_element_type=F32)
            + (jnp.dot(xh, wl, preferred_element_type=F32) + jnp.dot(xl, wh, preferred_element_type=F32)))


def _norm_mod(y, g, mod, which):
    shift = mod[3 * which:3 * which + 1]
    scale = mod[3 * which + 1:3 * which + 2]
    return _rms(y, g) * (1.0 + scale) + shift


def _ada_kernel(c_ref, w_ref, b_ref, o_ref):
    c = c_ref[...]
    s = c * _sigmoid(c)
    o_ref[0] = jnp.dot(s.astype(BF16), w_ref[0].astype(BF16), preferred_element_type=F32) + b_ref[0]


def _ada_all(cvec, ada_w, ada_b):
    tn = 1536
    out = pl.pallas_call(
        _ada_kernel,
        out_shape=jax.ShapeDtypeStruct((DEPTH, MOD_ROWS, 6 * D), F32),
        grid=(DEPTH, 6 * D // tn),
        in_specs=[
            pl.BlockSpec((MOD_ROWS, D), lambda l, n: (0, 0)),
            pl.BlockSpec((1, D, tn), lambda l, n: (l, 0, n)),
            pl.BlockSpec((1, 1, tn), lambda l, n: (l, 0, n)),
        ],
        out_specs=pl.BlockSpec((1, MOD_ROWS, tn), lambda l, n: (l, 0, n)),
        compiler_params=_cparams(2),
        name="ada_mod",
    )(cvec, ada_w, ada_b.reshape(DEPTH, 1, 6 * D))
    return out.reshape(DEPTH * MOD_ROWS, 6, D)


def _tok_spec(width):
    return pl.BlockSpec((TM, width), lambda j, *_: (j, 0))


def _mod_spec():
    return pl.BlockSpec((1, 6, D), lambda j, mr, *_: (mr[j], 0, 0))


def _full_spec(shape):
    nd = len(shape)
    return pl.BlockSpec(shape, lambda j, *_: (0,) * nd)


def _conv_in_kernel(mr_ref, y_ref, mod_ref, g_ref, w_ref, u_ref):
    h = _norm_mod(y_ref[...], g_ref[...], mod_ref[0], 0)
    ag = jnp.dot(h.astype(BF16), w_ref[...], preferred_element_type=F32)
    u_ref[...] = ag[:, :D] * _sigmoid(ag[:, D:])


def _conv_in(y, mods, mrow, g1, w_in):
    return pl.pallas_call(
        _conv_in_kernel,
        out_shape=jax.ShapeDtypeStruct((N_TOK, D), F32),
        grid_spec=pltpu.PrefetchScalarGridSpec(
            num_scalar_prefetch=1, grid=(NB,),
            in_specs=[_tok_spec(D), _mod_spec(), _full_spec((1, D)), _full_spec((D, 2 * D))],
            out_specs=_tok_spec(D)),
        compiler_params=_cparams(),
        name="conv_in",
    )(mrow, y, mods, g1, w_in)


def _conv_main_kernel(mr_ref, first_ref, last_ref, u_ref, up_ref, un_ref, wdw_ref, bdw_ref, lg_ref, lb_ref,
                      wout_ref, y_ref, mod_ref, o_ref, ext_ref, acc_ref):
    j = pl.program_id(0)
    zero = jnp.zeros((HALO, D), F32)
    ext_ref[0:HALO, :] = jnp.where(first_ref[j] == 1, zero, up_ref[...])
    ext_ref[HALO:HALO + TM, :] = u_ref[...]
    ext_ref[HALO + TM:2 * HALO + TM, :] = jnp.where(last_ref[j] == 1, zero, un_ref[...])

    off0 = HALO - CONV_PAD
    def strip(ci, carry):
        cs = pl.ds(pl.multiple_of(ci * LANE, LANE), LANE)
        wk = [jnp.broadcast_to(wdw_ref[k:k + 1, cs], (SUBLANE, LANE)) for k in range(CONV_WIDTH)]
        bias = jnp.broadcast_to(bdw_ref[:, cs], (SUBLANE, LANE))
        for base in range(0, TM, SUBLANE):
            acc = bias
            for k in range(CONV_WIDTH):
                acc = acc + ext_ref[base + off0 + k:base + off0 + k + SUBLANE, cs] * wk[k]
            acc_ref[base:base + SUBLANE, cs] = acc
        return carry

    lax.fori_loop(0, D // LANE, strip, 0)

    c = acc_ref[...]
    mu = jnp.mean(c, axis=-1, keepdims=True)
    cc = c - mu
    var = jnp.mean(cc * cc, axis=-1, keepdims=True)
    z = cc * lax.rsqrt(var + EPS) * lg_ref[...] + lb_ref[...]
    z = z * _sigmoid(z)
    out = jnp.dot(z.astype(BF16), wout_ref[...], preferred_element_type=F32)
    o_ref[...] = y_ref[...] + mod_ref[0][2:3] * out


def _conv_main(u, y, mods, mrow, w_dw, b_dw, ln_g, ln_b, w_out):
    nh = N_TOK // HALO
    per = TM // HALO
    return pl.pallas_call(
        _conv_main_kernel,
        out_shape=jax.ShapeDtypeStruct((N_TOK, D), F32),
        grid_spec=pltpu.PrefetchScalarGridSpec(
            num_scalar_prefetch=3, grid=(NB,),
            in_specs=[
                _tok_spec(D),
                pl.BlockSpec((HALO, D), lambda j, *_: (jnp.maximum(j * per - 1, 0), 0)),
                pl.BlockSpec((HALO, D), lambda j, *_: (jnp.minimum((j + 1) * per, nh - 1), 0)),
                _full_spec((CONV_WIDTH + 1, D)), _full_spec((1, D)), _full_spec((1, D)), _full_spec((1, D)),
                _full_spec((D, D)), _tok_spec(D), _mod_spec(),
            ],
            out_specs=_tok_spec(D),
            scratch_shapes=[pltpu.VMEM((TM + 2 * HALO, D), F32), pltpu.VMEM((TM, D), F32)]),
        compiler_params=_cparams(),
        name="conv_main",
    )(mrow, jnp.asarray(_SEQ_FIRST), jnp.asarray(_SEQ_LAST), u, u, u, w_dw, b_dw, ln_g, ln_b, w_out, y, mods)


def _rope_angles():
    rows = DEC_SEQ // GRID_W
    row = jnp.repeat(jnp.arange(rows, dtype=F32), GRID_W)
    col = jnp.tile(jnp.arange(GRID_W, dtype=F32), rows)
    axis_dim = HEAD_DIM // 2
    freqs = jnp.power(ROPE_THETA, -jnp.arange(axis_dim // 2, dtype=F32) * 2.0 / axis_dim)
    ang_r = row[:, None] * freqs[None, :]
    ang_c = col[:, None] * freqs[None, :]
    return jnp.concatenate([ang_r, ang_r, ang_c, ang_c], axis=-1)


def _rope_blocks():
    ang = _rope_angles()
    cos, sin = jnp.cos(ang), jnp.sin(ang)
    lane = np.arange(HEAD_DIM)
    lo = jnp.asarray(((lane % (HEAD_DIM // 2)) < HEAD_DIM // 4).astype(np.float32))
    sin_a = -sin * lo[None, :]
    sin_b = sin * (1.0 - lo)[None, :]
    nblk = DEC_SEQ // TM
    ident = jnp.ones((1, TM, HEAD_DIM), F32)
    zeros = jnp.zeros((1, TM, HEAD_DIM), F32)
    cos_t = jnp.concatenate([ident, cos.reshape(nblk, TM, HEAD_DIM)], axis=0)
    sa_t = jnp.concatenate([zeros, sin_a.reshape(nblk, TM, HEAD_DIM)], axis=0)
    sb_t = jnp.concatenate([zeros, sin_b.reshape(nblk, TM, HEAD_DIM)], axis=0)
    return cos_t, sa_t, sb_t


def _attn_qkv_kernel(mr_ref, ri_ref, y_ref, mod_ref, g_ref, w_ref, qg_ref, kg_ref, cos_ref, sa_ref, sb_ref,
                     q_ref, kb_ref, vb_ref, kf_ref, vf_ref):
    h = _norm_mod(y_ref[...], g_ref[...], mod_ref[0], 0)
    qkv = jnp.dot(h.astype(BF16), w_ref[...], preferred_element_type=F32)
    cos, sa, sb = cos_ref[0], sa_ref[0], sb_ref[0]
    quarter = HEAD_DIM // 4

    def head(x, g):
        xn = _rms(x, g)
        return xn * cos + pltpu.roll(xn, HEAD_DIM - quarter, 1) * sa + pltpu.roll(xn, quarter, 1) * sb

    scale = HEAD_DIM ** -0.5
    for hd in range(N_HEADS):
        sl = slice(hd * HEAD_DIM, (hd + 1) * HEAD_DIM)
        q_ref[:, sl] = (head(qkv[:, sl], qg_ref[...]) * scale).astype(BF16)
    for kv in range(N_KV_HEADS):
        sl = slice(kv * HEAD_DIM, (kv + 1) * HEAD_DIM)
        kr = head(qkv[:, Q_DIM + kv * HEAD_DIM:Q_DIM + (kv + 1) * HEAD_DIM], kg_ref[...])
        kf_ref[:, sl] = kr
        kb_ref[:, sl] = kr.astype(BF16)
    v = qkv[:, Q_DIM + KV_DIM:]
    vf_ref[...] = v
    vb_ref[...] = v.astype(BF16)


def _attn_qkv(y, mods, mrow, g1, w_qkv, q_g, k_g, rope):
    cos_t, sa_t, sb_t = rope
    rspec = pl.BlockSpec((1, TM, HEAD_DIM), lambda j, mr, ri: (ri[j], 0, 0))
    return pl.pallas_call(
        _attn_qkv_kernel,
        out_shape=(jax.ShapeDtypeStruct((N_TOK, Q_DIM), BF16), jax.ShapeDtypeStruct((N_TOK, KV_DIM), BF16),
                   jax.ShapeDtypeStruct((N_TOK, KV_DIM), BF16), jax.ShapeDtypeStruct((N_TOK, KV_DIM), F32),
                   jax.ShapeDtypeStruct((N_TOK, KV_DIM), F32)),
        grid_spec=pltpu.PrefetchScalarGridSpec(
            num_scalar_prefetch=2, grid=(NB,),
            in_specs=[_tok_spec(D), _mod_spec(), _full_spec((1, D)), _full_spec((D, QKV_DIM)),
                      _full_spec((1, HEAD_DIM)), _full_spec((1, HEAD_DIM)), rspec, rspec, rspec],
            out_specs=(_tok_spec(Q_DIM), _tok_spec(KV_DIM), _tok_spec(KV_DIM), _tok_spec(KV_DIM), _tok_spec(KV_DIM))),
        compiler_params=_cparams(),
        name="attn_qkv",
    )(mrow, jnp.asarray(_ROPE_IDX), y, mods, g1, w_qkv, q_g, k_g, cos_t, sa_t, sb_t)


def _attn_heads(q, ks, vs, o_scr):
    nt = (((1,), (1,)), ((), ()))
    for hd in range(N_HEADS):
        g = hd // GQA_GROUP
        qh = q[:, hd * HEAD_DIM:(hd + 1) * HEAD_DIM]
        gs = slice(g * HEAD_DIM, (g + 1) * HEAD_DIM)
        ss = [lax.dot_general(qh, k[:, gs], nt, preferred_element_type=F32) for k in ks]
        m = functools.reduce(jnp.maximum, [jnp.max(s, axis=-1, keepdims=True) for s in ss])
        ps = [jnp.exp(s - m) for s in ss]
        l = functools.reduce(lambda a, b: a + b, [jnp.sum(p, axis=-1, keepdims=True) for p in ps])
        o = functools.reduce(lambda a, b: a + b,
                             [jnp.dot(p.astype(BF16), v[:, gs], preferred_element_type=F32) for p, v in zip(ps, vs)])
        o_scr[:, hd * HEAD_DIM:(hd + 1) * HEAD_DIM] = (o / l).astype(BF16)


def _attn_ctx_kernel(q_ref, k_ref, v_ref, wo_ref, y_ref, mod_ref, o_ref, o_scr):
    _attn_heads(q_ref[...], [k_ref[...]], [v_ref[...]], o_scr)
    out = jnp.dot(o_scr[...], wo_ref[...], preferred_element_type=F32)
    o_ref[...] = y_ref[...] + mod_ref[0][2:3] * out


def _attn_lat_kernel(q_ref, k_ref, v_ref, ck_ref, cv_ref, wo_ref, y_ref, mod_ref, ctx_out_ref, o_ref, o_scr):
    del ctx_out_ref
    _attn_heads(q_ref[...], [k_ref[...], ck_ref[0].astype(BF16)], [v_ref[...], cv_ref[0].astype(BF16)], o_scr)
    out = jnp.dot(o_scr[...], wo_ref[...], preferred_element_type=F32)
    o_ref[...] = y_ref[...] + mod_ref[0][2:3] * out


def _attention(q, kb, vb, cache_k, cache_v, w_o, y, mods, layer):
    y_ctx = pl.pallas_call(
        _attn_ctx_kernel,
        out_shape=jax.ShapeDtypeStruct((N_TOK, D), F32),
        grid=(BATCH,),
        in_specs=[
            pl.BlockSpec((SEQ, Q_DIM), lambda s: (s, 0)),
            pl.BlockSpec((SEQ, KV_DIM), lambda s: (s, 0)),
            pl.BlockSpec((SEQ, KV_DIM), lambda s: (s, 0)),
            pl.BlockSpec((Q_DIM, D), lambda s: (0, 0)),
            pl.BlockSpec((SEQ, D), lambda s: (s, 0)),
            pl.BlockSpec((1, 6, D), lambda s: (layer * MOD_ROWS, 0, 0)),
        ],
        out_specs=pl.BlockSpec((SEQ, D), lambda s: (s, 0)),
        scratch_shapes=[pltpu.VMEM((SEQ, Q_DIM), BF16)],
        compiler_params=_cparams(),
        name="attn_ctx",
    )(q, kb, vb, w_o, y, mods)
    pb = NP_TOK // DEC_SEQ
    return pl.pallas_call(
        _attn_lat_kernel,
        out_shape=jax.ShapeDtypeStruct((N_TOK, D), F32),
        input_output_aliases={8: 0},
        grid=(DEC_BATCH, BLK_PER_DEC),
        in_specs=[
            pl.BlockSpec((TM, Q_DIM), lambda b, t: (NBP + b * BLK_PER_DEC + t, 0)),
            pl.BlockSpec((DEC_SEQ, KV_DIM), lambda b, t: (pb + b, 0)),
            pl.BlockSpec((DEC_SEQ, KV_DIM), lambda b, t: (pb + b, 0)),
            pl.BlockSpec((1, PAST_LEN, KV_DIM), lambda b, t: (b, 0, 0)),
            pl.BlockSpec((1, PAST_LEN, KV_DIM), lambda b, t: (b, 0, 0)),
            pl.BlockSpec((Q_DIM, D), lambda b, t: (0, 0)),
            pl.BlockSpec((TM, D), lambda b, t: (NBP + b * BLK_PER_DEC + t, 0)),
            pl.BlockSpec((1, 6, D), lambda b, t: (layer * MOD_ROWS + 1 + b, 0, 0)),
            pl.BlockSpec(memory_space=pl.ANY),
        ],
        out_specs=pl.BlockSpec((TM, D), lambda b, t: (NBP + b * BLK_PER_DEC + t, 0)),
        scratch_shapes=[pltpu.VMEM((TM, Q_DIM), BF16)],
        compiler_params=_cparams(2),
        name="attn_lat",
    )(q, kb, vb, cache_k, cache_v, w_o, y, mods, y_ctx)


def _log_sigmoid(x):
    return jnp.minimum(x, 0.0) - jnp.log(1.0 + jnp.exp(-jnp.abs(x)))


def _mlstm_in_kernel(mr_ref, y_ref, mod_ref, g_ref, w_ref, wg_ref, bg_ref, q_ref, k_ref, v_ref, o_ref, gt_ref):
    h = _norm_mod(y_ref[...], g_ref[...], mod_ref[0], 0)
    hb = h.astype(BF16)
    q_ref[...] = jnp.dot(hb, w_ref[:, 0:D], preferred_element_type=F32).astype(BF16)
    k_ref[...] = (jnp.dot(hb, w_ref[:, D:2 * D], preferred_element_type=F32) * (M_HEAD_DIM ** -0.5)).astype(BF16)
    v_ref[...] = jnp.dot(hb, w_ref[:, 2 * D:3 * D], preferred_element_type=F32).astype(BF16)
    o_ref[...] = _sigmoid(jnp.dot(hb, w_ref[:, 3 * D:4 * D], preferred_element_type=F32))
    gt = _dot_hi_lo(h, wg_ref) + bg_ref[...]
    lane = lax.broadcasted_iota(I32, gt.shape, 1)
    is_f = ((lane >= M_HEADS) & (lane < 2 * M_HEADS)) | ((lane >= 3 * M_HEADS) & (lane < 4 * M_HEADS))
    gt_ref[...] = jnp.where(is_f, _log_sigmoid(gt), gt)


def _mlstm_in(y, mods, mrow, g1, w_main, w_gate, b_gate):
    return pl.pallas_call(
        _mlstm_in_kernel,
        out_shape=(jax.ShapeDtypeStruct((N_TOK, D), BF16), jax.ShapeDtypeStruct((N_TOK, D), BF16),
                   jax.ShapeDtypeStruct((N_TOK, D), BF16), jax.ShapeDtypeStruct((N_TOK, D), F32),
                   jax.ShapeDtypeStruct((N_TOK, LANE), F32)),
        grid_spec=pltpu.PrefetchScalarGridSpec(
            num_scalar_prefetch=1, grid=(NB,),
            in_specs=[_tok_spec(D), _mod_spec(), _full_spec((1, D)), _full_spec((D, 4 * D)),
                      _full_spec((2, D, LANE)), _full_spec((1, LANE))],
            out_specs=(_tok_spec(D), _tok_spec(D), _tok_spec(D), _tok_spec(D), _tok_spec(LANE))),
        compiler_params=_cparams(),
        name="mlstm_in",
    )(mrow, y, mods, g1, w_main, w_gate, b_gate)


def _mlstm_chunk(d, c, m, q_ref, k_ref, v_ref, gc_ref, gr_ref, c_scr, n_scr):
    L = M_CHUNK
    r0 = pl.multiple_of(c * L, L)
    q = q_ref[pl.ds(r0, L), :]
    k = k_ref[pl.ds(r0, L), :]
    v = v_ref[pl.ds(r0, L), :]
    col = gc_ref[0, pl.ds(r0, L), :]
    row = gr_ref[0, c]
    i_col, lf_col = col[:, 2 * d:2 * d + 1], col[:, 2 * d + 1:2 * d + 2]
    i_row, lf_row = row[2 * d:2 * d + 1, :], row[2 * d + 1:2 * d + 2, :]
    t_idx = lax.broadcasted_iota(I32, (L, L), 0)
    s_idx = lax.broadcasted_iota(I32, (L, L), 1)
    if d == 0:
        mask, mask_t = s_idx <= t_idx, t_idx <= s_idx
    else:
        mask, mask_t = s_idx >= t_idx, t_idx >= s_idx
    b_col = jnp.sum(jnp.where(mask, lf_row, 0.0), axis=1, keepdims=True)
    b_row = jnp.sum(jnp.where(mask_t, lf_col, 0.0), axis=0, keepdims=True)
    log_d = jnp.where(mask, b_col - b_row + i_row, -jnp.inf)
    li = b_col + m
    m_r = jnp.maximum(li, jnp.max(log_d, axis=1, keepdims=True))
    a_int = jnp.exp(li - m_r)
    s = lax.dot_general(q, k, (((1,), (1,)), ((), ())), preferred_element_type=F32) * jnp.exp(log_d - m_r)
    cmat = c_scr[d]
    num = a_int * jnp.dot(q, cmat.astype(BF16), preferred_element_type=F32) \
        + jnp.dot(s.astype(BF16), v, preferred_element_type=F32)
    qn = jnp.sum(q.astype(F32) * n_scr[d], axis=1, keepdims=True)
    den = a_int * qn + jnp.sum(s, axis=1, keepdims=True)
    hh = num / jnp.maximum(jnp.abs(den), jnp.exp(-m_r))
    b_last = b_row[:, L - 1:L] if d == 0 else b_row[:, 0:1]
    log_w = b_last - b_col + i_col
    m_new = jnp.maximum(b_last + m, jnp.max(log_w, axis=0, keepdims=True))
    w = jnp.exp(log_w - m_new)
    decay = jnp.exp(b_last + m - m_new)
    kw = k.astype(F32) * w
    c_scr[d] = decay * cmat + lax.dot_general(kw.astype(BF16), v, (((0,), (0,)), ((), ())),
                                              preferred_element_type=F32)
    n_scr[d] = decay * n_scr[d] + jnp.sum(kw, axis=0, keepdims=True)
    return r0, hh, m_new


def _mlstm_scan_body(n_chunks, q_ref, k_ref, v_ref, gc_ref, gr_ref, h_ref, c_scr, n_scr, m0):
    h_ref[...] = jnp.zeros(h_ref.shape, F32)

    def body(c, carry):
        mf, mb = carry
        r0, hf, mf = _mlstm_chunk(0, c, mf, q_ref, k_ref, v_ref, gc_ref, gr_ref, c_scr, n_scr)
        h_ref[pl.ds(r0, M_CHUNK), :] += hf
        r1, hb, mb = _mlstm_chunk(1, n_chunks - 1 - c, mb, q_ref, k_ref, v_ref, gc_ref, gr_ref, c_scr, n_scr)
        h_ref[pl.ds(r1, M_CHUNK), :] += hb
        return mf, mb

    return lax.fori_loop(0, n_chunks, body, m0)


def _mlstm_scan_ctx_kernel(q_ref, k_ref, v_ref, gc_ref, gr_ref, h_ref, cn_ref, nn_ref, mn_ref, c_scr, n_scr):
    c_scr[...] = jnp.zeros(c_scr.shape, F32)
    n_scr[...] = jnp.zeros(n_scr.shape, F32)
    zero = jnp.zeros((1, 1), F32)
    mf, mb = _mlstm_scan_body(SEQ // M_CHUNK, q_ref, k_ref, v_ref, gc_ref, gr_ref, h_ref, c_scr, n_scr, (zero, zero))
    for d, m in ((0, mf), (1, mb)):
        cn_ref[0, d, 0] = c_scr[d]
        nn_ref[0, d, 0] = n_scr[d]
        mn_ref[0, d, 0] = jnp.broadcast_to(m, (1, LANE))


def _mlstm_scan_lat_kernel(q_ref, k_ref, v_ref, gc_ref, gr_ref, c0_ref, n0_ref, m0_ref, ctx_out_ref, h_ref,
                           c_scr, n_scr):
    del ctx_out_ref
    for d in range(2):
        c_scr[d] = c0_ref[0, d, 0]
        n_scr[d] = n0_ref[0, d, 0]
    m0 = (m0_ref[0, 0, 0], m0_ref[0, 1, 0])
    _mlstm_scan_body(DEC_SEQ // M_CHUNK, q_ref, k_ref, v_ref, gc_ref, gr_ref, h_ref, c_scr, n_scr, m0)


def _mlstm_scan(q, k, v, gates, state_c, state_n, state_m):
    g16 = gates[:, :4 * M_HEADS].reshape(N_TOK, 4, M_HEADS)
    gcol = jnp.transpose(g16, (2, 0, 1))
    grow = jnp.transpose(g16.reshape(N_TOK // M_CHUNK, M_CHUNK, 4, M_HEADS), (3, 0, 2, 1))
    hd = M_HEAD_DIM
    scratch = [pltpu.VMEM((2, hd, hd), F32), pltpu.VMEM((2, 1, hd), F32)]
    ncp = SEQ // M_CHUNK
    h_ctx, new_c, new_n, new_m = pl.pallas_call(
        _mlstm_scan_ctx_kernel,
        out_shape=(jax.ShapeDtypeStruct((N_TOK, D), F32),
                   jax.ShapeDtypeStruct((BATCH, 2, M_HEADS, hd, hd), F32),
                   jax.ShapeDtypeStruct((BATCH, 2, M_HEADS, 1, hd), F32),
                   jax.ShapeDtypeStruct((BATCH, 2, M_HEADS, 1, LANE), F32)),
        grid=(BATCH, M_HEADS),
        in_specs=[
            pl.BlockSpec((SEQ, hd), lambda s, h: (s, h)),
            pl.BlockSpec((SEQ, hd), lambda s, h: (s, h)),
            pl.BlockSpec((SEQ, hd), lambda s, h: (s, h)),
            pl.BlockSpec((1, SEQ, 4), lambda s, h: (h, s, 0)),
            pl.BlockSpec((1, ncp, 4, M_CHUNK), lambda s, h: (h, s, 0, 0)),
        ],
        out_specs=(
            pl.BlockSpec((SEQ, hd), lambda s, h: (s, h)),
            pl.BlockSpec((1, 2, 1, hd, hd), lambda s, h: (s, 0, h, 0, 0)),
            pl.BlockSpec((1, 2, 1, 1, hd), lambda s, h: (s, 0, h, 0, 0)),
            pl.BlockSpec((1, 2, 1, 1, LANE), lambda s, h: (s, 0, h, 0, 0)),
        ),
        scratch_shapes=scratch,
        compiler_params=_cparams(2),
        name="mlstm_scan_ctx",
    )(q, k, v, gcol, grow)
    ncl = DEC_SEQ // M_CHUNK
    pb = NP_TOK // DEC_SEQ
    h_all = pl.pallas_call(
        _mlstm_scan_lat_kernel,
        out_shape=jax.ShapeDtypeStruct((N_TOK, D), F32),
        input_output_aliases={8: 0},
        grid=(DEC_BATCH, M_HEADS),
        in_specs=[
            pl.BlockSpec((DEC_SEQ, hd), lambda b, h: (pb + b, h)),
            pl.BlockSpec((DEC_SEQ, hd), lambda b, h: (pb + b, h)),
            pl.BlockSpec((DEC_SEQ, hd), lambda b, h: (pb + b, h)),
            pl.BlockSpec((1, DEC_SEQ, 4), lambda b, h: (h, pb + b, 0)),
            pl.BlockSpec((1, ncl, 4, M_CHUNK), lambda b, h: (h, pb + b, 0, 0)),
            pl.BlockSpec((1, 2, 1, hd, hd), lambda b, h: (b, 0, h, 0, 0)),
            pl.BlockSpec((1, 2, 1, 1, hd), lambda b, h: (b, 0, h, 0, 0)),
            pl.BlockSpec((1, 2, 1, 1, 1), lambda b, h: (b, 0, h, 0, 0)),
            pl.BlockSpec(memory_space=pl.ANY),
        ],
        out_specs=pl.BlockSpec((DEC_SEQ, hd), lambda b, h: (pb + b, h)),
        scratch_shapes=scratch,
        compiler_params=_cparams(2),
        name="mlstm_scan_lat",
    )(q, k, v, gcol, grow, state_c, state_n, state_m, h_ctx)
    return h_all, new_c, new_n, new_m


def _mlstm_out_kernel(mr_ref, h_ref, o_ref, ng_ref, w_ref, y_ref, mod_ref, out_ref, x_scr):
    hc = o_ref[...] * h_ref[...]
    for hd in range(M_HEADS):
        sl = slice(hd * M_HEAD_DIM, (hd + 1) * M_HEAD_DIM)
        x_scr[:, sl] = _rms(hc[:, sl], ng_ref[:, sl]).astype(BF16)
    out = jnp.dot(x_scr[...], w_ref[...], preferred_element_type=F32)
    out_ref[...] = y_ref[...] + mod_ref[0][2:3] * out


def _mlstm_out(hsum, o, norm_g, w_out, y, mods, mrow):
    return pl.pallas_call(
        _mlstm_out_kernel,
        out_shape=jax.ShapeDtypeStruct((N_TOK, D), F32),
        grid_spec=pltpu.PrefetchScalarGridSpec(
            num_scalar_prefetch=1, grid=(NB,),
            in_specs=[_tok_spec(D), _tok_spec(D), _full_spec((1, D)), _full_spec((D, D)), _tok_spec(D), _mod_spec()],
            out_specs=_tok_spec(D),
            scratch_shapes=[pltpu.VMEM((TM, D), BF16)]),
        compiler_params=_cparams(),
        name="mlstm_out",
    )(mrow, hsum, o, norm_g, w_out, y, mods)


ROUTE_OFF = N_GROUPS
SLAB = D // LANE
V7X_SC_CORES = 2
V7X_SC_SUBCORES = 16
SC_WORKERS = V7X_SC_CORES * V7X_SC_SUBCORES
SC_WINDOW = 64


def _store_slabs(ref, x):
    rows = x.shape[0]
    for c in range(SLAB):
        ref[pl.ds(c, rows, stride=SLAB), :] = x[:, c * LANE:(c + 1) * LANE]


def _load_slabs(ref, dst, rows, dtype):
    for c in range(SLAB):
        dst[:, c * LANE:(c + 1) * LANE] = ref[pl.ds(c, rows, stride=SLAB), :].astype(dtype)


def _route_kernel(mr_ref, y_ref, mod_ref, g_ref, wr_ref, br_ref, x_ref, id_ref, wt_ref):
    x = _norm_mod(y_ref[...], g_ref[...], mod_ref[0], 1)
    _store_slabs(x_ref, x)
    lg = _dot_hi_lo(x, wr_ref) + br_ref[...]
    lane = lax.broadcasted_iota(I32, lg.shape, 1)
    ninf = -jnp.inf
    big = jnp.int32(LANE)
    lgg = jnp.where(lane < N_GROUPS, lg, ninf)
    gmax = jnp.max(lgg, axis=-1, keepdims=True)
    g_idx = jnp.min(jnp.where(lgg == gmax, lane, big), axis=-1, keepdims=True)
    g_w = 1.0 / jnp.sum(jnp.exp(lgg - gmax), axis=-1, keepdims=True)
    lo = ROUTE_OFF + g_idx * EXPERTS_PER_GROUP
    le = jnp.where((lane >= lo) & (lane < lo + EXPERTS_PER_GROUP), lg, ninf)
    m1 = jnp.max(le, axis=-1, keepdims=True)
    i1 = jnp.min(jnp.where(le == m1, lane, big), axis=-1, keepdims=True)
    le2 = jnp.where(lane == i1, ninf, le)
    m2 = jnp.max(le2, axis=-1, keepdims=True)
    i2 = jnp.min(jnp.where(le2 == m2, lane, big), axis=-1, keepdims=True)
    r = jnp.exp(m2 - m1)
    p1 = 1.0 / (1.0 + r)
    p2 = r / (1.0 + r)
    two = lax.broadcasted_iota(I32, (x.shape[0], TOP_K), 1)
    id_ref[...] = jnp.where(two == 0, i1 - ROUTE_OFF, i2 - ROUTE_OFF)
    wt_ref[...] = jnp.where(two == 0, g_w * p1, g_w * p2)


def _route(y, mods, mrow, g2, w_route, b_route):
    return pl.pallas_call(
        _route_kernel,
        out_shape=(jax.ShapeDtypeStruct((N_TOK * SLAB, LANE), F32), jax.ShapeDtypeStruct((N_TOK, TOP_K), I32),
                   jax.ShapeDtypeStruct((N_TOK, TOP_K), F32)),
        grid_spec=pltpu.PrefetchScalarGridSpec(
            num_scalar_prefetch=1, grid=(NB,),
            in_specs=[_tok_spec(D), _mod_spec(), _full_spec((1, D)), _full_spec((2, D, LANE)), _full_spec((1, LANE))],
            out_specs=(pl.BlockSpec((TM * SLAB, LANE), lambda j, *_: (j, 0)), _tok_spec(TOP_K), _tok_spec(TOP_K))),
        compiler_params=_cparams(),
        name="moe_route",
    )(mrow, y, mods, g2, w_route, b_route)


def _dispatch_tables(expert_id):
    flat_e = expert_id.reshape(-1)
    onehot = (flat_e[:, None] == jnp.arange(N_EXPERTS, dtype=I32)[None, :]).astype(I32)
    csum = jnp.cumsum(onehot, axis=0)
    counts = csum[-1]
    padded = ((counts + EBLK - 1) // EBLK) * EBLK
    pad_end = jnp.cumsum(padded)
    pad_start = pad_end - padded
    dest = jnp.sum((csum - 1 + pad_start[None, :]) * onehot, axis=1).astype(I32)
    blk = jnp.arange(N_EBLK, dtype=I32) * EBLK
    n_used = (pad_end[-1] // EBLK).astype(I32)
    last_e = jnp.sum((pad_end[-1] - 1 >= pad_end).astype(I32))
    block_e = jnp.minimum(jnp.sum((blk[:, None] >= pad_end[None, :]).astype(I32), axis=1), last_e).astype(I32)
    first = jnp.concatenate([jnp.ones((1,), I32), (block_e[1:] != block_e[:-1]).astype(I32)])
    dest2 = dest.reshape(N_TOK, TOP_K)
    return dest2[:, 0].reshape(1, N_TOK), dest2[:, 1].reshape(1, N_TOK), block_e, first, n_used.reshape(1)


def _sc_mesh():
    return plsc.VectorSubcoreMesh(core_axis_name="core", subcore_axis_name="subcore",
                                  num_cores=V7X_SC_CORES, num_subcores=V7X_SC_SUBCORES)


def _sc_worker():
    return lax.axis_index("core") * V7X_SC_SUBCORES + lax.axis_index("subcore")


def _sc_dispatch(x_slabs, d0, d1):
    per = N_TOK // SC_WORKERS

    @functools.partial(
        pl.kernel, out_type=jax.ShapeDtypeStruct((P_SLOTS, SLAB, LANE), F32), mesh=_sc_mesh(), name="moe_dispatch",
        scratch_types=[pltpu.VMEM((1, per), I32), pltpu.VMEM((1, per), I32), pltpu.VMEM((SC_WINDOW, SLAB, LANE), F32)])
    def run(x_hbm, d0_hbm, d1_hbm, o_hbm, i0_v, i1_v, buf):
        base = _sc_worker() * per
        pltpu.sync_copy(d0_hbm.at[:, pl.ds(base, per)], i0_v)
        pltpu.sync_copy(d1_hbm.at[:, pl.ds(base, per)], i1_v)

        @pl.loop(0, per // SC_WINDOW)
        def _(s):
            off = s * SC_WINDOW
            pltpu.sync_copy(x_hbm.at[pl.ds(base + off, SC_WINDOW)], buf)
            pltpu.sync_copy(buf, o_hbm.at[i0_v.at[0, pl.ds(off, SC_WINDOW)]])
            pltpu.sync_copy(buf, o_hbm.at[i1_v.at[0, pl.ds(off, SC_WINDOW)]])

    return run(x_slabs.reshape(N_TOK, SLAB, LANE), d0, d1)


def _sc_collect(y_slabs, dcat):
    per = N_ASSIGN // SC_WORKERS

    @functools.partial(
        pl.kernel, out_type=jax.ShapeDtypeStruct((N_ASSIGN, SLAB, LANE), F32), mesh=_sc_mesh(), name="moe_collect",
        scratch_types=[pltpu.VMEM((1, per), I32), pltpu.VMEM((SC_WINDOW, SLAB, LANE), F32)])
    def run(y_hbm, i_hbm, o_hbm, i_v, buf):
        base = _sc_worker() * per
        pltpu.sync_copy(i_hbm.at[:, pl.ds(base, per)], i_v)

        @pl.loop(0, per // SC_WINDOW)
        def _(s):
            off = s * SC_WINDOW
            pltpu.sync_copy(y_hbm.at[i_v.at[0, pl.ds(off, SC_WINDOW)]], buf)
            pltpu.sync_copy(buf, o_hbm.at[pl.ds(base + off, SC_WINDOW)])

    return run(y_slabs.reshape(P_SLOTS, SLAB, LANE), dcat)


def _expert_kernel(be_ref, first_ref, nu_ref, x_ref, wg_ref, wu_ref, wd_ref, y_ref, xs, wg_bf, wu_bf, wd_bf):
    i = pl.program_id(0)

    @pl.when((first_ref[i] == 1) & (i < nu_ref[0]))
    def _():
        wg_bf[...] = wg_ref[0, 0].astype(BF16)
        wu_bf[...] = wu_ref[0, 0].astype(BF16)
        wd_bf[...] = wd_ref[0, 0].astype(BF16)

    @pl.when(i < nu_ref[0])
    def _():
        _load_slabs(x_ref, xs, EBLK, BF16)
        xb = xs[...]
        g = jnp.dot(xb, wg_bf[...], preferred_element_type=F32)
        u = jnp.dot(xb, wu_bf[...], preferred_element_type=F32)
        hmid = (g * _sigmoid(g) * u).astype(BF16)
        _store_slabs(y_ref, jnp.dot(hmid, wd_bf[...], preferred_element_type=F32))


def _experts(x_sorted, block_e, first, n_used, w_gate, w_up, w_down, layer):
    wspec = lambda r, c: pl.BlockSpec((1, 1, r, c), lambda i, be, fi, nu: (layer, be[i], 0, 0))
    slab_spec = pl.BlockSpec((EBLK * SLAB, LANE), lambda i, be, fi, nu: (jnp.minimum(i, nu[0] - 1), 0))
    return pl.pallas_call(
        _expert_kernel,
        out_shape=jax.ShapeDtypeStruct((P_SLOTS * SLAB, LANE), F32),
        grid_spec=pltpu.PrefetchScalarGridSpec(
            num_scalar_prefetch=3, grid=(N_EBLK,),
            in_specs=[slab_spec, wspec(D, D_EXPERT), wspec(D, D_EXPERT), wspec(D_EXPERT, D)],
            out_specs=slab_spec,
            scratch_shapes=[
                pltpu.VMEM((EBLK, D), BF16),
                pltpu.VMEM((D, D_EXPERT), BF16), pltpu.VMEM((D, D_EXPERT), BF16), pltpu.VMEM((D_EXPERT, D), BF16),
            ]),
        compiler_params=_cparams(),
        name="moe_experts",
    )(block_e, first, n_used, x_sorted.reshape(P_SLOTS * SLAB, LANE), w_gate, w_up, w_down)


def _combine_kernel(final, mr_ref, e0_ref, e1_ref, wt_ref, y_ref, mod_ref, fg_ref, o_ref, a_scr, b_scr):
    _load_slabs(e0_ref, a_scr, TM, F32)
    _load_slabs(e1_ref, b_scr, TM, F32)
    wt = wt_ref[...]
    moe = wt[:, 0:1] * a_scr[...] + wt[:, 1:2] * b_scr[...]
    y_new = y_ref[...] + mod_ref[0][5:6] * moe
    o_ref[...] = _rms(y_new, fg_ref[...]) if final else y_new


def _combine(ym, wts, y, mods, mrow, final_g, blk0, nblk, final):
    tok = lambda width: pl.BlockSpec((TM, width), lambda j, *_: (blk0 + j, 0))
    slab0 = pl.BlockSpec((TM * SLAB, LANE), lambda j, *_: (blk0 + j, 0))
    slab1 = pl.BlockSpec((TM * SLAB, LANE), lambda j, *_: (NB + blk0 + j, 0))
    mod = pl.BlockSpec((1, 6, D), lambda j, mr: (mr[blk0 + j], 0, 0))
    return pl.pallas_call(
        functools.partial(_combine_kernel, final),
        out_shape=jax.ShapeDtypeStruct((nblk * TM, D), F32),
        grid_spec=pltpu.PrefetchScalarGridSpec(
            num_scalar_prefetch=1, grid=(nblk,),
            in_specs=[slab0, slab1, tok(TOP_K), tok(D), mod, _full_spec((1, D))],
            out_specs=pl.BlockSpec((TM, D), lambda j, *_: (j, 0)),
            scratch_shapes=[pltpu.VMEM((TM, D), F32), pltpu.VMEM((TM, D), F32)]),
        compiler_params=_cparams(),
        name="moe_combine",
    )(mrow, ym, ym, wts, y, mods, final_g)


def kernel(x_prompt, x_sample, cache_attn_k, cache_attn_v, state_mlstm_C, state_mlstm_n, state_mlstm_m, c, c_ctx, ada_w, ada_b, norm1_g, norm2_g, conv_w_in, conv_w_dw, conv_b_dw, conv_ln_g, conv_ln_b, conv_w_out, attn_w_qkv, attn_q_norm, attn_k_norm, attn_w_o, mlstm_w_in, mlstm_b_gate, mlstm_norm_g, mlstm_w_out, moe_w_group, moe_b_group, moe_w_router, moe_b_router, moe_w_gate, moe_w_up, moe_w_down, final_norm_g):
    y = jnp.concatenate([x_prompt.reshape(NP_TOK, D), x_sample.reshape(NS_TOK, D)], axis=0)
    cvec = jnp.concatenate([c_ctx[None, :], c, jnp.zeros((MOD_ROWS - 1 - DEC_BATCH, D), F32)], axis=0)
    mods = _ada_all(cvec, ada_w, ada_b)
    rope = _rope_blocks()
    new_k = new_v = new_c = new_n = new_m = None
    for i in range(DEPTH):
        kind, slot = i % 3, i // 3
        mrow = jnp.asarray(_MOD_ROW + i * MOD_ROWS)
        g1 = norm1_g[i].reshape(1, D)
        if kind == 0:
            u = _conv_in(y, mods, mrow, g1, conv_w_in[slot].astype(BF16))
            w_dw = jnp.concatenate([conv_w_dw[slot], jnp.zeros((1, D), F32)], axis=0)
            y = _conv_main(u, y, mods, mrow, w_dw, conv_b_dw[slot].reshape(1, D), conv_ln_g[slot].reshape(1, D),
                           conv_ln_b[slot].reshape(1, D), conv_w_out[slot].astype(BF16))
        elif kind == 1:
            q, kb, vb, kf, vf = _attn_qkv(y, mods, mrow, g1, attn_w_qkv[slot].astype(BF16),
                                          attn_q_norm[slot].reshape(1, HEAD_DIM), attn_k_norm[slot].reshape(1, HEAD_DIM),
                                          rope)
            new_k = kf[:NP_TOK].reshape(BATCH, 1, SEQ, N_KV_HEADS, HEAD_DIM)
            new_v = vf[:NP_TOK].reshape(BATCH, 1, SEQ, N_KV_HEADS, HEAD_DIM)
            ck = cache_attn_k[:, slot].reshape(DEC_BATCH, PAST_LEN, KV_DIM)
            cv = cache_attn_v[:, slot].reshape(DEC_BATCH, PAST_LEN, KV_DIM)
            y = _attention(q, kb, vb, ck, cv, attn_w_o[slot].astype(BF16), y, mods, i)
        else:
            w_in = mlstm_w_in[slot]
            w_gate = jnp.concatenate([w_in[:, 4 * D:], jnp.zeros((D, LANE - 4 * M_HEADS), F32)], axis=1)
            b_gate = jnp.concatenate([mlstm_b_gate[slot], jnp.zeros((LANE - 4 * M_HEADS,), F32)]).reshape(1, LANE)
            q, k, v, o, gates = _mlstm_in(y, mods, mrow, g1, w_in[:, :4 * D].astype(BF16), _split_hi_lo(w_gate), b_gate)
            sc = state_mlstm_C[:, slot]
            sn = state_mlstm_n[:, slot].reshape(DEC_BATCH, 2, M_HEADS, 1, M_HEAD_DIM)
            sm = state_mlstm_m[:, slot].reshape(DEC_BATCH, 2, M_HEADS, 1, 1)
            hsum, nc_, nn_, nm_ = _mlstm_scan(q, k, v, gates, sc, sn, sm)
            new_c = nc_[:, None]
            new_n = nn_.reshape(BATCH, 1, 2, M_HEADS, M_HEAD_DIM)
            new_m = nm_[..., 0, 0].reshape(BATCH, 1, 2, M_HEADS)
            y = _mlstm_out(hsum, o, mlstm_norm_g[slot].reshape(1, D), mlstm_w_out[slot].astype(BF16), y, mods, mrow)
        w_route = jnp.concatenate([moe_w_group[i], moe_w_router[i],
                                   jnp.zeros((D, LANE - N_GROUPS - N_EXPERTS), F32)], axis=1)
        b_route = jnp.concatenate([moe_b_group[i], moe_b_router[i],
                                   jnp.zeros((LANE - N_GROUPS - N_EXPERTS,), F32)]).reshape(1, LANE)
        x2, eid, ewt = _route(y, mods, mrow, norm2_g[i].reshape(1, D), _split_hi_lo(w_route), b_route)
        d0, d1, block_e, first, n_used = _dispatch_tables(eid)
        x_sorted = _sc_dispatch(x2, d0, d1)
        y_sorted = _experts(x_sorted, block_e, first, n_used, moe_w_gate, moe_w_up, moe_w_down, i)
        ym = _sc_collect(y_sorted, jnp.concatenate([d0, d1], axis=1))
        ym = ym.reshape(N_ASSIGN * SLAB, LANE)
        fg = final_norm_g.reshape(1, D)
        if i + 1 < DEPTH:
            y = _combine(ym, ewt, y, mods, mrow, fg, 0, NB, False)
        else:
            y_prompt = _combine(ym, ewt, y, mods, mrow, fg, 0, NBP, True).reshape(BATCH, SEQ, D)
            y_sample = _combine(ym, ewt, y, mods, mrow, fg, NBP, NB - NBP, True).reshape(DEC_BATCH, DEC_SEQ, D)
    return (y_prompt, y_sample, new_k, new_v, new_c, new_n, new_m)
```

```python
import functools

import jax
import jax.numpy as jnp
import numpy as np
from jax import lax
from jax.experimental import pallas as pl
from jax.experimental.pallas import tpu as pltpu
from jax.experimental.pallas import tpu_sc as plsc

F32 = jnp.float32
BF16 = jnp.bfloat16
I32 = jnp.int32

D = 1024
BATCH, SEQ = 16, 256
DEC_BATCH, DEC_SEQ = 8, 1024
PAST_LEN = 256
DEPTH = 4
GRID_W = 64
EPS = 1e-6
CONV_WIDTH = 31
CONV_PAD = CONV_WIDTH // 2
HEAD_DIM = 128
N_HEADS = 8
N_KV_HEADS = 2
GQA_GROUP = N_HEADS // N_KV_HEADS
Q_DIM = N_HEADS * HEAD_DIM
KV_DIM = N_KV_HEADS * HEAD_DIM
QKV_DIM = Q_DIM + 2 * KV_DIM
ROPE_THETA = 10000.0
M_HEADS = 4
M_HEAD_DIM = D // M_HEADS
M_CHUNK = 64
N_GROUPS = 4
EXPERTS_PER_GROUP = 8
N_EXPERTS = N_GROUPS * EXPERTS_PER_GROUP
TOP_K = 2
D_EXPERT = 512

NP_TOK = BATCH * SEQ
NS_TOK = DEC_BATCH * DEC_SEQ
N_TOK = NP_TOK + NS_TOK
TM = 256
NB = N_TOK // TM
NBP = NP_TOK // TM
BLK_PER_DEC = DEC_SEQ // TM
MOD_ROWS = 16
HALO = 16
LANE = 128
SUBLANE = 8

N_ASSIGN = N_TOK * TOP_K
EBLK = 256
N_EBLK = N_ASSIGN // EBLK + N_EXPERTS
P_SLOTS = N_EBLK * EBLK
N_PAD_SLOTS = P_SLOTS - N_ASSIGN

VMEM_LIMIT = 56 * 1024 * 1024


def _block_tables():
    j = np.arange(NB)
    is_p = j < NBP
    mod_row = np.where(is_p, 0, 1 + (j - NBP) // BLK_PER_DEC)
    rope_idx = np.where(is_p, 0, 1 + (j - NBP) % BLK_PER_DEC)
    first = np.where(is_p, 1, ((j - NBP) % BLK_PER_DEC == 0).astype(np.int64))
    last = np.where(is_p, 1, ((j - NBP) % BLK_PER_DEC == BLK_PER_DEC - 1).astype(np.int64))
    return (mod_row.astype(np.int32), rope_idx.astype(np.int32), first.astype(np.int32), last.astype(np.int32))


_MOD_ROW, _ROPE_IDX, _SEQ_FIRST, _SEQ_LAST = _block_tables()


def _cparams(n_axes=1):
    return pltpu.CompilerParams(dimension_semantics=("arbitrary",) * n_axes, vmem_limit_bytes=VMEM_LIMIT)


def _sigmoid(x):
    return 1.0 / (1.0 + jnp.exp(-x))


def _rms(x, g):
    return x * lax.rsqrt(jnp.mean(x * x, axis=-1, keepdims=True) + EPS) * g


def _split_hi_lo(w):
    hi = w.astype(BF16)
    return jnp.stack([hi, (w - hi.astype(F32)).astype(BF16)], axis=0)


def _dot_hi_lo(x, w_ref):
    xh = x.astype(BF16)
    xl = (x - xh.astype(F32)).astype(BF16)
    wh, wl = w_ref[0], w_ref[1]
    return (jnp.dot(xh, wh, preferred_element_type=F32)
            + (jnp.dot(xh, wl, preferred_element_type=F32) + jnp.dot(xl, wh, preferred_element_type=F32)))


def _norm_mod(y, g, mod, which):
    shift = mod[3 * which:3 * which + 1]
    scale = mod[3 * which + 1:3 * which + 2]
    return _rms(y, g) * (1.0 + scale) + shift


def _ada_kernel(c_ref, w_ref, b_ref, o_ref):
    c = c_ref[...]
    s = c * _sigmoid(c)
    o_ref[0] = jnp.dot(s.astype(BF16), w_ref[0].astype(BF16), preferred_element_type=F32) + b_ref[0]


def _ada_all(cvec, ada_w, ada_b):
    tn = 1536
    out = pl.pallas_call(
        _ada_kernel,
        out_shape=jax.ShapeDtypeStruct((DEPTH, MOD_ROWS, 6 * D), F32),
        grid=(DEPTH, 6 * D // tn),
        in_specs=[
            pl.BlockSpec((MOD_ROWS, D), lambda l, n: (0, 0)),
            pl.BlockSpec((1, D, tn), lambda l, n: (l, 0, n)),
            pl.BlockSpec((1, 1, tn), lambda l, n: (l, 0, n)),
        ],
        out_specs=pl.BlockSpec((1, MOD_ROWS, tn), lambda l, n: (l, 0, n)),
        compiler_params=_cparams(2),
        name="ada_mod",
    )(cvec, ada_w, ada_b.reshape(DEPTH, 1, 6 * D))
    return out.reshape(DEPTH * MOD_ROWS, 6, D)


def _tok_spec(width):
    return pl.BlockSpec((TM, width), lambda j, *_: (j, 0))


def _mod_spec():
    return pl.BlockSpec((1, 6, D), lambda j, mr, *_: (mr[j], 0, 0))


def _full_spec(shape):
    nd = len(shape)
    return pl.BlockSpec(shape, lambda j, *_: (0,) * nd)


def _conv_in_kernel(mr_ref, y_ref, mod_ref, g_ref, w_ref, u_ref):
    h = _norm_mod(y_ref[...], g_ref[...], mod_ref[0], 0)
    ag = jnp.dot(h.astype(BF16), w_ref[...], preferred_element_type=F32)
    u_ref[...] = ag[:, :D] * _sigmoid(ag[:, D:])


def _conv_in(y, mods, mrow, g1, w_in):
    return pl.pallas_call(
        _conv_in_kernel,
        out_shape=jax.ShapeDtypeStruct((N_TOK, D), F32),
        grid_spec=pltpu.PrefetchScalarGridSpec(
            num_scalar_prefetch=1, grid=(NB,),
            in_specs=[_tok_spec(D), _mod_spec(), _full_spec((1, D)), _full_spec((D, 2 * D))],
            out_specs=_tok_spec(D)),
        compiler_params=_cparams(),
        name="conv_in",
    )(mrow, y, mods, g1, w_in)


def _conv_main_kernel(mr_ref, first_ref, last_ref, u_ref, up_ref, un_ref, wdw_ref, bdw_ref, lg_ref, lb_ref,
                      wout_ref, y_ref, mod_ref, o_ref, ext_ref, acc_ref):
    j = pl.program_id(0)
    zero = jnp.zeros((HALO, D), F32)
    ext_ref[0:HALO, :] = jnp.where(first_ref[j] == 1, zero, up_ref[...])
    ext_ref[HALO:HALO + TM, :] = u_ref[...]
    ext_ref[HALO + TM:2 * HALO + TM, :] = jnp.where(last_ref[j] == 1, zero, un_ref[...])

    off0 = HALO - CONV_PAD
    n_a = (off0 + CONV_WIDTH - 1) // SUBLANE + 1
    n_chunks = TM // SUBLANE

    def strip(ci, carry):
        cs = pl.ds(pl.multiple_of(ci * LANE, LANE), LANE)
        wk = [jnp.broadcast_to(wdw_ref[k:k + 1, cs], (SUBLANE, LANE)) for k in range(CONV_WIDTH)]
        bias = jnp.broadcast_to(bdw_ref[:, cs], (SUBLANE, LANE))
        sub = lax.broadcasted_iota(I32, (SUBLANE, LANE), 0)
        prev_rot, prev_v0 = None, None
        for j in range(n_chunks + 1):
            tiles = [ext_ref[SUBLANE * (j + a):SUBLANE * (j + a + 1), cs] for a in range(n_a)]
            part = []
            for s in range(SUBLANE):
                acc = None
                for a in range(n_a):
                    k = SUBLANE * a + s - off0
                    if (0 <= k < CONV_WIDTH) and not (s == 0 and j == n_chunks):
                        term = tiles[a] * wk[k]
                        acc = term if acc is None else acc + term
                part.append(acc)
            rot = [None] + [pltpu.roll(part[s], SUBLANE - s, 0) for s in range(1, SUBLANE)]
            if j >= 1:
                out = prev_v0 + bias
                for s in range(1, SUBLANE):
                    out = out + jnp.where(sub < SUBLANE - s, prev_rot[s], rot[s])
                acc_ref[SUBLANE * (j - 1):SUBLANE * j, cs] = out
            prev_rot, prev_v0 = rot, part[0]
        return carry

    lax.fori_loop(0, D // LANE, strip, 0)

    c = acc_ref[...]
    mu = jnp.mean(c, axis=-1, keepdims=True)
    cc = c - mu
    var = jnp.mean(cc * cc, axis=-1, keepdims=True)
    z = cc * lax.rsqrt(var + EPS) * lg_ref[...] + lb_ref[...]
    z = z * _sigmoid(z)
    out = jnp.dot(z.astype(BF16), wout_ref[...], preferred_element_type=F32)
    o_ref[...] = y_ref[...] + mod_ref[0][2:3] * out


def _conv_main(u, y, mods, mrow, w_dw, b_dw, ln_g, ln_b, w_out):
    nh = N_TOK // HALO
    per = TM // HALO
    return pl.pallas_call(
        _conv_main_kernel,
        out_shape=jax.ShapeDtypeStruct((N_TOK, D), F32),
        grid_spec=pltpu.PrefetchScalarGridSpec(
            num_scalar_prefetch=3, grid=(NB,),
            in_specs=[
                _tok_spec(D),
                pl.BlockSpec((HALO, D), lambda j, *_: (jnp.maximum(j * per - 1, 0), 0)),
                pl.BlockSpec((HALO, D), lambda j, *_: (jnp.minimum((j + 1) * per, nh - 1), 0)),
                _full_spec((CONV_WIDTH + 1, D)), _full_spec((1, D)), _full_spec((1, D)), _full_spec((1, D)),
                _full_spec((D, D)), _tok_spec(D), _mod_spec(),
            ],
            out_specs=_tok_spec(D),
            scratch_shapes=[pltpu.VMEM((TM + 2 * HALO, D), F32), pltpu.VMEM((TM, D), F32)]),
        compiler_params=_cparams(),
        name="conv_main",
    )(mrow, jnp.asarray(_SEQ_FIRST), jnp.asarray(_SEQ_LAST), u, u, u, w_dw, b_dw, ln_g, ln_b, w_out, y, mods)


def _rope_angles():
    rows = DEC_SEQ // GRID_W
    row = jnp.repeat(jnp.arange(rows, dtype=F32), GRID_W)
    col = jnp.tile(jnp.arange(GRID_W, dtype=F32), rows)
    axis_dim = HEAD_DIM // 2
    freqs = jnp.power(ROPE_THETA, -jnp.arange(axis_dim // 2, dtype=F32) * 2.0 / axis_dim)
    ang_r = row[:, None] * freqs[None, :]
    ang_c = col[:, None] * freqs[None, :]
    return jnp.concatenate([ang_r, ang_r, ang_c, ang_c], axis=-1)


def _rope_blocks():
    ang = _rope_angles()
    cos, sin = jnp.cos(ang), jnp.sin(ang)
    lane = np.arange(HEAD_DIM)
    lo = jnp.asarray(((lane % (HEAD_DIM // 2)) < HEAD_DIM // 4).astype(np.float32))
    sin_a = -sin * lo[None, :]
    sin_b = sin * (1.0 - lo)[None, :]
    nblk = DEC_SEQ // TM
    ident = jnp.ones((1, TM, HEAD_DIM), F32)
    zeros = jnp.zeros((1, TM, HEAD_DIM), F32)
    cos_t = jnp.concatenate([ident, cos.reshape(nblk, TM, HEAD_DIM)], axis=0)
    sa_t = jnp.concatenate([zeros, sin_a.reshape(nblk, TM, HEAD_DIM)], axis=0)
    sb_t = jnp.concatenate([zeros, sin_b.reshape(nblk, TM, HEAD_DIM)], axis=0)
    return cos_t, sa_t, sb_t


def _attn_qkv_kernel(mr_ref, ri_ref, y_ref, mod_ref, g_ref, w_ref, qg_ref, kg_ref, cos_ref, sa_ref, sb_ref,
                     q_ref, kb_ref, vb_ref, kf_ref, vf_ref):
    h = _norm_mod(y_ref[...], g_ref[...], mod_ref[0], 0)
    qkv = jnp.dot(h.astype(BF16), w_ref[...], preferred_element_type=F32)
    cos, sa, sb = cos_ref[0], sa_ref[0], sb_ref[0]
    quarter = HEAD_DIM // 4

    def head(x, g):
        xn = _rms(x, g)
        return xn * cos + pltpu.roll(xn, HEAD_DIM - quarter, 1) * sa + pltpu.roll(xn, quarter, 1) * sb

    scale = HEAD_DIM ** -0.5
    for hd in range(N_HEADS):
        sl = slice(hd * HEAD_DIM, (hd + 1) * HEAD_DIM)
        q_ref[:, sl] = (head(qkv[:, sl], qg_ref[...]) * scale).astype(BF16)
    for kv in range(N_KV_HEADS):
        sl = slice(kv * HEAD_DIM, (kv + 1) * HEAD_DIM)
        kr = head(qkv[:, Q_DIM + kv * HEAD_DIM:Q_DIM + (kv + 1) * HEAD_DIM], kg_ref[...])
        kf_ref[:, sl] = kr
        kb_ref[:, sl] = kr.astype(BF16)
    v = qkv[:, Q_DIM + KV_DIM:]
    vf_ref[...] = v
    vb_ref[...] = v.astype(BF16)


def _attn_qkv(y, mods, mrow, g1, w_qkv, q_g, k_g, rope):
    cos_t, sa_t, sb_t = rope
    rspec = pl.BlockSpec((1, TM, HEAD_DIM), lambda j, mr, ri: (ri[j], 0, 0))
    return pl.pallas_call(
        _attn_qkv_kernel,
        out_shape=(jax.ShapeDtypeStruct((N_TOK, Q_DIM), BF16), jax.ShapeDtypeStruct((N_TOK, KV_DIM), BF16),
                   jax.ShapeDtypeStruct((N_TOK, KV_DIM), BF16), jax.ShapeDtypeStruct((N_TOK, KV_DIM), F32),
                   jax.ShapeDtypeStruct((N_TOK, KV_DIM), F32)),
        grid_spec=pltpu.PrefetchScalarGridSpec(
            num_scalar_prefetch=2, grid=(NB,),
            in_specs=[_tok_spec(D), _mod_spec(), _full_spec((1, D)), _full_spec((D, QKV_DIM)),
                      _full_spec((1, HEAD_DIM)), _full_spec((1, HEAD_DIM)), rspec, rspec, rspec],
            out_specs=(_tok_spec(Q_DIM), _tok_spec(KV_DIM), _tok_spec(KV_DIM), _tok_spec(KV_DIM), _tok_spec(KV_DIM))),
        compiler_params=_cparams(),
        name="attn_qkv",
    )(mrow, jnp.asarray(_ROPE_IDX), y, mods, g1, w_qkv, q_g, k_g, cos_t, sa_t, sb_t)


def _attn_heads(q, ks, vs, o_scr):
    nt = (((1,), (1,)), ((), ()))
    for hd in range(N_HEADS):
        g = hd // GQA_GROUP
        qh = q[:, hd * HEAD_DIM:(hd + 1) * HEAD_DIM]
        gs = slice(g * HEAD_DIM, (g + 1) * HEAD_DIM)
        ss = [lax.dot_general(qh, k[:, gs], nt, preferred_element_type=F32) for k in ks]
        m = functools.reduce(jnp.maximum, [jnp.max(s, axis=-1, keepdims=True) for s in ss])
        ps = [jnp.exp(s - m) for s in ss]
        l = functools.reduce(lambda a, b: a + b, [jnp.sum(p, axis=-1, keepdims=True) for p in ps])
        o = functools.reduce(lambda a, b: a + b,
                             [jnp.dot(p.astype(BF16), v[:, gs], preferred_element_type=F32) for p, v in zip(ps, vs)])
        o_scr[:, hd * HEAD_DIM:(hd + 1) * HEAD_DIM] = (o / l).astype(BF16)


def _attn_ctx_kernel(q_ref, k_ref, v_ref, wo_ref, y_ref, mod_ref, o_ref, o_scr):
    _attn_heads(q_ref[...], [k_ref[...]], [v_ref[...]], o_scr)
    out = jnp.dot(o_scr[...], wo_ref[...], preferred_element_type=F32)
    o_ref[...] = y_ref[...] + mod_ref[0][2:3] * out


def _attn_lat_kernel(q_ref, k_ref, v_ref, ck_ref, cv_ref, wo_ref, y_ref, mod_ref, ctx_out_ref, o_ref, o_scr):
    del ctx_out_ref
    _attn_heads(q_ref[...], [k_ref[...], ck_ref[0].astype(BF16)], [v_ref[...], cv_ref[0].astype(BF16)], o_scr)
    out = jnp.dot(o_scr[...], wo_ref[...], preferred_element_type=F32)
    o_ref[...] = y_ref[...] + mod_ref[0][2:3] * out


def _attention(q, kb, vb, cache_k, cache_v, w_o, y, mods, layer):
    y_ctx = pl.pallas_call(
        _attn_ctx_kernel,
        out_shape=jax.ShapeDtypeStruct((N_TOK, D), F32),
        grid=(BATCH,),
        in_specs=[
            pl.BlockSpec((SEQ, Q_DIM), lambda s: (s, 0)),
            pl.BlockSpec((SEQ, KV_DIM), lambda s: (s, 0)),
            pl.BlockSpec((SEQ, KV_DIM), lambda s: (s, 0)),
            pl.BlockSpec((Q_DIM, D), lambda s: (0, 0)),
            pl.BlockSpec((SEQ, D), lambda s: (s, 0)),
            pl.BlockSpec((1, 6, D), lambda s: (layer * MOD_ROWS, 0, 0)),
        ],
        out_specs=pl.BlockSpec((SEQ, D), lambda s: (s, 0)),
        scratch_shapes=[pltpu.VMEM((SEQ, Q_DIM), BF16)],
        compiler_params=_cparams(),
        name="attn_ctx",
    )(q, kb, vb, w_o, y, mods)
    pb = NP_TOK // DEC_SEQ
    return pl.pallas_call(
        _attn_lat_kernel,
        out_shape=jax.ShapeDtypeStruct((N_TOK, D), F32),
        input_output_aliases={8: 0},
        grid=(DEC_BATCH, BLK_PER_DEC),
        in_specs=[
            pl.BlockSpec((TM, Q_DIM), lambda b, t: (NBP + b * BLK_PER_DEC + t, 0)),
            pl.BlockSpec((DEC_SEQ, KV_DIM), lambda b, t: (pb + b, 0)),
            pl.BlockSpec((DEC_SEQ, KV_DIM), lambda b, t: (pb + b, 0)),
            pl.BlockSpec((1, PAST_LEN, KV_DIM), lambda b, t: (b, 0, 0)),
            pl.BlockSpec((1, PAST_LEN, KV_DIM), lambda b, t: (b, 0, 0)),
            pl.BlockSpec((Q_DIM, D), lambda b, t: (0, 0)),
            pl.BlockSpec((TM, D), lambda b, t: (NBP + b * BLK_PER_DEC + t, 0)),
            pl.BlockSpec((1, 6, D), lambda b, t: (layer * MOD_ROWS + 1 + b, 0, 0)),
            pl.BlockSpec(memory_space=pl.ANY),
        ],
        out_specs=pl.BlockSpec((TM, D), lambda b, t: (NBP + b * BLK_PER_DEC + t, 0)),
        scratch_shapes=[pltpu.VMEM((TM, Q_DIM), BF16)],
        compiler_params=_cparams(2),
        name="attn_lat",
    )(q, kb, vb, cache_k, cache_v, w_o, y, mods, y_ctx)


def _log_sigmoid(x):
    return jnp.minimum(x, 0.0) - jnp.log(1.0 + jnp.exp(-jnp.abs(x)))


def _mlstm_in_kernel(mr_ref, y_ref, mod_ref, g_ref, w_ref, wg_ref, bg_ref, q_ref, k_ref, v_ref, o_ref, gt_ref):
    h = _norm_mod(y_ref[...], g_ref[...], mod_ref[0], 0)
    hb = h.astype(BF16)
    q_ref[...] = jnp.dot(hb, w_ref[:, 0:D], preferred_element_type=F32).astype(BF16)
    k_ref[...] = (jnp.dot(hb, w_ref[:, D:2 * D], preferred_element_type=F32) * (M_HEAD_DIM ** -0.5)).astype(BF16)
    v_ref[...] = jnp.dot(hb, w_ref[:, 2 * D:3 * D], preferred_element_type=F32).astype(BF16)
    o_ref[...] = _sigmoid(jnp.dot(hb, w_ref[:, 3 * D:4 * D], preferred_element_type=F32))
    gt = _dot_hi_lo(h, wg_ref) + bg_ref[...]
    lane = lax.broadcasted_iota(I32, gt.shape, 1)
    is_f = ((lane >= M_HEADS) & (lane < 2 * M_HEADS)) | ((lane >= 3 * M_HEADS) & (lane < 4 * M_HEADS))
    gt_ref[...] = jnp.where(is_f, _log_sigmoid(gt), gt)


def _mlstm_in(y, mods, mrow, g1, w_main, w_gate, b_gate):
    return pl.pallas_call(
        _mlstm_in_kernel,
        out_shape=(jax.ShapeDtypeStruct((N_TOK, D), BF16), jax.ShapeDtypeStruct((N_TOK, D), BF16),
                   jax.ShapeDtypeStruct((N_TOK, D), BF16), jax.ShapeDtypeStruct((N_TOK, D), F32),
                   jax.ShapeDtypeStruct((N_TOK, LANE), F32)),
        grid_spec=pltpu.PrefetchScalarGridSpec(
            num_scalar_prefetch=1, grid=(NB,),
            in_specs=[_tok_spec(D), _mod_spec(), _full_spec((1, D)), _full_spec((D, 4 * D)),
                      _full_spec((2, D, LANE)), _full_spec((1, LANE))],
            out_specs=(_tok_spec(D), _tok_spec(D), _tok_spec(D), _tok_spec(D), _tok_spec(LANE))),
        compiler_params=_cparams(),
        name="mlstm_in",
    )(mrow, y, mods, g1, w_main, w_gate, b_gate)


def _mlstm_chunk(d, hd, c, m, q_ref, k_ref, v_ref, gc_ref, gr_ref, c_scr, n_scr):
    L = M_CHUNK
    r0 = pl.multiple_of(c * L, L)
    hs = slice(hd * M_HEAD_DIM, (hd + 1) * M_HEAD_DIM)
    q = q_ref[pl.ds(r0, L), hs]
    k = k_ref[pl.ds(r0, L), hs]
    v = v_ref[pl.ds(r0, L), hs]
    col = gc_ref[hd, pl.ds(r0, L), :]
    row = gr_ref[hd, c]
    i_col, lf_col = col[:, 2 * d:2 * d + 1], col[:, 2 * d + 1:2 * d + 2]
    i_row, lf_row = row[2 * d:2 * d + 1, :], row[2 * d + 1:2 * d + 2, :]
    t_idx = lax.broadcasted_iota(I32, (L, L), 0)
    s_idx = lax.broadcasted_iota(I32, (L, L), 1)
    if d == 0:
        mask, mask_t = s_idx <= t_idx, t_idx <= s_idx
    else:
        mask, mask_t = s_idx >= t_idx, t_idx >= s_idx
    b_col = jnp.sum(jnp.where(mask, lf_row, 0.0), axis=1, keepdims=True)
    b_row = jnp.sum(jnp.where(mask_t, lf_col, 0.0), axis=0, keepdims=True)
    log_d = jnp.where(mask, b_col - b_row + i_row, -jnp.inf)
    li = b_col + m
    m_r = jnp.maximum(li, jnp.max(log_d, axis=1, keepdims=True))
    a_int = jnp.exp(li - m_r)
    s = lax.dot_general(q, k, (((1,), (1,)), ((), ())), preferred_element_type=F32) * jnp.exp(log_d - m_r)
    cmat = c_scr[d, hd]
    num = a_int * jnp.dot(q, cmat.astype(BF16), preferred_element_type=F32) \
        + jnp.dot(s.astype(BF16), v, preferred_element_type=F32)
    qn = jnp.sum(q.astype(F32) * n_scr[d, hd], axis=1, keepdims=True)
    den = a_int * qn + jnp.sum(s, axis=1, keepdims=True)
    hh = num / jnp.maximum(jnp.abs(den), jnp.exp(-m_r))
    b_last = b_row[:, L - 1:L] if d == 0 else b_row[:, 0:1]
    log_w = b_last - b_col + i_col
    m_new = jnp.maximum(b_last + m, jnp.max(log_w, axis=0, keepdims=True))
    w = jnp.exp(log_w - m_new)
    decay = jnp.exp(b_last + m - m_new)
    kw = k.astype(F32) * w
    c_scr[d, hd] = decay * cmat + lax.dot_general(kw.astype(BF16), v, (((0,), (0,)), ((), ())),
                                                  preferred_element_type=F32)
    n_scr[d, hd] = decay * n_scr[d, hd] + jnp.sum(kw, axis=0, keepdims=True)
    return r0, hs, hh, m_new


def _mlstm_scan_body(n_chunks, q_ref, k_ref, v_ref, gc_ref, gr_ref, h_ref, hb_scr, c_scr, n_scr, m0):
    def body(c, ms):
        out = []
        for hd in range(M_HEADS):
            r0, hs, hf, mf = _mlstm_chunk(0, hd, c, ms[2 * hd], q_ref, k_ref, v_ref, gc_ref, gr_ref, c_scr, n_scr)
            h_ref[pl.ds(r0, M_CHUNK), hs] = hf
            r1, hs, hb, mb = _mlstm_chunk(1, hd, n_chunks - 1 - c, ms[2 * hd + 1], q_ref, k_ref, v_ref, gc_ref,
                                          gr_ref, c_scr, n_scr)
            hb_scr[pl.ds(r1, M_CHUNK), hs] = hb
            out += [mf, mb]
        return tuple(out)

    ms = lax.fori_loop(0, n_chunks, body, tuple(m0))
    h_ref[...] += hb_scr[...]
    return ms


def _mlstm_scan_ctx_kernel(q_ref, k_ref, v_ref, gc_ref, gr_ref, h_ref, cn_ref, nn_ref, mn_ref, hb_scr, c_scr, n_scr):
    c_scr[...] = jnp.zeros(c_scr.shape, F32)
    n_scr[...] = jnp.zeros(n_scr.shape, F32)
    zero = jnp.zeros((1, 1), F32)
    ms = _mlstm_scan_body(SEQ // M_CHUNK, q_ref, k_ref, v_ref, gc_ref, gr_ref, h_ref, hb_scr, c_scr, n_scr,
                          [zero] * (2 * M_HEADS))
    cn_ref[0] = c_scr[...]
    nn_ref[0] = n_scr[...]
    for hd in range(M_HEADS):
        for d in range(2):
            mn_ref[0, d, hd] = jnp.broadcast_to(ms[2 * hd + d], (1, LANE))


def _mlstm_scan_lat_kernel(q_ref, k_ref, v_ref, gc_ref, gr_ref, c0_ref, n0_ref, m0_ref, ctx_out_ref, h_ref,
                           hb_scr, c_scr, n_scr):
    del ctx_out_ref
    c_scr[...] = c0_ref[0]
    n_scr[...] = n0_ref[0]
    m0 = [m0_ref[0, d, hd] for hd in range(M_HEADS) for d in range(2)]
    _mlstm_scan_body(DEC_SEQ // M_CHUNK, q_ref, k_ref, v_ref, gc_ref, gr_ref, h_ref, hb_scr, c_scr, n_scr, m0)


def _mlstm_scan(q, k, v, gates, state_c, state_n, state_m):
    g16 = gates[:, :4 * M_HEADS].reshape(N_TOK, 4, M_HEADS)
    gcol = jnp.transpose(g16, (2, 0, 1))
    grow = jnp.transpose(g16.reshape(N_TOK // M_CHUNK, M_CHUNK, 4, M_HEADS), (3, 0, 2, 1))
    hd = M_HEAD_DIM
    state_scratch = [pltpu.VMEM((2, M_HEADS, hd, hd), F32), pltpu.VMEM((2, M_HEADS, 1, hd), F32)]
    ncp = SEQ // M_CHUNK
    h_ctx, new_c, new_n, new_m = pl.pallas_call(
        _mlstm_scan_ctx_kernel,
        out_shape=(jax.ShapeDtypeStruct((N_TOK, D), F32),
                   jax.ShapeDtypeStruct((BATCH, 2, M_HEADS, hd, hd), F32),
                   jax.ShapeDtypeStruct((BATCH, 2, M_HEADS, 1, hd), F32),
                   jax.ShapeDtypeStruct((BATCH, 2, M_HEADS, 1, LANE), F32)),
        grid=(BATCH,),
        in_specs=[
            pl.BlockSpec((SEQ, D), lambda s: (s, 0)),
            pl.BlockSpec((SEQ, D), lambda s: (s, 0)),
            pl.BlockSpec((SEQ, D), lambda s: (s, 0)),
            pl.BlockSpec((M_HEADS, SEQ, 4), lambda s: (0, s, 0)),
            pl.BlockSpec((M_HEADS, ncp, 4, M_CHUNK), lambda s: (0, s, 0, 0)),
        ],
        out_specs=(
            pl.BlockSpec((SEQ, D), lambda s: (s, 0)),
            pl.BlockSpec((1, 2, M_HEADS, hd, hd), lambda s: (s, 0, 0, 0, 0)),
            pl.BlockSpec((1, 2, M_HEADS, 1, hd), lambda s: (s, 0, 0, 0, 0)),
            pl.BlockSpec((1, 2, M_HEADS, 1, LANE), lambda s: (s, 0, 0, 0, 0)),
        ),
        scratch_shapes=[pltpu.VMEM((SEQ, D), F32)] + state_scratch,
        compiler_params=_cparams(),
        name="mlstm_scan_ctx",
    )(q, k, v, gcol, grow)
    ncl = DEC_SEQ // M_CHUNK
    pb = NP_TOK // DEC_SEQ
    h_all = pl.pallas_call(
        _mlstm_scan_lat_kernel,
        out_shape=jax.ShapeDtypeStruct((N_TOK, D), F32),
        input_output_aliases={8: 0},
        grid=(DEC_BATCH,),
        in_specs=[
            pl.BlockSpec((DEC_SEQ, D), lambda b: (pb + b, 0)),
            pl.BlockSpec((DEC_SEQ, D), lambda b: (pb + b, 0)),
            pl.BlockSpec((DEC_SEQ, D), lambda b: (pb + b, 0)),
            pl.BlockSpec((M_HEADS, DEC_SEQ, 4), lambda b: (0, pb + b, 0)),
            pl.BlockSpec((M_HEADS, ncl, 4, M_CHUNK), lambda b: (0, pb + b, 0, 0)),
            pl.BlockSpec((1, 2, M_HEADS, hd, hd), lambda b: (b, 0, 0, 0, 0)),
            pl.BlockSpec((1, 2, M_HEADS, 1, hd), lambda b: (b, 0, 0, 0, 0)),
            pl.BlockSpec((1, 2, M_HEADS, 1, 1), lambda b: (b, 0, 0, 0, 0)),
            pl.BlockSpec(memory_space=pl.ANY),
        ],
        out_specs=pl.BlockSpec((DEC_SEQ, D), lambda b: (pb + b, 0)),
        scratch_shapes=[pltpu.VMEM((DEC_SEQ, D), F32)] + state_scratch,
        compiler_params=_cparams(),
        name="mlstm_scan_lat",
    )(q, k, v, gcol, grow, state_c, state_n, state_m, h_ctx)
    return h_all, new_c, new_n, new_m


def _mlstm_out_kernel(mr_ref, h_ref, o_ref, ng_ref, w_ref, y_ref, mod_ref, out_ref, x_scr):
    hc = o_ref[...] * h_ref[...]
    for hd in range(M_HEADS):
        sl = slice(hd * M_HEAD_DIM, (hd + 1) * M_HEAD_DIM)
        x_scr[:, sl] = _rms(hc[:, sl], ng_ref[:, sl]).astype(BF16)
    out = jnp.dot(x_scr[...], w_ref[...], preferred_element_type=F32)
    out_ref[...] = y_ref[...] + mod_ref[0][2:3] * out


def _mlstm_out(hsum, o, norm_g, w_out, y, mods, mrow):
    return pl.pallas_call(
        _mlstm_out_kernel,
        out_shape=jax.ShapeDtypeStruct((N_TOK, D), F32),
        grid_spec=pltpu.PrefetchScalarGridSpec(
            num_scalar_prefetch=1, grid=(NB,),
            in_specs=[_tok_spec(D), _tok_spec(D), _full_spec((1, D)), _full_spec((D, D)), _tok_spec(D), _mod_spec()],
            out_specs=_tok_spec(D),
            scratch_shapes=[pltpu.VMEM((TM, D), BF16)]),
        compiler_params=_cparams(),
        name="mlstm_out",
    )(mrow, hsum, o, norm_g, w_out, y, mods)


ROUTE_OFF = N_GROUPS
SLAB = D // LANE
V7X_SC_CORES = 2
V7X_SC_SUBCORES = 16
SC_WORKERS = V7X_SC_CORES * V7X_SC_SUBCORES
SC_WINDOW = 64


def _store_slabs(ref, x):
    rows = x.shape[0]
    for c in range(SLAB):
        ref[pl.ds(c, rows, stride=SLAB), :] = x[:, c * LANE:(c + 1) * LANE]


def _load_slabs(ref, dst, rows, dtype):
    for c in range(SLAB):
        dst[:, c * LANE:(c + 1) * LANE] = ref[pl.ds(c, rows, stride=SLAB), :].astype(dtype)


def _route_kernel(mr_ref, y_ref, mod_ref, g_ref, wr_ref, br_ref, x_ref, id_ref, wt_ref):
    x = _norm_mod(y_ref[...], g_ref[...], mod_ref[0], 1)
    _store_slabs(x_ref, x)
    lg = _dot_hi_lo(x, wr_ref) + br_ref[...]
    lane = lax.broadcasted_iota(I32, lg.shape, 1)
    ninf = -jnp.inf
    big = jnp.int32(LANE)
    lgg = jnp.where(lane < N_GROUPS, lg, ninf)
    gmax = jnp.max(lgg, axis=-1, keepdims=True)
    g_idx = jnp.min(jnp.where(lgg == gmax, lane, big), axis=-1, keepdims=True)
    g_w = 1.0 / jnp.sum(jnp.exp(lgg - gmax), axis=-1, keepdims=True)
    lo = ROUTE_OFF + g_idx * EXPERTS_PER_GROUP
    le = jnp.where((lane >= lo) & (lane < lo + EXPERTS_PER_GROUP), lg, ninf)
    m1 = jnp.max(le, axis=-1, keepdims=True)
    i1 = jnp.min(jnp.where(le == m1, lane, big), axis=-1, keepdims=True)
    le2 = jnp.where(lane == i1, ninf, le)
    m2 = jnp.max(le2, axis=-1, keepdims=True)
    i2 = jnp.min(jnp.where(le2 == m2, lane, big), axis=-1, keepdims=True)
    r = jnp.exp(m2 - m1)
    p1 = 1.0 / (1.0 + r)
    p2 = r / (1.0 + r)
    two = lax.broadcasted_iota(I32, (x.shape[0], TOP_K), 1)
    id_ref[...] = jnp.where(two == 0, i1 - ROUTE_OFF, i2 - ROUTE_OFF)
    wt_ref[...] = jnp.where(two == 0, g_w * p1, g_w * p2)


def _route(y, mods, mrow, g2, w_route, b_route):
    return pl.pallas_call(
        _route_kernel,
        out_shape=(jax.ShapeDtypeStruct((N_TOK * SLAB, LANE), F32), jax.ShapeDtypeStruct((N_TOK, TOP_K), I32),
                   jax.ShapeDtypeStruct((N_TOK, TOP_K), F32)),
        grid_spec=pltpu.PrefetchScalarGridSpec(
            num_scalar_prefetch=1, grid=(NB,),
            in_specs=[_tok_spec(D), _mod_spec(), _full_spec((1, D)), _full_spec((2, D, LANE)), _full_spec((1, LANE))],
            out_specs=(pl.BlockSpec((TM * SLAB, LANE), lambda j, *_: (j, 0)), _tok_spec(TOP_K), _tok_spec(TOP_K))),
        compiler_params=_cparams(),
        name="moe_route",
    )(mrow, y, mods, g2, w_route, b_route)


def _dispatch_tables(expert_id):
    flat_e = expert_id.reshape(-1)
    onehot = (flat_e[:, None] == jnp.arange(N_EXPERTS, dtype=I32)[None, :]).astype(I32)
    csum = jnp.cumsum(onehot, axis=0)
    counts = csum[-1]
    padded = ((counts + EBLK - 1) // EBLK) * EBLK
    pad_end = jnp.cumsum(padded)
    pad_start = pad_end - padded
    dest = jnp.sum((csum - 1 + pad_start[None, :]) * onehot, axis=1).astype(I32)
    blk = jnp.arange(N_EBLK, dtype=I32) * EBLK
    n_used = (pad_end[-1] // EBLK).astype(I32)
    last_e = jnp.sum((pad_end[-1] - 1 >= pad_end).astype(I32))
    block_e = jnp.minimum(jnp.sum((blk[:, None] >= pad_end[None, :]).astype(I32), axis=1), last_e).astype(I32)
    first = jnp.concatenate([jnp.ones((1,), I32), (block_e[1:] != block_e[:-1]).astype(I32)])
    dest2 = dest.reshape(N_TOK, TOP_K)
    return dest2[:, 0].reshape(1, N_TOK), dest2[:, 1].reshape(1, N_TOK), block_e, first, n_used.reshape(1)


def _sc_mesh():
    return plsc.VectorSubcoreMesh(core_axis_name="core", subcore_axis_name="subcore",
                                  num_cores=V7X_SC_CORES, num_subcores=V7X_SC_SUBCORES)


def _sc_worker():
    return lax.axis_index("core") * V7X_SC_SUBCORES + lax.axis_index("subcore")


def _sc_dispatch(x_slabs, d0, d1):
    per = N_TOK // SC_WORKERS

    @functools.partial(
        pl.kernel, out_type=jax.ShapeDtypeStruct((P_SLOTS, SLAB, LANE), F32), mesh=_sc_mesh(), name="moe_dispatch",
        scratch_types=[pltpu.VMEM((1, per), I32), pltpu.VMEM((1, per), I32), pltpu.VMEM((SC_WINDOW, SLAB, LANE), F32)])
    def run(x_hbm, d0_hbm, d1_hbm, o_hbm, i0_v, i1_v, buf):
        base = _sc_worker() * per
        pltpu.sync_copy(d0_hbm.at[:, pl.ds(base, per)], i0_v)
        pltpu.sync_copy(d1_hbm.at[:, pl.ds(base, per)], i1_v)

        @pl.loop(0, per // SC_WINDOW)
        def _(s):
            off = s * SC_WINDOW
            pltpu.sync_copy(x_hbm.at[pl.ds(base + off, SC_WINDOW)], buf)
            pltpu.sync_copy(buf, o_hbm.at[i0_v.at[0, pl.ds(off, SC_WINDOW)]])
            pltpu.sync_copy(buf, o_hbm.at[i1_v.at[0, pl.ds(off, SC_WINDOW)]])

    return run(x_slabs.reshape(N_TOK, SLAB, LANE), d0, d1)


def _sc_collect(y_slabs, dcat):
    per = N_ASSIGN // SC_WORKERS

    @functools.partial(
        pl.kernel, out_type=jax.ShapeDtypeStruct((N_ASSIGN, SLAB, LANE), F32), mesh=_sc_mesh(), name="moe_collect",
        scratch_types=[pltpu.VMEM((1, per), I32), pltpu.VMEM((SC_WINDOW, SLAB, LANE), F32)])
    def run(y_hbm, i_hbm, o_hbm, i_v, buf):
        base = _sc_worker() * per
        pltpu.sync_copy(i_hbm.at[:, pl.ds(base, per)], i_v)

        @pl.loop(0, per // SC_WINDOW)
        def _(s):
            off = s * SC_WINDOW
            pltpu.sync_copy(y_hbm.at[i_v.at[0, pl.ds(off, SC_WINDOW)]], buf)
            pltpu.sync_copy(buf, o_hbm.at[pl.ds(base + off, SC_WINDOW)])

    return run(y_slabs.reshape(P_SLOTS, SLAB, LANE), dcat)


def _expert_kernel(be_ref, first_ref, nu_ref, x_ref, wg_ref, wu_ref, wd_ref, y_ref, xs, wg_bf, wu_bf, wd_bf):
    i = pl.program_id(0)

    @pl.when((first_ref[i] == 1) & (i < nu_ref[0]))
    def _():
        wg_bf[...] = wg_ref[0, 0].astype(BF16)
        wu_bf[...] = wu_ref[0, 0].astype(BF16)
        wd_bf[...] = wd_ref[0, 0].astype(BF16)

    @pl.when(i < nu_ref[0])
    def _():
        _load_slabs(x_ref, xs, EBLK, BF16)
        xb = xs[...]
        g = jnp.dot(xb, wg_bf[...], preferred_element_type=F32)
        u = jnp.dot(xb, wu_bf[...], preferred_element_type=F32)
        hmid = (g * _sigmoid(g) * u).astype(BF16)
        _store_slabs(y_ref, jnp.dot(hmid, wd_bf[...], preferred_element_type=F32))


def _experts(x_sorted, block_e, first, n_used, w_gate, w_up, w_down, layer):
    wspec = lambda r, c: pl.BlockSpec((1, 1, r, c), lambda i, be, fi, nu: (layer, be[i], 0, 0))
    slab_spec = pl.BlockSpec((EBLK * SLAB, LANE), lambda i, be, fi, nu: (jnp.minimum(i, nu[0] - 1), 0))
    return pl.pallas_call(
        _expert_kernel,
        out_shape=jax.ShapeDtypeStruct((P_SLOTS * SLAB, LANE), F32),
        grid_spec=pltpu.PrefetchScalarGridSpec(
            num_scalar_prefetch=3, grid=(N_EBLK,),
            in_specs=[slab_spec, wspec(D, D_EXPERT), wspec(D, D_EXPERT), wspec(D_EXPERT, D)],
            out_specs=slab_spec,
            scratch_shapes=[
                pltpu.VMEM((EBLK, D), BF16),
                pltpu.VMEM((D, D_EXPERT), BF16), pltpu.VMEM((D, D_EXPERT), BF16), pltpu.VMEM((D_EXPERT, D), BF16),
            ]),
        compiler_params=_cparams(),
        name="moe_experts",
    )(block_e, first, n_used, x_sorted.reshape(P_SLOTS * SLAB, LANE), w_gate, w_up, w_down)


def _combine_kernel(final, mr_ref, e0_ref, e1_ref, wt_ref, y_ref, mod_ref, fg_ref, o_ref, a_scr, b_scr):
    _load_slabs(e0_ref, a_scr, TM, F32)
    _load_slabs(e1_ref, b_scr, TM, F32)
    wt = wt_ref[...]
    moe = wt[:, 0:1] * a_scr[...] + wt[:, 1:2] * b_scr[...]
    y_new = y_ref[...] + mod_ref[0][5:6] * moe
    o_ref[...] = _rms(y_new, fg_ref[...]) if final else y_new


def _combine(ym, wts, y, mods, mrow, final_g, blk0, nblk, final):
    tok = lambda width: pl.BlockSpec((TM, width), lambda j, *_: (blk0 + j, 0))
    slab0 = pl.BlockSpec((TM * SLAB, LANE), lambda j, *_: (blk0 + j, 0))
    slab1 = pl.BlockSpec((TM * SLAB, LANE), lambda j, *_: (NB + blk0 + j, 0))
    mod = pl.BlockSpec((1, 6, D), lambda j, mr: (mr[blk0 + j], 0, 0))
    return pl.pallas_call(
        functools.partial(_combine_kernel, final),
        out_shape=jax.ShapeDtypeStruct((nblk * TM, D), F32),
        grid_spec=pltpu.PrefetchScalarGridSpec(
            num_scalar_prefetch=1, grid=(nblk,),
            in_specs=[slab0, slab1, tok(TOP_K), tok(D), mod, _full_spec((1, D))],
            out_specs=pl.BlockSpec((TM, D), lambda j, *_: (j, 0)),
            scratch_shapes=[pltpu.VMEM((TM, D), F32), pltpu.VMEM((TM, D), F32)]),
        compiler_params=_cparams(),
        name="moe_combine",
    )(mrow, ym, ym, wts, y, mods, final_g)


def kernel(x_prompt, x_sample, cache_attn_k, cache_attn_v, state_mlstm_C, state_mlstm_n, state_mlstm_m, c, c_ctx, ada_w, ada_b, norm1_g, norm2_g, conv_w_in, conv_w_dw, conv_b_dw, conv_ln_g, conv_ln_b, conv_w_out, attn_w_qkv, attn_q_norm, attn_k_norm, attn_w_o, mlstm_w_in, mlstm_b_gate, mlstm_norm_g, mlstm_w_out, moe_w_group, moe_b_group, moe_w_router, moe_b_router, moe_w_gate, moe_w_up, moe_w_down, final_norm_g):
    y = jnp.concatenate([x_prompt.reshape(NP_TOK, D), x_sample.reshape(NS_TOK, D)], axis=0)
    cvec = jnp.concatenate([c_ctx[None, :], c, jnp.zeros((MOD_ROWS - 1 - DEC_BATCH, D), F32)], axis=0)
    mods = _ada_all(cvec, ada_w, ada_b)
    rope = _rope_blocks()
    new_k = new_v = new_c = new_n = new_m = None
    for i in range(DEPTH):
        kind, slot = i % 3, i // 3
        mrow = jnp.asarray(_MOD_ROW + i * MOD_ROWS)
        g1 = norm1_g[i].reshape(1, D)
        if kind == 0:
            u = _conv_in(y, mods, mrow, g1, conv_w_in[slot].astype(BF16))
            w_dw = jnp.concatenate([conv_w_dw[slot], jnp.zeros((1, D), F32)], axis=0)
            y = _conv_main(u, y, mods, mrow, w_dw, conv_b_dw[slot].reshape(1, D), conv_ln_g[slot].reshape(1, D),
                           conv_ln_b[slot].reshape(1, D), conv_w_out[slot].astype(BF16))
        elif kind == 1:
            q, kb, vb, kf, vf = _attn_qkv(y, mods, mrow, g1, attn_w_qkv[slot].astype(BF16),
                                          attn_q_norm[slot].reshape(1, HEAD_DIM), attn_k_norm[slot].reshape(1, HEAD_DIM),
                                          rope)
            new_k = kf[:NP_TOK].reshape(BATCH, 1, SEQ, N_KV_HEADS, HEAD_DIM)
            new_v = vf[:NP_TOK].reshape(BATCH, 1, SEQ, N_KV_HEADS, HEAD_DIM)
            ck = cache_attn_k[:, slot].reshape(DEC_BATCH, PAST_LEN, KV_DIM)
            cv = cache_attn_v[:, slot].reshape(DEC_BATCH, PAST_LEN, KV_DIM)
            y = _attention(q, kb, vb, ck, cv, attn_w_o[slot].astype(BF16), y, mods, i)
        else:
            w_in = mlstm_w_in[slot]
            w_gate = jnp.concatenate([w_in[:, 4 * D:], jnp.zeros((D, LANE - 4 * M_HEADS), F32)], axis=1)
            b_gate = jnp.concatenate([mlstm_b_gate[slot], jnp.zeros((LANE - 4 * M_HEADS,), F32)]).reshape(1, LANE)
            q, k, v, o, gates = _mlstm_in(y, mods, mrow, g1, w_in[:, :4 * D].astype(BF16), _split_hi_lo(w_gate), b_gate)
            sc = state_mlstm_C[:, slot]
            sn = state_mlstm_n[:, slot].reshape(DEC_BATCH, 2, M_HEADS, 1, M_HEAD_DIM)
            sm = state_mlstm_m[:, slot].reshape(DEC_BATCH, 2, M_HEADS, 1, 1)
            hsum, nc_, nn_, nm_ = _mlstm_scan(q, k, v, gates, sc, sn, sm)
            new_c = nc_[:, None]
            new_n = nn_.reshape(BATCH, 1, 2, M_HEADS, M_HEAD_DIM)
            new_m = nm_[..., 0, 0].reshape(BATCH, 1, 2, M_HEADS)
            y = _mlstm_out(hsum, o, mlstm_norm_g[slot].reshape(1, D), mlstm_w_out[slot].astype(BF16), y, mods, mrow)
        w_route = jnp.concatenate([moe_w_group[i], moe_w_router[i],
                                   jnp.zeros((D, LANE - N_GROUPS - N_EXPERTS), F32)], axis=1)
        b_route = jnp.concatenate([moe_b_group[i], moe_b_router[i],
                                   jnp.zeros((LANE - N_GROUPS - N_EXPERTS,), F32)]).reshape(1, LANE)
        x2, eid, ewt = _route(y, mods, mrow, norm2_g[i].reshape(1, D), _split_hi_lo(w_route), b_route)
        d0, d1, block_e, first, n_used = _dispatch_tables(eid)
        x_sorted = _sc_dispatch(x2, d0, d1)
        y_sorted = _experts(x_sorted, block_e, first, n_used, moe_w_gate, moe_w_up, moe_w_down, i)
        ym = _sc_collect(y_sorted, jnp.concatenate([d0, d1], axis=1))
        ym = ym.reshape(N_ASSIGN * SLAB, LANE)
        fg = final_norm_g.reshape(1, D)
        if i + 1 < DEPTH:
            y = _combine(ym, ewt, y, mods, mrow, fg, 0, NB, False)
        else:
            y_prompt = _combine(ym, ewt, y, mods, mrow, fg, 0, NBP, True).reshape(BATCH, SEQ, D)
            y_sample = _combine(ym, ewt, y, mods, mrow, fg, NBP, NB - NBP, True).reshape(DEC_BATCH, DEC_SEQ, D)
    return (y_prompt, y_sample, new_k, new_v, new_c, new_n, new_m)
```

```python
import functools

import jax
import jax.numpy as jnp
import numpy as np
from jax import lax
from jax.experimental import pallas as pl
from jax.experimental.pallas import tpu as pltpu
from jax.experimental.pallas import tpu_sc as plsc

F32 = jnp.float32
BF16 = jnp.bfloat16
I32 = jnp.int32

D = 1024
BATCH, SEQ = 16, 256
DEC_BATCH, DEC_SEQ = 8, 1024
PAST_LEN = 256
DEPTH = 4
GRID_W = 64
EPS = 1e-6
CONV_WIDTH = 31
CONV_PAD = CONV_WIDTH // 2
HEAD_DIM = 128
N_HEADS = 8
N_KV_HEADS = 2
GQA_GROUP = N_HEADS // N_KV_HEADS
Q_DIM = N_HEADS * HEAD_DIM
KV_DIM = N_KV_HEADS * HEAD_DIM
QKV_DIM = Q_DIM + 2 * KV_DIM
ROPE_THETA = 10000.0
M_HEADS = 4
M_HEAD_DIM = D // M_HEADS
M_CHUNK = 64
N_GROUPS = 4
EXPERTS_PER_GROUP = 8
N_EXPERTS = N_GROUPS * EXPERTS_PER_GROUP
TOP_K = 2
D_EXPERT = 512

NP_TOK = BATCH * SEQ
NS_TOK = DEC_BATCH * DEC_SEQ
N_TOK = NP_TOK + NS_TOK
TM = 256
NB = N_TOK // TM
NBP = NP_TOK // TM
BLK_PER_DEC = DEC_SEQ // TM
MOD_ROWS = 16
HALO = 16
LANE = 128
SUBLANE = 8

N_ASSIGN = N_TOK * TOP_K
EBLK = 256
N_EBLK = N_ASSIGN // EBLK + N_EXPERTS
P_SLOTS = N_EBLK * EBLK
N_PAD_SLOTS = P_SLOTS - N_ASSIGN

VMEM_LIMIT = 56 * 1024 * 1024


def _block_tables():
    j = np.arange(NB)
    is_p = j < NBP
    mod_row = np.where(is_p, 0, 1 + (j - NBP) // BLK_PER_DEC)
    rope_idx = np.where(is_p, 0, 1 + (j - NBP) % BLK_PER_DEC)
    first = np.where(is_p, 1, ((j - NBP) % BLK_PER_DEC == 0).astype(np.int64))
    last = np.where(is_p, 1, ((j - NBP) % BLK_PER_DEC == BLK_PER_DEC - 1).astype(np.int64))
    return (mod_row.astype(np.int32), rope_idx.astype(np.int32), first.astype(np.int32), last.astype(np.int32))


_MOD_ROW, _ROPE_IDX, _SEQ_FIRST, _SEQ_LAST = _block_tables()


def _cparams(n_axes=1):
    return pltpu.CompilerParams(dimension_semantics=("arbitrary",) * n_axes, vmem_limit_bytes=VMEM_LIMIT)


def _sigmoid(x):
    return 1.0 / (1.0 + jnp.exp(-x))


def _rms(x, g):
    return x * lax.rsqrt(jnp.mean(x * x, axis=-1, keepdims=True) + EPS) * g


def _split_hi_lo(w):
    hi = w.astype(BF16)
    return jnp.stack([hi, (w - hi.astype(F32)).astype(BF16)], axis=0)


def _dot_hi_lo(x, w_ref):
    xh = x.astype(BF16)
    xl = (x - xh.astype(F32)).astype(BF16)
    wh, wl = w_ref[0], w_ref[1]
    return (jnp.dot(xh, wh, preferred_element_type=F32)
            + (jnp.dot(xh, wl, preferred_element_type=F32) + jnp.dot(xl, wh, preferred_element_type=F32)))


def _norm_mod(y, g, mod, which):
    shift = mod[3 * which:3 * which + 1]
    scale = mod[3 * which + 1:3 * which + 2]
    return _rms(y, g) * (1.0 + scale) + shift


def _ada_kernel(c_ref, w_ref, b_ref, o_ref):
    c = c_ref[...]
    s = c * _sigmoid(c)
    o_ref[0] = jnp.dot(s.astype(BF16), w_ref[0].astype(BF16), preferred_element_type=F32) + b_ref[0]


def _ada_all(cvec, ada_w, ada_b):
    tn = 1536
    out = pl.pallas_call(
        _ada_kernel,
        out_shape=jax.ShapeDtypeStruct((DEPTH, MOD_ROWS, 6 * D), F32),
        grid=(DEPTH, 6 * D // tn),
        in_specs=[
            pl.BlockSpec((MOD_ROWS, D), lambda l, n: (0, 0)),
            pl.BlockSpec((1, D, tn), lambda l, n: (l, 0, n)),
            pl.BlockSpec((1, 1, tn), lambda l, n: (l, 0, n)),
        ],
        out_specs=pl.BlockSpec((1, MOD_ROWS, tn), lambda l, n: (l, 0, n)),
        compiler_params=_cparams(2),
        name="ada_mod",
    )(cvec, ada_w, ada_b.reshape(DEPTH, 1, 6 * D))
    return out.reshape(DEPTH * MOD_ROWS, 6, D)


def _tok_spec(width):
    return pl.BlockSpec((TM, width), lambda j, *_: (j, 0))


def _mod_spec():
    return pl.BlockSpec((1, 6, D), lambda j, mr, *_: (mr[j], 0, 0))


def _full_spec(shape):
    nd = len(shape)
    return pl.BlockSpec(shape, lambda j, *_: (0,) * nd)


def _conv_in_kernel(mr_ref, y_ref, mod_ref, g_ref, w_ref, u_ref):
    h = _norm_mod(y_ref[...], g_ref[...], mod_ref[0], 0)
    ag = jnp.dot(h.astype(BF16), w_ref[...], preferred_element_type=F32)
    u_ref[...] = ag[:, :D] * _sigmoid(ag[:, D:])


def _conv_in(y, mods, mrow, g1, w_in):
    return pl.pallas_call(
        _conv_in_kernel,
        out_shape=jax.ShapeDtypeStruct((N_TOK, D), F32),
        grid_spec=pltpu.PrefetchScalarGridSpec(
            num_scalar_prefetch=1, grid=(NB,),
            in_specs=[_tok_spec(D), _mod_spec(), _full_spec((1, D)), _full_spec((D, 2 * D))],
            out_specs=_tok_spec(D)),
        compiler_params=_cparams(),
        name="conv_in",
    )(mrow, y, mods, g1, w_in)


def _conv_main_kernel(mr_ref, first_ref, last_ref, u_ref, up_ref, un_ref, wdw_ref, bdw_ref, lg_ref, lb_ref,
                      wout_ref, y_ref, mod_ref, o_ref, ext_ref, acc_ref):
    j = pl.program_id(0)
    zero = jnp.zeros((HALO, D), F32)
    ext_ref[0:HALO, :] = jnp.where(first_ref[j] == 1, zero, up_ref[...])
    ext_ref[HALO:HALO + TM, :] = u_ref[...]
    ext_ref[HALO + TM:2 * HALO + TM, :] = jnp.where(last_ref[j] == 1, zero, un_ref[...])

    off0 = HALO - CONV_PAD
    n_a = (off0 + CONV_WIDTH - 1) // SUBLANE + 1
    n_chunks = TM // SUBLANE

    def strip(ci, carry):
        cs = pl.ds(pl.multiple_of(ci * LANE, LANE), LANE)
        wk = [jnp.broadcast_to(wdw_ref[k:k + 1, cs], (SUBLANE, LANE)) for k in range(CONV_WIDTH)]
        bias = jnp.broadcast_to(bdw_ref[:, cs], (SUBLANE, LANE))
        sub = lax.broadcasted_iota(I32, (SUBLANE, LANE), 0)
        prev_rot, prev_v0 = None, None
        for j in range(n_chunks + 1):
            tiles = [ext_ref[SUBLANE * (j + a):SUBLANE * (j + a + 1), cs] for a in range(n_a)]
            part = []
            for s in range(SUBLANE):
                acc = None
                for a in range(n_a):
                    k = SUBLANE * a + s - off0
                    if (0 <= k < CONV_WIDTH) and not (s == 0 and j == n_chunks):
                        term = tiles[a] * wk[k]
                        acc = term if acc is None else acc + term
                part.append(acc)
            rot = [None] + [pltpu.roll(part[s], SUBLANE - s, 0) for s in range(1, SUBLANE)]
            if j >= 1:
                out = prev_v0 + bias
                for s in range(1, SUBLANE):
                    out = out + jnp.where(sub < SUBLANE - s, prev_rot[s], rot[s])
                acc_ref[SUBLANE * (j - 1):SUBLANE * j, cs] = out
            prev_rot, prev_v0 = rot, part[0]
        return carry

    lax.fori_loop(0, D // LANE, strip, 0)

    c = acc_ref[...]
    mu = jnp.mean(c, axis=-1, keepdims=True)
    cc = c - mu
    var = jnp.mean(cc * cc, axis=-1, keepdims=True)
    z = cc * lax.rsqrt(var + EPS) * lg_ref[...] + lb_ref[...]
    z = z * _sigmoid(z)
    out = jnp.dot(z.astype(BF16), wout_ref[...], preferred_element_type=F32)
    o_ref[...] = y_ref[...] + mod_ref[0][2:3] * out


def _conv_main(u, y, mods, mrow, w_dw, b_dw, ln_g, ln_b, w_out):
    nh = N_TOK // HALO
    per = TM // HALO
    return pl.pallas_call(
        _conv_main_kernel,
        out_shape=jax.ShapeDtypeStruct((N_TOK, D), F32),
        grid_spec=pltpu.PrefetchScalarGridSpec(
            num_scalar_prefetch=3, grid=(NB,),
            in_specs=[
                _tok_spec(D),
                pl.BlockSpec((HALO, D), lambda j, *_: (jnp.maximum(j * per - 1, 0), 0)),
                pl.BlockSpec((HALO, D), lambda j, *_: (jnp.minimum((j + 1) * per, nh - 1), 0)),
                _full_spec((CONV_WIDTH + 1, D)), _full_spec((1, D)), _full_spec((1, D)), _full_spec((1, D)),
                _full_spec((D, D)), _tok_spec(D), _mod_spec(),
            ],
            out_specs=_tok_spec(D),
            scratch_shapes=[pltpu.VMEM((TM + 2 * HALO, D), F32), pltpu.VMEM((TM, D), F32)]),
        compiler_params=_cparams(),
        name="conv_main",
    )(mrow, jnp.asarray(_SEQ_FIRST), jnp.asarray(_SEQ_LAST), u, u, u, w_dw, b_dw, ln_g, ln_b, w_out, y, mods)


def _rope_angles():
    rows = DEC_SEQ // GRID_W
    row = jnp.repeat(jnp.arange(rows, dtype=F32), GRID_W)
    col = jnp.tile(jnp.arange(GRID_W, dtype=F32), rows)
    axis_dim = HEAD_DIM // 2
    freqs = jnp.power(ROPE_THETA, -jnp.arange(axis_dim // 2, dtype=F32) * 2.0 / axis_dim)
    ang_r = row[:, None] * freqs[None, :]
    ang_c = col[:, None] * freqs[None, :]
    return jnp.concatenate([ang_r, ang_r, ang_c, ang_c], axis=-1)


def _rope_blocks():
    ang = _rope_angles()
    cos, sin = jnp.cos(ang), jnp.sin(ang)
    lane = np.arange(HEAD_DIM)
    lo = jnp.asarray(((lane % (HEAD_DIM // 2)) < HEAD_DIM // 4).astype(np.float32))
    sin_a = -sin * lo[None, :]
    sin_b = sin * (1.0 - lo)[None, :]
    nblk = DEC_SEQ // TM
    ident = jnp.ones((1, TM, HEAD_DIM), F32)
    zeros = jnp.zeros((1, TM, HEAD_DIM), F32)
    cos_t = jnp.concatenate([ident, cos.reshape(nblk, TM, HEAD_DIM)], axis=0)
    sa_t = jnp.concatenate([zeros, sin_a.reshape(nblk, TM, HEAD_DIM)], axis=0)
    sb_t = jnp.concatenate([zeros, sin_b.reshape(nblk, TM, HEAD_DIM)], axis=0)
    return cos_t, sa_t, sb_t


def _attn_qkv_kernel(mr_ref, ri_ref, y_ref, mod_ref, g_ref, w_ref, qg_ref, kg_ref, cos_ref, sa_ref, sb_ref,
                     q_ref, kb_ref, vb_ref, kf_ref, vf_ref):
    h = _norm_mod(y_ref[...], g_ref[...], mod_ref[0], 0)
    qkv = jnp.dot(h.astype(BF16), w_ref[...], preferred_element_type=F32)
    cos, sa, sb = cos_ref[0], sa_ref[0], sb_ref[0]
    quarter = HEAD_DIM // 4

    def head(x, g):
        xn = _rms(x, g)
        return xn * cos + pltpu.roll(xn, HEAD_DIM - quarter, 1) * sa + pltpu.roll(xn, quarter, 1) * sb

    scale = HEAD_DIM ** -0.5
    for hd in range(N_HEADS):
        sl = slice(hd * HEAD_DIM, (hd + 1) * HEAD_DIM)
        q_ref[:, sl] = (head(qkv[:, sl], qg_ref[...]) * scale).astype(BF16)
    for kv in range(N_KV_HEADS):
        sl = slice(kv * HEAD_DIM, (kv + 1) * HEAD_DIM)
        kr = head(qkv[:, Q_DIM + kv * HEAD_DIM:Q_DIM + (kv + 1) * HEAD_DIM], kg_ref[...])
        kf_ref[:, sl] = kr
        kb_ref[:, sl] = kr.astype(BF16)
    v = qkv[:, Q_DIM + KV_DIM:]
    vf_ref[...] = v
    vb_ref[...] = v.astype(BF16)


def _attn_qkv(y, mods, mrow, g1, w_qkv, q_g, k_g, rope):
    cos_t, sa_t, sb_t = rope
    rspec = pl.BlockSpec((1, TM, HEAD_DIM), lambda j, mr, ri: (ri[j], 0, 0))
    return pl.pallas_call(
        _attn_qkv_kernel,
        out_shape=(jax.ShapeDtypeStruct((N_TOK, Q_DIM), BF16), jax.ShapeDtypeStruct((N_TOK, KV_DIM), BF16),
                   jax.ShapeDtypeStruct((N_TOK, KV_DIM), BF16), jax.ShapeDtypeStruct((N_TOK, KV_DIM), F32),
                   jax.ShapeDtypeStruct((N_TOK, KV_DIM), F32)),
        grid_spec=pltpu.PrefetchScalarGridSpec(
            num_scalar_prefetch=2, grid=(NB,),
            in_specs=[_tok_spec(D), _mod_spec(), _full_spec((1, D)), _full_spec((D, QKV_DIM)),
                      _full_spec((1, HEAD_DIM)), _full_spec((1, HEAD_DIM)), rspec, rspec, rspec],
            out_specs=(_tok_spec(Q_DIM), _tok_spec(KV_DIM), _tok_spec(KV_DIM), _tok_spec(KV_DIM), _tok_spec(KV_DIM))),
        compiler_params=_cparams(),
        name="attn_qkv",
    )(mrow, jnp.asarray(_ROPE_IDX), y, mods, g1, w_qkv, q_g, k_g, cos_t, sa_t, sb_t)


def _attn_heads(q, ks, vs, o_scr):
    nt = (((1,), (1,)), ((), ()))
    for hd in range(N_HEADS):
        g = hd // GQA_GROUP
        qh = q[:, hd * HEAD_DIM:(hd + 1) * HEAD_DIM]
        gs = slice(g * HEAD_DIM, (g + 1) * HEAD_DIM)
        ss = [lax.dot_general(qh, k[:, gs], nt, preferred_element_type=F32) for k in ks]
        m = functools.reduce(jnp.maximum, [jnp.max(s, axis=-1, keepdims=True) for s in ss])
        ps = [jnp.exp(s - m) for s in ss]
        l = functools.reduce(lambda a, b: a + b, [jnp.sum(p, axis=-1, keepdims=True) for p in ps])
        o = functools.reduce(lambda a, b: a + b,
                             [jnp.dot(p.astype(BF16), v[:, gs], preferred_element_type=F32) for p, v in zip(ps, vs)])
        o_scr[:, hd * HEAD_DIM:(hd + 1) * HEAD_DIM] = (o / l).astype(BF16)


def _attn_ctx_kernel(q_ref, k_ref, v_ref, wo_ref, y_ref, mod_ref, o_ref, o_scr):
    _attn_heads(q_ref[...], [k_ref[...]], [v_ref[...]], o_scr)
    out = jnp.dot(o_scr[...], wo_ref[...], preferred_element_type=F32)
    o_ref[...] = y_ref[...] + mod_ref[0][2:3] * out


def _attn_lat_kernel(q_ref, k_ref, v_ref, ck_ref, cv_ref, wo_ref, y_ref, mod_ref, ctx_out_ref, o_ref, o_scr):
    del ctx_out_ref
    _attn_heads(q_ref[...], [k_ref[...], ck_ref[0].astype(BF16)], [v_ref[...], cv_ref[0].astype(BF16)], o_scr)
    out = jnp.dot(o_scr[...], wo_ref[...], preferred_element_type=F32)
    o_ref[...] = y_ref[...] + mod_ref[0][2:3] * out


def _attention(q, kb, vb, cache_k, cache_v, w_o, y, mods, layer):
    y_ctx = pl.pallas_call(
        _attn_ctx_kernel,
        out_shape=jax.ShapeDtypeStruct((N_TOK, D), F32),
        grid=(BATCH,),
        in_specs=[
            pl.BlockSpec((SEQ, Q_DIM), lambda s: (s, 0)),
            pl.BlockSpec((SEQ, KV_DIM), lambda s: (s, 0)),
            pl.BlockSpec((SEQ, KV_DIM), lambda s: (s, 0)),
            pl.BlockSpec((Q_DIM, D), lambda s: (0, 0)),
            pl.BlockSpec((SEQ, D), lambda s: (s, 0)),
            pl.BlockSpec((1, 6, D), lambda s: (layer * MOD_ROWS, 0, 0)),
        ],
        out_specs=pl.BlockSpec((SEQ, D), lambda s: (s, 0)),
        scratch_shapes=[pltpu.VMEM((SEQ, Q_DIM), BF16)],
        compiler_params=_cparams(),
        name="attn_ctx",
    )(q, kb, vb, w_o, y, mods)
    pb = NP_TOK // DEC_SEQ
    return pl.pallas_call(
        _attn_lat_kernel,
        out_shape=jax.ShapeDtypeStruct((N_TOK, D), F32),
        input_output_aliases={8: 0},
        grid=(DEC_BATCH, BLK_PER_DEC),
        in_specs=[
            pl.BlockSpec((TM, Q_DIM), lambda b, t: (NBP + b * BLK_PER_DEC + t, 0)),
            pl.BlockSpec((DEC_SEQ, KV_DIM), lambda b, t: (pb + b, 0)),
            pl.BlockSpec((DEC_SEQ, KV_DIM), lambda b, t: (pb + b, 0)),
            pl.BlockSpec((1, PAST_LEN, KV_DIM), lambda b, t: (b, 0, 0)),
            pl.BlockSpec((1, PAST_LEN, KV_DIM), lambda b, t: (b, 0, 0)),
            pl.BlockSpec((Q_DIM, D), lambda b, t: (0, 0)),
            pl.BlockSpec((TM, D), lambda b, t: (NBP + b * BLK_PER_DEC + t, 0)),
            pl.BlockSpec((1, 6, D), lambda b, t: (layer * MOD_ROWS + 1 + b, 0, 0)),
            pl.BlockSpec(memory_space=pl.ANY),
        ],
        out_specs=pl.BlockSpec((TM, D), lambda b, t: (NBP + b * BLK_PER_DEC + t, 0)),
        scratch_shapes=[pltpu.VMEM((TM, Q_DIM), BF16)],
        compiler_params=_cparams(2),
        name="attn_lat",
    )(q, kb, vb, cache_k, cache_v, w_o, y, mods, y_ctx)


def _log_sigmoid(x):
    return jnp.minimum(x, 0.0) - jnp.log(1.0 + jnp.exp(-jnp.abs(x)))


def _mlstm_in_kernel(mr_ref, y_ref, mod_ref, g_ref, w_ref, wg_ref, bg_ref, q_ref, k_ref, v_ref, o_ref, gt_ref):
    h = _norm_mod(y_ref[...], g_ref[...], mod_ref[0], 0)
    hb = h.astype(BF16)
    q_ref[...] = jnp.dot(hb, w_ref[:, 0:D], preferred_element_type=F32).astype(BF16)
    k_ref[...] = (jnp.dot(hb, w_ref[:, D:2 * D], preferred_element_type=F32) * (M_HEAD_DIM ** -0.5)).astype(BF16)
    v_ref[...] = jnp.dot(hb, w_ref[:, 2 * D:3 * D], preferred_element_type=F32).astype(BF16)
    o_ref[...] = _sigmoid(jnp.dot(hb, w_ref[:, 3 * D:4 * D], preferred_element_type=F32))
    gt = _dot_hi_lo(h, wg_ref) + bg_ref[...]
    lane = lax.broadcasted_iota(I32, gt.shape, 1)
    is_f = ((lane >= M_HEADS) & (lane < 2 * M_HEADS)) | ((lane >= 3 * M_HEADS) & (lane < 4 * M_HEADS))
    gt_ref[...] = jnp.where(is_f, _log_sigmoid(gt), gt)


def _mlstm_in(y, mods, mrow, g1, w_main, w_gate, b_gate):
    return pl.pallas_call(
        _mlstm_in_kernel,
        out_shape=(jax.ShapeDtypeStruct((N_TOK, D), BF16), jax.ShapeDtypeStruct((N_TOK, D), BF16),
                   jax.ShapeDtypeStruct((N_TOK, D), BF16), jax.ShapeDtypeStruct((N_TOK, D), F32),
                   jax.ShapeDtypeStruct((N_TOK, LANE), F32)),
        grid_spec=pltpu.PrefetchScalarGridSpec(
            num_scalar_prefetch=1, grid=(NB,),
            in_specs=[_tok_spec(D), _mod_spec(), _full_spec((1, D)), _full_spec((D, 4 * D)),
                      _full_spec((2, D, LANE)), _full_spec((1, LANE))],
            out_specs=(_tok_spec(D), _tok_spec(D), _tok_spec(D), _tok_spec(D), _tok_spec(LANE))),
        compiler_params=_cparams(),
        name="mlstm_in",
    )(mrow, y, mods, g1, w_main, w_gate, b_gate)


def _mlstm_chunk(d, hd, c, m, q_ref, k_ref, v_ref, gc_ref, gr_ref, c_scr, n_scr):
    L = M_CHUNK
    r0 = pl.multiple_of(c * L, L)
    hs = slice(hd * M_HEAD_DIM, (hd + 1) * M_HEAD_DIM)
    q = q_ref[pl.ds(r0, L), hs]
    k = k_ref[pl.ds(r0, L), hs]
    v = v_ref[pl.ds(r0, L), hs]
    col = gc_ref[hd, pl.ds(r0, L), :]
    row = gr_ref[hd, c]
    i_col, lf_col = col[:, 2 * d:2 * d + 1], col[:, 2 * d + 1:2 * d + 2]
    i_row, lf_row = row[2 * d:2 * d + 1, :], row[2 * d + 1:2 * d + 2, :]
    t_idx = lax.broadcasted_iota(I32, (L, L), 0)
    s_idx = lax.broadcasted_iota(I32, (L, L), 1)
    if d == 0:
        mask, mask_t = s_idx <= t_idx, t_idx <= s_idx
    else:
        mask, mask_t = s_idx >= t_idx, t_idx >= s_idx
    b_col = jnp.sum(jnp.where(mask, lf_row, 0.0), axis=1, keepdims=True)
    b_row = jnp.sum(jnp.where(mask_t, lf_col, 0.0), axis=0, keepdims=True)
    log_d = jnp.where(mask, b_col - b_row + i_row, -jnp.inf)
    li = b_col + m
    m_r = jnp.maximum(li, jnp.max(log_d, axis=1, keepdims=True))
    a_int = jnp.exp(li - m_r)
    s = lax.dot_general(q, k, (((1,), (1,)), ((), ())), preferred_element_type=F32) * jnp.exp(log_d - m_r)
    cmat = c_scr[d, hd]
    num = a_int * jnp.dot(q, cmat.astype(BF16), preferred_element_type=F32) \
        + jnp.dot(s.astype(BF16), v, preferred_element_type=F32)
    qn = jnp.sum(q.astype(F32) * n_scr[d, hd], axis=1, keepdims=True)
    den = a_int * qn + jnp.sum(s, axis=1, keepdims=True)
    hh = num / jnp.maximum(jnp.abs(den), jnp.exp(-m_r))
    b_last = b_row[:, L - 1:L] if d == 0 else b_row[:, 0:1]
    log_w = b_last - b_col + i_col
    m_new = jnp.maximum(b_last + m, jnp.max(log_w, axis=0, keepdims=True))
    w = jnp.exp(log_w - m_new)
    decay = jnp.exp(b_last + m - m_new)
    kw = k.astype(F32) * w
    c_scr[d, hd] = decay * cmat + lax.dot_general(kw.astype(BF16), v, (((0,), (0,)), ((), ())),
                                                  preferred_element_type=F32)
    w_row = jnp.exp(b_last - b_row + i_row - m_new)
    n_scr[d, hd] = decay * n_scr[d, hd] + jnp.dot(w_row.astype(BF16), k, preferred_element_type=F32)
    return r0, hs, hh, m_new


def _mlstm_scan_body(n_chunks, q_ref, k_ref, v_ref, gc_ref, gr_ref, h_ref, hb_scr, c_scr, n_scr, m0):
    def body(c, ms):
        out = []
        for hd in range(M_HEADS):
            r0, hs, hf, mf = _mlstm_chunk(0, hd, c, ms[2 * hd], q_ref, k_ref, v_ref, gc_ref, gr_ref, c_scr, n_scr)
            h_ref[pl.ds(r0, M_CHUNK), hs] = hf
            r1, hs, hb, mb = _mlstm_chunk(1, hd, n_chunks - 1 - c, ms[2 * hd + 1], q_ref, k_ref, v_ref, gc_ref,
                                          gr_ref, c_scr, n_scr)
            hb_scr[pl.ds(r1, M_CHUNK), hs] = hb
            out += [mf, mb]
        return tuple(out)

    ms = lax.fori_loop(0, n_chunks, body, tuple(m0))
    h_ref[...] += hb_scr[...]
    return ms


def _mlstm_scan_ctx_kernel(q_ref, k_ref, v_ref, gc_ref, gr_ref, h_ref, cn_ref, nn_ref, mn_ref, hb_scr, c_scr, n_scr):
    c_scr[...] = jnp.zeros(c_scr.shape, F32)
    n_scr[...] = jnp.zeros(n_scr.shape, F32)
    zero = jnp.zeros((1, 1), F32)
    ms = _mlstm_scan_body(SEQ // M_CHUNK, q_ref, k_ref, v_ref, gc_ref, gr_ref, h_ref, hb_scr, c_scr, n_scr,
                          [zero] * (2 * M_HEADS))
    cn_ref[0] = c_scr[...]
    nn_ref[0] = n_scr[...]
    for hd in range(M_HEADS):
        for d in range(2):
            mn_ref[0, d, hd] = jnp.broadcast_to(ms[2 * hd + d], (1, LANE))


def _mlstm_scan_lat_kernel(q_ref, k_ref, v_ref, gc_ref, gr_ref, c0_ref, n0_ref, m0_ref, ctx_out_ref, h_ref,
                           hb_scr, c_scr, n_scr):
    del ctx_out_ref
    c_scr[...] = c0_ref[0]
    n_scr[...] = n0_ref[0]
    m0 = [m0_ref[0, d, hd] for hd in range(M_HEADS) for d in range(2)]
    _mlstm_scan_body(DEC_SEQ // M_CHUNK, q_ref, k_ref, v_ref, gc_ref, gr_ref, h_ref, hb_scr, c_scr, n_scr, m0)


def _mlstm_scan(q, k, v, gates, state_c, state_n, state_m):
    g16 = gates[:, :4 * M_HEADS].reshape(N_TOK, 4, M_HEADS)
    gcol = jnp.transpose(g16, (2, 0, 1))
    grow = jnp.transpose(g16.reshape(N_TOK // M_CHUNK, M_CHUNK, 4, M_HEADS), (3, 0, 2, 1))
    hd = M_HEAD_DIM
    state_scratch = [pltpu.VMEM((2, M_HEADS, hd, hd), F32), pltpu.VMEM((2, M_HEADS, 1, hd), F32)]
    ncp = SEQ // M_CHUNK
    h_ctx, new_c, new_n, new_m = pl.pallas_call(
        _mlstm_scan_ctx_kernel,
        out_shape=(jax.ShapeDtypeStruct((N_TOK, D), F32),
                   jax.ShapeDtypeStruct((BATCH, 2, M_HEADS, hd, hd), F32),
                   jax.ShapeDtypeStruct((BATCH, 2, M_HEADS, 1, hd), F32),
                   jax.ShapeDtypeStruct((BATCH, 2, M_HEADS, 1, LANE), F32)),
        grid=(BATCH,),
        in_specs=[
            pl.BlockSpec((SEQ, D), lambda s: (s, 0)),
            pl.BlockSpec((SEQ, D), lambda s: (s, 0)),
            pl.BlockSpec((SEQ, D), lambda s: (s, 0)),
            pl.BlockSpec((M_HEADS, SEQ, 4), lambda s: (0, s, 0)),
            pl.BlockSpec((M_HEADS, ncp, 4, M_CHUNK), lambda s: (0, s, 0, 0)),
        ],
        out_specs=(
            pl.BlockSpec((SEQ, D), lambda s: (s, 0)),
            pl.BlockSpec((1, 2, M_HEADS, hd, hd), lambda s: (s, 0, 0, 0, 0)),
            pl.BlockSpec((1, 2, M_HEADS, 1, hd), lambda s: (s, 0, 0, 0, 0)),
            pl.BlockSpec((1, 2, M_HEADS, 1, LANE), lambda s: (s, 0, 0, 0, 0)),
        ),
        scratch_shapes=[pltpu.VMEM((SEQ, D), F32)] + state_scratch,
        compiler_params=_cparams(),
        name="mlstm_scan_ctx",
    )(q, k, v, gcol, grow)
    ncl = DEC_SEQ // M_CHUNK
    pb = NP_TOK // DEC_SEQ
    h_all = pl.pallas_call(
        _mlstm_scan_lat_kernel,
        out_shape=jax.ShapeDtypeStruct((N_TOK, D), F32),
        input_output_aliases={8: 0},
        grid=(DEC_BATCH,),
        in_specs=[
            pl.BlockSpec((DEC_SEQ, D), lambda b: (pb + b, 0)),
            pl.BlockSpec((DEC_SEQ, D), lambda b: (pb + b, 0)),
            pl.BlockSpec((DEC_SEQ, D), lambda b: (pb + b, 0)),
            pl.BlockSpec((M_HEADS, DEC_SEQ, 4), lambda b: (0, pb + b, 0)),
            pl.BlockSpec((M_HEADS, ncl, 4, M_CHUNK), lambda b: (0, pb + b, 0, 0)),
            pl.BlockSpec((1, 2, M_HEADS, hd, hd), lambda b: (b, 0, 0, 0, 0)),
            pl.BlockSpec((1, 2, M_HEADS, 1, hd), lambda b: (b, 0, 0, 0, 0)),
            pl.BlockSpec((1, 2, M_HEADS, 1, 1), lambda b: (b, 0, 0, 0, 0)),
            pl.BlockSpec(memory_space=pl.ANY),
        ],
        out_specs=pl.BlockSpec((DEC_SEQ, D), lambda b: (pb + b, 0)),
        scratch_shapes=[pltpu.VMEM((DEC_SEQ, D), F32)] + state_scratch,
        compiler_params=_cparams(),
        name="mlstm_scan_lat",
    )(q, k, v, gcol, grow, state_c, state_n, state_m, h_ctx)
    return h_all, new_c, new_n, new_m


def _mlstm_out_kernel(mr_ref, h_ref, o_ref, ng_ref, w_ref, y_ref, mod_ref, out_ref, x_scr):
    hc = o_ref[...] * h_ref[...]
    for hd in range(M_HEADS):
        sl = slice(hd * M_HEAD_DIM, (hd + 1) * M_HEAD_DIM)
        x_scr[:, sl] = _rms(hc[:, sl], ng_ref[:, sl]).astype(BF16)
    out = jnp.dot(x_scr[...], w_ref[...], preferred_element_type=F32)
    out_ref[...] = y_ref[...] + mod_ref[0][2:3] * out


def _mlstm_out(hsum, o, norm_g, w_out, y, mods, mrow):
    return pl.pallas_call(
        _mlstm_out_kernel,
        out_shape=jax.ShapeDtypeStruct((N_TOK, D), F32),
        grid_spec=pltpu.PrefetchScalarGridSpec(
            num_scalar_prefetch=1, grid=(NB,),
            in_specs=[_tok_spec(D), _tok_spec(D), _full_spec((1, D)), _full_spec((D, D)), _tok_spec(D), _mod_spec()],
            out_specs=_tok_spec(D),
            scratch_shapes=[pltpu.VMEM((TM, D), BF16)]),
        compiler_params=_cparams(),
        name="mlstm_out",
    )(mrow, hsum, o, norm_g, w_out, y, mods)


ROUTE_OFF = N_GROUPS
SLAB = D // (2 * LANE)
V7X_SC_CORES = 2
V7X_SC_SUBCORES = 16
SC_WORKERS = V7X_SC_CORES * V7X_SC_SUBCORES
SC_WINDOW = 128
HI_MASK = -65536


def _bf16_bits(x):
    return lax.bitcast_convert_type(x.astype(BF16).astype(F32), I32)


def _store_slabs(ref, x):
    rows = x.shape[0]
    for c in range(SLAB):
        lo = lax.shift_right_logical(_bf16_bits(x[:, (2 * c) * LANE:(2 * c + 1) * LANE]), 16)
        hi = _bf16_bits(x[:, (2 * c + 1) * LANE:(2 * c + 2) * LANE]) & HI_MASK
        ref[pl.ds(c, rows, stride=SLAB), :] = lo | hi


def _load_slabs(ref, dst, rows, dtype):
    for c in range(SLAB):
        w = ref[pl.ds(c, rows, stride=SLAB), :]
        lo = lax.bitcast_convert_type(lax.shift_left(w, 16), F32)
        hi = lax.bitcast_convert_type(w & HI_MASK, F32)
        dst[:, (2 * c) * LANE:(2 * c + 1) * LANE] = lo.astype(dtype)
        dst[:, (2 * c + 1) * LANE:(2 * c + 2) * LANE] = hi.astype(dtype)


def _route_kernel(mr_ref, y_ref, mod_ref, g_ref, wr_ref, br_ref, x_ref, id_ref, wt_ref):
    x = _norm_mod(y_ref[...], g_ref[...], mod_ref[0], 1)
    _store_slabs(x_ref, x)
    lg = _dot_hi_lo(x, wr_ref) + br_ref[...]
    lane = lax.broadcasted_iota(I32, lg.shape, 1)
    ninf = -jnp.inf
    big = jnp.int32(LANE)
    lgg = jnp.where(lane < N_GROUPS, lg, ninf)
    gmax = jnp.max(lgg, axis=-1, keepdims=True)
    g_idx = jnp.min(jnp.where(lgg == gmax, lane, big), axis=-1, keepdims=True)
    g_w = 1.0 / jnp.sum(jnp.exp(lgg - gmax), axis=-1, keepdims=True)
    lo = ROUTE_OFF + g_idx * EXPERTS_PER_GROUP
    le = jnp.where((lane >= lo) & (lane < lo + EXPERTS_PER_GROUP), lg, ninf)
    m1 = jnp.max(le, axis=-1, keepdims=True)
    i1 = jnp.min(jnp.where(le == m1, lane, big), axis=-1, keepdims=True)
    le2 = jnp.where(lane == i1, ninf, le)
    m2 = jnp.max(le2, axis=-1, keepdims=True)
    i2 = jnp.min(jnp.where(le2 == m2, lane, big), axis=-1, keepdims=True)
    r = jnp.exp(m2 - m1)
    p1 = 1.0 / (1.0 + r)
    p2 = r / (1.0 + r)
    two = lax.broadcasted_iota(I32, (x.shape[0], TOP_K), 1)
    id_ref[...] = jnp.where(two == 0, i1 - ROUTE_OFF, i2 - ROUTE_OFF)
    wt_ref[...] = jnp.where(two == 0, g_w * p1, g_w * p2)


def _route(y, mods, mrow, g2, w_route, b_route):
    return pl.pallas_call(
        _route_kernel,
        out_shape=(jax.ShapeDtypeStruct((N_TOK * SLAB, LANE), I32), jax.ShapeDtypeStruct((N_TOK, TOP_K), I32),
                   jax.ShapeDtypeStruct((N_TOK, TOP_K), F32)),
        grid_spec=pltpu.PrefetchScalarGridSpec(
            num_scalar_prefetch=1, grid=(NB,),
            in_specs=[_tok_spec(D), _mod_spec(), _full_spec((1, D)), _full_spec((2, D, LANE)), _full_spec((1, LANE))],
            out_specs=(pl.BlockSpec((TM * SLAB, LANE), lambda j, *_: (j, 0)), _tok_spec(TOP_K), _tok_spec(TOP_K))),
        compiler_params=_cparams(),
        name="moe_route",
    )(mrow, y, mods, g2, w_route, b_route)


def _dispatch_tables(expert_id):
    flat_e = expert_id.reshape(-1)
    onehot = (flat_e[:, None] == jnp.arange(N_EXPERTS, dtype=I32)[None, :]).astype(I32)
    csum = jnp.cumsum(onehot, axis=0)
    counts = csum[-1]
    padded = ((counts + EBLK - 1) // EBLK) * EBLK
    pad_end = jnp.cumsum(padded)
    pad_start = pad_end - padded
    dest = jnp.sum((csum - 1 + pad_start[None, :]) * onehot, axis=1).astype(I32)
    n_blk = (padded // EBLK).astype(I32)
    blk_start = (pad_start // EBLK).astype(I32)
    n_used = (pad_end[-1] // EBLK).astype(I32).reshape(1)
    dest2 = dest.reshape(N_TOK, TOP_K)
    return dest2[:, 0].reshape(1, N_TOK), dest2[:, 1].reshape(1, N_TOK), blk_start, n_blk, n_used


def _sc_mesh():
    return plsc.VectorSubcoreMesh(core_axis_name="core", subcore_axis_name="subcore",
                                  num_cores=V7X_SC_CORES, num_subcores=V7X_SC_SUBCORES)


def _sc_worker():
    return lax.axis_index("core") * V7X_SC_SUBCORES + lax.axis_index("subcore")


def _sc_dispatch(x_slabs, d0, d1):
    per = N_TOK // SC_WORKERS

    @functools.partial(
        pl.kernel, out_type=jax.ShapeDtypeStruct((P_SLOTS, SLAB, LANE), I32), mesh=_sc_mesh(), name="moe_dispatch",
        scratch_types=[pltpu.VMEM((1, per), I32), pltpu.VMEM((1, per), I32), pltpu.VMEM((SC_WINDOW, SLAB, LANE), I32)])
    def run(x_hbm, d0_hbm, d1_hbm, o_hbm, i0_v, i1_v, buf):
        base = _sc_worker() * per
        pltpu.sync_copy(d0_hbm.at[:, pl.ds(base, per)], i0_v)
        pltpu.sync_copy(d1_hbm.at[:, pl.ds(base, per)], i1_v)

        @pl.loop(0, per // SC_WINDOW)
        def _(s):
            off = s * SC_WINDOW
            pltpu.sync_copy(x_hbm.at[pl.ds(base + off, SC_WINDOW)], buf)
            pltpu.sync_copy(buf, o_hbm.at[i0_v.at[0, pl.ds(off, SC_WINDOW)]])
            pltpu.sync_copy(buf, o_hbm.at[i1_v.at[0, pl.ds(off, SC_WINDOW)]])

    return run(x_slabs.reshape(N_TOK, SLAB, LANE), d0, d1)


def _sc_collect(y_slabs, dcat):
    per = N_ASSIGN // SC_WORKERS

    @functools.partial(
        pl.kernel, out_type=jax.ShapeDtypeStruct((N_ASSIGN, SLAB, LANE), I32), mesh=_sc_mesh(), name="moe_collect",
        scratch_types=[pltpu.VMEM((1, per), I32), pltpu.VMEM((SC_WINDOW, SLAB, LANE), I32)])
    def run(y_hbm, i_hbm, o_hbm, i_v, buf):
        base = _sc_worker() * per
        pltpu.sync_copy(i_hbm.at[:, pl.ds(base, per)], i_v)

        @pl.loop(0, per // SC_WINDOW)
        def _(s):
            off = s * SC_WINDOW
            pltpu.sync_copy(y_hbm.at[i_v.at[0, pl.ds(off, SC_WINDOW)]], buf)
            pltpu.sync_copy(buf, o_hbm.at[pl.ds(base + off, SC_WINDOW)])

    return run(y_slabs.reshape(P_SLOTS, SLAB, LANE), dcat)


EROWS = EBLK * SLAB


def _expert_kernel(bs_ref, nb_ref, nu_ref, wg_ref, wu_ref, wd_ref, x_hbm, y_hbm,
                   xbuf, ybuf, xs, wg_bf, wu_bf, wd_bf, isem, osem):
    e = pl.program_id(0)
    n_used = nu_ref[0]
    b0 = bs_ref[e]
    nb = nb_ref[e]

    def in_copy(g, slot):
        return pltpu.make_async_copy(x_hbm.at[pl.ds(pl.multiple_of(g * EROWS, EROWS), EROWS)], xbuf.at[slot],
                                     isem.at[slot])

    def out_copy(g, slot):
        return pltpu.make_async_copy(ybuf.at[slot], y_hbm.at[pl.ds(pl.multiple_of(g * EROWS, EROWS), EROWS)],
                                     osem.at[slot])

    @pl.when(e == 0)
    def _():
        in_copy(0, 0).start()

    @pl.when(nb > 0)
    def _():
        wg_bf[...] = wg_ref[0, 0].astype(BF16)
        wu_bf[...] = wu_ref[0, 0].astype(BF16)
        wd_bf[...] = wd_ref[0, 0].astype(BF16)

    def block(k, carry):
        g = b0 + k
        slot = lax.rem(g, 2)
        in_copy(g, slot).wait()

        @pl.when(g + 1 < n_used)
        def _():
            in_copy(g + 1, 1 - slot).start()

        _load_slabs(xbuf.at[slot], xs, EBLK, BF16)
        xb = xs[...]
        gt = jnp.dot(xb, wg_bf[...], preferred_element_type=F32)
        up = jnp.dot(xb, wu_bf[...], preferred_element_type=F32)
        hmid = (gt * _sigmoid(gt) * up).astype(BF16)
        res = jnp.dot(hmid, wd_bf[...], preferred_element_type=F32)

        @pl.when(g >= 2)
        def _():
            out_copy(g - 2, slot).wait()

        _store_slabs(ybuf.at[slot], res)
        out_copy(g, slot).start()
        return carry

    lax.fori_loop(0, nb, block, 0)

    @pl.when(e == pl.num_programs(0) - 1)
    def _():
        last = n_used - 1
        out_copy(last, lax.rem(last, 2)).wait()

        @pl.when(n_used >= 2)
        def _():
            out_copy(last - 1, lax.rem(last - 1, 2)).wait()


def _experts(x_sorted, blk_start, n_blk, n_used, w_gate, w_up, w_down, layer):
    wspec = lambda r, c: pl.BlockSpec((1, 1, r, c), lambda e, *_: (layer, e, 0, 0))
    return pl.pallas_call(
        _expert_kernel,
        out_shape=jax.ShapeDtypeStruct((P_SLOTS * SLAB, LANE), I32),
        grid_spec=pltpu.PrefetchScalarGridSpec(
            num_scalar_prefetch=3, grid=(N_EXPERTS,),
            in_specs=[wspec(D, D_EXPERT), wspec(D, D_EXPERT), wspec(D_EXPERT, D), pl.BlockSpec(memory_space=pl.ANY)],
            out_specs=pl.BlockSpec(memory_space=pl.ANY),
            scratch_shapes=[
                pltpu.VMEM((2, EROWS, LANE), I32), pltpu.VMEM((2, EROWS, LANE), I32),
                pltpu.VMEM((EBLK, D), BF16),
                pltpu.VMEM((D, D_EXPERT), BF16), pltpu.VMEM((D, D_EXPERT), BF16), pltpu.VMEM((D_EXPERT, D), BF16),
                pltpu.SemaphoreType.DMA((2,)), pltpu.SemaphoreType.DMA((2,)),
            ]),
        compiler_params=_cparams(),
        name="moe_experts",
    )(blk_start, n_blk, n_used, w_gate, w_up, w_down, x_sorted.reshape(P_SLOTS * SLAB, LANE))


def _combine_kernel(final, mr_ref, e0_ref, e1_ref, wt_ref, y_ref, mod_ref, fg_ref, o_ref, a_scr, b_scr):
    _load_slabs(e0_ref, a_scr, TM, F32)
    _load_slabs(e1_ref, b_scr, TM, F32)
    wt = wt_ref[...]
    moe = wt[:, 0:1] * a_scr[...] + wt[:, 1:2] * b_scr[...]
    y_new = y_ref[...] + mod_ref[0][5:6] * moe
    o_ref[...] = _rms(y_new, fg_ref[...]) if final else y_new


def _combine(ym, wts, y, mods, mrow, final_g, blk0, nblk, final):
    tok = lambda width: pl.BlockSpec((TM, width), lambda j, *_: (blk0 + j, 0))
    slab0 = pl.BlockSpec((TM * SLAB, LANE), lambda j, *_: (blk0 + j, 0))
    slab1 = pl.BlockSpec((TM * SLAB, LANE), lambda j, *_: (NB + blk0 + j, 0))
    mod = pl.BlockSpec((1, 6, D), lambda j, mr: (mr[blk0 + j], 0, 0))
    return pl.pallas_call(
        functools.partial(_combine_kernel, final),
        out_shape=jax.ShapeDtypeStruct((nblk * TM, D), F32),
        grid_spec=pltpu.PrefetchScalarGridSpec(
            num_scalar_prefetch=1, grid=(nblk,),
            in_specs=[slab0, slab1, tok(TOP_K), tok(D), mod, _full_spec((1, D))],
            out_specs=pl.BlockSpec((TM, D), lambda j, *_: (j, 0)),
            scratch_shapes=[pltpu.VMEM((TM, D), F32), pltpu.VMEM((TM, D), F32)]),
        compiler_params=_cparams(),
        name="moe_combine",
    )(mrow, ym, ym, wts, y, mods, final_g)


def kernel(x_prompt, x_sample, cache_attn_k, cache_attn_v, state_mlstm_C, state_mlstm_n, state_mlstm_m, c, c_ctx, ada_w, ada_b, norm1_g, norm2_g, conv_w_in, conv_w_dw, conv_b_dw, conv_ln_g, conv_ln_b, conv_w_out, attn_w_qkv, attn_q_norm, attn_k_norm, attn_w_o, mlstm_w_in, mlstm_b_gate, mlstm_norm_g, mlstm_w_out, moe_w_group, moe_b_group, moe_w_router, moe_b_router, moe_w_gate, moe_w_up, moe_w_down, final_norm_g):
    y = jnp.concatenate([x_prompt.reshape(NP_TOK, D), x_sample.reshape(NS_TOK, D)], axis=0)
    cvec = jnp.concatenate([c_ctx[None, :], c, jnp.zeros((MOD_ROWS - 1 - DEC_BATCH, D), F32)], axis=0)
    mods = _ada_all(cvec, ada_w, ada_b)
    rope = _rope_blocks()
    new_k = new_v = new_c = new_n = new_m = None
    for i in range(DEPTH):
        kind, slot = i % 3, i // 3
        mrow = jnp.asarray(_MOD_ROW + i * MOD_ROWS)
        g1 = norm1_g[i].reshape(1, D)
        if kind == 0:
            u = _conv_in(y, mods, mrow, g1, conv_w_in[slot].astype(BF16))
            w_dw = jnp.concatenate([conv_w_dw[slot], jnp.zeros((1, D), F32)], axis=0)
            y = _conv_main(u, y, mods, mrow, w_dw, conv_b_dw[slot].reshape(1, D), conv_ln_g[slot].reshape(1, D),
                           conv_ln_b[slot].reshape(1, D), conv_w_out[slot].astype(BF16))
        elif kind == 1:
            q, kb, vb, kf, vf = _attn_qkv(y, mods, mrow, g1, attn_w_qkv[slot].astype(BF16),
                                          attn_q_norm[slot].reshape(1, HEAD_DIM), attn_k_norm[slot].reshape(1, HEAD_DIM),
                                          rope)
            new_k = kf[:NP_TOK].reshape(BATCH, 1, SEQ, N_KV_HEADS, HEAD_DIM)
            new_v = vf[:NP_TOK].reshape(BATCH, 1, SEQ, N_KV_HEADS, HEAD_DIM)
            ck = cache_attn_k[:, slot].reshape(DEC_BATCH, PAST_LEN, KV_DIM)
            cv = cache_attn_v[:, slot].reshape(DEC_BATCH, PAST_LEN, KV_DIM)
            y = _attention(q, kb, vb, ck, cv, attn_w_o[slot].astype(BF16), y, mods, i)
        else:
            w_in = mlstm_w_in[slot]
            w_gate = jnp.concatenate([w_in[:, 4 * D:], jnp.zeros((D, LANE - 4 * M_HEADS), F32)], axis=1)
            b_gate = jnp.concatenate([mlstm_b_gate[slot], jnp.zeros((LANE - 4 * M_HEADS,), F32)]).reshape(1, LANE)
            q, k, v, o, gates = _mlstm_in(y, mods, mrow, g1, w_in[:, :4 * D].astype(BF16), _split_hi_lo(w_gate), b_gate)
            sc = state_mlstm_C[:, slot]
            sn = state_mlstm_n[:, slot].reshape(DEC_BATCH, 2, M_HEADS, 1, M_HEAD_DIM)
            sm = state_mlstm_m[:, slot].reshape(DEC_BATCH, 2, M_HEADS, 1, 1)
            hsum, nc_, nn_, nm_ = _mlstm_scan(q, k, v, gates, sc, sn, sm)
            new_c = nc_[:, None]
            new_n = nn_.reshape(BATCH, 1, 2, M_HEADS, M_HEAD_DIM)
            new_m = nm_[..., 0, 0].reshape(BATCH, 1, 2, M_HEADS)
            y = _mlstm_out(hsum, o, mlstm_norm_g[slot].reshape(1, D), mlstm_w_out[slot].astype(BF16), y, mods, mrow)
        w_route = jnp.concatenate([moe_w_group[i], moe_w_router[i],
                                   jnp.zeros((D, LANE - N_GROUPS - N_EXPERTS), F32)], axis=1)
        b_route = jnp.concatenate([moe_b_group[i], moe_b_router[i],
                                   jnp.zeros((LANE - N_GROUPS - N_EXPERTS,), F32)]).reshape(1, LANE)
        x2, eid, ewt = _route(y, mods, mrow, norm2_g[i].reshape(1, D), _split_hi_lo(w_route), b_route)
        d0, d1, blk_start, n_blk, n_used = _dispatch_tables(eid)
        x_sorted = _sc_dispatch(x2, d0, d1)
        y_sorted = _experts(x_sorted, blk_start, n_blk, n_used, moe_w_gate, moe_w_up, moe_w_down, i)
        ym = _sc_collect(y_sorted, jnp.concatenate([d0, d1], axis=1))
        ym = ym.reshape(N_ASSIGN * SLAB, LANE)
        fg = final_norm_g.reshape(1, D)
        if i + 1 < DEPTH:
            y = _combine(ym, ewt, y, mods, mrow, fg, 0, NB, False)
        else:
            y_prompt = _combine(ym, ewt, y, mods, mrow, fg, 0, NBP, True).reshape(BATCH, SEQ, D)
            y_sample = _combine(ym, ewt, y, mods, mrow, fg, NBP, NB - NBP, True).reshape(DEC_BATCH, DEC_SEQ, D)
    return (y_prompt, y_sample, new_k, new_v, new_c, new_n, new_m)
```

```python
import functools

import jax
import jax.numpy as jnp
import numpy as np
from jax import lax
from jax.experimental import pallas as pl
from jax.experimental.pallas import tpu as pltpu
from jax.experimental.pallas import tpu_sc as plsc

F32 = jnp.float32
BF16 = jnp.bfloat16
I32 = jnp.int32

D = 1024
BATCH, SEQ = 16, 256
DEC_BATCH, DEC_SEQ = 8, 1024
PAST_LEN = 256
DEPTH = 4
GRID_W = 64
EPS = 1e-6
CONV_WIDTH = 31
CONV_PAD = CONV_WIDTH // 2
HEAD_DIM = 128
N_HEADS = 8
N_KV_HEADS = 2
GQA_GROUP = N_HEADS // N_KV_HEADS
Q_DIM = N_HEADS * HEAD_DIM
KV_DIM = N_KV_HEADS * HEAD_DIM
QKV_DIM = Q_DIM + 2 * KV_DIM
ROPE_THETA = 10000.0
M_HEADS = 4
M_HEAD_DIM = D // M_HEADS
M_CHUNK = 64
N_GROUPS = 4
EXPERTS_PER_GROUP = 8
N_EXPERTS = N_GROUPS * EXPERTS_PER_GROUP
TOP_K = 2
D_EXPERT = 512

NP_TOK = BATCH * SEQ
NS_TOK = DEC_BATCH * DEC_SEQ
N_TOK = NP_TOK + NS_TOK
TM = 256
NB = N_TOK // TM
NBP = NP_TOK // TM
BLK_PER_DEC = DEC_SEQ // TM
MOD_ROWS = 16
HALO = 16
LANE = 128
SUBLANE = 8

N_ASSIGN = N_TOK * TOP_K
EBLK = 256
N_EBLK = N_ASSIGN // EBLK + N_EXPERTS
P_SLOTS = N_EBLK * EBLK
N_PAD_SLOTS = P_SLOTS - N_ASSIGN

VMEM_LIMIT = 56 * 1024 * 1024


def _block_tables():
    j = np.arange(NB)
    is_p = j < NBP
    mod_row = np.where(is_p, 0, 1 + (j - NBP) // BLK_PER_DEC)
    rope_idx = np.where(is_p, 0, 1 + (j - NBP) % BLK_PER_DEC)
    first = np.where(is_p, 1, ((j - NBP) % BLK_PER_DEC == 0).astype(np.int64))
    last = np.where(is_p, 1, ((j - NBP) % BLK_PER_DEC == BLK_PER_DEC - 1).astype(np.int64))
    return (mod_row.astype(np.int32), rope_idx.astype(np.int32), first.astype(np.int32), last.astype(np.int32))


_MOD_ROW, _ROPE_IDX, _SEQ_FIRST, _SEQ_LAST = _block_tables()


def _cparams(n_axes=1):
    return pltpu.CompilerParams(dimension_semantics=("arbitrary",) * n_axes, vmem_limit_bytes=VMEM_LIMIT)


def _sigmoid(x):
    return 1.0 / (1.0 + jnp.exp(-x))


def _rms(x, g):
    return x * lax.rsqrt(jnp.mean(x * x, axis=-1, keepdims=True) + EPS) * g


def _split_hi_lo(w):
    hi = w.astype(BF16)
    return jnp.stack([hi, (w - hi.astype(F32)).astype(BF16)], axis=0)


def _dot_hi_lo(x, w_ref):
    xh = x.astype(BF16)
    xl = (x - xh.astype(F32)).astype(BF16)
    wh, wl = w_ref[0], w_ref[1]
    return (jnp.dot(xh, wh, preferred_element_type=F32)
            + (jnp.dot(xh, wl, preferred_element_type=F32) + jnp.dot(xl, wh, preferred_element_type=F32)))


def _norm_mod(y, g, mod, which):
    shift = mod[3 * which:3 * which + 1]
    scale = mod[3 * which + 1:3 * which + 2]
    return _rms(y, g) * (1.0 + scale) + shift


def _ada_kernel(c_ref, w_ref, b_ref, o_ref):
    c = c_ref[...]
    s = c * _sigmoid(c)
    o_ref[0] = jnp.dot(s.astype(BF16), w_ref[0].astype(BF16), preferred_element_type=F32) + b_ref[0]


def _ada_all(cvec, ada_w, ada_b):
    tn = 1536
    out = pl.pallas_call(
        _ada_kernel,
        out_shape=jax.ShapeDtypeStruct((DEPTH, MOD_ROWS, 6 * D), F32),
        grid=(DEPTH, 6 * D // tn),
        in_specs=[
            pl.BlockSpec((MOD_ROWS, D), lambda l, n: (0, 0)),
            pl.BlockSpec((1, D, tn), lambda l, n: (l, 0, n)),
            pl.BlockSpec((1, 1, tn), lambda l, n: (l, 0, n)),
        ],
        out_specs=pl.BlockSpec((1, MOD_ROWS, tn), lambda l, n: (l, 0, n)),
        compiler_params=_cparams(2),
        name="ada_mod",
    )(cvec, ada_w, ada_b.reshape(DEPTH, 1, 6 * D))
    return out.reshape(DEPTH * MOD_ROWS, 6, D)


def _tok_spec(width):
    return pl.BlockSpec((TM, width), lambda j, *_: (j, 0))


def _mod_spec():
    return pl.BlockSpec((1, 6, D), lambda j, mr, *_: (mr[j], 0, 0))


def _full_spec(shape):
    nd = len(shape)
    return pl.BlockSpec(shape, lambda j, *_: (0,) * nd)


def _conv_in_kernel(mr_ref, y_ref, mod_ref, g_ref, w_ref, u_ref):
    h = _norm_mod(y_ref[...], g_ref[...], mod_ref[0], 0)
    ag = jnp.dot(h.astype(BF16), w_ref[...], preferred_element_type=F32)
    u_ref[...] = ag[:, :D] * _sigmoid(ag[:, D:])


def _conv_in(y, mods, mrow, g1, w_in):
    return pl.pallas_call(
        _conv_in_kernel,
        out_shape=jax.ShapeDtypeStruct((N_TOK, D), F32),
        grid_spec=pltpu.PrefetchScalarGridSpec(
            num_scalar_prefetch=1, grid=(NB,),
            in_specs=[_tok_spec(D), _mod_spec(), _full_spec((1, D)), _full_spec((D, 2 * D))],
            out_specs=_tok_spec(D)),
        compiler_params=_cparams(),
        name="conv_in",
    )(mrow, y, mods, g1, w_in)


def _conv_main_kernel(mr_ref, first_ref, last_ref, u_ref, up_ref, un_ref, wdw_ref, bdw_ref, lg_ref, lb_ref,
                      wout_ref, y_ref, mod_ref, o_ref, ext_ref, acc_ref):
    j = pl.program_id(0)
    zero = jnp.zeros((HALO, D), F32)
    ext_ref[0:HALO, :] = jnp.where(first_ref[j] == 1, zero, up_ref[...])
    ext_ref[HALO:HALO + TM, :] = u_ref[...]
    ext_ref[HALO + TM:2 * HALO + TM, :] = jnp.where(last_ref[j] == 1, zero, un_ref[...])

    off0 = HALO - CONV_PAD
    n_a = (off0 + CONV_WIDTH - 1) // SUBLANE + 1
    n_chunks = TM // SUBLANE

    def strip(ci, carry):
        cs = pl.ds(pl.multiple_of(ci * LANE, LANE), LANE)
        wk = [jnp.broadcast_to(wdw_ref[k:k + 1, cs], (SUBLANE, LANE)) for k in range(CONV_WIDTH)]
        bias = jnp.broadcast_to(bdw_ref[:, cs], (SUBLANE, LANE))
        sub = lax.broadcasted_iota(I32, (SUBLANE, LANE), 0)
        prev_rot, prev_v0 = None, None
        for j in range(n_chunks + 1):
            tiles = [ext_ref[SUBLANE * (j + a):SUBLANE * (j + a + 1), cs] for a in range(n_a)]
            part = []
            for s in range(SUBLANE):
                acc = None
                for a in range(n_a):
                    k = SUBLANE * a + s - off0
                    if (0 <= k < CONV_WIDTH) and not (s == 0 and j == n_chunks):
                        term = tiles[a] * wk[k]
                        acc = term if acc is None else acc + term
                part.append(acc)
            rot = [None] + [pltpu.roll(part[s], SUBLANE - s, 0) for s in range(1, SUBLANE)]
            if j >= 1:
                out = prev_v0 + bias
                for s in range(1, SUBLANE):
                    out = out + jnp.where(sub < SUBLANE - s, prev_rot[s], rot[s])
                acc_ref[SUBLANE * (j - 1):SUBLANE * j, cs] = out
            prev_rot, prev_v0 = rot, part[0]
        return carry

    lax.fori_loop(0, D // LANE, strip, 0)

    c = acc_ref[...]
    mu = jnp.mean(c, axis=-1, keepdims=True)
    cc = c - mu
    var = jnp.mean(cc * cc, axis=-1, keepdims=True)
    z = cc * lax.rsqrt(var + EPS) * lg_ref[...] + lb_ref[...]
    z = z * _sigmoid(z)
    out = jnp.dot(z.astype(BF16), wout_ref[...], preferred_element_type=F32)
    o_ref[...] = y_ref[...] + mod_ref[0][2:3] * out


def _conv_main(u, y, mods, mrow, w_dw, b_dw, ln_g, ln_b, w_out):
    nh = N_TOK // HALO
    per = TM // HALO
    return pl.pallas_call(
        _conv_main_kernel,
        out_shape=jax.ShapeDtypeStruct((N_TOK, D), F32),
        grid_spec=pltpu.PrefetchScalarGridSpec(
            num_scalar_prefetch=3, grid=(NB,),
            in_specs=[
                _tok_spec(D),
                pl.BlockSpec((HALO, D), lambda j, *_: (jnp.maximum(j * per - 1, 0), 0)),
                pl.BlockSpec((HALO, D), lambda j, *_: (jnp.minimum((j + 1) * per, nh - 1), 0)),
                _full_spec((CONV_WIDTH + 1, D)), _full_spec((1, D)), _full_spec((1, D)), _full_spec((1, D)),
                _full_spec((D, D)), _tok_spec(D), _mod_spec(),
            ],
            out_specs=_tok_spec(D),
            scratch_shapes=[pltpu.VMEM((TM + 2 * HALO, D), F32), pltpu.VMEM((TM, D), F32)]),
        compiler_params=_cparams(),
        name="conv_main",
    )(mrow, jnp.asarray(_SEQ_FIRST), jnp.asarray(_SEQ_LAST), u, u, u, w_dw, b_dw, ln_g, ln_b, w_out, y, mods)


def _rope_angles():
    rows = DEC_SEQ // GRID_W
    row = jnp.repeat(jnp.arange(rows, dtype=F32), GRID_W)
    col = jnp.tile(jnp.arange(GRID_W, dtype=F32), rows)
    axis_dim = HEAD_DIM // 2
    freqs = jnp.power(ROPE_THETA, -jnp.arange(axis_dim // 2, dtype=F32) * 2.0 / axis_dim)
    ang_r = row[:, None] * freqs[None, :]
    ang_c = col[:, None] * freqs[None, :]
    return jnp.concatenate([ang_r, ang_r, ang_c, ang_c], axis=-1)


def _rope_blocks():
    ang = _rope_angles()
    cos, sin = jnp.cos(ang), jnp.sin(ang)
    lane = np.arange(HEAD_DIM)
    lo = jnp.asarray(((lane % (HEAD_DIM // 2)) < HEAD_DIM // 4).astype(np.float32))
    sin_a = -sin * lo[None, :]
    sin_b = sin * (1.0 - lo)[None, :]
    nblk = DEC_SEQ // TM
    ident = jnp.ones((1, TM, HEAD_DIM), F32)
    zeros = jnp.zeros((1, TM, HEAD_DIM), F32)
    cos_t = jnp.concatenate([ident, cos.reshape(nblk, TM, HEAD_DIM)], axis=0)
    sa_t = jnp.concatenate([zeros, sin_a.reshape(nblk, TM, HEAD_DIM)], axis=0)
    sb_t = jnp.concatenate([zeros, sin_b.reshape(nblk, TM, HEAD_DIM)], axis=0)
    return cos_t, sa_t, sb_t


def _attn_qkv_kernel(mr_ref, ri_ref, y_ref, mod_ref, g_ref, w_ref, qg_ref, kg_ref, cos_ref, sa_ref, sb_ref,
                     q_ref, kb_ref, vb_ref, kf_ref, vf_ref):
    h = _norm_mod(y_ref[...], g_ref[...], mod_ref[0], 0)
    qkv = jnp.dot(h.astype(BF16), w_ref[...], preferred_element_type=F32)
    cos, sa, sb = cos_ref[0], sa_ref[0], sb_ref[0]
    quarter = HEAD_DIM // 4

    def head(x, g):
        xn = _rms(x, g)
        return xn * cos + pltpu.roll(xn, HEAD_DIM - quarter, 1) * sa + pltpu.roll(xn, quarter, 1) * sb

    scale = HEAD_DIM ** -0.5
    for hd in range(N_HEADS):
        sl = slice(hd * HEAD_DIM, (hd + 1) * HEAD_DIM)
        q_ref[:, sl] = (head(qkv[:, sl], qg_ref[...]) * scale).astype(BF16)
    for kv in range(N_KV_HEADS):
        sl = slice(kv * HEAD_DIM, (kv + 1) * HEAD_DIM)
        kr = head(qkv[:, Q_DIM + kv * HEAD_DIM:Q_DIM + (kv + 1) * HEAD_DIM], kg_ref[...])
        kf_ref[:, sl] = kr
        kb_ref[:, sl] = kr.astype(BF16)
    v = qkv[:, Q_DIM + KV_DIM:]
    vf_ref[...] = v
    vb_ref[...] = v.astype(BF16)


def _attn_qkv(y, mods, mrow, g1, w_qkv, q_g, k_g, rope):
    cos_t, sa_t, sb_t = rope
    rspec = pl.BlockSpec((1, TM, HEAD_DIM), lambda j, mr, ri: (ri[j], 0, 0))
    return pl.pallas_call(
        _attn_qkv_kernel,
        out_shape=(jax.ShapeDtypeStruct((N_TOK, Q_DIM), BF16), jax.ShapeDtypeStruct((N_TOK, KV_DIM), BF16),
                   jax.ShapeDtypeStruct((N_TOK, KV_DIM), BF16), jax.ShapeDtypeStruct((N_TOK, KV_DIM), F32),
                   jax.ShapeDtypeStruct((N_TOK, KV_DIM), F32)),
        grid_spec=pltpu.PrefetchScalarGridSpec(
            num_scalar_prefetch=2, grid=(NB,),
            in_specs=[_tok_spec(D), _mod_spec(), _full_spec((1, D)), _full_spec((D, QKV_DIM)),
                      _full_spec((1, HEAD_DIM)), _full_spec((1, HEAD_DIM)), rspec, rspec, rspec],
            out_specs=(_tok_spec(Q_DIM), _tok_spec(KV_DIM), _tok_spec(KV_DIM), _tok_spec(KV_DIM), _tok_spec(KV_DIM))),
        compiler_params=_cparams(),
        name="attn_qkv",
    )(mrow, jnp.asarray(_ROPE_IDX), y, mods, g1, w_qkv, q_g, k_g, cos_t, sa_t, sb_t)


def _attn_heads(q, ks, vs, o_scr):
    nt = (((1,), (1,)), ((), ()))
    for hd in range(N_HEADS):
        g = hd // GQA_GROUP
        qh = q[:, hd * HEAD_DIM:(hd + 1) * HEAD_DIM]
        gs = slice(g * HEAD_DIM, (g + 1) * HEAD_DIM)
        ss = [lax.dot_general(qh, k[:, gs], nt, preferred_element_type=F32) for k in ks]
        m = functools.reduce(jnp.maximum, [jnp.max(s, axis=-1, keepdims=True) for s in ss])
        ps = [jnp.exp(s - m) for s in ss]
        l = functools.reduce(lambda a, b: a + b, [jnp.sum(p, axis=-1, keepdims=True) for p in ps])
        o = functools.reduce(lambda a, b: a + b,
                             [jnp.dot(p.astype(BF16), v[:, gs], preferred_element_type=F32) for p, v in zip(ps, vs)])
        o_scr[:, hd * HEAD_DIM:(hd + 1) * HEAD_DIM] = (o / l).astype(BF16)


def _attn_ctx_kernel(q_ref, k_ref, v_ref, wo_ref, y_ref, mod_ref, o_ref, o_scr):
    _attn_heads(q_ref[...], [k_ref[...]], [v_ref[...]], o_scr)
    out = jnp.dot(o_scr[...], wo_ref[...], preferred_element_type=F32)
    o_ref[...] = y_ref[...] + mod_ref[0][2:3] * out


def _attn_lat_kernel(q_ref, k_ref, v_ref, ck_ref, cv_ref, wo_ref, y_ref, mod_ref, ctx_out_ref, o_ref, o_scr):
    del ctx_out_ref
    _attn_heads(q_ref[...], [k_ref[...], ck_ref[0].astype(BF16)], [v_ref[...], cv_ref[0].astype(BF16)], o_scr)
    out = jnp.dot(o_scr[...], wo_ref[...], preferred_element_type=F32)
    o_ref[...] = y_ref[...] + mod_ref[0][2:3] * out


def _attention(q, kb, vb, cache_k, cache_v, w_o, y, mods, layer):
    y_ctx = pl.pallas_call(
        _attn_ctx_kernel,
        out_shape=jax.ShapeDtypeStruct((N_TOK, D), F32),
        grid=(BATCH,),
        in_specs=[
            pl.BlockSpec((SEQ, Q_DIM), lambda s: (s, 0)),
            pl.BlockSpec((SEQ, KV_DIM), lambda s: (s, 0)),
            pl.BlockSpec((SEQ, KV_DIM), lambda s: (s, 0)),
            pl.BlockSpec((Q_DIM, D), lambda s: (0, 0)),
            pl.BlockSpec((SEQ, D), lambda s: (s, 0)),
            pl.BlockSpec((1, 6, D), lambda s: (layer * MOD_ROWS, 0, 0)),
        ],
        out_specs=pl.BlockSpec((SEQ, D), lambda s: (s, 0)),
        scratch_shapes=[pltpu.VMEM((SEQ, Q_DIM), BF16)],
        compiler_params=_cparams(),
        name="attn_ctx",
    )(q, kb, vb, w_o, y, mods)
    pb = NP_TOK // DEC_SEQ
    return pl.pallas_call(
        _attn_lat_kernel,
        out_shape=jax.ShapeDtypeStruct((N_TOK, D), F32),
        input_output_aliases={8: 0},
        grid=(DEC_BATCH, BLK_PER_DEC),
        in_specs=[
            pl.BlockSpec((TM, Q_DIM), lambda b, t: (NBP + b * BLK_PER_DEC + t, 0)),
            pl.BlockSpec((DEC_SEQ, KV_DIM), lambda b, t: (pb + b, 0)),
            pl.BlockSpec((DEC_SEQ, KV_DIM), lambda b, t: (pb + b, 0)),
            pl.BlockSpec((1, PAST_LEN, KV_DIM), lambda b, t: (b, 0, 0)),
            pl.BlockSpec((1, PAST_LEN, KV_DIM), lambda b, t: (b, 0, 0)),
            pl.BlockSpec((Q_DIM, D), lambda b, t: (0, 0)),
            pl.BlockSpec((TM, D), lambda b, t: (NBP + b * BLK_PER_DEC + t, 0)),
            pl.BlockSpec((1, 6, D), lambda b, t: (layer * MOD_ROWS + 1 + b, 0, 0)),
            pl.BlockSpec(memory_space=pl.ANY),
        ],
        out_specs=pl.BlockSpec((TM, D), lambda b, t: (NBP + b * BLK_PER_DEC + t, 0)),
        scratch_shapes=[pltpu.VMEM((TM, Q_DIM), BF16)],
        compiler_params=_cparams(2),
        name="attn_lat",
    )(q, kb, vb, cache_k, cache_v, w_o, y, mods, y_ctx)


def _log_sigmoid(x):
    return jnp.minimum(x, 0.0) - jnp.log(1.0 + jnp.exp(-jnp.abs(x)))


def _mlstm_in_kernel(mr_ref, y_ref, mod_ref, g_ref, w_ref, wg_ref, bg_ref, q_ref, k_ref, v_ref, o_ref, gt_ref):
    h = _norm_mod(y_ref[...], g_ref[...], mod_ref[0], 0)
    hb = h.astype(BF16)
    q_ref[...] = jnp.dot(hb, w_ref[:, 0:D], preferred_element_type=F32).astype(BF16)
    k_ref[...] = (jnp.dot(hb, w_ref[:, D:2 * D], preferred_element_type=F32) * (M_HEAD_DIM ** -0.5)).astype(BF16)
    v_ref[...] = jnp.dot(hb, w_ref[:, 2 * D:3 * D], preferred_element_type=F32).astype(BF16)
    o_ref[...] = _sigmoid(jnp.dot(hb, w_ref[:, 3 * D:4 * D], preferred_element_type=F32))
    gt = _dot_hi_lo(h, wg_ref) + bg_ref[...]
    lane = lax.broadcasted_iota(I32, gt.shape, 1)
    is_f = ((lane >= M_HEADS) & (lane < 2 * M_HEADS)) | ((lane >= 3 * M_HEADS) & (lane < 4 * M_HEADS))
    gt_ref[...] = jnp.where(is_f, _log_sigmoid(gt), gt)


def _mlstm_in(y, mods, mrow, g1, w_main, w_gate, b_gate):
    return pl.pallas_call(
        _mlstm_in_kernel,
        out_shape=(jax.ShapeDtypeStruct((N_TOK, D), BF16), jax.ShapeDtypeStruct((N_TOK, D), BF16),
                   jax.ShapeDtypeStruct((N_TOK, D), BF16), jax.ShapeDtypeStruct((N_TOK, D), F32),
                   jax.ShapeDtypeStruct((N_TOK, LANE), F32)),
        grid_spec=pltpu.PrefetchScalarGridSpec(
            num_scalar_prefetch=1, grid=(NB,),
            in_specs=[_tok_spec(D), _mod_spec(), _full_spec((1, D)), _full_spec((D, 4 * D)),
                      _full_spec((2, D, LANE)), _full_spec((1, LANE))],
            out_specs=(_tok_spec(D), _tok_spec(D), _tok_spec(D), _tok_spec(D), _tok_spec(LANE))),
        compiler_params=_cparams(),
        name="mlstm_in",
    )(mrow, y, mods, g1, w_main, w_gate, b_gate)


def _mlstm_load(hd, c, q_ref, k_ref, v_ref, gc_ref, gr_ref):
    r0 = pl.multiple_of(c * M_CHUNK, M_CHUNK)
    hs = slice(hd * M_HEAD_DIM, (hd + 1) * M_HEAD_DIM)
    rows = pl.ds(r0, M_CHUNK)
    return rows, hs, q_ref[rows, hs], k_ref[rows, hs], v_ref[rows, hs], gc_ref[hd, rows, :], gr_ref[hd, c]


def _mlstm_chunks(chains, ms, loaded, c_scr, n_scr):
    L = M_CHUNK
    n = range(len(chains))
    t_idx = lax.broadcasted_iota(I32, (L, L), 0)
    s_idx = lax.broadcasted_iota(I32, (L, L), 1)
    masks = {0: (s_idx <= t_idx, t_idx <= s_idx), 1: (s_idx >= t_idx, t_idx >= s_idx)}
    q = [ld[2] for ld in loaded]
    k = [ld[3] for ld in loaded]
    v = [ld[4] for ld in loaded]
    i_col = [ld[5][:, 2 * d:2 * d + 1] for (_, d), ld in zip(chains, loaded)]
    lf_col = [ld[5][:, 2 * d + 1:2 * d + 2] for (_, d), ld in zip(chains, loaded)]
    i_row = [ld[6][2 * d:2 * d + 1, :] for (_, d), ld in zip(chains, loaded)]
    lf_row = [ld[6][2 * d + 1:2 * d + 2, :] for (_, d), ld in zip(chains, loaded)]
    mask = [masks[d][0] for _, d in chains]
    mask_t = [masks[d][1] for _, d in chains]
    b_col = [jnp.sum(jnp.where(mask[i], lf_row[i], 0.0), axis=1, keepdims=True) for i in n]
    b_row = [jnp.sum(jnp.where(mask_t[i], lf_col[i], 0.0), axis=0, keepdims=True) for i in n]
    log_d = [jnp.where(mask[i], b_col[i] - b_row[i] + i_row[i], -jnp.inf) for i in n]
    li = [b_col[i] + ms[i] for i in n]
    m_r = [jnp.maximum(li[i], jnp.max(log_d[i], axis=1, keepdims=True)) for i in n]
    a_int = [jnp.exp(li[i] - m_r[i]) for i in n]
    dmat = [jnp.exp(log_d[i] - m_r[i]) for i in n]
    cmat = [c_scr[d, hd] for hd, d in chains]
    nvec = [n_scr[d, hd] for hd, d in chains]
    gram = [lax.dot_general(q[i], k[i], (((1,), (1,)), ((), ())), preferred_element_type=F32) for i in n]
    inter = [jnp.dot(q[i], cmat[i].astype(BF16), preferred_element_type=F32) for i in n]
    s = [gram[i] * dmat[i] for i in n]
    intra = [jnp.dot(s[i].astype(BF16), v[i], preferred_element_type=F32) for i in n]
    qn = [jnp.sum(q[i].astype(F32) * nvec[i], axis=1, keepdims=True) for i in n]
    den = [a_int[i] * qn[i] + jnp.sum(s[i], axis=1, keepdims=True) for i in n]
    hh = [(a_int[i] * inter[i] + intra[i]) / jnp.maximum(jnp.abs(den[i]), jnp.exp(-m_r[i])) for i in n]
    b_last = [b_row[i][:, L - 1:L] if chains[i][1] == 0 else b_row[i][:, 0:1] for i in n]
    log_w = [b_last[i] - b_col[i] + i_col[i] for i in n]
    m_new = [jnp.maximum(b_last[i] + ms[i], jnp.max(log_w[i], axis=0, keepdims=True)) for i in n]
    w = [jnp.exp(log_w[i] - m_new[i]) for i in n]
    decay = [jnp.exp(b_last[i] + ms[i] - m_new[i]) for i in n]
    kw = [k[i].astype(F32) * w[i] for i in n]
    kv = [lax.dot_general(kw[i].astype(BF16), v[i], (((0,), (0,)), ((), ())), preferred_element_type=F32) for i in n]
    for i, (hd, d) in enumerate(chains):
        c_scr[d, hd] = decay[i] * cmat[i] + kv[i]
        n_scr[d, hd] = decay[i] * nvec[i] + jnp.sum(kw[i], axis=0, keepdims=True)
    return hh, m_new


def _mlstm_scan_body(n_chunks, q_ref, k_ref, v_ref, gc_ref, gr_ref, h_ref, hb_scr, c_scr, n_scr, m0):
    chains = [(hd, d) for hd in range(M_HEADS) for d in range(2)]

    def body(c, ms):
        loaded = [_mlstm_load(hd, c if d == 0 else n_chunks - 1 - c, q_ref, k_ref, v_ref, gc_ref, gr_ref)
                  for hd, d in chains]
        hh, m_new = _mlstm_chunks(chains, ms, loaded, c_scr, n_scr)
        for (hd, d), ld, h in zip(chains, loaded, hh):
            dst = h_ref if d == 0 else hb_scr
            dst[ld[0], ld[1]] = h
        return tuple(m_new)

    ms = lax.fori_loop(0, n_chunks, body, tuple(m0))
    h_ref[...] += hb_scr[...]
    return ms


def _mlstm_scan_ctx_kernel(q_ref, k_ref, v_ref, gc_ref, gr_ref, h_ref, cn_ref, nn_ref, mn_ref, hb_scr, c_scr, n_scr):
    c_scr[...] = jnp.zeros(c_scr.shape, F32)
    n_scr[...] = jnp.zeros(n_scr.shape, F32)
    zero = jnp.zeros((1, 1), F32)
    ms = _mlstm_scan_body(SEQ // M_CHUNK, q_ref, k_ref, v_ref, gc_ref, gr_ref, h_ref, hb_scr, c_scr, n_scr,
                          [zero] * (2 * M_HEADS))
    cn_ref[0] = c_scr[...]
    nn_ref[0] = n_scr[...]
    for hd in range(M_HEADS):
        for d in range(2):
            mn_ref[0, d, hd] = jnp.broadcast_to(ms[2 * hd + d], (1, LANE))


def _mlstm_scan_lat_kernel(q_ref, k_ref, v_ref, gc_ref, gr_ref, c0_ref, n0_ref, m0_ref, ctx_out_ref, h_ref,
                           hb_scr, c_scr, n_scr):
    del ctx_out_ref
    c_scr[...] = c0_ref[0]
    n_scr[...] = n0_ref[0]
    m0 = [m0_ref[0, d, hd] for hd in range(M_HEADS) for d in range(2)]
    _mlstm_scan_body(DEC_SEQ // M_CHUNK, q_ref, k_ref, v_ref, gc_ref, gr_ref, h_ref, hb_scr, c_scr, n_scr, m0)


def _mlstm_scan(q, k, v, gates, state_c, state_n, state_m):
    g16 = gates[:, :4 * M_HEADS].reshape(N_TOK, 4, M_HEADS)
    gcol = jnp.transpose(g16, (2, 0, 1))
    grow = jnp.transpose(g16.reshape(N_TOK // M_CHUNK, M_CHUNK, 4, M_HEADS), (3, 0, 2, 1))
    hd = M_HEAD_DIM
    state_scratch = [pltpu.VMEM((2, M_HEADS, hd, hd), F32), pltpu.VMEM((2, M_HEADS, 1, hd), F32)]
    ncp = SEQ // M_CHUNK
    h_ctx, new_c, new_n, new_m = pl.pallas_call(
        _mlstm_scan_ctx_kernel,
        out_shape=(jax.ShapeDtypeStruct((N_TOK, D), F32),
                   jax.ShapeDtypeStruct((BATCH, 2, M_HEADS, hd, hd), F32),
                   jax.ShapeDtypeStruct((BATCH, 2, M_HEADS, 1, hd), F32),
                   jax.ShapeDtypeStruct((BATCH, 2, M_HEADS, 1, LANE), F32)),
        grid=(BATCH,),
        in_specs=[
            pl.BlockSpec((SEQ, D), lambda s: (s, 0)),
            pl.BlockSpec((SEQ, D), lambda s: (s, 0)),
            pl.BlockSpec((SEQ, D), lambda s: (s, 0)),
            pl.BlockSpec((M_HEADS, SEQ, 4), lambda s: (0, s, 0)),
            pl.BlockSpec((M_HEADS, ncp, 4, M_CHUNK), lambda s: (0, s, 0, 0)),
        ],
        out_specs=(
            pl.BlockSpec((SEQ, D), lambda s: (s, 0)),
            pl.BlockSpec((1, 2, M_HEADS, hd, hd), lambda s: (s, 0, 0, 0, 0)),
            pl.BlockSpec((1, 2, M_HEADS, 1, hd), lambda s: (s, 0, 0, 0, 0)),
            pl.BlockSpec((1, 2, M_HEADS, 1, LANE), lambda s: (s, 0, 0, 0, 0)),
        ),
        scratch_shapes=[pltpu.VMEM((SEQ, D), F32)] + state_scratch,
        compiler_params=_cparams(),
        name="mlstm_scan_ctx",
    )(q, k, v, gcol, grow)
    ncl = DEC_SEQ // M_CHUNK
    pb = NP_TOK // DEC_SEQ
    h_all = pl.pallas_call(
        _mlstm_scan_lat_kernel,
        out_shape=jax.ShapeDtypeStruct((N_TOK, D), F32),
        input_output_aliases={8: 0},
        grid=(DEC_BATCH,),
        in_specs=[
            pl.BlockSpec((DEC_SEQ, D), lambda b: (pb + b, 0)),
            pl.BlockSpec((DEC_SEQ, D), lambda b: (pb + b, 0)),
            pl.BlockSpec((DEC_SEQ, D), lambda b: (pb + b, 0)),
            pl.BlockSpec((M_HEADS, DEC_SEQ, 4), lambda b: (0, pb + b, 0)),
            pl.BlockSpec((M_HEADS, ncl, 4, M_CHUNK), lambda b: (0, pb + b, 0, 0)),
            pl.BlockSpec((1, 2, M_HEADS, hd, hd), lambda b: (b, 0, 0, 0, 0)),
            pl.BlockSpec((1, 2, M_HEADS, 1, hd), lambda b: (b, 0, 0, 0, 0)),
            pl.BlockSpec((1, 2, M_HEADS, 1, 1), lambda b: (b, 0, 0, 0, 0)),
            pl.BlockSpec(memory_space=pl.ANY),
        ],
        out_specs=pl.BlockSpec((DEC_SEQ, D), lambda b: (pb + b, 0)),
        scratch_shapes=[pltpu.VMEM((DEC_SEQ, D), F32)] + state_scratch,
        compiler_params=_cparams(),
        name="mlstm_scan_lat",
    )(q, k, v, gcol, grow, state_c, state_n, state_m, h_ctx)
    return h_all, new_c, new_n, new_m


def _mlstm_out_kernel(mr_ref, h_ref, o_ref, ng_ref, w_ref, y_ref, mod_ref, out_ref, x_scr):
    hc = o_ref[...] * h_ref[...]
    for hd in range(M_HEADS):
        sl = slice(hd * M_HEAD_DIM, (hd + 1) * M_HEAD_DIM)
        x_scr[:, sl] = _rms(hc[:, sl], ng_ref[:, sl]).astype(BF16)
    out = jnp.dot(x_scr[...], w_ref[...], preferred_element_type=F32)
    out_ref[...] = y_ref[...] + mod_ref[0][2:3] * out


def _mlstm_out(hsum, o, norm_g, w_out, y, mods, mrow):
    return pl.pallas_call(
        _mlstm_out_kernel,
        out_shape=jax.ShapeDtypeStruct((N_TOK, D), F32),
        grid_spec=pltpu.PrefetchScalarGridSpec(
            num_scalar_prefetch=1, grid=(NB,),
            in_specs=[_tok_spec(D), _tok_spec(D), _full_spec((1, D)), _full_spec((D, D)), _tok_spec(D), _mod_spec()],
            out_specs=_tok_spec(D),
            scratch_shapes=[pltpu.VMEM((TM, D), BF16)]),
        compiler_params=_cparams(),
        name="mlstm_out",
    )(mrow, hsum, o, norm_g, w_out, y, mods)


ROUTE_OFF = N_GROUPS
SLAB = D // (2 * LANE)
V7X_SC_CORES = 2
V7X_SC_SUBCORES = 16
SC_WORKERS = V7X_SC_CORES * V7X_SC_SUBCORES
SC_WINDOW = 128
HI_MASK = -65536


def _bf16_bits(x):
    return lax.bitcast_convert_type(x.astype(BF16).astype(F32), I32)


def _store_slabs(ref, x):
    rows = x.shape[0]
    for c in range(SLAB):
        lo = lax.shift_right_logical(_bf16_bits(x[:, (2 * c) * LANE:(2 * c + 1) * LANE]), 16)
        hi = _bf16_bits(x[:, (2 * c + 1) * LANE:(2 * c + 2) * LANE]) & HI_MASK
        ref[pl.ds(c, rows, stride=SLAB), :] = lo | hi


def _load_slabs(ref, dst, rows, dtype):
    for c in range(SLAB):
        w = ref[pl.ds(c, rows, stride=SLAB), :]
        lo = lax.bitcast_convert_type(lax.shift_left(w, 16), F32)
        hi = lax.bitcast_convert_type(w & HI_MASK, F32)
        dst[:, (2 * c) * LANE:(2 * c + 1) * LANE] = lo.astype(dtype)
        dst[:, (2 * c + 1) * LANE:(2 * c + 2) * LANE] = hi.astype(dtype)


def _route_kernel(mr_ref, y_ref, mod_ref, g_ref, wr_ref, br_ref, x_ref, id_ref, wt_ref):
    x = _norm_mod(y_ref[...], g_ref[...], mod_ref[0], 1)
    _store_slabs(x_ref, x)
    lg = _dot_hi_lo(x, wr_ref) + br_ref[...]
    lane = lax.broadcasted_iota(I32, lg.shape, 1)
    ninf = -jnp.inf
    big = jnp.int32(LANE)
    lgg = jnp.where(lane < N_GROUPS, lg, ninf)
    gmax = jnp.max(lgg, axis=-1, keepdims=True)
    g_idx = jnp.min(jnp.where(lgg == gmax, lane, big), axis=-1, keepdims=True)
    g_w = 1.0 / jnp.sum(jnp.exp(lgg - gmax), axis=-1, keepdims=True)
    lo = ROUTE_OFF + g_idx * EXPERTS_PER_GROUP
    le = jnp.where((lane >= lo) & (lane < lo + EXPERTS_PER_GROUP), lg, ninf)
    m1 = jnp.max(le, axis=-1, keepdims=True)
    i1 = jnp.min(jnp.where(le == m1, lane, big), axis=-1, keepdims=True)
    le2 = jnp.where(lane == i1, ninf, le)
    m2 = jnp.max(le2, axis=-1, keepdims=True)
    i2 = jnp.min(jnp.where(le2 == m2, lane, big), axis=-1, keepdims=True)
    r = jnp.exp(m2 - m1)
    p1 = 1.0 / (1.0 + r)
    p2 = r / (1.0 + r)
    two = lax.broadcasted_iota(I32, (x.shape[0], TOP_K), 1)
    id_ref[...] = jnp.where(two == 0, i1 - ROUTE_OFF, i2 - ROUTE_OFF)
    wt_ref[...] = jnp.where(two == 0, g_w * p1, g_w * p2)


def _route(y, mods, mrow, g2, w_route, b_route):
    return pl.pallas_call(
        _route_kernel,
        out_shape=(jax.ShapeDtypeStruct((N_TOK * SLAB, LANE), I32), jax.ShapeDtypeStruct((N_TOK, TOP_K), I32),
                   jax.ShapeDtypeStruct((N_TOK, TOP_K), F32)),
        grid_spec=pltpu.PrefetchScalarGridSpec(
            num_scalar_prefetch=1, grid=(NB,),
            in_specs=[_tok_spec(D), _mod_spec(), _full_spec((1, D)), _full_spec((2, D, LANE)), _full_spec((1, LANE))],
            out_specs=(pl.BlockSpec((TM * SLAB, LANE), lambda j, *_: (j, 0)), _tok_spec(TOP_K), _tok_spec(TOP_K))),
        compiler_params=_cparams(),
        name="moe_route",
    )(mrow, y, mods, g2, w_route, b_route)


def _dispatch_tables(expert_id):
    flat_e = expert_id.reshape(-1)
    onehot = (flat_e[:, None] == jnp.arange(N_EXPERTS, dtype=I32)[None, :]).astype(I32)
    csum = jnp.cumsum(onehot, axis=0)
    counts = csum[-1]
    padded = ((counts + EBLK - 1) // EBLK) * EBLK
    pad_end = jnp.cumsum(padded)
    pad_start = pad_end - padded
    dest = jnp.sum((csum - 1 + pad_start[None, :]) * onehot, axis=1).astype(I32)
    n_blk = (padded // EBLK).astype(I32)
    blk_start = (pad_start // EBLK).astype(I32)
    n_used = (pad_end[-1] // EBLK).astype(I32).reshape(1)
    dest2 = dest.reshape(N_TOK, TOP_K)
    return dest2[:, 0].reshape(1, N_TOK), dest2[:, 1].reshape(1, N_TOK), blk_start, n_blk, n_used


def _sc_mesh():
    return plsc.VectorSubcoreMesh(core_axis_name="core", subcore_axis_name="subcore",
                                  num_cores=V7X_SC_CORES, num_subcores=V7X_SC_SUBCORES)


def _sc_worker():
    return lax.axis_index("core") * V7X_SC_SUBCORES + lax.axis_index("subcore")


def _sc_dispatch(x_slabs, d0, d1):
    per = N_TOK // SC_WORKERS

    @functools.partial(
        pl.kernel, out_type=jax.ShapeDtypeStruct((P_SLOTS, SLAB, LANE), I32), mesh=_sc_mesh(), name="moe_dispatch",
        scratch_types=[pltpu.VMEM((1, per), I32), pltpu.VMEM((1, per), I32), pltpu.VMEM((SC_WINDOW, SLAB, LANE), I32)])
    def run(x_hbm, d0_hbm, d1_hbm, o_hbm, i0_v, i1_v, buf):
        base = _sc_worker() * per
        pltpu.sync_copy(d0_hbm.at[:, pl.ds(base, per)], i0_v)
        pltpu.sync_copy(d1_hbm.at[:, pl.ds(base, per)], i1_v)

        @pl.loop(0, per // SC_WINDOW)
        def _(s):
            off = s * SC_WINDOW
            pltpu.sync_copy(x_hbm.at[pl.ds(base + off, SC_WINDOW)], buf)
            pltpu.sync_copy(buf, o_hbm.at[i0_v.at[0, pl.ds(off, SC_WINDOW)]])
            pltpu.sync_copy(buf, o_hbm.at[i1_v.at[0, pl.ds(off, SC_WINDOW)]])

    return run(x_slabs.reshape(N_TOK, SLAB, LANE), d0, d1)


def _sc_collect(y_slabs, dcat):
    per = N_ASSIGN // SC_WORKERS

    @functools.partial(
        pl.kernel, out_type=jax.ShapeDtypeStruct((N_ASSIGN, SLAB, LANE), I32), mesh=_sc_mesh(), name="moe_collect",
        scratch_types=[pltpu.VMEM((1, per), I32), pltpu.VMEM((SC_WINDOW, SLAB, LANE), I32)])
    def run(y_hbm, i_hbm, o_hbm, i_v, buf):
        base = _sc_worker() * per
        pltpu.sync_copy(i_hbm.at[:, pl.ds(base, per)], i_v)

        @pl.loop(0, per // SC_WINDOW)
        def _(s):
            off = s * SC_WINDOW
            pltpu.sync_copy(y_hbm.at[i_v.at[0, pl.ds(off, SC_WINDOW)]], buf)
            pltpu.sync_copy(buf, o_hbm.at[pl.ds(base + off, SC_WINDOW)])

    return run(y_slabs.reshape(P_SLOTS, SLAB, LANE), dcat)


EROWS = EBLK * SLAB


def _expert_kernel(bs_ref, nb_ref, nu_ref, wg_ref, wu_ref, wd_ref, x_hbm, y_hbm,
                   xbuf, ybuf, xs, wg_bf, wu_bf, wd_bf, isem, osem):
    e = pl.program_id(0)
    n_used = nu_ref[0]
    b0 = bs_ref[e]
    nb = nb_ref[e]

    def in_copy(g, slot):
        return pltpu.make_async_copy(x_hbm.at[pl.ds(pl.multiple_of(g * EROWS, EROWS), EROWS)], xbuf.at[slot],
                                     isem.at[slot])

    def out_copy(g, slot):
        return pltpu.make_async_copy(ybuf.at[slot], y_hbm.at[pl.ds(pl.multiple_of(g * EROWS, EROWS), EROWS)],
                                     osem.at[slot])

    @pl.when(e == 0)
    def _():
        in_copy(0, 0).start(priority=1)

    @pl.when(nb > 0)
    def _():
        wg_bf[...] = wg_ref[0, 0].astype(BF16)
        wu_bf[...] = wu_ref[0, 0].astype(BF16)
        wd_bf[...] = wd_ref[0, 0].astype(BF16)

    def block(k, carry):
        g = b0 + k
        slot = lax.rem(g, 2)
        in_copy(g, slot).wait()

        @pl.when(g + 1 < n_used)
        def _():
            in_copy(g + 1, 1 - slot).start(priority=1)

        _load_slabs(xbuf.at[slot], xs, EBLK, BF16)
        xb = xs[...]
        gt = jnp.dot(xb, wg_bf[...], preferred_element_type=F32)
        up = jnp.dot(xb, wu_bf[...], preferred_element_type=F32)
        hmid = (gt * _sigmoid(gt) * up).astype(BF16)
        res = jnp.dot(hmid, wd_bf[...], preferred_element_type=F32)

        @pl.when(g >= 2)
        def _():
            out_copy(g - 2, slot).wait()

        _store_slabs(ybuf.at[slot], res)
        out_copy(g, slot).start()
        return carry

    lax.fori_loop(0, nb, block, 0)

    @pl.when(e == pl.num_programs(0) - 1)
    def _():
        last = n_used - 1
        out_copy(last, lax.rem(last, 2)).wait()

        @pl.when(n_used >= 2)
        def _():
            out_copy(last - 1, lax.rem(last - 1, 2)).wait()


def _experts(x_sorted, blk_start, n_blk, n_used, w_gate, w_up, w_down, layer):
    wspec = lambda r, c: pl.BlockSpec((1, 1, r, c), lambda e, *_: (layer, e, 0, 0))
    return pl.pallas_call(
        _expert_kernel,
        out_shape=jax.ShapeDtypeStruct((P_SLOTS * SLAB, LANE), I32),
        grid_spec=pltpu.PrefetchScalarGridSpec(
            num_scalar_prefetch=3, grid=(N_EXPERTS,),
            in_specs=[wspec(D, D_EXPERT), wspec(D, D_EXPERT), wspec(D_EXPERT, D), pl.BlockSpec(memory_space=pl.ANY)],
            out_specs=pl.BlockSpec(memory_space=pl.ANY),
            scratch_shapes=[
                pltpu.VMEM((2, EROWS, LANE), I32), pltpu.VMEM((2, EROWS, LANE), I32),
                pltpu.VMEM((EBLK, D), BF16),
                pltpu.VMEM((D, D_EXPERT), BF16), pltpu.VMEM((D, D_EXPERT), BF16), pltpu.VMEM((D_EXPERT, D), BF16),
                pltpu.SemaphoreType.DMA((2,)), pltpu.SemaphoreType.DMA((2,)),
            ]),
        compiler_params=_cparams(),
        name="moe_experts",
    )(blk_start, n_blk, n_used, w_gate, w_up, w_down, x_sorted.reshape(P_SLOTS * SLAB, LANE))


def _combine_kernel(final, mr_ref, e0_ref, e1_ref, wt_ref, y_ref, mod_ref, fg_ref, o_ref, a_scr, b_scr):
    _load_slabs(e0_ref, a_scr, TM, F32)
    _load_slabs(e1_ref, b_scr, TM, F32)
    wt = wt_ref[...]
    moe = wt[:, 0:1] * a_scr[...] + wt[:, 1:2] * b_scr[...]
    y_new = y_ref[...] + mod_ref[0][5:6] * moe
    o_ref[...] = _rms(y_new, fg_ref[...]) if final else y_new


def _combine(ym, wts, y, mods, mrow, final_g, blk0, nblk, final):
    tok = lambda width: pl.BlockSpec((TM, width), lambda j, *_: (blk0 + j, 0))
    slab0 = pl.BlockSpec((TM * SLAB, LANE), lambda j, *_: (blk0 + j, 0))
    slab1 = pl.BlockSpec((TM * SLAB, LANE), lambda j, *_: (NB + blk0 + j, 0))
    mod = pl.BlockSpec((1, 6, D), lambda j, mr: (mr[blk0 + j], 0, 0))
    return pl.pallas_call(
        functools.partial(_combine_kernel, final),
        out_shape=jax.ShapeDtypeStruct((nblk * TM, D), F32),
        grid_spec=pltpu.PrefetchScalarGridSpec(
            num_scalar_prefetch=1, grid=(nblk,),
            in_specs=[slab0, slab1, tok(TOP_K), tok(D), mod, _full_spec((1, D))],
            out_specs=pl.BlockSpec((TM, D), lambda j, *_: (j, 0)),
            scratch_shapes=[pltpu.VMEM((TM, D), F32), pltpu.VMEM((TM, D), F32)]),
        compiler_params=_cparams(),
        name="moe_combine",
    )(mrow, ym, ym, wts, y, mods, final_g)


def kernel(x_prompt, x_sample, cache_attn_k, cache_attn_v, state_mlstm_C, state_mlstm_n, state_mlstm_m, c, c_ctx, ada_w, ada_b, norm1_g, norm2_g, conv_w_in, conv_w_dw, conv_b_dw, conv_ln_g, conv_ln_b, conv_w_out, attn_w_qkv, attn_q_norm, attn_k_norm, attn_w_o, mlstm_w_in, mlstm_b_gate, mlstm_norm_g, mlstm_w_out, moe_w_group, moe_b_group, moe_w_router, moe_b_router, moe_w_gate, moe_w_up, moe_w_down, final_norm_g):
    y = jnp.concatenate([x_prompt.reshape(NP_TOK, D), x_sample.reshape(NS_TOK, D)], axis=0)
    cvec = jnp.concatenate([c_ctx[None, :], c, jnp.zeros((MOD_ROWS - 1 - DEC_BATCH, D), F32)], axis=0)
    mods = _ada_all(cvec, ada_w, ada_b)
    rope = _rope_blocks()
    new_k = new_v = new_c = new_n = new_m = None
    for i in range(DEPTH):
        kind, slot = i % 3, i // 3
        mrow = jnp.asarray(_MOD_ROW + i * MOD_ROWS)
        g1 = norm1_g[i].reshape(1, D)
        if kind == 0:
            u = _conv_in(y, mods, mrow, g1, conv_w_in[slot].astype(BF16))
            w_dw = jnp.concatenate([conv_w_dw[slot], jnp.zeros((1, D), F32)], axis=0)
            y = _conv_main(u, y, mods, mrow, w_dw, conv_b_dw[slot].reshape(1, D), conv_ln_g[slot].reshape(1, D),
                           conv_ln_b[slot].reshape(1, D), conv_w_out[slot].astype(BF16))
        elif kind == 1:
            q, kb, vb, kf, vf = _attn_qkv(y, mods, mrow, g1, attn_w_qkv[slot].astype(BF16),
                                          attn_q_norm[slot].reshape(1, HEAD_DIM), attn_k_norm[slot].reshape(1, HEAD_DIM),
                                          rope)
            new_k = kf[:NP_TOK].reshape(BATCH, 1, SEQ, N_KV_HEADS, HEAD_DIM)
            new_v = vf[:NP_TOK].reshape(BATCH, 1, SEQ, N_KV_HEADS, HEAD_DIM)
            ck = cache_attn_k[:, slot].reshape(DEC_BATCH, PAST_LEN, KV_DIM)
            cv = cache_attn_v[:, slot].reshape(DEC_BATCH, PAST_LEN, KV_DIM)
            y = _attention(q, kb, vb, ck, cv, attn_w_o[slot].astype(BF16), y, mods, i)
        else:
            w_in = mlstm_w_in[slot]
            w_gate = jnp.concatenate([w_in[:, 4 * D:], jnp.zeros((D, LANE - 4 * M_HEADS), F32)], axis=1)
            b_gate = jnp.concatenate([mlstm_b_gate[slot], jnp.zeros((LANE - 4 * M_HEADS,), F32)]).reshape(1, LANE)
            q, k, v, o, gates = _mlstm_in(y, mods, mrow, g1, w_in[:, :4 * D].astype(BF16), _split_hi_lo(w_gate), b_gate)
            sc = state_mlstm_C[:, slot]
            sn = state_mlstm_n[:, slot].reshape(DEC_BATCH, 2, M_HEADS, 1, M_HEAD_DIM)
            sm = state_mlstm_m[:, slot].reshape(DEC_BATCH, 2, M_HEADS, 1, 1)
            hsum, nc_, nn_, nm_ = _mlstm_scan(q, k, v, gates, sc, sn, sm)
            new_c = nc_[:, None]
            new_n = nn_.reshape(BATCH, 1, 2, M_HEADS, M_HEAD_DIM)
            new_m = nm_[..., 0, 0].reshape(BATCH, 1, 2, M_HEADS)
            y = _mlstm_out(hsum, o, mlstm_norm_g[slot].reshape(1, D), mlstm_w_out[slot].astype(BF16), y, mods, mrow)
        w_route = jnp.concatenate([moe_w_group[i], moe_w_router[i],
                                   jnp.zeros((D, LANE - N_GROUPS - N_EXPERTS), F32)], axis=1)
        b_route = jnp.concatenate([moe_b_group[i], moe_b_router[i],
                                   jnp.zeros((LANE - N_GROUPS - N_EXPERTS,), F32)]).reshape(1, LANE)
        x2, eid, ewt = _route(y, mods, mrow, norm2_g[i].reshape(1, D), _split_hi_lo(w_route), b_route)
        d0, d1, blk_start, n_blk, n_used = _dispatch_tables(eid)
        x_sorted = _sc_dispatch(x2, d0, d1)
        y_sorted = _experts(x_sorted, blk_start, n_blk, n_used, moe_w_gate, moe_w_up, moe_w_down, i)
        ym = _sc_collect(y_sorted, jnp.concatenate([d0, d1], axis=1))
        ym = ym.reshape(N_ASSIGN * SLAB, LANE)
        fg = final_norm_g.reshape(1, D)
        if i + 1 < DEPTH:
            y = _combine(ym, ewt, y, mods, mrow, fg, 0, NB, False)
        else:
            y_prompt = _combine(ym, ewt, y, mods, mrow, fg, 0, NBP, True).reshape(BATCH, SEQ, D)
            y_sample = _combine(ym, ewt, y, mods, mrow, fg, NBP, NB - NBP, True).reshape(DEC_BATCH, DEC_SEQ, D)
    return (y_prompt, y_sample, new_k, new_v, new_c, new_n, new_m)
```

```python
import functools

import jax
import jax.numpy as jnp
import numpy as np
from jax import lax
from jax.experimental import pallas as pl
from jax.experimental.pallas import tpu as pltpu
from jax.experimental.pallas import tpu_sc as plsc

F32 = jnp.float32
BF16 = jnp.bfloat16
I32 = jnp.int32

D = 1024
BATCH, SEQ = 16, 256
DEC_BATCH, DEC_SEQ = 8, 1024
PAST_LEN = 256
DEPTH = 4
GRID_W = 64
EPS = 1e-6
CONV_WIDTH = 31
CONV_PAD = CONV_WIDTH // 2
HEAD_DIM = 128
N_HEADS = 8
N_KV_HEADS = 2
GQA_GROUP = N_HEADS // N_KV_HEADS
Q_DIM = N_HEADS * HEAD_DIM
KV_DIM = N_KV_HEADS * HEAD_DIM
QKV_DIM = Q_DIM + 2 * KV_DIM
ROPE_THETA = 10000.0
M_HEADS = 4
M_HEAD_DIM = D // M_HEADS
M_CHUNK = 64
N_GROUPS = 4
EXPERTS_PER_GROUP = 8
N_EXPERTS = N_GROUPS * EXPERTS_PER_GROUP
TOP_K = 2
D_EXPERT = 512

NP_TOK = BATCH * SEQ
NS_TOK = DEC_BATCH * DEC_SEQ
N_TOK = NP_TOK + NS_TOK
TM = 256
NB = N_TOK // TM
NBP = NP_TOK // TM
BLK_PER_DEC = DEC_SEQ // TM
MOD_ROWS = 16
HALO = 16
LANE = 128
SUBLANE = 8

N_ASSIGN = N_TOK * TOP_K
EBLK = 256
N_EBLK = N_ASSIGN // EBLK + N_EXPERTS
P_SLOTS = N_EBLK * EBLK
N_PAD_SLOTS = P_SLOTS - N_ASSIGN

VMEM_LIMIT = 56 * 1024 * 1024


def _block_tables():
    j = np.arange(NB)
    is_p = j < NBP
    mod_row = np.where(is_p, 0, 1 + (j - NBP) // BLK_PER_DEC)
    rope_idx = np.where(is_p, 0, 1 + (j - NBP) % BLK_PER_DEC)
    first = np.where(is_p, 1, ((j - NBP) % BLK_PER_DEC == 0).astype(np.int64))
    last = np.where(is_p, 1, ((j - NBP) % BLK_PER_DEC == BLK_PER_DEC - 1).astype(np.int64))
    return (mod_row.astype(np.int32), rope_idx.astype(np.int32), first.astype(np.int32), last.astype(np.int32))


_MOD_ROW, _ROPE_IDX, _SEQ_FIRST, _SEQ_LAST = _block_tables()


def _cparams(n_axes=1):
    return pltpu.CompilerParams(dimension_semantics=("arbitrary",) * n_axes, vmem_limit_bytes=VMEM_LIMIT)


def _sigmoid(x):
    return 1.0 / (1.0 + jnp.exp(-x))


def _rms(x, g):
    return x * lax.rsqrt(jnp.mean(x * x, axis=-1, keepdims=True) + EPS) * g


def _split_hi_lo(w):
    hi = w.astype(BF16)
    return jnp.stack([hi, (w - hi.astype(F32)).astype(BF16)], axis=0)


def _dot_hi_lo(x, w_ref):
    xh = x.astype(BF16)
    xl = (x - xh.astype(F32)).astype(BF16)
    wh, wl = w_ref[0], w_ref[1]
    return (jnp.dot(xh, wh, preferred_element_type=F32)
            + (jnp.dot(xh, wl, preferred_element_type=F32) + jnp.dot(xl, wh, preferred_element_type=F32)))


def _norm_mod(y, g, mod, which):
    shift = mod[3 * which:3 * which + 1]
    scale = mod[3 * which + 1:3 * which + 2]
    return _rms(y, g) * (1.0 + scale) + shift


def _ada_kernel(c_ref, w_ref, b_ref, o_ref):
    c = c_ref[...]
    s = c * _sigmoid(c)
    o_ref[0] = jnp.dot(s.astype(BF16), w_ref[0].astype(BF16), preferred_element_type=F32) + b_ref[0]


def _ada_all(cvec, ada_w, ada_b):
    tn = 1536
    out = pl.pallas_call(
        _ada_kernel,
        out_shape=jax.ShapeDtypeStruct((DEPTH, MOD_ROWS, 6 * D), F32),
        grid=(DEPTH, 6 * D // tn),
        in_specs=[
            pl.BlockSpec((MOD_ROWS, D), lambda l, n: (0, 0)),
            pl.BlockSpec((1, D, tn), lambda l, n: (l, 0, n)),
            pl.BlockSpec((1, 1, tn), lambda l, n: (l, 0, n)),
        ],
        out_specs=pl.BlockSpec((1, MOD_ROWS, tn), lambda l, n: (l, 0, n)),
        compiler_params=_cparams(2),
        name="ada_mod",
    )(cvec, ada_w, ada_b.reshape(DEPTH, 1, 6 * D))
    return out.reshape(DEPTH * MOD_ROWS, 6, D)


def _tok_spec(width):
    return pl.BlockSpec((TM, width), lambda j, *_: (j, 0))


def _mod_spec():
    return pl.BlockSpec((1, 6, D), lambda j, mr, *_: (mr[j], 0, 0))


def _full_spec(shape):
    nd = len(shape)
    return pl.BlockSpec(shape, lambda j, *_: (0,) * nd)


def _conv_in_kernel(mr_ref, y_ref, mod_ref, g_ref, w_ref, u_ref):
    h = _norm_mod(y_ref[...], g_ref[...], mod_ref[0], 0)
    ag = jnp.dot(h.astype(BF16), w_ref[...], preferred_element_type=F32)
    u_ref[...] = ag[:, :D] * _sigmoid(ag[:, D:])


def _conv_in(y, mods, mrow, g1, w_in):
    return pl.pallas_call(
        _conv_in_kernel,
        out_shape=jax.ShapeDtypeStruct((N_TOK, D), F32),
        grid_spec=pltpu.PrefetchScalarGridSpec(
            num_scalar_prefetch=1, grid=(NB,),
            in_specs=[_tok_spec(D), _mod_spec(), _full_spec((1, D)), _full_spec((D, 2 * D))],
            out_specs=_tok_spec(D)),
        compiler_params=_cparams(),
        name="conv_in",
    )(mrow, y, mods, g1, w_in)


def _conv_main_kernel(mr_ref, first_ref, last_ref, u_ref, up_ref, un_ref, wdw_ref, bdw_ref, lg_ref, lb_ref,
                      wout_ref, y_ref, mod_ref, o_ref, ext_ref, acc_ref):
    j = pl.program_id(0)
    zero = jnp.zeros((HALO, D), F32)
    ext_ref[0:HALO, :] = jnp.where(first_ref[j] == 1, zero, up_ref[...])
    ext_ref[HALO:HALO + TM, :] = u_ref[...]
    ext_ref[HALO + TM:2 * HALO + TM, :] = jnp.where(last_ref[j] == 1, zero, un_ref[...])

    off0 = HALO - CONV_PAD
    n_a = (off0 + CONV_WIDTH - 1) // SUBLANE + 1
    n_chunks = TM // SUBLANE

    def strip(ci, carry):
        cs = pl.ds(pl.multiple_of(ci * LANE, LANE), LANE)
        wk = [jnp.broadcast_to(wdw_ref[k:k + 1, cs], (SUBLANE, LANE)) for k in range(CONV_WIDTH)]
        bias = jnp.broadcast_to(bdw_ref[:, cs], (SUBLANE, LANE))
        sub = lax.broadcasted_iota(I32, (SUBLANE, LANE), 0)
        prev_rot, prev_v0 = None, None
        for j in range(n_chunks + 1):
            tiles = [ext_ref[SUBLANE * (j + a):SUBLANE * (j + a + 1), cs] for a in range(n_a)]
            part = []
            for s in range(SUBLANE):
                acc = None
                for a in range(n_a):
                    k = SUBLANE * a + s - off0
                    if (0 <= k < CONV_WIDTH) and not (s == 0 and j == n_chunks):
                        term = tiles[a] * wk[k]
                        acc = term if acc is None else acc + term
                part.append(acc)
            rot = [None] + [pltpu.roll(part[s], SUBLANE - s, 0) for s in range(1, SUBLANE)]
            if j >= 1:
                out = prev_v0 + bias
                for s in range(1, SUBLANE):
                    out = out + jnp.where(sub < SUBLANE - s, prev_rot[s], rot[s])
                acc_ref[SUBLANE * (j - 1):SUBLANE * j, cs] = out
            prev_rot, prev_v0 = rot, part[0]
        return carry

    lax.fori_loop(0, D // LANE, strip, 0)

    c = acc_ref[...]
    mu = jnp.mean(c, axis=-1, keepdims=True)
    cc = c - mu
    var = jnp.mean(cc * cc, axis=-1, keepdims=True)
    z = cc * lax.rsqrt(var + EPS) * lg_ref[...] + lb_ref[...]
    z = z * _sigmoid(z)
    out = jnp.dot(z.astype(BF16), wout_ref[...], preferred_element_type=F32)
    o_ref[...] = y_ref[...] + mod_ref[0][2:3] * out


def _conv_main(u, y, mods, mrow, w_dw, b_dw, ln_g, ln_b, w_out):
    nh = N_TOK // HALO
    per = TM // HALO
    return pl.pallas_call(
        _conv_main_kernel,
        out_shape=jax.ShapeDtypeStruct((N_TOK, D), F32),
        grid_spec=pltpu.PrefetchScalarGridSpec(
            num_scalar_prefetch=3, grid=(NB,),
            in_specs=[
                _tok_spec(D),
                pl.BlockSpec((HALO, D), lambda j, *_: (jnp.maximum(j * per - 1, 0), 0)),
                pl.BlockSpec((HALO, D), lambda j, *_: (jnp.minimum((j + 1) * per, nh - 1), 0)),
                _full_spec((CONV_WIDTH + 1, D)), _full_spec((1, D)), _full_spec((1, D)), _full_spec((1, D)),
                _full_spec((D, D)), _tok_spec(D), _mod_spec(),
            ],
            out_specs=_tok_spec(D),
            scratch_shapes=[pltpu.VMEM((TM + 2 * HALO, D), F32), pltpu.VMEM((TM, D), F32)]),
        compiler_params=_cparams(),
        name="conv_main",
    )(mrow, jnp.asarray(_SEQ_FIRST), jnp.asarray(_SEQ_LAST), u, u, u, w_dw, b_dw, ln_g, ln_b, w_out, y, mods)


def _rope_angles():
    rows = DEC_SEQ // GRID_W
    row = jnp.repeat(jnp.arange(rows, dtype=F32), GRID_W)
    col = jnp.tile(jnp.arange(GRID_W, dtype=F32), rows)
    axis_dim = HEAD_DIM // 2
    freqs = jnp.power(ROPE_THETA, -jnp.arange(axis_dim // 2, dtype=F32) * 2.0 / axis_dim)
    ang_r = row[:, None] * freqs[None, :]
    ang_c = col[:, None] * freqs[None, :]
    return jnp.concatenate([ang_r, ang_r, ang_c, ang_c], axis=-1)


def _rope_blocks():
    ang = _rope_angles()
    cos, sin = jnp.cos(ang), jnp.sin(ang)
    lane = np.arange(HEAD_DIM)
    lo = jnp.asarray(((lane % (HEAD_DIM // 2)) < HEAD_DIM // 4).astype(np.float32))
    sin_a = -sin * lo[None, :]
    sin_b = sin * (1.0 - lo)[None, :]
    nblk = DEC_SEQ // TM
    ident = jnp.ones((1, TM, HEAD_DIM), F32)
    zeros = jnp.zeros((1, TM, HEAD_DIM), F32)
    cos_t = jnp.concatenate([ident, cos.reshape(nblk, TM, HEAD_DIM)], axis=0)
    sa_t = jnp.concatenate([zeros, sin_a.reshape(nblk, TM, HEAD_DIM)], axis=0)
    sb_t = jnp.concatenate([zeros, sin_b.reshape(nblk, TM, HEAD_DIM)], axis=0)
    return cos_t, sa_t, sb_t


def _attn_qkv_kernel(mr_ref, ri_ref, y_ref, mod_ref, g_ref, w_ref, qg_ref, kg_ref, cos_ref, sa_ref, sb_ref,
                     q_ref, kb_ref, vb_ref, kf_ref, vf_ref):
    h = _norm_mod(y_ref[...], g_ref[...], mod_ref[0], 0)
    qkv = jnp.dot(h.astype(BF16), w_ref[...], preferred_element_type=F32)
    cos, sa, sb = cos_ref[0], sa_ref[0], sb_ref[0]
    quarter = HEAD_DIM // 4

    def head(x, g):
        xn = _rms(x, g)
        return xn * cos + pltpu.roll(xn, HEAD_DIM - quarter, 1) * sa + pltpu.roll(xn, quarter, 1) * sb

    scale = HEAD_DIM ** -0.5
    for hd in range(N_HEADS):
        sl = slice(hd * HEAD_DIM, (hd + 1) * HEAD_DIM)
        q_ref[:, sl] = (head(qkv[:, sl], qg_ref[...]) * scale).astype(BF16)
    for kv in range(N_KV_HEADS):
        sl = slice(kv * HEAD_DIM, (kv + 1) * HEAD_DIM)
        kr = head(qkv[:, Q_DIM + kv * HEAD_DIM:Q_DIM + (kv + 1) * HEAD_DIM], kg_ref[...])
        kf_ref[:, sl] = kr
        kb_ref[:, sl] = kr.astype(BF16)
    v = qkv[:, Q_DIM + KV_DIM:]
    vf_ref[...] = v
    vb_ref[...] = v.astype(BF16)


def _attn_qkv(y, mods, mrow, g1, w_qkv, q_g, k_g, rope):
    cos_t, sa_t, sb_t = rope
    rspec = pl.BlockSpec((1, TM, HEAD_DIM), lambda j, mr, ri: (ri[j], 0, 0))
    return pl.pallas_call(
        _attn_qkv_kernel,
        out_shape=(jax.ShapeDtypeStruct((N_TOK, Q_DIM), BF16), jax.ShapeDtypeStruct((N_TOK, KV_DIM), BF16),
                   jax.ShapeDtypeStruct((N_TOK, KV_DIM), BF16), jax.ShapeDtypeStruct((N_TOK, KV_DIM), F32),
                   jax.ShapeDtypeStruct((N_TOK, KV_DIM), F32)),
        grid_spec=pltpu.PrefetchScalarGridSpec(
            num_scalar_prefetch=2, grid=(NB,),
            in_specs=[_tok_spec(D), _mod_spec(), _full_spec((1, D)), _full_spec((D, QKV_DIM)),
                      _full_spec((1, HEAD_DIM)), _full_spec((1, HEAD_DIM)), rspec, rspec, rspec],
            out_specs=(_tok_spec(Q_DIM), _tok_spec(KV_DIM), _tok_spec(KV_DIM), _tok_spec(KV_DIM), _tok_spec(KV_DIM))),
        compiler_params=_cparams(),
        name="attn_qkv",
    )(mrow, jnp.asarray(_ROPE_IDX), y, mods, g1, w_qkv, q_g, k_g, cos_t, sa_t, sb_t)


def _attn_heads(q, ks, vs, o_scr):
    nt = (((1,), (1,)), ((), ()))
    for hd in range(N_HEADS):
        g = hd // GQA_GROUP
        qh = q[:, hd * HEAD_DIM:(hd + 1) * HEAD_DIM]
        gs = slice(g * HEAD_DIM, (g + 1) * HEAD_DIM)
        ss = [lax.dot_general(qh, k[:, gs], nt, preferred_element_type=F32) for k in ks]
        m = functools.reduce(jnp.maximum, [jnp.max(s, axis=-1, keepdims=True) for s in ss])
        ps = [jnp.exp(s - m) for s in ss]
        l = functools.reduce(lambda a, b: a + b, [jnp.sum(p, axis=-1, keepdims=True) for p in ps])
        o = functools.reduce(lambda a, b: a + b,
                             [jnp.dot(p.astype(BF16), v[:, gs], preferred_element_type=F32) for p, v in zip(ps, vs)])
        o_scr[:, hd * HEAD_DIM:(hd + 1) * HEAD_DIM] = (o / l).astype(BF16)


def _attn_ctx_kernel(q_ref, k_ref, v_ref, wo_ref, y_ref, mod_ref, o_ref, o_scr):
    _attn_heads(q_ref[...], [k_ref[...]], [v_ref[...]], o_scr)
    out = jnp.dot(o_scr[...], wo_ref[...], preferred_element_type=F32)
    o_ref[...] = y_ref[...] + mod_ref[0][2:3] * out


def _attn_lat_kernel(q_ref, k_ref, v_ref, ck_ref, cv_ref, wo_ref, y_ref, mod_ref, ctx_out_ref, o_ref, o_scr):
    del ctx_out_ref
    _attn_heads(q_ref[...], [k_ref[...], ck_ref[0].astype(BF16)], [v_ref[...], cv_ref[0].astype(BF16)], o_scr)
    out = jnp.dot(o_scr[...], wo_ref[...], preferred_element_type=F32)
    o_ref[...] = y_ref[...] + mod_ref[0][2:3] * out


def _attention(q, kb, vb, cache_k, cache_v, w_o, y, mods, layer):
    y_ctx = pl.pallas_call(
        _attn_ctx_kernel,
        out_shape=jax.ShapeDtypeStruct((N_TOK, D), F32),
        grid=(BATCH,),
        in_specs=[
            pl.BlockSpec((SEQ, Q_DIM), lambda s: (s, 0)),
            pl.BlockSpec((SEQ, KV_DIM), lambda s: (s, 0)),
            pl.BlockSpec((SEQ, KV_DIM), lambda s: (s, 0)),
            pl.BlockSpec((Q_DIM, D), lambda s: (0, 0)),
            pl.BlockSpec((SEQ, D), lambda s: (s, 0)),
            pl.BlockSpec((1, 6, D), lambda s: (layer * MOD_ROWS, 0, 0)),
        ],
        out_specs=pl.BlockSpec((SEQ, D), lambda s: (s, 0)),
        scratch_shapes=[pltpu.VMEM((SEQ, Q_DIM), BF16)],
        compiler_params=_cparams(),
        name="attn_ctx",
    )(q, kb, vb, w_o, y, mods)
    pb = NP_TOK // DEC_SEQ
    return pl.pallas_call(
        _attn_lat_kernel,
        out_shape=jax.ShapeDtypeStruct((N_TOK, D), F32),
        input_output_aliases={8: 0},
        grid=(DEC_BATCH, BLK_PER_DEC),
        in_specs=[
            pl.BlockSpec((TM, Q_DIM), lambda b, t: (NBP + b * BLK_PER_DEC + t, 0)),
            pl.BlockSpec((DEC_SEQ, KV_DIM), lambda b, t: (pb + b, 0)),
            pl.BlockSpec((DEC_SEQ, KV_DIM), lambda b, t: (pb + b, 0)),
            pl.BlockSpec((1, PAST_LEN, KV_DIM), lambda b, t: (b, 0, 0)),
            pl.BlockSpec((1, PAST_LEN, KV_DIM), lambda b, t: (b, 0, 0)),
            pl.BlockSpec((Q_DIM, D), lambda b, t: (0, 0)),
            pl.BlockSpec((TM, D), lambda b, t: (NBP + b * BLK_PER_DEC + t, 0)),
            pl.BlockSpec((1, 6, D), lambda b, t: (layer * MOD_ROWS + 1 + b, 0, 0)),
            pl.BlockSpec(memory_space=pl.ANY),
        ],
        out_specs=pl.BlockSpec((TM, D), lambda b, t: (NBP + b * BLK_PER_DEC + t, 0)),
        scratch_shapes=[pltpu.VMEM((TM, Q_DIM), BF16)],
        compiler_params=_cparams(2),
        name="attn_lat",
    )(q, kb, vb, cache_k, cache_v, w_o, y, mods, y_ctx)


def _log_sigmoid(x):
    return jnp.minimum(x, 0.0) - jnp.log(1.0 + jnp.exp(-jnp.abs(x)))


def _mlstm_in_kernel(mr_ref, y_ref, mod_ref, g_ref, w_ref, wg_ref, bg_ref, q_ref, k_ref, v_ref, o_ref, gt_ref):
    h = _norm_mod(y_ref[...], g_ref[...], mod_ref[0], 0)
    hb = h.astype(BF16)
    q_ref[...] = jnp.dot(hb, w_ref[:, 0:D], preferred_element_type=F32).astype(BF16)
    k_ref[...] = (jnp.dot(hb, w_ref[:, D:2 * D], preferred_element_type=F32) * (M_HEAD_DIM ** -0.5)).astype(BF16)
    v_ref[...] = jnp.dot(hb, w_ref[:, 2 * D:3 * D], preferred_element_type=F32).astype(BF16)
    o_ref[...] = _sigmoid(jnp.dot(hb, w_ref[:, 3 * D:4 * D], preferred_element_type=F32))
    gt = _dot_hi_lo(h, wg_ref) + bg_ref[...]
    lane = lax.broadcasted_iota(I32, gt.shape, 1)
    is_f = ((lane >= M_HEADS) & (lane < 2 * M_HEADS)) | ((lane >= 3 * M_HEADS) & (lane < 4 * M_HEADS))
    gt_ref[...] = jnp.where(is_f, _log_sigmoid(gt), gt)


def _mlstm_in(y, mods, mrow, g1, w_main, w_gate, b_gate):
    return pl.pallas_call(
        _mlstm_in_kernel,
        out_shape=(jax.ShapeDtypeStruct((N_TOK, D), BF16), jax.ShapeDtypeStruct((N_TOK, D), BF16),
                   jax.ShapeDtypeStruct((N_TOK, D), BF16), jax.ShapeDtypeStruct((N_TOK, D), F32),
                   jax.ShapeDtypeStruct((N_TOK, LANE), F32)),
        grid_spec=pltpu.PrefetchScalarGridSpec(
            num_scalar_prefetch=1, grid=(NB,),
            in_specs=[_tok_spec(D), _mod_spec(), _full_spec((1, D)), _full_spec((D, 4 * D)),
                      _full_spec((2, D, LANE)), _full_spec((1, LANE))],
            out_specs=(_tok_spec(D), _tok_spec(D), _tok_spec(D), _tok_spec(D), _tok_spec(LANE))),
        compiler_params=_cparams(),
        name="mlstm_in",
    )(mrow, y, mods, g1, w_main, w_gate, b_gate)


def _mlstm_load(hd, c, q_ref, k_ref, v_ref, gc_ref, gr_ref):
    r0 = pl.multiple_of(c * M_CHUNK, M_CHUNK)
    hs = slice(hd * M_HEAD_DIM, (hd + 1) * M_HEAD_DIM)
    rows = pl.ds(r0, M_CHUNK)
    return rows, hs, q_ref[rows, hs], k_ref[rows, hs], v_ref[rows, hs], gc_ref[hd, rows, :], gr_ref[hd, c]


def _mlstm_chunks(chains, ms, loaded, c_scr, n_scr):
    L = M_CHUNK
    n = range(len(chains))
    t_idx = lax.broadcasted_iota(I32, (L, L), 0)
    s_idx = lax.broadcasted_iota(I32, (L, L), 1)
    masks = {0: (s_idx <= t_idx, t_idx <= s_idx), 1: (s_idx >= t_idx, t_idx >= s_idx)}
    q = [ld[2] for ld in loaded]
    k = [ld[3] for ld in loaded]
    v = [ld[4] for ld in loaded]
    i_col = [ld[5][:, 2 * d:2 * d + 1] for (_, d), ld in zip(chains, loaded)]
    lf_col = [ld[5][:, 2 * d + 1:2 * d + 2] for (_, d), ld in zip(chains, loaded)]
    i_row = [ld[6][2 * d:2 * d + 1, :] for (_, d), ld in zip(chains, loaded)]
    lf_row = [ld[6][2 * d + 1:2 * d + 2, :] for (_, d), ld in zip(chains, loaded)]
    mask = [masks[d][0] for _, d in chains]
    mask_t = [masks[d][1] for _, d in chains]
    b_col = [jnp.sum(jnp.where(mask[i], lf_row[i], 0.0), axis=1, keepdims=True) for i in n]
    b_row = [jnp.sum(jnp.where(mask_t[i], lf_col[i], 0.0), axis=0, keepdims=True) for i in n]
    log_d = [jnp.where(mask[i], b_col[i] - b_row[i] + i_row[i], -jnp.inf) for i in n]
    li = [b_col[i] + ms[i] for i in n]
    m_r = [jnp.maximum(li[i], jnp.max(log_d[i], axis=1, keepdims=True)) for i in n]
    a_int = [jnp.exp(li[i] - m_r[i]) for i in n]
    dmat = [jnp.exp(log_d[i] - m_r[i]) for i in n]
    cmat = [c_scr[d, hd] for hd, d in chains]
    nvec = [n_scr[d, hd] for hd, d in chains]
    gram = [lax.dot_general(q[i], k[i], (((1,), (1,)), ((), ())), preferred_element_type=F32) for i in n]
    inter = [jnp.dot(q[i], cmat[i].astype(BF16), preferred_element_type=F32) for i in n]
    s = [gram[i] * dmat[i] for i in n]
    intra = [jnp.dot(s[i].astype(BF16), v[i], preferred_element_type=F32) for i in n]
    qn = [jnp.sum(q[i].astype(F32) * nvec[i], axis=1, keepdims=True) for i in n]
    den = [a_int[i] * qn[i] + jnp.sum(s[i], axis=1, keepdims=True) for i in n]
    hh = [(a_int[i] * inter[i] + intra[i]) / jnp.maximum(jnp.abs(den[i]), jnp.exp(-m_r[i])) for i in n]
    b_last = [b_row[i][:, L - 1:L] if chains[i][1] == 0 else b_row[i][:, 0:1] for i in n]
    log_w = [b_last[i] - b_col[i] + i_col[i] for i in n]
    m_new = [jnp.maximum(b_last[i] + ms[i], jnp.max(log_w[i], axis=0, keepdims=True)) for i in n]
    w = [jnp.exp(log_w[i] - m_new[i]) for i in n]
    decay = [jnp.exp(b_last[i] + ms[i] - m_new[i]) for i in n]
    kw = [k[i].astype(F32) * w[i] for i in n]
    kv = [lax.dot_general(kw[i].astype(BF16), v[i], (((0,), (0,)), ((), ())), preferred_element_type=F32) for i in n]
    for i, (hd, d) in enumerate(chains):
        c_scr[d, hd] = decay[i] * cmat[i] + kv[i]
        n_scr[d, hd] = decay[i] * nvec[i] + jnp.sum(kw[i], axis=0, keepdims=True)
    return hh, m_new


def _mlstm_scan_body(n_chunks, q_ref, k_ref, v_ref, gc_ref, gr_ref, h_ref, hb_scr, c_scr, n_scr, m0):
    chains = [(hd, d) for hd in range(M_HEADS) for d in range(2)]

    def body(c, ms):
        loaded = [_mlstm_load(hd, c if d == 0 else n_chunks - 1 - c, q_ref, k_ref, v_ref, gc_ref, gr_ref)
                  for hd, d in chains]
        hh, m_new = _mlstm_chunks(chains, ms, loaded, c_scr, n_scr)
        for (hd, d), ld, h in zip(chains, loaded, hh):
            dst = h_ref if d == 0 else hb_scr
            dst[ld[0], ld[1]] = h
        return tuple(m_new)

    ms = lax.fori_loop(0, n_chunks, body, tuple(m0))
    h_ref[...] += hb_scr[...]
    return ms


def _mlstm_scan_ctx_kernel(q_ref, k_ref, v_ref, gc_ref, gr_ref, h_ref, cn_ref, nn_ref, mn_ref, hb_scr, c_scr, n_scr):
    c_scr[...] = jnp.zeros(c_scr.shape, F32)
    n_scr[...] = jnp.zeros(n_scr.shape, F32)
    zero = jnp.zeros((1, 1), F32)
    ms = _mlstm_scan_body(SEQ // M_CHUNK, q_ref, k_ref, v_ref, gc_ref, gr_ref, h_ref, hb_scr, c_scr, n_scr,
                          [zero] * (2 * M_HEADS))
    cn_ref[0] = c_scr[...]
    nn_ref[0] = n_scr[...]
    for hd in range(M_HEADS):
        for d in range(2):
            mn_ref[0, d, hd] = jnp.broadcast_to(ms[2 * hd + d], (1, LANE))


def _mlstm_scan_lat_kernel(q_ref, k_ref, v_ref, gc_ref, gr_ref, c0_ref, n0_ref, m0_ref, ctx_out_ref, h_ref,
                           hb_scr, c_scr, n_scr):
    del ctx_out_ref
    c_scr[...] = c0_ref[0]
    n_scr[...] = n0_ref[0]
    m0 = [m0_ref[0, d, hd] for hd in range(M_HEADS) for d in range(2)]
    _mlstm_scan_body(DEC_SEQ // M_CHUNK, q_ref, k_ref, v_ref, gc_ref, gr_ref, h_ref, hb_scr, c_scr, n_scr, m0)


def _mlstm_scan(q, k, v, gates, state_c, state_n, state_m):
    g16 = gates[:, :4 * M_HEADS].reshape(N_TOK, 4, M_HEADS)
    gcol = jnp.transpose(g16, (2, 0, 1))
    grow = jnp.transpose(g16.reshape(N_TOK // M_CHUNK, M_CHUNK, 4, M_HEADS), (3, 0, 2, 1))
    hd = M_HEAD_DIM
    state_scratch = [pltpu.VMEM((2, M_HEADS, hd, hd), F32), pltpu.VMEM((2, M_HEADS, 1, hd), F32)]
    ncp = SEQ // M_CHUNK
    h_ctx, new_c, new_n, new_m = pl.pallas_call(
        _mlstm_scan_ctx_kernel,
        out_shape=(jax.ShapeDtypeStruct((N_TOK, D), F32),
                   jax.ShapeDtypeStruct((BATCH, 2, M_HEADS, hd, hd), F32),
                   jax.ShapeDtypeStruct((BATCH, 2, M_HEADS, 1, hd), F32),
                   jax.ShapeDtypeStruct((BATCH, 2, M_HEADS, 1, LANE), F32)),
        grid=(BATCH,),
        in_specs=[
            pl.BlockSpec((SEQ, D), lambda s: (s, 0)),
            pl.BlockSpec((SEQ, D), lambda s: (s, 0)),
            pl.BlockSpec((SEQ, D), lambda s: (s, 0)),
            pl.BlockSpec((M_HEADS, SEQ, 4), lambda s: (0, s, 0)),
            pl.BlockSpec((M_HEADS, ncp, 4, M_CHUNK), lambda s: (0, s, 0, 0)),
        ],
        out_specs=(
            pl.BlockSpec((SEQ, D), lambda s: (s, 0)),
            pl.BlockSpec((1, 2, M_HEADS, hd, hd), lambda s: (s, 0, 0, 0, 0)),
            pl.BlockSpec((1, 2, M_HEADS, 1, hd), lambda s: (s, 0, 0, 0, 0)),
            pl.BlockSpec((1, 2, M_HEADS, 1, LANE), lambda s: (s, 0, 0, 0, 0)),
        ),
        scratch_shapes=[pltpu.VMEM((SEQ, D), F32)] + state_scratch,
        compiler_params=_cparams(),
        name="mlstm_scan_ctx",
    )(q, k, v, gcol, grow)
    ncl = DEC_SEQ // M_CHUNK
    pb = NP_TOK // DEC_SEQ
    h_all = pl.pallas_call(
        _mlstm_scan_lat_kernel,
        out_shape=jax.ShapeDtypeStruct((N_TOK, D), F32),
        input_output_aliases={8: 0},
        grid=(DEC_BATCH,),
        in_specs=[
            pl.BlockSpec((DEC_SEQ, D), lambda b: (pb + b, 0)),
            pl.BlockSpec((DEC_SEQ, D), lambda b: (pb + b, 0)),
            pl.BlockSpec((DEC_SEQ, D), lambda b: (pb + b, 0)),
            pl.BlockSpec((M_HEADS, DEC_SEQ, 4), lambda b: (0, pb + b, 0)),
            pl.BlockSpec((M_HEADS, ncl, 4, M_CHUNK), lambda b: (0, pb + b, 0, 0)),
            pl.BlockSpec((1, 2, M_HEADS, hd, hd), lambda b: (b, 0, 0, 0, 0)),
            pl.BlockSpec((1, 2, M_HEADS, 1, hd), lambda b: (b, 0, 0, 0, 0)),
            pl.BlockSpec((1, 2, M_HEADS, 1, 1), lambda b: (b, 0, 0, 0, 0)),
            pl.BlockSpec(memory_space=pl.ANY),
        ],
        out_specs=pl.BlockSpec((DEC_SEQ, D), lambda b: (pb + b, 0)),
        scratch_shapes=[pltpu.VMEM((DEC_SEQ, D), F32)] + state_scratch,
        compiler_params=_cparams(),
        name="mlstm_scan_lat",
    )(q, k, v, gcol, grow, state_c, state_n, state_m, h_ctx)
    return h_all, new_c, new_n, new_m


def _mlstm_out_kernel(mr_ref, h_ref, o_ref, ng_ref, w_ref, y_ref, mod_ref, out_ref, x_scr):
    hc = o_ref[...] * h_ref[...]
    for hd in range(M_HEADS):
        sl = slice(hd * M_HEAD_DIM, (hd + 1) * M_HEAD_DIM)
        x_scr[:, sl] = _rms(hc[:, sl], ng_ref[:, sl]).astype(BF16)
    out = jnp.dot(x_scr[...], w_ref[...], preferred_element_type=F32)
    out_ref[...] = y_ref[...] + mod_ref[0][2:3] * out


def _mlstm_out(hsum, o, norm_g, w_out, y, mods, mrow):
    return pl.pallas_call(
        _mlstm_out_kernel,
        out_shape=jax.ShapeDtypeStruct((N_TOK, D), F32),
        grid_spec=pltpu.PrefetchScalarGridSpec(
            num_scalar_prefetch=1, grid=(NB,),
            in_specs=[_tok_spec(D), _tok_spec(D), _full_spec((1, D)), _full_spec((D, D)), _tok_spec(D), _mod_spec()],
            out_specs=_tok_spec(D),
            scratch_shapes=[pltpu.VMEM((TM, D), BF16)]),
        compiler_params=_cparams(),
        name="mlstm_out",
    )(mrow, hsum, o, norm_g, w_out, y, mods)


ROUTE_OFF = N_GROUPS
SLAB = D // (2 * LANE)
V7X_SC_CORES = 2
V7X_SC_SUBCORES = 16
SC_WORKERS = V7X_SC_CORES * V7X_SC_SUBCORES
SC_WINDOW = 128
HI_MASK = -65536


def _bf16_bits(x):
    return lax.bitcast_convert_type(x.astype(BF16).astype(F32), I32)


def _store_slabs(ref, x):
    rows = x.shape[0]
    for c in range(SLAB):
        lo = lax.shift_right_logical(_bf16_bits(x[:, (2 * c) * LANE:(2 * c + 1) * LANE]), 16)
        hi = _bf16_bits(x[:, (2 * c + 1) * LANE:(2 * c + 2) * LANE]) & HI_MASK
        ref[pl.ds(c, rows, stride=SLAB), :] = lo | hi


def _load_slabs(ref, dst, rows, dtype):
    for c in range(SLAB):
        w = ref[pl.ds(c, rows, stride=SLAB), :]
        lo = lax.bitcast_convert_type(lax.shift_left(w, 16), F32)
        hi = lax.bitcast_convert_type(w & HI_MASK, F32)
        dst[:, (2 * c) * LANE:(2 * c + 1) * LANE] = lo.astype(dtype)
        dst[:, (2 * c + 1) * LANE:(2 * c + 2) * LANE] = hi.astype(dtype)


def _route_kernel(mr_ref, y_ref, mod_ref, g_ref, wr_ref, br_ref, x_ref, id_ref, wt_ref):
    x = _norm_mod(y_ref[...], g_ref[...], mod_ref[0], 1)
    _store_slabs(x_ref, x)
    lg = _dot_hi_lo(x, wr_ref) + br_ref[...]
    lane = lax.broadcasted_iota(I32, lg.shape, 1)
    ninf = -jnp.inf
    big = jnp.int32(LANE)
    lgg = jnp.where(lane < N_GROUPS, lg, ninf)
    gmax = jnp.max(lgg, axis=-1, keepdims=True)
    g_idx = jnp.min(jnp.where(lgg == gmax, lane, big), axis=-1, keepdims=True)
    g_w = 1.0 / jnp.sum(jnp.exp(lgg - gmax), axis=-1, keepdims=True)
    lo = ROUTE_OFF + g_idx * EXPERTS_PER_GROUP
    le = jnp.where((lane >= lo) & (lane < lo + EXPERTS_PER_GROUP), lg, ninf)
    m1 = jnp.max(le, axis=-1, keepdims=True)
    i1 = jnp.min(jnp.where(le == m1, lane, big), axis=-1, keepdims=True)
    le2 = jnp.where(lane == i1, ninf, le)
    m2 = jnp.max(le2, axis=-1, keepdims=True)
    i2 = jnp.min(jnp.where(le2 == m2, lane, big), axis=-1, keepdims=True)
    r = jnp.exp(m2 - m1)
    p1 = 1.0 / (1.0 + r)
    p2 = r / (1.0 + r)
    two = lax.broadcasted_iota(I32, (x.shape[0], TOP_K), 1)
    id_ref[...] = jnp.where(two == 0, i1 - ROUTE_OFF, i2 - ROUTE_OFF)
    wt_ref[...] = jnp.where(two == 0, g_w * p1, g_w * p2)


def _route(y, mods, mrow, g2, w_route, b_route):
    return pl.pallas_call(
        _route_kernel,
        out_shape=(jax.ShapeDtypeStruct((N_TOK * SLAB, LANE), I32), jax.ShapeDtypeStruct((N_TOK, TOP_K), I32),
                   jax.ShapeDtypeStruct((N_TOK, TOP_K), F32)),
        grid_spec=pltpu.PrefetchScalarGridSpec(
            num_scalar_prefetch=1, grid=(NB,),
            in_specs=[_tok_spec(D), _mod_spec(), _full_spec((1, D)), _full_spec((2, D, LANE)), _full_spec((1, LANE))],
            out_specs=(pl.BlockSpec((TM * SLAB, LANE), lambda j, *_: (j, 0)), _tok_spec(TOP_K), _tok_spec(TOP_K))),
        compiler_params=_cparams(),
        name="moe_route",
    )(mrow, y, mods, g2, w_route, b_route)


def _dispatch_tables(expert_id):
    flat_e = expert_id.reshape(-1)
    onehot = (flat_e[:, None] == jnp.arange(N_EXPERTS, dtype=I32)[None, :]).astype(I32)
    csum = jnp.cumsum(onehot, axis=0)
    counts = csum[-1]
    padded = ((counts + EBLK - 1) // EBLK) * EBLK
    pad_end = jnp.cumsum(padded)
    pad_start = pad_end - padded
    dest = jnp.sum((csum - 1 + pad_start[None, :]) * onehot, axis=1).astype(I32)
    n_blk = (padded // EBLK).astype(I32)
    blk_start = (pad_start // EBLK).astype(I32)
    n_used = (pad_end[-1] // EBLK).astype(I32).reshape(1)
    dest2 = dest.reshape(N_TOK, TOP_K)
    return dest2[:, 0].reshape(1, N_TOK), dest2[:, 1].reshape(1, N_TOK), blk_start, n_blk, n_used


def _sc_mesh():
    return plsc.VectorSubcoreMesh(core_axis_name="core", subcore_axis_name="subcore",
                                  num_cores=V7X_SC_CORES, num_subcores=V7X_SC_SUBCORES)


def _sc_worker():
    return lax.axis_index("core") * V7X_SC_SUBCORES + lax.axis_index("subcore")


def _sc_dispatch(x_slabs, d0, d1):
    per = N_TOK // SC_WORKERS

    @functools.partial(
        pl.kernel, out_type=jax.ShapeDtypeStruct((P_SLOTS, SLAB, LANE), I32), mesh=_sc_mesh(), name="moe_dispatch",
        scratch_types=[pltpu.VMEM((1, per), I32), pltpu.VMEM((1, per), I32), pltpu.VMEM((SC_WINDOW, SLAB, LANE), I32)])
    def run(x_hbm, d0_hbm, d1_hbm, o_hbm, i0_v, i1_v, buf):
        base = _sc_worker() * per
        pltpu.sync_copy(d0_hbm.at[:, pl.ds(base, per)], i0_v)
        pltpu.sync_copy(d1_hbm.at[:, pl.ds(base, per)], i1_v)

        @pl.loop(0, per // SC_WINDOW)
        def _(s):
            off = s * SC_WINDOW
            pltpu.sync_copy(x_hbm.at[pl.ds(base + off, SC_WINDOW)], buf)
            pltpu.sync_copy(buf, o_hbm.at[i0_v.at[0, pl.ds(off, SC_WINDOW)]])
            pltpu.sync_copy(buf, o_hbm.at[i1_v.at[0, pl.ds(off, SC_WINDOW)]])

    return run(x_slabs.reshape(N_TOK, SLAB, LANE), d0, d1)


def _sc_collect(y_slabs, dcat):
    per = N_ASSIGN // SC_WORKERS

    @functools.partial(
        pl.kernel, out_type=jax.ShapeDtypeStruct((N_ASSIGN, SLAB, LANE), I32), mesh=_sc_mesh(), name="moe_collect",
        scratch_types=[pltpu.VMEM((1, per), I32), pltpu.VMEM((SC_WINDOW, SLAB, LANE), I32)])
    def run(y_hbm, i_hbm, o_hbm, i_v, buf):
        base = _sc_worker() * per
        pltpu.sync_copy(i_hbm.at[:, pl.ds(base, per)], i_v)

        @pl.loop(0, per // SC_WINDOW)
        def _(s):
            off = s * SC_WINDOW
            pltpu.sync_copy(y_hbm.at[i_v.at[0, pl.ds(off, SC_WINDOW)]], buf)
            pltpu.sync_copy(buf, o_hbm.at[pl.ds(base + off, SC_WINDOW)])

    return run(y_slabs.reshape(P_SLOTS, SLAB, LANE), dcat)


EROWS = EBLK * SLAB


W_CHUNKS = 8


def _expert_kernel(layer, bs_ref, nb_ref, nu_ref, wg_hbm, wu_hbm, wd_hbm, x_hbm, y_hbm,
                   xbuf, ybuf, xs, wg_f, wu_f, wd_f, wg_bf, wu_bf, wd_bf, isem, osem, wsem):
    e = pl.program_id(0)
    n_exp = pl.num_programs(0)
    n_used = nu_ref[0]
    b0 = bs_ref[e]
    nb = nb_ref[e]
    wslot = lax.rem(e, 2)

    def weight_copies(ex, slot):
        out = []
        for hbm, buf in ((wg_hbm, wg_f), (wu_hbm, wu_f), (wd_hbm, wd_f)):
            rows = buf.shape[1] // W_CHUNKS
            for c in range(W_CHUNKS):
                rs = pl.ds(c * rows, rows)
                out.append(pltpu.make_async_copy(hbm.at[layer, ex, rs], buf.at[slot, rs], wsem.at[slot]))
        return out

    def start_weights(ex, slot):
        for i, cp in enumerate(weight_copies(ex, slot)):
            cp.start(priority=i % 2)

    @pl.when(e == 0)
    def _():
        start_weights(0, 0)

    for cp in weight_copies(e, wslot):
        cp.wait()

    @pl.when(e + 1 < n_exp)
    def _():
        start_weights(e + 1, 1 - wslot)

    def in_copy(g, slot):
        return pltpu.make_async_copy(x_hbm.at[pl.ds(pl.multiple_of(g * EROWS, EROWS), EROWS)], xbuf.at[slot],
                                     isem.at[slot])

    def out_copy(g, slot):
        return pltpu.make_async_copy(ybuf.at[slot], y_hbm.at[pl.ds(pl.multiple_of(g * EROWS, EROWS), EROWS)],
                                     osem.at[slot])

    @pl.when(e == 0)
    def _():
        in_copy(0, 0).start(priority=1)

    @pl.when(nb > 0)
    def _():
        wg_bf[...] = wg_f[wslot].astype(BF16)
        wu_bf[...] = wu_f[wslot].astype(BF16)
        wd_bf[...] = wd_f[wslot].astype(BF16)

    def block(k, carry):
        g = b0 + k
        slot = lax.rem(g, 2)
        in_copy(g, slot).wait()

        @pl.when(g + 1 < n_used)
        def _():
            in_copy(g + 1, 1 - slot).start(priority=1)

        _load_slabs(xbuf.at[slot], xs, EBLK, BF16)
        xb = xs[...]
        gt = jnp.dot(xb, wg_bf[...], preferred_element_type=F32)
        up = jnp.dot(xb, wu_bf[...], preferred_element_type=F32)
        hmid = (gt * _sigmoid(gt) * up).astype(BF16)
        res = jnp.dot(hmid, wd_bf[...], preferred_element_type=F32)

        @pl.when(g >= 2)
        def _():
            out_copy(g - 2, slot).wait()

        _store_slabs(ybuf.at[slot], res)
        out_copy(g, slot).start()
        return carry

    lax.fori_loop(0, nb, block, 0)

    @pl.when(e == n_exp - 1)
    def _():
        last = n_used - 1
        out_copy(last, lax.rem(last, 2)).wait()

        @pl.when(n_used >= 2)
        def _():
            out_copy(last - 1, lax.rem(last - 1, 2)).wait()


def _experts(x_sorted, blk_start, n_blk, n_used, w_gate, w_up, w_down, layer):
    any_spec = pl.BlockSpec(memory_space=pl.ANY)
    return pl.pallas_call(
        functools.partial(_expert_kernel, layer),
        out_shape=jax.ShapeDtypeStruct((P_SLOTS * SLAB, LANE), I32),
        grid_spec=pltpu.PrefetchScalarGridSpec(
            num_scalar_prefetch=3, grid=(N_EXPERTS,),
            in_specs=[any_spec, any_spec, any_spec, any_spec],
            out_specs=any_spec,
            scratch_shapes=[
                pltpu.VMEM((2, EROWS, LANE), I32), pltpu.VMEM((2, EROWS, LANE), I32),
                pltpu.VMEM((EBLK, D), BF16),
                pltpu.VMEM((2, D, D_EXPERT), F32), pltpu.VMEM((2, D, D_EXPERT), F32), pltpu.VMEM((2, D_EXPERT, D), F32),
                pltpu.VMEM((D, D_EXPERT), BF16), pltpu.VMEM((D, D_EXPERT), BF16), pltpu.VMEM((D_EXPERT, D), BF16),
                pltpu.SemaphoreType.DMA((2,)), pltpu.SemaphoreType.DMA((2,)), pltpu.SemaphoreType.DMA((2,)),
            ]),
        compiler_params=_cparams(),
        name="moe_experts",
    )(blk_start, n_blk, n_used, w_gate, w_up, w_down, x_sorted.reshape(P_SLOTS * SLAB, LANE))


def _combine_kernel(final, mr_ref, e0_ref, e1_ref, wt_ref, y_ref, mod_ref, fg_ref, o_ref, a_scr, b_scr):
    _load_slabs(e0_ref, a_scr, TM, F32)
    _load_slabs(e1_ref, b_scr, TM, F32)
    wt = wt_ref[...]
    moe = wt[:, 0:1] * a_scr[...] + wt[:, 1:2] * b_scr[...]
    y_new = y_ref[...] + mod_ref[0][5:6] * moe
    o_ref[...] = _rms(y_new, fg_ref[...]) if final else y_new


def _combine(ym, wts, y, mods, mrow, final_g, blk0, nblk, final):
    tok = lambda width: pl.BlockSpec((TM, width), lambda j, *_: (blk0 + j, 0))
    slab0 = pl.BlockSpec((TM * SLAB, LANE), lambda j, *_: (blk0 + j, 0))
    slab1 = pl.BlockSpec((TM * SLAB, LANE), lambda j, *_: (NB + blk0 + j, 0))
    mod = pl.BlockSpec((1, 6, D), lambda j, mr: (mr[blk0 + j], 0, 0))
    return pl.pallas_call(
        functools.partial(_combine_kernel, final),
        out_shape=jax.ShapeDtypeStruct((nblk * TM, D), F32),
        grid_spec=pltpu.PrefetchScalarGridSpec(
            num_scalar_prefetch=1, grid=(nblk,),
            in_specs=[slab0, slab1, tok(TOP_K), tok(D), mod, _full_spec((1, D))],
            out_specs=pl.BlockSpec((TM, D), lambda j, *_: (j, 0)),
            scratch_shapes=[pltpu.VMEM((TM, D), F32), pltpu.VMEM((TM, D), F32)]),
        compiler_params=_cparams(),
        name="moe_combine",
    )(mrow, ym, ym, wts, y, mods, final_g)


def kernel(x_prompt, x_sample, cache_attn_k, cache_attn_v, state_mlstm_C, state_mlstm_n, state_mlstm_m, c, c_ctx, ada_w, ada_b, norm1_g, norm2_g, conv_w_in, conv_w_dw, conv_b_dw, conv_ln_g, conv_ln_b, conv_w_out, attn_w_qkv, attn_q_norm, attn_k_norm, attn_w_o, mlstm_w_in, mlstm_b_gate, mlstm_norm_g, mlstm_w_out, moe_w_group, moe_b_group, moe_w_router, moe_b_router, moe_w_gate, moe_w_up, moe_w_down, final_norm_g):
    y = jnp.concatenate([x_prompt.reshape(NP_TOK, D), x_sample.reshape(NS_TOK, D)], axis=0)
    cvec = jnp.concatenate([c_ctx[None, :], c, jnp.zeros((MOD_ROWS - 1 - DEC_BATCH, D), F32)], axis=0)
    mods = _ada_all(cvec, ada_w, ada_b)
    rope = _rope_blocks()
    new_k = new_v = new_c = new_n = new_m = None
    for i in range(DEPTH):
        kind, slot = i % 3, i // 3
        mrow = jnp.asarray(_MOD_ROW + i * MOD_ROWS)
        g1 = norm1_g[i].reshape(1, D)
        if kind == 0:
            u = _conv_in(y, mods, mrow, g1, conv_w_in[slot].astype(BF16))
            w_dw = jnp.concatenate([conv_w_dw[slot], jnp.zeros((1, D), F32)], axis=0)
            y = _conv_main(u, y, mods, mrow, w_dw, conv_b_dw[slot].reshape(1, D), conv_ln_g[slot].reshape(1, D),
                           conv_ln_b[slot].reshape(1, D), conv_w_out[slot].astype(BF16))
        elif kind == 1:
            q, kb, vb, kf, vf = _attn_qkv(y, mods, mrow, g1, attn_w_qkv[slot].astype(BF16),
                                          attn_q_norm[slot].reshape(1, HEAD_DIM), attn_k_norm[slot].reshape(1, HEAD_DIM),
                                          rope)
            new_k = kf[:NP_TOK].reshape(BATCH, 1, SEQ, N_KV_HEADS, HEAD_DIM)
            new_v = vf[:NP_TOK].reshape(BATCH, 1, SEQ, N_KV_HEADS, HEAD_DIM)
            ck = cache_attn_k[:, slot].reshape(DEC_BATCH, PAST_LEN, KV_DIM)
            cv = cache_attn_v[:, slot].reshape(DEC_BATCH, PAST_LEN, KV_DIM)
            y = _attention(q, kb, vb, ck, cv, attn_w_o[slot].astype(BF16), y, mods, i)
        else:
            w_in = mlstm_w_in[slot]
            w_gate = jnp.concatenate([w_in[:, 4 * D:], jnp.zeros((D, LANE - 4 * M_HEADS), F32)], axis=1)
            b_gate = jnp.concatenate([mlstm_b_gate[slot], jnp.zeros((LANE - 4 * M_HEADS,), F32)]).reshape(1, LANE)
            q, k, v, o, gates = _mlstm_in(y, mods, mrow, g1, w_in[:, :4 * D].astype(BF16), _split_hi_lo(w_gate), b_gate)
            sc = state_mlstm_C[:, slot]
            sn = state_mlstm_n[:, slot].reshape(DEC_BATCH, 2, M_HEADS, 1, M_HEAD_DIM)
            sm = state_mlstm_m[:, slot].reshape(DEC_BATCH, 2, M_HEADS, 1, 1)
            hsum, nc_, nn_, nm_ = _mlstm_scan(q, k, v, gates, sc, sn, sm)
            new_c = nc_[:, None]
            new_n = nn_.reshape(BATCH, 1, 2, M_HEADS, M_HEAD_DIM)
            new_m = nm_[..., 0, 0].reshape(BATCH, 1, 2, M_HEADS)
            y = _mlstm_out(hsum, o, mlstm_norm_g[slot].reshape(1, D), mlstm_w_out[slot].astype(BF16), y, mods, mrow)
        w_route = jnp.concatenate([moe_w_group[i], moe_w_router[i],
                                   jnp.zeros((D, LANE - N_GROUPS - N_EXPERTS), F32)], axis=1)
        b_route = jnp.concatenate([moe_b_group[i], moe_b_router[i],
                                   jnp.zeros((LANE - N_GROUPS - N_EXPERTS,), F32)]).reshape(1, LANE)
        x2, eid, ewt = _route(y, mods, mrow, norm2_g[i].reshape(1, D), _split_hi_lo(w_route), b_route)
        d0, d1, blk_start, n_blk, n_used = _dispatch_tables(eid)
        x_sorted = _sc_dispatch(x2, d0, d1)
        y_sorted = _experts(x_sorted, blk_start, n_blk, n_used, moe_w_gate, moe_w_up, moe_w_down, i)
        ym = _sc_collect(y_sorted, jnp.concatenate([d0, d1], axis=1))
        ym = ym.reshape(N_ASSIGN * SLAB, LANE)
        fg = final_norm_g.reshape(1, D)
        if i + 1 < DEPTH:
            y = _combine(ym, ewt, y, mods, mrow, fg, 0, NB, False)
        else:
            y_prompt = _combine(ym, ewt, y, mods, mrow, fg, 0, NBP, True).reshape(BATCH, SEQ, D)
            y_sample = _combine(ym, ewt, y, mods, mrow, fg, NBP, NB - NBP, True).reshape(DEC_BATCH, DEC_SEQ, D)
    return (y_prompt, y_sample, new_k, new_v, new_c, new_n, new_m)
```

```python
import functools

import jax
import jax.numpy as jnp
import numpy as np
from jax import lax
from jax.experimental import pallas as pl
from jax.experimental.pallas import tpu as pltpu
from jax.experimental.pallas import tpu_sc as plsc

F32 = jnp.float32
BF16 = jnp.bfloat16
I32 = jnp.int32

D = 1024
BATCH, SEQ = 16, 256
DEC_BATCH, DEC_SEQ = 8, 1024
PAST_LEN = 256
DEPTH = 4
GRID_W = 64
EPS = 1e-6
CONV_WIDTH = 31
CONV_PAD = CONV_WIDTH // 2
HEAD_DIM = 128
N_HEADS = 8
N_KV_HEADS = 2
GQA_GROUP = N_HEADS // N_KV_HEADS
Q_DIM = N_HEADS * HEAD_DIM
KV_DIM = N_KV_HEADS * HEAD_DIM
QKV_DIM = Q_DIM + 2 * KV_DIM
ROPE_THETA = 10000.0
M_HEADS = 4
M_HEAD_DIM = D // M_HEADS
M_CHUNK = 64
N_GROUPS = 4
EXPERTS_PER_GROUP = 8
N_EXPERTS = N_GROUPS * EXPERTS_PER_GROUP
TOP_K = 2
D_EXPERT = 512

NP_TOK = BATCH * SEQ
NS_TOK = DEC_BATCH * DEC_SEQ
N_TOK = NP_TOK + NS_TOK
TM = 256
NB = N_TOK // TM
NBP = NP_TOK // TM
BLK_PER_DEC = DEC_SEQ // TM
MOD_ROWS = 16
HALO = 16
LANE = 128
SUBLANE = 8

N_ASSIGN = N_TOK * TOP_K
EBLK = 256
N_EBLK = N_ASSIGN // EBLK + N_EXPERTS
P_SLOTS = N_EBLK * EBLK
N_PAD_SLOTS = P_SLOTS - N_ASSIGN

VMEM_LIMIT = 56 * 1024 * 1024


def _block_tables():
    j = np.arange(NB)
    is_p = j < NBP
    mod_row = np.where(is_p, 0, 1 + (j - NBP) // BLK_PER_DEC)
    rope_idx = np.where(is_p, 0, 1 + (j - NBP) % BLK_PER_DEC)
    first = np.where(is_p, 1, ((j - NBP) % BLK_PER_DEC == 0).astype(np.int64))
    last = np.where(is_p, 1, ((j - NBP) % BLK_PER_DEC == BLK_PER_DEC - 1).astype(np.int64))
    return (mod_row.astype(np.int32), rope_idx.astype(np.int32), first.astype(np.int32), last.astype(np.int32))


_MOD_ROW, _ROPE_IDX, _SEQ_FIRST, _SEQ_LAST = _block_tables()


def _cparams(n_axes=1):
    return pltpu.CompilerParams(dimension_semantics=("arbitrary",) * n_axes, vmem_limit_bytes=VMEM_LIMIT)


def _sigmoid(x):
    return 1.0 / (1.0 + jnp.exp(-x))


def _rms(x, g):
    return x * lax.rsqrt(jnp.mean(x * x, axis=-1, keepdims=True) + EPS) * g


def _split_hi_lo(w):
    hi = w.astype(BF16)
    return jnp.concatenate([hi, (w - hi.astype(F32)).astype(BF16)], axis=1)


def _dot_hi_lo(x, w_ref):
    n = w_ref.shape[1] // 2
    xh = x.astype(BF16)
    xl = (x - xh.astype(F32)).astype(BF16)
    both = jnp.dot(xh, w_ref[...], preferred_element_type=F32)
    return both[:, :n] + (both[:, n:] + jnp.dot(xl, w_ref[:, :n], preferred_element_type=F32))


def _norm_mod(y, g, mod, which):
    shift = mod[3 * which:3 * which + 1]
    scale = mod[3 * which + 1:3 * which + 2]
    return _rms(y, g) * (1.0 + scale) + shift


def _ada_kernel(c_ref, w_ref, b_ref, o_ref):
    c = c_ref[...]
    s = c * _sigmoid(c)
    o_ref[0] = jnp.dot(s.astype(BF16), w_ref[0].astype(BF16), preferred_element_type=F32) + b_ref[0]


def _ada_all(cvec, ada_w, ada_b):
    tn = 1536
    out = pl.pallas_call(
        _ada_kernel,
        out_shape=jax.ShapeDtypeStruct((DEPTH, MOD_ROWS, 6 * D), F32),
        grid=(DEPTH, 6 * D // tn),
        in_specs=[
            pl.BlockSpec((MOD_ROWS, D), lambda l, n: (0, 0)),
            pl.BlockSpec((1, D, tn), lambda l, n: (l, 0, n)),
            pl.BlockSpec((1, 1, tn), lambda l, n: (l, 0, n)),
        ],
        out_specs=pl.BlockSpec((1, MOD_ROWS, tn), lambda l, n: (l, 0, n)),
        compiler_params=_cparams(2),
        name="ada_mod",
    )(cvec, ada_w, ada_b.reshape(DEPTH, 1, 6 * D))
    return out.reshape(DEPTH * MOD_ROWS, 6, D)


def _tok_spec(width):
    return pl.BlockSpec((TM, width), lambda j, *_: (j, 0))


def _mod_spec():
    return pl.BlockSpec((1, 6, D), lambda j, mr, *_: (mr[j], 0, 0))


def _full_spec(shape):
    nd = len(shape)
    return pl.BlockSpec(shape, lambda j, *_: (0,) * nd)


def _conv_in_kernel(mr_ref, y_ref, mod_ref, g_ref, w_ref, u_ref):
    h = _norm_mod(y_ref[...], g_ref[...], mod_ref[0], 0)
    ag = jnp.dot(h.astype(BF16), w_ref[...], preferred_element_type=F32)
    u_ref[...] = ag[:, :D] * _sigmoid(ag[:, D:])


def _conv_in(y, mods, mrow, g1, w_in):
    return pl.pallas_call(
        _conv_in_kernel,
        out_shape=jax.ShapeDtypeStruct((N_TOK, D), F32),
        grid_spec=pltpu.PrefetchScalarGridSpec(
            num_scalar_prefetch=1, grid=(NB,),
            in_specs=[_tok_spec(D), _mod_spec(), _full_spec((1, D)), _full_spec((D, 2 * D))],
            out_specs=_tok_spec(D)),
        compiler_params=_cparams(),
        name="conv_in",
    )(mrow, y, mods, g1, w_in)


def _conv_main_kernel(mr_ref, first_ref, last_ref, u_ref, up_ref, un_ref, wdw_ref, bdw_ref, lg_ref, lb_ref,
                      wout_ref, y_ref, mod_ref, o_ref, ext_ref, acc_ref):
    j = pl.program_id(0)
    zero = jnp.zeros((HALO, D), F32)
    ext_ref[0:HALO, :] = jnp.where(first_ref[j] == 1, zero, up_ref[...])
    ext_ref[HALO:HALO + TM, :] = u_ref[...]
    ext_ref[HALO + TM:2 * HALO + TM, :] = jnp.where(last_ref[j] == 1, zero, un_ref[...])

    off0 = HALO - CONV_PAD
    n_a = (off0 + CONV_WIDTH - 1) // SUBLANE + 1
    n_chunks = TM // SUBLANE

    def strip(ci, carry):
        cs = pl.ds(pl.multiple_of(ci * LANE, LANE), LANE)
        wk = [jnp.broadcast_to(wdw_ref[k:k + 1, cs], (SUBLANE, LANE)) for k in range(CONV_WIDTH)]
        bias = jnp.broadcast_to(bdw_ref[:, cs], (SUBLANE, LANE))
        sub = lax.broadcasted_iota(I32, (SUBLANE, LANE), 0)
        prev_rot, prev_v0 = None, None
        for j in range(n_chunks + 1):
            tiles = [ext_ref[SUBLANE * (j + a):SUBLANE * (j + a + 1), cs] for a in range(n_a)]
            part = []
            for s in range(SUBLANE):
                acc = None
                for a in range(n_a):
                    k = SUBLANE * a + s - off0
                    if (0 <= k < CONV_WIDTH) and not (s == 0 and j == n_chunks):
                        term = tiles[a] * wk[k]
                        acc = term if acc is None else acc + term
                part.append(acc)
            rot = [None] + [pltpu.roll(part[s], SUBLANE - s, 0) for s in range(1, SUBLANE)]
            if j >= 1:
                out = prev_v0 + bias
                for s in range(1, SUBLANE):
                    out = out + jnp.where(sub < SUBLANE - s, prev_rot[s], rot[s])
                acc_ref[SUBLANE * (j - 1):SUBLANE * j, cs] = out
            prev_rot, prev_v0 = rot, part[0]
        return carry

    lax.fori_loop(0, D // LANE, strip, 0)

    c = acc_ref[...]
    mu = jnp.mean(c, axis=-1, keepdims=True)
    cc = c - mu
    var = jnp.mean(cc * cc, axis=-1, keepdims=True)
    z = cc * lax.rsqrt(var + EPS) * lg_ref[...] + lb_ref[...]
    z = z * _sigmoid(z)
    out = jnp.dot(z.astype(BF16), wout_ref[...], preferred_element_type=F32)
    o_ref[...] = y_ref[...] + mod_ref[0][2:3] * out


def _conv_main(u, y, mods, mrow, w_dw, b_dw, ln_g, ln_b, w_out):
    nh = N_TOK // HALO
    per = TM // HALO
    return pl.pallas_call(
        _conv_main_kernel,
        out_shape=jax.ShapeDtypeStruct((N_TOK, D), F32),
        grid_spec=pltpu.PrefetchScalarGridSpec(
            num_scalar_prefetch=3, grid=(NB,),
            in_specs=[
                _tok_spec(D),
                pl.BlockSpec((HALO, D), lambda j, *_: (jnp.maximum(j * per - 1, 0), 0)),
                pl.BlockSpec((HALO, D), lambda j, *_: (jnp.minimum((j + 1) * per, nh - 1), 0)),
                _full_spec((CONV_WIDTH + 1, D)), _full_spec((1, D)), _full_spec((1, D)), _full_spec((1, D)),
                _full_spec((D, D)), _tok_spec(D), _mod_spec(),
            ],
            out_specs=_tok_spec(D),
            scratch_shapes=[pltpu.VMEM((TM + 2 * HALO, D), F32), pltpu.VMEM((TM, D), F32)]),
        compiler_params=_cparams(),
        name="conv_main",
    )(mrow, jnp.asarray(_SEQ_FIRST), jnp.asarray(_SEQ_LAST), u, u, u, w_dw, b_dw, ln_g, ln_b, w_out, y, mods)


def _rope_angles():
    rows = DEC_SEQ // GRID_W
    row = jnp.repeat(jnp.arange(rows, dtype=F32), GRID_W)
    col = jnp.tile(jnp.arange(GRID_W, dtype=F32), rows)
    axis_dim = HEAD_DIM // 2
    freqs = jnp.power(ROPE_THETA, -jnp.arange(axis_dim // 2, dtype=F32) * 2.0 / axis_dim)
    ang_r = row[:, None] * freqs[None, :]
    ang_c = col[:, None] * freqs[None, :]
    return jnp.concatenate([ang_r, ang_r, ang_c, ang_c], axis=-1)


def _rope_blocks():
    ang = _rope_angles()
    cos, sin = jnp.cos(ang), jnp.sin(ang)
    lane = np.arange(HEAD_DIM)
    lo = jnp.asarray(((lane % (HEAD_DIM // 2)) < HEAD_DIM // 4).astype(np.float32))
    sin_a = -sin * lo[None, :]
    sin_b = sin * (1.0 - lo)[None, :]
    nblk = DEC_SEQ // TM
    ident = jnp.ones((1, TM, HEAD_DIM), F32)
    zeros = jnp.zeros((1, TM, HEAD_DIM), F32)
    cos_t = jnp.concatenate([ident, cos.reshape(nblk, TM, HEAD_DIM)], axis=0)
    sa_t = jnp.concatenate([zeros, sin_a.reshape(nblk, TM, HEAD_DIM)], axis=0)
    sb_t = jnp.concatenate([zeros, sin_b.reshape(nblk, TM, HEAD_DIM)], axis=0)
    return cos_t, sa_t, sb_t


def _attn_qkv_kernel(mr_ref, ri_ref, y_ref, mod_ref, g_ref, w_ref, qg_ref, kg_ref, cos_ref, sa_ref, sb_ref,
                     q_ref, kb_ref, vb_ref, kf_ref, vf_ref):
    h = _norm_mod(y_ref[...], g_ref[...], mod_ref[0], 0)
    qkv = jnp.dot(h.astype(BF16), w_ref[...], preferred_element_type=F32)
    cos, sa, sb = cos_ref[0], sa_ref[0], sb_ref[0]
    quarter = HEAD_DIM // 4

    def head(x, g):
        xn = _rms(x, g)
        return xn * cos + pltpu.roll(xn, HEAD_DIM - quarter, 1) * sa + pltpu.roll(xn, quarter, 1) * sb

    scale = HEAD_DIM ** -0.5
    for hd in range(N_HEADS):
        sl = slice(hd * HEAD_DIM, (hd + 1) * HEAD_DIM)
        q_ref[:, sl] = (head(qkv[:, sl], qg_ref[...]) * scale).astype(BF16)
    for kv in range(N_KV_HEADS):
        sl = slice(kv * HEAD_DIM, (kv + 1) * HEAD_DIM)
        kr = head(qkv[:, Q_DIM + kv * HEAD_DIM:Q_DIM + (kv + 1) * HEAD_DIM], kg_ref[...])
        kf_ref[:, sl] = kr
        kb_ref[:, sl] = kr.astype(BF16)
    v = qkv[:, Q_DIM + KV_DIM:]
    vf_ref[...] = v
    vb_ref[...] = v.astype(BF16)


def _attn_qkv(y, mods, mrow, g1, w_qkv, q_g, k_g, rope):
    cos_t, sa_t, sb_t = rope
    rspec = pl.BlockSpec((1, TM, HEAD_DIM), lambda j, mr, ri: (ri[j], 0, 0))
    return pl.pallas_call(
        _attn_qkv_kernel,
        out_shape=(jax.ShapeDtypeStruct((N_TOK, Q_DIM), BF16), jax.ShapeDtypeStruct((N_TOK, KV_DIM), BF16),
                   jax.ShapeDtypeStruct((N_TOK, KV_DIM), BF16), jax.ShapeDtypeStruct((N_TOK, KV_DIM), F32),
                   jax.ShapeDtypeStruct((N_TOK, KV_DIM), F32)),
        grid_spec=pltpu.PrefetchScalarGridSpec(
            num_scalar_prefetch=2, grid=(NB,),
            in_specs=[_tok_spec(D), _mod_spec(), _full_spec((1, D)), _full_spec((D, QKV_DIM)),
                      _full_spec((1, HEAD_DIM)), _full_spec((1, HEAD_DIM)), rspec, rspec, rspec],
            out_specs=(_tok_spec(Q_DIM), _tok_spec(KV_DIM), _tok_spec(KV_DIM), _tok_spec(KV_DIM), _tok_spec(KV_DIM))),
        compiler_params=_cparams(),
        name="attn_qkv",
    )(mrow, jnp.asarray(_ROPE_IDX), y, mods, g1, w_qkv, q_g, k_g, cos_t, sa_t, sb_t)


def _attn_heads(q, ks, vs, o_scr):
    nt = (((1,), (1,)), ((), ()))
    for hd in range(N_HEADS):
        g = hd // GQA_GROUP
        qh = q[:, hd * HEAD_DIM:(hd + 1) * HEAD_DIM]
        gs = slice(g * HEAD_DIM, (g + 1) * HEAD_DIM)
        ss = [lax.dot_general(qh, k[:, gs], nt, preferred_element_type=F32) for k in ks]
        m = functools.reduce(jnp.maximum, [jnp.max(s, axis=-1, keepdims=True) for s in ss])
        ps = [jnp.exp(s - m) for s in ss]
        l = functools.reduce(lambda a, b: a + b, [jnp.sum(p, axis=-1, keepdims=True) for p in ps])
        o = functools.reduce(lambda a, b: a + b,
                             [jnp.dot(p.astype(BF16), v[:, gs], preferred_element_type=F32) for p, v in zip(ps, vs)])
        o_scr[:, hd * HEAD_DIM:(hd + 1) * HEAD_DIM] = (o / l).astype(BF16)


def _attn_ctx_kernel(q_ref, k_ref, v_ref, wo_ref, y_ref, mod_ref, o_ref, o_scr):
    _attn_heads(q_ref[...], [k_ref[...]], [v_ref[...]], o_scr)
    out = jnp.dot(o_scr[...], wo_ref[...], preferred_element_type=F32)
    o_ref[...] = y_ref[...] + mod_ref[0][2:3] * out


def _attn_lat_kernel(q_ref, k_ref, v_ref, ck_ref, cv_ref, wo_ref, y_ref, mod_ref, ctx_out_ref, o_ref, o_scr):
    del ctx_out_ref
    _attn_heads(q_ref[...], [k_ref[...], ck_ref[0].astype(BF16)], [v_ref[...], cv_ref[0].astype(BF16)], o_scr)
    out = jnp.dot(o_scr[...], wo_ref[...], preferred_element_type=F32)
    o_ref[...] = y_ref[...] + mod_ref[0][2:3] * out


def _attention(q, kb, vb, cache_k, cache_v, w_o, y, mods, layer):
    y_ctx = pl.pallas_call(
        _attn_ctx_kernel,
        out_shape=jax.ShapeDtypeStruct((N_TOK, D), F32),
        grid=(BATCH,),
        in_specs=[
            pl.BlockSpec((SEQ, Q_DIM), lambda s: (s, 0)),
            pl.BlockSpec((SEQ, KV_DIM), lambda s: (s, 0)),
            pl.BlockSpec((SEQ, KV_DIM), lambda s: (s, 0)),
            pl.BlockSpec((Q_DIM, D), lambda s: (0, 0)),
            pl.BlockSpec((SEQ, D), lambda s: (s, 0)),
            pl.BlockSpec((1, 6, D), lambda s: (layer * MOD_ROWS, 0, 0)),
        ],
        out_specs=pl.BlockSpec((SEQ, D), lambda s: (s, 0)),
        scratch_shapes=[pltpu.VMEM((SEQ, Q_DIM), BF16)],
        compiler_params=_cparams(),
        name="attn_ctx",
    )(q, kb, vb, w_o, y, mods)
    pb = NP_TOK // DEC_SEQ
    return pl.pallas_call(
        _attn_lat_kernel,
        out_shape=jax.ShapeDtypeStruct((N_TOK, D), F32),
        input_output_aliases={8: 0},
        grid=(DEC_BATCH, BLK_PER_DEC),
        in_specs=[
            pl.BlockSpec((TM, Q_DIM), lambda b, t: (NBP + b * BLK_PER_DEC + t, 0)),
            pl.BlockSpec((DEC_SEQ, KV_DIM), lambda b, t: (pb + b, 0)),
            pl.BlockSpec((DEC_SEQ, KV_DIM), lambda b, t: (pb + b, 0)),
            pl.BlockSpec((1, PAST_LEN, KV_DIM), lambda b, t: (b, 0, 0)),
            pl.BlockSpec((1, PAST_LEN, KV_DIM), lambda b, t: (b, 0, 0)),
            pl.BlockSpec((Q_DIM, D), lambda b, t: (0, 0)),
            pl.BlockSpec((TM, D), lambda b, t: (NBP + b * BLK_PER_DEC + t, 0)),
            pl.BlockSpec((1, 6, D), lambda b, t: (layer * MOD_ROWS + 1 + b, 0, 0)),
            pl.BlockSpec(memory_space=pl.ANY),
        ],
        out_specs=pl.BlockSpec((TM, D), lambda b, t: (NBP + b * BLK_PER_DEC + t, 0)),
        scratch_shapes=[pltpu.VMEM((TM, Q_DIM), BF16)],
        compiler_params=_cparams(2),
        name="attn_lat",
    )(q, kb, vb, cache_k, cache_v, w_o, y, mods, y_ctx)


def _log_sigmoid(x):
    return jnp.minimum(x, 0.0) - jnp.log(1.0 + jnp.exp(-jnp.abs(x)))


def _mlstm_in_kernel(mr_ref, y_ref, mod_ref, g_ref, wf_ref, wg_ref, bg_ref, q_ref, k_ref, v_ref, o_ref, gt_ref,
                     w_ref):
    @pl.when(pl.program_id(0) == 0)
    def _():
        for c in range(4):
            w_ref[:, c * D:(c + 1) * D] = wf_ref[:, c * D:(c + 1) * D].astype(BF16)

    h = _norm_mod(y_ref[...], g_ref[...], mod_ref[0], 0)
    hb = h.astype(BF16)
    q_ref[...] = jnp.dot(hb, w_ref[:, 0:D], preferred_element_type=F32).astype(BF16)
    k_ref[...] = (jnp.dot(hb, w_ref[:, D:2 * D], preferred_element_type=F32) * (M_HEAD_DIM ** -0.5)).astype(BF16)
    v_ref[...] = jnp.dot(hb, w_ref[:, 2 * D:3 * D], preferred_element_type=F32).astype(BF16)
    o_ref[...] = _sigmoid(jnp.dot(hb, w_ref[:, 3 * D:4 * D], preferred_element_type=F32))
    gt = _dot_hi_lo(h, wg_ref) + bg_ref[...]
    lane = lax.broadcasted_iota(I32, gt.shape, 1)
    is_f = ((lane >= M_HEADS) & (lane < 2 * M_HEADS)) | ((lane >= 3 * M_HEADS) & (lane < 4 * M_HEADS))
    gt_ref[...] = jnp.where(is_f, _log_sigmoid(gt), gt)


def _mlstm_in(y, mods, mrow, g1, w_in, w_gate, b_gate):
    w_spec = pl.BlockSpec(w_in.shape, lambda j, *_: (0, 0), pipeline_mode=pl.Buffered(1))
    return pl.pallas_call(
        _mlstm_in_kernel,
        out_shape=(jax.ShapeDtypeStruct((N_TOK, D), BF16), jax.ShapeDtypeStruct((N_TOK, D), BF16),
                   jax.ShapeDtypeStruct((N_TOK, D), BF16), jax.ShapeDtypeStruct((N_TOK, D), F32),
                   jax.ShapeDtypeStruct((N_TOK, LANE), F32)),
        grid_spec=pltpu.PrefetchScalarGridSpec(
            num_scalar_prefetch=1, grid=(NB,),
            in_specs=[_tok_spec(D), _mod_spec(), _full_spec((1, D)), w_spec,
                      _full_spec((D, 2 * LANE)), _full_spec((1, LANE))],
            out_specs=(_tok_spec(D), _tok_spec(D), _tok_spec(D), _tok_spec(D), _tok_spec(LANE)),
            scratch_shapes=[pltpu.VMEM((D, 4 * D), BF16)]),
        compiler_params=_cparams(),
        name="mlstm_in",
    )(mrow, y, mods, g1, w_in, w_gate, b_gate)


def _mlstm_load(hd, c, q_ref, k_ref, v_ref, gc_ref, gr_ref):
    r0 = pl.multiple_of(c * M_CHUNK, M_CHUNK)
    hs = slice(hd * M_HEAD_DIM, (hd + 1) * M_HEAD_DIM)
    rows = pl.ds(r0, M_CHUNK)
    return rows, hs, q_ref[rows, hs], k_ref[rows, hs], v_ref[rows, hs], gc_ref[hd, rows, :], gr_ref[hd, c]


def _mlstm_chunks(chains, ms, loaded, c_scr, n_scr):
    L = M_CHUNK
    n = range(len(chains))
    t_idx = lax.broadcasted_iota(I32, (L, L), 0)
    s_idx = lax.broadcasted_iota(I32, (L, L), 1)
    masks = {0: (s_idx <= t_idx, t_idx <= s_idx), 1: (s_idx >= t_idx, t_idx >= s_idx)}
    q = [ld[2] for ld in loaded]
    k = [ld[3] for ld in loaded]
    v = [ld[4] for ld in loaded]
    i_col = [ld[5][:, 2 * d:2 * d + 1] for (_, d), ld in zip(chains, loaded)]
    lf_col = [ld[5][:, 2 * d + 1:2 * d + 2] for (_, d), ld in zip(chains, loaded)]
    i_row = [ld[6][2 * d:2 * d + 1, :] for (_, d), ld in zip(chains, loaded)]
    lf_row = [ld[6][2 * d + 1:2 * d + 2, :] for (_, d), ld in zip(chains, loaded)]
    mask = [masks[d][0] for _, d in chains]
    mask_t = [masks[d][1] for _, d in chains]
    b_col = [jnp.sum(jnp.where(mask[i], lf_row[i], 0.0), axis=1, keepdims=True) for i in n]
    b_row = [jnp.sum(jnp.where(mask_t[i], lf_col[i], 0.0), axis=0, keepdims=True) for i in n]
    log_d = [jnp.where(mask[i], b_col[i] - b_row[i] + i_row[i], -jnp.inf) for i in n]
    li = [b_col[i] + ms[i] for i in n]
    m_r = [jnp.maximum(li[i], jnp.max(log_d[i], axis=1, keepdims=True)) for i in n]
    a_int = [jnp.exp(li[i] - m_r[i]) for i in n]
    dmat = [jnp.exp(log_d[i] - m_r[i]) for i in n]
    cmat = [c_scr[d, hd] for hd, d in chains]
    nvec = [n_scr[d, hd] for hd, d in chains]
    gram = [lax.dot_general(q[i], k[i], (((1,), (1,)), ((), ())), preferred_element_type=F32) for i in n]
    inter = [jnp.dot(q[i], cmat[i].astype(BF16), preferred_element_type=F32) for i in n]
    s = [gram[i] * dmat[i] for i in n]
    intra = [jnp.dot(s[i].astype(BF16), v[i], preferred_element_type=F32) for i in n]
    qn = [jnp.sum(q[i].astype(F32) * nvec[i], axis=1, keepdims=True) for i in n]
    den = [a_int[i] * qn[i] + jnp.sum(s[i], axis=1, keepdims=True) for i in n]
    hh = [(a_int[i] * inter[i] + intra[i]) / jnp.maximum(jnp.abs(den[i]), jnp.exp(-m_r[i])) for i in n]
    b_last = [b_row[i][:, L - 1:L] if chains[i][1] == 0 else b_row[i][:, 0:1] for i in n]
    log_w = [b_last[i] - b_col[i] + i_col[i] for i in n]
    m_new = [jnp.maximum(b_last[i] + ms[i], jnp.max(log_w[i], axis=0, keepdims=True)) for i in n]
    w = [jnp.exp(log_w[i] - m_new[i]) for i in n]
    decay = [jnp.exp(b_last[i] + ms[i] - m_new[i]) for i in n]
    kw = [k[i].astype(F32) * w[i] for i in n]
    kv = [lax.dot_general(kw[i].astype(BF16), v[i], (((0,), (0,)), ((), ())), preferred_element_type=F32) for i in n]
    for i, (hd, d) in enumerate(chains):
        c_scr[d, hd] = decay[i] * cmat[i] + kv[i]
        n_scr[d, hd] = decay[i] * nvec[i] + jnp.sum(kw[i], axis=0, keepdims=True)
    return hh, m_new


def _mlstm_scan_body(n_chunks, q_ref, k_ref, v_ref, gc_ref, gr_ref, h_ref, hb_scr, c_scr, n_scr, m0):
    chains = [(hd, d) for hd in range(M_HEADS) for d in range(2)]

    def body(c, ms):
        loaded = [_mlstm_load(hd, c if d == 0 else n_chunks - 1 - c, q_ref, k_ref, v_ref, gc_ref, gr_ref)
                  for hd, d in chains]
        hh, m_new = _mlstm_chunks(chains, ms, loaded, c_scr, n_scr)
        for (hd, d), ld, h in zip(chains, loaded, hh):
            dst = h_ref if d == 0 else hb_scr
            dst[ld[0], ld[1]] = h
        return tuple(m_new)

    ms = lax.fori_loop(0, n_chunks, body, tuple(m0))
    h_ref[...] += hb_scr[...]
    return ms


def _mlstm_scan_ctx_kernel(q_ref, k_ref, v_ref, gc_ref, gr_ref, h_ref, cn_ref, nn_ref, mn_ref, hb_scr, c_scr, n_scr):
    c_scr[...] = jnp.zeros(c_scr.shape, F32)
    n_scr[...] = jnp.zeros(n_scr.shape, F32)
    zero = jnp.zeros((1, 1), F32)
    ms = _mlstm_scan_body(SEQ // M_CHUNK, q_ref, k_ref, v_ref, gc_ref, gr_ref, h_ref, hb_scr, c_scr, n_scr,
                          [zero] * (2 * M_HEADS))
    cn_ref[0] = c_scr[...]
    nn_ref[0] = n_scr[...]
    for hd in range(M_HEADS):
        for d in range(2):
            mn_ref[0, d, hd] = jnp.broadcast_to(ms[2 * hd + d], (1, LANE))


def _mlstm_scan_lat_kernel(q_ref, k_ref, v_ref, gc_ref, gr_ref, c0_ref, n0_ref, m0_ref, ctx_out_ref, h_ref,
                           hb_scr, c_scr, n_scr):
    del ctx_out_ref
    c_scr[...] = c0_ref[0]
    n_scr[...] = n0_ref[0]
    m0 = [m0_ref[0, d, hd] for hd in range(M_HEADS) for d in range(2)]
    _mlstm_scan_body(DEC_SEQ // M_CHUNK, q_ref, k_ref, v_ref, gc_ref, gr_ref, h_ref, hb_scr, c_scr, n_scr, m0)


def _mlstm_scan(q, k, v, gates, state_c, state_n, state_m):
    g16 = gates[:, :4 * M_HEADS].reshape(N_TOK, 4, M_HEADS)
    gcol = jnp.transpose(g16, (2, 0, 1))
    grow = jnp.transpose(g16.reshape(N_TOK // M_CHUNK, M_CHUNK, 4, M_HEADS), (3, 0, 2, 1))
    hd = M_HEAD_DIM
    state_scratch = [pltpu.VMEM((2, M_HEADS, hd, hd), F32), pltpu.VMEM((2, M_HEADS, 1, hd), F32)]
    ncp = SEQ // M_CHUNK
    h_ctx, new_c, new_n, new_m = pl.pallas_call(
        _mlstm_scan_ctx_kernel,
        out_shape=(jax.ShapeDtypeStruct((N_TOK, D), F32),
                   jax.ShapeDtypeStruct((BATCH, 2, M_HEADS, hd, hd), F32),
                   jax.ShapeDtypeStruct((BATCH, 2, M_HEADS, 1, hd), F32),
                   jax.ShapeDtypeStruct((BATCH, 2, M_HEADS, 1, LANE), F32)),
        grid=(BATCH,),
        in_specs=[
            pl.BlockSpec((SEQ, D), lambda s: (s, 0)),
            pl.BlockSpec((SEQ, D), lambda s: (s, 0)),
            pl.BlockSpec((SEQ, D), lambda s: (s, 0)),
            pl.BlockSpec((M_HEADS, SEQ, 4), lambda s: (0, s, 0)),
            pl.BlockSpec((M_HEADS, ncp, 4, M_CHUNK), lambda s: (0, s, 0, 0)),
        ],
        out_specs=(
            pl.BlockSpec((SEQ, D), lambda s: (s, 0)),
            pl.BlockSpec((1, 2, M_HEADS, hd, hd), lambda s: (s, 0, 0, 0, 0)),
            pl.BlockSpec((1, 2, M_HEADS, 1, hd), lambda s: (s, 0, 0, 0, 0)),
            pl.BlockSpec((1, 2, M_HEADS, 1, LANE), lambda s: (s, 0, 0, 0, 0)),
        ),
        scratch_shapes=[pltpu.VMEM((SEQ, D), F32)] + state_scratch,
        compiler_params=_cparams(),
        name="mlstm_scan_ctx",
    )(q, k, v, gcol, grow)
    ncl = DEC_SEQ // M_CHUNK
    pb = NP_TOK // DEC_SEQ
    h_all = pl.pallas_call(
        _mlstm_scan_lat_kernel,
        out_shape=jax.ShapeDtypeStruct((N_TOK, D), F32),
        input_output_aliases={8: 0},
        grid=(DEC_BATCH,),
        in_specs=[
            pl.BlockSpec((DEC_SEQ, D), lambda b: (pb + b, 0)),
            pl.BlockSpec((DEC_SEQ, D), lambda b: (pb + b, 0)),
            pl.BlockSpec((DEC_SEQ, D), lambda b: (pb + b, 0)),
            pl.BlockSpec((M_HEADS, DEC_SEQ, 4), lambda b: (0, pb + b, 0)),
            pl.BlockSpec((M_HEADS, ncl, 4, M_CHUNK), lambda b: (0, pb + b, 0, 0)),
            pl.BlockSpec((1, 2, M_HEADS, hd, hd), lambda b: (b, 0, 0, 0, 0)),
            pl.BlockSpec((1, 2, M_HEADS, 1, hd), lambda b: (b, 0, 0, 0, 0)),
            pl.BlockSpec((1, 2, M_HEADS, 1, 1), lambda b: (b, 0, 0, 0, 0)),
            pl.BlockSpec(memory_space=pl.ANY),
        ],
        out_specs=pl.BlockSpec((DEC_SEQ, D), lambda b: (pb + b, 0)),
        scratch_shapes=[pltpu.VMEM((DEC_SEQ, D), F32)] + state_scratch,
        compiler_params=_cparams(),
        name="mlstm_scan_lat",
    )(q, k, v, gcol, grow, state_c, state_n, state_m, h_ctx)
    return h_all, new_c, new_n, new_m


def _mlstm_out_kernel(mr_ref, h_ref, o_ref, ng_ref, w_ref, y_ref, mod_ref, out_ref, x_scr):
    hc = o_ref[...] * h_ref[...]
    for hd in range(M_HEADS):
        sl = slice(hd * M_HEAD_DIM, (hd + 1) * M_HEAD_DIM)
        x_scr[:, sl] = _rms(hc[:, sl], ng_ref[:, sl]).astype(BF16)
    out = jnp.dot(x_scr[...], w_ref[...], preferred_element_type=F32)
    out_ref[...] = y_ref[...] + mod_ref[0][2:3] * out


def _mlstm_out(hsum, o, norm_g, w_out, y, mods, mrow):
    return pl.pallas_call(
        _mlstm_out_kernel,
        out_shape=jax.ShapeDtypeStruct((N_TOK, D), F32),
        grid_spec=pltpu.PrefetchScalarGridSpec(
            num_scalar_prefetch=1, grid=(NB,),
            in_specs=[_tok_spec(D), _tok_spec(D), _full_spec((1, D)), _full_spec((D, D)), _tok_spec(D), _mod_spec()],
            out_specs=_tok_spec(D),
            scratch_shapes=[pltpu.VMEM((TM, D), BF16)]),
        compiler_params=_cparams(),
        name="mlstm_out",
    )(mrow, hsum, o, norm_g, w_out, y, mods)


ROUTE_OFF = N_GROUPS
SLAB = D // (2 * LANE)
V7X_SC_CORES = 2
V7X_SC_SUBCORES = 16
SC_WORKERS = V7X_SC_CORES * V7X_SC_SUBCORES
SC_WINDOW = 128
HI_MASK = -65536


def _bf16_bits(x):
    return lax.bitcast_convert_type(x.astype(BF16).astype(F32), I32)


def _store_slabs(ref, x):
    rows = x.shape[0]
    for c in range(SLAB):
        lo = lax.shift_right_logical(_bf16_bits(x[:, (2 * c) * LANE:(2 * c + 1) * LANE]), 16)
        hi = _bf16_bits(x[:, (2 * c + 1) * LANE:(2 * c + 2) * LANE]) & HI_MASK
        ref[pl.ds(c, rows, stride=SLAB), :] = lo | hi


def _load_slabs(ref, dst, rows, dtype):
    for c in range(SLAB):
        w = ref[pl.ds(c, rows, stride=SLAB), :]
        lo = lax.bitcast_convert_type(lax.shift_left(w, 16), F32)
        hi = lax.bitcast_convert_type(w & HI_MASK, F32)
        dst[:, (2 * c) * LANE:(2 * c + 1) * LANE] = lo.astype(dtype)
        dst[:, (2 * c + 1) * LANE:(2 * c + 2) * LANE] = hi.astype(dtype)


def _route_kernel(mr_ref, y_ref, mod_ref, g_ref, wr_ref, br_ref, x_ref, id_ref, wt_ref):
    x = _norm_mod(y_ref[...], g_ref[...], mod_ref[0], 1)
    _store_slabs(x_ref, x)
    lg = _dot_hi_lo(x, wr_ref) + br_ref[...]
    lane = lax.broadcasted_iota(I32, lg.shape, 1).astype(F32)
    ninf = -jnp.inf
    big = float(LANE)
    lgg = jnp.where(lane < N_GROUPS, lg, ninf)
    gmax = jnp.max(lgg, axis=-1, keepdims=True)
    g_idx = jnp.min(jnp.where(lgg == gmax, lane, big), axis=-1, keepdims=True)
    g_w = 1.0 / jnp.sum(jnp.exp(lgg - gmax), axis=-1, keepdims=True)
    lo = ROUTE_OFF + g_idx * EXPERTS_PER_GROUP
    le = jnp.where((lane >= lo) & (lane < lo + EXPERTS_PER_GROUP), lg, ninf)
    m1 = jnp.max(le, axis=-1, keepdims=True)
    i1 = jnp.min(jnp.where(le == m1, lane, big), axis=-1, keepdims=True)
    le2 = jnp.where(lane == i1, ninf, le)
    m2 = jnp.max(le2, axis=-1, keepdims=True)
    i2 = jnp.min(jnp.where(le2 == m2, lane, big), axis=-1, keepdims=True)
    r = jnp.exp(m2 - m1)
    p1 = 1.0 / (1.0 + r)
    p2 = r / (1.0 + r)
    two = lax.broadcasted_iota(I32, (x.shape[0], TOP_K), 1)
    id_ref[...] = (jnp.where(two == 0, i1, i2) - ROUTE_OFF).astype(I32)
    wt_ref[...] = jnp.where(two == 0, g_w * p1, g_w * p2)


def _route(y, mods, mrow, g2, w_route, b_route):
    return pl.pallas_call(
        _route_kernel,
        out_shape=(jax.ShapeDtypeStruct((N_TOK * SLAB, LANE), I32), jax.ShapeDtypeStruct((N_TOK, TOP_K), I32),
                   jax.ShapeDtypeStruct((N_TOK, TOP_K), F32)),
        grid_spec=pltpu.PrefetchScalarGridSpec(
            num_scalar_prefetch=1, grid=(NB,),
            in_specs=[_tok_spec(D), _mod_spec(), _full_spec((1, D)), _full_spec((D, 2 * LANE)), _full_spec((1, LANE))],
            out_specs=(pl.BlockSpec((TM * SLAB, LANE), lambda j, *_: (j, 0)), _tok_spec(TOP_K), _tok_spec(TOP_K))),
        compiler_params=_cparams(),
        name="moe_route",
    )(mrow, y, mods, g2, w_route, b_route)


def _dispatch_tables(expert_id):
    flat_e = expert_id.reshape(-1)
    onehot = (flat_e[:, None] == jnp.arange(N_EXPERTS, dtype=I32)[None, :]).astype(I32)
    csum = jnp.cumsum(onehot, axis=0)
    counts = csum[-1]
    padded = ((counts + EBLK - 1) // EBLK) * EBLK
    pad_end = jnp.cumsum(padded)
    pad_start = pad_end - padded
    dest = jnp.sum((csum - 1 + pad_start[None, :]) * onehot, axis=1).astype(I32)
    n_blk = (padded // EBLK).astype(I32)
    blk_start = (pad_start // EBLK).astype(I32)
    n_used = (pad_end[-1] // EBLK).astype(I32).reshape(1)
    dest2 = dest.reshape(N_TOK, TOP_K)
    return dest2[:, 0].reshape(1, N_TOK), dest2[:, 1].reshape(1, N_TOK), blk_start, n_blk, n_used


def _sc_mesh():
    return plsc.VectorSubcoreMesh(core_axis_name="core", subcore_axis_name="subcore",
                                  num_cores=V7X_SC_CORES, num_subcores=V7X_SC_SUBCORES)


def _sc_worker():
    return lax.axis_index("core") * V7X_SC_SUBCORES + lax.axis_index("subcore")


def _sc_dispatch(x_slabs, d0, d1):
    per = N_TOK // SC_WORKERS

    @functools.partial(
        pl.kernel, out_type=jax.ShapeDtypeStruct((P_SLOTS, SLAB, LANE), I32), mesh=_sc_mesh(), name="moe_dispatch",
        scratch_types=[pltpu.VMEM((1, per), I32), pltpu.VMEM((1, per), I32), pltpu.VMEM((SC_WINDOW, SLAB, LANE), I32)])
    def run(x_hbm, d0_hbm, d1_hbm, o_hbm, i0_v, i1_v, buf):
        base = _sc_worker() * per
        pltpu.sync_copy(d0_hbm.at[:, pl.ds(base, per)], i0_v)
        pltpu.sync_copy(d1_hbm.at[:, pl.ds(base, per)], i1_v)

        @pl.loop(0, per // SC_WINDOW)
        def _(s):
            off = s * SC_WINDOW
            pltpu.sync_copy(x_hbm.at[pl.ds(base + off, SC_WINDOW)], buf)
            pltpu.sync_copy(buf, o_hbm.at[i0_v.at[0, pl.ds(off, SC_WINDOW)]])
            pltpu.sync_copy(buf, o_hbm.at[i1_v.at[0, pl.ds(off, SC_WINDOW)]])

    return run(x_slabs.reshape(N_TOK, SLAB, LANE), d0, d1)


def _sc_collect(y_slabs, dcat):
    per = N_ASSIGN // SC_WORKERS

    @functools.partial(
        pl.kernel, out_type=jax.ShapeDtypeStruct((N_ASSIGN, SLAB, LANE), I32), mesh=_sc_mesh(), name="moe_collect",
        scratch_types=[pltpu.VMEM((1, per), I32), pltpu.VMEM((SC_WINDOW, SLAB, LANE), I32)])
    def run(y_hbm, i_hbm, o_hbm, i_v, buf):
        base = _sc_worker() * per
        pltpu.sync_copy(i_hbm.at[:, pl.ds(base, per)], i_v)

        @pl.loop(0, per // SC_WINDOW)
        def _(s):
            off = s * SC_WINDOW
            pltpu.sync_copy(y_hbm.at[i_v.at[0, pl.ds(off, SC_WINDOW)]], buf)
            pltpu.sync_copy(buf, o_hbm.at[pl.ds(base + off, SC_WINDOW)])

    return run(y_slabs.reshape(P_SLOTS, SLAB, LANE), dcat)


EROWS = EBLK * SLAB


W_CHUNKS = 8


def _expert_kernel(layer, bs_ref, nb_ref, nu_ref, wg_hbm, wu_hbm, wd_hbm, x_hbm, y_hbm,
                   xbuf, ybuf, xs, wg_f, wu_f, wd_f, wg_bf, wu_bf, wd_bf, isem, osem, wsem):
    e = pl.program_id(0)
    n_exp = pl.num_programs(0)
    n_used = nu_ref[0]
    b0 = bs_ref[e]
    nb = nb_ref[e]
    wslot = lax.rem(e, 2)

    def weight_copies(ex, slot):
        out = []
        for hbm, buf in ((wg_hbm, wg_f), (wu_hbm, wu_f), (wd_hbm, wd_f)):
            rows = buf.shape[1] // W_CHUNKS
            for c in range(W_CHUNKS):
                rs = pl.ds(c * rows, rows)
                out.append(pltpu.make_async_copy(hbm.at[layer, ex, rs], buf.at[slot, rs], wsem.at[slot]))
        return out

    def start_weights(ex, slot):
        for i, cp in enumerate(weight_copies(ex, slot)):
            cp.start(priority=i % 2)

    @pl.when(e == 0)
    def _():
        start_weights(0, 0)

    for cp in weight_copies(e, wslot):
        cp.wait()

    @pl.when(e + 1 < n_exp)
    def _():
        start_weights(e + 1, 1 - wslot)

    def in_copy(g, slot):
        return pltpu.make_async_copy(x_hbm.at[pl.ds(pl.multiple_of(g * EROWS, EROWS), EROWS)], xbuf.at[slot],
                                     isem.at[slot])

    def out_copy(g, slot):
        return pltpu.make_async_copy(ybuf.at[slot], y_hbm.at[pl.ds(pl.multiple_of(g * EROWS, EROWS), EROWS)],
                                     osem.at[slot])

    @pl.when(e == 0)
    def _():
        in_copy(0, 0).start(priority=1)

    @pl.when(nb > 0)
    def _():
        wg_bf[...] = wg_f[wslot].astype(BF16)
        wu_bf[...] = wu_f[wslot].astype(BF16)
        wd_bf[...] = wd_f[wslot].astype(BF16)

    def block(k, carry):
        g = b0 + k
        slot = lax.rem(g, 2)
        in_copy(g, slot).wait()

        @pl.when(g + 1 < n_used)
        def _():
            in_copy(g + 1, 1 - slot).start(priority=1)

        _load_slabs(xbuf.at[slot], xs, EBLK, BF16)
        xb = xs[...]
        gt = jnp.dot(xb, wg_bf[...], preferred_element_type=F32)
        up = jnp.dot(xb, wu_bf[...], preferred_element_type=F32)
        hmid = (gt * _sigmoid(gt) * up).astype(BF16)
        res = jnp.dot(hmid, wd_bf[...], preferred_element_type=F32)

        @pl.when(g >= 2)
        def _():
            out_copy(g - 2, slot).wait()

        _store_slabs(ybuf.at[slot], res)
        out_copy(g, slot).start()
        return carry

    lax.fori_loop(0, nb, block, 0)

    @pl.when(e == n_exp - 1)
    def _():
        last = n_used - 1
        out_copy(last, lax.rem(last, 2)).wait()

        @pl.when(n_used >= 2)
        def _():
            out_copy(last - 1, lax.rem(last - 1, 2)).wait()


def _experts(x_sorted, blk_start, n_blk, n_used, w_gate, w_up, w_down, layer):
    any_spec = pl.BlockSpec(memory_space=pl.ANY)
    return pl.pallas_call(
        functools.partial(_expert_kernel, layer),
        out_shape=jax.ShapeDtypeStruct((P_SLOTS * SLAB, LANE), I32),
        grid_spec=pltpu.PrefetchScalarGridSpec(
            num_scalar_prefetch=3, grid=(N_EXPERTS,),
            in_specs=[any_spec, any_spec, any_spec, any_spec],
            out_specs=any_spec,
            scratch_shapes=[
                pltpu.VMEM((2, EROWS, LANE), I32), pltpu.VMEM((2, EROWS, LANE), I32),
                pltpu.VMEM((EBLK, D), BF16),
                pltpu.VMEM((2, D, D_EXPERT), F32), pltpu.VMEM((2, D, D_EXPERT), F32), pltpu.VMEM((2, D_EXPERT, D), F32),
                pltpu.VMEM((D, D_EXPERT), BF16), pltpu.VMEM((D, D_EXPERT), BF16), pltpu.VMEM((D_EXPERT, D), BF16),
                pltpu.SemaphoreType.DMA((2,)), pltpu.SemaphoreType.DMA((2,)), pltpu.SemaphoreType.DMA((2,)),
            ]),
        compiler_params=_cparams(),
        name="moe_experts",
    )(blk_start, n_blk, n_used, w_gate, w_up, w_down, x_sorted.reshape(P_SLOTS * SLAB, LANE))


def _combine_kernel(final, mr_ref, e0_ref, e1_ref, wt_ref, y_ref, mod_ref, fg_ref, o_ref, a_scr, b_scr):
    _load_slabs(e0_ref, a_scr, TM, F32)
    _load_slabs(e1_ref, b_scr, TM, F32)
    wt = wt_ref[...]
    moe = wt[:, 0:1] * a_scr[...] + wt[:, 1:2] * b_scr[...]
    y_new = y_ref[...] + mod_ref[0][5:6] * moe
    o_ref[...] = _rms(y_new, fg_ref[...]) if final else y_new


def _combine(ym, wts, y, mods, mrow, final_g, blk0, nblk, final):
    tok = lambda width: pl.BlockSpec((TM, width), lambda j, *_: (blk0 + j, 0))
    slab0 = pl.BlockSpec((TM * SLAB, LANE), lambda j, *_: (blk0 + j, 0))
    slab1 = pl.BlockSpec((TM * SLAB, LANE), lambda j, *_: (NB + blk0 + j, 0))
    mod = pl.BlockSpec((1, 6, D), lambda j, mr: (mr[blk0 + j], 0, 0))
    return pl.pallas_call(
        functools.partial(_combine_kernel, final),
        out_shape=jax.ShapeDtypeStruct((nblk * TM, D), F32),
        grid_spec=pltpu.PrefetchScalarGridSpec(
            num_scalar_prefetch=1, grid=(nblk,),
            in_specs=[slab0, slab1, tok(TOP_K), tok(D), mod, _full_spec((1, D))],
            out_specs=pl.BlockSpec((TM, D), lambda j, *_: (j, 0)),
            scratch_shapes=[pltpu.VMEM((TM, D), F32), pltpu.VMEM((TM, D), F32)]),
        compiler_params=_cparams(),
        name="moe_combine",
    )(mrow, ym, ym, wts, y, mods, final_g)


def kernel(x_prompt, x_sample, cache_attn_k, cache_attn_v, state_mlstm_C, state_mlstm_n, state_mlstm_m, c, c_ctx, ada_w, ada_b, norm1_g, norm2_g, conv_w_in, conv_w_dw, conv_b_dw, conv_ln_g, conv_ln_b, conv_w_out, attn_w_qkv, attn_q_norm, attn_k_norm, attn_w_o, mlstm_w_in, mlstm_b_gate, mlstm_norm_g, mlstm_w_out, moe_w_group, moe_b_group, moe_w_router, moe_b_router, moe_w_gate, moe_w_up, moe_w_down, final_norm_g):
    y = jnp.concatenate([x_prompt.reshape(NP_TOK, D), x_sample.reshape(NS_TOK, D)], axis=0)
    cvec = jnp.concatenate([c_ctx[None, :], c, jnp.zeros((MOD_ROWS - 1 - DEC_BATCH, D), F32)], axis=0)
    mods = _ada_all(cvec, ada_w, ada_b)
    rope = _rope_blocks()
    new_k = new_v = new_c = new_n = new_m = None
    for i in range(DEPTH):
        kind, slot = i % 3, i // 3
        mrow = jnp.asarray(_MOD_ROW + i * MOD_ROWS)
        g1 = norm1_g[i].reshape(1, D)
        if kind == 0:
            u = _conv_in(y, mods, mrow, g1, conv_w_in[slot].astype(BF16))
            w_dw = jnp.concatenate([conv_w_dw[slot], jnp.zeros((1, D), F32)], axis=0)
            y = _conv_main(u, y, mods, mrow, w_dw, conv_b_dw[slot].reshape(1, D), conv_ln_g[slot].reshape(1, D),
                           conv_ln_b[slot].reshape(1, D), conv_w_out[slot].astype(BF16))
        elif kind == 1:
            q, kb, vb, kf, vf = _attn_qkv(y, mods, mrow, g1, attn_w_qkv[slot].astype(BF16),
                                          attn_q_norm[slot].reshape(1, HEAD_DIM), attn_k_norm[slot].reshape(1, HEAD_DIM),
                                          rope)
            new_k = kf[:NP_TOK].reshape(BATCH, 1, SEQ, N_KV_HEADS, HEAD_DIM)
            new_v = vf[:NP_TOK].reshape(BATCH, 1, SEQ, N_KV_HEADS, HEAD_DIM)
            ck = cache_attn_k[:, slot].reshape(DEC_BATCH, PAST_LEN, KV_DIM)
            cv = cache_attn_v[:, slot].reshape(DEC_BATCH, PAST_LEN, KV_DIM)
            y = _attention(q, kb, vb, ck, cv, attn_w_o[slot].astype(BF16), y, mods, i)
        else:
            w_in = mlstm_w_in[slot]
            w_gate = jnp.concatenate([w_in[:, 4 * D:], jnp.zeros((D, LANE - 4 * M_HEADS), F32)], axis=1)
            b_gate = jnp.concatenate([mlstm_b_gate[slot], jnp.zeros((LANE - 4 * M_HEADS,), F32)]).reshape(1, LANE)
            q, k, v, o, gates = _mlstm_in(y, mods, mrow, g1, w_in, _split_hi_lo(w_gate), b_gate)
            sc = state_mlstm_C[:, slot]
            sn = state_mlstm_n[:, slot].reshape(DEC_BATCH, 2, M_HEADS, 1, M_HEAD_DIM)
            sm = state_mlstm_m[:, slot].reshape(DEC_BATCH, 2, M_HEADS, 1, 1)
            hsum, nc_, nn_, nm_ = _mlstm_scan(q, k, v, gates, sc, sn, sm)
            new_c = nc_[:, None]
            new_n = nn_.reshape(BATCH, 1, 2, M_HEADS, M_HEAD_DIM)
            new_m = nm_[..., 0, 0].reshape(BATCH, 1, 2, M_HEADS)
            y = _mlstm_out(hsum, o, mlstm_norm_g[slot].reshape(1, D), mlstm_w_out[slot].astype(BF16), y, mods, mrow)
        w_route = jnp.concatenate([moe_w_group[i], moe_w_router[i],
                                   jnp.zeros((D, LANE - N_GROUPS - N_EXPERTS), F32)], axis=1)
        b_route = jnp.concatenate([moe_b_group[i], moe_b_router[i],
                                   jnp.zeros((LANE - N_GROUPS - N_EXPERTS,), F32)]).reshape(1, LANE)
        x2, eid, ewt = _route(y, mods, mrow, norm2_g[i].reshape(1, D), _split_hi_lo(w_route), b_route)
        d0, d1, blk_start, n_blk, n_used = _dispatch_tables(eid)
        x_sorted = _sc_dispatch(x2, d0, d1)
        y_sorted = _experts(x_sorted, blk_start, n_blk, n_used, moe_w_gate, moe_w_up, moe_w_down, i)
        ym = _sc_collect(y_sorted, jnp.concatenate([d0, d1], axis=1))
        ym = ym.reshape(N_ASSIGN * SLAB, LANE)
        fg = final_norm_g.reshape(1, D)
        if i + 1 < DEPTH:
            y = _combine(ym, ewt, y, mods, mrow, fg, 0, NB, False)
        else:
            y_prompt = _combine(ym, ewt, y, mods, mrow, fg, 0, NBP, True).reshape(BATCH, SEQ, D)
            y_sample = _combine(ym, ewt, y, mods, mrow, fg, NBP, NB - NBP, True).reshape(DEC_BATCH, DEC_SEQ, D)
    return (y_prompt, y_sample, new_k, new_v, new_c, new_n, new_m)
```

```python
import functools

import jax
import jax.numpy as jnp
import numpy as np
from jax import lax
from jax.experimental import pallas as pl
from jax.experimental.pallas import tpu as pltpu
from jax.experimental.pallas import tpu_sc as plsc

F32 = jnp.float32
BF16 = jnp.bfloat16
I32 = jnp.int32

D = 1024
BATCH, SEQ = 16, 256
DEC_BATCH, DEC_SEQ = 8, 1024
PAST_LEN = 256
DEPTH = 4
GRID_W = 64
EPS = 1e-6
CONV_WIDTH = 31
CONV_PAD = CONV_WIDTH // 2
HEAD_DIM = 128
N_HEADS = 8
N_KV_HEADS = 2
GQA_GROUP = N_HEADS // N_KV_HEADS
Q_DIM = N_HEADS * HEAD_DIM
KV_DIM = N_KV_HEADS * HEAD_DIM
QKV_DIM = Q_DIM + 2 * KV_DIM
ROPE_THETA = 10000.0
M_HEADS = 4
M_HEAD_DIM = D // M_HEADS
M_CHUNK = 64
N_GROUPS = 4
EXPERTS_PER_GROUP = 8
N_EXPERTS = N_GROUPS * EXPERTS_PER_GROUP
TOP_K = 2
D_EXPERT = 512

NP_TOK = BATCH * SEQ
NS_TOK = DEC_BATCH * DEC_SEQ
N_TOK = NP_TOK + NS_TOK
TM = 512
NB = N_TOK // TM
NBP = NP_TOK // TM
BLK_PER_DEC = DEC_SEQ // TM
SB = 256
NSB = N_TOK // SB
NSBP = NP_TOK // SB
SB_PER_DEC = DEC_SEQ // SB
MOD_ROWS = 16
HALO = 16
LANE = 128
SUBLANE = 8

N_ASSIGN = N_TOK * TOP_K
EBLK = 256
N_EBLK = N_ASSIGN // EBLK + N_EXPERTS
P_SLOTS = N_EBLK * EBLK
N_PAD_SLOTS = P_SLOTS - N_ASSIGN

VMEM_LIMIT = 56 * 1024 * 1024


def _block_tables(nb, nbp, per_dec):
    j = np.arange(nb)
    is_p = j < nbp
    mod_row = np.where(is_p, 0, 1 + (j - nbp) // per_dec)
    rope_idx = np.where(is_p, 0, 1 + (j - nbp) % per_dec)
    first = np.where(is_p, 1, ((j - nbp) % per_dec == 0).astype(np.int64))
    last = np.where(is_p, 1, ((j - nbp) % per_dec == per_dec - 1).astype(np.int64))
    return (mod_row.astype(np.int32), rope_idx.astype(np.int32), first.astype(np.int32), last.astype(np.int32))


_MOD_ROW, _ROPE_IDX, _, _ = _block_tables(NB, NBP, BLK_PER_DEC)
_MOD_ROW_SB, _, _SEQ_FIRST, _SEQ_LAST = _block_tables(NSB, NSBP, SB_PER_DEC)


def _cparams(n_axes=1):
    return pltpu.CompilerParams(dimension_semantics=("arbitrary",) * n_axes, vmem_limit_bytes=VMEM_LIMIT)


def _sigmoid(x):
    return 1.0 / (1.0 + jnp.exp(-x))


def _rms(x, g):
    return x * lax.rsqrt(jnp.mean(x * x, axis=-1, keepdims=True) + EPS) * g


def _split_hi_lo(w):
    hi = w.astype(BF16)
    return jnp.concatenate([hi, (w - hi.astype(F32)).astype(BF16)], axis=1)


def _dot_hi_lo(x, w_ref):
    n = w_ref.shape[1] // 2
    xh = x.astype(BF16)
    xl = (x - xh.astype(F32)).astype(BF16)
    both = jnp.dot(xh, w_ref[...], preferred_element_type=F32)
    return both[:, :n] + (both[:, n:] + jnp.dot(xl, w_ref[:, :n], preferred_element_type=F32))


def _norm_mod(y, g, mod, which):
    shift = mod[3 * which:3 * which + 1]
    scale = mod[3 * which + 1:3 * which + 2]
    return _rms(y, g) * (1.0 + scale) + shift


def _ada_kernel(c_ref, w_ref, b_ref, o_ref):
    c = c_ref[...]
    s = c * _sigmoid(c)
    o_ref[0] = jnp.dot(s.astype(BF16), w_ref[0].astype(BF16), preferred_element_type=F32) + b_ref[0]


def _ada_all(cvec, ada_w, ada_b):
    tn = 1536
    out = pl.pallas_call(
        _ada_kernel,
        out_shape=jax.ShapeDtypeStruct((DEPTH, MOD_ROWS, 6 * D), F32),
        grid=(DEPTH, 6 * D // tn),
        in_specs=[
            pl.BlockSpec((MOD_ROWS, D), lambda l, n: (0, 0)),
            pl.BlockSpec((1, D, tn), lambda l, n: (l, 0, n)),
            pl.BlockSpec((1, 1, tn), lambda l, n: (l, 0, n)),
        ],
        out_specs=pl.BlockSpec((1, MOD_ROWS, tn), lambda l, n: (l, 0, n)),
        compiler_params=_cparams(2),
        name="ada_mod",
    )(cvec, ada_w, ada_b.reshape(DEPTH, 1, 6 * D))
    return out.reshape(DEPTH * MOD_ROWS, 6, D)


def _tok_spec(width):
    return pl.BlockSpec((TM, width), lambda j, *_: (j, 0))


def _mod_spec():
    return pl.BlockSpec((1, 6, D), lambda j, mr, *_: (mr[j], 0, 0))


def _full_spec(shape):
    nd = len(shape)
    return pl.BlockSpec(shape, lambda j, *_: (0,) * nd)


def _conv_in_kernel(mr_ref, y_ref, mod_ref, g_ref, w_ref, u_ref):
    h = _norm_mod(y_ref[...], g_ref[...], mod_ref[0], 0)
    ag = jnp.dot(h.astype(BF16), w_ref[...], preferred_element_type=F32)
    u_ref[...] = ag[:, :D] * _sigmoid(ag[:, D:])


def _conv_in(y, mods, mrow, g1, w_in):
    return pl.pallas_call(
        _conv_in_kernel,
        out_shape=jax.ShapeDtypeStruct((N_TOK, D), F32),
        grid_spec=pltpu.PrefetchScalarGridSpec(
            num_scalar_prefetch=1, grid=(NB,),
            in_specs=[_tok_spec(D), _mod_spec(), _full_spec((1, D)), _full_spec((D, 2 * D))],
            out_specs=_tok_spec(D)),
        compiler_params=_cparams(),
        name="conv_in",
    )(mrow, y, mods, g1, w_in)


def _conv_main_kernel(mr_ref, first_ref, last_ref, u_ref, up_ref, un_ref, wdw_ref, bdw_ref, lg_ref, lb_ref,
                      wout_ref, y_ref, mod_ref, o_ref, ext_ref, acc_ref):
    j = pl.program_id(0)
    zero = jnp.zeros((HALO, D), F32)
    ext_ref[0:HALO, :] = jnp.where(first_ref[j] == 1, zero, up_ref[...])
    ext_ref[HALO:HALO + SB, :] = u_ref[...]
    ext_ref[HALO + SB:2 * HALO + SB, :] = jnp.where(last_ref[j] == 1, zero, un_ref[...])

    off0 = HALO - CONV_PAD
    n_a = (off0 + CONV_WIDTH - 1) // SUBLANE + 1
    n_chunks = SB // SUBLANE

    def strip(ci, carry):
        cs = pl.ds(pl.multiple_of(ci * LANE, LANE), LANE)
        wk = [jnp.broadcast_to(wdw_ref[k:k + 1, cs], (SUBLANE, LANE)) for k in range(CONV_WIDTH)]
        bias = jnp.broadcast_to(bdw_ref[:, cs], (SUBLANE, LANE))
        sub = lax.broadcasted_iota(I32, (SUBLANE, LANE), 0)
        prev_rot, prev_v0 = None, None
        for j in range(n_chunks + 1):
            tiles = [ext_ref[SUBLANE * (j + a):SUBLANE * (j + a + 1), cs] for a in range(n_a)]
            part = []
            for s in range(SUBLANE):
                acc = None
                for a in range(n_a):
                    k = SUBLANE * a + s - off0
                    if (0 <= k < CONV_WIDTH) and not (s == 0 and j == n_chunks):
                        term = tiles[a] * wk[k]
                        acc = term if acc is None else acc + term
                part.append(acc)
            rot = [None] + [pltpu.roll(part[s], SUBLANE - s, 0) for s in range(1, SUBLANE)]
            if j >= 1:
                out = prev_v0 + bias
                for s in range(1, SUBLANE):
                    out = out + jnp.where(sub < SUBLANE - s, prev_rot[s], rot[s])
                acc_ref[SUBLANE * (j - 1):SUBLANE * j, cs] = out
            prev_rot, prev_v0 = rot, part[0]
        return carry

    lax.fori_loop(0, D // LANE, strip, 0)

    c = acc_ref[...]
    mu = jnp.mean(c, axis=-1, keepdims=True)
    cc = c - mu
    var = jnp.mean(cc * cc, axis=-1, keepdims=True)
    z = cc * lax.rsqrt(var + EPS) * lg_ref[...] + lb_ref[...]
    z = z * _sigmoid(z)
    out = jnp.dot(z.astype(BF16), wout_ref[...], preferred_element_type=F32)
    o_ref[...] = y_ref[...] + mod_ref[0][2:3] * out


def _conv_main(u, y, mods, mrow, w_dw, b_dw, ln_g, ln_b, w_out):
    nh = N_TOK // HALO
    per = SB // HALO
    sb_spec = pl.BlockSpec((SB, D), lambda j, *_: (j, 0))
    return pl.pallas_call(
        _conv_main_kernel,
        out_shape=jax.ShapeDtypeStruct((N_TOK, D), F32),
        grid_spec=pltpu.PrefetchScalarGridSpec(
            num_scalar_prefetch=3, grid=(NSB,),
            in_specs=[
                sb_spec,
                pl.BlockSpec((HALO, D), lambda j, *_: (jnp.maximum(j * per - 1, 0), 0)),
                pl.BlockSpec((HALO, D), lambda j, *_: (jnp.minimum((j + 1) * per, nh - 1), 0)),
                _full_spec((CONV_WIDTH + 1, D)), _full_spec((1, D)), _full_spec((1, D)), _full_spec((1, D)),
                _full_spec((D, D)), sb_spec, _mod_spec(),
            ],
            out_specs=sb_spec,
            scratch_shapes=[pltpu.VMEM((SB + 2 * HALO, D), F32), pltpu.VMEM((SB, D), F32)]),
        compiler_params=_cparams(),
        name="conv_main",
    )(mrow, jnp.asarray(_SEQ_FIRST), jnp.asarray(_SEQ_LAST), u, u, u, w_dw, b_dw, ln_g, ln_b, w_out, y, mods)


def _rope_angles():
    rows = DEC_SEQ // GRID_W
    row = jnp.repeat(jnp.arange(rows, dtype=F32), GRID_W)
    col = jnp.tile(jnp.arange(GRID_W, dtype=F32), rows)
    axis_dim = HEAD_DIM // 2
    freqs = jnp.power(ROPE_THETA, -jnp.arange(axis_dim // 2, dtype=F32) * 2.0 / axis_dim)
    ang_r = row[:, None] * freqs[None, :]
    ang_c = col[:, None] * freqs[None, :]
    return jnp.concatenate([ang_r, ang_r, ang_c, ang_c], axis=-1)


def _rope_blocks():
    ang = _rope_angles()
    cos, sin = jnp.cos(ang), jnp.sin(ang)
    lane = np.arange(HEAD_DIM)
    lo = jnp.asarray(((lane % (HEAD_DIM // 2)) < HEAD_DIM // 4).astype(np.float32))
    sin_a = -sin * lo[None, :]
    sin_b = sin * (1.0 - lo)[None, :]
    nblk = DEC_SEQ // TM
    ident = jnp.ones((1, TM, HEAD_DIM), F32)
    zeros = jnp.zeros((1, TM, HEAD_DIM), F32)
    cos_t = jnp.concatenate([ident, cos.reshape(nblk, TM, HEAD_DIM)], axis=0)
    sa_t = jnp.concatenate([zeros, sin_a.reshape(nblk, TM, HEAD_DIM)], axis=0)
    sb_t = jnp.concatenate([zeros, sin_b.reshape(nblk, TM, HEAD_DIM)], axis=0)
    return cos_t, sa_t, sb_t


def _attn_qkv_kernel(mr_ref, ri_ref, y_ref, mod_ref, g_ref, w_ref, qg_ref, kg_ref, cos_ref, sa_ref, sb_ref,
                     q_ref, kb_ref, vb_ref, kf_ref, vf_ref):
    h = _norm_mod(y_ref[...], g_ref[...], mod_ref[0], 0)
    qkv = jnp.dot(h.astype(BF16), w_ref[...], preferred_element_type=F32)
    cos, sa, sb = cos_ref[0], sa_ref[0], sb_ref[0]
    quarter = HEAD_DIM // 4

    def head(x, g):
        xn = _rms(x, g)
        return xn * cos + pltpu.roll(xn, HEAD_DIM - quarter, 1) * sa + pltpu.roll(xn, quarter, 1) * sb

    scale = HEAD_DIM ** -0.5
    for hd in range(N_HEADS):
        sl = slice(hd * HEAD_DIM, (hd + 1) * HEAD_DIM)
        q_ref[:, sl] = (head(qkv[:, sl], qg_ref[...]) * scale).astype(BF16)
    for kv in range(N_KV_HEADS):
        sl = slice(kv * HEAD_DIM, (kv + 1) * HEAD_DIM)
        kr = head(qkv[:, Q_DIM + kv * HEAD_DIM:Q_DIM + (kv + 1) * HEAD_DIM], kg_ref[...])
        kf_ref[:, sl] = kr
        kb_ref[:, sl] = kr.astype(BF16)
    v = qkv[:, Q_DIM + KV_DIM:]
    vf_ref[...] = v
    vb_ref[...] = v.astype(BF16)


def _attn_qkv(y, mods, mrow, g1, w_qkv, q_g, k_g, rope):
    cos_t, sa_t, sb_t = rope
    rspec = pl.BlockSpec((1, TM, HEAD_DIM), lambda j, mr, ri: (ri[j], 0, 0))
    return pl.pallas_call(
        _attn_qkv_kernel,
        out_shape=(jax.ShapeDtypeStruct((N_TOK, Q_DIM), BF16), jax.ShapeDtypeStruct((N_TOK, KV_DIM), BF16),
                   jax.ShapeDtypeStruct((N_TOK, KV_DIM), BF16), jax.ShapeDtypeStruct((N_TOK, KV_DIM), F32),
                   jax.ShapeDtypeStruct((N_TOK, KV_DIM), F32)),
        grid_spec=pltpu.PrefetchScalarGridSpec(
            num_scalar_prefetch=2, grid=(NB,),
            in_specs=[_tok_spec(D), _mod_spec(), _full_spec((1, D)), _full_spec((D, QKV_DIM)),
                      _full_spec((1, HEAD_DIM)), _full_spec((1, HEAD_DIM)), rspec, rspec, rspec],
            out_specs=(_tok_spec(Q_DIM), _tok_spec(KV_DIM), _tok_spec(KV_DIM), _tok_spec(KV_DIM), _tok_spec(KV_DIM))),
        compiler_params=_cparams(),
        name="attn_qkv",
    )(mrow, jnp.asarray(_ROPE_IDX), y, mods, g1, w_qkv, q_g, k_g, cos_t, sa_t, sb_t)


def _attn_heads(q, ks, vs, o_scr):
    nt = (((1,), (1,)), ((), ()))
    for hd in range(N_HEADS):
        g = hd // GQA_GROUP
        qh = q[:, hd * HEAD_DIM:(hd + 1) * HEAD_DIM]
        gs = slice(g * HEAD_DIM, (g + 1) * HEAD_DIM)
        ss = [lax.dot_general(qh, k[:, gs], nt, preferred_element_type=F32) for k in ks]
        m = functools.reduce(jnp.maximum, [jnp.max(s, axis=-1, keepdims=True) for s in ss])
        ps = [jnp.exp(s - m) for s in ss]
        l = functools.reduce(lambda a, b: a + b, [jnp.sum(p, axis=-1, keepdims=True) for p in ps])
        o = functools.reduce(lambda a, b: a + b,
                             [jnp.dot(p.astype(BF16), v[:, gs], preferred_element_type=F32) for p, v in zip(ps, vs)])
        o_scr[:, hd * HEAD_DIM:(hd + 1) * HEAD_DIM] = (o / l).astype(BF16)


def _attn_ctx_kernel(q_ref, k_ref, v_ref, wo_ref, y_ref, mod_ref, o_ref, o_scr):
    _attn_heads(q_ref[...], [k_ref[...]], [v_ref[...]], o_scr)
    out = jnp.dot(o_scr[...], wo_ref[...], preferred_element_type=F32)
    o_ref[...] = y_ref[...] + mod_ref[0][2:3] * out


def _attn_lat_kernel(q_ref, k_ref, v_ref, ck_ref, cv_ref, wo_ref, y_ref, mod_ref, ctx_out_ref, o_ref, o_scr):
    del ctx_out_ref
    _attn_heads(q_ref[...], [k_ref[...], ck_ref[0].astype(BF16)], [v_ref[...], cv_ref[0].astype(BF16)], o_scr)
    out = jnp.dot(o_scr[...], wo_ref[...], preferred_element_type=F32)
    o_ref[...] = y_ref[...] + mod_ref[0][2:3] * out


def _attention(q, kb, vb, cache_k, cache_v, w_o, y, mods, layer):
    y_ctx = pl.pallas_call(
        _attn_ctx_kernel,
        out_shape=jax.ShapeDtypeStruct((N_TOK, D), F32),
        grid=(BATCH,),
        in_specs=[
            pl.BlockSpec((SEQ, Q_DIM), lambda s: (s, 0)),
            pl.BlockSpec((SEQ, KV_DIM), lambda s: (s, 0)),
            pl.BlockSpec((SEQ, KV_DIM), lambda s: (s, 0)),
            pl.BlockSpec((Q_DIM, D), lambda s: (0, 0)),
            pl.BlockSpec((SEQ, D), lambda s: (s, 0)),
            pl.BlockSpec((1, 6, D), lambda s: (layer * MOD_ROWS, 0, 0)),
        ],
        out_specs=pl.BlockSpec((SEQ, D), lambda s: (s, 0)),
        scratch_shapes=[pltpu.VMEM((SEQ, Q_DIM), BF16)],
        compiler_params=_cparams(),
        name="attn_ctx",
    )(q, kb, vb, w_o, y, mods)
    pb = NP_TOK // DEC_SEQ
    return pl.pallas_call(
        _attn_lat_kernel,
        out_shape=jax.ShapeDtypeStruct((N_TOK, D), F32),
        input_output_aliases={8: 0},
        grid=(DEC_BATCH, SB_PER_DEC),
        in_specs=[
            pl.BlockSpec((SB, Q_DIM), lambda b, t: (NSBP + b * SB_PER_DEC + t, 0)),
            pl.BlockSpec((DEC_SEQ, KV_DIM), lambda b, t: (pb + b, 0)),
            pl.BlockSpec((DEC_SEQ, KV_DIM), lambda b, t: (pb + b, 0)),
            pl.BlockSpec((1, PAST_LEN, KV_DIM), lambda b, t: (b, 0, 0)),
            pl.BlockSpec((1, PAST_LEN, KV_DIM), lambda b, t: (b, 0, 0)),
            pl.BlockSpec((Q_DIM, D), lambda b, t: (0, 0)),
            pl.BlockSpec((SB, D), lambda b, t: (NSBP + b * SB_PER_DEC + t, 0)),
            pl.BlockSpec((1, 6, D), lambda b, t: (layer * MOD_ROWS + 1 + b, 0, 0)),
            pl.BlockSpec(memory_space=pl.ANY),
        ],
        out_specs=pl.BlockSpec((SB, D), lambda b, t: (NSBP + b * SB_PER_DEC + t, 0)),
        scratch_shapes=[pltpu.VMEM((SB, Q_DIM), BF16)],
        compiler_params=_cparams(2),
        name="attn_lat",
    )(q, kb, vb, cache_k, cache_v, w_o, y, mods, y_ctx)


def _log_sigmoid(x):
    return jnp.minimum(x, 0.0) - jnp.log(1.0 + jnp.exp(-jnp.abs(x)))


def _mlstm_in_kernel(mr_ref, y_ref, mod_ref, g_ref, wf_ref, wg_ref, bg_ref, q_ref, k_ref, v_ref, o_ref, gt_ref,
                     w_ref):
    @pl.when(pl.program_id(0) == 0)
    def _():
        for c in range(4):
            w_ref[:, c * D:(c + 1) * D] = wf_ref[:, c * D:(c + 1) * D].astype(BF16)

    h = _norm_mod(y_ref[...], g_ref[...], mod_ref[0], 0)
    hb = h.astype(BF16)
    q_ref[...] = jnp.dot(hb, w_ref[:, 0:D], preferred_element_type=F32).astype(BF16)
    k_ref[...] = (jnp.dot(hb, w_ref[:, D:2 * D], preferred_element_type=F32) * (M_HEAD_DIM ** -0.5)).astype(BF16)
    v_ref[...] = jnp.dot(hb, w_ref[:, 2 * D:3 * D], preferred_element_type=F32).astype(BF16)
    o_ref[...] = _sigmoid(jnp.dot(hb, w_ref[:, 3 * D:4 * D], preferred_element_type=F32))
    gt = _dot_hi_lo(h, wg_ref) + bg_ref[...]
    lane = lax.broadcasted_iota(I32, gt.shape, 1)
    is_f = ((lane >= M_HEADS) & (lane < 2 * M_HEADS)) | ((lane >= 3 * M_HEADS) & (lane < 4 * M_HEADS))
    gt_ref[...] = jnp.where(is_f, _log_sigmoid(gt), gt)


def _mlstm_in(y, mods, mrow, g1, w_in, w_gate, b_gate):
    w_spec = pl.BlockSpec(w_in.shape, lambda j, *_: (0, 0), pipeline_mode=pl.Buffered(1))
    return pl.pallas_call(
        _mlstm_in_kernel,
        out_shape=(jax.ShapeDtypeStruct((N_TOK, D), BF16), jax.ShapeDtypeStruct((N_TOK, D), BF16),
                   jax.ShapeDtypeStruct((N_TOK, D), BF16), jax.ShapeDtypeStruct((N_TOK, D), F32),
                   jax.ShapeDtypeStruct((N_TOK, LANE), F32)),
        grid_spec=pltpu.PrefetchScalarGridSpec(
            num_scalar_prefetch=1, grid=(NB,),
            in_specs=[_tok_spec(D), _mod_spec(), _full_spec((1, D)), w_spec,
                      _full_spec((D, 2 * LANE)), _full_spec((1, LANE))],
            out_specs=(_tok_spec(D), _tok_spec(D), _tok_spec(D), _tok_spec(D), _tok_spec(LANE)),
            scratch_shapes=[pltpu.VMEM((D, 4 * D), BF16)]),
        compiler_params=_cparams(),
        name="mlstm_in",
    )(mrow, y, mods, g1, w_in, w_gate, b_gate)


def _mlstm_load(hd, c, q_ref, k_ref, v_ref, gc_ref, gr_ref):
    r0 = pl.multiple_of(c * M_CHUNK, M_CHUNK)
    hs = slice(hd * M_HEAD_DIM, (hd + 1) * M_HEAD_DIM)
    rows = pl.ds(r0, M_CHUNK)
    return rows, hs, q_ref[rows, hs], k_ref[rows, hs], v_ref[rows, hs], gc_ref[hd, rows, :], gr_ref[hd, c]


def _mlstm_chunks(chains, ms, loaded, c_scr, n_scr):
    L = M_CHUNK
    n = range(len(chains))
    t_idx = lax.broadcasted_iota(I32, (L, L), 0)
    s_idx = lax.broadcasted_iota(I32, (L, L), 1)
    masks = {0: (s_idx <= t_idx, t_idx <= s_idx), 1: (s_idx >= t_idx, t_idx >= s_idx)}
    q = [ld[2] for ld in loaded]
    k = [ld[3] for ld in loaded]
    v = [ld[4] for ld in loaded]
    i_col = [ld[5][:, 2 * d:2 * d + 1] for (_, d), ld in zip(chains, loaded)]
    lf_col = [ld[5][:, 2 * d + 1:2 * d + 2] for (_, d), ld in zip(chains, loaded)]
    i_row = [ld[6][2 * d:2 * d + 1, :] for (_, d), ld in zip(chains, loaded)]
    lf_row = [ld[6][2 * d + 1:2 * d + 2, :] for (_, d), ld in zip(chains, loaded)]
    mask = [masks[d][0] for _, d in chains]
    mask_t = [masks[d][1] for _, d in chains]
    b_col = [jnp.sum(jnp.where(mask[i], lf_row[i], 0.0), axis=1, keepdims=True) for i in n]
    b_row = [jnp.sum(jnp.where(mask_t[i], lf_col[i], 0.0), axis=0, keepdims=True) for i in n]
    log_d = [jnp.where(mask[i], b_col[i] - b_row[i] + i_row[i], -jnp.inf) for i in n]
    li = [b_col[i] + ms[i] for i in n]
    m_r = [jnp.maximum(li[i], jnp.max(log_d[i], axis=1, keepdims=True)) for i in n]
    a_int = [jnp.exp(li[i] - m_r[i]) for i in n]
    dmat = [jnp.exp(log_d[i] - m_r[i]) for i in n]
    cmat = [c_scr[d, hd] for hd, d in chains]
    nvec = [n_scr[d, hd] for hd, d in chains]
    gram = [lax.dot_general(q[i], k[i], (((1,), (1,)), ((), ())), preferred_element_type=F32) for i in n]
    inter = [jnp.dot(q[i], cmat[i].astype(BF16), preferred_element_type=F32) for i in n]
    s = [gram[i] * dmat[i] for i in n]
    intra = [jnp.dot(s[i].astype(BF16), v[i], preferred_element_type=F32) for i in n]
    qn = [jnp.sum(q[i].astype(F32) * nvec[i], axis=1, keepdims=True) for i in n]
    den = [a_int[i] * qn[i] + jnp.sum(s[i], axis=1, keepdims=True) for i in n]
    hh = [(a_int[i] * inter[i] + intra[i]) / jnp.maximum(jnp.abs(den[i]), jnp.exp(-m_r[i])) for i in n]
    b_last = [b_row[i][:, L - 1:L] if chains[i][1] == 0 else b_row[i][:, 0:1] for i in n]
    log_w = [b_last[i] - b_col[i] + i_col[i] for i in n]
    m_new = [jnp.maximum(b_last[i] + ms[i], jnp.max(log_w[i], axis=0, keepdims=True)) for i in n]
    w = [jnp.exp(log_w[i] - m_new[i]) for i in n]
    decay = [jnp.exp(b_last[i] + ms[i] - m_new[i]) for i in n]
    kw = [k[i].astype(F32) * w[i] for i in n]
    kv = [lax.dot_general(kw[i].astype(BF16), v[i], (((0,), (0,)), ((), ())), preferred_element_type=F32) for i in n]
    for i, (hd, d) in enumerate(chains):
        c_scr[d, hd] = decay[i] * cmat[i] + kv[i]
        n_scr[d, hd] = decay[i] * nvec[i] + jnp.sum(kw[i], axis=0, keepdims=True)
    return hh, m_new


def _mlstm_scan_body(n_chunks, q_ref, k_ref, v_ref, gc_ref, gr_ref, h_ref, hb_scr, c_scr, n_scr, m0):
    chains = [(hd, d) for hd in range(M_HEADS) for d in range(2)]

    def body(c, ms):
        loaded = [_mlstm_load(hd, c if d == 0 else n_chunks - 1 - c, q_ref, k_ref, v_ref, gc_ref, gr_ref)
                  for hd, d in chains]
        hh, m_new = _mlstm_chunks(chains, ms, loaded, c_scr, n_scr)
        for (hd, d), ld, h in zip(chains, loaded, hh):
            dst = h_ref if d == 0 else hb_scr
            dst[ld[0], ld[1]] = h
        return tuple(m_new)

    ms = lax.fori_loop(0, n_chunks, body, tuple(m0))
    h_ref[...] += hb_scr[...]
    return ms


def _mlstm_scan_ctx_kernel(q_ref, k_ref, v_ref, gc_ref, gr_ref, h_ref, cn_ref, nn_ref, mn_ref, hb_scr, c_scr, n_scr):
    c_scr[...] = jnp.zeros(c_scr.shape, F32)
    n_scr[...] = jnp.zeros(n_scr.shape, F32)
    zero = jnp.zeros((1, 1), F32)
    ms = _mlstm_scan_body(SEQ // M_CHUNK, q_ref, k_ref, v_ref, gc_ref, gr_ref, h_ref, hb_scr, c_scr, n_scr,
                          [zero] * (2 * M_HEADS))
    cn_ref[0] = c_scr[...]
    nn_ref[0] = n_scr[...]
    for hd in range(M_HEADS):
        for d in range(2):
            mn_ref[0, d, hd] = jnp.broadcast_to(ms[2 * hd + d], (1, LANE))


def _mlstm_scan_lat_kernel(q_ref, k_ref, v_ref, gc_ref, gr_ref, c0_ref, n0_ref, m0_ref, ctx_out_ref, h_ref,
                           hb_scr, c_scr, n_scr):
    del ctx_out_ref
    c_scr[...] = c0_ref[0]
    n_scr[...] = n0_ref[0]
    m0 = [m0_ref[0, d, hd] for hd in range(M_HEADS) for d in range(2)]
    _mlstm_scan_body(DEC_SEQ // M_CHUNK, q_ref, k_ref, v_ref, gc_ref, gr_ref, h_ref, hb_scr, c_scr, n_scr, m0)


def _mlstm_scan(q, k, v, gates, state_c, state_n, state_m):
    g16 = gates[:, :4 * M_HEADS].reshape(N_TOK, 4, M_HEADS)
    gcol = jnp.transpose(g16, (2, 0, 1))
    grow = jnp.transpose(g16.reshape(N_TOK // M_CHUNK, M_CHUNK, 4, M_HEADS), (3, 0, 2, 1))
    hd = M_HEAD_DIM
    state_scratch = [pltpu.VMEM((2, M_HEADS, hd, hd), F32), pltpu.VMEM((2, M_HEADS, 1, hd), F32)]
    ncp = SEQ // M_CHUNK
    h_ctx, new_c, new_n, new_m = pl.pallas_call(
        _mlstm_scan_ctx_kernel,
        out_shape=(jax.ShapeDtypeStruct((N_TOK, D), F32),
                   jax.ShapeDtypeStruct((BATCH, 2, M_HEADS, hd, hd), F32),
                   jax.ShapeDtypeStruct((BATCH, 2, M_HEADS, 1, hd), F32),
                   jax.ShapeDtypeStruct((BATCH, 2, M_HEADS, 1, LANE), F32)),
        grid=(BATCH,),
        in_specs=[
            pl.BlockSpec((SEQ, D), lambda s: (s, 0)),
            pl.BlockSpec((SEQ, D), lambda s: (s, 0)),
            pl.BlockSpec((SEQ, D), lambda s: (s, 0)),
            pl.BlockSpec((M_HEADS, SEQ, 4), lambda s: (0, s, 0)),
            pl.BlockSpec((M_HEADS, ncp, 4, M_CHUNK), lambda s: (0, s, 0, 0)),
        ],
        out_specs=(
            pl.BlockSpec((SEQ, D), lambda s: (s, 0)),
            pl.BlockSpec((1, 2, M_HEADS, hd, hd), lambda s: (s, 0, 0, 0, 0)),
            pl.BlockSpec((1, 2, M_HEADS, 1, hd), lambda s: (s, 0, 0, 0, 0)),
            pl.BlockSpec((1, 2, M_HEADS, 1, LANE), lambda s: (s, 0, 0, 0, 0)),
        ),
        scratch_shapes=[pltpu.VMEM((SEQ, D), F32)] + state_scratch,
        compiler_params=_cparams(),
        name="mlstm_scan_ctx",
    )(q, k, v, gcol, grow)
    ncl = DEC_SEQ // M_CHUNK
    pb = NP_TOK // DEC_SEQ
    h_all = pl.pallas_call(
        _mlstm_scan_lat_kernel,
        out_shape=jax.ShapeDtypeStruct((N_TOK, D), F32),
        input_output_aliases={8: 0},
        grid=(DEC_BATCH,),
        in_specs=[
            pl.BlockSpec((DEC_SEQ, D), lambda b: (pb + b, 0)),
            pl.BlockSpec((DEC_SEQ, D), lambda b: (pb + b, 0)),
            pl.BlockSpec((DEC_SEQ, D), lambda b: (pb + b, 0)),
            pl.BlockSpec((M_HEADS, DEC_SEQ, 4), lambda b: (0, pb + b, 0)),
            pl.BlockSpec((M_HEADS, ncl, 4, M_CHUNK), lambda b: (0, pb + b, 0, 0)),
            pl.BlockSpec((1, 2, M_HEADS, hd, hd), lambda b: (b, 0, 0, 0, 0)),
            pl.BlockSpec((1, 2, M_HEADS, 1, hd), lambda b: (b, 0, 0, 0, 0)),
            pl.BlockSpec((1, 2, M_HEADS, 1, 1), lambda b: (b, 0, 0, 0, 0)),
            pl.BlockSpec(memory_space=pl.ANY),
        ],
        out_specs=pl.BlockSpec((DEC_SEQ, D), lambda b: (pb + b, 0)),
        scratch_shapes=[pltpu.VMEM((DEC_SEQ, D), F32)] + state_scratch,
        compiler_params=_cparams(),
        name="mlstm_scan_lat",
    )(q, k, v, gcol, grow, state_c, state_n, state_m, h_ctx)
    return h_all, new_c, new_n, new_m


def _mlstm_out_kernel(mr_ref, h_ref, o_ref, ng_ref, w_ref, y_ref, mod_ref, out_ref, x_scr):
    hc = o_ref[...] * h_ref[...]
    for hd in range(M_HEADS):
        sl = slice(hd * M_HEAD_DIM, (hd + 1) * M_HEAD_DIM)
        x_scr[:, sl] = _rms(hc[:, sl], ng_ref[:, sl]).astype(BF16)
    out = jnp.dot(x_scr[...], w_ref[...], preferred_element_type=F32)
    out_ref[...] = y_ref[...] + mod_ref[0][2:3] * out


def _mlstm_out(hsum, o, norm_g, w_out, y, mods, mrow):
    return pl.pallas_call(
        _mlstm_out_kernel,
        out_shape=jax.ShapeDtypeStruct((N_TOK, D), F32),
        grid_spec=pltpu.PrefetchScalarGridSpec(
            num_scalar_prefetch=1, grid=(NB,),
            in_specs=[_tok_spec(D), _tok_spec(D), _full_spec((1, D)), _full_spec((D, D)), _tok_spec(D), _mod_spec()],
            out_specs=_tok_spec(D),
            scratch_shapes=[pltpu.VMEM((TM, D), BF16)]),
        compiler_params=_cparams(),
        name="mlstm_out",
    )(mrow, hsum, o, norm_g, w_out, y, mods)


ROUTE_OFF = N_GROUPS
SLAB = D // (2 * LANE)
V7X_SC_CORES = 2
V7X_SC_SUBCORES = 16
SC_WORKERS = V7X_SC_CORES * V7X_SC_SUBCORES
SC_WINDOW = 128
HI_MASK = -65536


def _bf16_bits(x):
    return lax.bitcast_convert_type(x.astype(BF16).astype(F32), I32)


def _store_slabs(ref, x):
    rows = x.shape[0]
    for c in range(SLAB):
        lo = lax.shift_right_logical(_bf16_bits(x[:, (2 * c) * LANE:(2 * c + 1) * LANE]), 16)
        hi = _bf16_bits(x[:, (2 * c + 1) * LANE:(2 * c + 2) * LANE]) & HI_MASK
        ref[pl.ds(c, rows, stride=SLAB), :] = lo | hi


def _load_slabs(ref, dst, rows, dtype):
    for c in range(SLAB):
        w = ref[pl.ds(c, rows, stride=SLAB), :]
        lo = lax.bitcast_convert_type(lax.shift_left(w, 16), F32)
        hi = lax.bitcast_convert_type(w & HI_MASK, F32)
        dst[:, (2 * c) * LANE:(2 * c + 1) * LANE] = lo.astype(dtype)
        dst[:, (2 * c + 1) * LANE:(2 * c + 2) * LANE] = hi.astype(dtype)


def _route_kernel(mr_ref, y_ref, mod_ref, g_ref, wr_ref, br_ref, x_ref, id_ref, wt_ref):
    x = _norm_mod(y_ref[...], g_ref[...], mod_ref[0], 1)
    _store_slabs(x_ref, x)
    lg = _dot_hi_lo(x, wr_ref) + br_ref[...]
    lane = lax.broadcasted_iota(I32, lg.shape, 1).astype(F32)
    ninf = -jnp.inf
    big = float(LANE)
    lgg = jnp.where(lane < N_GROUPS, lg, ninf)
    gmax = jnp.max(lgg, axis=-1, keepdims=True)
    g_idx = jnp.min(jnp.where(lgg == gmax, lane, big), axis=-1, keepdims=True)
    g_w = 1.0 / jnp.sum(jnp.exp(lgg - gmax), axis=-1, keepdims=True)
    lo = ROUTE_OFF + g_idx * EXPERTS_PER_GROUP
    le = jnp.where((lane >= lo) & (lane < lo + EXPERTS_PER_GROUP), lg, ninf)
    m1 = jnp.max(le, axis=-1, keepdims=True)
    i1 = jnp.min(jnp.where(le == m1, lane, big), axis=-1, keepdims=True)
    le2 = jnp.where(lane == i1, ninf, le)
    m2 = jnp.max(le2, axis=-1, keepdims=True)
    i2 = jnp.min(jnp.where(le2 == m2, lane, big), axis=-1, keepdims=True)
    r = jnp.exp(m2 - m1)
    p1 = 1.0 / (1.0 + r)
    p2 = r / (1.0 + r)
    two = lax.broadcasted_iota(I32, (x.shape[0], TOP_K), 1)
    id_ref[...] = (jnp.where(two == 0, i1, i2) - ROUTE_OFF).astype(I32)
    wt_ref[...] = jnp.where(two == 0, g_w * p1, g_w * p2)


def _route(y, mods, mrow, g2, w_route, b_route):
    return pl.pallas_call(
        _route_kernel,
        out_shape=(jax.ShapeDtypeStruct((N_TOK * SLAB, LANE), I32), jax.ShapeDtypeStruct((N_TOK, TOP_K), I32),
                   jax.ShapeDtypeStruct((N_TOK, TOP_K), F32)),
        grid_spec=pltpu.PrefetchScalarGridSpec(
            num_scalar_prefetch=1, grid=(NB,),
            in_specs=[_tok_spec(D), _mod_spec(), _full_spec((1, D)), _full_spec((D, 2 * LANE)), _full_spec((1, LANE))],
            out_specs=(pl.BlockSpec((TM * SLAB, LANE), lambda j, *_: (j, 0)), _tok_spec(TOP_K), _tok_spec(TOP_K))),
        compiler_params=_cparams(),
        name="moe_route",
    )(mrow, y, mods, g2, w_route, b_route)


def _dispatch_tables(expert_id):
    flat_e = expert_id.reshape(-1)
    onehot = (flat_e[:, None] == jnp.arange(N_EXPERTS, dtype=I32)[None, :]).astype(I32)
    csum = jnp.cumsum(onehot, axis=0)
    counts = csum[-1]
    padded = ((counts + EBLK - 1) // EBLK) * EBLK
    pad_end = jnp.cumsum(padded)
    pad_start = pad_end - padded
    dest = jnp.sum((csum - 1 + pad_start[None, :]) * onehot, axis=1).astype(I32)
    n_blk = (padded // EBLK).astype(I32)
    blk_start = (pad_start // EBLK).astype(I32)
    n_used = (pad_end[-1] // EBLK).astype(I32).reshape(1)
    dest2 = dest.reshape(N_TOK, TOP_K)
    return dest2[:, 0].reshape(1, N_TOK), dest2[:, 1].reshape(1, N_TOK), blk_start, n_blk, n_used


def _sc_mesh():
    return plsc.VectorSubcoreMesh(core_axis_name="core", subcore_axis_name="subcore",
                                  num_cores=V7X_SC_CORES, num_subcores=V7X_SC_SUBCORES)


def _sc_worker():
    return lax.axis_index("core") * V7X_SC_SUBCORES + lax.axis_index("subcore")


def _sc_dispatch(x_slabs, d0, d1):
    per = N_TOK // SC_WORKERS

    @functools.partial(
        pl.kernel, out_type=jax.ShapeDtypeStruct((P_SLOTS, SLAB, LANE), I32), mesh=_sc_mesh(), name="moe_dispatch",
        scratch_types=[pltpu.VMEM((1, per), I32), pltpu.VMEM((1, per), I32), pltpu.VMEM((SC_WINDOW, SLAB, LANE), I32)])
    def run(x_hbm, d0_hbm, d1_hbm, o_hbm, i0_v, i1_v, buf):
        base = _sc_worker() * per
        pltpu.sync_copy(d0_hbm.at[:, pl.ds(base, per)], i0_v)
        pltpu.sync_copy(d1_hbm.at[:, pl.ds(base, per)], i1_v)

        @pl.loop(0, per // SC_WINDOW)
        def _(s):
            off = s * SC_WINDOW
            pltpu.sync_copy(x_hbm.at[pl.ds(base + off, SC_WINDOW)], buf)
            pltpu.sync_copy(buf, o_hbm.at[i0_v.at[0, pl.ds(off, SC_WINDOW)]])
            pltpu.sync_copy(buf, o_hbm.at[i1_v.at[0, pl.ds(off, SC_WINDOW)]])

    return run(x_slabs.reshape(N_TOK, SLAB, LANE), d0, d1)


def _sc_collect(y_slabs, dcat):
    per = N_ASSIGN // SC_WORKERS

    @functools.partial(
        pl.kernel, out_type=jax.ShapeDtypeStruct((N_ASSIGN, SLAB, LANE), I32), mesh=_sc_mesh(), name="moe_collect",
        scratch_types=[pltpu.VMEM((1, per), I32), pltpu.VMEM((SC_WINDOW, SLAB, LANE), I32)])
    def run(y_hbm, i_hbm, o_hbm, i_v, buf):
        base = _sc_worker() * per
        pltpu.sync_copy(i_hbm.at[:, pl.ds(base, per)], i_v)

        @pl.loop(0, per // SC_WINDOW)
        def _(s):
            off = s * SC_WINDOW
            pltpu.sync_copy(y_hbm.at[i_v.at[0, pl.ds(off, SC_WINDOW)]], buf)
            pltpu.sync_copy(buf, o_hbm.at[pl.ds(base + off, SC_WINDOW)])

    return run(y_slabs.reshape(P_SLOTS, SLAB, LANE), dcat)


EROWS = EBLK * SLAB


W_CHUNKS = 8
W_SLOTS = 3


def _expert_kernel(layer, bs_ref, nb_ref, nu_ref, wg_hbm, wu_hbm, wd_hbm, x_hbm, y_hbm,
                   xbuf, ybuf, xs, wg_f, wu_f, wd_f, wg_bf, wu_bf, wd_bf, isem, osem, wsem):
    e = pl.program_id(0)
    n_exp = pl.num_programs(0)
    n_used = nu_ref[0]
    b0 = bs_ref[e]
    nb = nb_ref[e]
    wslot = lax.rem(e, W_SLOTS)

    def weight_copies(ex, slot):
        out = []
        for hbm, buf in ((wg_hbm, wg_f), (wu_hbm, wu_f), (wd_hbm, wd_f)):
            rows = buf.shape[1] // W_CHUNKS
            for c in range(W_CHUNKS):
                rs = pl.ds(c * rows, rows)
                out.append(pltpu.make_async_copy(hbm.at[layer, ex, rs], buf.at[slot, rs], wsem.at[slot]))
        return out

    def start_weights(ex, slot):
        for i, cp in enumerate(weight_copies(ex, slot)):
            cp.start(priority=i % 2)

    ahead = W_SLOTS - 1

    @pl.when(e == 0)
    def _():
        for ex in range(ahead):
            start_weights(ex, ex)

    for cp in weight_copies(e, wslot):
        cp.wait()

    @pl.when(e + ahead < n_exp)
    def _():
        start_weights(e + ahead, lax.rem(e + ahead, W_SLOTS))

    def in_copy(g, slot):
        return pltpu.make_async_copy(x_hbm.at[pl.ds(pl.multiple_of(g * EROWS, EROWS), EROWS)], xbuf.at[slot],
                                     isem.at[slot])

    def out_copy(g, slot):
        return pltpu.make_async_copy(ybuf.at[slot], y_hbm.at[pl.ds(pl.multiple_of(g * EROWS, EROWS), EROWS)],
                                     osem.at[slot])

    @pl.when(e == 0)
    def _():
        in_copy(0, 0).start(priority=1)

    @pl.when(nb > 0)
    def _():
        wg_bf[...] = wg_f[wslot].astype(BF16)
        wu_bf[...] = wu_f[wslot].astype(BF16)
        wd_bf[...] = wd_f[wslot].astype(BF16)

    def block(k, carry):
        g = b0 + k
        slot = lax.rem(g, 2)
        in_copy(g, slot).wait()

        @pl.when(g + 1 < n_used)
        def _():
            in_copy(g + 1, 1 - slot).start(priority=1)

        _load_slabs(xbuf.at[slot], xs, EBLK, BF16)
        xb = xs[...]
        gt = jnp.dot(xb, wg_bf[...], preferred_element_type=F32)
        up = jnp.dot(xb, wu_bf[...], preferred_element_type=F32)
        hmid = (gt * _sigmoid(gt) * up).astype(BF16)
        res = jnp.dot(hmid, wd_bf[...], preferred_element_type=F32)

        @pl.when(g >= 2)
        def _():
            out_copy(g - 2, slot).wait()

        _store_slabs(ybuf.at[slot], res)
        out_copy(g, slot).start()
        return carry

    lax.fori_loop(0, nb, block, 0)

    @pl.when(e == n_exp - 1)
    def _():
        last = n_used - 1
        out_copy(last, lax.rem(last, 2)).wait()

        @pl.when(n_used >= 2)
        def _():
            out_copy(last - 1, lax.rem(last - 1, 2)).wait()


def _experts(x_sorted, blk_start, n_blk, n_used, w_gate, w_up, w_down, layer):
    any_spec = pl.BlockSpec(memory_space=pl.ANY)
    return pl.pallas_call(
        functools.partial(_expert_kernel, layer),
        out_shape=jax.ShapeDtypeStruct((P_SLOTS * SLAB, LANE), I32),
        grid_spec=pltpu.PrefetchScalarGridSpec(
            num_scalar_prefetch=3, grid=(N_EXPERTS,),
            in_specs=[any_spec, any_spec, any_spec, any_spec],
            out_specs=any_spec,
            scratch_shapes=[
                pltpu.VMEM((2, EROWS, LANE), I32), pltpu.VMEM((2, EROWS, LANE), I32),
                pltpu.VMEM((EBLK, D), BF16),
                pltpu.VMEM((W_SLOTS, D, D_EXPERT), F32), pltpu.VMEM((W_SLOTS, D, D_EXPERT), F32),
                pltpu.VMEM((W_SLOTS, D_EXPERT, D), F32),
                pltpu.VMEM((D, D_EXPERT), BF16), pltpu.VMEM((D, D_EXPERT), BF16), pltpu.VMEM((D_EXPERT, D), BF16),
                pltpu.SemaphoreType.DMA((2,)), pltpu.SemaphoreType.DMA((2,)), pltpu.SemaphoreType.DMA((W_SLOTS,)),
            ]),
        compiler_params=_cparams(),
        name="moe_experts",
    )(blk_start, n_blk, n_used, w_gate, w_up, w_down, x_sorted.reshape(P_SLOTS * SLAB, LANE))


def _combine_kernel(final, mr_ref, e0_ref, e1_ref, wt_ref, y_ref, mod_ref, fg_ref, o_ref, a_scr, b_scr):
    _load_slabs(e0_ref, a_scr, TM, F32)
    _load_slabs(e1_ref, b_scr, TM, F32)
    wt = wt_ref[...]
    moe = wt[:, 0:1] * a_scr[...] + wt[:, 1:2] * b_scr[...]
    y_new = y_ref[...] + mod_ref[0][5:6] * moe
    o_ref[...] = _rms(y_new, fg_ref[...]) if final else y_new


def _combine(ym, wts, y, mods, mrow, final_g, blk0, nblk, final):
    tok = lambda width: pl.BlockSpec((TM, width), lambda j, *_: (blk0 + j, 0))
    slab0 = pl.BlockSpec((TM * SLAB, LANE), lambda j, *_: (blk0 + j, 0))
    slab1 = pl.BlockSpec((TM * SLAB, LANE), lambda j, *_: (NB + blk0 + j, 0))
    mod = pl.BlockSpec((1, 6, D), lambda j, mr: (mr[blk0 + j], 0, 0))
    return pl.pallas_call(
        functools.partial(_combine_kernel, final),
        out_shape=jax.ShapeDtypeStruct((nblk * TM, D), F32),
        grid_spec=pltpu.PrefetchScalarGridSpec(
            num_scalar_prefetch=1, grid=(nblk,),
            in_specs=[slab0, slab1, tok(TOP_K), tok(D), mod, _full_spec((1, D))],
            out_specs=pl.BlockSpec((TM, D), lambda j, *_: (j, 0)),
            scratch_shapes=[pltpu.VMEM((TM, D), F32), pltpu.VMEM((TM, D), F32)]),
        compiler_params=_cparams(),
        name="moe_combine",
    )(mrow, ym, ym, wts, y, mods, final_g)


def kernel(x_prompt, x_sample, cache_attn_k, cache_attn_v, state_mlstm_C, state_mlstm_n, state_mlstm_m, c, c_ctx, ada_w, ada_b, norm1_g, norm2_g, conv_w_in, conv_w_dw, conv_b_dw, conv_ln_g, conv_ln_b, conv_w_out, attn_w_qkv, attn_q_norm, attn_k_norm, attn_w_o, mlstm_w_in, mlstm_b_gate, mlstm_norm_g, mlstm_w_out, moe_w_group, moe_b_group, moe_w_router, moe_b_router, moe_w_gate, moe_w_up, moe_w_down, final_norm_g):
    y = jnp.concatenate([x_prompt.reshape(NP_TOK, D), x_sample.reshape(NS_TOK, D)], axis=0)
    cvec = jnp.concatenate([c_ctx[None, :], c, jnp.zeros((MOD_ROWS - 1 - DEC_BATCH, D), F32)], axis=0)
    mods = _ada_all(cvec, ada_w, ada_b)
    rope = _rope_blocks()
    new_k = new_v = new_c = new_n = new_m = None
    for i in range(DEPTH):
        kind, slot = i % 3, i // 3
        mrow = jnp.asarray(_MOD_ROW + i * MOD_ROWS)
        g1 = norm1_g[i].reshape(1, D)
        if kind == 0:
            u = _conv_in(y, mods, mrow, g1, conv_w_in[slot].astype(BF16))
            w_dw = jnp.concatenate([conv_w_dw[slot], jnp.zeros((1, D), F32)], axis=0)
            y = _conv_main(u, y, mods, jnp.asarray(_MOD_ROW_SB + i * MOD_ROWS), w_dw, conv_b_dw[slot].reshape(1, D), conv_ln_g[slot].reshape(1, D),
                           conv_ln_b[slot].reshape(1, D), conv_w_out[slot].astype(BF16))
        elif kind == 1:
            q, kb, vb, kf, vf = _attn_qkv(y, mods, mrow, g1, attn_w_qkv[slot].astype(BF16),
                                          attn_q_norm[slot].reshape(1, HEAD_DIM), attn_k_norm[slot].reshape(1, HEAD_DIM),
                                          rope)
            new_k = kf[:NP_TOK].reshape(BATCH, 1, SEQ, N_KV_HEADS, HEAD_DIM)
            new_v = vf[:NP_TOK].reshape(BATCH, 1, SEQ, N_KV_HEADS, HEAD_DIM)
            ck = cache_attn_k[:, slot].reshape(DEC_BATCH, PAST_LEN, KV_DIM)
            cv = cache_attn_v[:, slot].reshape(DEC_BATCH, PAST_LEN, KV_DIM)
            y = _attention(q, kb, vb, ck, cv, attn_w_o[slot].astype(BF16), y, mods, i)
        else:
            w_in = mlstm_w_in[slot]
            w_gate = jnp.concatenate([w_in[:, 4 * D:], jnp.zeros((D, LANE - 4 * M_HEADS), F32)], axis=1)
            b_gate = jnp.concatenate([mlstm_b_gate[slot], jnp.zeros((LANE - 4 * M_HEADS,), F32)]).reshape(1, LANE)
            q, k, v, o, gates = _mlstm_in(y, mods, mrow, g1, w_in, _split_hi_lo(w_gate), b_gate)
            sc = state_mlstm_C[:, slot]
            sn = state_mlstm_n[:, slot].reshape(DEC_BATCH, 2, M_HEADS, 1, M_HEAD_DIM)
            sm = state_mlstm_m[:, slot].reshape(DEC_BATCH, 2, M_HEADS, 1, 1)
            hsum, nc_, nn_, nm_ = _mlstm_scan(q, k, v, gates, sc, sn, sm)
            new_c = nc_[:, None]
            new_n = nn_.reshape(BATCH, 1, 2, M_HEADS, M_HEAD_DIM)
            new_m = nm_[..., 0, 0].reshape(BATCH, 1, 2, M_HEADS)
            y = _mlstm_out(hsum, o, mlstm_norm_g[slot].reshape(1, D), mlstm_w_out[slot].astype(BF16), y, mods, mrow)
        w_route = jnp.concatenate([moe_w_group[i], moe_w_router[i],
                                   jnp.zeros((D, LANE - N_GROUPS - N_EXPERTS), F32)], axis=1)
        b_route = jnp.concatenate([moe_b_group[i], moe_b_router[i],
                                   jnp.zeros((LANE - N_GROUPS - N_EXPERTS,), F32)]).reshape(1, LANE)
        x2, eid, ewt = _route(y, mods, mrow, norm2_g[i].reshape(1, D), _split_hi_lo(w_route), b_route)
        d0, d1, blk_start, n_blk, n_used = _dispatch_tables(eid)
        x_sorted = _sc_dispatch(x2, d0, d1)
        y_sorted = _experts(x_sorted, blk_start, n_blk, n_used, moe_w_gate, moe_w_up, moe_w_down, i)
        ym = _sc_collect(y_sorted, jnp.concatenate([d0, d1], axis=1))
        ym = ym.reshape(N_ASSIGN * SLAB, LANE)
        fg = final_norm_g.reshape(1, D)
        if i + 1 < DEPTH:
            y = _combine(ym, ewt, y, mods, mrow, fg, 0, NB, False)
        else:
            y_prompt = _combine(ym, ewt, y, mods, mrow, fg, 0, NBP, True).reshape(BATCH, SEQ, D)
            y_sample = _combine(ym, ewt, y, mods, mrow, fg, NBP, NB - NBP, True).reshape(DEC_BATCH, DEC_SEQ, D)
    return (y_prompt, y_sample, new_k, new_v, new_c, new_n, new_m)
```

```python
import functools

import jax
import jax.numpy as jnp
import numpy as np
from jax import lax
from jax.experimental import pallas as pl
from jax.experimental.pallas import tpu as pltpu
from jax.experimental.pallas import tpu_sc as plsc

F32 = jnp.float32
BF16 = jnp.bfloat16
I32 = jnp.int32

D = 1024
BATCH, SEQ = 16, 256
DEC_BATCH, DEC_SEQ = 8, 1024
PAST_LEN = 256
DEPTH = 4
GRID_W = 64
EPS = 1e-6
CONV_WIDTH = 31
CONV_PAD = CONV_WIDTH // 2
HEAD_DIM = 128
N_HEADS = 8
N_KV_HEADS = 2
GQA_GROUP = N_HEADS // N_KV_HEADS
Q_DIM = N_HEADS * HEAD_DIM
KV_DIM = N_KV_HEADS * HEAD_DIM
QKV_DIM = Q_DIM + 2 * KV_DIM
ROPE_THETA = 10000.0
M_HEADS = 4
M_HEAD_DIM = D // M_HEADS
M_CHUNK = 64
N_GROUPS = 4
EXPERTS_PER_GROUP = 8
N_EXPERTS = N_GROUPS * EXPERTS_PER_GROUP
TOP_K = 2
D_EXPERT = 512

NP_TOK = BATCH * SEQ
NS_TOK = DEC_BATCH * DEC_SEQ
N_TOK = NP_TOK + NS_TOK
TM = 512
NB = N_TOK // TM
NBP = NP_TOK // TM
BLK_PER_DEC = DEC_SEQ // TM
SB = 256
NSB = N_TOK // SB
NSBP = NP_TOK // SB
SB_PER_DEC = DEC_SEQ // SB
MOD_ROWS = 16
HALO = 16
LANE = 128
SUBLANE = 8

N_ASSIGN = N_TOK * TOP_K
EBLK = 256
N_EBLK = N_ASSIGN // EBLK + N_EXPERTS
P_SLOTS = N_EBLK * EBLK
N_PAD_SLOTS = P_SLOTS - N_ASSIGN

VMEM_LIMIT = 56 * 1024 * 1024


def _block_tables(nb, nbp, per_dec):
    j = np.arange(nb)
    is_p = j < nbp
    mod_row = np.where(is_p, 0, 1 + (j - nbp) // per_dec)
    rope_idx = np.where(is_p, 0, 1 + (j - nbp) % per_dec)
    first = np.where(is_p, 1, ((j - nbp) % per_dec == 0).astype(np.int64))
    last = np.where(is_p, 1, ((j - nbp) % per_dec == per_dec - 1).astype(np.int64))
    return (mod_row.astype(np.int32), rope_idx.astype(np.int32), first.astype(np.int32), last.astype(np.int32))


_MOD_ROW, _, _, _ = _block_tables(NB, NBP, BLK_PER_DEC)
_MOD_ROW_SB, _ROPE_IDX_SB, _SEQ_FIRST, _SEQ_LAST = _block_tables(NSB, NSBP, SB_PER_DEC)


def _cparams(n_axes=1):
    return pltpu.CompilerParams(dimension_semantics=("arbitrary",) * n_axes, vmem_limit_bytes=VMEM_LIMIT)


def _sigmoid(x):
    return 1.0 / (1.0 + jnp.exp(-x))


def _rms(x, g):
    return x * lax.rsqrt(jnp.mean(x * x, axis=-1, keepdims=True) + EPS) * g


def _split_hi_lo(w):
    hi = w.astype(BF16)
    return jnp.concatenate([hi, (w - hi.astype(F32)).astype(BF16)], axis=1)


def _dot_hi_lo(x, w_ref):
    n = w_ref.shape[1] // 2
    xh = x.astype(BF16)
    xl = (x - xh.astype(F32)).astype(BF16)
    both = jnp.dot(xh, w_ref[...], preferred_element_type=F32)
    return both[:, :n] + (both[:, n:] + jnp.dot(xl, w_ref[:, :n], preferred_element_type=F32))


def _norm_mod(y, g, mod, which):
    shift = mod[3 * which:3 * which + 1]
    scale = mod[3 * which + 1:3 * which + 2]
    return _rms(y, g) * (1.0 + scale) + shift


def _ada_kernel(c_ref, w_ref, b_ref, o_ref):
    c = c_ref[...]
    s = c * _sigmoid(c)
    o_ref[0] = jnp.dot(s.astype(BF16), w_ref[0].astype(BF16), preferred_element_type=F32) + b_ref[0]


def _ada_all(cvec, ada_w, ada_b):
    tn = 1536
    out = pl.pallas_call(
        _ada_kernel,
        out_shape=jax.ShapeDtypeStruct((DEPTH, MOD_ROWS, 6 * D), F32),
        grid=(DEPTH, 6 * D // tn),
        in_specs=[
            pl.BlockSpec((MOD_ROWS, D), lambda l, n: (0, 0)),
            pl.BlockSpec((1, D, tn), lambda l, n: (l, 0, n)),
            pl.BlockSpec((1, 1, tn), lambda l, n: (l, 0, n)),
        ],
        out_specs=pl.BlockSpec((1, MOD_ROWS, tn), lambda l, n: (l, 0, n)),
        compiler_params=_cparams(2),
        name="ada_mod",
    )(cvec, ada_w, ada_b.reshape(DEPTH, 1, 6 * D))
    return out.reshape(DEPTH * MOD_ROWS, 6, D)


def _tok_spec(width, rows=TM):
    return pl.BlockSpec((rows, width), lambda j, *_: (j, 0))


def _mod_spec():
    return pl.BlockSpec((1, 6, D), lambda j, mr, *_: (mr[j], 0, 0))


def _full_spec(shape):
    nd = len(shape)
    return pl.BlockSpec(shape, lambda j, *_: (0,) * nd)


def _conv_in_kernel(mr_ref, y_ref, mod_ref, g_ref, w_ref, u_ref):
    h = _norm_mod(y_ref[...], g_ref[...], mod_ref[0], 0)
    ag = jnp.dot(h.astype(BF16), w_ref[...], preferred_element_type=F32)
    u_ref[...] = ag[:, :D] * _sigmoid(ag[:, D:])


def _conv_in(y, mods, mrow, g1, w_in):
    return pl.pallas_call(
        _conv_in_kernel,
        out_shape=jax.ShapeDtypeStruct((N_TOK, D), F32),
        grid_spec=pltpu.PrefetchScalarGridSpec(
            num_scalar_prefetch=1, grid=(NB,),
            in_specs=[_tok_spec(D), _mod_spec(), _full_spec((1, D)), _full_spec((D, 2 * D))],
            out_specs=_tok_spec(D)),
        compiler_params=_cparams(),
        name="conv_in",
    )(mrow, y, mods, g1, w_in)


def _conv_main_kernel(mr_ref, first_ref, last_ref, u_ref, up_ref, un_ref, wdw_ref, bdw_ref, lg_ref, lb_ref,
                      wout_ref, y_ref, mod_ref, o_ref, ext_ref, acc_ref):
    j = pl.program_id(0)
    zero = jnp.zeros((HALO, D), F32)
    ext_ref[0:HALO, :] = jnp.where(first_ref[j] == 1, zero, up_ref[...])
    ext_ref[HALO:HALO + SB, :] = u_ref[...]
    ext_ref[HALO + SB:2 * HALO + SB, :] = jnp.where(last_ref[j] == 1, zero, un_ref[...])

    off0 = HALO - CONV_PAD
    n_a = (off0 + CONV_WIDTH - 1) // SUBLANE + 1
    n_chunks = SB // SUBLANE

    def strip(ci, carry):
        cs = pl.ds(pl.multiple_of(ci * LANE, LANE), LANE)
        wk = [jnp.broadcast_to(wdw_ref[k:k + 1, cs], (SUBLANE, LANE)) for k in range(CONV_WIDTH)]
        bias = jnp.broadcast_to(bdw_ref[:, cs], (SUBLANE, LANE))
        sub = lax.broadcasted_iota(I32, (SUBLANE, LANE), 0)
        prev_rot, prev_v0 = None, None
        for j in range(n_chunks + 1):
            tiles = [ext_ref[SUBLANE * (j + a):SUBLANE * (j + a + 1), cs] for a in range(n_a)]
            part = []
            for s in range(SUBLANE):
                acc = None
                for a in range(n_a):
                    k = SUBLANE * a + s - off0
                    if (0 <= k < CONV_WIDTH) and not (s == 0 and j == n_chunks):
                        term = tiles[a] * wk[k]
                        acc = term if acc is None else acc + term
                part.append(acc)
            rot = [None] + [pltpu.roll(part[s], SUBLANE - s, 0) for s in range(1, SUBLANE)]
            if j >= 1:
                out = prev_v0 + bias
                for s in range(1, SUBLANE):
                    out = out + jnp.where(sub < SUBLANE - s, prev_rot[s], rot[s])
                acc_ref[SUBLANE * (j - 1):SUBLANE * j, cs] = out
            prev_rot, prev_v0 = rot, part[0]
        return carry

    lax.fori_loop(0, D // LANE, strip, 0)

    c = acc_ref[...]
    mu = jnp.mean(c, axis=-1, keepdims=True)
    cc = c - mu
    var = jnp.mean(cc * cc, axis=-1, keepdims=True)
    z = cc * lax.rsqrt(var + EPS) * lg_ref[...] + lb_ref[...]
    z = z * _sigmoid(z)
    out = jnp.dot(z.astype(BF16), wout_ref[...], preferred_element_type=F32)
    o_ref[...] = y_ref[...] + mod_ref[0][2:3] * out


def _conv_main(u, y, mods, mrow, w_dw, b_dw, ln_g, ln_b, w_out):
    nh = N_TOK // HALO
    per = SB // HALO
    sb_spec = pl.BlockSpec((SB, D), lambda j, *_: (j, 0))
    return pl.pallas_call(
        _conv_main_kernel,
        out_shape=jax.ShapeDtypeStruct((N_TOK, D), F32),
        grid_spec=pltpu.PrefetchScalarGridSpec(
            num_scalar_prefetch=3, grid=(NSB,),
            in_specs=[
                sb_spec,
                pl.BlockSpec((HALO, D), lambda j, *_: (jnp.maximum(j * per - 1, 0), 0)),
                pl.BlockSpec((HALO, D), lambda j, *_: (jnp.minimum((j + 1) * per, nh - 1), 0)),
                _full_spec((CONV_WIDTH + 1, D)), _full_spec((1, D)), _full_spec((1, D)), _full_spec((1, D)),
                _full_spec((D, D)), sb_spec, _mod_spec(),
            ],
            out_specs=sb_spec,
            scratch_shapes=[pltpu.VMEM((SB + 2 * HALO, D), F32), pltpu.VMEM((SB, D), F32)]),
        compiler_params=_cparams(),
        name="conv_main",
    )(mrow, jnp.asarray(_SEQ_FIRST), jnp.asarray(_SEQ_LAST), u, u, u, w_dw, b_dw, ln_g, ln_b, w_out, y, mods)


def _rope_angles():
    rows = DEC_SEQ // GRID_W
    row = jnp.repeat(jnp.arange(rows, dtype=F32), GRID_W)
    col = jnp.tile(jnp.arange(GRID_W, dtype=F32), rows)
    axis_dim = HEAD_DIM // 2
    freqs = jnp.power(ROPE_THETA, -jnp.arange(axis_dim // 2, dtype=F32) * 2.0 / axis_dim)
    ang_r = row[:, None] * freqs[None, :]
    ang_c = col[:, None] * freqs[None, :]
    return jnp.concatenate([ang_r, ang_r, ang_c, ang_c], axis=-1)


def _rope_blocks():
    ang = _rope_angles()
    cos, sin = jnp.cos(ang), jnp.sin(ang)
    lane = np.arange(HEAD_DIM)
    lo = jnp.asarray(((lane % (HEAD_DIM // 2)) < HEAD_DIM // 4).astype(np.float32))
    sin_a = -sin * lo[None, :]
    sin_b = sin * (1.0 - lo)[None, :]
    nblk = DEC_SEQ // SB
    ident = jnp.ones((1, SB, HEAD_DIM), F32)
    zeros = jnp.zeros((1, SB, HEAD_DIM), F32)
    cos_t = jnp.concatenate([ident, cos.reshape(nblk, SB, HEAD_DIM)], axis=0)
    sa_t = jnp.concatenate([zeros, sin_a.reshape(nblk, SB, HEAD_DIM)], axis=0)
    sb_t = jnp.concatenate([zeros, sin_b.reshape(nblk, SB, HEAD_DIM)], axis=0)
    return cos_t, sa_t, sb_t


def _attn_qkv_kernel(mr_ref, ri_ref, y_ref, mod_ref, g_ref, w_ref, qg_ref, kg_ref, cos_ref, sa_ref, sb_ref,
                     q_ref, kb_ref, vb_ref, kf_ref, vf_ref):
    h = _norm_mod(y_ref[...], g_ref[...], mod_ref[0], 0)
    qkv = jnp.dot(h.astype(BF16), w_ref[...], preferred_element_type=F32)
    cos, sa, sb = cos_ref[0], sa_ref[0], sb_ref[0]
    quarter = HEAD_DIM // 4

    def head(x, g):
        xn = _rms(x, g)
        return xn * cos + pltpu.roll(xn, HEAD_DIM - quarter, 1) * sa + pltpu.roll(xn, quarter, 1) * sb

    scale = HEAD_DIM ** -0.5
    for hd in range(N_HEADS):
        sl = slice(hd * HEAD_DIM, (hd + 1) * HEAD_DIM)
        q_ref[:, sl] = (head(qkv[:, sl], qg_ref[...]) * scale).astype(BF16)
    for kv in range(N_KV_HEADS):
        sl = slice(kv * HEAD_DIM, (kv + 1) * HEAD_DIM)
        kr = head(qkv[:, Q_DIM + kv * HEAD_DIM:Q_DIM + (kv + 1) * HEAD_DIM], kg_ref[...])
        kf_ref[:, sl] = kr
        kb_ref[:, sl] = kr.astype(BF16)
    v = qkv[:, Q_DIM + KV_DIM:]
    vf_ref[...] = v
    vb_ref[...] = v.astype(BF16)


def _attn_qkv(y, mods, mrow, g1, w_qkv, q_g, k_g, rope):
    cos_t, sa_t, sb_t = rope
    rspec = pl.BlockSpec((1, SB, HEAD_DIM), lambda j, mr, ri: (ri[j], 0, 0))
    return pl.pallas_call(
        _attn_qkv_kernel,
        out_shape=(jax.ShapeDtypeStruct((N_TOK, Q_DIM), BF16), jax.ShapeDtypeStruct((N_TOK, KV_DIM), BF16),
                   jax.ShapeDtypeStruct((N_TOK, KV_DIM), BF16), jax.ShapeDtypeStruct((N_TOK, KV_DIM), F32),
                   jax.ShapeDtypeStruct((N_TOK, KV_DIM), F32)),
        grid_spec=pltpu.PrefetchScalarGridSpec(
            num_scalar_prefetch=2, grid=(NSB,),
            in_specs=[_tok_spec(D, SB), _mod_spec(), _full_spec((1, D)), _full_spec((D, QKV_DIM)),
                      _full_spec((1, HEAD_DIM)), _full_spec((1, HEAD_DIM)), rspec, rspec, rspec],
            out_specs=(_tok_spec(Q_DIM, SB), _tok_spec(KV_DIM, SB), _tok_spec(KV_DIM, SB), _tok_spec(KV_DIM, SB),
                       _tok_spec(KV_DIM, SB))),
        compiler_params=_cparams(),
        name="attn_qkv",
    )(mrow, jnp.asarray(_ROPE_IDX_SB), y, mods, g1, w_qkv, q_g, k_g, cos_t, sa_t, sb_t)


def _attn_heads(q, ks, vs, o_scr):
    nt = (((1,), (1,)), ((), ()))
    for hd in range(N_HEADS):
        g = hd // GQA_GROUP
        qh = q[:, hd * HEAD_DIM:(hd + 1) * HEAD_DIM]
        gs = slice(g * HEAD_DIM, (g + 1) * HEAD_DIM)
        ss = [lax.dot_general(qh, k[:, gs], nt, preferred_element_type=F32) for k in ks]
        m = functools.reduce(jnp.maximum, [jnp.max(s, axis=-1, keepdims=True) for s in ss])
        ps = [jnp.exp(s - m) for s in ss]
        l = functools.reduce(lambda a, b: a + b, [jnp.sum(p, axis=-1, keepdims=True) for p in ps])
        o = functools.reduce(lambda a, b: a + b,
                             [jnp.dot(p.astype(BF16), v[:, gs], preferred_element_type=F32) for p, v in zip(ps, vs)])
        o_scr[:, hd * HEAD_DIM:(hd + 1) * HEAD_DIM] = (o / l).astype(BF16)


def _attn_ctx_kernel(q_ref, k_ref, v_ref, wo_ref, y_ref, mod_ref, o_ref, o_scr):
    _attn_heads(q_ref[...], [k_ref[...]], [v_ref[...]], o_scr)
    out = jnp.dot(o_scr[...], wo_ref[...], preferred_element_type=F32)
    o_ref[...] = y_ref[...] + mod_ref[0][2:3] * out


def _attn_lat_kernel(q_ref, k_ref, v_ref, ck_ref, cv_ref, wo_ref, y_ref, mod_ref, ctx_out_ref, o_ref, o_scr):
    del ctx_out_ref
    _attn_heads(q_ref[...], [k_ref[...], ck_ref[0].astype(BF16)], [v_ref[...], cv_ref[0].astype(BF16)], o_scr)
    out = jnp.dot(o_scr[...], wo_ref[...], preferred_element_type=F32)
    o_ref[...] = y_ref[...] + mod_ref[0][2:3] * out


def _attention(q, kb, vb, cache_k, cache_v, w_o, y, mods, layer):
    y_ctx = pl.pallas_call(
        _attn_ctx_kernel,
        out_shape=jax.ShapeDtypeStruct((N_TOK, D), F32),
        grid=(BATCH,),
        in_specs=[
            pl.BlockSpec((SEQ, Q_DIM), lambda s: (s, 0)),
            pl.BlockSpec((SEQ, KV_DIM), lambda s: (s, 0)),
            pl.BlockSpec((SEQ, KV_DIM), lambda s: (s, 0)),
            pl.BlockSpec((Q_DIM, D), lambda s: (0, 0)),
            pl.BlockSpec((SEQ, D), lambda s: (s, 0)),
            pl.BlockSpec((1, 6, D), lambda s: (layer * MOD_ROWS, 0, 0)),
        ],
        out_specs=pl.BlockSpec((SEQ, D), lambda s: (s, 0)),
        scratch_shapes=[pltpu.VMEM((SEQ, Q_DIM), BF16)],
        compiler_params=_cparams(),
        name="attn_ctx",
    )(q, kb, vb, w_o, y, mods)
    pb = NP_TOK // DEC_SEQ
    return pl.pallas_call(
        _attn_lat_kernel,
        out_shape=jax.ShapeDtypeStruct((N_TOK, D), F32),
        input_output_aliases={8: 0},
        grid=(DEC_BATCH, SB_PER_DEC),
        in_specs=[
            pl.BlockSpec((SB, Q_DIM), lambda b, t: (NSBP + b * SB_PER_DEC + t, 0)),
            pl.BlockSpec((DEC_SEQ, KV_DIM), lambda b, t: (pb + b, 0)),
            pl.BlockSpec((DEC_SEQ, KV_DIM), lambda b, t: (pb + b, 0)),
            pl.BlockSpec((1, PAST_LEN, KV_DIM), lambda b, t: (b, 0, 0)),
            pl.BlockSpec((1, PAST_LEN, KV_DIM), lambda b, t: (b, 0, 0)),
            pl.BlockSpec((Q_DIM, D), lambda b, t: (0, 0)),
            pl.BlockSpec((SB, D), lambda b, t: (NSBP + b * SB_PER_DEC + t, 0)),
            pl.BlockSpec((1, 6, D), lambda b, t: (layer * MOD_ROWS + 1 + b, 0, 0)),
            pl.BlockSpec(memory_space=pl.ANY),
        ],
        out_specs=pl.BlockSpec((SB, D), lambda b, t: (NSBP + b * SB_PER_DEC + t, 0)),
        scratch_shapes=[pltpu.VMEM((SB, Q_DIM), BF16)],
        compiler_params=_cparams(2),
        name="attn_lat",
    )(q, kb, vb, cache_k, cache_v, w_o, y, mods, y_ctx)


def _log_sigmoid(x):
    return jnp.minimum(x, 0.0) - jnp.log(1.0 + jnp.exp(-jnp.abs(x)))


def _mlstm_in_kernel(mr_ref, y_ref, mod_ref, g_ref, wf_ref, wg_ref, bg_ref, q_ref, k_ref, v_ref, o_ref, gt_ref,
                     w_ref):
    @pl.when(pl.program_id(0) == 0)
    def _():
        for c in range(4):
            w_ref[:, c * D:(c + 1) * D] = wf_ref[0, :, c * D:(c + 1) * D].astype(BF16)

    h = _norm_mod(y_ref[...], g_ref[...], mod_ref[0], 0)
    hb = h.astype(BF16)
    q_ref[...] = jnp.dot(hb, w_ref[:, 0:D], preferred_element_type=F32).astype(BF16)
    k_ref[...] = (jnp.dot(hb, w_ref[:, D:2 * D], preferred_element_type=F32) * (M_HEAD_DIM ** -0.5)).astype(BF16)
    v_ref[...] = jnp.dot(hb, w_ref[:, 2 * D:3 * D], preferred_element_type=F32).astype(BF16)
    o_ref[...] = _sigmoid(jnp.dot(hb, w_ref[:, 3 * D:4 * D], preferred_element_type=F32))
    gt = _dot_hi_lo(h, wg_ref) + bg_ref[...]
    lane = lax.broadcasted_iota(I32, gt.shape, 1)
    is_f = ((lane >= M_HEADS) & (lane < 2 * M_HEADS)) | ((lane >= 3 * M_HEADS) & (lane < 4 * M_HEADS))
    gt_ref[...] = jnp.where(is_f, _log_sigmoid(gt), gt)


def _mlstm_in(y, mods, mrow, g1, w_in_all, slot, w_gate, b_gate):
    w_spec = pl.BlockSpec((1,) + w_in_all.shape[1:], lambda j, *_: (slot, 0, 0), pipeline_mode=pl.Buffered(1))
    return pl.pallas_call(
        _mlstm_in_kernel,
        out_shape=(jax.ShapeDtypeStruct((N_TOK, D), BF16), jax.ShapeDtypeStruct((N_TOK, D), BF16),
                   jax.ShapeDtypeStruct((N_TOK, D), BF16), jax.ShapeDtypeStruct((N_TOK, D), F32),
                   jax.ShapeDtypeStruct((N_TOK, LANE), F32)),
        grid_spec=pltpu.PrefetchScalarGridSpec(
            num_scalar_prefetch=1, grid=(NB,),
            in_specs=[_tok_spec(D), _mod_spec(), _full_spec((1, D)), w_spec,
                      _full_spec((D, 2 * LANE)), _full_spec((1, LANE))],
            out_specs=(_tok_spec(D), _tok_spec(D), _tok_spec(D), _tok_spec(D), _tok_spec(LANE)),
            scratch_shapes=[pltpu.VMEM((D, 4 * D), BF16)]),
        compiler_params=_cparams(),
        name="mlstm_in",
    )(mrow, y, mods, g1, w_in_all, w_gate, b_gate)


def _mlstm_load(hd, c, q_ref, k_ref, v_ref, gc_ref, gr_ref):
    r0 = pl.multiple_of(c * M_CHUNK, M_CHUNK)
    hs = slice(hd * M_HEAD_DIM, (hd + 1) * M_HEAD_DIM)
    rows = pl.ds(r0, M_CHUNK)
    return rows, hs, q_ref[rows, hs], k_ref[rows, hs], v_ref[rows, hs], gc_ref[hd, rows, :], gr_ref[hd, c]


def _mlstm_chunks(chains, ms, loaded, c_scr, n_scr):
    L = M_CHUNK
    n = range(len(chains))
    t_idx = lax.broadcasted_iota(I32, (L, L), 0)
    s_idx = lax.broadcasted_iota(I32, (L, L), 1)
    masks = {0: (s_idx <= t_idx, t_idx <= s_idx), 1: (s_idx >= t_idx, t_idx >= s_idx)}
    q = [ld[2] for ld in loaded]
    k = [ld[3] for ld in loaded]
    v = [ld[4] for ld in loaded]
    i_col = [ld[5][:, 2 * d:2 * d + 1] for (_, d), ld in zip(chains, loaded)]
    lf_col = [ld[5][:, 2 * d + 1:2 * d + 2] for (_, d), ld in zip(chains, loaded)]
    i_row = [ld[6][2 * d:2 * d + 1, :] for (_, d), ld in zip(chains, loaded)]
    lf_row = [ld[6][2 * d + 1:2 * d + 2, :] for (_, d), ld in zip(chains, loaded)]
    mask = [masks[d][0] for _, d in chains]
    mask_t = [masks[d][1] for _, d in chains]
    b_col = [jnp.sum(jnp.where(mask[i], lf_row[i], 0.0), axis=1, keepdims=True) for i in n]
    b_row = [jnp.sum(jnp.where(mask_t[i], lf_col[i], 0.0), axis=0, keepdims=True) for i in n]
    log_d = [jnp.where(mask[i], b_col[i] - b_row[i] + i_row[i], -jnp.inf) for i in n]
    li = [b_col[i] + ms[i] for i in n]
    m_r = [jnp.maximum(li[i], jnp.max(log_d[i], axis=1, keepdims=True)) for i in n]
    a_int = [jnp.exp(li[i] - m_r[i]) for i in n]
    dmat = [jnp.exp(log_d[i] - m_r[i]) for i in n]
    cmat = [c_scr[d, hd] for hd, d in chains]
    nvec = [n_scr[d, hd] for hd, d in chains]
    gram = [lax.dot_general(q[i], k[i], (((1,), (1,)), ((), ())), preferred_element_type=F32) for i in n]
    inter = [jnp.dot(q[i], cmat[i].astype(BF16), preferred_element_type=F32) for i in n]
    s = [gram[i] * dmat[i] for i in n]
    intra = [jnp.dot(s[i].astype(BF16), v[i], preferred_element_type=F32) for i in n]
    qn = [jnp.sum(q[i].astype(F32) * nvec[i], axis=1, keepdims=True) for i in n]
    den = [a_int[i] * qn[i] + jnp.sum(s[i], axis=1, keepdims=True) for i in n]
    hh = [(a_int[i] * inter[i] + intra[i]) / jnp.maximum(jnp.abs(den[i]), jnp.exp(-m_r[i])) for i in n]
    b_last = [b_row[i][:, L - 1:L] if chains[i][1] == 0 else b_row[i][:, 0:1] for i in n]
    log_w = [b_last[i] - b_col[i] + i_col[i] for i in n]
    m_new = [jnp.maximum(b_last[i] + ms[i], jnp.max(log_w[i], axis=0, keepdims=True)) for i in n]
    w = [jnp.exp(log_w[i] - m_new[i]) for i in n]
    decay = [jnp.exp(b_last[i] + ms[i] - m_new[i]) for i in n]
    kw = [k[i].astype(F32) * w[i] for i in n]
    kv = [lax.dot_general(kw[i].astype(BF16), v[i], (((0,), (0,)), ((), ())), preferred_element_type=F32) for i in n]
    for i, (hd, d) in enumerate(chains):
        c_scr[d, hd] = decay[i] * cmat[i] + kv[i]
        n_scr[d, hd] = decay[i] * nvec[i] + jnp.sum(kw[i], axis=0, keepdims=True)
    return hh, m_new


def _mlstm_scan_body(n_chunks, q_ref, k_ref, v_ref, gc_ref, gr_ref, h_ref, hb_scr, c_scr, n_scr, m0):
    chains = [(hd, d) for hd in range(M_HEADS) for d in range(2)]

    def body(c, ms):
        loaded = [_mlstm_load(hd, c if d == 0 else n_chunks - 1 - c, q_ref, k_ref, v_ref, gc_ref, gr_ref)
                  for hd, d in chains]
        hh, m_new = _mlstm_chunks(chains, ms, loaded, c_scr, n_scr)
        for (hd, d), ld, h in zip(chains, loaded, hh):
            dst = h_ref if d == 0 else hb_scr
            dst[ld[0], ld[1]] = h
        return tuple(m_new)

    ms = lax.fori_loop(0, n_chunks, body, tuple(m0))
    h_ref[...] += hb_scr[...]
    return ms


def _mlstm_scan_ctx_kernel(q_ref, k_ref, v_ref, gc_ref, gr_ref, h_ref, cn_ref, nn_ref, mn_ref, hb_scr, c_scr, n_scr):
    c_scr[...] = jnp.zeros(c_scr.shape, F32)
    n_scr[...] = jnp.zeros(n_scr.shape, F32)
    zero = jnp.zeros((1, 1), F32)
    ms = _mlstm_scan_body(SEQ // M_CHUNK, q_ref, k_ref, v_ref, gc_ref, gr_ref, h_ref, hb_scr, c_scr, n_scr,
                          [zero] * (2 * M_HEADS))
    cn_ref[0] = c_scr[...]
    nn_ref[0] = n_scr[...]
    for hd in range(M_HEADS):
        for d in range(2):
            mn_ref[0, d, hd] = jnp.broadcast_to(ms[2 * hd + d], (1, LANE))


def _mlstm_scan_lat_kernel(q_ref, k_ref, v_ref, gc_ref, gr_ref, c0_ref, n0_ref, m0_ref, ctx_out_ref, h_ref,
                           hb_scr, c_scr, n_scr):
    del ctx_out_ref
    c_scr[...] = c0_ref[0]
    n_scr[...] = n0_ref[0]
    m0 = [m0_ref[0, d, hd] for hd in range(M_HEADS) for d in range(2)]
    _mlstm_scan_body(DEC_SEQ // M_CHUNK, q_ref, k_ref, v_ref, gc_ref, gr_ref, h_ref, hb_scr, c_scr, n_scr, m0)


def _mlstm_scan(q, k, v, gates, state_c, state_n, state_m):
    g16 = gates[:, :4 * M_HEADS].reshape(N_TOK, 4, M_HEADS)
    gcol = jnp.transpose(g16, (2, 0, 1))
    grow = jnp.transpose(g16.reshape(N_TOK // M_CHUNK, M_CHUNK, 4, M_HEADS), (3, 0, 2, 1))
    hd = M_HEAD_DIM
    state_scratch = [pltpu.VMEM((2, M_HEADS, hd, hd), F32), pltpu.VMEM((2, M_HEADS, 1, hd), F32)]
    ncp = SEQ // M_CHUNK
    h_ctx, new_c, new_n, new_m = pl.pallas_call(
        _mlstm_scan_ctx_kernel,
        out_shape=(jax.ShapeDtypeStruct((N_TOK, D), F32),
                   jax.ShapeDtypeStruct((BATCH, 2, M_HEADS, hd, hd), F32),
                   jax.ShapeDtypeStruct((BATCH, 2, M_HEADS, 1, hd), F32),
                   jax.ShapeDtypeStruct((BATCH, 2, M_HEADS, 1, LANE), F32)),
        grid=(BATCH,),
        in_specs=[
            pl.BlockSpec((SEQ, D), lambda s: (s, 0)),
            pl.BlockSpec((SEQ, D), lambda s: (s, 0)),
            pl.BlockSpec((SEQ, D), lambda s: (s, 0)),
            pl.BlockSpec((M_HEADS, SEQ, 4), lambda s: (0, s, 0)),
            pl.BlockSpec((M_HEADS, ncp, 4, M_CHUNK), lambda s: (0, s, 0, 0)),
        ],
        out_specs=(
            pl.BlockSpec((SEQ, D), lambda s: (s, 0)),
            pl.BlockSpec((1, 2, M_HEADS, hd, hd), lambda s: (s, 0, 0, 0, 0)),
            pl.BlockSpec((1, 2, M_HEADS, 1, hd), lambda s: (s, 0, 0, 0, 0)),
            pl.BlockSpec((1, 2, M_HEADS, 1, LANE), lambda s: (s, 0, 0, 0, 0)),
        ),
        scratch_shapes=[pltpu.VMEM((SEQ, D), F32)] + state_scratch,
        compiler_params=_cparams(),
        name="mlstm_scan_ctx",
    )(q, k, v, gcol, grow)
    ncl = DEC_SEQ // M_CHUNK
    pb = NP_TOK // DEC_SEQ
    h_all = pl.pallas_call(
        _mlstm_scan_lat_kernel,
        out_shape=jax.ShapeDtypeStruct((N_TOK, D), F32),
        input_output_aliases={8: 0},
        grid=(DEC_BATCH,),
        in_specs=[
            pl.BlockSpec((DEC_SEQ, D), lambda b: (pb + b, 0)),
            pl.BlockSpec((DEC_SEQ, D), lambda b: (pb + b, 0)),
            pl.BlockSpec((DEC_SEQ, D), lambda b: (pb + b, 0)),
            pl.BlockSpec((M_HEADS, DEC_SEQ, 4), lambda b: (0, pb + b, 0)),
            pl.BlockSpec((M_HEADS, ncl, 4, M_CHUNK), lambda b: (0, pb + b, 0, 0)),
            pl.BlockSpec((1, 2, M_HEADS, hd, hd), lambda b: (b, 0, 0, 0, 0)),
            pl.BlockSpec((1, 2, M_HEADS, 1, hd), lambda b: (b, 0, 0, 0, 0)),
            pl.BlockSpec((1, 2, M_HEADS, 1, 1), lambda b: (b, 0, 0, 0, 0)),
            pl.BlockSpec(memory_space=pl.ANY),
        ],
        out_specs=pl.BlockSpec((DEC_SEQ, D), lambda b: (pb + b, 0)),
        scratch_shapes=[pltpu.VMEM((DEC_SEQ, D), F32)] + state_scratch,
        compiler_params=_cparams(),
        name="mlstm_scan_lat",
    )(q, k, v, gcol, grow, state_c, state_n, state_m, h_ctx)
    return h_all, new_c, new_n, new_m


def _mlstm_out_kernel(mr_ref, h_ref, o_ref, ng_ref, w_ref, y_ref, mod_ref, out_ref, x_scr):
    hc = o_ref[...] * h_ref[...]
    for hd in range(M_HEADS):
        sl = slice(hd * M_HEAD_DIM, (hd + 1) * M_HEAD_DIM)
        x_scr[:, sl] = _rms(hc[:, sl], ng_ref[:, sl]).astype(BF16)
    out = jnp.dot(x_scr[...], w_ref[...], preferred_element_type=F32)
    out_ref[...] = y_ref[...] + mod_ref[0][2:3] * out


def _mlstm_out(hsum, o, norm_g, w_out, y, mods, mrow):
    return pl.pallas_call(
        _mlstm_out_kernel,
        out_shape=jax.ShapeDtypeStruct((N_TOK, D), F32),
        grid_spec=pltpu.PrefetchScalarGridSpec(
            num_scalar_prefetch=1, grid=(NB,),
            in_specs=[_tok_spec(D), _tok_spec(D), _full_spec((1, D)), _full_spec((D, D)), _tok_spec(D), _mod_spec()],
            out_specs=_tok_spec(D),
            scratch_shapes=[pltpu.VMEM((TM, D), BF16)]),
        compiler_params=_cparams(),
        name="mlstm_out",
    )(mrow, hsum, o, norm_g, w_out, y, mods)


ROUTE_OFF = N_GROUPS
SLAB = D // (2 * LANE)
V7X_SC_CORES = 2
V7X_SC_SUBCORES = 16
SC_WORKERS = V7X_SC_CORES * V7X_SC_SUBCORES
SC_WINDOW = 128
HI_MASK = -65536


def _bf16_bits(x):
    return lax.bitcast_convert_type(x.astype(BF16).astype(F32), I32)


def _store_slabs(ref, x):
    rows = x.shape[0]
    for c in range(SLAB):
        lo = lax.shift_right_logical(_bf16_bits(x[:, (2 * c) * LANE:(2 * c + 1) * LANE]), 16)
        hi = _bf16_bits(x[:, (2 * c + 1) * LANE:(2 * c + 2) * LANE]) & HI_MASK
        ref[pl.ds(c, rows, stride=SLAB), :] = lo | hi


def _load_slabs(ref, dst, rows, dtype):
    for c in range(SLAB):
        w = ref[pl.ds(c, rows, stride=SLAB), :]
        lo = lax.bitcast_convert_type(lax.shift_left(w, 16), F32)
        hi = lax.bitcast_convert_type(w & HI_MASK, F32)
        dst[:, (2 * c) * LANE:(2 * c + 1) * LANE] = lo.astype(dtype)
        dst[:, (2 * c + 1) * LANE:(2 * c + 2) * LANE] = hi.astype(dtype)


def _route_kernel(mr_ref, y_ref, mod_ref, g_ref, wr_ref, br_ref, x_ref, id_ref, wt_ref, rk_ref, cnt_ref, cnt_scr):
    x = _norm_mod(y_ref[...], g_ref[...], mod_ref[0], 1)
    _store_slabs(x_ref, x)
    lg = _dot_hi_lo(x, wr_ref) + br_ref[...]
    lane = lax.broadcasted_iota(I32, lg.shape, 1).astype(F32)
    ninf = -jnp.inf
    big = float(LANE)
    lgg = jnp.where(lane < N_GROUPS, lg, ninf)
    gmax = jnp.max(lgg, axis=-1, keepdims=True)
    g_idx = jnp.min(jnp.where(lgg == gmax, lane, big), axis=-1, keepdims=True)
    g_w = 1.0 / jnp.sum(jnp.exp(lgg - gmax), axis=-1, keepdims=True)
    lo = ROUTE_OFF + g_idx * EXPERTS_PER_GROUP
    le = jnp.where((lane >= lo) & (lane < lo + EXPERTS_PER_GROUP), lg, ninf)
    m1 = jnp.max(le, axis=-1, keepdims=True)
    i1 = jnp.min(jnp.where(le == m1, lane, big), axis=-1, keepdims=True)
    le2 = jnp.where(lane == i1, ninf, le)
    m2 = jnp.max(le2, axis=-1, keepdims=True)
    i2 = jnp.min(jnp.where(le2 == m2, lane, big), axis=-1, keepdims=True)
    r = jnp.exp(m2 - m1)
    p1 = 1.0 / (1.0 + r)
    p2 = r / (1.0 + r)
    two = lax.broadcasted_iota(I32, (x.shape[0], TOP_K), 1)
    id_ref[...] = (jnp.where(two == 0, i1, i2) - ROUTE_OFF).astype(I32)
    wt_ref[...] = jnp.where(two == 0, g_w * p1, g_w * p2)
    @pl.when(pl.program_id(0) == 0)
    def _():
        cnt_scr[...] = jnp.zeros(cnt_scr.shape, F32)

    rows = x.shape[0]
    oh1 = (lane == i1).astype(F32)
    oh2 = (lane == i2).astype(F32)
    both = oh1 + oh2
    tri = (lax.broadcasted_iota(I32, (rows, rows), 1) < lax.broadcasted_iota(I32, (rows, rows), 0)).astype(BF16)
    before = jnp.dot(tri, both.astype(BF16), preferred_element_type=F32) + cnt_scr[...]
    rk1 = jnp.sum(oh1 * before, axis=-1, keepdims=True)
    rk2 = jnp.sum(oh2 * before, axis=-1, keepdims=True)
    rk_ref[...] = jnp.where(two == 0, rk1, rk2).astype(I32)
    cnt_scr[...] = cnt_scr[...] + jnp.sum(both, axis=0, keepdims=True)
    cnt_ref[...] = cnt_scr[...]


def _route(y, mods, mrow, g2, w_route, b_route):
    return pl.pallas_call(
        _route_kernel,
        out_shape=(jax.ShapeDtypeStruct((N_TOK * SLAB, LANE), I32), jax.ShapeDtypeStruct((N_TOK, TOP_K), I32),
                   jax.ShapeDtypeStruct((N_TOK, TOP_K), F32), jax.ShapeDtypeStruct((N_TOK, TOP_K), I32),
                   jax.ShapeDtypeStruct((1, LANE), F32)),
        grid_spec=pltpu.PrefetchScalarGridSpec(
            num_scalar_prefetch=1, grid=(NB,),
            in_specs=[_tok_spec(D), _mod_spec(), _full_spec((1, D)), _full_spec((D, 2 * LANE)), _full_spec((1, LANE))],
            out_specs=(pl.BlockSpec((TM * SLAB, LANE), lambda j, *_: (j, 0)), _tok_spec(TOP_K), _tok_spec(TOP_K),
                       _tok_spec(TOP_K), _full_spec((1, LANE))),
            scratch_shapes=[pltpu.VMEM((1, LANE), F32)]),
        compiler_params=_cparams(),
        name="moe_route",
    )(mrow, y, mods, g2, w_route, b_route)


def _dispatch_tables(expert_id, rank, lane_counts):
    counts = lane_counts[0, ROUTE_OFF:ROUTE_OFF + N_EXPERTS].astype(I32)
    padded = ((counts + EBLK - 1) // EBLK) * EBLK
    pad_end = jnp.cumsum(padded)
    pad_start = pad_end - padded
    dest = jnp.take(pad_start, expert_id, axis=0) + rank
    n_blk = (padded // EBLK).astype(I32)
    blk_start = (pad_start // EBLK).astype(I32)
    n_used = (pad_end[-1] // EBLK).astype(I32).reshape(1)
    return dest[:, 0].reshape(1, N_TOK), dest[:, 1].reshape(1, N_TOK), blk_start, n_blk, n_used


def _sc_mesh():
    return plsc.VectorSubcoreMesh(core_axis_name="core", subcore_axis_name="subcore",
                                  num_cores=V7X_SC_CORES, num_subcores=V7X_SC_SUBCORES)


def _sc_worker():
    return lax.axis_index("core") * V7X_SC_SUBCORES + lax.axis_index("subcore")


def _sc_dispatch(x_slabs, d0, d1):
    per = N_TOK // SC_WORKERS

    @functools.partial(
        pl.kernel, out_type=jax.ShapeDtypeStruct((P_SLOTS, SLAB, LANE), I32), mesh=_sc_mesh(), name="moe_dispatch",
        scratch_types=[pltpu.VMEM((1, per), I32), pltpu.VMEM((1, per), I32), pltpu.VMEM((SC_WINDOW, SLAB, LANE), I32)])
    def run(x_hbm, d0_hbm, d1_hbm, o_hbm, i0_v, i1_v, buf):
        base = _sc_worker() * per
        pltpu.sync_copy(d0_hbm.at[:, pl.ds(base, per)], i0_v)
        pltpu.sync_copy(d1_hbm.at[:, pl.ds(base, per)], i1_v)

        @pl.loop(0, per // SC_WINDOW)
        def _(s):
            off = s * SC_WINDOW
            pltpu.sync_copy(x_hbm.at[pl.ds(base + off, SC_WINDOW)], buf)
            pltpu.sync_copy(buf, o_hbm.at[i0_v.at[0, pl.ds(off, SC_WINDOW)]])
            pltpu.sync_copy(buf, o_hbm.at[i1_v.at[0, pl.ds(off, SC_WINDOW)]])

    return run(x_slabs.reshape(N_TOK, SLAB, LANE), d0, d1)


def _sc_collect(y_slabs, dcat):
    per = N_ASSIGN // SC_WORKERS

    @functools.partial(
        pl.kernel, out_type=jax.ShapeDtypeStruct((N_ASSIGN, SLAB, LANE), I32), mesh=_sc_mesh(), name="moe_collect",
        scratch_types=[pltpu.VMEM((1, per), I32), pltpu.VMEM((SC_WINDOW, SLAB, LANE), I32)])
    def run(y_hbm, i_hbm, o_hbm, i_v, buf):
        base = _sc_worker() * per
        pltpu.sync_copy(i_hbm.at[:, pl.ds(base, per)], i_v)

        @pl.loop(0, per // SC_WINDOW)
        def _(s):
            off = s * SC_WINDOW
            pltpu.sync_copy(y_hbm.at[i_v.at[0, pl.ds(off, SC_WINDOW)]], buf)
            pltpu.sync_copy(buf, o_hbm.at[pl.ds(base + off, SC_WINDOW)])

    return run(y_slabs.reshape(P_SLOTS, SLAB, LANE), dcat)


EROWS = EBLK * SLAB


W_CHUNKS = 8
W_SLOTS = 2


def _expert_kernel(layer, bs_ref, nb_ref, nu_ref, wg_hbm, wu_hbm, wd_hbm, x_hbm, y_hbm,
                   xbuf, ybuf, xs, wg_f, wu_f, wd_f, wg_bf, wu_bf, wd_bf, isem, osem, wsem):
    e = pl.program_id(0)
    n_exp = pl.num_programs(0)
    n_used = nu_ref[0]
    b0 = bs_ref[e]
    nb = nb_ref[e]
    wslot = lax.rem(e, W_SLOTS)

    def weight_copies(ex, slot):
        out = []
        for hbm, buf in ((wg_hbm, wg_f), (wu_hbm, wu_f), (wd_hbm, wd_f)):
            rows = buf.shape[1] // W_CHUNKS
            for c in range(W_CHUNKS):
                rs = pl.ds(c * rows, rows)
                out.append(pltpu.make_async_copy(hbm.at[layer, ex, rs], buf.at[slot, rs], wsem.at[slot]))
        return out

    def start_weights(ex, slot):
        for i, cp in enumerate(weight_copies(ex, slot)):
            cp.start(priority=i % 2)

    ahead = W_SLOTS - 1

    @pl.when(e == 0)
    def _():
        for ex in range(ahead):
            start_weights(ex, ex)

    for cp in weight_copies(e, wslot):
        cp.wait()

    @pl.when(e + ahead < n_exp)
    def _():
        start_weights(e + ahead, lax.rem(e + ahead, W_SLOTS))

    def in_copy(g, slot):
        return pltpu.make_async_copy(x_hbm.at[pl.ds(pl.multiple_of(g * EROWS, EROWS), EROWS)], xbuf.at[slot],
                                     isem.at[slot])

    def out_copy(g, slot):
        return pltpu.make_async_copy(ybuf.at[slot], y_hbm.at[pl.ds(pl.multiple_of(g * EROWS, EROWS), EROWS)],
                                     osem.at[slot])

    @pl.when(e == 0)
    def _():
        in_copy(0, 0).start(priority=1)

    @pl.when(nb > 0)
    def _():
        wg_bf[...] = wg_f[wslot].astype(BF16)
        wu_bf[...] = wu_f[wslot].astype(BF16)
        wd_bf[...] = wd_f[wslot].astype(BF16)

    def block(k, carry):
        g = b0 + k
        slot = lax.rem(g, 2)
        in_copy(g, slot).wait()

        @pl.when(g + 1 < n_used)
        def _():
            in_copy(g + 1, 1 - slot).start(priority=1)

        _load_slabs(xbuf.at[slot], xs, EBLK, BF16)
        xb = xs[...]
        gt = jnp.dot(xb, wg_bf[...], preferred_element_type=F32)
        up = jnp.dot(xb, wu_bf[...], preferred_element_type=F32)
        hmid = (gt * _sigmoid(gt) * up).astype(BF16)
        res = jnp.dot(hmid, wd_bf[...], preferred_element_type=F32)

        @pl.when(g >= 2)
        def _():
            out_copy(g - 2, slot).wait()

        _store_slabs(ybuf.at[slot], res)
        out_copy(g, slot).start()
        return carry

    lax.fori_loop(0, nb, block, 0)

    @pl.when(e == n_exp - 1)
    def _():
        last = n_used - 1
        out_copy(last, lax.rem(last, 2)).wait()

        @pl.when(n_used >= 2)
        def _():
            out_copy(last - 1, lax.rem(last - 1, 2)).wait()


def _experts(x_sorted, blk_start, n_blk, n_used, w_gate, w_up, w_down, layer):
    any_spec = pl.BlockSpec(memory_space=pl.ANY)
    return pl.pallas_call(
        functools.partial(_expert_kernel, layer),
        out_shape=jax.ShapeDtypeStruct((P_SLOTS * SLAB, LANE), I32),
        grid_spec=pltpu.PrefetchScalarGridSpec(
            num_scalar_prefetch=3, grid=(N_EXPERTS,),
            in_specs=[any_spec, any_spec, any_spec, any_spec],
            out_specs=any_spec,
            scratch_shapes=[
                pltpu.VMEM((2, EROWS, LANE), I32), pltpu.VMEM((2, EROWS, LANE), I32),
                pltpu.VMEM((EBLK, D), BF16),
                pltpu.VMEM((W_SLOTS, D, D_EXPERT), F32), pltpu.VMEM((W_SLOTS, D, D_EXPERT), F32),
                pltpu.VMEM((W_SLOTS, D_EXPERT, D), F32),
                pltpu.VMEM((D, D_EXPERT), BF16), pltpu.VMEM((D, D_EXPERT), BF16), pltpu.VMEM((D_EXPERT, D), BF16),
                pltpu.SemaphoreType.DMA((2,)), pltpu.SemaphoreType.DMA((2,)), pltpu.SemaphoreType.DMA((W_SLOTS,)),
            ]),
        compiler_params=_cparams(),
        name="moe_experts",
    )(blk_start, n_blk, n_used, w_gate, w_up, w_down, x_sorted.reshape(P_SLOTS * SLAB, LANE))


def _combine_kernel(final, mr_ref, e0_ref, e1_ref, wt_ref, y_ref, mod_ref, fg_ref, o_ref, a_scr, b_scr):
    _load_slabs(e0_ref, a_scr, TM, F32)
    _load_slabs(e1_ref, b_scr, TM, F32)
    wt = wt_ref[...]
    moe = wt[:, 0:1] * a_scr[...] + wt[:, 1:2] * b_scr[...]
    y_new = y_ref[...] + mod_ref[0][5:6] * moe
    o_ref[...] = _rms(y_new, fg_ref[...]) if final else y_new


def _combine(ym, wts, y, mods, mrow, final_g, blk0, nblk, final):
    tok = lambda width: pl.BlockSpec((TM, width), lambda j, *_: (blk0 + j, 0))
    slab0 = pl.BlockSpec((TM * SLAB, LANE), lambda j, *_: (blk0 + j, 0))
    slab1 = pl.BlockSpec((TM * SLAB, LANE), lambda j, *_: (NB + blk0 + j, 0))
    mod = pl.BlockSpec((1, 6, D), lambda j, mr: (mr[blk0 + j], 0, 0))
    return pl.pallas_call(
        functools.partial(_combine_kernel, final),
        out_shape=jax.ShapeDtypeStruct((nblk * TM, D), F32),
        grid_spec=pltpu.PrefetchScalarGridSpec(
            num_scalar_prefetch=1, grid=(nblk,),
            in_specs=[slab0, slab1, tok(TOP_K), tok(D), mod, _full_spec((1, D))],
            out_specs=pl.BlockSpec((TM, D), lambda j, *_: (j, 0)),
            scratch_shapes=[pltpu.VMEM((TM, D), F32), pltpu.VMEM((TM, D), F32)]),
        compiler_params=_cparams(),
        name="moe_combine",
    )(mrow, ym, ym, wts, y, mods, final_g)


def kernel(x_prompt, x_sample, cache_attn_k, cache_attn_v, state_mlstm_C, state_mlstm_n, state_mlstm_m, c, c_ctx, ada_w, ada_b, norm1_g, norm2_g, conv_w_in, conv_w_dw, conv_b_dw, conv_ln_g, conv_ln_b, conv_w_out, attn_w_qkv, attn_q_norm, attn_k_norm, attn_w_o, mlstm_w_in, mlstm_b_gate, mlstm_norm_g, mlstm_w_out, moe_w_group, moe_b_group, moe_w_router, moe_b_router, moe_w_gate, moe_w_up, moe_w_down, final_norm_g):
    y = jnp.concatenate([x_prompt.reshape(NP_TOK, D), x_sample.reshape(NS_TOK, D)], axis=0)
    cvec = jnp.concatenate([c_ctx[None, :], c, jnp.zeros((MOD_ROWS - 1 - DEC_BATCH, D), F32)], axis=0)
    mods = _ada_all(cvec, ada_w, ada_b)
    rope = _rope_blocks()
    new_k = new_v = new_c = new_n = new_m = None
    for i in range(DEPTH):
        kind, slot = i % 3, i // 3
        mrow = jnp.asarray(_MOD_ROW + i * MOD_ROWS)
        mrow_sb = jnp.asarray(_MOD_ROW_SB + i * MOD_ROWS)
        g1 = norm1_g[i].reshape(1, D)
        if kind == 0:
            u = _conv_in(y, mods, mrow, g1, conv_w_in[slot].astype(BF16))
            w_dw = jnp.concatenate([conv_w_dw[slot], jnp.zeros((1, D), F32)], axis=0)
            y = _conv_main(u, y, mods, mrow_sb, w_dw, conv_b_dw[slot].reshape(1, D), conv_ln_g[slot].reshape(1, D),
                           conv_ln_b[slot].reshape(1, D), conv_w_out[slot].astype(BF16))
        elif kind == 1:
            q, kb, vb, kf, vf = _attn_qkv(y, mods, mrow_sb, g1, attn_w_qkv[slot].astype(BF16),
                                          attn_q_norm[slot].reshape(1, HEAD_DIM), attn_k_norm[slot].reshape(1, HEAD_DIM),
                                          rope)
            new_k = kf[:NP_TOK].reshape(BATCH, 1, SEQ, N_KV_HEADS, HEAD_DIM)
            new_v = vf[:NP_TOK].reshape(BATCH, 1, SEQ, N_KV_HEADS, HEAD_DIM)
            ck = cache_attn_k[:, slot].reshape(DEC_BATCH, PAST_LEN, KV_DIM)
            cv = cache_attn_v[:, slot].reshape(DEC_BATCH, PAST_LEN, KV_DIM)
            y = _attention(q, kb, vb, ck, cv, attn_w_o[slot].astype(BF16), y, mods, i)
        else:
            w_in = mlstm_w_in[slot]
            w_gate = jnp.concatenate([w_in[:, 4 * D:], jnp.zeros((D, LANE - 4 * M_HEADS), F32)], axis=1)
            b_gate = jnp.concatenate([mlstm_b_gate[slot], jnp.zeros((LANE - 4 * M_HEADS,), F32)]).reshape(1, LANE)
            q, k, v, o, gates = _mlstm_in(y, mods, mrow, g1, mlstm_w_in, slot, _split_hi_lo(w_gate), b_gate)
            sc = state_mlstm_C[:, slot]
            sn = state_mlstm_n[:, slot].reshape(DEC_BATCH, 2, M_HEADS, 1, M_HEAD_DIM)
            sm = state_mlstm_m[:, slot].reshape(DEC_BATCH, 2, M_HEADS, 1, 1)
            hsum, nc_, nn_, nm_ = _mlstm_scan(q, k, v, gates, sc, sn, sm)
            new_c = nc_[:, None]
            new_n = nn_.reshape(BATCH, 1, 2, M_HEADS, M_HEAD_DIM)
            new_m = nm_[..., 0, 0].reshape(BATCH, 1, 2, M_HEADS)
            y = _mlstm_out(hsum, o, mlstm_norm_g[slot].reshape(1, D), mlstm_w_out[slot].astype(BF16), y, mods, mrow)
        w_route = jnp.concatenate([moe_w_group[i], moe_w_router[i],
                                   jnp.zeros((D, LANE - N_GROUPS - N_EXPERTS), F32)], axis=1)
        b_route = jnp.concatenate([moe_b_group[i], moe_b_router[i],
                                   jnp.zeros((LANE - N_GROUPS - N_EXPERTS,), F32)]).reshape(1, LANE)
        x2, eid, ewt, rank, cnt = _route(y, mods, mrow, norm2_g[i].reshape(1, D), _split_hi_lo(w_route), b_route)
        d0, d1, blk_start, n_blk, n_used = _dispatch_tables(eid, rank, cnt)
        x_sorted = _sc_dispatch(x2, d0, d1)
        y_sorted = _experts(x_sorted, blk_start, n_blk, n_used, moe_w_gate, moe_w_up, moe_w_down, i)
        ym = _sc_collect(y_sorted, jnp.concatenate([d0, d1], axis=1))
        ym = ym.reshape(N_ASSIGN * SLAB, LANE)
        fg = final_norm_g.reshape(1, D)
        if i + 1 < DEPTH:
            y = _combine(ym, ewt, y, mods, mrow, fg, 0, NB, False)
        else:
            y_prompt = _combine(ym, ewt, y, mods, mrow, fg, 0, NBP, True).reshape(BATCH, SEQ, D)
            y_sample = _combine(ym, ewt, y, mods, mrow, fg, NBP, NB - NBP, True).reshape(DEC_BATCH, DEC_SEQ, D)
    return (y_prompt, y_sample, new_k, new_v, new_c, new_n, new_m)
```

```python
import functools

import jax
import jax.numpy as jnp
import numpy as np
from jax import lax
from jax.experimental import pallas as pl
from jax.experimental.pallas import tpu as pltpu
from jax.experimental.pallas import tpu_sc as plsc

F32 = jnp.float32
BF16 = jnp.bfloat16
I32 = jnp.int32

D = 1024
BATCH, SEQ = 16, 256
DEC_BATCH, DEC_SEQ = 8, 1024
PAST_LEN = 256
DEPTH = 4
GRID_W = 64
EPS = 1e-6
CONV_WIDTH = 31
CONV_PAD = CONV_WIDTH // 2
HEAD_DIM = 128
N_HEADS = 8
N_KV_HEADS = 2
GQA_GROUP = N_HEADS // N_KV_HEADS
Q_DIM = N_HEADS * HEAD_DIM
KV_DIM = N_KV_HEADS * HEAD_DIM
QKV_DIM = Q_DIM + 2 * KV_DIM
ROPE_THETA = 10000.0
M_HEADS = 4
M_HEAD_DIM = D // M_HEADS
M_CHUNK = 64
N_GROUPS = 4
EXPERTS_PER_GROUP = 8
N_EXPERTS = N_GROUPS * EXPERTS_PER_GROUP
TOP_K = 2
D_EXPERT = 512

NP_TOK = BATCH * SEQ
NS_TOK = DEC_BATCH * DEC_SEQ
N_TOK = NP_TOK + NS_TOK
TM = 512
NB = N_TOK // TM
NBP = NP_TOK // TM
BLK_PER_DEC = DEC_SEQ // TM
SB = 256
NSB = N_TOK // SB
NSBP = NP_TOK // SB
SB_PER_DEC = DEC_SEQ // SB
MOD_ROWS = 16
HALO = 16
LANE = 128
SUBLANE = 8

N_ASSIGN = N_TOK * TOP_K
EBLK = 256
N_EBLK = N_ASSIGN // EBLK + N_EXPERTS
P_SLOTS = N_EBLK * EBLK
N_PAD_SLOTS = P_SLOTS - N_ASSIGN

VMEM_LIMIT = 56 * 1024 * 1024


def _block_tables(nb, nbp, per_dec):
    j = np.arange(nb)
    is_p = j < nbp
    mod_row = np.where(is_p, 0, 1 + (j - nbp) // per_dec)
    rope_idx = np.where(is_p, 0, 1 + (j - nbp) % per_dec)
    first = np.where(is_p, 1, ((j - nbp) % per_dec == 0).astype(np.int64))
    last = np.where(is_p, 1, ((j - nbp) % per_dec == per_dec - 1).astype(np.int64))
    return (mod_row.astype(np.int32), rope_idx.astype(np.int32), first.astype(np.int32), last.astype(np.int32))


_MOD_ROW, _, _, _ = _block_tables(NB, NBP, BLK_PER_DEC)
_MOD_ROW_SB, _ROPE_IDX_SB, _SEQ_FIRST, _SEQ_LAST = _block_tables(NSB, NSBP, SB_PER_DEC)


def _cparams(n_axes=1):
    return pltpu.CompilerParams(dimension_semantics=("arbitrary",) * n_axes, vmem_limit_bytes=VMEM_LIMIT)


def _sigmoid(x):
    return 1.0 / (1.0 + jnp.exp(-x))


def _rms(x, g):
    return x * lax.rsqrt(jnp.mean(x * x, axis=-1, keepdims=True) + EPS) * g


def _split_hi_lo(w):
    hi = w.astype(BF16)
    return jnp.concatenate([hi, (w - hi.astype(F32)).astype(BF16)], axis=1)


def _dot_hi_lo(x, w_ref):
    n = w_ref.shape[1] // 2
    xh = x.astype(BF16)
    xl = (x - xh.astype(F32)).astype(BF16)
    both = jnp.dot(xh, w_ref[...], preferred_element_type=F32)
    return both[:, :n] + (both[:, n:] + jnp.dot(xl, w_ref[:, :n], preferred_element_type=F32))


def _norm_mod(y, g, mod, which):
    shift = mod[3 * which:3 * which + 1]
    scale = mod[3 * which + 1:3 * which + 2]
    return _rms(y, g) * (1.0 + scale) + shift


def _ada_kernel(c_ref, w_ref, b_ref, o_ref):
    c = c_ref[...]
    s = c * _sigmoid(c)
    o_ref[0] = jnp.dot(s.astype(BF16), w_ref[0].astype(BF16), preferred_element_type=F32) + b_ref[0]


def _ada_all(cvec, ada_w, ada_b):
    tn = 1536
    out = pl.pallas_call(
        _ada_kernel,
        out_shape=jax.ShapeDtypeStruct((DEPTH, MOD_ROWS, 6 * D), F32),
        grid=(DEPTH, 6 * D // tn),
        in_specs=[
            pl.BlockSpec((MOD_ROWS, D), lambda l, n: (0, 0)),
            pl.BlockSpec((1, D, tn), lambda l, n: (l, 0, n)),
            pl.BlockSpec((1, 1, tn), lambda l, n: (l, 0, n)),
        ],
        out_specs=pl.BlockSpec((1, MOD_ROWS, tn), lambda l, n: (l, 0, n)),
        compiler_params=_cparams(2),
        name="ada_mod",
    )(cvec, ada_w, ada_b.reshape(DEPTH, 1, 6 * D))
    return out.reshape(DEPTH * MOD_ROWS, 6, D)


def _tok_spec(width, rows=TM):
    return pl.BlockSpec((rows, width), lambda j, *_: (j, 0))


def _mod_spec():
    return pl.BlockSpec((1, 6, D), lambda j, mr, *_: (mr[j], 0, 0))


def _full_spec(shape):
    nd = len(shape)
    return pl.BlockSpec(shape, lambda j, *_: (0,) * nd)


def _conv_in_kernel(mr_ref, y_ref, mod_ref, g_ref, w_ref, u_ref):
    h = _norm_mod(y_ref[...], g_ref[...], mod_ref[0], 0)
    ag = jnp.dot(h.astype(BF16), w_ref[...], preferred_element_type=F32)
    u_ref[...] = ag[:, :D] * _sigmoid(ag[:, D:])


def _conv_in(y, mods, mrow, g1, w_in):
    return pl.pallas_call(
        _conv_in_kernel,
        out_shape=jax.ShapeDtypeStruct((N_TOK, D), F32),
        grid_spec=pltpu.PrefetchScalarGridSpec(
            num_scalar_prefetch=1, grid=(NB,),
            in_specs=[_tok_spec(D), _mod_spec(), _full_spec((1, D)), _full_spec((D, 2 * D))],
            out_specs=_tok_spec(D)),
        compiler_params=_cparams(),
        name="conv_in",
    )(mrow, y, mods, g1, w_in)


def _conv_main_kernel(mr_ref, first_ref, last_ref, u_ref, up_ref, un_ref, wdw_ref, bdw_ref, lg_ref, lb_ref,
                      wout_ref, y_ref, mod_ref, o_ref, ext_ref, acc_ref):
    j = pl.program_id(0)
    zero = jnp.zeros((HALO, D), F32)
    ext_ref[0:HALO, :] = jnp.where(first_ref[j] == 1, zero, up_ref[...])
    ext_ref[HALO:HALO + SB, :] = u_ref[...]
    ext_ref[HALO + SB:2 * HALO + SB, :] = jnp.where(last_ref[j] == 1, zero, un_ref[...])

    off0 = HALO - CONV_PAD
    n_a = (off0 + CONV_WIDTH - 1) // SUBLANE + 1
    n_chunks = SB // SUBLANE

    def strip(ci, carry):
        cs = pl.ds(pl.multiple_of(ci * LANE, LANE), LANE)
        wk = [jnp.broadcast_to(wdw_ref[k:k + 1, cs], (SUBLANE, LANE)) for k in range(CONV_WIDTH)]
        bias = jnp.broadcast_to(bdw_ref[:, cs], (SUBLANE, LANE))
        sub = lax.broadcasted_iota(I32, (SUBLANE, LANE), 0)
        prev_rot, prev_v0 = None, None
        for j in range(n_chunks + 1):
            tiles = [ext_ref[SUBLANE * (j + a):SUBLANE * (j + a + 1), cs] for a in range(n_a)]
            part = []
            for s in range(SUBLANE):
                acc = None
                for a in range(n_a):
                    k = SUBLANE * a + s - off0
                    if (0 <= k < CONV_WIDTH) and not (s == 0 and j == n_chunks):
                        term = tiles[a] * wk[k]
                        acc = term if acc is None else acc + term
                part.append(acc)
            rot = [None] + [pltpu.roll(part[s], SUBLANE - s, 0) for s in range(1, SUBLANE)]
            if j >= 1:
                out = prev_v0 + bias
                for s in range(1, SUBLANE):
                    out = out + jnp.where(sub < SUBLANE - s, prev_rot[s], rot[s])
                acc_ref[SUBLANE * (j - 1):SUBLANE * j, cs] = out
            prev_rot, prev_v0 = rot, part[0]
        return carry

    lax.fori_loop(0, D // LANE, strip, 0)

    c = acc_ref[...]
    mu = jnp.mean(c, axis=-1, keepdims=True)
    cc = c - mu
    var = jnp.mean(cc * cc, axis=-1, keepdims=True)
    z = cc * lax.rsqrt(var + EPS) * lg_ref[...] + lb_ref[...]
    z = z * _sigmoid(z)
    out = jnp.dot(z.astype(BF16), wout_ref[...], preferred_element_type=F32)
    o_ref[...] = y_ref[...] + mod_ref[0][2:3] * out


def _conv_main(u, y, mods, mrow, w_dw, b_dw, ln_g, ln_b, w_out):
    nh = N_TOK // HALO
    per = SB // HALO
    sb_spec = pl.BlockSpec((SB, D), lambda j, *_: (j, 0))
    return pl.pallas_call(
        _conv_main_kernel,
        out_shape=jax.ShapeDtypeStruct((N_TOK, D), F32),
        grid_spec=pltpu.PrefetchScalarGridSpec(
            num_scalar_prefetch=3, grid=(NSB,),
            in_specs=[
                sb_spec,
                pl.BlockSpec((HALO, D), lambda j, *_: (jnp.maximum(j * per - 1, 0), 0)),
                pl.BlockSpec((HALO, D), lambda j, *_: (jnp.minimum((j + 1) * per, nh - 1), 0)),
                _full_spec((CONV_WIDTH + 1, D)), _full_spec((1, D)), _full_spec((1, D)), _full_spec((1, D)),
                _full_spec((D, D)), sb_spec, _mod_spec(),
            ],
            out_specs=sb_spec,
            scratch_shapes=[pltpu.VMEM((SB + 2 * HALO, D), F32), pltpu.VMEM((SB, D), F32)]),
        compiler_params=_cparams(),
        name="conv_main",
    )(mrow, jnp.asarray(_SEQ_FIRST), jnp.asarray(_SEQ_LAST), u, u, u, w_dw, b_dw, ln_g, ln_b, w_out, y, mods)


def _rope_angles():
    rows = DEC_SEQ // GRID_W
    row = jnp.repeat(jnp.arange(rows, dtype=F32), GRID_W)
    col = jnp.tile(jnp.arange(GRID_W, dtype=F32), rows)
    axis_dim = HEAD_DIM // 2
    freqs = jnp.power(ROPE_THETA, -jnp.arange(axis_dim // 2, dtype=F32) * 2.0 / axis_dim)
    ang_r = row[:, None] * freqs[None, :]
    ang_c = col[:, None] * freqs[None, :]
    return jnp.concatenate([ang_r, ang_r, ang_c, ang_c], axis=-1)


def _rope_blocks():
    ang = _rope_angles()
    cos, sin = jnp.cos(ang), jnp.sin(ang)
    lane = np.arange(HEAD_DIM)
    lo = jnp.asarray(((lane % (HEAD_DIM // 2)) < HEAD_DIM // 4).astype(np.float32))
    sin_a = -sin * lo[None, :]
    sin_b = sin * (1.0 - lo)[None, :]
    nblk = DEC_SEQ // SB
    ident = jnp.ones((1, SB, HEAD_DIM), F32)
    zeros = jnp.zeros((1, SB, HEAD_DIM), F32)
    cos_t = jnp.concatenate([ident, cos.reshape(nblk, SB, HEAD_DIM)], axis=0)
    sa_t = jnp.concatenate([zeros, sin_a.reshape(nblk, SB, HEAD_DIM)], axis=0)
    sb_t = jnp.concatenate([zeros, sin_b.reshape(nblk, SB, HEAD_DIM)], axis=0)
    return cos_t, sa_t, sb_t


def _attn_qkv_kernel(mr_ref, ri_ref, y_ref, mod_ref, g_ref, w_ref, qg_ref, kg_ref, cos_ref, sa_ref, sb_ref,
                     q_ref, kb_ref, vb_ref, kf_ref, vf_ref):
    h = _norm_mod(y_ref[...], g_ref[...], mod_ref[0], 0)
    qkv = jnp.dot(h.astype(BF16), w_ref[...], preferred_element_type=F32)
    cos, sa, sb = cos_ref[0], sa_ref[0], sb_ref[0]
    quarter = HEAD_DIM // 4

    def head(x, g):
        xn = _rms(x, g)
        return xn * cos + pltpu.roll(xn, HEAD_DIM - quarter, 1) * sa + pltpu.roll(xn, quarter, 1) * sb

    scale = HEAD_DIM ** -0.5
    for hd in range(N_HEADS):
        sl = slice(hd * HEAD_DIM, (hd + 1) * HEAD_DIM)
        q_ref[:, sl] = (head(qkv[:, sl], qg_ref[...]) * scale).astype(BF16)
    for kv in range(N_KV_HEADS):
        sl = slice(kv * HEAD_DIM, (kv + 1) * HEAD_DIM)
        kr = head(qkv[:, Q_DIM + kv * HEAD_DIM:Q_DIM + (kv + 1) * HEAD_DIM], kg_ref[...])
        kf_ref[:, sl] = kr
        kb_ref[:, sl] = kr.astype(BF16)
    v = qkv[:, Q_DIM + KV_DIM:]
    vf_ref[...] = v
    vb_ref[...] = v.astype(BF16)


def _attn_qkv(y, mods, mrow, g1, w_qkv, q_g, k_g, rope):
    cos_t, sa_t, sb_t = rope
    rspec = pl.BlockSpec((1, SB, HEAD_DIM), lambda j, mr, ri: (ri[j], 0, 0))
    return pl.pallas_call(
        _attn_qkv_kernel,
        out_shape=(jax.ShapeDtypeStruct((N_TOK, Q_DIM), BF16), jax.ShapeDtypeStruct((N_TOK, KV_DIM), BF16),
                   jax.ShapeDtypeStruct((N_TOK, KV_DIM), BF16), jax.ShapeDtypeStruct((N_TOK, KV_DIM), F32),
                   jax.ShapeDtypeStruct((N_TOK, KV_DIM), F32)),
        grid_spec=pltpu.PrefetchScalarGridSpec(
            num_scalar_prefetch=2, grid=(NSB,),
            in_specs=[_tok_spec(D, SB), _mod_spec(), _full_spec((1, D)), _full_spec((D, QKV_DIM)),
                      _full_spec((1, HEAD_DIM)), _full_spec((1, HEAD_DIM)), rspec, rspec, rspec],
            out_specs=(_tok_spec(Q_DIM, SB), _tok_spec(KV_DIM, SB), _tok_spec(KV_DIM, SB), _tok_spec(KV_DIM, SB),
                       _tok_spec(KV_DIM, SB))),
        compiler_params=_cparams(),
        name="attn_qkv",
    )(mrow, jnp.asarray(_ROPE_IDX_SB), y, mods, g1, w_qkv, q_g, k_g, cos_t, sa_t, sb_t)


def _attn_heads(q, ks, vs, o_scr):
    nt = (((1,), (1,)), ((), ()))
    for hd in range(N_HEADS):
        g = hd // GQA_GROUP
        qh = q[:, hd * HEAD_DIM:(hd + 1) * HEAD_DIM]
        gs = slice(g * HEAD_DIM, (g + 1) * HEAD_DIM)
        ss = [lax.dot_general(qh, k[:, gs], nt, preferred_element_type=F32) for k in ks]
        m = functools.reduce(jnp.maximum, [jnp.max(s, axis=-1, keepdims=True) for s in ss])
        ps = [jnp.exp(s - m) for s in ss]
        l = functools.reduce(lambda a, b: a + b, [jnp.sum(p, axis=-1, keepdims=True) for p in ps])
        o = functools.reduce(lambda a, b: a + b,
                             [jnp.dot(p.astype(BF16), v[:, gs], preferred_element_type=F32) for p, v in zip(ps, vs)])
        o_scr[:, hd * HEAD_DIM:(hd + 1) * HEAD_DIM] = (o / l).astype(BF16)


def _attn_ctx_kernel(q_ref, k_ref, v_ref, wo_ref, y_ref, mod_ref, o_ref, o_scr):
    _attn_heads(q_ref[...], [k_ref[...]], [v_ref[...]], o_scr)
    out = jnp.dot(o_scr[...], wo_ref[...], preferred_element_type=F32)
    o_ref[...] = y_ref[...] + mod_ref[0][2:3] * out


def _attn_lat_kernel(q_ref, k_ref, v_ref, ck_ref, cv_ref, wo_ref, y_ref, mod_ref, ctx_out_ref, o_ref, o_scr):
    del ctx_out_ref
    _attn_heads(q_ref[...], [k_ref[...], ck_ref[0].astype(BF16)], [v_ref[...], cv_ref[0].astype(BF16)], o_scr)
    out = jnp.dot(o_scr[...], wo_ref[...], preferred_element_type=F32)
    o_ref[...] = y_ref[...] + mod_ref[0][2:3] * out


def _attention(q, kb, vb, cache_k, cache_v, w_o, y, mods, layer):
    y_ctx = pl.pallas_call(
        _attn_ctx_kernel,
        out_shape=jax.ShapeDtypeStruct((N_TOK, D), F32),
        grid=(BATCH,),
        in_specs=[
            pl.BlockSpec((SEQ, Q_DIM), lambda s: (s, 0)),
            pl.BlockSpec((SEQ, KV_DIM), lambda s: (s, 0)),
            pl.BlockSpec((SEQ, KV_DIM), lambda s: (s, 0)),
            pl.BlockSpec((Q_DIM, D), lambda s: (0, 0)),
            pl.BlockSpec((SEQ, D), lambda s: (s, 0)),
            pl.BlockSpec((1, 6, D), lambda s: (layer * MOD_ROWS, 0, 0)),
        ],
        out_specs=pl.BlockSpec((SEQ, D), lambda s: (s, 0)),
        scratch_shapes=[pltpu.VMEM((SEQ, Q_DIM), BF16)],
        compiler_params=_cparams(),
        name="attn_ctx",
    )(q, kb, vb, w_o, y, mods)
    pb = NP_TOK // DEC_SEQ
    return pl.pallas_call(
        _attn_lat_kernel,
        out_shape=jax.ShapeDtypeStruct((N_TOK, D), F32),
        input_output_aliases={8: 0},
        grid=(DEC_BATCH, SB_PER_DEC),
        in_specs=[
            pl.BlockSpec((SB, Q_DIM), lambda b, t: (NSBP + b * SB_PER_DEC + t, 0)),
            pl.BlockSpec((DEC_SEQ, KV_DIM), lambda b, t: (pb + b, 0)),
            pl.BlockSpec((DEC_SEQ, KV_DIM), lambda b, t: (pb + b, 0)),
            pl.BlockSpec((1, PAST_LEN, KV_DIM), lambda b, t: (b, 0, 0)),
            pl.BlockSpec((1, PAST_LEN, KV_DIM), lambda b, t: (b, 0, 0)),
            pl.BlockSpec((Q_DIM, D), lambda b, t: (0, 0)),
            pl.BlockSpec((SB, D), lambda b, t: (NSBP + b * SB_PER_DEC + t, 0)),
            pl.BlockSpec((1, 6, D), lambda b, t: (layer * MOD_ROWS + 1 + b, 0, 0)),
            pl.BlockSpec(memory_space=pl.ANY),
        ],
        out_specs=pl.BlockSpec((SB, D), lambda b, t: (NSBP + b * SB_PER_DEC + t, 0)),
        scratch_shapes=[pltpu.VMEM((SB, Q_DIM), BF16)],
        compiler_params=_cparams(2),
        name="attn_lat",
    )(q, kb, vb, cache_k, cache_v, w_o, y, mods, y_ctx)


def _log_sigmoid(x):
    return jnp.minimum(x, 0.0) - jnp.log(1.0 + jnp.exp(-jnp.abs(x)))


def _mlstm_in_kernel(mr_ref, y_ref, mod_ref, g_ref, wf_ref, wg_ref, bg_ref, q_ref, k_ref, v_ref, o_ref, gt_ref,
                     w_ref):
    @pl.when(pl.program_id(0) == 0)
    def _():
        for c in range(4):
            w_ref[:, c * D:(c + 1) * D] = wf_ref[0, :, c * D:(c + 1) * D].astype(BF16)

    h = _norm_mod(y_ref[...], g_ref[...], mod_ref[0], 0)
    hb = h.astype(BF16)
    q_ref[...] = jnp.dot(hb, w_ref[:, 0:D], preferred_element_type=F32).astype(BF16)
    k_ref[...] = (jnp.dot(hb, w_ref[:, D:2 * D], preferred_element_type=F32) * (M_HEAD_DIM ** -0.5)).astype(BF16)
    v_ref[...] = jnp.dot(hb, w_ref[:, 2 * D:3 * D], preferred_element_type=F32).astype(BF16)
    o_ref[...] = _sigmoid(jnp.dot(hb, w_ref[:, 3 * D:4 * D], preferred_element_type=F32))
    gt = _dot_hi_lo(h, wg_ref) + bg_ref[...]
    lane = lax.broadcasted_iota(I32, gt.shape, 1)
    is_f = ((lane >= M_HEADS) & (lane < 2 * M_HEADS)) | ((lane >= 3 * M_HEADS) & (lane < 4 * M_HEADS))
    gt_ref[...] = jnp.where(is_f, _log_sigmoid(gt), gt)


def _mlstm_in(y, mods, mrow, g1, w_in_all, slot, w_gate, b_gate):
    w_spec = pl.BlockSpec((1,) + w_in_all.shape[1:], lambda j, *_: (slot, 0, 0), pipeline_mode=pl.Buffered(1))
    return pl.pallas_call(
        _mlstm_in_kernel,
        out_shape=(jax.ShapeDtypeStruct((N_TOK, D), BF16), jax.ShapeDtypeStruct((N_TOK, D), BF16),
                   jax.ShapeDtypeStruct((N_TOK, D), BF16), jax.ShapeDtypeStruct((N_TOK, D), F32),
                   jax.ShapeDtypeStruct((N_TOK, LANE), F32)),
        grid_spec=pltpu.PrefetchScalarGridSpec(
            num_scalar_prefetch=1, grid=(NB,),
            in_specs=[_tok_spec(D), _mod_spec(), _full_spec((1, D)), w_spec,
                      _full_spec((D, 2 * LANE)), _full_spec((1, LANE))],
            out_specs=(_tok_spec(D), _tok_spec(D), _tok_spec(D), _tok_spec(D), _tok_spec(LANE)),
            scratch_shapes=[pltpu.VMEM((D, 4 * D), BF16)]),
        compiler_params=_cparams(),
        name="mlstm_in",
    )(mrow, y, mods, g1, w_in_all, w_gate, b_gate)


def _mlstm_load(hd, c, q_ref, k_ref, v_ref, gc_ref, gr_ref):
    r0 = pl.multiple_of(c * M_CHUNK, M_CHUNK)
    hs = slice(hd * M_HEAD_DIM, (hd + 1) * M_HEAD_DIM)
    rows = pl.ds(r0, M_CHUNK)
    return rows, hs, q_ref[rows, hs], k_ref[rows, hs], v_ref[rows, hs], gc_ref[hd, rows, :], gr_ref[hd, c]


def _mlstm_chunks(chains, ms, loaded, c_scr, n_scr):
    L = M_CHUNK
    n = range(len(chains))
    t_idx = lax.broadcasted_iota(I32, (L, L), 0)
    s_idx = lax.broadcasted_iota(I32, (L, L), 1)
    masks = {0: (s_idx <= t_idx, t_idx <= s_idx), 1: (s_idx >= t_idx, t_idx >= s_idx)}
    q = [ld[2] for ld in loaded]
    k = [ld[3] for ld in loaded]
    v = [ld[4] for ld in loaded]
    i_col = [ld[5][:, 2 * d:2 * d + 1] for (_, d), ld in zip(chains, loaded)]
    lf_col = [ld[5][:, 2 * d + 1:2 * d + 2] for (_, d), ld in zip(chains, loaded)]
    i_row = [ld[6][2 * d:2 * d + 1, :] for (_, d), ld in zip(chains, loaded)]
    lf_row = [ld[6][2 * d + 1:2 * d + 2, :] for (_, d), ld in zip(chains, loaded)]
    mask = [masks[d][0] for _, d in chains]
    mask_t = [masks[d][1] for _, d in chains]
    b_col = [jnp.sum(jnp.where(mask[i], lf_row[i], 0.0), axis=1, keepdims=True) for i in n]
    b_row = [jnp.sum(jnp.where(mask_t[i], lf_col[i], 0.0), axis=0, keepdims=True) for i in n]
    log_d = [jnp.where(mask[i], b_col[i] - b_row[i] + i_row[i], -jnp.inf) for i in n]
    li = [b_col[i] + ms[i] for i in n]
    m_r = [jnp.maximum(li[i], jnp.max(log_d[i], axis=1, keepdims=True)) for i in n]
    a_int = [jnp.exp(li[i] - m_r[i]) for i in n]
    dmat = [jnp.exp(log_d[i] - m_r[i]) for i in n]
    cmat = [c_scr[d, hd] for hd, d in chains]
    nvec = [n_scr[d, hd] for hd, d in chains]
    gram = [lax.dot_general(q[i], k[i], (((1,), (1,)), ((), ())), preferred_element_type=F32) for i in n]
    inter = [jnp.dot(q[i], cmat[i].astype(BF16), preferred_element_type=F32) for i in n]
    s = [gram[i] * dmat[i] for i in n]
    intra = [jnp.dot(s[i].astype(BF16), v[i], preferred_element_type=F32) for i in n]
    qn = [jnp.sum(q[i].astype(F32) * nvec[i], axis=1, keepdims=True) for i in n]
    den = [a_int[i] * qn[i] + jnp.sum(s[i], axis=1, keepdims=True) for i in n]
    hh = [(a_int[i] * inter[i] + intra[i]) / jnp.maximum(jnp.abs(den[i]), jnp.exp(-m_r[i])) for i in n]
    b_last = [b_row[i][:, L - 1:L] if chains[i][1] == 0 else b_row[i][:, 0:1] for i in n]
    log_w = [b_last[i] - b_col[i] + i_col[i] for i in n]
    m_new = [jnp.maximum(b_last[i] + ms[i], jnp.max(log_w[i], axis=0, keepdims=True)) for i in n]
    w = [jnp.exp(log_w[i] - m_new[i]) for i in n]
    decay = [jnp.exp(b_last[i] + ms[i] - m_new[i]) for i in n]
    kw = [k[i].astype(F32) * w[i] for i in n]
    kv = [lax.dot_general(kw[i].astype(BF16), v[i], (((0,), (0,)), ((), ())), preferred_element_type=F32) for i in n]
    for i, (hd, d) in enumerate(chains):
        c_scr[d, hd] = decay[i] * cmat[i] + kv[i]
        n_scr[d, hd] = decay[i] * nvec[i] + jnp.sum(kw[i], axis=0, keepdims=True)
    return hh, m_new


def _mlstm_scan_body(n_chunks, q_ref, k_ref, v_ref, gc_ref, gr_ref, h_ref, hb_scr, c_scr, n_scr, m0):
    chains = [(hd, d) for hd in range(M_HEADS) for d in range(2)]

    def body(c, ms):
        loaded = [_mlstm_load(hd, c if d == 0 else n_chunks - 1 - c, q_ref, k_ref, v_ref, gc_ref, gr_ref)
                  for hd, d in chains]
        hh, m_new = _mlstm_chunks(chains, ms, loaded, c_scr, n_scr)
        for (hd, d), ld, h in zip(chains, loaded, hh):
            dst = h_ref if d == 0 else hb_scr
            dst[ld[0], ld[1]] = h
        return tuple(m_new)

    ms = lax.fori_loop(0, n_chunks, body, tuple(m0))
    h_ref[...] += hb_scr[...]
    return ms


def _mlstm_scan_ctx_kernel(q_ref, k_ref, v_ref, gc_ref, gr_ref, h_ref, cn_ref, nn_ref, mn_ref, hb_scr, c_scr, n_scr):
    c_scr[...] = jnp.zeros(c_scr.shape, F32)
    n_scr[...] = jnp.zeros(n_scr.shape, F32)
    zero = jnp.zeros((1, 1), F32)
    ms = _mlstm_scan_body(SEQ // M_CHUNK, q_ref, k_ref, v_ref, gc_ref, gr_ref, h_ref, hb_scr, c_scr, n_scr,
                          [zero] * (2 * M_HEADS))
    cn_ref[0] = c_scr[...]
    nn_ref[0] = n_scr[...]
    for hd in range(M_HEADS):
        for d in range(2):
            mn_ref[0, d, hd] = jnp.broadcast_to(ms[2 * hd + d], (1, LANE))


def _mlstm_scan_lat_kernel(q_ref, k_ref, v_ref, gc_ref, gr_ref, c0_ref, n0_ref, m0_ref, ctx_out_ref, h_ref,
                           hb_scr, c_scr, n_scr):
    del ctx_out_ref
    c_scr[...] = c0_ref[0]
    n_scr[...] = n0_ref[0]
    m0 = [m0_ref[0, d, hd] for hd in range(M_HEADS) for d in range(2)]
    _mlstm_scan_body(DEC_SEQ // M_CHUNK, q_ref, k_ref, v_ref, gc_ref, gr_ref, h_ref, hb_scr, c_scr, n_scr, m0)


def _mlstm_scan(q, k, v, gates, state_c, state_n, state_m):
    g16 = gates[:, :4 * M_HEADS].reshape(N_TOK, 4, M_HEADS)
    gcol = jnp.transpose(g16, (2, 0, 1))
    grow = jnp.transpose(g16.reshape(N_TOK // M_CHUNK, M_CHUNK, 4, M_HEADS), (3, 0, 2, 1))
    hd = M_HEAD_DIM
    state_scratch = [pltpu.VMEM((2, M_HEADS, hd, hd), F32), pltpu.VMEM((2, M_HEADS, 1, hd), F32)]
    ncp = SEQ // M_CHUNK
    h_ctx, new_c, new_n, new_m = pl.pallas_call(
        _mlstm_scan_ctx_kernel,
        out_shape=(jax.ShapeDtypeStruct((N_TOK, D), F32),
                   jax.ShapeDtypeStruct((BATCH, 2, M_HEADS, hd, hd), F32),
                   jax.ShapeDtypeStruct((BATCH, 2, M_HEADS, 1, hd), F32),
                   jax.ShapeDtypeStruct((BATCH, 2, M_HEADS, 1, LANE), F32)),
        grid=(BATCH,),
        in_specs=[
            pl.BlockSpec((SEQ, D), lambda s: (s, 0)),
            pl.BlockSpec((SEQ, D), lambda s: (s, 0)),
            pl.BlockSpec((SEQ, D), lambda s: (s, 0)),
            pl.BlockSpec((M_HEADS, SEQ, 4), lambda s: (0, s, 0)),
            pl.BlockSpec((M_HEADS, ncp, 4, M_CHUNK), lambda s: (0, s, 0, 0)),
        ],
        out_specs=(
            pl.BlockSpec((SEQ, D), lambda s: (s, 0)),
            pl.BlockSpec((1, 2, M_HEADS, hd, hd), lambda s: (s, 0, 0, 0, 0)),
            pl.BlockSpec((1, 2, M_HEADS, 1, hd), lambda s: (s, 0, 0, 0, 0)),
            pl.BlockSpec((1, 2, M_HEADS, 1, LANE), lambda s: (s, 0, 0, 0, 0)),
        ),
        scratch_shapes=[pltpu.VMEM((SEQ, D), F32)] + state_scratch,
        compiler_params=_cparams(),
        name="mlstm_scan_ctx",
    )(q, k, v, gcol, grow)
    ncl = DEC_SEQ // M_CHUNK
    pb = NP_TOK // DEC_SEQ
    h_all = pl.pallas_call(
        _mlstm_scan_lat_kernel,
        out_shape=jax.ShapeDtypeStruct((N_TOK, D), F32),
        input_output_aliases={8: 0},
        grid=(DEC_BATCH,),
        in_specs=[
            pl.BlockSpec((DEC_SEQ, D), lambda b: (pb + b, 0)),
            pl.BlockSpec((DEC_SEQ, D), lambda b: (pb + b, 0)),
            pl.BlockSpec((DEC_SEQ, D), lambda b: (pb + b, 0)),
            pl.BlockSpec((M_HEADS, DEC_SEQ, 4), lambda b: (0, pb + b, 0)),
            pl.BlockSpec((M_HEADS, ncl, 4, M_CHUNK), lambda b: (0, pb + b, 0, 0)),
            pl.BlockSpec((1, 2, M_HEADS, hd, hd), lambda b: (b, 0, 0, 0, 0)),
            pl.BlockSpec((1, 2, M_HEADS, 1, hd), lambda b: (b, 0, 0, 0, 0)),
            pl.BlockSpec((1, 2, M_HEADS, 1, 1), lambda b: (b, 0, 0, 0, 0)),
            pl.BlockSpec(memory_space=pl.ANY),
        ],
        out_specs=pl.BlockSpec((DEC_SEQ, D), lambda b: (pb + b, 0)),
        scratch_shapes=[pltpu.VMEM((DEC_SEQ, D), F32)] + state_scratch,
        compiler_params=_cparams(),
        name="mlstm_scan_lat",
    )(q, k, v, gcol, grow, state_c, state_n, state_m, h_ctx)
    return h_all, new_c, new_n, new_m


def _mlstm_out_kernel(mr_ref, h_ref, o_ref, ng_ref, w_ref, y_ref, mod_ref, out_ref, x_scr):
    hc = o_ref[...] * h_ref[...]
    for hd in range(M_HEADS):
        sl = slice(hd * M_HEAD_DIM, (hd + 1) * M_HEAD_DIM)
        x_scr[:, sl] = _rms(hc[:, sl], ng_ref[:, sl]).astype(BF16)
    out = jnp.dot(x_scr[...], w_ref[...], preferred_element_type=F32)
    out_ref[...] = y_ref[...] + mod_ref[0][2:3] * out


def _mlstm_out(hsum, o, norm_g, w_out, y, mods, mrow):
    return pl.pallas_call(
        _mlstm_out_kernel,
        out_shape=jax.ShapeDtypeStruct((N_TOK, D), F32),
        grid_spec=pltpu.PrefetchScalarGridSpec(
            num_scalar_prefetch=1, grid=(NB,),
            in_specs=[_tok_spec(D), _tok_spec(D), _full_spec((1, D)), _full_spec((D, D)), _tok_spec(D), _mod_spec()],
            out_specs=_tok_spec(D),
            scratch_shapes=[pltpu.VMEM((TM, D), BF16)]),
        compiler_params=_cparams(),
        name="mlstm_out",
    )(mrow, hsum, o, norm_g, w_out, y, mods)


ROUTE_OFF = N_GROUPS
SLAB = D // (2 * LANE)
V7X_SC_CORES = 2
V7X_SC_SUBCORES = 16
SC_WORKERS = V7X_SC_CORES * V7X_SC_SUBCORES
SC_WINDOW = 128
HI_MASK = -65536


def _bf16_bits(x):
    return lax.bitcast_convert_type(x.astype(BF16).astype(F32), I32)


def _store_slabs(ref, x):
    rows = x.shape[0]
    for c in range(SLAB):
        lo = lax.shift_right_logical(_bf16_bits(x[:, (2 * c) * LANE:(2 * c + 1) * LANE]), 16)
        hi = _bf16_bits(x[:, (2 * c + 1) * LANE:(2 * c + 2) * LANE]) & HI_MASK
        ref[pl.ds(c, rows, stride=SLAB), :] = lo | hi


def _load_slabs(ref, dst, rows, dtype):
    for c in range(SLAB):
        w = ref[pl.ds(c, rows, stride=SLAB), :]
        lo = lax.bitcast_convert_type(lax.shift_left(w, 16), F32)
        hi = lax.bitcast_convert_type(w & HI_MASK, F32)
        dst[:, (2 * c) * LANE:(2 * c + 1) * LANE] = lo.astype(dtype)
        dst[:, (2 * c + 1) * LANE:(2 * c + 2) * LANE] = hi.astype(dtype)


def _route_kernel(mr_ref, y_ref, mod_ref, g_ref, wr_ref, br_ref, tri_ref, x_ref, id_ref, wt_ref, rk_ref, cnt_ref,
                  cnt_scr):
    x = _norm_mod(y_ref[...], g_ref[...], mod_ref[0], 1)
    _store_slabs(x_ref, x)
    lg = _dot_hi_lo(x, wr_ref) + br_ref[...]
    lane = lax.broadcasted_iota(I32, lg.shape, 1).astype(F32)
    ninf = -jnp.inf
    big = float(LANE)
    lgg = jnp.where(lane < N_GROUPS, lg, ninf)
    gmax = jnp.max(lgg, axis=-1, keepdims=True)
    g_idx = jnp.min(jnp.where(lgg == gmax, lane, big), axis=-1, keepdims=True)
    g_w = 1.0 / jnp.sum(jnp.exp(lgg - gmax), axis=-1, keepdims=True)
    lo = ROUTE_OFF + g_idx * EXPERTS_PER_GROUP
    le = jnp.where((lane >= lo) & (lane < lo + EXPERTS_PER_GROUP), lg, ninf)
    m1 = jnp.max(le, axis=-1, keepdims=True)
    i1 = jnp.min(jnp.where(le == m1, lane, big), axis=-1, keepdims=True)
    le2 = jnp.where(lane == i1, ninf, le)
    m2 = jnp.max(le2, axis=-1, keepdims=True)
    i2 = jnp.min(jnp.where(le2 == m2, lane, big), axis=-1, keepdims=True)
    r = jnp.exp(m2 - m1)
    p1 = 1.0 / (1.0 + r)
    p2 = r / (1.0 + r)
    two = lax.broadcasted_iota(I32, (x.shape[0], TOP_K), 1)
    id_ref[...] = (jnp.where(two == 0, i1, i2) - ROUTE_OFF).astype(I32)
    wt_ref[...] = jnp.where(two == 0, g_w * p1, g_w * p2)
    @pl.when(pl.program_id(0) == 0)
    def _():
        cnt_scr[...] = jnp.zeros(cnt_scr.shape, F32)

    oh1 = (lane == i1).astype(F32)
    oh2 = (lane == i2).astype(F32)
    both = oh1 + oh2
    before = jnp.dot(tri_ref[...], both.astype(BF16), preferred_element_type=F32) + cnt_scr[...]
    rk1 = jnp.sum(oh1 * before, axis=-1, keepdims=True)
    rk2 = jnp.sum(oh2 * before, axis=-1, keepdims=True)
    rk_ref[...] = jnp.where(two == 0, rk1, rk2).astype(I32)
    cnt_scr[...] = cnt_scr[...] + jnp.sum(both, axis=0, keepdims=True)
    cnt_ref[...] = cnt_scr[...]


def _route(y, mods, mrow, g2, w_route, b_route):
    return pl.pallas_call(
        _route_kernel,
        out_shape=(jax.ShapeDtypeStruct((N_TOK * SLAB, LANE), I32), jax.ShapeDtypeStruct((N_TOK, TOP_K), I32),
                   jax.ShapeDtypeStruct((N_TOK, TOP_K), F32), jax.ShapeDtypeStruct((N_TOK, TOP_K), I32),
                   jax.ShapeDtypeStruct((1, LANE), F32)),
        grid_spec=pltpu.PrefetchScalarGridSpec(
            num_scalar_prefetch=1, grid=(NB,),
            in_specs=[_tok_spec(D), _mod_spec(), _full_spec((1, D)), _full_spec((D, 2 * LANE)), _full_spec((1, LANE)),
                      _full_spec((TM, TM))],
            out_specs=(pl.BlockSpec((TM * SLAB, LANE), lambda j, *_: (j, 0)), _tok_spec(TOP_K), _tok_spec(TOP_K),
                       _tok_spec(TOP_K), _full_spec((1, LANE))),
            scratch_shapes=[pltpu.VMEM((1, LANE), F32)]),
        compiler_params=_cparams(),
        name="moe_route",
    )(mrow, y, mods, g2, w_route, b_route, jnp.asarray(np.tril(np.ones((TM, TM), np.float32), -1), dtype=BF16))


def _slot_kernel(id_ref, rk_ref, ps_ref, o_ref):
    rows = id_ref.shape[0]
    lane = lax.broadcasted_iota(I32, (rows, LANE), 1)
    ids, rks = id_ref[...], rk_ref[...]
    acc = jnp.zeros((rows, LANE), F32)
    for k in range(TOP_K):
        start = jnp.sum(jnp.where(lane == ids[:, k:k + 1], ps_ref[...], 0.0), axis=-1, keepdims=True)
        acc = jnp.where(lane == k, start + rks[:, k:k + 1].astype(F32), acc)
    o_ref[...] = jnp.transpose(acc)[0:TOP_K, :].astype(I32)


def _slots(expert_id, rank, pad_start):
    table = jnp.concatenate([pad_start.astype(F32), jnp.zeros((LANE - N_EXPERTS,), F32)]).reshape(1, LANE)
    return pl.pallas_call(
        _slot_kernel,
        out_shape=jax.ShapeDtypeStruct((TOP_K, N_TOK), I32),
        grid=(NB,),
        in_specs=[pl.BlockSpec((TM, TOP_K), lambda j: (j, 0)), pl.BlockSpec((TM, TOP_K), lambda j: (j, 0)),
                  pl.BlockSpec((1, LANE), lambda j: (0, 0))],
        out_specs=pl.BlockSpec((TOP_K, TM), lambda j: (0, j)),
        compiler_params=_cparams(),
        name="moe_slots",
    )(expert_id, rank, table)


def _dispatch_tables(expert_id, rank, lane_counts):
    counts = lane_counts[0, ROUTE_OFF:ROUTE_OFF + N_EXPERTS].astype(I32)
    padded = ((counts + EBLK - 1) // EBLK) * EBLK
    pad_end = jnp.cumsum(padded)
    pad_start = pad_end - padded
    dest = _slots(expert_id, rank, pad_start)
    n_blk = (padded // EBLK).astype(I32)
    blk_start = (pad_start // EBLK).astype(I32)
    n_used = (pad_end[-1] // EBLK).astype(I32).reshape(1)
    return dest, blk_start, n_blk, n_used


def _sc_mesh():
    return plsc.VectorSubcoreMesh(core_axis_name="core", subcore_axis_name="subcore",
                                  num_cores=V7X_SC_CORES, num_subcores=V7X_SC_SUBCORES)


def _sc_worker():
    return lax.axis_index("core") * V7X_SC_SUBCORES + lax.axis_index("subcore")


def _sc_dispatch(x_slabs, d0, d1):
    per = N_TOK // SC_WORKERS

    @functools.partial(
        pl.kernel, out_type=jax.ShapeDtypeStruct((P_SLOTS, SLAB, LANE), I32), mesh=_sc_mesh(), name="moe_dispatch",
        scratch_types=[pltpu.VMEM((1, per), I32), pltpu.VMEM((1, per), I32), pltpu.VMEM((SC_WINDOW, SLAB, LANE), I32)])
    def run(x_hbm, d0_hbm, d1_hbm, o_hbm, i0_v, i1_v, buf):
        base = _sc_worker() * per
        pltpu.sync_copy(d0_hbm.at[:, pl.ds(base, per)], i0_v)
        pltpu.sync_copy(d1_hbm.at[:, pl.ds(base, per)], i1_v)

        @pl.loop(0, per // SC_WINDOW)
        def _(s):
            off = s * SC_WINDOW
            pltpu.sync_copy(x_hbm.at[pl.ds(base + off, SC_WINDOW)], buf)
            pltpu.sync_copy(buf, o_hbm.at[i0_v.at[0, pl.ds(off, SC_WINDOW)]])
            pltpu.sync_copy(buf, o_hbm.at[i1_v.at[0, pl.ds(off, SC_WINDOW)]])

    return run(x_slabs.reshape(N_TOK, SLAB, LANE), d0, d1)


def _sc_collect(y_slabs, dcat):
    per = N_ASSIGN // SC_WORKERS

    @functools.partial(
        pl.kernel, out_type=jax.ShapeDtypeStruct((N_ASSIGN, SLAB, LANE), I32), mesh=_sc_mesh(), name="moe_collect",
        scratch_types=[pltpu.VMEM((1, per), I32), pltpu.VMEM((SC_WINDOW, SLAB, LANE), I32)])
    def run(y_hbm, i_hbm, o_hbm, i_v, buf):
        base = _sc_worker() * per
        pltpu.sync_copy(i_hbm.at[:, pl.ds(base, per)], i_v)

        @pl.loop(0, per // SC_WINDOW)
        def _(s):
            off = s * SC_WINDOW
            pltpu.sync_copy(y_hbm.at[i_v.at[0, pl.ds(off, SC_WINDOW)]], buf)
            pltpu.sync_copy(buf, o_hbm.at[pl.ds(base + off, SC_WINDOW)])

    return run(y_slabs.reshape(P_SLOTS, SLAB, LANE), dcat)


EROWS = EBLK * SLAB


W_CHUNKS = 8
W_SLOTS = 2


def _expert_kernel(layer, bs_ref, nb_ref, nu_ref, wg_hbm, wu_hbm, wd_hbm, x_hbm, y_hbm,
                   xbuf, ybuf, xs, wg_f, wu_f, wd_f, wg_bf, wu_bf, wd_bf, isem, osem, wsem):
    e = pl.program_id(0)
    n_exp = pl.num_programs(0)
    n_used = nu_ref[0]
    b0 = bs_ref[e]
    nb = nb_ref[e]
    wslot = lax.rem(e, W_SLOTS)

    def weight_copies(ex, slot):
        out = []
        for hbm, buf in ((wg_hbm, wg_f), (wu_hbm, wu_f), (wd_hbm, wd_f)):
            rows = buf.shape[1] // W_CHUNKS
            for c in range(W_CHUNKS):
                rs = pl.ds(c * rows, rows)
                out.append(pltpu.make_async_copy(hbm.at[layer, ex, rs], buf.at[slot, rs], wsem.at[slot]))
        return out

    def start_weights(ex, slot):
        for i, cp in enumerate(weight_copies(ex, slot)):
            cp.start(priority=i % 2)

    ahead = W_SLOTS - 1

    @pl.when(e == 0)
    def _():
        for ex in range(ahead):
            start_weights(ex, ex)

    for cp in weight_copies(e, wslot):
        cp.wait()

    @pl.when(e + ahead < n_exp)
    def _():
        start_weights(e + ahead, lax.rem(e + ahead, W_SLOTS))

    def in_copy(g, slot):
        return pltpu.make_async_copy(x_hbm.at[pl.ds(pl.multiple_of(g * EROWS, EROWS), EROWS)], xbuf.at[slot],
                                     isem.at[slot])

    def out_copy(g, slot):
        return pltpu.make_async_copy(ybuf.at[slot], y_hbm.at[pl.ds(pl.multiple_of(g * EROWS, EROWS), EROWS)],
                                     osem.at[slot])

    @pl.when(e == 0)
    def _():
        in_copy(0, 0).start(priority=1)

    @pl.when(nb > 0)
    def _():
        wg_bf[...] = wg_f[wslot].astype(BF16)
        wu_bf[...] = wu_f[wslot].astype(BF16)
        wd_bf[...] = wd_f[wslot].astype(BF16)

    def block(k, carry):
        g = b0 + k
        slot = lax.rem(g, 2)
        in_copy(g, slot).wait()

        @pl.when(g + 1 < n_used)
        def _():
            in_copy(g + 1, 1 - slot).start(priority=1)

        _load_slabs(xbuf.at[slot], xs, EBLK, BF16)
        xb = xs[...]
        gt = jnp.dot(xb, wg_bf[...], preferred_element_type=F32)
        up = jnp.dot(xb, wu_bf[...], preferred_element_type=F32)
        hmid = (gt * _sigmoid(gt) * up).astype(BF16)
        res = jnp.dot(hmid, wd_bf[...], preferred_element_type=F32)

        @pl.when(g >= 2)
        def _():
            out_copy(g - 2, slot).wait()

        _store_slabs(ybuf.at[slot], res)
        out_copy(g, slot).start()
        return carry

    lax.fori_loop(0, nb, block, 0)

    @pl.when(e == n_exp - 1)
    def _():
        last = n_used - 1
        out_copy(last, lax.rem(last, 2)).wait()

        @pl.when(n_used >= 2)
        def _():
            out_copy(last - 1, lax.rem(last - 1, 2)).wait()


def _experts(x_sorted, blk_start, n_blk, n_used, w_gate, w_up, w_down, layer):
    any_spec = pl.BlockSpec(memory_space=pl.ANY)
    return pl.pallas_call(
        functools.partial(_expert_kernel, layer),
        out_shape=jax.ShapeDtypeStruct((P_SLOTS * SLAB, LANE), I32),
        grid_spec=pltpu.PrefetchScalarGridSpec(
            num_scalar_prefetch=3, grid=(N_EXPERTS,),
            in_specs=[any_spec, any_spec, any_spec, any_spec],
            out_specs=any_spec,
            scratch_shapes=[
                pltpu.VMEM((2, EROWS, LANE), I32), pltpu.VMEM((2, EROWS, LANE), I32),
                pltpu.VMEM((EBLK, D), BF16),
                pltpu.VMEM((W_SLOTS, D, D_EXPERT), F32), pltpu.VMEM((W_SLOTS, D, D_EXPERT), F32),
                pltpu.VMEM((W_SLOTS, D_EXPERT, D), F32),
                pltpu.VMEM((D, D_EXPERT), BF16), pltpu.VMEM((D, D_EXPERT), BF16), pltpu.VMEM((D_EXPERT, D), BF16),
                pltpu.SemaphoreType.DMA((2,)), pltpu.SemaphoreType.DMA((2,)), pltpu.SemaphoreType.DMA((W_SLOTS,)),
            ]),
        compiler_params=_cparams(),
        name="moe_experts",
    )(blk_start, n_blk, n_used, w_gate, w_up, w_down, x_sorted.reshape(P_SLOTS * SLAB, LANE))


def _combine_kernel(final, mr_ref, e0_ref, e1_ref, wt_ref, y_ref, mod_ref, fg_ref, o_ref, a_scr, b_scr):
    _load_slabs(e0_ref, a_scr, TM, F32)
    _load_slabs(e1_ref, b_scr, TM, F32)
    wt = wt_ref[...]
    moe = wt[:, 0:1] * a_scr[...] + wt[:, 1:2] * b_scr[...]
    y_new = y_ref[...] + mod_ref[0][5:6] * moe
    o_ref[...] = _rms(y_new, fg_ref[...]) if final else y_new


def _combine(ym, wts, y, mods, mrow, final_g, blk0, nblk, final):
    tok = lambda width: pl.BlockSpec((TM, width), lambda j, *_: (blk0 + j, 0))
    slab0 = pl.BlockSpec((TM * SLAB, LANE), lambda j, *_: (blk0 + j, 0))
    slab1 = pl.BlockSpec((TM * SLAB, LANE), lambda j, *_: (NB + blk0 + j, 0))
    mod = pl.BlockSpec((1, 6, D), lambda j, mr: (mr[blk0 + j], 0, 0))
    return pl.pallas_call(
        functools.partial(_combine_kernel, final),
        out_shape=jax.ShapeDtypeStruct((nblk * TM, D), F32),
        grid_spec=pltpu.PrefetchScalarGridSpec(
            num_scalar_prefetch=1, grid=(nblk,),
            in_specs=[slab0, slab1, tok(TOP_K), tok(D), mod, _full_spec((1, D))],
            out_specs=pl.BlockSpec((TM, D), lambda j, *_: (j, 0)),
            scratch_shapes=[pltpu.VMEM((TM, D), F32), pltpu.VMEM((TM, D), F32)]),
        compiler_params=_cparams(),
        name="moe_combine",
    )(mrow, ym, ym, wts, y, mods, final_g)


def kernel(x_prompt, x_sample, cache_attn_k, cache_attn_v, state_mlstm_C, state_mlstm_n, state_mlstm_m, c, c_ctx, ada_w, ada_b, norm1_g, norm2_g, conv_w_in, conv_w_dw, conv_b_dw, conv_ln_g, conv_ln_b, conv_w_out, attn_w_qkv, attn_q_norm, attn_k_norm, attn_w_o, mlstm_w_in, mlstm_b_gate, mlstm_norm_g, mlstm_w_out, moe_w_group, moe_b_group, moe_w_router, moe_b_router, moe_w_gate, moe_w_up, moe_w_down, final_norm_g):
    y = jnp.concatenate([x_prompt.reshape(NP_TOK, D), x_sample.reshape(NS_TOK, D)], axis=0)
    cvec = jnp.concatenate([c_ctx[None, :], c, jnp.zeros((MOD_ROWS - 1 - DEC_BATCH, D), F32)], axis=0)
    mods = _ada_all(cvec, ada_w, ada_b)
    rope = _rope_blocks()
    new_k = new_v = new_c = new_n = new_m = None
    for i in range(DEPTH):
        kind, slot = i % 3, i // 3
        mrow = jnp.asarray(_MOD_ROW + i * MOD_ROWS)
        mrow_sb = jnp.asarray(_MOD_ROW_SB + i * MOD_ROWS)
        g1 = norm1_g[i].reshape(1, D)
        if kind == 0:
            u = _conv_in(y, mods, mrow, g1, conv_w_in[slot].astype(BF16))
            w_dw = jnp.concatenate([conv_w_dw[slot], jnp.zeros((1, D), F32)], axis=0)
            y = _conv_main(u, y, mods, mrow_sb, w_dw, conv_b_dw[slot].reshape(1, D), conv_ln_g[slot].reshape(1, D),
                           conv_ln_b[slot].reshape(1, D), conv_w_out[slot].astype(BF16))
        elif kind == 1:
            q, kb, vb, kf, vf = _attn_qkv(y, mods, mrow_sb, g1, attn_w_qkv[slot].astype(BF16),
                                          attn_q_norm[slot].reshape(1, HEAD_DIM), attn_k_norm[slot].reshape(1, HEAD_DIM),
                                          rope)
            new_k = kf[:NP_TOK].reshape(BATCH, 1, SEQ, N_KV_HEADS, HEAD_DIM)
            new_v = vf[:NP_TOK].reshape(BATCH, 1, SEQ, N_KV_HEADS, HEAD_DIM)
            ck = cache_attn_k[:, slot].reshape(DEC_BATCH, PAST_LEN, KV_DIM)
            cv = cache_attn_v[:, slot].reshape(DEC_BATCH, PAST_LEN, KV_DIM)
            y = _attention(q, kb, vb, ck, cv, attn_w_o[slot].astype(BF16), y, mods, i)
        else:
            w_in = mlstm_w_in[slot]
            w_gate = jnp.concatenate([w_in[:, 4 * D:], jnp.zeros((D, LANE - 4 * M_HEADS), F32)], axis=1)
            b_gate = jnp.concatenate([mlstm_b_gate[slot], jnp.zeros((LANE - 4 * M_HEADS,), F32)]).reshape(1, LANE)
            q, k, v, o, gates = _mlstm_in(y, mods, mrow, g1, mlstm_w_in, slot, _split_hi_lo(w_gate), b_gate)
            sc = state_mlstm_C[:, slot]
            sn = state_mlstm_n[:, slot].reshape(DEC_BATCH, 2, M_HEADS, 1, M_HEAD_DIM)
            sm = state_mlstm_m[:, slot].reshape(DEC_BATCH, 2, M_HEADS, 1, 1)
            hsum, nc_, nn_, nm_ = _mlstm_scan(q, k, v, gates, sc, sn, sm)
            new_c = nc_[:, None]
            new_n = nn_.reshape(BATCH, 1, 2, M_HEADS, M_HEAD_DIM)
            new_m = nm_[..., 0, 0].reshape(BATCH, 1, 2, M_HEADS)
            y = _mlstm_out(hsum, o, mlstm_norm_g[slot].reshape(1, D), mlstm_w_out[slot].astype(BF16), y, mods, mrow)
        w_route = jnp.concatenate([moe_w_group[i], moe_w_router[i],
                                   jnp.zeros((D, LANE - N_GROUPS - N_EXPERTS), F32)], axis=1)
        b_route = jnp.concatenate([moe_b_group[i], moe_b_router[i],
                                   jnp.zeros((LANE - N_GROUPS - N_EXPERTS,), F32)]).reshape(1, LANE)
        x2, eid, ewt, rank, cnt = _route(y, mods, mrow, norm2_g[i].reshape(1, D), _split_hi_lo(w_route), b_route)
        dest, blk_start, n_blk, n_used = _dispatch_tables(eid, rank, cnt)
        x_sorted = _sc_dispatch(x2, dest[0:1], dest[1:2])
        y_sorted = _experts(x_sorted, blk_start, n_blk, n_used, moe_w_gate, moe_w_up, moe_w_down, i)
        ym = _sc_collect(y_sorted, dest.reshape(1, N_ASSIGN))
        ym = ym.reshape(N_ASSIGN * SLAB, LANE)
        fg = final_norm_g.reshape(1, D)
        if i + 1 < DEPTH:
            y = _combine(ym, ewt, y, mods, mrow, fg, 0, NB, False)
        else:
            y_prompt = _combine(ym, ewt, y, mods, mrow, fg, 0, NBP, True).reshape(BATCH, SEQ, D)
            y_sample = _combine(ym, ewt, y, mods, mrow, fg, NBP, NB - NBP, True).reshape(DEC_BATCH, DEC_SEQ, D)
    return (y_prompt, y_sample, new_k, new_v, new_c, new_n, new_m)
```

```python
import functools

import jax
import jax.numpy as jnp
import numpy as np
from jax import lax
from jax.experimental import pallas as pl
from jax.experimental.pallas import tpu as pltpu
from jax.experimental.pallas import tpu_sc as plsc

F32 = jnp.float32
BF16 = jnp.bfloat16
I32 = jnp.int32

D = 1024
BATCH, SEQ = 16, 256
DEC_BATCH, DEC_SEQ = 8, 1024
PAST_LEN = 256
DEPTH = 4
GRID_W = 64
EPS = 1e-6
CONV_WIDTH = 31
CONV_PAD = CONV_WIDTH // 2
HEAD_DIM = 128
N_HEADS = 8
N_KV_HEADS = 2
GQA_GROUP = N_HEADS // N_KV_HEADS
Q_DIM = N_HEADS * HEAD_DIM
KV_DIM = N_KV_HEADS * HEAD_DIM
QKV_DIM = Q_DIM + 2 * KV_DIM
ROPE_THETA = 10000.0
M_HEADS = 4
M_HEAD_DIM = D // M_HEADS
M_CHUNK = 64
N_GROUPS = 4
EXPERTS_PER_GROUP = 8
N_EXPERTS = N_GROUPS * EXPERTS_PER_GROUP
TOP_K = 2
D_EXPERT = 512

NP_TOK = BATCH * SEQ
NS_TOK = DEC_BATCH * DEC_SEQ
N_TOK = NP_TOK + NS_TOK
TM = 512
NB = N_TOK // TM
NBP = NP_TOK // TM
BLK_PER_DEC = DEC_SEQ // TM
SB = 256
NSB = N_TOK // SB
NSBP = NP_TOK // SB
SB_PER_DEC = DEC_SEQ // SB
MOD_ROWS = 16
HALO = 16
LANE = 128
SUBLANE = 8

N_ASSIGN = N_TOK * TOP_K
EBLK = 256
N_EBLK = N_ASSIGN // EBLK + N_EXPERTS
P_SLOTS = N_EBLK * EBLK
N_PAD_SLOTS = P_SLOTS - N_ASSIGN

VMEM_LIMIT = 56 * 1024 * 1024


def _block_tables(nb, nbp, per_dec):
    j = np.arange(nb)
    is_p = j < nbp
    mod_row = np.where(is_p, 0, 1 + (j - nbp) // per_dec)
    rope_idx = np.where(is_p, 0, 1 + (j - nbp) % per_dec)
    first = np.where(is_p, 1, ((j - nbp) % per_dec == 0).astype(np.int64))
    last = np.where(is_p, 1, ((j - nbp) % per_dec == per_dec - 1).astype(np.int64))
    return (mod_row.astype(np.int32), rope_idx.astype(np.int32), first.astype(np.int32), last.astype(np.int32))


_MOD_ROW, _, _, _ = _block_tables(NB, NBP, BLK_PER_DEC)
_MOD_ROW_SB, _ROPE_IDX_SB, _SEQ_FIRST, _SEQ_LAST = _block_tables(NSB, NSBP, SB_PER_DEC)


def _cparams(n_axes=1):
    return pltpu.CompilerParams(dimension_semantics=("arbitrary",) * n_axes, vmem_limit_bytes=VMEM_LIMIT)


def _sigmoid(x):
    return 1.0 / (1.0 + jnp.exp(-x))


def _rms(x, g):
    return x * lax.rsqrt(jnp.mean(x * x, axis=-1, keepdims=True) + EPS) * g


def _split_hi_lo(w):
    hi = w.astype(BF16)
    return jnp.concatenate([hi, (w - hi.astype(F32)).astype(BF16)], axis=1)


def _dot_hi_lo(x, w_ref):
    n = w_ref.shape[1] // 2
    xh = x.astype(BF16)
    xl = (x - xh.astype(F32)).astype(BF16)
    both = jnp.dot(xh, w_ref[...], preferred_element_type=F32)
    return both[:, :n] + (both[:, n:] + jnp.dot(xl, w_ref[:, :n], preferred_element_type=F32))


def _norm_mod(y, g, mod, which):
    shift = mod[3 * which:3 * which + 1]
    scale = mod[3 * which + 1:3 * which + 2]
    return _rms(y, g) * (1.0 + scale) + shift


def _ada_kernel(c_ref, w_ref, b_ref, o_ref):
    c = c_ref[...]
    s = c * _sigmoid(c)
    o_ref[0] = jnp.dot(s.astype(BF16), w_ref[0].astype(BF16), preferred_element_type=F32) + b_ref[0]


def _ada_all(cvec, ada_w, ada_b):
    tn = 1536
    out = pl.pallas_call(
        _ada_kernel,
        out_shape=jax.ShapeDtypeStruct((DEPTH, MOD_ROWS, 6 * D), F32),
        grid=(DEPTH, 6 * D // tn),
        in_specs=[
            pl.BlockSpec((MOD_ROWS, D), lambda l, n: (0, 0)),
            pl.BlockSpec((1, D, tn), lambda l, n: (l, 0, n)),
            pl.BlockSpec((1, 1, tn), lambda l, n: (l, 0, n)),
        ],
        out_specs=pl.BlockSpec((1, MOD_ROWS, tn), lambda l, n: (l, 0, n)),
        compiler_params=_cparams(2),
        name="ada_mod",
    )(cvec, ada_w, ada_b.reshape(DEPTH, 1, 6 * D))
    return out.reshape(DEPTH * MOD_ROWS, 6, D)


def _tok_spec(width, rows=TM):
    return pl.BlockSpec((rows, width), lambda j, *_: (j, 0))


def _mod_spec():
    return pl.BlockSpec((1, 6, D), lambda j, mr, *_: (mr[j], 0, 0))


def _full_spec(shape):
    nd = len(shape)
    return pl.BlockSpec(shape, lambda j, *_: (0,) * nd)


def _conv_in_kernel(mr_ref, y_ref, mod_ref, g_ref, w_ref, u_ref):
    h = _norm_mod(y_ref[...], g_ref[...], mod_ref[0], 0)
    ag = jnp.dot(h.astype(BF16), w_ref[...], preferred_element_type=F32)
    u_ref[...] = ag[:, :D] * _sigmoid(ag[:, D:])


def _conv_in(y, mods, mrow, g1, w_in):
    return pl.pallas_call(
        _conv_in_kernel,
        out_shape=jax.ShapeDtypeStruct((N_TOK, D), F32),
        grid_spec=pltpu.PrefetchScalarGridSpec(
            num_scalar_prefetch=1, grid=(NB,),
            in_specs=[_tok_spec(D), _mod_spec(), _full_spec((1, D)), _full_spec((D, 2 * D))],
            out_specs=_tok_spec(D)),
        compiler_params=_cparams(),
        name="conv_in",
    )(mrow, y, mods, g1, w_in)


def _conv_main_kernel(mr_ref, first_ref, last_ref, u_ref, up_ref, un_ref, wdw_ref, bdw_ref, lg_ref, lb_ref,
                      wout_ref, y_ref, mod_ref, o_ref, ext_ref, acc_ref):
    j = pl.program_id(0)
    zero = jnp.zeros((HALO, D), F32)
    ext_ref[0:HALO, :] = jnp.where(first_ref[j] == 1, zero, up_ref[...])
    ext_ref[HALO:HALO + SB, :] = u_ref[...]
    ext_ref[HALO + SB:2 * HALO + SB, :] = jnp.where(last_ref[j] == 1, zero, un_ref[...])

    off0 = HALO - CONV_PAD
    n_a = (off0 + CONV_WIDTH - 1) // SUBLANE + 1
    n_chunks = SB // SUBLANE

    def strip(ci, carry):
        cs = pl.ds(pl.multiple_of(ci * LANE, LANE), LANE)
        wk = [jnp.broadcast_to(wdw_ref[k:k + 1, cs], (SUBLANE, LANE)) for k in range(CONV_WIDTH)]
        bias = jnp.broadcast_to(bdw_ref[:, cs], (SUBLANE, LANE))
        sub = lax.broadcasted_iota(I32, (SUBLANE, LANE), 0)
        prev_rot, prev_v0 = None, None
        for j in range(n_chunks + 1):
            tiles = [ext_ref[SUBLANE * (j + a):SUBLANE * (j + a + 1), cs] for a in range(n_a)]
            part = []
            for s in range(SUBLANE):
                acc = None
                for a in range(n_a):
                    k = SUBLANE * a + s - off0
                    if (0 <= k < CONV_WIDTH) and not (s == 0 and j == n_chunks):
                        term = tiles[a] * wk[k]
                        acc = term if acc is None else acc + term
                part.append(acc)
            rot = [None] + [pltpu.roll(part[s], SUBLANE - s, 0) for s in range(1, SUBLANE)]
            if j >= 1:
                out = prev_v0 + bias
                for s in range(1, SUBLANE):
                    out = out + jnp.where(sub < SUBLANE - s, prev_rot[s], rot[s])
                acc_ref[SUBLANE * (j - 1):SUBLANE * j, cs] = out
            prev_rot, prev_v0 = rot, part[0]
        return carry

    lax.fori_loop(0, D // LANE, strip, 0)

    c = acc_ref[...]
    mu = jnp.mean(c, axis=-1, keepdims=True)
    cc = c - mu
    var = jnp.mean(cc * cc, axis=-1, keepdims=True)
    z = cc * lax.rsqrt(var + EPS) * lg_ref[...] + lb_ref[...]
    z = z * _sigmoid(z)
    out = jnp.dot(z.astype(BF16), wout_ref[...], preferred_element_type=F32)
    o_ref[...] = y_ref[...] + mod_ref[0][2:3] * out


def _conv_main(u, y, mods, mrow, w_dw, b_dw, ln_g, ln_b, w_out):
    nh = N_TOK // HALO
    per = SB // HALO
    sb_spec = pl.BlockSpec((SB, D), lambda j, *_: (j, 0))
    return pl.pallas_call(
        _conv_main_kernel,
        out_shape=jax.ShapeDtypeStruct((N_TOK, D), F32),
        grid_spec=pltpu.PrefetchScalarGridSpec(
            num_scalar_prefetch=3, grid=(NSB,),
            in_specs=[
                sb_spec,
                pl.BlockSpec((HALO, D), lambda j, *_: (jnp.maximum(j * per - 1, 0), 0)),
                pl.BlockSpec((HALO, D), lambda j, *_: (jnp.minimum((j + 1) * per, nh - 1), 0)),
                _full_spec((CONV_WIDTH + 1, D)), _full_spec((1, D)), _full_spec((1, D)), _full_spec((1, D)),
                _full_spec((D, D)), sb_spec, _mod_spec(),
            ],
            out_specs=sb_spec,
            scratch_shapes=[pltpu.VMEM((SB + 2 * HALO, D), F32), pltpu.VMEM((SB, D), F32)]),
        compiler_params=_cparams(),
        name="conv_main",
    )(mrow, jnp.asarray(_SEQ_FIRST), jnp.asarray(_SEQ_LAST), u, u, u, w_dw, b_dw, ln_g, ln_b, w_out, y, mods)


def _rope_angles():
    rows = DEC_SEQ // GRID_W
    row = jnp.repeat(jnp.arange(rows, dtype=F32), GRID_W)
    col = jnp.tile(jnp.arange(GRID_W, dtype=F32), rows)
    axis_dim = HEAD_DIM // 2
    freqs = jnp.power(ROPE_THETA, -jnp.arange(axis_dim // 2, dtype=F32) * 2.0 / axis_dim)
    ang_r = row[:, None] * freqs[None, :]
    ang_c = col[:, None] * freqs[None, :]
    return jnp.concatenate([ang_r, ang_r, ang_c, ang_c], axis=-1)


def _rope_blocks():
    ang = _rope_angles()
    cos, sin = jnp.cos(ang), jnp.sin(ang)
    lane = np.arange(HEAD_DIM)
    lo = jnp.asarray(((lane % (HEAD_DIM // 2)) < HEAD_DIM // 4).astype(np.float32))
    sin_a = -sin * lo[None, :]
    sin_b = sin * (1.0 - lo)[None, :]
    nblk = DEC_SEQ // SB
    ident = jnp.ones((1, SB, HEAD_DIM), F32)
    zeros = jnp.zeros((1, SB, HEAD_DIM), F32)
    cos_t = jnp.concatenate([ident, cos.reshape(nblk, SB, HEAD_DIM)], axis=0)
    sa_t = jnp.concatenate([zeros, sin_a.reshape(nblk, SB, HEAD_DIM)], axis=0)
    sb_t = jnp.concatenate([zeros, sin_b.reshape(nblk, SB, HEAD_DIM)], axis=0)
    return cos_t, sa_t, sb_t


def _attn_qkv_kernel(mr_ref, ri_ref, y_ref, mod_ref, g_ref, w_ref, qg_ref, kg_ref, cos_ref, sa_ref, sb_ref,
                     q_ref, kb_ref, vb_ref, kf_ref, vf_ref):
    h = _norm_mod(y_ref[...], g_ref[...], mod_ref[0], 0)
    qkv = jnp.dot(h.astype(BF16), w_ref[...], preferred_element_type=F32)
    cos, sa, sb = cos_ref[0], sa_ref[0], sb_ref[0]
    quarter = HEAD_DIM // 4

    def head(x, g):
        xn = _rms(x, g)
        return xn * cos + pltpu.roll(xn, HEAD_DIM - quarter, 1) * sa + pltpu.roll(xn, quarter, 1) * sb

    scale = HEAD_DIM ** -0.5
    for hd in range(N_HEADS):
        sl = slice(hd * HEAD_DIM, (hd + 1) * HEAD_DIM)
        q_ref[:, sl] = (head(qkv[:, sl], qg_ref[...]) * scale).astype(BF16)
    for kv in range(N_KV_HEADS):
        sl = slice(kv * HEAD_DIM, (kv + 1) * HEAD_DIM)
        kr = head(qkv[:, Q_DIM + kv * HEAD_DIM:Q_DIM + (kv + 1) * HEAD_DIM], kg_ref[...])
        kf_ref[:, sl] = kr
        kb_ref[:, sl] = kr.astype(BF16)
    v = qkv[:, Q_DIM + KV_DIM:]
    vf_ref[...] = v
    vb_ref[...] = v.astype(BF16)


def _attn_qkv(y, mods, mrow, g1, w_qkv, q_g, k_g, rope):
    cos_t, sa_t, sb_t = rope
    rspec = pl.BlockSpec((1, SB, HEAD_DIM), lambda j, mr, ri: (ri[j], 0, 0))
    return pl.pallas_call(
        _attn_qkv_kernel,
        out_shape=(jax.ShapeDtypeStruct((N_TOK, Q_DIM), BF16), jax.ShapeDtypeStruct((N_TOK, KV_DIM), BF16),
                   jax.ShapeDtypeStruct((N_TOK, KV_DIM), BF16), jax.ShapeDtypeStruct((N_TOK, KV_DIM), F32),
                   jax.ShapeDtypeStruct((N_TOK, KV_DIM), F32)),
        grid_spec=pltpu.PrefetchScalarGridSpec(
            num_scalar_prefetch=2, grid=(NSB,),
            in_specs=[_tok_spec(D, SB), _mod_spec(), _full_spec((1, D)), _full_spec((D, QKV_DIM)),
                      _full_spec((1, HEAD_DIM)), _full_spec((1, HEAD_DIM)), rspec, rspec, rspec],
            out_specs=(_tok_spec(Q_DIM, SB), _tok_spec(KV_DIM, SB), _tok_spec(KV_DIM, SB), _tok_spec(KV_DIM, SB),
                       _tok_spec(KV_DIM, SB))),
        compiler_params=_cparams(),
        name="attn_qkv",
    )(mrow, jnp.asarray(_ROPE_IDX_SB), y, mods, g1, w_qkv, q_g, k_g, cos_t, sa_t, sb_t)


def _attn_heads(q, ks, vs, o_scr):
    nt = (((1,), (1,)), ((), ()))
    for hd in range(N_HEADS):
        g = hd // GQA_GROUP
        qh = q[:, hd * HEAD_DIM:(hd + 1) * HEAD_DIM]
        gs = slice(g * HEAD_DIM, (g + 1) * HEAD_DIM)
        ss = [lax.dot_general(qh, k[:, gs], nt, preferred_element_type=F32) for k in ks]
        m = functools.reduce(jnp.maximum, [jnp.max(s, axis=-1, keepdims=True) for s in ss])
        ps = [jnp.exp(s - m) for s in ss]
        l = functools.reduce(lambda a, b: a + b, [jnp.sum(p, axis=-1, keepdims=True) for p in ps])
        o = functools.reduce(lambda a, b: a + b,
                             [jnp.dot(p.astype(BF16), v[:, gs], preferred_element_type=F32) for p, v in zip(ps, vs)])
        o_scr[:, hd * HEAD_DIM:(hd + 1) * HEAD_DIM] = (o / l).astype(BF16)


def _attn_ctx_kernel(q_ref, k_ref, v_ref, wo_ref, y_ref, mod_ref, o_ref, o_scr):
    _attn_heads(q_ref[...], [k_ref[...]], [v_ref[...]], o_scr)
    out = jnp.dot(o_scr[...], wo_ref[...], preferred_element_type=F32)
    o_ref[...] = y_ref[...] + mod_ref[0][2:3] * out


def _attn_lat_kernel(q_ref, k_ref, v_ref, ck_ref, cv_ref, wo_ref, y_ref, mod_ref, ctx_out_ref, o_ref, o_scr):
    del ctx_out_ref
    _attn_heads(q_ref[...], [k_ref[...], ck_ref[0].astype(BF16)], [v_ref[...], cv_ref[0].astype(BF16)], o_scr)
    out = jnp.dot(o_scr[...], wo_ref[...], preferred_element_type=F32)
    o_ref[...] = y_ref[...] + mod_ref[0][2:3] * out


def _attention(q, kb, vb, cache_k, cache_v, w_o, y, mods, layer):
    y_ctx = pl.pallas_call(
        _attn_ctx_kernel,
        out_shape=jax.ShapeDtypeStruct((N_TOK, D), F32),
        grid=(BATCH,),
        in_specs=[
            pl.BlockSpec((SEQ, Q_DIM), lambda s: (s, 0)),
            pl.BlockSpec((SEQ, KV_DIM), lambda s: (s, 0)),
            pl.BlockSpec((SEQ, KV_DIM), lambda s: (s, 0)),
            pl.BlockSpec((Q_DIM, D), lambda s: (0, 0)),
            pl.BlockSpec((SEQ, D), lambda s: (s, 0)),
            pl.BlockSpec((1, 6, D), lambda s: (layer * MOD_ROWS, 0, 0)),
        ],
        out_specs=pl.BlockSpec((SEQ, D), lambda s: (s, 0)),
        scratch_shapes=[pltpu.VMEM((SEQ, Q_DIM), BF16)],
        compiler_params=_cparams(),
        name="attn_ctx",
    )(q, kb, vb, w_o, y, mods)
    pb = NP_TOK // DEC_SEQ
    return pl.pallas_call(
        _attn_lat_kernel,
        out_shape=jax.ShapeDtypeStruct((N_TOK, D), F32),
        input_output_aliases={8: 0},
        grid=(DEC_BATCH, SB_PER_DEC),
        in_specs=[
            pl.BlockSpec((SB, Q_DIM), lambda b, t: (NSBP + b * SB_PER_DEC + t, 0)),
            pl.BlockSpec((DEC_SEQ, KV_DIM), lambda b, t: (pb + b, 0)),
            pl.BlockSpec((DEC_SEQ, KV_DIM), lambda b, t: (pb + b, 0)),
            pl.BlockSpec((1, PAST_LEN, KV_DIM), lambda b, t: (b, 0, 0)),
            pl.BlockSpec((1, PAST_LEN, KV_DIM), lambda b, t: (b, 0, 0)),
            pl.BlockSpec((Q_DIM, D), lambda b, t: (0, 0)),
            pl.BlockSpec((SB, D), lambda b, t: (NSBP + b * SB_PER_DEC + t, 0)),
            pl.BlockSpec((1, 6, D), lambda b, t: (layer * MOD_ROWS + 1 + b, 0, 0)),
            pl.BlockSpec(memory_space=pl.ANY),
        ],
        out_specs=pl.BlockSpec((SB, D), lambda b, t: (NSBP + b * SB_PER_DEC + t, 0)),
        scratch_shapes=[pltpu.VMEM((SB, Q_DIM), BF16)],
        compiler_params=_cparams(2),
        name="attn_lat",
    )(q, kb, vb, cache_k, cache_v, w_o, y, mods, y_ctx)


def _log_sigmoid(x):
    return jnp.minimum(x, 0.0) - jnp.log(1.0 + jnp.exp(-jnp.abs(x)))


def _mlstm_in_kernel(mr_ref, y_ref, mod_ref, g_ref, wf_ref, wg_ref, bg_ref, q_ref, k_ref, v_ref, o_ref, gt_ref,
                     w_ref):
    @pl.when(pl.program_id(0) == 0)
    def _():
        for c in range(4):
            w_ref[:, c * D:(c + 1) * D] = wf_ref[0, :, c * D:(c + 1) * D].astype(BF16)

    h = _norm_mod(y_ref[...], g_ref[...], mod_ref[0], 0)
    hb = h.astype(BF16)
    q_ref[...] = jnp.dot(hb, w_ref[:, 0:D], preferred_element_type=F32).astype(BF16)
    k_ref[...] = (jnp.dot(hb, w_ref[:, D:2 * D], preferred_element_type=F32) * (M_HEAD_DIM ** -0.5)).astype(BF16)
    v_ref[...] = jnp.dot(hb, w_ref[:, 2 * D:3 * D], preferred_element_type=F32).astype(BF16)
    o_ref[...] = _sigmoid(jnp.dot(hb, w_ref[:, 3 * D:4 * D], preferred_element_type=F32))
    gt = _dot_hi_lo(h, wg_ref) + bg_ref[...]
    lane = lax.broadcasted_iota(I32, gt.shape, 1)
    is_f = ((lane >= M_HEADS) & (lane < 2 * M_HEADS)) | ((lane >= 3 * M_HEADS) & (lane < 4 * M_HEADS))
    gt_ref[...] = jnp.where(is_f, _log_sigmoid(gt), gt)


def _mlstm_in(y, mods, mrow, g1, w_in_all, slot, w_gate, b_gate):
    w_spec = pl.BlockSpec((1,) + w_in_all.shape[1:], lambda j, *_: (slot, 0, 0), pipeline_mode=pl.Buffered(1))
    return pl.pallas_call(
        _mlstm_in_kernel,
        out_shape=(jax.ShapeDtypeStruct((N_TOK, D), BF16), jax.ShapeDtypeStruct((N_TOK, D), BF16),
                   jax.ShapeDtypeStruct((N_TOK, D), BF16), jax.ShapeDtypeStruct((N_TOK, D), F32),
                   jax.ShapeDtypeStruct((N_TOK, LANE), F32)),
        grid_spec=pltpu.PrefetchScalarGridSpec(
            num_scalar_prefetch=1, grid=(NB,),
            in_specs=[_tok_spec(D), _mod_spec(), _full_spec((1, D)), w_spec,
                      _full_spec((D, 2 * LANE)), _full_spec((1, LANE))],
            out_specs=(_tok_spec(D), _tok_spec(D), _tok_spec(D), _tok_spec(D), _tok_spec(LANE)),
            scratch_shapes=[pltpu.VMEM((D, 4 * D), BF16)]),
        compiler_params=_cparams(),
        name="mlstm_in",
    )(mrow, y, mods, g1, w_in_all, w_gate, b_gate)


def _mlstm_load(hd, c, q_ref, k_ref, v_ref, gc_ref, gr_ref):
    r0 = pl.multiple_of(c * M_CHUNK, M_CHUNK)
    hs = slice(hd * M_HEAD_DIM, (hd + 1) * M_HEAD_DIM)
    rows = pl.ds(r0, M_CHUNK)
    return rows, hs, q_ref[rows, hs], k_ref[rows, hs], v_ref[rows, hs], gc_ref[hd, rows, :], gr_ref[hd, c]


def _mlstm_chunks(chains, ms, loaded, c_scr, n_scr):
    L = M_CHUNK
    n = range(len(chains))
    t_idx = lax.broadcasted_iota(I32, (L, L), 0)
    s_idx = lax.broadcasted_iota(I32, (L, L), 1)
    masks = {0: (s_idx <= t_idx, t_idx <= s_idx), 1: (s_idx >= t_idx, t_idx >= s_idx)}
    q = [ld[2] for ld in loaded]
    k = [ld[3] for ld in loaded]
    v = [ld[4] for ld in loaded]
    i_col = [ld[5][:, 2 * d:2 * d + 1] for (_, d), ld in zip(chains, loaded)]
    lf_col = [ld[5][:, 2 * d + 1:2 * d + 2] for (_, d), ld in zip(chains, loaded)]
    i_row = [ld[6][2 * d:2 * d + 1, :] for (_, d), ld in zip(chains, loaded)]
    lf_row = [ld[6][2 * d + 1:2 * d + 2, :] for (_, d), ld in zip(chains, loaded)]
    mask = [masks[d][0] for _, d in chains]
    mask_t = [masks[d][1] for _, d in chains]
    b_col = [jnp.sum(jnp.where(mask[i], lf_row[i], 0.0), axis=1, keepdims=True) for i in n]
    b_row = [jnp.sum(jnp.where(mask_t[i], lf_col[i], 0.0), axis=0, keepdims=True) for i in n]
    log_d = [jnp.where(mask[i], b_col[i] - b_row[i] + i_row[i], -jnp.inf) for i in n]
    li = [b_col[i] + ms[i] for i in n]
    m_r = [jnp.maximum(li[i], jnp.max(log_d[i], axis=1, keepdims=True)) for i in n]
    a_int = [jnp.exp(li[i] - m_r[i]) for i in n]
    dmat = [jnp.exp(log_d[i] - m_r[i]) for i in n]
    cmat = [c_scr[d, hd] for hd, d in chains]
    nvec = [n_scr[d, hd] for hd, d in chains]
    gram = [lax.dot_general(q[i], k[i], (((1,), (1,)), ((), ())), preferred_element_type=F32) for i in n]
    inter = [jnp.dot(q[i], cmat[i].astype(BF16), preferred_element_type=F32) for i in n]
    s = [gram[i] * dmat[i] for i in n]
    intra = [jnp.dot(s[i].astype(BF16), v[i], preferred_element_type=F32) for i in n]
    qn = [jnp.sum(q[i].astype(F32) * nvec[i], axis=1, keepdims=True) for i in n]
    den = [a_int[i] * qn[i] + jnp.sum(s[i], axis=1, keepdims=True) for i in n]
    hh = [(a_int[i] * inter[i] + intra[i]) / jnp.maximum(jnp.abs(den[i]), jnp.exp(-m_r[i])) for i in n]
    b_last = [b_row[i][:, L - 1:L] if chains[i][1] == 0 else b_row[i][:, 0:1] for i in n]
    log_w = [b_last[i] - b_col[i] + i_col[i] for i in n]
    m_new = [jnp.maximum(b_last[i] + ms[i], jnp.max(log_w[i], axis=0, keepdims=True)) for i in n]
    w = [jnp.exp(log_w[i] - m_new[i]) for i in n]
    decay = [jnp.exp(b_last[i] + ms[i] - m_new[i]) for i in n]
    kw = [k[i].astype(F32) * w[i] for i in n]
    kv = [lax.dot_general(kw[i].astype(BF16), v[i], (((0,), (0,)), ((), ())), preferred_element_type=F32) for i in n]
    for i, (hd, d) in enumerate(chains):
        c_scr[d, hd] = decay[i] * cmat[i] + kv[i]
        n_scr[d, hd] = decay[i] * nvec[i] + jnp.sum(kw[i], axis=0, keepdims=True)
    return hh, m_new


def _mlstm_scan_body(n_chunks, q_ref, k_ref, v_ref, gc_ref, gr_ref, h_ref, hb_scr, c_scr, n_scr, m0):
    chains = [(hd, d) for hd in range(M_HEADS) for d in range(2)]

    def body(c, ms):
        loaded = [_mlstm_load(hd, c if d == 0 else n_chunks - 1 - c, q_ref, k_ref, v_ref, gc_ref, gr_ref)
                  for hd, d in chains]
        hh, m_new = _mlstm_chunks(chains, ms, loaded, c_scr, n_scr)
        for (hd, d), ld, h in zip(chains, loaded, hh):
            dst = h_ref if d == 0 else hb_scr
            dst[ld[0], ld[1]] = h
        return tuple(m_new)

    ms = lax.fori_loop(0, n_chunks, body, tuple(m0))
    h_ref[...] += hb_scr[...]
    return ms


def _mlstm_scan_ctx_kernel(q_ref, k_ref, v_ref, gc_ref, gr_ref, h_ref, cn_ref, nn_ref, mn_ref, hb_scr, c_scr, n_scr):
    c_scr[...] = jnp.zeros(c_scr.shape, F32)
    n_scr[...] = jnp.zeros(n_scr.shape, F32)
    zero = jnp.zeros((1, 1), F32)
    ms = _mlstm_scan_body(SEQ // M_CHUNK, q_ref, k_ref, v_ref, gc_ref, gr_ref, h_ref, hb_scr, c_scr, n_scr,
                          [zero] * (2 * M_HEADS))
    cn_ref[0] = c_scr[...]
    nn_ref[0] = n_scr[...]
    for hd in range(M_HEADS):
        for d in range(2):
            mn_ref[0, d, hd] = jnp.broadcast_to(ms[2 * hd + d], (1, LANE))


def _mlstm_scan_lat_kernel(q_ref, k_ref, v_ref, gc_ref, gr_ref, c0_ref, n0_ref, m0_ref, ctx_out_ref, h_ref,
                           hb_scr, c_scr, n_scr):
    del ctx_out_ref
    c_scr[...] = c0_ref[0]
    n_scr[...] = n0_ref[0]
    m0 = [m0_ref[0, d, hd] for hd in range(M_HEADS) for d in range(2)]
    _mlstm_scan_body(DEC_SEQ // M_CHUNK, q_ref, k_ref, v_ref, gc_ref, gr_ref, h_ref, hb_scr, c_scr, n_scr, m0)


def _mlstm_scan(q, k, v, gates, state_c, state_n, state_m):
    g16 = gates[:, :4 * M_HEADS].reshape(N_TOK, 4, M_HEADS)
    gcol = jnp.transpose(g16, (2, 0, 1))
    grow = jnp.transpose(g16.reshape(N_TOK // M_CHUNK, M_CHUNK, 4, M_HEADS), (3, 0, 2, 1))
    hd = M_HEAD_DIM
    state_scratch = [pltpu.VMEM((2, M_HEADS, hd, hd), F32), pltpu.VMEM((2, M_HEADS, 1, hd), F32)]
    ncp = SEQ // M_CHUNK
    h_ctx, new_c, new_n, new_m = pl.pallas_call(
        _mlstm_scan_ctx_kernel,
        out_shape=(jax.ShapeDtypeStruct((N_TOK, D), F32),
                   jax.ShapeDtypeStruct((BATCH, 2, M_HEADS, hd, hd), F32),
                   jax.ShapeDtypeStruct((BATCH, 2, M_HEADS, 1, hd), F32),
                   jax.ShapeDtypeStruct((BATCH, 2, M_HEADS, 1, LANE), F32)),
        grid=(BATCH,),
        in_specs=[
            pl.BlockSpec((SEQ, D), lambda s: (s, 0)),
            pl.BlockSpec((SEQ, D), lambda s: (s, 0)),
            pl.BlockSpec((SEQ, D), lambda s: (s, 0)),
            pl.BlockSpec((M_HEADS, SEQ, 4), lambda s: (0, s, 0)),
            pl.BlockSpec((M_HEADS, ncp, 4, M_CHUNK), lambda s: (0, s, 0, 0)),
        ],
        out_specs=(
            pl.BlockSpec((SEQ, D), lambda s: (s, 0)),
            pl.BlockSpec((1, 2, M_HEADS, hd, hd), lambda s: (s, 0, 0, 0, 0)),
            pl.BlockSpec((1, 2, M_HEADS, 1, hd), lambda s: (s, 0, 0, 0, 0)),
            pl.BlockSpec((1, 2, M_HEADS, 1, LANE), lambda s: (s, 0, 0, 0, 0)),
        ),
        scratch_shapes=[pltpu.VMEM((SEQ, D), F32)] + state_scratch,
        compiler_params=_cparams(),
        name="mlstm_scan_ctx",
    )(q, k, v, gcol, grow)
    ncl = DEC_SEQ // M_CHUNK
    pb = NP_TOK // DEC_SEQ
    h_all = pl.pallas_call(
        _mlstm_scan_lat_kernel,
        out_shape=jax.ShapeDtypeStruct((N_TOK, D), F32),
        input_output_aliases={8: 0},
        grid=(DEC_BATCH,),
        in_specs=[
            pl.BlockSpec((DEC_SEQ, D), lambda b: (pb + b, 0)),
            pl.BlockSpec((DEC_SEQ, D), lambda b: (pb + b, 0)),
            pl.BlockSpec((DEC_SEQ, D), lambda b: (pb + b, 0)),
            pl.BlockSpec((M_HEADS, DEC_SEQ, 4), lambda b: (0, pb + b, 0)),
            pl.BlockSpec((M_HEADS, ncl, 4, M_CHUNK), lambda b: (0, pb + b, 0, 0)),
            pl.BlockSpec((1, 2, M_HEADS, hd, hd), lambda b: (b, 0, 0, 0, 0)),
            pl.BlockSpec((1, 2, M_HEADS, 1, hd), lambda b: (b, 0, 0, 0, 0)),
            pl.BlockSpec((1, 2, M_HEADS, 1, 1), lambda b: (b, 0, 0, 0, 0)),
            pl.BlockSpec(memory_space=pl.ANY),
        ],
        out_specs=pl.BlockSpec((DEC_SEQ, D), lambda b: (pb + b, 0)),
        scratch_shapes=[pltpu.VMEM((DEC_SEQ, D), F32)] + state_scratch,
        compiler_params=_cparams(),
        name="mlstm_scan_lat",
    )(q, k, v, gcol, grow, state_c, state_n, state_m, h_ctx)
    return h_all, new_c, new_n, new_m


def _mlstm_out_kernel(mr_ref, h_ref, o_ref, ng_ref, w_ref, y_ref, mod_ref, out_ref, x_scr):
    hc = o_ref[...] * h_ref[...]
    for hd in range(M_HEADS):
        sl = slice(hd * M_HEAD_DIM, (hd + 1) * M_HEAD_DIM)
        x_scr[:, sl] = _rms(hc[:, sl], ng_ref[:, sl]).astype(BF16)
    out = jnp.dot(x_scr[...], w_ref[...], preferred_element_type=F32)
    out_ref[...] = y_ref[...] + mod_ref[0][2:3] * out


def _mlstm_out(hsum, o, norm_g, w_out, y, mods, mrow):
    return pl.pallas_call(
        _mlstm_out_kernel,
        out_shape=jax.ShapeDtypeStruct((N_TOK, D), F32),
        grid_spec=pltpu.PrefetchScalarGridSpec(
            num_scalar_prefetch=1, grid=(NB,),
            in_specs=[_tok_spec(D), _tok_spec(D), _full_spec((1, D)), _full_spec((D, D)), _tok_spec(D), _mod_spec()],
            out_specs=_tok_spec(D),
            scratch_shapes=[pltpu.VMEM((TM, D), BF16)]),
        compiler_params=_cparams(),
        name="mlstm_out",
    )(mrow, hsum, o, norm_g, w_out, y, mods)


ROUTE_OFF = N_GROUPS
SLAB = D // (2 * LANE)
V7X_SC_CORES = 2
V7X_SC_SUBCORES = 16
SC_WORKERS = V7X_SC_CORES * V7X_SC_SUBCORES
SC_WINDOW = 128
HI_MASK = -65536


def _bf16_bits(x):
    return lax.bitcast_convert_type(x.astype(BF16).astype(F32), I32)


def _store_slabs(ref, x):
    rows = x.shape[0]
    for c in range(SLAB):
        lo = lax.shift_right_logical(_bf16_bits(x[:, (2 * c) * LANE:(2 * c + 1) * LANE]), 16)
        hi = _bf16_bits(x[:, (2 * c + 1) * LANE:(2 * c + 2) * LANE]) & HI_MASK
        ref[pl.ds(c, rows, stride=SLAB), :] = lo | hi


def _load_slabs(ref, dst, rows, dtype):
    for c in range(SLAB):
        w = ref[pl.ds(c, rows, stride=SLAB), :]
        lo = lax.bitcast_convert_type(lax.shift_left(w, 16), F32)
        hi = lax.bitcast_convert_type(w & HI_MASK, F32)
        dst[:, (2 * c) * LANE:(2 * c + 1) * LANE] = lo.astype(dtype)
        dst[:, (2 * c + 1) * LANE:(2 * c + 2) * LANE] = hi.astype(dtype)


def _route_kernel(mr_ref, y_ref, mod_ref, g_ref, wr_ref, br_ref, tri_ref, x_ref, wt_ref, meta_ref, cnt_ref, cnt_scr):
    x = _norm_mod(y_ref[...], g_ref[...], mod_ref[0], 1)
    _store_slabs(x_ref, x)
    lg = _dot_hi_lo(x, wr_ref) + br_ref[...]
    lane = lax.broadcasted_iota(I32, lg.shape, 1).astype(F32)
    ninf = -jnp.inf
    big = float(LANE)
    lgg = jnp.where(lane < N_GROUPS, lg, ninf)
    gmax = jnp.max(lgg, axis=-1, keepdims=True)
    g_idx = jnp.min(jnp.where(lgg == gmax, lane, big), axis=-1, keepdims=True)
    g_w = 1.0 / jnp.sum(jnp.exp(lgg - gmax), axis=-1, keepdims=True)
    lo = ROUTE_OFF + g_idx * EXPERTS_PER_GROUP
    le = jnp.where((lane >= lo) & (lane < lo + EXPERTS_PER_GROUP), lg, ninf)
    m1 = jnp.max(le, axis=-1, keepdims=True)
    i1 = jnp.min(jnp.where(le == m1, lane, big), axis=-1, keepdims=True)
    le2 = jnp.where(lane == i1, ninf, le)
    m2 = jnp.max(le2, axis=-1, keepdims=True)
    i2 = jnp.min(jnp.where(le2 == m2, lane, big), axis=-1, keepdims=True)
    r = jnp.exp(m2 - m1)
    p1 = 1.0 / (1.0 + r)
    p2 = r / (1.0 + r)
    two = lax.broadcasted_iota(I32, (x.shape[0], TOP_K), 1)
    wt_ref[...] = jnp.where(two == 0, g_w * p1, g_w * p2)
    @pl.when(pl.program_id(0) == 0)
    def _():
        cnt_scr[...] = jnp.zeros(cnt_scr.shape, F32)

    oh1 = (lane == i1).astype(F32)
    oh2 = (lane == i2).astype(F32)
    both = oh1 + oh2
    before = jnp.dot(tri_ref[...], both.astype(BF16), preferred_element_type=F32) + cnt_scr[...]
    rk1 = jnp.sum(oh1 * before, axis=-1, keepdims=True)
    rk2 = jnp.sum(oh2 * before, axis=-1, keepdims=True)
    cnt_scr[...] = cnt_scr[...] + jnp.sum(both, axis=0, keepdims=True)
    cnt_ref[...] = cnt_scr[...]
    cols = (i1 - ROUTE_OFF, i2 - ROUTE_OFF, rk1, rk2)
    packed = jnp.zeros(lg.shape, F32)
    for c, val in enumerate(cols):
        packed = jnp.where(lane == c, val, packed)
    meta_ref[...] = jnp.transpose(packed)[0:len(cols), :].astype(I32)


def _route(y, mods, mrow, g2, w_route, b_route):
    return pl.pallas_call(
        _route_kernel,
        out_shape=(jax.ShapeDtypeStruct((N_TOK * SLAB, LANE), I32), jax.ShapeDtypeStruct((N_TOK, TOP_K), F32),
                   jax.ShapeDtypeStruct((2 * TOP_K, N_TOK), I32), jax.ShapeDtypeStruct((1, LANE), F32)),
        grid_spec=pltpu.PrefetchScalarGridSpec(
            num_scalar_prefetch=1, grid=(NB,),
            in_specs=[_tok_spec(D), _mod_spec(), _full_spec((1, D)), _full_spec((D, 2 * LANE)), _full_spec((1, LANE)),
                      _full_spec((TM, TM))],
            out_specs=(pl.BlockSpec((TM * SLAB, LANE), lambda j, *_: (j, 0)), _tok_spec(TOP_K),
                       pl.BlockSpec((2 * TOP_K, TM), lambda j, *_: (0, j)), _full_spec((1, LANE))),
            scratch_shapes=[pltpu.VMEM((1, LANE), F32)]),
        compiler_params=_cparams(),
        name="moe_route",
    )(mrow, y, mods, g2, w_route, b_route, jnp.asarray(np.tril(np.ones((TM, TM), np.float32), -1), dtype=BF16))


SLOT_COLS = 2048


def _slot_kernel(meta_ref, ps_ref, o_ref):
    sub = lax.broadcasted_iota(I32, (N_EXPERTS, SLOT_COLS), 0)
    meta = meta_ref[...]
    table = ps_ref[...]
    for k in range(TOP_K):
        start = jnp.sum(jnp.where(sub == meta[k:k + 1, :], table, 0), axis=0, keepdims=True)
        o_ref[k:k + 1, :] = start + meta[TOP_K + k:TOP_K + k + 1, :]


def _slots(meta, pad_start):
    return pl.pallas_call(
        _slot_kernel,
        out_shape=jax.ShapeDtypeStruct((TOP_K, N_TOK), I32),
        grid=(N_TOK // SLOT_COLS,),
        in_specs=[pl.BlockSpec((2 * TOP_K, SLOT_COLS), lambda j: (0, j)), pl.BlockSpec((N_EXPERTS, 1), lambda j: (0, 0))],
        out_specs=pl.BlockSpec((TOP_K, SLOT_COLS), lambda j: (0, j)),
        compiler_params=_cparams(),
        name="moe_slots",
    )(meta, pad_start.astype(I32).reshape(N_EXPERTS, 1))


def _dispatch_tables(meta, lane_counts):
    counts = lane_counts[0, ROUTE_OFF:ROUTE_OFF + N_EXPERTS].astype(I32)
    padded = ((counts + EBLK - 1) // EBLK) * EBLK
    pad_end = jnp.cumsum(padded)
    pad_start = pad_end - padded
    dest = _slots(meta, pad_start)
    n_blk = (padded // EBLK).astype(I32)
    blk_start = (pad_start // EBLK).astype(I32)
    n_used = (pad_end[-1] // EBLK).astype(I32).reshape(1)
    return dest, blk_start, n_blk, n_used


def _sc_mesh():
    return plsc.VectorSubcoreMesh(core_axis_name="core", subcore_axis_name="subcore",
                                  num_cores=V7X_SC_CORES, num_subcores=V7X_SC_SUBCORES)


def _sc_worker():
    return lax.axis_index("core") * V7X_SC_SUBCORES + lax.axis_index("subcore")


def _sc_dispatch(x_slabs, d0, d1):
    per = N_TOK // SC_WORKERS

    @functools.partial(
        pl.kernel, out_type=jax.ShapeDtypeStruct((P_SLOTS, SLAB, LANE), I32), mesh=_sc_mesh(), name="moe_dispatch",
        scratch_types=[pltpu.VMEM((1, per), I32), pltpu.VMEM((1, per), I32), pltpu.VMEM((SC_WINDOW, SLAB, LANE), I32)])
    def run(x_hbm, d0_hbm, d1_hbm, o_hbm, i0_v, i1_v, buf):
        base = _sc_worker() * per
        pltpu.sync_copy(d0_hbm.at[:, pl.ds(base, per)], i0_v)
        pltpu.sync_copy(d1_hbm.at[:, pl.ds(base, per)], i1_v)

        @pl.loop(0, per // SC_WINDOW)
        def _(s):
            off = s * SC_WINDOW
            pltpu.sync_copy(x_hbm.at[pl.ds(base + off, SC_WINDOW)], buf)
            pltpu.sync_copy(buf, o_hbm.at[i0_v.at[0, pl.ds(off, SC_WINDOW)]])
            pltpu.sync_copy(buf, o_hbm.at[i1_v.at[0, pl.ds(off, SC_WINDOW)]])

    return run(x_slabs.reshape(N_TOK, SLAB, LANE), d0, d1)


def _sc_collect(y_slabs, dcat):
    per = N_ASSIGN // SC_WORKERS

    @functools.partial(
        pl.kernel, out_type=jax.ShapeDtypeStruct((N_ASSIGN, SLAB, LANE), I32), mesh=_sc_mesh(), name="moe_collect",
        scratch_types=[pltpu.VMEM((1, per), I32), pltpu.VMEM((SC_WINDOW, SLAB, LANE), I32)])
    def run(y_hbm, i_hbm, o_hbm, i_v, buf):
        base = _sc_worker() * per
        pltpu.sync_copy(i_hbm.at[:, pl.ds(base, per)], i_v)

        @pl.loop(0, per // SC_WINDOW)
        def _(s):
            off = s * SC_WINDOW
            pltpu.sync_copy(y_hbm.at[i_v.at[0, pl.ds(off, SC_WINDOW)]], buf)
            pltpu.sync_copy(buf, o_hbm.at[pl.ds(base + off, SC_WINDOW)])

    return run(y_slabs.reshape(P_SLOTS, SLAB, LANE), dcat)


EROWS = EBLK * SLAB


W_CHUNKS = 8
W_SLOTS = 2


def _expert_kernel(layer, bs_ref, nb_ref, nu_ref, wg_hbm, wu_hbm, wd_hbm, x_hbm, y_hbm,
                   xbuf, ybuf, xs, wg_f, wu_f, wd_f, wg_bf, wu_bf, wd_bf, isem, osem, wsem):
    e = pl.program_id(0)
    n_exp = pl.num_programs(0)
    n_used = nu_ref[0]
    b0 = bs_ref[e]
    nb = nb_ref[e]
    wslot = lax.rem(e, W_SLOTS)

    def weight_copies(ex, slot):
        out = []
        for hbm, buf in ((wg_hbm, wg_f), (wu_hbm, wu_f), (wd_hbm, wd_f)):
            rows = buf.shape[1] // W_CHUNKS
            for c in range(W_CHUNKS):
                rs = pl.ds(c * rows, rows)
                out.append(pltpu.make_async_copy(hbm.at[layer, ex, rs], buf.at[slot, rs], wsem.at[slot]))
        return out

    def start_weights(ex, slot):
        for i, cp in enumerate(weight_copies(ex, slot)):
            cp.start(priority=i % 2)

    ahead = W_SLOTS - 1

    @pl.when(e == 0)
    def _():
        for ex in range(ahead):
            start_weights(ex, ex)

    for cp in weight_copies(e, wslot):
        cp.wait()

    @pl.when(e + ahead < n_exp)
    def _():
        start_weights(e + ahead, lax.rem(e + ahead, W_SLOTS))

    def in_copy(g, slot):
        return pltpu.make_async_copy(x_hbm.at[pl.ds(pl.multiple_of(g * EROWS, EROWS), EROWS)], xbuf.at[slot],
                                     isem.at[slot])

    def out_copy(g, slot):
        return pltpu.make_async_copy(ybuf.at[slot], y_hbm.at[pl.ds(pl.multiple_of(g * EROWS, EROWS), EROWS)],
                                     osem.at[slot])

    @pl.when(e == 0)
    def _():
        in_copy(0, 0).start(priority=1)

    @pl.when(nb > 0)
    def _():
        wg_bf[...] = wg_f[wslot].astype(BF16)
        wu_bf[...] = wu_f[wslot].astype(BF16)
        wd_bf[...] = wd_f[wslot].astype(BF16)

    def block(k, carry):
        g = b0 + k
        slot = lax.rem(g, 2)
        in_copy(g, slot).wait()

        @pl.when(g + 1 < n_used)
        def _():
            in_copy(g + 1, 1 - slot).start(priority=1)

        _load_slabs(xbuf.at[slot], xs, EBLK, BF16)
        xb = xs[...]
        gt = jnp.dot(xb, wg_bf[...], preferred_element_type=F32)
        up = jnp.dot(xb, wu_bf[...], preferred_element_type=F32)
        hmid = (gt * _sigmoid(gt) * up).astype(BF16)
        res = jnp.dot(hmid, wd_bf[...], preferred_element_type=F32)

        @pl.when(g >= 2)
        def _():
            out_copy(g - 2, slot).wait()

        _store_slabs(ybuf.at[slot], res)
        out_copy(g, slot).start()
        return carry

    lax.fori_loop(0, nb, block, 0)

    @pl.when(e == n_exp - 1)
    def _():
        last = n_used - 1
        out_copy(last, lax.rem(last, 2)).wait()

        @pl.when(n_used >= 2)
        def _():
            out_copy(last - 1, lax.rem(last - 1, 2)).wait()


def _experts(x_sorted, blk_start, n_blk, n_used, w_gate, w_up, w_down, layer):
    any_spec = pl.BlockSpec(memory_space=pl.ANY)
    return pl.pallas_call(
        functools.partial(_expert_kernel, layer),
        out_shape=jax.ShapeDtypeStruct((P_SLOTS * SLAB, LANE), I32),
        grid_spec=pltpu.PrefetchScalarGridSpec(
            num_scalar_prefetch=3, grid=(N_EXPERTS,),
            in_specs=[any_spec, any_spec, any_spec, any_spec],
            out_specs=any_spec,
            scratch_shapes=[
                pltpu.VMEM((2, EROWS, LANE), I32), pltpu.VMEM((2, EROWS, LANE), I32),
                pltpu.VMEM((EBLK, D), BF16),
                pltpu.VMEM((W_SLOTS, D, D_EXPERT), F32), pltpu.VMEM((W_SLOTS, D, D_EXPERT), F32),
                pltpu.VMEM((W_SLOTS, D_EXPERT, D), F32),
                pltpu.VMEM((D, D_EXPERT), BF16), pltpu.VMEM((D, D_EXPERT), BF16), pltpu.VMEM((D_EXPERT, D), BF16),
                pltpu.SemaphoreType.DMA((2,)), pltpu.SemaphoreType.DMA((2,)), pltpu.SemaphoreType.DMA((W_SLOTS,)),
            ]),
        compiler_params=_cparams(),
        name="moe_experts",
    )(blk_start, n_blk, n_used, w_gate, w_up, w_down, x_sorted.reshape(P_SLOTS * SLAB, LANE))


def _combine_kernel(final, mr_ref, e0_ref, e1_ref, wt_ref, y_ref, mod_ref, fg_ref, o_ref, a_scr, b_scr):
    _load_slabs(e0_ref, a_scr, TM, F32)
    _load_slabs(e1_ref, b_scr, TM, F32)
    wt = wt_ref[...]
    moe = wt[:, 0:1] * a_scr[...] + wt[:, 1:2] * b_scr[...]
    y_new = y_ref[...] + mod_ref[0][5:6] * moe
    o_ref[...] = _rms(y_new, fg_ref[...]) if final else y_new


def _combine(ym, wts, y, mods, mrow, final_g, blk0, nblk, final):
    tok = lambda width: pl.BlockSpec((TM, width), lambda j, *_: (blk0 + j, 0))
    slab0 = pl.BlockSpec((TM * SLAB, LANE), lambda j, *_: (blk0 + j, 0))
    slab1 = pl.BlockSpec((TM * SLAB, LANE), lambda j, *_: (NB + blk0 + j, 0))
    mod = pl.BlockSpec((1, 6, D), lambda j, mr: (mr[blk0 + j], 0, 0))
    return pl.pallas_call(
        functools.partial(_combine_kernel, final),
        out_shape=jax.ShapeDtypeStruct((nblk * TM, D), F32),
        grid_spec=pltpu.PrefetchScalarGridSpec(
            num_scalar_prefetch=1, grid=(nblk,),
            in_specs=[slab0, slab1, tok(TOP_K), tok(D), mod, _full_spec((1, D))],
            out_specs=pl.BlockSpec((TM, D), lambda j, *_: (j, 0)),
            scratch_shapes=[pltpu.VMEM((TM, D), F32), pltpu.VMEM((TM, D), F32)]),
        compiler_params=_cparams(),
        name="moe_combine",
    )(mrow, ym, ym, wts, y, mods, final_g)


def kernel(x_prompt, x_sample, cache_attn_k, cache_attn_v, state_mlstm_C, state_mlstm_n, state_mlstm_m, c, c_ctx, ada_w, ada_b, norm1_g, norm2_g, conv_w_in, conv_w_dw, conv_b_dw, conv_ln_g, conv_ln_b, conv_w_out, attn_w_qkv, attn_q_norm, attn_k_norm, attn_w_o, mlstm_w_in, mlstm_b_gate, mlstm_norm_g, mlstm_w_out, moe_w_group, moe_b_group, moe_w_router, moe_b_router, moe_w_gate, moe_w_up, moe_w_down, final_norm_g):
    y = jnp.concatenate([x_prompt.reshape(NP_TOK, D), x_sample.reshape(NS_TOK, D)], axis=0)
    cvec = jnp.concatenate([c_ctx[None, :], c, jnp.zeros((MOD_ROWS - 1 - DEC_BATCH, D), F32)], axis=0)
    mods = _ada_all(cvec, ada_w, ada_b)
    rope = _rope_blocks()
    new_k = new_v = new_c = new_n = new_m = None
    for i in range(DEPTH):
        kind, slot = i % 3, i // 3
        mrow = jnp.asarray(_MOD_ROW + i * MOD_ROWS)
        mrow_sb = jnp.asarray(_MOD_ROW_SB + i * MOD_ROWS)
        g1 = norm1_g[i].reshape(1, D)
        if kind == 0:
            u = _conv_in(y, mods, mrow, g1, conv_w_in[slot].astype(BF16))
            w_dw = jnp.concatenate([conv_w_dw[slot], jnp.zeros((1, D), F32)], axis=0)
            y = _conv_main(u, y, mods, mrow_sb, w_dw, conv_b_dw[slot].reshape(1, D), conv_ln_g[slot].reshape(1, D),
                           conv_ln_b[slot].reshape(1, D), conv_w_out[slot].astype(BF16))
        elif kind == 1:
            q, kb, vb, kf, vf = _attn_qkv(y, mods, mrow_sb, g1, attn_w_qkv[slot].astype(BF16),
                                          attn_q_norm[slot].reshape(1, HEAD_DIM), attn_k_norm[slot].reshape(1, HEAD_DIM),
                                          rope)
            new_k = kf[:NP_TOK].reshape(BATCH, 1, SEQ, N_KV_HEADS, HEAD_DIM)
            new_v = vf[:NP_TOK].reshape(BATCH, 1, SEQ, N_KV_HEADS, HEAD_DIM)
            ck = cache_attn_k[:, slot].reshape(DEC_BATCH, PAST_LEN, KV_DIM)
            cv = cache_attn_v[:, slot].reshape(DEC_BATCH, PAST_LEN, KV_DIM)
            y = _attention(q, kb, vb, ck, cv, attn_w_o[slot].astype(BF16), y, mods, i)
        else:
            w_in = mlstm_w_in[slot]
            w_gate = jnp.concatenate([w_in[:, 4 * D:], jnp.zeros((D, LANE - 4 * M_HEADS), F32)], axis=1)
            b_gate = jnp.concatenate([mlstm_b_gate[slot], jnp.zeros((LANE - 4 * M_HEADS,), F32)]).reshape(1, LANE)
            q, k, v, o, gates = _mlstm_in(y, mods, mrow, g1, mlstm_w_in, slot, _split_hi_lo(w_gate), b_gate)
            sc = state_mlstm_C[:, slot]
            sn = state_mlstm_n[:, slot].reshape(DEC_BATCH, 2, M_HEADS, 1, M_HEAD_DIM)
            sm = state_mlstm_m[:, slot].reshape(DEC_BATCH, 2, M_HEADS, 1, 1)
            hsum, nc_, nn_, nm_ = _mlstm_scan(q, k, v, gates, sc, sn, sm)
            new_c = nc_[:, None]
            new_n = nn_.reshape(BATCH, 1, 2, M_HEADS, M_HEAD_DIM)
            new_m = nm_[..., 0, 0].reshape(BATCH, 1, 2, M_HEADS)
            y = _mlstm_out(hsum, o, mlstm_norm_g[slot].reshape(1, D), mlstm_w_out[slot].astype(BF16), y, mods, mrow)
        w_route = jnp.concatenate([moe_w_group[i], moe_w_router[i],
                                   jnp.zeros((D, LANE - N_GROUPS - N_EXPERTS), F32)], axis=1)
        b_route = jnp.concatenate([moe_b_group[i], moe_b_router[i],
                                   jnp.zeros((LANE - N_GROUPS - N_EXPERTS,), F32)]).reshape(1, LANE)
        x2, ewt, meta, cnt = _route(y, mods, mrow, norm2_g[i].reshape(1, D), _split_hi_lo(w_route), b_route)
        dest, blk_start, n_blk, n_used = _dispatch_tables(meta, cnt)
        x_sorted = _sc_dispatch(x2, dest[0:1], dest[1:2])
        y_sorted = _experts(x_sorted, blk_start, n_blk, n_used, moe_w_gate, moe_w_up, moe_w_down, i)
        ym = _sc_collect(y_sorted, dest.reshape(1, N_ASSIGN))
        ym = ym.reshape(N_ASSIGN * SLAB, LANE)
        fg = final_norm_g.reshape(1, D)
        if i + 1 < DEPTH:
            y = _combine(ym, ewt, y, mods, mrow, fg, 0, NB, False)
        else:
            y_prompt = _combine(ym, ewt, y, mods, mrow, fg, 0, NBP, True).reshape(BATCH, SEQ, D)
            y_sample = _combine(ym, ewt, y, mods, mrow, fg, NBP, NB - NBP, True).reshape(DEC_BATCH, DEC_SEQ, D)
    return (y_prompt, y_sample, new_k, new_v, new_c, new_n, new_m)
```

```python
import functools

import jax
import jax.numpy as jnp
import numpy as np
from jax import lax
from jax.experimental import pallas as pl
from jax.experimental.pallas import tpu as pltpu
from jax.experimental.pallas import tpu_sc as plsc

F32 = jnp.float32
BF16 = jnp.bfloat16
I32 = jnp.int32

D = 1024
BATCH, SEQ = 16, 256
DEC_BATCH, DEC_SEQ = 8, 1024
PAST_LEN = 256
DEPTH = 4
GRID_W = 64
EPS = 1e-6
CONV_WIDTH = 31
CONV_PAD = CONV_WIDTH // 2
HEAD_DIM = 128
N_HEADS = 8
N_KV_HEADS = 2
GQA_GROUP = N_HEADS // N_KV_HEADS
Q_DIM = N_HEADS * HEAD_DIM
KV_DIM = N_KV_HEADS * HEAD_DIM
QKV_DIM = Q_DIM + 2 * KV_DIM
ROPE_THETA = 10000.0
M_HEADS = 4
M_HEAD_DIM = D // M_HEADS
M_CHUNK = 64
N_GROUPS = 4
EXPERTS_PER_GROUP = 8
N_EXPERTS = N_GROUPS * EXPERTS_PER_GROUP
TOP_K = 2
D_EXPERT = 512

NP_TOK = BATCH * SEQ
NS_TOK = DEC_BATCH * DEC_SEQ
N_TOK = NP_TOK + NS_TOK
TM = 512
NB = N_TOK // TM
NBP = NP_TOK // TM
BLK_PER_DEC = DEC_SEQ // TM
SB = 256
NSB = N_TOK // SB
NSBP = NP_TOK // SB
SB_PER_DEC = DEC_SEQ // SB
MOD_ROWS = 16
HALO = 16
LANE = 128
SUBLANE = 8

N_ASSIGN = N_TOK * TOP_K
EBLK = 256
N_EBLK = N_ASSIGN // EBLK + N_EXPERTS
P_SLOTS = N_EBLK * EBLK
N_PAD_SLOTS = P_SLOTS - N_ASSIGN

VMEM_LIMIT = 56 * 1024 * 1024


def _block_tables(nb, nbp, per_dec):
    j = np.arange(nb)
    is_p = j < nbp
    mod_row = np.where(is_p, 0, 1 + (j - nbp) // per_dec)
    rope_idx = np.where(is_p, 0, 1 + (j - nbp) % per_dec)
    first = np.where(is_p, 1, ((j - nbp) % per_dec == 0).astype(np.int64))
    last = np.where(is_p, 1, ((j - nbp) % per_dec == per_dec - 1).astype(np.int64))
    return (mod_row.astype(np.int32), rope_idx.astype(np.int32), first.astype(np.int32), last.astype(np.int32))


_MOD_ROW, _, _, _ = _block_tables(NB, NBP, BLK_PER_DEC)
_MOD_ROW_SB, _ROPE_IDX_SB, _SEQ_FIRST, _SEQ_LAST = _block_tables(NSB, NSBP, SB_PER_DEC)


def _cparams(n_axes=1):
    return pltpu.CompilerParams(dimension_semantics=("arbitrary",) * n_axes, vmem_limit_bytes=VMEM_LIMIT)


def _sigmoid(x):
    return 1.0 / (1.0 + jnp.exp(-x))


def _rms(x, g):
    return x * lax.rsqrt(jnp.mean(x * x, axis=-1, keepdims=True) + EPS) * g


def _split_hi_lo(w):
    hi = w.astype(BF16)
    return jnp.concatenate([hi, (w - hi.astype(F32)).astype(BF16)], axis=1)


def _dot_hi_lo(x, w_ref):
    n = w_ref.shape[1] // 2
    xh = x.astype(BF16)
    xl = (x - xh.astype(F32)).astype(BF16)
    both = jnp.dot(xh, w_ref[...], preferred_element_type=F32)
    return both[:, :n] + (both[:, n:] + jnp.dot(xl, w_ref[:, :n], preferred_element_type=F32))


def _norm_mod(y, g, mod, which):
    shift = mod[3 * which:3 * which + 1]
    scale = mod[3 * which + 1:3 * which + 2]
    return _rms(y, g) * (1.0 + scale) + shift


def _ada_kernel(c_ref, w_ref, b_ref, o_ref):
    c = c_ref[...]
    s = c * _sigmoid(c)
    o_ref[0] = jnp.dot(s.astype(BF16), w_ref[0].astype(BF16), preferred_element_type=F32) + b_ref[0]


def _ada_all(cvec, ada_w, ada_b):
    tn = 1536
    out = pl.pallas_call(
        _ada_kernel,
        out_shape=jax.ShapeDtypeStruct((DEPTH, MOD_ROWS, 6 * D), F32),
        grid=(DEPTH, 6 * D // tn),
        in_specs=[
            pl.BlockSpec((MOD_ROWS, D), lambda l, n: (0, 0)),
            pl.BlockSpec((1, D, tn), lambda l, n: (l, 0, n)),
            pl.BlockSpec((1, 1, tn), lambda l, n: (l, 0, n)),
        ],
        out_specs=pl.BlockSpec((1, MOD_ROWS, tn), lambda l, n: (l, 0, n)),
        compiler_params=_cparams(2),
        name="ada_mod",
    )(cvec, ada_w, ada_b.reshape(DEPTH, 1, 6 * D))
    return out.reshape(DEPTH * MOD_ROWS, 6, D)


def _tok_spec(width, rows=TM):
    return pl.BlockSpec((rows, width), lambda j, *_: (j, 0))


def _mod_spec():
    return pl.BlockSpec((1, 6, D), lambda j, mr, *_: (mr[j], 0, 0))


def _full_spec(shape):
    nd = len(shape)
    return pl.BlockSpec(shape, lambda j, *_: (0,) * nd)


def _conv_in_kernel(mr_ref, y_ref, mod_ref, g_ref, w_ref, u_ref):
    h = _norm_mod(y_ref[...], g_ref[...], mod_ref[0], 0)
    ag = jnp.dot(h.astype(BF16), w_ref[...], preferred_element_type=F32)
    u_ref[...] = (ag[:, :D] * _sigmoid(ag[:, D:])).astype(BF16)


def _conv_in(y, mods, mrow, g1, w_in):
    return pl.pallas_call(
        _conv_in_kernel,
        out_shape=jax.ShapeDtypeStruct((N_TOK, D), BF16),
        grid_spec=pltpu.PrefetchScalarGridSpec(
            num_scalar_prefetch=1, grid=(NB,),
            in_specs=[_tok_spec(D), _mod_spec(), _full_spec((1, D)), _full_spec((D, 2 * D))],
            out_specs=_tok_spec(D)),
        compiler_params=_cparams(),
        name="conv_in",
    )(mrow, y, mods, g1, w_in)


def _conv_main_kernel(mr_ref, first_ref, last_ref, u_ref, up_ref, un_ref, wdw_ref, bdw_ref, lg_ref, lb_ref,
                      wout_ref, y_ref, mod_ref, o_ref, ext_ref, acc_ref):
    j = pl.program_id(0)
    zero = jnp.zeros((HALO, D), F32)
    ext_ref[0:HALO, :] = jnp.where(first_ref[j] == 1, zero, up_ref[...].astype(F32))
    ext_ref[HALO:HALO + SB, :] = u_ref[...].astype(F32)
    ext_ref[HALO + SB:2 * HALO + SB, :] = jnp.where(last_ref[j] == 1, zero, un_ref[...].astype(F32))

    off0 = HALO - CONV_PAD
    n_a = (off0 + CONV_WIDTH - 1) // SUBLANE + 1
    n_chunks = SB // SUBLANE

    def strip(ci, carry):
        cs = pl.ds(pl.multiple_of(ci * LANE, LANE), LANE)
        wk = [jnp.broadcast_to(wdw_ref[k:k + 1, cs], (SUBLANE, LANE)) for k in range(CONV_WIDTH)]
        bias = jnp.broadcast_to(bdw_ref[:, cs], (SUBLANE, LANE))
        sub = lax.broadcasted_iota(I32, (SUBLANE, LANE), 0)
        prev_rot, prev_v0 = None, None
        for j in range(n_chunks + 1):
            tiles = [ext_ref[SUBLANE * (j + a):SUBLANE * (j + a + 1), cs] for a in range(n_a)]
            part = []
            for s in range(SUBLANE):
                acc = None
                for a in range(n_a):
                    k = SUBLANE * a + s - off0
                    if (0 <= k < CONV_WIDTH) and not (s == 0 and j == n_chunks):
                        term = tiles[a] * wk[k]
                        acc = term if acc is None else acc + term
                part.append(acc)
            rot = [None] + [pltpu.roll(part[s], SUBLANE - s, 0) for s in range(1, SUBLANE)]
            if j >= 1:
                out = prev_v0 + bias
                for s in range(1, SUBLANE):
                    out = out + jnp.where(sub < SUBLANE - s, prev_rot[s], rot[s])
                acc_ref[SUBLANE * (j - 1):SUBLANE * j, cs] = out
            prev_rot, prev_v0 = rot, part[0]
        return carry

    lax.fori_loop(0, D // LANE, strip, 0)

    c = acc_ref[...]
    mu = jnp.mean(c, axis=-1, keepdims=True)
    cc = c - mu
    var = jnp.mean(cc * cc, axis=-1, keepdims=True)
    z = cc * lax.rsqrt(var + EPS) * lg_ref[...] + lb_ref[...]
    z = z * _sigmoid(z)
    out = jnp.dot(z.astype(BF16), wout_ref[...], preferred_element_type=F32)
    o_ref[...] = y_ref[...] + mod_ref[0][2:3] * out


def _conv_main(u, y, mods, mrow, w_dw, b_dw, ln_g, ln_b, w_out):
    nh = N_TOK // HALO
    per = SB // HALO
    sb_spec = pl.BlockSpec((SB, D), lambda j, *_: (j, 0))
    return pl.pallas_call(
        _conv_main_kernel,
        out_shape=jax.ShapeDtypeStruct((N_TOK, D), F32),
        grid_spec=pltpu.PrefetchScalarGridSpec(
            num_scalar_prefetch=3, grid=(NSB,),
            in_specs=[
                sb_spec,
                pl.BlockSpec((HALO, D), lambda j, *_: (jnp.maximum(j * per - 1, 0), 0)),
                pl.BlockSpec((HALO, D), lambda j, *_: (jnp.minimum((j + 1) * per, nh - 1), 0)),
                _full_spec((CONV_WIDTH + 1, D)), _full_spec((1, D)), _full_spec((1, D)), _full_spec((1, D)),
                _full_spec((D, D)), sb_spec, _mod_spec(),
            ],
            out_specs=sb_spec,
            scratch_shapes=[pltpu.VMEM((SB + 2 * HALO, D), F32), pltpu.VMEM((SB, D), F32)]),
        compiler_params=_cparams(),
        name="conv_main",
    )(mrow, jnp.asarray(_SEQ_FIRST), jnp.asarray(_SEQ_LAST), u, u, u, w_dw, b_dw, ln_g, ln_b, w_out, y, mods)


def _rope_angles():
    rows = DEC_SEQ // GRID_W
    row = jnp.repeat(jnp.arange(rows, dtype=F32), GRID_W)
    col = jnp.tile(jnp.arange(GRID_W, dtype=F32), rows)
    axis_dim = HEAD_DIM // 2
    freqs = jnp.power(ROPE_THETA, -jnp.arange(axis_dim // 2, dtype=F32) * 2.0 / axis_dim)
    ang_r = row[:, None] * freqs[None, :]
    ang_c = col[:, None] * freqs[None, :]
    return jnp.concatenate([ang_r, ang_r, ang_c, ang_c], axis=-1)


def _rope_blocks():
    ang = _rope_angles()
    cos, sin = jnp.cos(ang), jnp.sin(ang)
    lane = np.arange(HEAD_DIM)
    lo = jnp.asarray(((lane % (HEAD_DIM // 2)) < HEAD_DIM // 4).astype(np.float32))
    sin_a = -sin * lo[None, :]
    sin_b = sin * (1.0 - lo)[None, :]
    nblk = DEC_SEQ // SB
    ident = jnp.ones((1, SB, HEAD_DIM), F32)
    zeros = jnp.zeros((1, SB, HEAD_DIM), F32)
    cos_t = jnp.concatenate([ident, cos.reshape(nblk, SB, HEAD_DIM)], axis=0)
    sa_t = jnp.concatenate([zeros, sin_a.reshape(nblk, SB, HEAD_DIM)], axis=0)
    sb_t = jnp.concatenate([zeros, sin_b.reshape(nblk, SB, HEAD_DIM)], axis=0)
    return cos_t, sa_t, sb_t


def _attn_qkv_kernel(mr_ref, ri_ref, y_ref, mod_ref, g_ref, w_ref, qg_ref, kg_ref, cos_ref, sa_ref, sb_ref,
                     q_ref, kb_ref, vb_ref, kf_ref, vf_ref):
    h = _norm_mod(y_ref[...], g_ref[...], mod_ref[0], 0)
    qkv = jnp.dot(h.astype(BF16), w_ref[...], preferred_element_type=F32)
    cos, sa, sb = cos_ref[0], sa_ref[0], sb_ref[0]
    quarter = HEAD_DIM // 4

    def head(x, g):
        xn = _rms(x, g)
        return xn * cos + pltpu.roll(xn, HEAD_DIM - quarter, 1) * sa + pltpu.roll(xn, quarter, 1) * sb

    scale = HEAD_DIM ** -0.5
    for hd in range(N_HEADS):
        sl = slice(hd * HEAD_DIM, (hd + 1) * HEAD_DIM)
        q_ref[:, sl] = (head(qkv[:, sl], qg_ref[...]) * scale).astype(BF16)
    for kv in range(N_KV_HEADS):
        sl = slice(kv * HEAD_DIM, (kv + 1) * HEAD_DIM)
        kr = head(qkv[:, Q_DIM + kv * HEAD_DIM:Q_DIM + (kv + 1) * HEAD_DIM], kg_ref[...])
        kf_ref[:, sl] = kr
        kb_ref[:, sl] = kr.astype(BF16)
    v = qkv[:, Q_DIM + KV_DIM:]
    vf_ref[...] = v
    vb_ref[...] = v.astype(BF16)


def _attn_qkv(y, mods, mrow, g1, w_qkv, q_g, k_g, rope):
    cos_t, sa_t, sb_t = rope
    rspec = pl.BlockSpec((1, SB, HEAD_DIM), lambda j, mr, ri: (ri[j], 0, 0))
    return pl.pallas_call(
        _attn_qkv_kernel,
        out_shape=(jax.ShapeDtypeStruct((N_TOK, Q_DIM), BF16), jax.ShapeDtypeStruct((N_TOK, KV_DIM), BF16),
                   jax.ShapeDtypeStruct((N_TOK, KV_DIM), BF16), jax.ShapeDtypeStruct((N_TOK, KV_DIM), F32),
                   jax.ShapeDtypeStruct((N_TOK, KV_DIM), F32)),
        grid_spec=pltpu.PrefetchScalarGridSpec(
            num_scalar_prefetch=2, grid=(NSB,),
            in_specs=[_tok_spec(D, SB), _mod_spec(), _full_spec((1, D)), _full_spec((D, QKV_DIM)),
                      _full_spec((1, HEAD_DIM)), _full_spec((1, HEAD_DIM)), rspec, rspec, rspec],
            out_specs=(_tok_spec(Q_DIM, SB), _tok_spec(KV_DIM, SB), _tok_spec(KV_DIM, SB), _tok_spec(KV_DIM, SB),
                       _tok_spec(KV_DIM, SB))),
        compiler_params=_cparams(),
        name="attn_qkv",
    )(mrow, jnp.asarray(_ROPE_IDX_SB), y, mods, g1, w_qkv, q_g, k_g, cos_t, sa_t, sb_t)


def _attn_heads(q, ks, vs, o_scr):
    nt = (((1,), (1,)), ((), ()))
    for hd in range(N_HEADS):
        g = hd // GQA_GROUP
        qh = q[:, hd * HEAD_DIM:(hd + 1) * HEAD_DIM]
        gs = slice(g * HEAD_DIM, (g + 1) * HEAD_DIM)
        ss = [lax.dot_general(qh, k[:, gs], nt, preferred_element_type=F32) for k in ks]
        m = functools.reduce(jnp.maximum, [jnp.max(s, axis=-1, keepdims=True) for s in ss])
        ps = [jnp.exp(s - m) for s in ss]
        l = functools.reduce(lambda a, b: a + b, [jnp.sum(p, axis=-1, keepdims=True) for p in ps])
        o = functools.reduce(lambda a, b: a + b,
                             [jnp.dot(p.astype(BF16), v[:, gs], preferred_element_type=F32) for p, v in zip(ps, vs)])
        o_scr[:, hd * HEAD_DIM:(hd + 1) * HEAD_DIM] = (o / l).astype(BF16)


def _attn_ctx_kernel(q_ref, k_ref, v_ref, wo_ref, y_ref, mod_ref, o_ref, o_scr):
    _attn_heads(q_ref[...], [k_ref[...]], [v_ref[...]], o_scr)
    out = jnp.dot(o_scr[...], wo_ref[...], preferred_element_type=F32)
    o_ref[...] = y_ref[...] + mod_ref[0][2:3] * out


def _attn_lat_kernel(q_ref, k_ref, v_ref, ck_ref, cv_ref, wo_ref, y_ref, mod_ref, ctx_out_ref, o_ref, o_scr):
    del ctx_out_ref
    _attn_heads(q_ref[...], [k_ref[...], ck_ref[0].astype(BF16)], [v_ref[...], cv_ref[0].astype(BF16)], o_scr)
    out = jnp.dot(o_scr[...], wo_ref[...], preferred_element_type=F32)
    o_ref[...] = y_ref[...] + mod_ref[0][2:3] * out


def _attention(q, kb, vb, cache_k, cache_v, w_o, y, mods, layer):
    y_ctx = pl.pallas_call(
        _attn_ctx_kernel,
        out_shape=jax.ShapeDtypeStruct((N_TOK, D), F32),
        grid=(BATCH,),
        in_specs=[
            pl.BlockSpec((SEQ, Q_DIM), lambda s: (s, 0)),
            pl.BlockSpec((SEQ, KV_DIM), lambda s: (s, 0)),
            pl.BlockSpec((SEQ, KV_DIM), lambda s: (s, 0)),
            pl.BlockSpec((Q_DIM, D), lambda s: (0, 0)),
            pl.BlockSpec((SEQ, D), lambda s: (s, 0)),
            pl.BlockSpec((1, 6, D), lambda s: (layer * MOD_ROWS, 0, 0)),
        ],
        out_specs=pl.BlockSpec((SEQ, D), lambda s: (s, 0)),
        scratch_shapes=[pltpu.VMEM((SEQ, Q_DIM), BF16)],
        compiler_params=_cparams(),
        name="attn_ctx",
    )(q, kb, vb, w_o, y, mods)
    pb = NP_TOK // DEC_SEQ
    return pl.pallas_call(
        _attn_lat_kernel,
        out_shape=jax.ShapeDtypeStruct((N_TOK, D), F32),
        input_output_aliases={8: 0},
        grid=(DEC_BATCH, SB_PER_DEC),
        in_specs=[
            pl.BlockSpec((SB, Q_DIM), lambda b, t: (NSBP + b * SB_PER_DEC + t, 0)),
            pl.BlockSpec((DEC_SEQ, KV_DIM), lambda b, t: (pb + b, 0)),
            pl.BlockSpec((DEC_SEQ, KV_DIM), lambda b, t: (pb + b, 0)),
            pl.BlockSpec((1, PAST_LEN, KV_DIM), lambda b, t: (b, 0, 0)),
            pl.BlockSpec((1, PAST_LEN, KV_DIM), lambda b, t: (b, 0, 0)),
            pl.BlockSpec((Q_DIM, D), lambda b, t: (0, 0)),
            pl.BlockSpec((SB, D), lambda b, t: (NSBP + b * SB_PER_DEC + t, 0)),
            pl.BlockSpec((1, 6, D), lambda b, t: (layer * MOD_ROWS + 1 + b, 0, 0)),
            pl.BlockSpec(memory_space=pl.ANY),
        ],
        out_specs=pl.BlockSpec((SB, D), lambda b, t: (NSBP + b * SB_PER_DEC + t, 0)),
        scratch_shapes=[pltpu.VMEM((SB, Q_DIM), BF16)],
        compiler_params=_cparams(2),
        name="attn_lat",
    )(q, kb, vb, cache_k, cache_v, w_o, y, mods, y_ctx)


def _log_sigmoid(x):
    return jnp.minimum(x, 0.0) - jnp.log(1.0 + jnp.exp(-jnp.abs(x)))


def _mlstm_in_kernel(mr_ref, y_ref, mod_ref, g_ref, wf_ref, wg_ref, bg_ref, q_ref, k_ref, v_ref, o_ref, gt_ref,
                     w_ref):
    @pl.when(pl.program_id(0) == 0)
    def _():
        for c in range(4):
            w_ref[:, c * D:(c + 1) * D] = wf_ref[0, :, c * D:(c + 1) * D].astype(BF16)

    h = _norm_mod(y_ref[...], g_ref[...], mod_ref[0], 0)
    hb = h.astype(BF16)
    q_ref[...] = jnp.dot(hb, w_ref[:, 0:D], preferred_element_type=F32).astype(BF16)
    k_ref[...] = (jnp.dot(hb, w_ref[:, D:2 * D], preferred_element_type=F32) * (M_HEAD_DIM ** -0.5)).astype(BF16)
    v_ref[...] = jnp.dot(hb, w_ref[:, 2 * D:3 * D], preferred_element_type=F32).astype(BF16)
    o_ref[...] = _sigmoid(jnp.dot(hb, w_ref[:, 3 * D:4 * D], preferred_element_type=F32)).astype(BF16)
    gt = _dot_hi_lo(h, wg_ref) + bg_ref[...]
    lane = lax.broadcasted_iota(I32, gt.shape, 1)
    is_f = ((lane >= M_HEADS) & (lane < 2 * M_HEADS)) | ((lane >= 3 * M_HEADS) & (lane < 4 * M_HEADS))
    gt_ref[...] = jnp.where(is_f, _log_sigmoid(gt), gt)


def _mlstm_in(y, mods, mrow, g1, w_in_all, slot, w_gate, b_gate):
    w_spec = pl.BlockSpec((1,) + w_in_all.shape[1:], lambda j, *_: (slot, 0, 0), pipeline_mode=pl.Buffered(1))
    return pl.pallas_call(
        _mlstm_in_kernel,
        out_shape=(jax.ShapeDtypeStruct((N_TOK, D), BF16), jax.ShapeDtypeStruct((N_TOK, D), BF16),
                   jax.ShapeDtypeStruct((N_TOK, D), BF16), jax.ShapeDtypeStruct((N_TOK, D), BF16),
                   jax.ShapeDtypeStruct((N_TOK, LANE), F32)),
        grid_spec=pltpu.PrefetchScalarGridSpec(
            num_scalar_prefetch=1, grid=(NB,),
            in_specs=[_tok_spec(D), _mod_spec(), _full_spec((1, D)), w_spec,
                      _full_spec((D, 2 * LANE)), _full_spec((1, LANE))],
            out_specs=(_tok_spec(D), _tok_spec(D), _tok_spec(D), _tok_spec(D), _tok_spec(LANE)),
            scratch_shapes=[pltpu.VMEM((D, 4 * D), BF16)]),
        compiler_params=_cparams(),
        name="mlstm_in",
    )(mrow, y, mods, g1, w_in_all, w_gate, b_gate)


def _mlstm_load(hd, c, q_ref, k_ref, v_ref, gc_ref, gr_ref):
    r0 = pl.multiple_of(c * M_CHUNK, M_CHUNK)
    hs = slice(hd * M_HEAD_DIM, (hd + 1) * M_HEAD_DIM)
    rows = pl.ds(r0, M_CHUNK)
    return rows, hs, q_ref[rows, hs], k_ref[rows, hs], v_ref[rows, hs], gc_ref[hd, rows, :], gr_ref[hd, c]


def _mlstm_chunks(chains, ms, loaded, c_scr, n_scr):
    L = M_CHUNK
    n = range(len(chains))
    t_idx = lax.broadcasted_iota(I32, (L, L), 0)
    s_idx = lax.broadcasted_iota(I32, (L, L), 1)
    masks = {0: (s_idx <= t_idx, t_idx <= s_idx), 1: (s_idx >= t_idx, t_idx >= s_idx)}
    q = [ld[2] for ld in loaded]
    k = [ld[3] for ld in loaded]
    v = [ld[4] for ld in loaded]
    i_col = [ld[5][:, 2 * d:2 * d + 1] for (_, d), ld in zip(chains, loaded)]
    lf_col = [ld[5][:, 2 * d + 1:2 * d + 2] for (_, d), ld in zip(chains, loaded)]
    i_row = [ld[6][2 * d:2 * d + 1, :] for (_, d), ld in zip(chains, loaded)]
    lf_row = [ld[6][2 * d + 1:2 * d + 2, :] for (_, d), ld in zip(chains, loaded)]
    mask = [masks[d][0] for _, d in chains]
    mask_t = [masks[d][1] for _, d in chains]
    b_col = [jnp.sum(jnp.where(mask[i], lf_row[i], 0.0), axis=1, keepdims=True) for i in n]
    b_row = [jnp.sum(jnp.where(mask_t[i], lf_col[i], 0.0), axis=0, keepdims=True) for i in n]
    log_d = [jnp.where(mask[i], b_col[i] - b_row[i] + i_row[i], -jnp.inf) for i in n]
    li = [b_col[i] + ms[i] for i in n]
    m_r = [jnp.maximum(li[i], jnp.max(log_d[i], axis=1, keepdims=True)) for i in n]
    a_int = [jnp.exp(li[i] - m_r[i]) for i in n]
    dmat = [jnp.exp(log_d[i] - m_r[i]) for i in n]
    cmat = [c_scr[d, hd] for hd, d in chains]
    nvec = [n_scr[d, hd] for hd, d in chains]
    gram = [lax.dot_general(q[i], k[i], (((1,), (1,)), ((), ())), preferred_element_type=F32) for i in n]
    inter = [jnp.dot(q[i], cmat[i].astype(BF16), preferred_element_type=F32) for i in n]
    s = [gram[i] * dmat[i] for i in n]
    intra = [jnp.dot(s[i].astype(BF16), v[i], preferred_element_type=F32) for i in n]
    qn = [jnp.sum(q[i].astype(F32) * nvec[i], axis=1, keepdims=True) for i in n]
    den = [a_int[i] * qn[i] + jnp.sum(s[i], axis=1, keepdims=True) for i in n]
    hh = [(a_int[i] * inter[i] + intra[i]) / jnp.maximum(jnp.abs(den[i]), jnp.exp(-m_r[i])) for i in n]
    b_last = [b_row[i][:, L - 1:L] if chains[i][1] == 0 else b_row[i][:, 0:1] for i in n]
    log_w = [b_last[i] - b_col[i] + i_col[i] for i in n]
    m_new = [jnp.maximum(b_last[i] + ms[i], jnp.max(log_w[i], axis=0, keepdims=True)) for i in n]
    w = [jnp.exp(log_w[i] - m_new[i]) for i in n]
    decay = [jnp.exp(b_last[i] + ms[i] - m_new[i]) for i in n]
    kw = [k[i].astype(F32) * w[i] for i in n]
    kv = [lax.dot_general(kw[i].astype(BF16), v[i], (((0,), (0,)), ((), ())), preferred_element_type=F32) for i in n]
    for i, (hd, d) in enumerate(chains):
        c_scr[d, hd] = decay[i] * cmat[i] + kv[i]
        n_scr[d, hd] = decay[i] * nvec[i] + jnp.sum(kw[i], axis=0, keepdims=True)
    return hh, m_new


def _mlstm_scan_body(n_chunks, q_ref, k_ref, v_ref, gc_ref, gr_ref, h_ref, hb_scr, c_scr, n_scr, m0):
    chains = [(hd, d) for hd in range(M_HEADS) for d in range(2)]

    def body(c, ms):
        loaded = [_mlstm_load(hd, c if d == 0 else n_chunks - 1 - c, q_ref, k_ref, v_ref, gc_ref, gr_ref)
                  for hd, d in chains]
        hh, m_new = _mlstm_chunks(chains, ms, loaded, c_scr, n_scr)
        for (hd, d), ld, h in zip(chains, loaded, hh):
            dst = h_ref if d == 0 else hb_scr
            dst[ld[0], ld[1]] = h
        return tuple(m_new)

    ms = lax.fori_loop(0, n_chunks, body, tuple(m0))
    h_ref[...] += hb_scr[...]
    return ms


def _mlstm_scan_ctx_kernel(q_ref, k_ref, v_ref, gc_ref, gr_ref, h_ref, cn_ref, nn_ref, mn_ref, hb_scr, c_scr, n_scr):
    c_scr[...] = jnp.zeros(c_scr.shape, F32)
    n_scr[...] = jnp.zeros(n_scr.shape, F32)
    zero = jnp.zeros((1, 1), F32)
    ms = _mlstm_scan_body(SEQ // M_CHUNK, q_ref, k_ref, v_ref, gc_ref, gr_ref, h_ref, hb_scr, c_scr, n_scr,
                          [zero] * (2 * M_HEADS))
    cn_ref[0] = c_scr[...]
    nn_ref[0] = n_scr[...]
    for hd in range(M_HEADS):
        for d in range(2):
            mn_ref[0, d, hd] = jnp.broadcast_to(ms[2 * hd + d], (1, LANE))


def _mlstm_scan_lat_kernel(q_ref, k_ref, v_ref, gc_ref, gr_ref, c0_ref, n0_ref, m0_ref, ctx_out_ref, h_ref,
                           hb_scr, c_scr, n_scr):
    del ctx_out_ref
    c_scr[...] = c0_ref[0]
    n_scr[...] = n0_ref[0]
    m0 = [m0_ref[0, d, hd] for hd in range(M_HEADS) for d in range(2)]
    _mlstm_scan_body(DEC_SEQ // M_CHUNK, q_ref, k_ref, v_ref, gc_ref, gr_ref, h_ref, hb_scr, c_scr, n_scr, m0)


def _mlstm_scan(q, k, v, gates, state_c, state_n, state_m):
    g16 = gates[:, :4 * M_HEADS].reshape(N_TOK, 4, M_HEADS)
    gcol = jnp.transpose(g16, (2, 0, 1))
    grow = jnp.transpose(g16.reshape(N_TOK // M_CHUNK, M_CHUNK, 4, M_HEADS), (3, 0, 2, 1))
    hd = M_HEAD_DIM
    state_scratch = [pltpu.VMEM((2, M_HEADS, hd, hd), F32), pltpu.VMEM((2, M_HEADS, 1, hd), F32)]
    ncp = SEQ // M_CHUNK
    h_ctx, new_c, new_n, new_m = pl.pallas_call(
        _mlstm_scan_ctx_kernel,
        out_shape=(jax.ShapeDtypeStruct((N_TOK, D), F32),
                   jax.ShapeDtypeStruct((BATCH, 2, M_HEADS, hd, hd), F32),
                   jax.ShapeDtypeStruct((BATCH, 2, M_HEADS, 1, hd), F32),
                   jax.ShapeDtypeStruct((BATCH, 2, M_HEADS, 1, LANE), F32)),
        grid=(BATCH,),
        in_specs=[
            pl.BlockSpec((SEQ, D), lambda s: (s, 0)),
            pl.BlockSpec((SEQ, D), lambda s: (s, 0)),
            pl.BlockSpec((SEQ, D), lambda s: (s, 0)),
            pl.BlockSpec((M_HEADS, SEQ, 4), lambda s: (0, s, 0)),
            pl.BlockSpec((M_HEADS, ncp, 4, M_CHUNK), lambda s: (0, s, 0, 0)),
        ],
        out_specs=(
            pl.BlockSpec((SEQ, D), lambda s: (s, 0)),
            pl.BlockSpec((1, 2, M_HEADS, hd, hd), lambda s: (s, 0, 0, 0, 0)),
            pl.BlockSpec((1, 2, M_HEADS, 1, hd), lambda s: (s, 0, 0, 0, 0)),
            pl.BlockSpec((1, 2, M_HEADS, 1, LANE), lambda s: (s, 0, 0, 0, 0)),
        ),
        scratch_shapes=[pltpu.VMEM((SEQ, D), F32)] + state_scratch,
        compiler_params=_cparams(),
        name="mlstm_scan_ctx",
    )(q, k, v, gcol, grow)
    ncl = DEC_SEQ // M_CHUNK
    pb = NP_TOK // DEC_SEQ
    h_all = pl.pallas_call(
        _mlstm_scan_lat_kernel,
        out_shape=jax.ShapeDtypeStruct((N_TOK, D), F32),
        input_output_aliases={8: 0},
        grid=(DEC_BATCH,),
        in_specs=[
            pl.BlockSpec((DEC_SEQ, D), lambda b: (pb + b, 0)),
            pl.BlockSpec((DEC_SEQ, D), lambda b: (pb + b, 0)),
            pl.BlockSpec((DEC_SEQ, D), lambda b: (pb + b, 0)),
            pl.BlockSpec((M_HEADS, DEC_SEQ, 4), lambda b: (0, pb + b, 0)),
            pl.BlockSpec((M_HEADS, ncl, 4, M_CHUNK), lambda b: (0, pb + b, 0, 0)),
            pl.BlockSpec((1, 2, M_HEADS, hd, hd), lambda b: (b, 0, 0, 0, 0)),
            pl.BlockSpec((1, 2, M_HEADS, 1, hd), lambda b: (b, 0, 0, 0, 0)),
            pl.BlockSpec((1, 2, M_HEADS, 1, 1), lambda b: (b, 0, 0, 0, 0)),
            pl.BlockSpec(memory_space=pl.ANY),
        ],
        out_specs=pl.BlockSpec((DEC_SEQ, D), lambda b: (pb + b, 0)),
        scratch_shapes=[pltpu.VMEM((DEC_SEQ, D), F32)] + state_scratch,
        compiler_params=_cparams(),
        name="mlstm_scan_lat",
    )(q, k, v, gcol, grow, state_c, state_n, state_m, h_ctx)
    return h_all, new_c, new_n, new_m


def _mlstm_out_kernel(mr_ref, h_ref, o_ref, ng_ref, w_ref, y_ref, mod_ref, out_ref, x_scr):
    hc = o_ref[...] * h_ref[...]
    for hd in range(M_HEADS):
        sl = slice(hd * M_HEAD_DIM, (hd + 1) * M_HEAD_DIM)
        x_scr[:, sl] = _rms(hc[:, sl], ng_ref[:, sl]).astype(BF16)
    out = jnp.dot(x_scr[...], w_ref[...], preferred_element_type=F32)
    out_ref[...] = y_ref[...] + mod_ref[0][2:3] * out


def _mlstm_out(hsum, o, norm_g, w_out, y, mods, mrow):
    return pl.pallas_call(
        _mlstm_out_kernel,
        out_shape=jax.ShapeDtypeStruct((N_TOK, D), F32),
        grid_spec=pltpu.PrefetchScalarGridSpec(
            num_scalar_prefetch=1, grid=(NB,),
            in_specs=[_tok_spec(D), _tok_spec(D), _full_spec((1, D)), _full_spec((D, D)), _tok_spec(D), _mod_spec()],
            out_specs=_tok_spec(D),
            scratch_shapes=[pltpu.VMEM((TM, D), BF16)]),
        compiler_params=_cparams(),
        name="mlstm_out",
    )(mrow, hsum, o, norm_g, w_out, y, mods)


ROUTE_OFF = N_GROUPS
SLAB = D // (2 * LANE)
V7X_SC_CORES = 2
V7X_SC_SUBCORES = 16
SC_WORKERS = V7X_SC_CORES * V7X_SC_SUBCORES
SC_WINDOW = 128
HI_MASK = -65536


def _bf16_bits(x):
    return lax.bitcast_convert_type(x.astype(BF16).astype(F32), I32)


def _store_slabs(ref, x):
    rows = x.shape[0]
    for c in range(SLAB):
        lo = lax.shift_right_logical(_bf16_bits(x[:, (2 * c) * LANE:(2 * c + 1) * LANE]), 16)
        hi = _bf16_bits(x[:, (2 * c + 1) * LANE:(2 * c + 2) * LANE]) & HI_MASK
        ref[pl.ds(c, rows, stride=SLAB), :] = lo | hi


def _load_slabs(ref, dst, rows, dtype):
    for c in range(SLAB):
        w = ref[pl.ds(c, rows, stride=SLAB), :]
        lo = lax.bitcast_convert_type(lax.shift_left(w, 16), F32)
        hi = lax.bitcast_convert_type(w & HI_MASK, F32)
        dst[:, (2 * c) * LANE:(2 * c + 1) * LANE] = lo.astype(dtype)
        dst[:, (2 * c + 1) * LANE:(2 * c + 2) * LANE] = hi.astype(dtype)


def _route_kernel(mr_ref, y_ref, mod_ref, g_ref, wr_ref, br_ref, tri_ref, x_ref, wt_ref, meta_ref, cnt_ref, cnt_scr):
    x = _norm_mod(y_ref[...], g_ref[...], mod_ref[0], 1)
    _store_slabs(x_ref, x)
    lg = _dot_hi_lo(x, wr_ref) + br_ref[...]
    lane = lax.broadcasted_iota(I32, lg.shape, 1).astype(F32)
    ninf = -jnp.inf
    big = float(LANE)
    lgg = jnp.where(lane < N_GROUPS, lg, ninf)
    gmax = jnp.max(lgg, axis=-1, keepdims=True)
    g_idx = jnp.min(jnp.where(lgg == gmax, lane, big), axis=-1, keepdims=True)
    g_w = 1.0 / jnp.sum(jnp.exp(lgg - gmax), axis=-1, keepdims=True)
    lo = ROUTE_OFF + g_idx * EXPERTS_PER_GROUP
    le = jnp.where((lane >= lo) & (lane < lo + EXPERTS_PER_GROUP), lg, ninf)
    m1 = jnp.max(le, axis=-1, keepdims=True)
    i1 = jnp.min(jnp.where(le == m1, lane, big), axis=-1, keepdims=True)
    le2 = jnp.where(lane == i1, ninf, le)
    m2 = jnp.max(le2, axis=-1, keepdims=True)
    i2 = jnp.min(jnp.where(le2 == m2, lane, big), axis=-1, keepdims=True)
    r = jnp.exp(m2 - m1)
    p1 = 1.0 / (1.0 + r)
    p2 = r / (1.0 + r)
    two = lax.broadcasted_iota(I32, (x.shape[0], TOP_K), 1)
    wt_ref[...] = jnp.where(two == 0, g_w * p1, g_w * p2)
    @pl.when(pl.program_id(0) == 0)
    def _():
        cnt_scr[...] = jnp.zeros(cnt_scr.shape, F32)

    oh1 = (lane == i1).astype(F32)
    oh2 = (lane == i2).astype(F32)
    both = oh1 + oh2
    before = jnp.dot(tri_ref[...], both.astype(BF16), preferred_element_type=F32) + cnt_scr[...]
    rk1 = jnp.sum(oh1 * before, axis=-1, keepdims=True)
    rk2 = jnp.sum(oh2 * before, axis=-1, keepdims=True)
    cnt_scr[...] = cnt_scr[...] + jnp.sum(both, axis=0, keepdims=True)
    cnt_ref[...] = cnt_scr[...]
    cols = (i1 - ROUTE_OFF, i2 - ROUTE_OFF, rk1, rk2)
    packed = jnp.zeros(lg.shape, F32)
    for c, val in enumerate(cols):
        packed = jnp.where(lane == c, val, packed)
    meta_ref[...] = jnp.transpose(packed)[0:len(cols), :].astype(I32)


def _route(y, mods, mrow, g2, w_route, b_route):
    return pl.pallas_call(
        _route_kernel,
        out_shape=(jax.ShapeDtypeStruct((N_TOK * SLAB, LANE), I32), jax.ShapeDtypeStruct((N_TOK, TOP_K), F32),
                   jax.ShapeDtypeStruct((2 * TOP_K, N_TOK), I32), jax.ShapeDtypeStruct((1, LANE), F32)),
        grid_spec=pltpu.PrefetchScalarGridSpec(
            num_scalar_prefetch=1, grid=(NB,),
            in_specs=[_tok_spec(D), _mod_spec(), _full_spec((1, D)), _full_spec((D, 2 * LANE)), _full_spec((1, LANE)),
                      _full_spec((TM, TM))],
            out_specs=(pl.BlockSpec((TM * SLAB, LANE), lambda j, *_: (j, 0)), _tok_spec(TOP_K),
                       pl.BlockSpec((2 * TOP_K, TM), lambda j, *_: (0, j)), _full_spec((1, LANE))),
            scratch_shapes=[pltpu.VMEM((1, LANE), F32)]),
        compiler_params=_cparams(),
        name="moe_route",
    )(mrow, y, mods, g2, w_route, b_route, jnp.asarray(np.tril(np.ones((TM, TM), np.float32), -1), dtype=BF16))


SLOT_COLS = 2048


def _slot_kernel(meta_ref, ps_ref, o_ref):
    sub = lax.broadcasted_iota(I32, (N_EXPERTS, SLOT_COLS), 0)
    meta = meta_ref[...]
    table = ps_ref[...]
    for k in range(TOP_K):
        start = jnp.sum(jnp.where(sub == meta[k:k + 1, :], table, 0), axis=0, keepdims=True)
        o_ref[k:k + 1, :] = start + meta[TOP_K + k:TOP_K + k + 1, :]


def _slots(meta, pad_start):
    return pl.pallas_call(
        _slot_kernel,
        out_shape=jax.ShapeDtypeStruct((TOP_K, N_TOK), I32),
        grid=(N_TOK // SLOT_COLS,),
        in_specs=[pl.BlockSpec((2 * TOP_K, SLOT_COLS), lambda j: (0, j)), pl.BlockSpec((N_EXPERTS, 1), lambda j: (0, 0))],
        out_specs=pl.BlockSpec((TOP_K, SLOT_COLS), lambda j: (0, j)),
        compiler_params=_cparams(),
        name="moe_slots",
    )(meta, pad_start.astype(I32).reshape(N_EXPERTS, 1))


def _dispatch_tables(meta, lane_counts):
    counts = lane_counts[0, ROUTE_OFF:ROUTE_OFF + N_EXPERTS].astype(I32)
    padded = ((counts + EBLK - 1) // EBLK) * EBLK
    pad_end = jnp.cumsum(padded)
    pad_start = pad_end - padded
    dest = _slots(meta, pad_start)
    n_blk = (padded // EBLK).astype(I32)
    blk_start = (pad_start // EBLK).astype(I32)
    n_used = (pad_end[-1] // EBLK).astype(I32).reshape(1)
    return dest, blk_start, n_blk, n_used


def _sc_mesh():
    return plsc.VectorSubcoreMesh(core_axis_name="core", subcore_axis_name="subcore",
                                  num_cores=V7X_SC_CORES, num_subcores=V7X_SC_SUBCORES)


def _sc_worker():
    return lax.axis_index("core") * V7X_SC_SUBCORES + lax.axis_index("subcore")


def _sc_dispatch(x_slabs, d0, d1):
    per = N_TOK // SC_WORKERS

    @functools.partial(
        pl.kernel, out_type=jax.ShapeDtypeStruct((P_SLOTS, SLAB, LANE), I32), mesh=_sc_mesh(), name="moe_dispatch",
        scratch_types=[pltpu.VMEM((1, per), I32), pltpu.VMEM((1, per), I32), pltpu.VMEM((SC_WINDOW, SLAB, LANE), I32)])
    def run(x_hbm, d0_hbm, d1_hbm, o_hbm, i0_v, i1_v, buf):
        base = _sc_worker() * per
        pltpu.sync_copy(d0_hbm.at[:, pl.ds(base, per)], i0_v)
        pltpu.sync_copy(d1_hbm.at[:, pl.ds(base, per)], i1_v)

        @pl.loop(0, per // SC_WINDOW)
        def _(s):
            off = s * SC_WINDOW
            pltpu.sync_copy(x_hbm.at[pl.ds(base + off, SC_WINDOW)], buf)
            pltpu.sync_copy(buf, o_hbm.at[i0_v.at[0, pl.ds(off, SC_WINDOW)]])
            pltpu.sync_copy(buf, o_hbm.at[i1_v.at[0, pl.ds(off, SC_WINDOW)]])

    return run(x_slabs.reshape(N_TOK, SLAB, LANE), d0, d1)


def _sc_collect(y_slabs, dcat):
    per = N_ASSIGN // SC_WORKERS

    @functools.partial(
        pl.kernel, out_type=jax.ShapeDtypeStruct((N_ASSIGN, SLAB, LANE), I32), mesh=_sc_mesh(), name="moe_collect",
        scratch_types=[pltpu.VMEM((1, per), I32), pltpu.VMEM((SC_WINDOW, SLAB, LANE), I32)])
    def run(y_hbm, i_hbm, o_hbm, i_v, buf):
        base = _sc_worker() * per
        pltpu.sync_copy(i_hbm.at[:, pl.ds(base, per)], i_v)

        @pl.loop(0, per // SC_WINDOW)
        def _(s):
            off = s * SC_WINDOW
            pltpu.sync_copy(y_hbm.at[i_v.at[0, pl.ds(off, SC_WINDOW)]], buf)
            pltpu.sync_copy(buf, o_hbm.at[pl.ds(base + off, SC_WINDOW)])

    return run(y_slabs.reshape(P_SLOTS, SLAB, LANE), dcat)


EROWS = EBLK * SLAB


W_CHUNKS = 8
W_SLOTS = 2


def _expert_kernel(layer, bs_ref, nb_ref, nu_ref, wg_hbm, wu_hbm, wd_hbm, x_hbm, y_hbm,
                   xbuf, ybuf, xs, wg_f, wu_f, wd_f, wg_bf, wu_bf, wd_bf, isem, osem, wsem):
    e = pl.program_id(0)
    n_exp = pl.num_programs(0)
    n_used = nu_ref[0]
    b0 = bs_ref[e]
    nb = nb_ref[e]
    wslot = lax.rem(e, W_SLOTS)

    def weight_copies(ex, slot):
        out = []
        for hbm, buf in ((wg_hbm, wg_f), (wu_hbm, wu_f), (wd_hbm, wd_f)):
            rows = buf.shape[1] // W_CHUNKS
            for c in range(W_CHUNKS):
                rs = pl.ds(c * rows, rows)
                out.append(pltpu.make_async_copy(hbm.at[layer, ex, rs], buf.at[slot, rs], wsem.at[slot]))
        return out

    def start_weights(ex, slot):
        for i, cp in enumerate(weight_copies(ex, slot)):
            cp.start(priority=i % 2)

    ahead = W_SLOTS - 1

    @pl.when(e == 0)
    def _():
        for ex in range(ahead):
            start_weights(ex, ex)

    for cp in weight_copies(e, wslot):
        cp.wait()

    @pl.when(e + ahead < n_exp)
    def _():
        start_weights(e + ahead, lax.rem(e + ahead, W_SLOTS))

    def in_copy(g, slot):
        return pltpu.make_async_copy(x_hbm.at[pl.ds(pl.multiple_of(g * EROWS, EROWS), EROWS)], xbuf.at[slot],
                                     isem.at[slot])

    def out_copy(g, slot):
        return pltpu.make_async_copy(ybuf.at[slot], y_hbm.at[pl.ds(pl.multiple_of(g * EROWS, EROWS), EROWS)],
                                     osem.at[slot])

    @pl.when(e == 0)
    def _():
        in_copy(0, 0).start(priority=1)

    @pl.when(nb > 0)
    def _():
        wg_bf[...] = wg_f[wslot].astype(BF16)
        wu_bf[...] = wu_f[wslot].astype(BF16)
        wd_bf[...] = wd_f[wslot].astype(BF16)

    def block(k, carry):
        g = b0 + k
        slot = lax.rem(g, 2)
        in_copy(g, slot).wait()

        @pl.when(g + 1 < n_used)
        def _():
            in_copy(g + 1, 1 - slot).start(priority=1)

        _load_slabs(xbuf.at[slot], xs, EBLK, BF16)
        xb = xs[...]
        gt = jnp.dot(xb, wg_bf[...], preferred_element_type=F32)
        up = jnp.dot(xb, wu_bf[...], preferred_element_type=F32)
        hmid = (gt * _sigmoid(gt) * up).astype(BF16)
        res = jnp.dot(hmid, wd_bf[...], preferred_element_type=F32)

        @pl.when(g >= 2)
        def _():
            out_copy(g - 2, slot).wait()

        _store_slabs(ybuf.at[slot], res)
        out_copy(g, slot).start()
        return carry

    lax.fori_loop(0, nb, block, 0)

    @pl.when(e == n_exp - 1)
    def _():
        last = n_used - 1
        out_copy(last, lax.rem(last, 2)).wait()

        @pl.when(n_used >= 2)
        def _():
            out_copy(last - 1, lax.rem(last - 1, 2)).wait()


def _experts(x_sorted, blk_start, n_blk, n_used, w_gate, w_up, w_down, layer):
    any_spec = pl.BlockSpec(memory_space=pl.ANY)
    return pl.pallas_call(
        functools.partial(_expert_kernel, layer),
        out_shape=jax.ShapeDtypeStruct((P_SLOTS * SLAB, LANE), I32),
        grid_spec=pltpu.PrefetchScalarGridSpec(
            num_scalar_prefetch=3, grid=(N_EXPERTS,),
            in_specs=[any_spec, any_spec, any_spec, any_spec],
            out_specs=any_spec,
            scratch_shapes=[
                pltpu.VMEM((2, EROWS, LANE), I32), pltpu.VMEM((2, EROWS, LANE), I32),
                pltpu.VMEM((EBLK, D), BF16),
                pltpu.VMEM((W_SLOTS, D, D_EXPERT), F32), pltpu.VMEM((W_SLOTS, D, D_EXPERT), F32),
                pltpu.VMEM((W_SLOTS, D_EXPERT, D), F32),
                pltpu.VMEM((D, D_EXPERT), BF16), pltpu.VMEM((D, D_EXPERT), BF16), pltpu.VMEM((D_EXPERT, D), BF16),
                pltpu.SemaphoreType.DMA((2,)), pltpu.SemaphoreType.DMA((2,)), pltpu.SemaphoreType.DMA((W_SLOTS,)),
            ]),
        compiler_params=_cparams(),
        name="moe_experts",
    )(blk_start, n_blk, n_used, w_gate, w_up, w_down, x_sorted.reshape(P_SLOTS * SLAB, LANE))


def _combine_kernel(final, mr_ref, e0_ref, e1_ref, wt_ref, y_ref, mod_ref, fg_ref, o_ref, a_scr, b_scr):
    _load_slabs(e0_ref, a_scr, TM, F32)
    _load_slabs(e1_ref, b_scr, TM, F32)
    wt = wt_ref[...]
    moe = wt[:, 0:1] * a_scr[...] + wt[:, 1:2] * b_scr[...]
    y_new = y_ref[...] + mod_ref[0][5:6] * moe
    o_ref[...] = _rms(y_new, fg_ref[...]) if final else y_new


def _combine(ym, wts, y, mods, mrow, final_g, blk0, nblk, final):
    tok = lambda width: pl.BlockSpec((TM, width), lambda j, *_: (blk0 + j, 0))
    slab0 = pl.BlockSpec((TM * SLAB, LANE), lambda j, *_: (blk0 + j, 0))
    slab1 = pl.BlockSpec((TM * SLAB, LANE), lambda j, *_: (NB + blk0 + j, 0))
    mod = pl.BlockSpec((1, 6, D), lambda j, mr: (mr[blk0 + j], 0, 0))
    return pl.pallas_call(
        functools.partial(_combine_kernel, final),
        out_shape=jax.ShapeDtypeStruct((nblk * TM, D), F32),
        grid_spec=pltpu.PrefetchScalarGridSpec(
            num_scalar_prefetch=1, grid=(nblk,),
            in_specs=[slab0, slab1, tok(TOP_K), tok(D), mod, _full_spec((1, D))],
            out_specs=pl.BlockSpec((TM, D), lambda j, *_: (j, 0)),
            scratch_shapes=[pltpu.VMEM((TM, D), F32), pltpu.VMEM((TM, D), F32)]),
        compiler_params=_cparams(),
        name="moe_combine",
    )(mrow, ym, ym, wts, y, mods, final_g)


def kernel(x_prompt, x_sample, cache_attn_k, cache_attn_v, state_mlstm_C, state_mlstm_n, state_mlstm_m, c, c_ctx, ada_w, ada_b, norm1_g, norm2_g, conv_w_in, conv_w_dw, conv_b_dw, conv_ln_g, conv_ln_b, conv_w_out, attn_w_qkv, attn_q_norm, attn_k_norm, attn_w_o, mlstm_w_in, mlstm_b_gate, mlstm_norm_g, mlstm_w_out, moe_w_group, moe_b_group, moe_w_router, moe_b_router, moe_w_gate, moe_w_up, moe_w_down, final_norm_g):
    y = jnp.concatenate([x_prompt.reshape(NP_TOK, D), x_sample.reshape(NS_TOK, D)], axis=0)
    cvec = jnp.concatenate([c_ctx[None, :], c, jnp.zeros((MOD_ROWS - 1 - DEC_BATCH, D), F32)], axis=0)
    mods = _ada_all(cvec, ada_w, ada_b)
    rope = _rope_blocks()
    new_k = new_v = new_c = new_n = new_m = None
    for i in range(DEPTH):
        kind, slot = i % 3, i // 3
        mrow = jnp.asarray(_MOD_ROW + i * MOD_ROWS)
        mrow_sb = jnp.asarray(_MOD_ROW_SB + i * MOD_ROWS)
        g1 = norm1_g[i].reshape(1, D)
        if kind == 0:
            u = _conv_in(y, mods, mrow, g1, conv_w_in[slot].astype(BF16))
            w_dw = jnp.concatenate([conv_w_dw[slot], jnp.zeros((1, D), F32)], axis=0)
            y = _conv_main(u, y, mods, mrow_sb, w_dw, conv_b_dw[slot].reshape(1, D), conv_ln_g[slot].reshape(1, D),
                           conv_ln_b[slot].reshape(1, D), conv_w_out[slot].astype(BF16))
        elif kind == 1:
            q, kb, vb, kf, vf = _attn_qkv(y, mods, mrow_sb, g1, attn_w_qkv[slot].astype(BF16),
                                          attn_q_norm[slot].reshape(1, HEAD_DIM), attn_k_norm[slot].reshape(1, HEAD_DIM),
                                          rope)
            new_k = kf[:NP_TOK].reshape(BATCH, 1, SEQ, N_KV_HEADS, HEAD_DIM)
            new_v = vf[:NP_TOK].reshape(BATCH, 1, SEQ, N_KV_HEADS, HEAD_DIM)
            ck = cache_attn_k[:, slot].reshape(DEC_BATCH, PAST_LEN, KV_DIM)
            cv = cache_attn_v[:, slot].reshape(DEC_BATCH, PAST_LEN, KV_DIM)
            y = _attention(q, kb, vb, ck, cv, attn_w_o[slot].astype(BF16), y, mods, i)
        else:
            w_in = mlstm_w_in[slot]
            w_gate = jnp.concatenate([w_in[:, 4 * D:], jnp.zeros((D, LANE - 4 * M_HEADS), F32)], axis=1)
            b_gate = jnp.concatenate([mlstm_b_gate[slot], jnp.zeros((LANE - 4 * M_HEADS,), F32)]).reshape(1, LANE)
            q, k, v, o, gates = _mlstm_in(y, mods, mrow, g1, mlstm_w_in, slot, _split_hi_lo(w_gate), b_gate)
            sc = state_mlstm_C[:, slot]
            sn = state_mlstm_n[:, slot].reshape(DEC_BATCH, 2, M_HEADS, 1, M_HEAD_DIM)
            sm = state_mlstm_m[:, slot].reshape(DEC_BATCH, 2, M_HEADS, 1, 1)
            hsum, nc_, nn_, nm_ = _mlstm_scan(q, k, v, gates, sc, sn, sm)
            new_c = nc_[:, None]
            new_n = nn_.reshape(BATCH, 1, 2, M_HEADS, M_HEAD_DIM)
            new_m = nm_[..., 0, 0].reshape(BATCH, 1, 2, M_HEADS)
            y = _mlstm_out(hsum, o, mlstm_norm_g[slot].reshape(1, D), mlstm_w_out[slot].astype(BF16), y, mods, mrow)
        w_route = jnp.concatenate([moe_w_group[i], moe_w_router[i],
                                   jnp.zeros((D, LANE - N_GROUPS - N_EXPERTS), F32)], axis=1)
        b_route = jnp.concatenate([moe_b_group[i], moe_b_router[i],
                                   jnp.zeros((LANE - N_GROUPS - N_EXPERTS,), F32)]).reshape(1, LANE)
        x2, ewt, meta, cnt = _route(y, mods, mrow, norm2_g[i].reshape(1, D), _split_hi_lo(w_route), b_route)
        dest, blk_start, n_blk, n_used = _dispatch_tables(meta, cnt)
        x_sorted = _sc_dispatch(x2, dest[0:1], dest[1:2])
        y_sorted = _experts(x_sorted, blk_start, n_blk, n_used, moe_w_gate, moe_w_up, moe_w_down, i)
        ym = _sc_collect(y_sorted, dest.reshape(1, N_ASSIGN))
        ym = ym.reshape(N_ASSIGN * SLAB, LANE)
        fg = final_norm_g.reshape(1, D)
        if i + 1 < DEPTH:
            y = _combine(ym, ewt, y, mods, mrow, fg, 0, NB, False)
        else:
            y_prompt = _combine(ym, ewt, y, mods, mrow, fg, 0, NBP, True).reshape(BATCH, SEQ, D)
            y_sample = _combine(ym, ewt, y, mods, mrow, fg, NBP, NB - NBP, True).reshape(DEC_BATCH, DEC_SEQ, D)
    return (y_prompt, y_sample, new_k, new_v, new_c, new_n, new_m)
```

```python
import functools

import jax
import jax.numpy as jnp
import numpy as np
from jax import lax
from jax.experimental import pallas as pl
from jax.experimental.pallas import tpu as pltpu
from jax.experimental.pallas import tpu_sc as plsc

F32 = jnp.float32
BF16 = jnp.bfloat16
I32 = jnp.int32

D = 1024
BATCH, SEQ = 16, 256
DEC_BATCH, DEC_SEQ = 8, 1024
PAST_LEN = 256
DEPTH = 4
GRID_W = 64
EPS = 1e-6
CONV_WIDTH = 31
CONV_PAD = CONV_WIDTH // 2
HEAD_DIM = 128
N_HEADS = 8
N_KV_HEADS = 2
GQA_GROUP = N_HEADS // N_KV_HEADS
Q_DIM = N_HEADS * HEAD_DIM
KV_DIM = N_KV_HEADS * HEAD_DIM
QKV_DIM = Q_DIM + 2 * KV_DIM
ROPE_THETA = 10000.0
M_HEADS = 4
M_HEAD_DIM = D // M_HEADS
M_CHUNK = 64
N_GROUPS = 4
EXPERTS_PER_GROUP = 8
N_EXPERTS = N_GROUPS * EXPERTS_PER_GROUP
TOP_K = 2
D_EXPERT = 512

NP_TOK = BATCH * SEQ
NS_TOK = DEC_BATCH * DEC_SEQ
N_TOK = NP_TOK + NS_TOK
TM = 512
NB = N_TOK // TM
NBP = NP_TOK // TM
BLK_PER_DEC = DEC_SEQ // TM
SB = 256
NSB = N_TOK // SB
NSBP = NP_TOK // SB
SB_PER_DEC = DEC_SEQ // SB
MOD_ROWS = 16
HALO = 16
LANE = 128
SUBLANE = 8

N_ASSIGN = N_TOK * TOP_K
EBLK = 256
N_EBLK = N_ASSIGN // EBLK + N_EXPERTS
P_SLOTS = N_EBLK * EBLK
N_PAD_SLOTS = P_SLOTS - N_ASSIGN

VMEM_LIMIT = 56 * 1024 * 1024


def _block_tables(nb, nbp, per_dec):
    j = np.arange(nb)
    is_p = j < nbp
    mod_row = np.where(is_p, 0, 1 + (j - nbp) // per_dec)
    rope_idx = np.where(is_p, 0, 1 + (j - nbp) % per_dec)
    first = np.where(is_p, 1, ((j - nbp) % per_dec == 0).astype(np.int64))
    last = np.where(is_p, 1, ((j - nbp) % per_dec == per_dec - 1).astype(np.int64))
    return (mod_row.astype(np.int32), rope_idx.astype(np.int32), first.astype(np.int32), last.astype(np.int32))


_MOD_ROW, _, _, _ = _block_tables(NB, NBP, BLK_PER_DEC)
_MOD_ROW_SB, _ROPE_IDX_SB, _SEQ_FIRST, _SEQ_LAST = _block_tables(NSB, NSBP, SB_PER_DEC)


def _cparams(n_axes=1):
    return pltpu.CompilerParams(dimension_semantics=("arbitrary",) * n_axes, vmem_limit_bytes=VMEM_LIMIT)


def _sigmoid(x):
    return 1.0 / (1.0 + jnp.exp(-x))


def _rms(x, g):
    return x * lax.rsqrt(jnp.mean(x * x, axis=-1, keepdims=True) + EPS) * g


def _split_hi_lo(w):
    hi = w.astype(BF16)
    return jnp.concatenate([hi, (w - hi.astype(F32)).astype(BF16)], axis=1)


def _dot_hi_lo(x, w_ref):
    n = w_ref.shape[1] // 2
    xh = x.astype(BF16)
    xl = (x - xh.astype(F32)).astype(BF16)
    both = jnp.dot(xh, w_ref[...], preferred_element_type=F32)
    return both[:, :n] + (both[:, n:] + jnp.dot(xl, w_ref[:, :n], preferred_element_type=F32))


def _norm_mod(y, g, mod, which):
    shift = mod[3 * which:3 * which + 1]
    scale = mod[3 * which + 1:3 * which + 2]
    return _rms(y, g) * (1.0 + scale) + shift


def _ada_kernel(c_ref, w_ref, b_ref, o_ref):
    c = c_ref[...]
    s = c * _sigmoid(c)
    o_ref[0] = jnp.dot(s.astype(BF16), w_ref[0].astype(BF16), preferred_element_type=F32) + b_ref[0]


def _ada_all(cvec, ada_w, ada_b):
    tn = 1536
    out = pl.pallas_call(
        _ada_kernel,
        out_shape=jax.ShapeDtypeStruct((DEPTH, MOD_ROWS, 6 * D), F32),
        grid=(DEPTH, 6 * D // tn),
        in_specs=[
            pl.BlockSpec((MOD_ROWS, D), lambda l, n: (0, 0)),
            pl.BlockSpec((1, D, tn), lambda l, n: (l, 0, n)),
            pl.BlockSpec((1, 1, tn), lambda l, n: (l, 0, n)),
        ],
        out_specs=pl.BlockSpec((1, MOD_ROWS, tn), lambda l, n: (l, 0, n)),
        compiler_params=_cparams(2),
        name="ada_mod",
    )(cvec, ada_w, ada_b.reshape(DEPTH, 1, 6 * D))
    return out.reshape(DEPTH * MOD_ROWS, 6, D)


def _tok_spec(width, rows=TM):
    return pl.BlockSpec((rows, width), lambda j, *_: (j, 0))


def _mod_spec():
    return pl.BlockSpec((1, 6, D), lambda j, mr, *_: (mr[j], 0, 0))


def _full_spec(shape):
    nd = len(shape)
    return pl.BlockSpec(shape, lambda j, *_: (0,) * nd)


def _conv_in_kernel(mr_ref, y_ref, mod_ref, g_ref, w_ref, u_ref):
    h = _norm_mod(y_ref[...], g_ref[...], mod_ref[0], 0)
    ag = jnp.dot(h.astype(BF16), w_ref[...], preferred_element_type=F32)
    u_ref[...] = ag[:, :D] * _sigmoid(ag[:, D:])


def _conv_in(y, mods, mrow, g1, w_in):
    return pl.pallas_call(
        _conv_in_kernel,
        out_shape=jax.ShapeDtypeStruct((N_TOK, D), F32),
        grid_spec=pltpu.PrefetchScalarGridSpec(
            num_scalar_prefetch=1, grid=(NB,),
            in_specs=[_tok_spec(D), _mod_spec(), _full_spec((1, D)), _full_spec((D, 2 * D))],
            out_specs=_tok_spec(D)),
        compiler_params=_cparams(),
        name="conv_in",
    )(mrow, y, mods, g1, w_in)


def _conv_main_kernel(mr_ref, first_ref, last_ref, u_ref, up_ref, un_ref, wdw_ref, bdw_ref, lg_ref, lb_ref,
                      wout_ref, y_ref, mod_ref, o_ref, ext_ref, acc_ref):
    j = pl.program_id(0)
    zero = jnp.zeros((HALO, D), F32)
    ext_ref[0:HALO, :] = jnp.where(first_ref[j] == 1, zero, up_ref[...])
    ext_ref[HALO:HALO + SB, :] = u_ref[...]
    ext_ref[HALO + SB:2 * HALO + SB, :] = jnp.where(last_ref[j] == 1, zero, un_ref[...])

    off0 = HALO - CONV_PAD
    n_a = (off0 + CONV_WIDTH - 1) // SUBLANE + 1
    n_chunks = SB // SUBLANE

    def strip(ci, carry):
        cs = pl.ds(pl.multiple_of(ci * LANE, LANE), LANE)
        wk = [jnp.broadcast_to(wdw_ref[k:k + 1, cs], (SUBLANE, LANE)) for k in range(CONV_WIDTH)]
        bias = jnp.broadcast_to(bdw_ref[:, cs], (SUBLANE, LANE))
        sub = lax.broadcasted_iota(I32, (SUBLANE, LANE), 0)
        prev_rot, prev_v0 = None, None
        for j in range(n_chunks + 1):
            tiles = [ext_ref[SUBLANE * (j + a):SUBLANE * (j + a + 1), cs] for a in range(n_a)]
            part = []
            for s in range(SUBLANE):
                acc = None
                for a in range(n_a):
                    k = SUBLANE * a + s - off0
                    if (0 <= k < CONV_WIDTH) and not (s == 0 and j == n_chunks):
                        term = tiles[a] * wk[k]
                        acc = term if acc is None else acc + term
                part.append(acc)
            rot = [None] + [pltpu.roll(part[s], SUBLANE - s, 0) for s in range(1, SUBLANE)]
            if j >= 1:
                out = prev_v0 + bias
                for s in range(1, SUBLANE):
                    out = out + jnp.where(sub < SUBLANE - s, prev_rot[s], rot[s])
                acc_ref[SUBLANE * (j - 1):SUBLANE * j, cs] = out
            prev_rot, prev_v0 = rot, part[0]
        return carry

    lax.fori_loop(0, D // LANE, strip, 0)

    c = acc_ref[...]
    mu = jnp.mean(c, axis=-1, keepdims=True)
    cc = c - mu
    var = jnp.mean(cc * cc, axis=-1, keepdims=True)
    z = cc * lax.rsqrt(var + EPS) * lg_ref[...] + lb_ref[...]
    z = z * _sigmoid(z)
    out = jnp.dot(z.astype(BF16), wout_ref[...], preferred_element_type=F32)
    o_ref[...] = y_ref[...] + mod_ref[0][2:3] * out


def _conv_main(u, y, mods, mrow, w_dw, b_dw, ln_g, ln_b, w_out):
    nh = N_TOK // HALO
    per = SB // HALO
    sb_spec = pl.BlockSpec((SB, D), lambda j, *_: (j, 0))
    return pl.pallas_call(
        _conv_main_kernel,
        out_shape=jax.ShapeDtypeStruct((N_TOK, D), F32),
        grid_spec=pltpu.PrefetchScalarGridSpec(
            num_scalar_prefetch=3, grid=(NSB,),
            in_specs=[
                sb_spec,
                pl.BlockSpec((HALO, D), lambda j, *_: (jnp.maximum(j * per - 1, 0), 0)),
                pl.BlockSpec((HALO, D), lambda j, *_: (jnp.minimum((j + 1) * per, nh - 1), 0)),
                _full_spec((CONV_WIDTH + 1, D)), _full_spec((1, D)), _full_spec((1, D)), _full_spec((1, D)),
                _full_spec((D, D)), sb_spec, _mod_spec(),
            ],
            out_specs=sb_spec,
            scratch_shapes=[pltpu.VMEM((SB + 2 * HALO, D), F32), pltpu.VMEM((SB, D), F32)]),
        compiler_params=_cparams(),
        name="conv_main",
    )(mrow, jnp.asarray(_SEQ_FIRST), jnp.asarray(_SEQ_LAST), u, u, u, w_dw, b_dw, ln_g, ln_b, w_out, y, mods)


def _rope_angles():
    rows = DEC_SEQ // GRID_W
    row = jnp.repeat(jnp.arange(rows, dtype=F32), GRID_W)
    col = jnp.tile(jnp.arange(GRID_W, dtype=F32), rows)
    axis_dim = HEAD_DIM // 2
    freqs = jnp.power(ROPE_THETA, -jnp.arange(axis_dim // 2, dtype=F32) * 2.0 / axis_dim)
    ang_r = row[:, None] * freqs[None, :]
    ang_c = col[:, None] * freqs[None, :]
    return jnp.concatenate([ang_r, ang_r, ang_c, ang_c], axis=-1)


def _rope_blocks():
    ang = _rope_angles()
    cos, sin = jnp.cos(ang), jnp.sin(ang)
    lane = np.arange(HEAD_DIM)
    lo = jnp.asarray(((lane % (HEAD_DIM // 2)) < HEAD_DIM // 4).astype(np.float32))
    sin_a = -sin * lo[None, :]
    sin_b = sin * (1.0 - lo)[None, :]
    nblk = DEC_SEQ // SB
    ident = jnp.ones((1, SB, HEAD_DIM), F32)
    zeros = jnp.zeros((1, SB, HEAD_DIM), F32)
    cos_t = jnp.concatenate([ident, cos.reshape(nblk, SB, HEAD_DIM)], axis=0)
    sa_t = jnp.concatenate([zeros, sin_a.reshape(nblk, SB, HEAD_DIM)], axis=0)
    sb_t = jnp.concatenate([zeros, sin_b.reshape(nblk, SB, HEAD_DIM)], axis=0)
    return cos_t, sa_t, sb_t


def _attn_qkv_kernel(mr_ref, ri_ref, y_ref, mod_ref, g_ref, w_ref, qg_ref, kg_ref, cos_ref, sa_ref, sb_ref,
                     q_ref, kb_ref, vb_ref, kf_ref, vf_ref):
    h = _norm_mod(y_ref[...], g_ref[...], mod_ref[0], 0)
    qkv = jnp.dot(h.astype(BF16), w_ref[...], preferred_element_type=F32)
    cos, sa, sb = cos_ref[0], sa_ref[0], sb_ref[0]
    quarter = HEAD_DIM // 4

    def head(x, g):
        xn = _rms(x, g)
        return xn * cos + pltpu.roll(xn, HEAD_DIM - quarter, 1) * sa + pltpu.roll(xn, quarter, 1) * sb

    scale = HEAD_DIM ** -0.5
    for hd in range(N_HEADS):
        sl = slice(hd * HEAD_DIM, (hd + 1) * HEAD_DIM)
        q_ref[:, sl] = (head(qkv[:, sl], qg_ref[...]) * scale).astype(BF16)
    for kv in range(N_KV_HEADS):
        sl = slice(kv * HEAD_DIM, (kv + 1) * HEAD_DIM)
        kr = head(qkv[:, Q_DIM + kv * HEAD_DIM:Q_DIM + (kv + 1) * HEAD_DIM], kg_ref[...])
        kf_ref[:, sl] = kr
        kb_ref[:, sl] = kr.astype(BF16)
    v = qkv[:, Q_DIM + KV_DIM:]
    vf_ref[...] = v
    vb_ref[...] = v.astype(BF16)


def _attn_qkv(y, mods, mrow, g1, w_qkv, q_g, k_g, rope):
    cos_t, sa_t, sb_t = rope
    rspec = pl.BlockSpec((1, SB, HEAD_DIM), lambda j, mr, ri: (ri[j], 0, 0))
    return pl.pallas_call(
        _attn_qkv_kernel,
        out_shape=(jax.ShapeDtypeStruct((N_TOK, Q_DIM), BF16), jax.ShapeDtypeStruct((N_TOK, KV_DIM), BF16),
                   jax.ShapeDtypeStruct((N_TOK, KV_DIM), BF16), jax.ShapeDtypeStruct((N_TOK, KV_DIM), F32),
                   jax.ShapeDtypeStruct((N_TOK, KV_DIM), F32)),
        grid_spec=pltpu.PrefetchScalarGridSpec(
            num_scalar_prefetch=2, grid=(NSB,),
            in_specs=[_tok_spec(D, SB), _mod_spec(), _full_spec((1, D)), _full_spec((D, QKV_DIM)),
                      _full_spec((1, HEAD_DIM)), _full_spec((1, HEAD_DIM)), rspec, rspec, rspec],
            out_specs=(_tok_spec(Q_DIM, SB), _tok_spec(KV_DIM, SB), _tok_spec(KV_DIM, SB), _tok_spec(KV_DIM, SB),
                       _tok_spec(KV_DIM, SB))),
        compiler_params=_cparams(),
        name="attn_qkv",
    )(mrow, jnp.asarray(_ROPE_IDX_SB), y, mods, g1, w_qkv, q_g, k_g, cos_t, sa_t, sb_t)


def _attn_heads(q, ks, vs, o_scr):
    nt = (((1,), (1,)), ((), ()))
    for hd in range(N_HEADS):
        g = hd // GQA_GROUP
        qh = q[:, hd * HEAD_DIM:(hd + 1) * HEAD_DIM]
        gs = slice(g * HEAD_DIM, (g + 1) * HEAD_DIM)
        ss = [lax.dot_general(qh, k[:, gs], nt, preferred_element_type=F32) for k in ks]
        m = functools.reduce(jnp.maximum, [jnp.max(s, axis=-1, keepdims=True) for s in ss])
        ps = [jnp.exp(s - m) for s in ss]
        l = functools.reduce(lambda a, b: a + b, [jnp.sum(p, axis=-1, keepdims=True) for p in ps])
        o = functools.reduce(lambda a, b: a + b,
                             [jnp.dot(p.astype(BF16), v[:, gs], preferred_element_type=F32) for p, v in zip(ps, vs)])
        o_scr[:, hd * HEAD_DIM:(hd + 1) * HEAD_DIM] = (o / l).astype(BF16)


def _attn_ctx_kernel(q_ref, k_ref, v_ref, wo_ref, y_ref, mod_ref, o_ref, o_scr):
    _attn_heads(q_ref[...], [k_ref[...]], [v_ref[...]], o_scr)
    out = jnp.dot(o_scr[...], wo_ref[...], preferred_element_type=F32)
    o_ref[...] = y_ref[...] + mod_ref[0][2:3] * out


def _attn_lat_kernel(q_ref, k_ref, v_ref, ck_ref, cv_ref, wo_ref, y_ref, mod_ref, ctx_out_ref, o_ref, o_scr):
    del ctx_out_ref
    _attn_heads(q_ref[...], [k_ref[...], ck_ref[0].astype(BF16)], [v_ref[...], cv_ref[0].astype(BF16)], o_scr)
    out = jnp.dot(o_scr[...], wo_ref[...], preferred_element_type=F32)
    o_ref[...] = y_ref[...] + mod_ref[0][2:3] * out


def _attention(q, kb, vb, cache_k, cache_v, w_o, y, mods, layer):
    y_ctx = pl.pallas_call(
        _attn_ctx_kernel,
        out_shape=jax.ShapeDtypeStruct((N_TOK, D), F32),
        grid=(BATCH,),
        in_specs=[
            pl.BlockSpec((SEQ, Q_DIM), lambda s: (s, 0)),
            pl.BlockSpec((SEQ, KV_DIM), lambda s: (s, 0)),
            pl.BlockSpec((SEQ, KV_DIM), lambda s: (s, 0)),
            pl.BlockSpec((Q_DIM, D), lambda s: (0, 0)),
            pl.BlockSpec((SEQ, D), lambda s: (s, 0)),
            pl.BlockSpec((1, 6, D), lambda s: (layer * MOD_ROWS, 0, 0)),
        ],
        out_specs=pl.BlockSpec((SEQ, D), lambda s: (s, 0)),
        scratch_shapes=[pltpu.VMEM((SEQ, Q_DIM), BF16)],
        compiler_params=_cparams(),
        name="attn_ctx",
    )(q, kb, vb, w_o, y, mods)
    pb = NP_TOK // DEC_SEQ
    return pl.pallas_call(
        _attn_lat_kernel,
        out_shape=jax.ShapeDtypeStruct((N_TOK, D), F32),
        input_output_aliases={8: 0},
        grid=(DEC_BATCH, SB_PER_DEC),
        in_specs=[
            pl.BlockSpec((SB, Q_DIM), lambda b, t: (NSBP + b * SB_PER_DEC + t, 0)),
            pl.BlockSpec((DEC_SEQ, KV_DIM), lambda b, t: (pb + b, 0)),
            pl.BlockSpec((DEC_SEQ, KV_DIM), lambda b, t: (pb + b, 0)),
            pl.BlockSpec((1, PAST_LEN, KV_DIM), lambda b, t: (b, 0, 0)),
            pl.BlockSpec((1, PAST_LEN, KV_DIM), lambda b, t: (b, 0, 0)),
            pl.BlockSpec((Q_DIM, D), lambda b, t: (0, 0)),
            pl.BlockSpec((SB, D), lambda b, t: (NSBP + b * SB_PER_DEC + t, 0)),
            pl.BlockSpec((1, 6, D), lambda b, t: (layer * MOD_ROWS + 1 + b, 0, 0)),
            pl.BlockSpec(memory_space=pl.ANY),
        ],
        out_specs=pl.BlockSpec((SB, D), lambda b, t: (NSBP + b * SB_PER_DEC + t, 0)),
        scratch_shapes=[pltpu.VMEM((SB, Q_DIM), BF16)],
        compiler_params=_cparams(2),
        name="attn_lat",
    )(q, kb, vb, cache_k, cache_v, w_o, y, mods, y_ctx)


def _log_sigmoid(x):
    return jnp.minimum(x, 0.0) - jnp.log(1.0 + jnp.exp(-jnp.abs(x)))


def _mlstm_in_kernel(mr_ref, y_ref, mod_ref, g_ref, wf_ref, wg_ref, bg_ref, q_ref, k_ref, v_ref, o_ref, gt_ref,
                     w_ref):
    @pl.when(pl.program_id(0) == 0)
    def _():
        for c in range(4):
            w_ref[:, c * D:(c + 1) * D] = wf_ref[0, :, c * D:(c + 1) * D].astype(BF16)

    h = _norm_mod(y_ref[...], g_ref[...], mod_ref[0], 0)
    hb = h.astype(BF16)
    q_ref[...] = jnp.dot(hb, w_ref[:, 0:D], preferred_element_type=F32).astype(BF16)
    k_ref[...] = (jnp.dot(hb, w_ref[:, D:2 * D], preferred_element_type=F32) * (M_HEAD_DIM ** -0.5)).astype(BF16)
    v_ref[...] = jnp.dot(hb, w_ref[:, 2 * D:3 * D], preferred_element_type=F32).astype(BF16)
    o_ref[...] = _sigmoid(jnp.dot(hb, w_ref[:, 3 * D:4 * D], preferred_element_type=F32))
    gt = _dot_hi_lo(h, wg_ref) + bg_ref[...]
    lane = lax.broadcasted_iota(I32, gt.shape, 1)
    is_f = ((lane >= M_HEADS) & (lane < 2 * M_HEADS)) | ((lane >= 3 * M_HEADS) & (lane < 4 * M_HEADS))
    gt_ref[...] = jnp.where(is_f, _log_sigmoid(gt), gt)


def _mlstm_in(y, mods, mrow, g1, w_in_all, slot, w_gate, b_gate):
    w_spec = pl.BlockSpec((1,) + w_in_all.shape[1:], lambda j, *_: (slot, 0, 0), pipeline_mode=pl.Buffered(1))
    return pl.pallas_call(
        _mlstm_in_kernel,
        out_shape=(jax.ShapeDtypeStruct((N_TOK, D), BF16), jax.ShapeDtypeStruct((N_TOK, D), BF16),
                   jax.ShapeDtypeStruct((N_TOK, D), BF16), jax.ShapeDtypeStruct((N_TOK, D), F32),
                   jax.ShapeDtypeStruct((N_TOK, LANE), F32)),
        grid_spec=pltpu.PrefetchScalarGridSpec(
            num_scalar_prefetch=1, grid=(NB,),
            in_specs=[_tok_spec(D), _mod_spec(), _full_spec((1, D)), w_spec,
                      _full_spec((D, 2 * LANE)), _full_spec((1, LANE))],
            out_specs=(_tok_spec(D), _tok_spec(D), _tok_spec(D), _tok_spec(D), _tok_spec(LANE)),
            scratch_shapes=[pltpu.VMEM((D, 4 * D), BF16)]),
        compiler_params=_cparams(),
        name="mlstm_in",
    )(mrow, y, mods, g1, w_in_all, w_gate, b_gate)


def _mlstm_load(hd, c, q_ref, k_ref, v_ref, gc_ref, gr_ref):
    r0 = pl.multiple_of(c * M_CHUNK, M_CHUNK)
    hs = slice(hd * M_HEAD_DIM, (hd + 1) * M_HEAD_DIM)
    rows = pl.ds(r0, M_CHUNK)
    return rows, hs, q_ref[rows, hs], k_ref[rows, hs], v_ref[rows, hs], gc_ref[hd, rows, :], gr_ref[hd, c]


def _mlstm_chunks(chains, ms, loaded, c_scr, n_scr):
    L = M_CHUNK
    n = range(len(chains))
    t_idx = lax.broadcasted_iota(I32, (L, L), 0)
    s_idx = lax.broadcasted_iota(I32, (L, L), 1)
    masks = {0: (s_idx <= t_idx, t_idx <= s_idx), 1: (s_idx >= t_idx, t_idx >= s_idx)}
    q = [ld[2] for ld in loaded]
    k = [ld[3] for ld in loaded]
    v = [ld[4] for ld in loaded]
    i_col = [ld[5][:, 2 * d:2 * d + 1] for (_, d), ld in zip(chains, loaded)]
    lf_col = [ld[5][:, 2 * d + 1:2 * d + 2] for (_, d), ld in zip(chains, loaded)]
    i_row = [ld[6][2 * d:2 * d + 1, :] for (_, d), ld in zip(chains, loaded)]
    lf_row = [ld[6][2 * d + 1:2 * d + 2, :] for (_, d), ld in zip(chains, loaded)]
    mask = [masks[d][0] for _, d in chains]
    mask_t = [masks[d][1] for _, d in chains]
    b_col = [jnp.sum(jnp.where(mask[i], lf_row[i], 0.0), axis=1, keepdims=True) for i in n]
    b_row = [jnp.sum(jnp.where(mask_t[i], lf_col[i], 0.0), axis=0, keepdims=True) for i in n]
    log_d = [jnp.where(mask[i], b_col[i] - b_row[i] + i_row[i], -jnp.inf) for i in n]
    li = [b_col[i] + ms[i] for i in n]
    m_r = [jnp.maximum(li[i], jnp.max(log_d[i], axis=1, keepdims=True)) for i in n]
    a_int = [jnp.exp(li[i] - m_r[i]) for i in n]
    dmat = [jnp.exp(log_d[i] - m_r[i]) for i in n]
    cmat = [c_scr[d, hd] for hd, d in chains]
    nvec = [n_scr[d, hd] for hd, d in chains]
    gram = [lax.dot_general(q[i], k[i], (((1,), (1,)), ((), ())), preferred_element_type=F32) for i in n]
    inter = [jnp.dot(q[i], cmat[i].astype(BF16), preferred_element_type=F32) for i in n]
    s = [gram[i] * dmat[i] for i in n]
    intra = [jnp.dot(s[i].astype(BF16), v[i], preferred_element_type=F32) for i in n]
    qn = [jnp.sum(q[i].astype(F32) * nvec[i], axis=1, keepdims=True) for i in n]
    den = [a_int[i] * qn[i] + jnp.sum(s[i], axis=1, keepdims=True) for i in n]
    hh = [(a_int[i] * inter[i] + intra[i]) / jnp.maximum(jnp.abs(den[i]), jnp.exp(-m_r[i])) for i in n]
    b_last = [b_row[i][:, L - 1:L] if chains[i][1] == 0 else b_row[i][:, 0:1] for i in n]
    log_w = [b_last[i] - b_col[i] + i_col[i] for i in n]
    m_new = [jnp.maximum(b_last[i] + ms[i], jnp.max(log_w[i], axis=0, keepdims=True)) for i in n]
    w = [jnp.exp(log_w[i] - m_new[i]) for i in n]
    decay = [jnp.exp(b_last[i] + ms[i] - m_new[i]) for i in n]
    kw = [k[i].astype(F32) * w[i] for i in n]
    kv = [lax.dot_general(kw[i].astype(BF16), v[i], (((0,), (0,)), ((), ())), preferred_element_type=F32) for i in n]
    for i, (hd, d) in enumerate(chains):
        c_scr[d, hd] = decay[i] * cmat[i] + kv[i]
        n_scr[d, hd] = decay[i] * nvec[i] + jnp.sum(kw[i], axis=0, keepdims=True)
    return hh, m_new


def _mlstm_scan_body(n_chunks, q_ref, k_ref, v_ref, gc_ref, gr_ref, h_ref, hb_scr, c_scr, n_scr, m0):
    chains = [(hd, d) for hd in range(M_HEADS) for d in range(2)]

    def body(c, ms):
        loaded = [_mlstm_load(hd, c if d == 0 else n_chunks - 1 - c, q_ref, k_ref, v_ref, gc_ref, gr_ref)
                  for hd, d in chains]
        hh, m_new = _mlstm_chunks(chains, ms, loaded, c_scr, n_scr)
        for (hd, d), ld, h in zip(chains, loaded, hh):
            dst = h_ref if d == 0 else hb_scr
            dst[ld[0], ld[1]] = h
        return tuple(m_new)

    ms = lax.fori_loop(0, n_chunks, body, tuple(m0))
    h_ref[...] += hb_scr[...]
    return ms


def _mlstm_scan_ctx_kernel(q_ref, k_ref, v_ref, gc_ref, gr_ref, h_ref, cn_ref, nn_ref, mn_ref, hb_scr, c_scr, n_scr):
    c_scr[...] = jnp.zeros(c_scr.shape, F32)
    n_scr[...] = jnp.zeros(n_scr.shape, F32)
    zero = jnp.zeros((1, 1), F32)
    ms = _mlstm_scan_body(SEQ // M_CHUNK, q_ref, k_ref, v_ref, gc_ref, gr_ref, h_ref, hb_scr, c_scr, n_scr,
                          [zero] * (2 * M_HEADS))
    cn_ref[0] = c_scr[...]
    nn_ref[0] = n_scr[...]
    for hd in range(M_HEADS):
        for d in range(2):
            mn_ref[0, d, hd] = jnp.broadcast_to(ms[2 * hd + d], (1, LANE))


def _mlstm_scan_lat_kernel(q_ref, k_ref, v_ref, gc_ref, gr_ref, c0_ref, n0_ref, m0_ref, ctx_out_ref, h_ref,
                           hb_scr, c_scr, n_scr):
    del ctx_out_ref
    c_scr[...] = c0_ref[0]
    n_scr[...] = n0_ref[0]
    m0 = [m0_ref[0, d, hd] for hd in range(M_HEADS) for d in range(2)]
    _mlstm_scan_body(DEC_SEQ // M_CHUNK, q_ref, k_ref, v_ref, gc_ref, gr_ref, h_ref, hb_scr, c_scr, n_scr, m0)


def _mlstm_scan(q, k, v, gates, state_c, state_n, state_m):
    g16 = gates[:, :4 * M_HEADS].reshape(N_TOK, 4, M_HEADS)
    gcol = jnp.transpose(g16, (2, 0, 1))
    grow = jnp.transpose(g16.reshape(N_TOK // M_CHUNK, M_CHUNK, 4, M_HEADS), (3, 0, 2, 1))
    hd = M_HEAD_DIM
    state_scratch = [pltpu.VMEM((2, M_HEADS, hd, hd), F32), pltpu.VMEM((2, M_HEADS, 1, hd), F32)]
    ncp = SEQ // M_CHUNK
    h_ctx, new_c, new_n, new_m = pl.pallas_call(
        _mlstm_scan_ctx_kernel,
        out_shape=(jax.ShapeDtypeStruct((N_TOK, D), F32),
                   jax.ShapeDtypeStruct((BATCH, 2, M_HEADS, hd, hd), F32),
                   jax.ShapeDtypeStruct((BATCH, 2, M_HEADS, 1, hd), F32),
                   jax.ShapeDtypeStruct((BATCH, 2, M_HEADS, 1, LANE), F32)),
        grid=(BATCH,),
        in_specs=[
            pl.BlockSpec((SEQ, D), lambda s: (s, 0)),
            pl.BlockSpec((SEQ, D), lambda s: (s, 0)),
            pl.BlockSpec((SEQ, D), lambda s: (s, 0)),
            pl.BlockSpec((M_HEADS, SEQ, 4), lambda s: (0, s, 0)),
            pl.BlockSpec((M_HEADS, ncp, 4, M_CHUNK), lambda s: (0, s, 0, 0)),
        ],
        out_specs=(
            pl.BlockSpec((SEQ, D), lambda s: (s, 0)),
            pl.BlockSpec((1, 2, M_HEADS, hd, hd), lambda s: (s, 0, 0, 0, 0)),
            pl.BlockSpec((1, 2, M_HEADS, 1, hd), lambda s: (s, 0, 0, 0, 0)),
            pl.BlockSpec((1, 2, M_HEADS, 1, LANE), lambda s: (s, 0, 0, 0, 0)),
        ),
        scratch_shapes=[pltpu.VMEM((SEQ, D), F32)] + state_scratch,
        compiler_params=_cparams(),
        name="mlstm_scan_ctx",
    )(q, k, v, gcol, grow)
    ncl = DEC_SEQ // M_CHUNK
    pb = NP_TOK // DEC_SEQ
    h_all = pl.pallas_call(
        _mlstm_scan_lat_kernel,
        out_shape=jax.ShapeDtypeStruct((N_TOK, D), F32),
        input_output_aliases={8: 0},
        grid=(DEC_BATCH,),
        in_specs=[
            pl.BlockSpec((DEC_SEQ, D), lambda b: (pb + b, 0)),
            pl.BlockSpec((DEC_SEQ, D), lambda b: (pb + b, 0)),
            pl.BlockSpec((DEC_SEQ, D), lambda b: (pb + b, 0)),
            pl.BlockSpec((M_HEADS, DEC_SEQ, 4), lambda b: (0, pb + b, 0)),
            pl.BlockSpec((M_HEADS, ncl, 4, M_CHUNK), lambda b: (0, pb + b, 0, 0)),
            pl.BlockSpec((1, 2, M_HEADS, hd, hd), lambda b: (b, 0, 0, 0, 0)),
            pl.BlockSpec((1, 2, M_HEADS, 1, hd), lambda b: (b, 0, 0, 0, 0)),
            pl.BlockSpec((1, 2, M_HEADS, 1, 1), lambda b: (b, 0, 0, 0, 0)),
            pl.BlockSpec(memory_space=pl.ANY),
        ],
        out_specs=pl.BlockSpec((DEC_SEQ, D), lambda b: (pb + b, 0)),
        scratch_shapes=[pltpu.VMEM((DEC_SEQ, D), F32)] + state_scratch,
        compiler_params=_cparams(),
        name="mlstm_scan_lat",
    )(q, k, v, gcol, grow, state_c, state_n, state_m, h_ctx)
    return h_all, new_c, new_n, new_m


def _mlstm_out_kernel(mr_ref, h_ref, o_ref, ng_ref, w_ref, y_ref, mod_ref, out_ref, x_scr):
    hc = o_ref[...] * h_ref[...]
    for hd in range(M_HEADS):
        sl = slice(hd * M_HEAD_DIM, (hd + 1) * M_HEAD_DIM)
        x_scr[:, sl] = _rms(hc[:, sl], ng_ref[:, sl]).astype(BF16)
    out = jnp.dot(x_scr[...], w_ref[...], preferred_element_type=F32)
    out_ref[...] = y_ref[...] + mod_ref[0][2:3] * out


def _mlstm_out(hsum, o, norm_g, w_out, y, mods, mrow):
    return pl.pallas_call(
        _mlstm_out_kernel,
        out_shape=jax.ShapeDtypeStruct((N_TOK, D), F32),
        grid_spec=pltpu.PrefetchScalarGridSpec(
            num_scalar_prefetch=1, grid=(NB,),
            in_specs=[_tok_spec(D), _tok_spec(D), _full_spec((1, D)), _full_spec((D, D)), _tok_spec(D), _mod_spec()],
            out_specs=_tok_spec(D),
            scratch_shapes=[pltpu.VMEM((TM, D), BF16)]),
        compiler_params=_cparams(),
        name="mlstm_out",
    )(mrow, hsum, o, norm_g, w_out, y, mods)


ROUTE_OFF = N_GROUPS
SLAB = D // (2 * LANE)
V7X_SC_CORES = 2
V7X_SC_SUBCORES = 16
SC_WORKERS = V7X_SC_CORES * V7X_SC_SUBCORES
SC_WINDOW = 128
HI_MASK = -65536


def _bf16_bits(x):
    return lax.bitcast_convert_type(x.astype(BF16).astype(F32), I32)


def _store_slabs(ref, x):
    rows = x.shape[0]
    for c in range(SLAB):
        lo = lax.shift_right_logical(_bf16_bits(x[:, (2 * c) * LANE:(2 * c + 1) * LANE]), 16)
        hi = _bf16_bits(x[:, (2 * c + 1) * LANE:(2 * c + 2) * LANE]) & HI_MASK
        ref[pl.ds(c, rows, stride=SLAB), :] = lo | hi


def _load_slabs(ref, dst, rows, dtype):
    for c in range(SLAB):
        w = ref[pl.ds(c, rows, stride=SLAB), :]
        lo = lax.bitcast_convert_type(lax.shift_left(w, 16), F32)
        hi = lax.bitcast_convert_type(w & HI_MASK, F32)
        dst[:, (2 * c) * LANE:(2 * c + 1) * LANE] = lo.astype(dtype)
        dst[:, (2 * c + 1) * LANE:(2 * c + 2) * LANE] = hi.astype(dtype)


def _route_kernel(mr_ref, y_ref, mod_ref, g_ref, wr_ref, br_ref, tri_ref, x_ref, wt_ref, meta_ref, cnt_ref, cnt_scr):
    x = _norm_mod(y_ref[...], g_ref[...], mod_ref[0], 1)
    _store_slabs(x_ref, x)
    lg = _dot_hi_lo(x, wr_ref) + br_ref[...]
    lane = lax.broadcasted_iota(I32, lg.shape, 1).astype(F32)
    ninf = -jnp.inf
    big = float(LANE)
    lgg = jnp.where(lane < N_GROUPS, lg, ninf)
    gmax = jnp.max(lgg, axis=-1, keepdims=True)
    g_idx = jnp.min(jnp.where(lgg == gmax, lane, big), axis=-1, keepdims=True)
    g_w = 1.0 / jnp.sum(jnp.exp(lgg - gmax), axis=-1, keepdims=True)
    lo = ROUTE_OFF + g_idx * EXPERTS_PER_GROUP
    le = jnp.where((lane >= lo) & (lane < lo + EXPERTS_PER_GROUP), lg, ninf)
    m1 = jnp.max(le, axis=-1, keepdims=True)
    i1 = jnp.min(jnp.where(le == m1, lane, big), axis=-1, keepdims=True)
    le2 = jnp.where(lane == i1, ninf, le)
    m2 = jnp.max(le2, axis=-1, keepdims=True)
    i2 = jnp.min(jnp.where(le2 == m2, lane, big), axis=-1, keepdims=True)
    r = jnp.exp(m2 - m1)
    p1 = 1.0 / (1.0 + r)
    p2 = r / (1.0 + r)
    two = lax.broadcasted_iota(I32, (x.shape[0], TOP_K), 1)
    wt_ref[...] = jnp.where(two == 0, g_w * p1, g_w * p2)
    @pl.when(pl.program_id(0) == 0)
    def _():
        cnt_scr[...] = jnp.zeros(cnt_scr.shape, F32)

    oh1 = (lane == i1).astype(F32)
    oh2 = (lane == i2).astype(F32)
    both = oh1 + oh2
    before = jnp.dot(tri_ref[...], both.astype(BF16), preferred_element_type=F32) + cnt_scr[...]
    rk1 = jnp.sum(oh1 * before, axis=-1, keepdims=True)
    rk2 = jnp.sum(oh2 * before, axis=-1, keepdims=True)
    cnt_scr[...] = cnt_scr[...] + jnp.sum(both, axis=0, keepdims=True)
    cnt_ref[...] = cnt_scr[...]
    cols = (i1 - ROUTE_OFF, i2 - ROUTE_OFF, rk1, rk2)
    packed = jnp.zeros(lg.shape, F32)
    for c, val in enumerate(cols):
        packed = jnp.where(lane == c, val, packed)
    meta_ref[...] = jnp.transpose(packed)[0:len(cols), :].astype(I32)


def _route(y, mods, mrow, g2, w_route, b_route):
    return pl.pallas_call(
        _route_kernel,
        out_shape=(jax.ShapeDtypeStruct((N_TOK * SLAB, LANE), I32), jax.ShapeDtypeStruct((N_TOK, TOP_K), F32),
                   jax.ShapeDtypeStruct((2 * TOP_K, N_TOK), I32), jax.ShapeDtypeStruct((1, LANE), F32)),
        grid_spec=pltpu.PrefetchScalarGridSpec(
            num_scalar_prefetch=1, grid=(NB,),
            in_specs=[_tok_spec(D), _mod_spec(), _full_spec((1, D)), _full_spec((D, 2 * LANE)), _full_spec((1, LANE)),
                      _full_spec((TM, TM))],
            out_specs=(pl.BlockSpec((TM * SLAB, LANE), lambda j, *_: (j, 0)), _tok_spec(TOP_K),
                       pl.BlockSpec((2 * TOP_K, TM), lambda j, *_: (0, j)), _full_spec((1, LANE))),
            scratch_shapes=[pltpu.VMEM((1, LANE), F32)]),
        compiler_params=_cparams(),
        name="moe_route",
    )(mrow, y, mods, g2, w_route, b_route, jnp.asarray(np.tril(np.ones((TM, TM), np.float32), -1), dtype=BF16))


SLOT_COLS = 2048


def _slot_kernel(meta_ref, ps_ref, o_ref):
    sub = lax.broadcasted_iota(I32, (N_EXPERTS, SLOT_COLS), 0)
    meta = meta_ref[...]
    table = ps_ref[...]
    for k in range(TOP_K):
        start = jnp.sum(jnp.where(sub == meta[k:k + 1, :], table, 0), axis=0, keepdims=True)
        o_ref[k:k + 1, :] = start + meta[TOP_K + k:TOP_K + k + 1, :]


def _slots(meta, pad_start):
    return pl.pallas_call(
        _slot_kernel,
        out_shape=jax.ShapeDtypeStruct((TOP_K, N_TOK), I32),
        grid=(N_TOK // SLOT_COLS,),
        in_specs=[pl.BlockSpec((2 * TOP_K, SLOT_COLS), lambda j: (0, j)), pl.BlockSpec((N_EXPERTS, 1), lambda j: (0, 0))],
        out_specs=pl.BlockSpec((TOP_K, SLOT_COLS), lambda j: (0, j)),
        compiler_params=_cparams(),
        name="moe_slots",
    )(meta, pad_start.astype(I32).reshape(N_EXPERTS, 1))


def _dispatch_tables(meta, lane_counts):
    counts = lane_counts[0, ROUTE_OFF:ROUTE_OFF + N_EXPERTS].astype(I32)
    padded = ((counts + EBLK - 1) // EBLK) * EBLK
    pad_end = jnp.cumsum(padded)
    pad_start = pad_end - padded
    dest = _slots(meta, pad_start)
    n_blk = (padded // EBLK).astype(I32)
    blk_start = (pad_start // EBLK).astype(I32)
    n_used = (pad_end[-1] // EBLK).astype(I32).reshape(1)
    return dest, blk_start, n_blk, n_used


def _sc_mesh():
    return plsc.VectorSubcoreMesh(core_axis_name="core", subcore_axis_name="subcore",
                                  num_cores=V7X_SC_CORES, num_subcores=V7X_SC_SUBCORES)


def _sc_worker():
    return lax.axis_index("core") * V7X_SC_SUBCORES + lax.axis_index("subcore")


def _sc_dispatch(x_slabs, d0, d1):
    per = N_TOK // SC_WORKERS

    @functools.partial(
        pl.kernel, out_type=jax.ShapeDtypeStruct((P_SLOTS, SLAB, LANE), I32), mesh=_sc_mesh(), name="moe_dispatch",
        scratch_types=[pltpu.VMEM((1, per), I32), pltpu.VMEM((1, per), I32), pltpu.VMEM((SC_WINDOW, SLAB, LANE), I32)])
    def run(x_hbm, d0_hbm, d1_hbm, o_hbm, i0_v, i1_v, buf):
        base = _sc_worker() * per
        pltpu.sync_copy(d0_hbm.at[:, pl.ds(base, per)], i0_v)
        pltpu.sync_copy(d1_hbm.at[:, pl.ds(base, per)], i1_v)

        @pl.loop(0, per // SC_WINDOW)
        def _(s):
            off = s * SC_WINDOW
            pltpu.sync_copy(x_hbm.at[pl.ds(base + off, SC_WINDOW)], buf)
            pltpu.sync_copy(buf, o_hbm.at[i0_v.at[0, pl.ds(off, SC_WINDOW)]])
            pltpu.sync_copy(buf, o_hbm.at[i1_v.at[0, pl.ds(off, SC_WINDOW)]])

    return run(x_slabs.reshape(N_TOK, SLAB, LANE), d0, d1)


def _sc_collect(y_slabs, dcat):
    per = N_ASSIGN // SC_WORKERS

    @functools.partial(
        pl.kernel, out_type=jax.ShapeDtypeStruct((N_ASSIGN, SLAB, LANE), I32), mesh=_sc_mesh(), name="moe_collect",
        scratch_types=[pltpu.VMEM((1, per), I32), pltpu.VMEM((SC_WINDOW, SLAB, LANE), I32)])
    def run(y_hbm, i_hbm, o_hbm, i_v, buf):
        base = _sc_worker() * per
        pltpu.sync_copy(i_hbm.at[:, pl.ds(base, per)], i_v)

        @pl.loop(0, per // SC_WINDOW)
        def _(s):
            off = s * SC_WINDOW
            pltpu.sync_copy(y_hbm.at[i_v.at[0, pl.ds(off, SC_WINDOW)]], buf)
            pltpu.sync_copy(buf, o_hbm.at[pl.ds(base + off, SC_WINDOW)])

    return run(y_slabs.reshape(P_SLOTS, SLAB, LANE), dcat)


SC_LANES = 16
PACK_ROWS = 64


def _pack_bf16_pair(a, b):
    def rne(x):
        bits = plsc.bitcast(x, I32)
        return bits + 0x7FFF + (lax.shift_right_logical(bits, 16) & 1)
    return lax.shift_right_logical(rne(a), 16) | (rne(b) & HI_MASK)


def _sc_pack_weights(w, layer, after):
    n_layers, n_exp, n_rows, n_cols = w.shape
    rows = n_exp * n_rows
    half = n_cols // 2
    per = rows // SC_WORKERS

    @functools.partial(
        pl.kernel, out_type=jax.ShapeDtypeStruct((rows, half), I32), mesh=_sc_mesh(), name="moe_wpack",
        compiler_params=pltpu.CompilerParams(needs_layout_passes=False),
        scratch_types=[pltpu.VMEM((PACK_ROWS, n_cols), F32), pltpu.VMEM((PACK_ROWS, half), I32)])
    def run(w_hbm, after_hbm, o_hbm, in_v, out_v):
        del after_hbm
        base = _sc_worker() * per

        @pl.loop(0, per // PACK_ROWS)
        def _(blk):
            r0 = base + blk * PACK_ROWS
            pltpu.sync_copy(w_hbm.at[layer, pl.ds(r0, PACK_ROWS)], in_v)

            @pl.loop(0, PACK_ROWS)
            def _(r):
                @pl.loop(0, half, step=SC_LANES)
                def _(c):
                    a = in_v[r, pl.ds(c, SC_LANES)]
                    b = in_v[r, pl.ds(c + half, SC_LANES)]
                    out_v[r, pl.ds(c, SC_LANES)] = _pack_bf16_pair(a, b)

            pltpu.sync_copy(out_v, o_hbm.at[pl.ds(r0, PACK_ROWS)])

    return run(w.reshape(n_layers, rows, n_cols), after)


EROWS = EBLK * SLAB


W_CHUNKS = 8
W_SLOTS = 2


def _unpack_weight(dst, words):
    half = words.shape[1]
    dst[:, :half] = lax.bitcast_convert_type(lax.shift_left(words, 16), F32).astype(BF16)
    dst[:, half:] = lax.bitcast_convert_type(words & HI_MASK, F32).astype(BF16)


def _expert_kernel(bs_ref, nb_ref, nu_ref, wg_hbm, wu_hbm, wd_hbm, x_hbm, y_hbm,
                   xbuf, ybuf, xs, wg_f, wu_f, wd_f, wg_bf, wu_bf, wd_bf, isem, osem, wsem):
    e = pl.program_id(0)
    n_exp = pl.num_programs(0)
    n_used = nu_ref[0]
    b0 = bs_ref[e]
    nb = nb_ref[e]
    wslot = lax.rem(e, W_SLOTS)

    def weight_copies(ex, slot):
        out = []
        for hbm, buf in ((wg_hbm, wg_f), (wu_hbm, wu_f), (wd_hbm, wd_f)):
            n_rows = buf.shape[1]
            rows = n_rows // W_CHUNKS
            for c in range(W_CHUNKS):
                src = pl.ds(pl.multiple_of(ex * n_rows + c * rows, rows), rows)
                out.append(pltpu.make_async_copy(hbm.at[src], buf.at[slot, pl.ds(c * rows, rows)], wsem.at[slot]))
        return out

    def start_weights(ex, slot):
        for i, cp in enumerate(weight_copies(ex, slot)):
            cp.start(priority=i % 2)

    ahead = W_SLOTS - 1

    @pl.when(e == 0)
    def _():
        for ex in range(ahead):
            start_weights(ex, ex)

    for cp in weight_copies(e, wslot):
        cp.wait()

    @pl.when(e + ahead < n_exp)
    def _():
        start_weights(e + ahead, lax.rem(e + ahead, W_SLOTS))

    def in_copy(g, slot):
        return pltpu.make_async_copy(x_hbm.at[pl.ds(pl.multiple_of(g * EROWS, EROWS), EROWS)], xbuf.at[slot],
                                     isem.at[slot])

    def out_copy(g, slot):
        return pltpu.make_async_copy(ybuf.at[slot], y_hbm.at[pl.ds(pl.multiple_of(g * EROWS, EROWS), EROWS)],
                                     osem.at[slot])

    @pl.when(e == 0)
    def _():
        in_copy(0, 0).start(priority=1)

    @pl.when(nb > 0)
    def _():
        _unpack_weight(wg_bf, wg_f[wslot])
        _unpack_weight(wu_bf, wu_f[wslot])
        _unpack_weight(wd_bf, wd_f[wslot])

    def block(k, carry):
        g = b0 + k
        slot = lax.rem(g, 2)
        in_copy(g, slot).wait()

        @pl.when(g + 1 < n_used)
        def _():
            in_copy(g + 1, 1 - slot).start(priority=1)

        _load_slabs(xbuf.at[slot], xs, EBLK, BF16)
        xb = xs[...]
        gt = jnp.dot(xb, wg_bf[...], preferred_element_type=F32)
        up = jnp.dot(xb, wu_bf[...], preferred_element_type=F32)
        hmid = (gt * _sigmoid(gt) * up).astype(BF16)
        res = jnp.dot(hmid, wd_bf[...], preferred_element_type=F32)

        @pl.when(g >= 2)
        def _():
            out_copy(g - 2, slot).wait()

        _store_slabs(ybuf.at[slot], res)
        out_copy(g, slot).start()
        return carry

    lax.fori_loop(0, nb, block, 0)

    @pl.when(e == n_exp - 1)
    def _():
        last = n_used - 1
        out_copy(last, lax.rem(last, 2)).wait()

        @pl.when(n_used >= 2)
        def _():
            out_copy(last - 1, lax.rem(last - 1, 2)).wait()


def _experts(x_sorted, blk_start, n_blk, n_used, w_gate, w_up, w_down):
    any_spec = pl.BlockSpec(memory_space=pl.ANY)
    return pl.pallas_call(
        _expert_kernel,
        out_shape=jax.ShapeDtypeStruct((P_SLOTS * SLAB, LANE), I32),
        grid_spec=pltpu.PrefetchScalarGridSpec(
            num_scalar_prefetch=3, grid=(N_EXPERTS,),
            in_specs=[any_spec, any_spec, any_spec, any_spec],
            out_specs=any_spec,
            scratch_shapes=[
                pltpu.VMEM((2, EROWS, LANE), I32), pltpu.VMEM((2, EROWS, LANE), I32),
                pltpu.VMEM((EBLK, D), BF16),
                pltpu.VMEM((W_SLOTS, D, D_EXPERT // 2), I32), pltpu.VMEM((W_SLOTS, D, D_EXPERT // 2), I32),
                pltpu.VMEM((W_SLOTS, D_EXPERT, D // 2), I32),
                pltpu.VMEM((D, D_EXPERT), BF16), pltpu.VMEM((D, D_EXPERT), BF16), pltpu.VMEM((D_EXPERT, D), BF16),
                pltpu.SemaphoreType.DMA((2,)), pltpu.SemaphoreType.DMA((2,)), pltpu.SemaphoreType.DMA((W_SLOTS,)),
            ]),
        compiler_params=_cparams(),
        name="moe_experts",
    )(blk_start, n_blk, n_used, w_gate, w_up, w_down, x_sorted.reshape(P_SLOTS * SLAB, LANE))


def _combine_kernel(final, mr_ref, e0_ref, e1_ref, wt_ref, y_ref, mod_ref, fg_ref, o_ref, a_scr, b_scr):
    _load_slabs(e0_ref, a_scr, TM, F32)
    _load_slabs(e1_ref, b_scr, TM, F32)
    wt = wt_ref[...]
    moe = wt[:, 0:1] * a_scr[...] + wt[:, 1:2] * b_scr[...]
    y_new = y_ref[...] + mod_ref[0][5:6] * moe
    o_ref[...] = _rms(y_new, fg_ref[...]) if final else y_new


def _combine(ym, wts, y, mods, mrow, final_g, blk0, nblk, final):
    tok = lambda width: pl.BlockSpec((TM, width), lambda j, *_: (blk0 + j, 0))
    slab0 = pl.BlockSpec((TM * SLAB, LANE), lambda j, *_: (blk0 + j, 0))
    slab1 = pl.BlockSpec((TM * SLAB, LANE), lambda j, *_: (NB + blk0 + j, 0))
    mod = pl.BlockSpec((1, 6, D), lambda j, mr: (mr[blk0 + j], 0, 0))
    return pl.pallas_call(
        functools.partial(_combine_kernel, final),
        out_shape=jax.ShapeDtypeStruct((nblk * TM, D), F32),
        grid_spec=pltpu.PrefetchScalarGridSpec(
            num_scalar_prefetch=1, grid=(nblk,),
            in_specs=[slab0, slab1, tok(TOP_K), tok(D), mod, _full_spec((1, D))],
            out_specs=pl.BlockSpec((TM, D), lambda j, *_: (j, 0)),
            scratch_shapes=[pltpu.VMEM((TM, D), F32), pltpu.VMEM((TM, D), F32)]),
        compiler_params=_cparams(),
        name="moe_combine",
    )(mrow, ym, ym, wts, y, mods, final_g)


def kernel(x_prompt, x_sample, cache_attn_k, cache_attn_v, state_mlstm_C, state_mlstm_n, state_mlstm_m, c, c_ctx, ada_w, ada_b, norm1_g, norm2_g, conv_w_in, conv_w_dw, conv_b_dw, conv_ln_g, conv_ln_b, conv_w_out, attn_w_qkv, attn_q_norm, attn_k_norm, attn_w_o, mlstm_w_in, mlstm_b_gate, mlstm_norm_g, mlstm_w_out, moe_w_group, moe_b_group, moe_w_router, moe_b_router, moe_w_gate, moe_w_up, moe_w_down, final_norm_g):
    y = jnp.concatenate([x_prompt.reshape(NP_TOK, D), x_sample.reshape(NS_TOK, D)], axis=0)
    cvec = jnp.concatenate([c_ctx[None, :], c, jnp.zeros((MOD_ROWS - 1 - DEC_BATCH, D), F32)], axis=0)
    mods = _ada_all(cvec, ada_w, ada_b)
    rope = _rope_blocks()
    expert_w = (moe_w_gate, moe_w_up, moe_w_down)
    packed_w = tuple(_sc_pack_weights(w, 0, c) for w in expert_w)
    new_k = new_v = new_c = new_n = new_m = None
    for i in range(DEPTH):
        kind, slot = i % 3, i // 3
        mrow = jnp.asarray(_MOD_ROW + i * MOD_ROWS)
        mrow_sb = jnp.asarray(_MOD_ROW_SB + i * MOD_ROWS)
        g1 = norm1_g[i].reshape(1, D)
        if kind == 0:
            u = _conv_in(y, mods, mrow, g1, conv_w_in[slot].astype(BF16))
            w_dw = jnp.concatenate([conv_w_dw[slot], jnp.zeros((1, D), F32)], axis=0)
            y = _conv_main(u, y, mods, mrow_sb, w_dw, conv_b_dw[slot].reshape(1, D), conv_ln_g[slot].reshape(1, D),
                           conv_ln_b[slot].reshape(1, D), conv_w_out[slot].astype(BF16))
        elif kind == 1:
            q, kb, vb, kf, vf = _attn_qkv(y, mods, mrow_sb, g1, attn_w_qkv[slot].astype(BF16),
                                          attn_q_norm[slot].reshape(1, HEAD_DIM), attn_k_norm[slot].reshape(1, HEAD_DIM),
                                          rope)
            new_k = kf[:NP_TOK].reshape(BATCH, 1, SEQ, N_KV_HEADS, HEAD_DIM)
            new_v = vf[:NP_TOK].reshape(BATCH, 1, SEQ, N_KV_HEADS, HEAD_DIM)
            ck = cache_attn_k[:, slot].reshape(DEC_BATCH, PAST_LEN, KV_DIM)
            cv = cache_attn_v[:, slot].reshape(DEC_BATCH, PAST_LEN, KV_DIM)
            y = _attention(q, kb, vb, ck, cv, attn_w_o[slot].astype(BF16), y, mods, i)
        else:
            w_in = mlstm_w_in[slot]
            w_gate = jnp.concatenate([w_in[:, 4 * D:], jnp.zeros((D, LANE - 4 * M_HEADS), F32)], axis=1)
            b_gate = jnp.concatenate([mlstm_b_gate[slot], jnp.zeros((LANE - 4 * M_HEADS,), F32)]).reshape(1, LANE)
            q, k, v, o, gates = _mlstm_in(y, mods, mrow, g1, mlstm_w_in, slot, _split_hi_lo(w_gate), b_gate)
            sc = state_mlstm_C[:, slot]
            sn = state_mlstm_n[:, slot].reshape(DEC_BATCH, 2, M_HEADS, 1, M_HEAD_DIM)
            sm = state_mlstm_m[:, slot].reshape(DEC_BATCH, 2, M_HEADS, 1, 1)
            hsum, nc_, nn_, nm_ = _mlstm_scan(q, k, v, gates, sc, sn, sm)
            new_c = nc_[:, None]
            new_n = nn_.reshape(BATCH, 1, 2, M_HEADS, M_HEAD_DIM)
            new_m = nm_[..., 0, 0].reshape(BATCH, 1, 2, M_HEADS)
            y = _mlstm_out(hsum, o, mlstm_norm_g[slot].reshape(1, D), mlstm_w_out[slot].astype(BF16), y, mods, mrow)
        w_route = jnp.concatenate([moe_w_group[i], moe_w_router[i],
                                   jnp.zeros((D, LANE - N_GROUPS - N_EXPERTS), F32)], axis=1)
        b_route = jnp.concatenate([moe_b_group[i], moe_b_router[i],
                                   jnp.zeros((LANE - N_GROUPS - N_EXPERTS,), F32)]).reshape(1, LANE)
        x2, ewt, meta, cnt = _route(y, mods, mrow, norm2_g[i].reshape(1, D), _split_hi_lo(w_route), b_route)
        dest, blk_start, n_blk, n_used = _dispatch_tables(meta, cnt)
        x_sorted = _sc_dispatch(x2, dest[0:1], dest[1:2])
        y_sorted = _experts(x_sorted, blk_start, n_blk, n_used, *packed_w)
        ym = _sc_collect(y_sorted, dest.reshape(1, N_ASSIGN))
        if i + 1 < DEPTH:
            packed_w = tuple(_sc_pack_weights(w, i + 1, ym) for w in expert_w)
        ym = ym.reshape(N_ASSIGN * SLAB, LANE)
        fg = final_norm_g.reshape(1, D)
        if i + 1 < DEPTH:
            y = _combine(ym, ewt, y, mods, mrow, fg, 0, NB, False)
        else:
            y_prompt = _combine(ym, ewt, y, mods, mrow, fg, 0, NBP, True).reshape(BATCH, SEQ, D)
            y_sample = _combine(ym, ewt, y, mods, mrow, fg, NBP, NB - NBP, True).reshape(DEC_BATCH, DEC_SEQ, D)
    return (y_prompt, y_sample, new_k, new_v, new_c, new_n, new_m)
```

```python
import functools

import jax
import jax.numpy as jnp
import numpy as np
from jax import lax
from jax.experimental import pallas as pl
from jax.experimental.pallas import tpu as pltpu
from jax.experimental.pallas import tpu_sc as plsc

F32 = jnp.float32
BF16 = jnp.bfloat16
I32 = jnp.int32

D = 1024
BATCH, SEQ = 16, 256
DEC_BATCH, DEC_SEQ = 8, 1024
PAST_LEN = 256
DEPTH = 4
GRID_W = 64
EPS = 1e-6
CONV_WIDTH = 31
CONV_PAD = CONV_WIDTH // 2
HEAD_DIM = 128
N_HEADS = 8
N_KV_HEADS = 2
GQA_GROUP = N_HEADS // N_KV_HEADS
Q_DIM = N_HEADS * HEAD_DIM
KV_DIM = N_KV_HEADS * HEAD_DIM
QKV_DIM = Q_DIM + 2 * KV_DIM
ROPE_THETA = 10000.0
M_HEADS = 4
M_HEAD_DIM = D // M_HEADS
M_CHUNK = 64
N_GROUPS = 4
EXPERTS_PER_GROUP = 8
N_EXPERTS = N_GROUPS * EXPERTS_PER_GROUP
TOP_K = 2
D_EXPERT = 512

NP_TOK = BATCH * SEQ
NS_TOK = DEC_BATCH * DEC_SEQ
N_TOK = NP_TOK + NS_TOK
TM = 512
NB = N_TOK // TM
NBP = NP_TOK // TM
BLK_PER_DEC = DEC_SEQ // TM
TL = 1024
NL = N_TOK // TL
NLP = NP_TOK // TL
SB = 256
NSB = N_TOK // SB
NSBP = NP_TOK // SB
SB_PER_DEC = DEC_SEQ // SB
MOD_ROWS = 16
HALO = 16
LANE = 128
SUBLANE = 8

N_ASSIGN = N_TOK * TOP_K
EBLK = 256
N_EBLK = N_ASSIGN // EBLK + N_EXPERTS
P_SLOTS = N_EBLK * EBLK
N_PAD_SLOTS = P_SLOTS - N_ASSIGN

VMEM_LIMIT = 56 * 1024 * 1024


def _block_tables(nb, nbp, per_dec):
    j = np.arange(nb)
    is_p = j < nbp
    mod_row = np.where(is_p, 0, 1 + (j - nbp) // per_dec)
    rope_idx = np.where(is_p, 0, 1 + (j - nbp) % per_dec)
    first = np.where(is_p, 1, ((j - nbp) % per_dec == 0).astype(np.int64))
    last = np.where(is_p, 1, ((j - nbp) % per_dec == per_dec - 1).astype(np.int64))
    return (mod_row.astype(np.int32), rope_idx.astype(np.int32), first.astype(np.int32), last.astype(np.int32))


_MOD_ROW, _, _, _ = _block_tables(NB, NBP, BLK_PER_DEC)
_MOD_ROW_L, _, _, _ = _block_tables(NL, NLP, DEC_SEQ // TL)
_MOD_ROW_SB, _ROPE_IDX_SB, _SEQ_FIRST, _SEQ_LAST = _block_tables(NSB, NSBP, SB_PER_DEC)


def _cparams(n_axes=1):
    return pltpu.CompilerParams(dimension_semantics=("arbitrary",) * n_axes, vmem_limit_bytes=VMEM_LIMIT)


def _sigmoid(x):
    return 1.0 / (1.0 + jnp.exp(-x))


def _rms(x, g):
    return x * lax.rsqrt(jnp.mean(x * x, axis=-1, keepdims=True) + EPS) * g


def _norm_mod(y, g, mod, which):
    shift = mod[3 * which:3 * which + 1]
    scale = mod[3 * which + 1:3 * which + 2]
    return _rms(y, g) * (1.0 + scale) + shift


def _ada_kernel(c_ref, w_ref, b_ref, o_ref):
    c = c_ref[...]
    s = c * _sigmoid(c)
    o_ref[0] = jnp.dot(s.astype(BF16), w_ref[0].astype(BF16), preferred_element_type=F32) + b_ref[0]


def _ada_all(cvec, ada_w, ada_b):
    tn = 1536
    out = pl.pallas_call(
        _ada_kernel,
        out_shape=jax.ShapeDtypeStruct((DEPTH, MOD_ROWS, 6 * D), F32),
        grid=(DEPTH, 6 * D // tn),
        in_specs=[
            pl.BlockSpec((MOD_ROWS, D), lambda l, n: (0, 0)),
            pl.BlockSpec((1, D, tn), lambda l, n: (l, 0, n)),
            pl.BlockSpec((1, 1, tn), lambda l, n: (l, 0, n)),
        ],
        out_specs=pl.BlockSpec((1, MOD_ROWS, tn), lambda l, n: (l, 0, n)),
        compiler_params=_cparams(2),
        name="ada_mod",
    )(cvec, ada_w, ada_b.reshape(DEPTH, 1, 6 * D))
    return out.reshape(DEPTH * MOD_ROWS, 6, D)


def _tok_spec(width, rows=TM):
    return pl.BlockSpec((rows, width), lambda j, *_: (j, 0))


def _mod_spec():
    return pl.BlockSpec((1, 6, D), lambda j, mr, *_: (mr[j], 0, 0))


def _full_spec(shape):
    nd = len(shape)
    return pl.BlockSpec(shape, lambda j, *_: (0,) * nd)


def _conv_in_kernel(mr_ref, y_ref, mod_ref, g_ref, w_ref, u_ref):
    h = _norm_mod(y_ref[...], g_ref[...], mod_ref[0], 0)
    ag = jnp.dot(h.astype(BF16), w_ref[...], preferred_element_type=F32)
    u_ref[...] = ag[:, :D] * _sigmoid(ag[:, D:])


def _conv_in(y, mods, mrow, g1, w_in):
    return pl.pallas_call(
        _conv_in_kernel,
        out_shape=jax.ShapeDtypeStruct((N_TOK, D), F32),
        grid_spec=pltpu.PrefetchScalarGridSpec(
            num_scalar_prefetch=1, grid=(NL,),
            in_specs=[_tok_spec(D, TL), _mod_spec(), _full_spec((1, D)), _full_spec((D, 2 * D))],
            out_specs=_tok_spec(D, TL)),
        compiler_params=_cparams(),
        name="conv_in",
    )(mrow, y, mods, g1, w_in)


def _conv_main_kernel(mr_ref, first_ref, last_ref, u_ref, up_ref, un_ref, wdw_ref, bdw_ref, lg_ref, lb_ref,
                      wout_ref, y_ref, mod_ref, o_ref, ext_ref, acc_ref):
    j = pl.program_id(0)
    zero = jnp.zeros((HALO, D), F32)
    ext_ref[0:HALO, :] = jnp.where(first_ref[j] == 1, zero, up_ref[...])
    ext_ref[HALO:HALO + SB, :] = u_ref[...]
    ext_ref[HALO + SB:2 * HALO + SB, :] = jnp.where(last_ref[j] == 1, zero, un_ref[...])

    off0 = HALO - CONV_PAD
    n_a = (off0 + CONV_WIDTH - 1) // SUBLANE + 1
    n_chunks = SB // SUBLANE

    def strip(ci, carry):
        cs = pl.ds(pl.multiple_of(ci * LANE, LANE), LANE)
        wk = [jnp.broadcast_to(wdw_ref[k:k + 1, cs], (SUBLANE, LANE)) for k in range(CONV_WIDTH)]
        bias = jnp.broadcast_to(bdw_ref[:, cs], (SUBLANE, LANE))
        sub = lax.broadcasted_iota(I32, (SUBLANE, LANE), 0)
        prev_rot, prev_v0 = None, None
        for j in range(n_chunks + 1):
            tiles = [ext_ref[SUBLANE * (j + a):SUBLANE * (j + a + 1), cs] for a in range(n_a)]
            part = []
            for s in range(SUBLANE):
                acc = None
                for a in range(n_a):
                    k = SUBLANE * a + s - off0
                    if (0 <= k < CONV_WIDTH) and not (s == 0 and j == n_chunks):
                        term = tiles[a] * wk[k]
                        acc = term if acc is None else acc + term
                part.append(acc)
            rot = [None] + [pltpu.roll(part[s], SUBLANE - s, 0) for s in range(1, SUBLANE)]
            if j >= 1:
                out = prev_v0 + bias
                for s in range(1, SUBLANE):
                    out = out + jnp.where(sub < SUBLANE - s, prev_rot[s], rot[s])
                acc_ref[SUBLANE * (j - 1):SUBLANE * j, cs] = out
            prev_rot, prev_v0 = rot, part[0]
        return carry

    lax.fori_loop(0, D // LANE, strip, 0)

    c = acc_ref[...]
    mu = jnp.mean(c, axis=-1, keepdims=True)
    cc = c - mu
    var = jnp.mean(cc * cc, axis=-1, keepdims=True)
    z = cc * lax.rsqrt(var + EPS) * lg_ref[...] + lb_ref[...]
    z = z * _sigmoid(z)
    out = jnp.dot(z.astype(BF16), wout_ref[...], preferred_element_type=F32)
    o_ref[...] = y_ref[...] + mod_ref[0][2:3] * out


def _conv_main(u, y, mods, mrow, w_dw, b_dw, ln_g, ln_b, w_out):
    nh = N_TOK // HALO
    per = SB // HALO
    sb_spec = pl.BlockSpec((SB, D), lambda j, *_: (j, 0))
    return pl.pallas_call(
        _conv_main_kernel,
        out_shape=jax.ShapeDtypeStruct((N_TOK, D), F32),
        grid_spec=pltpu.PrefetchScalarGridSpec(
            num_scalar_prefetch=3, grid=(NSB,),
            in_specs=[
                sb_spec,
                pl.BlockSpec((HALO, D), lambda j, *_: (jnp.maximum(j * per - 1, 0), 0)),
                pl.BlockSpec((HALO, D), lambda j, *_: (jnp.minimum((j + 1) * per, nh - 1), 0)),
                _full_spec((CONV_WIDTH + 1, D)), _full_spec((1, D)), _full_spec((1, D)), _full_spec((1, D)),
                _full_spec((D, D)), sb_spec, _mod_spec(),
            ],
            out_specs=sb_spec,
            scratch_shapes=[pltpu.VMEM((SB + 2 * HALO, D), F32), pltpu.VMEM((SB, D), F32)]),
        compiler_params=_cparams(),
        name="conv_main",
    )(mrow, jnp.asarray(_SEQ_FIRST), jnp.asarray(_SEQ_LAST), u, u, u, w_dw, b_dw, ln_g, ln_b, w_out, y, mods)


def _rope_angles():
    rows = DEC_SEQ // GRID_W
    row = jnp.repeat(jnp.arange(rows, dtype=F32), GRID_W)
    col = jnp.tile(jnp.arange(GRID_W, dtype=F32), rows)
    axis_dim = HEAD_DIM // 2
    freqs = jnp.power(ROPE_THETA, -jnp.arange(axis_dim // 2, dtype=F32) * 2.0 / axis_dim)
    ang_r = row[:, None] * freqs[None, :]
    ang_c = col[:, None] * freqs[None, :]
    return jnp.concatenate([ang_r, ang_r, ang_c, ang_c], axis=-1)


def _rope_blocks():
    ang = _rope_angles()
    cos, sin = jnp.cos(ang), jnp.sin(ang)
    lane = np.arange(HEAD_DIM)
    lo = jnp.asarray(((lane % (HEAD_DIM // 2)) < HEAD_DIM // 4).astype(np.float32))
    sin_a = -sin * lo[None, :]
    sin_b = sin * (1.0 - lo)[None, :]
    nblk = DEC_SEQ // SB
    ident = jnp.ones((1, SB, HEAD_DIM), F32)
    zeros = jnp.zeros((1, SB, HEAD_DIM), F32)
    cos_t = jnp.concatenate([ident, cos.reshape(nblk, SB, HEAD_DIM)], axis=0)
    sa_t = jnp.concatenate([zeros, sin_a.reshape(nblk, SB, HEAD_DIM)], axis=0)
    sb_t = jnp.concatenate([zeros, sin_b.reshape(nblk, SB, HEAD_DIM)], axis=0)
    return cos_t, sa_t, sb_t


def _attn_qkv_kernel(mr_ref, ri_ref, y_ref, mod_ref, g_ref, w_ref, qg_ref, kg_ref, cos_ref, sa_ref, sb_ref,
                     q_ref, kb_ref, vb_ref, kf_ref, vf_ref):
    h = _norm_mod(y_ref[...], g_ref[...], mod_ref[0], 0)
    qkv = jnp.dot(h.astype(BF16), w_ref[...], preferred_element_type=F32)
    cos, sa, sb = cos_ref[0], sa_ref[0], sb_ref[0]
    quarter = HEAD_DIM // 4

    def head(x, g):
        xn = _rms(x, g)
        return xn * cos + pltpu.roll(xn, HEAD_DIM - quarter, 1) * sa + pltpu.roll(xn, quarter, 1) * sb

    scale = HEAD_DIM ** -0.5
    for hd in range(N_HEADS):
        sl = slice(hd * HEAD_DIM, (hd + 1) * HEAD_DIM)
        q_ref[:, sl] = (head(qkv[:, sl], qg_ref[...]) * scale).astype(BF16)
    for kv in range(N_KV_HEADS):
        sl = slice(kv * HEAD_DIM, (kv + 1) * HEAD_DIM)
        kr = head(qkv[:, Q_DIM + kv * HEAD_DIM:Q_DIM + (kv + 1) * HEAD_DIM], kg_ref[...])
        kf_ref[:, sl] = kr
        kb_ref[:, sl] = kr.astype(BF16)
    v = qkv[:, Q_DIM + KV_DIM:]
    vf_ref[...] = v
    vb_ref[...] = v.astype(BF16)


def _attn_qkv(y, mods, mrow, g1, w_qkv, q_g, k_g, rope):
    cos_t, sa_t, sb_t = rope
    rspec = pl.BlockSpec((1, SB, HEAD_DIM), lambda j, mr, ri: (ri[j], 0, 0))
    return pl.pallas_call(
        _attn_qkv_kernel,
        out_shape=(jax.ShapeDtypeStruct((N_TOK, Q_DIM), BF16), jax.ShapeDtypeStruct((N_TOK, KV_DIM), BF16),
                   jax.ShapeDtypeStruct((N_TOK, KV_DIM), BF16), jax.ShapeDtypeStruct((N_TOK, KV_DIM), F32),
                   jax.ShapeDtypeStruct((N_TOK, KV_DIM), F32)),
        grid_spec=pltpu.PrefetchScalarGridSpec(
            num_scalar_prefetch=2, grid=(NSB,),
            in_specs=[_tok_spec(D, SB), _mod_spec(), _full_spec((1, D)), _full_spec((D, QKV_DIM)),
                      _full_spec((1, HEAD_DIM)), _full_spec((1, HEAD_DIM)), rspec, rspec, rspec],
            out_specs=(_tok_spec(Q_DIM, SB), _tok_spec(KV_DIM, SB), _tok_spec(KV_DIM, SB), _tok_spec(KV_DIM, SB),
                       _tok_spec(KV_DIM, SB))),
        compiler_params=_cparams(),
        name="attn_qkv",
    )(mrow, jnp.asarray(_ROPE_IDX_SB), y, mods, g1, w_qkv, q_g, k_g, cos_t, sa_t, sb_t)


def _attn_heads(q, ks, vs, o_scr):
    nt = (((1,), (1,)), ((), ()))
    for hd in range(N_HEADS):
        g = hd // GQA_GROUP
        qh = q[:, hd * HEAD_DIM:(hd + 1) * HEAD_DIM]
        gs = slice(g * HEAD_DIM, (g + 1) * HEAD_DIM)
        ss = [lax.dot_general(qh, k[:, gs], nt, preferred_element_type=F32) for k in ks]
        m = functools.reduce(jnp.maximum, [jnp.max(s, axis=-1, keepdims=True) for s in ss])
        ps = [jnp.exp(s - m) for s in ss]
        l = functools.reduce(lambda a, b: a + b, [jnp.sum(p, axis=-1, keepdims=True) for p in ps])
        o = functools.reduce(lambda a, b: a + b,
                             [jnp.dot(p.astype(BF16), v[:, gs], preferred_element_type=F32) for p, v in zip(ps, vs)])
        o_scr[:, hd * HEAD_DIM:(hd + 1) * HEAD_DIM] = (o / l).astype(BF16)


def _attn_ctx_kernel(q_ref, k_ref, v_ref, wo_ref, y_ref, mod_ref, o_ref, o_scr):
    _attn_heads(q_ref[...], [k_ref[...]], [v_ref[...]], o_scr)
    out = jnp.dot(o_scr[...], wo_ref[...], preferred_element_type=F32)
    o_ref[...] = y_ref[...] + mod_ref[0][2:3] * out


def _attn_lat_kernel(q_ref, k_ref, v_ref, ck_ref, cv_ref, wo_ref, y_ref, mod_ref, ctx_out_ref, o_ref, o_scr):
    del ctx_out_ref
    _attn_heads(q_ref[...], [k_ref[...], ck_ref[0].astype(BF16)], [v_ref[...], cv_ref[0].astype(BF16)], o_scr)
    out = jnp.dot(o_scr[...], wo_ref[...], preferred_element_type=F32)
    o_ref[...] = y_ref[...] + mod_ref[0][2:3] * out


def _attention(q, kb, vb, cache_k, cache_v, w_o, y, mods, layer):
    y_ctx = pl.pallas_call(
        _attn_ctx_kernel,
        out_shape=jax.ShapeDtypeStruct((N_TOK, D), F32),
        grid=(BATCH,),
        in_specs=[
            pl.BlockSpec((SEQ, Q_DIM), lambda s: (s, 0)),
            pl.BlockSpec((SEQ, KV_DIM), lambda s: (s, 0)),
            pl.BlockSpec((SEQ, KV_DIM), lambda s: (s, 0)),
            pl.BlockSpec((Q_DIM, D), lambda s: (0, 0)),
            pl.BlockSpec((SEQ, D), lambda s: (s, 0)),
            pl.BlockSpec((1, 6, D), lambda s: (layer * MOD_ROWS, 0, 0)),
        ],
        out_specs=pl.BlockSpec((SEQ, D), lambda s: (s, 0)),
        scratch_shapes=[pltpu.VMEM((SEQ, Q_DIM), BF16)],
        compiler_params=_cparams(),
        name="attn_ctx",
    )(q, kb, vb, w_o, y, mods)
    pb = NP_TOK // DEC_SEQ
    return pl.pallas_call(
        _attn_lat_kernel,
        out_shape=jax.ShapeDtypeStruct((N_TOK, D), F32),
        input_output_aliases={8: 0},
        grid=(DEC_BATCH, SB_PER_DEC),
        in_specs=[
            pl.BlockSpec((SB, Q_DIM), lambda b, t: (NSBP + b * SB_PER_DEC + t, 0)),
            pl.BlockSpec((DEC_SEQ, KV_DIM), lambda b, t: (pb + b, 0)),
            pl.BlockSpec((DEC_SEQ, KV_DIM), lambda b, t: (pb + b, 0)),
            pl.BlockSpec((1, PAST_LEN, KV_DIM), lambda b, t: (b, 0, 0)),
            pl.BlockSpec((1, PAST_LEN, KV_DIM), lambda b, t: (b, 0, 0)),
            pl.BlockSpec((Q_DIM, D), lambda b, t: (0, 0)),
            pl.BlockSpec((SB, D), lambda b, t: (NSBP + b * SB_PER_DEC + t, 0)),
            pl.BlockSpec((1, 6, D), lambda b, t: (layer * MOD_ROWS + 1 + b, 0, 0)),
            pl.BlockSpec(memory_space=pl.ANY),
        ],
        out_specs=pl.BlockSpec((SB, D), lambda b, t: (NSBP + b * SB_PER_DEC + t, 0)),
        scratch_shapes=[pltpu.VMEM((SB, Q_DIM), BF16)],
        compiler_params=_cparams(2),
        name="attn_lat",
    )(q, kb, vb, cache_k, cache_v, w_o, y, mods, y_ctx)


def _log_sigmoid(x):
    return jnp.minimum(x, 0.0) - jnp.log(1.0 + jnp.exp(-jnp.abs(x)))


def _mlstm_in_kernel(mr_ref, y_ref, mod_ref, g_ref, wf_ref, wg_ref, bg_ref, q_ref, k_ref, v_ref, o_ref, gt_ref,
                     w_ref):
    @pl.when(pl.program_id(0) == 0)
    def _():
        for c in range(4):
            w_ref[:, c * D:(c + 1) * D] = wf_ref[0, :, c * D:(c + 1) * D].astype(BF16)

    h = _norm_mod(y_ref[...], g_ref[...], mod_ref[0], 0)
    hb = h.astype(BF16)
    q_ref[...] = jnp.dot(hb, w_ref[:, 0:D], preferred_element_type=F32).astype(BF16)
    k_ref[...] = (jnp.dot(hb, w_ref[:, D:2 * D], preferred_element_type=F32) * (M_HEAD_DIM ** -0.5)).astype(BF16)
    v_ref[...] = jnp.dot(hb, w_ref[:, 2 * D:3 * D], preferred_element_type=F32).astype(BF16)
    o_ref[...] = _sigmoid(jnp.dot(hb, w_ref[:, 3 * D:4 * D], preferred_element_type=F32))
    gt = jnp.dot(hb, wg_ref[...], preferred_element_type=F32) + bg_ref[...]
    lane = lax.broadcasted_iota(I32, gt.shape, 1)
    is_f = ((lane >= M_HEADS) & (lane < 2 * M_HEADS)) | ((lane >= 3 * M_HEADS) & (lane < 4 * M_HEADS))
    gt_ref[...] = jnp.where(is_f, _log_sigmoid(gt), gt)


def _mlstm_in(y, mods, mrow, g1, w_in_all, slot, w_gate, b_gate):
    w_spec = pl.BlockSpec((1,) + w_in_all.shape[1:], lambda j, *_: (slot, 0, 0), pipeline_mode=pl.Buffered(1))
    return pl.pallas_call(
        _mlstm_in_kernel,
        out_shape=(jax.ShapeDtypeStruct((N_TOK, D), BF16), jax.ShapeDtypeStruct((N_TOK, D), BF16),
                   jax.ShapeDtypeStruct((N_TOK, D), BF16), jax.ShapeDtypeStruct((N_TOK, D), F32),
                   jax.ShapeDtypeStruct((N_TOK, LANE), F32)),
        grid_spec=pltpu.PrefetchScalarGridSpec(
            num_scalar_prefetch=1, grid=(NB,),
            in_specs=[_tok_spec(D), _mod_spec(), _full_spec((1, D)), w_spec,
                      _full_spec((D, LANE)), _full_spec((1, LANE))],
            out_specs=(_tok_spec(D), _tok_spec(D), _tok_spec(D), _tok_spec(D), _tok_spec(LANE)),
            scratch_shapes=[pltpu.VMEM((D, 4 * D), BF16)]),
        compiler_params=_cparams(),
        name="mlstm_in",
    )(mrow, y, mods, g1, w_in_all, w_gate, b_gate)


def _mlstm_load(hd, c, q_ref, k_ref, v_ref, gc_ref, gr_ref):
    r0 = pl.multiple_of(c * M_CHUNK, M_CHUNK)
    hs = slice(hd * M_HEAD_DIM, (hd + 1) * M_HEAD_DIM)
    rows = pl.ds(r0, M_CHUNK)
    return rows, hs, q_ref[rows, hs], k_ref[rows, hs], v_ref[rows, hs], gc_ref[hd, rows, :], gr_ref[hd, c]


def _mlstm_chunks(chains, ms, loaded, c_scr, n_scr):
    L = M_CHUNK
    n = range(len(chains))
    t_idx = lax.broadcasted_iota(I32, (L, L), 0)
    s_idx = lax.broadcasted_iota(I32, (L, L), 1)
    masks = {0: (s_idx <= t_idx, t_idx <= s_idx), 1: (s_idx >= t_idx, t_idx >= s_idx)}
    q = [ld[2] for ld in loaded]
    k = [ld[3] for ld in loaded]
    v = [ld[4] for ld in loaded]
    i_col = [ld[5][:, 2 * d:2 * d + 1] for (_, d), ld in zip(chains, loaded)]
    lf_col = [ld[5][:, 2 * d + 1:2 * d + 2] for (_, d), ld in zip(chains, loaded)]
    i_row = [ld[6][2 * d:2 * d + 1, :] for (_, d), ld in zip(chains, loaded)]
    lf_row = [ld[6][2 * d + 1:2 * d + 2, :] for (_, d), ld in zip(chains, loaded)]
    mask = [masks[d][0] for _, d in chains]
    mask_t = [masks[d][1] for _, d in chains]
    b_col = [jnp.sum(jnp.where(mask[i], lf_row[i], 0.0), axis=1, keepdims=True) for i in n]
    b_row = [jnp.sum(jnp.where(mask_t[i], lf_col[i], 0.0), axis=0, keepdims=True) for i in n]
    log_d = [jnp.where(mask[i], b_col[i] - b_row[i] + i_row[i], -jnp.inf) for i in n]
    li = [b_col[i] + ms[i] for i in n]
    m_r = [jnp.maximum(li[i], jnp.max(log_d[i], axis=1, keepdims=True)) for i in n]
    a_int = [jnp.exp(li[i] - m_r[i]) for i in n]
    dmat = [jnp.exp(log_d[i] - m_r[i]) for i in n]
    cmat = [c_scr[d, hd] for hd, d in chains]
    nvec = [n_scr[d, hd] for hd, d in chains]
    gram = [lax.dot_general(q[i], k[i], (((1,), (1,)), ((), ())), preferred_element_type=F32) for i in n]
    inter = [jnp.dot(q[i], cmat[i].astype(BF16), preferred_element_type=F32) for i in n]
    s = [gram[i] * dmat[i] for i in n]
    intra = [jnp.dot(s[i].astype(BF16), v[i], preferred_element_type=F32) for i in n]
    qn = [jnp.sum(q[i].astype(F32) * nvec[i], axis=1, keepdims=True) for i in n]
    den = [a_int[i] * qn[i] + jnp.sum(s[i], axis=1, keepdims=True) for i in n]
    hh = [(a_int[i] * inter[i] + intra[i]) / jnp.maximum(jnp.abs(den[i]), jnp.exp(-m_r[i])) for i in n]
    b_last = [b_row[i][:, L - 1:L] if chains[i][1] == 0 else b_row[i][:, 0:1] for i in n]
    log_w = [b_last[i] - b_col[i] + i_col[i] for i in n]
    m_new = [jnp.maximum(b_last[i] + ms[i], jnp.max(log_w[i], axis=0, keepdims=True)) for i in n]
    w = [jnp.exp(log_w[i] - m_new[i]) for i in n]
    decay = [jnp.exp(b_last[i] + ms[i] - m_new[i]) for i in n]
    kw = [k[i].astype(F32) * w[i] for i in n]
    kv = [lax.dot_general(kw[i].astype(BF16), v[i], (((0,), (0,)), ((), ())), preferred_element_type=F32) for i in n]
    for i, (hd, d) in enumerate(chains):
        c_scr[d, hd] = decay[i] * cmat[i] + kv[i]
        n_scr[d, hd] = decay[i] * nvec[i] + jnp.sum(kw[i], axis=0, keepdims=True)
    return hh, m_new


def _mlstm_scan_body(n_chunks, q_ref, k_ref, v_ref, gc_ref, gr_ref, h_ref, hb_scr, c_scr, n_scr, m0):
    chains = [(hd, d) for hd in range(M_HEADS) for d in range(2)]

    def body(c, ms):
        loaded = [_mlstm_load(hd, c if d == 0 else n_chunks - 1 - c, q_ref, k_ref, v_ref, gc_ref, gr_ref)
                  for hd, d in chains]
        hh, m_new = _mlstm_chunks(chains, ms, loaded, c_scr, n_scr)
        for (hd, d), ld, h in zip(chains, loaded, hh):
            dst = h_ref if d == 0 else hb_scr
            dst[ld[0], ld[1]] = h
        return tuple(m_new)

    ms = lax.fori_loop(0, n_chunks, body, tuple(m0))
    h_ref[...] += hb_scr[...]
    return ms


def _mlstm_scan_ctx_kernel(q_ref, k_ref, v_ref, gc_ref, gr_ref, h_ref, cn_ref, nn_ref, mn_ref, hb_scr, c_scr, n_scr):
    c_scr[...] = jnp.zeros(c_scr.shape, F32)
    n_scr[...] = jnp.zeros(n_scr.shape, F32)
    zero = jnp.zeros((1, 1), F32)
    ms = _mlstm_scan_body(SEQ // M_CHUNK, q_ref, k_ref, v_ref, gc_ref, gr_ref, h_ref, hb_scr, c_scr, n_scr,
                          [zero] * (2 * M_HEADS))
    cn_ref[0] = c_scr[...]
    nn_ref[0] = n_scr[...]
    for hd in range(M_HEADS):
        for d in range(2):
            mn_ref[0, d, hd] = jnp.broadcast_to(ms[2 * hd + d], (1, LANE))


def _mlstm_scan_lat_kernel(q_ref, k_ref, v_ref, gc_ref, gr_ref, c0_ref, n0_ref, m0_ref, ctx_out_ref, h_ref,
                           hb_scr, c_scr, n_scr):
    del ctx_out_ref
    c_scr[...] = c0_ref[0]
    n_scr[...] = n0_ref[0]
    m0 = [m0_ref[0, d, hd] for hd in range(M_HEADS) for d in range(2)]
    _mlstm_scan_body(DEC_SEQ // M_CHUNK, q_ref, k_ref, v_ref, gc_ref, gr_ref, h_ref, hb_scr, c_scr, n_scr, m0)


def _mlstm_scan(q, k, v, gates, state_c, state_n, state_m):
    g16 = gates[:, :4 * M_HEADS].reshape(N_TOK, 4, M_HEADS)
    gcol = jnp.transpose(g16, (2, 0, 1))
    grow = jnp.transpose(g16.reshape(N_TOK // M_CHUNK, M_CHUNK, 4, M_HEADS), (3, 0, 2, 1))
    hd = M_HEAD_DIM
    state_scratch = [pltpu.VMEM((2, M_HEADS, hd, hd), F32), pltpu.VMEM((2, M_HEADS, 1, hd), F32)]
    ncp = SEQ // M_CHUNK
    h_ctx, new_c, new_n, new_m = pl.pallas_call(
        _mlstm_scan_ctx_kernel,
        out_shape=(jax.ShapeDtypeStruct((N_TOK, D), F32),
                   jax.ShapeDtypeStruct((BATCH, 2, M_HEADS, hd, hd), F32),
                   jax.ShapeDtypeStruct((BATCH, 2, M_HEADS, 1, hd), F32),
                   jax.ShapeDtypeStruct((BATCH, 2, M_HEADS, 1, LANE), F32)),
        grid=(BATCH,),
        in_specs=[
            pl.BlockSpec((SEQ, D), lambda s: (s, 0)),
            pl.BlockSpec((SEQ, D), lambda s: (s, 0)),
            pl.BlockSpec((SEQ, D), lambda s: (s, 0)),
            pl.BlockSpec((M_HEADS, SEQ, 4), lambda s: (0, s, 0)),
            pl.BlockSpec((M_HEADS, ncp, 4, M_CHUNK), lambda s: (0, s, 0, 0)),
        ],
        out_specs=(
            pl.BlockSpec((SEQ, D), lambda s: (s, 0)),
            pl.BlockSpec((1, 2, M_HEADS, hd, hd), lambda s: (s, 0, 0, 0, 0)),
            pl.BlockSpec((1, 2, M_HEADS, 1, hd), lambda s: (s, 0, 0, 0, 0)),
            pl.BlockSpec((1, 2, M_HEADS, 1, LANE), lambda s: (s, 0, 0, 0, 0)),
        ),
        scratch_shapes=[pltpu.VMEM((SEQ, D), F32)] + state_scratch,
        compiler_params=_cparams(),
        name="mlstm_scan_ctx",
    )(q, k, v, gcol, grow)
    ncl = DEC_SEQ // M_CHUNK
    pb = NP_TOK // DEC_SEQ
    h_all = pl.pallas_call(
        _mlstm_scan_lat_kernel,
        out_shape=jax.ShapeDtypeStruct((N_TOK, D), F32),
        input_output_aliases={8: 0},
        grid=(DEC_BATCH,),
        in_specs=[
            pl.BlockSpec((DEC_SEQ, D), lambda b: (pb + b, 0)),
            pl.BlockSpec((DEC_SEQ, D), lambda b: (pb + b, 0)),
            pl.BlockSpec((DEC_SEQ, D), lambda b: (pb + b, 0)),
            pl.BlockSpec((M_HEADS, DEC_SEQ, 4), lambda b: (0, pb + b, 0)),
            pl.BlockSpec((M_HEADS, ncl, 4, M_CHUNK), lambda b: (0, pb + b, 0, 0)),
            pl.BlockSpec((1, 2, M_HEADS, hd, hd), lambda b: (b, 0, 0, 0, 0)),
            pl.BlockSpec((1, 2, M_HEADS, 1, hd), lambda b: (b, 0, 0, 0, 0)),
            pl.BlockSpec((1, 2, M_HEADS, 1, 1), lambda b: (b, 0, 0, 0, 0)),
            pl.BlockSpec(memory_space=pl.ANY),
        ],
        out_specs=pl.BlockSpec((DEC_SEQ, D), lambda b: (pb + b, 0)),
        scratch_shapes=[pltpu.VMEM((DEC_SEQ, D), F32)] + state_scratch,
        compiler_params=_cparams(),
        name="mlstm_scan_lat",
    )(q, k, v, gcol, grow, state_c, state_n, state_m, h_ctx)
    return h_all, new_c, new_n, new_m


def _mlstm_out_kernel(mr_ref, h_ref, o_ref, ng_ref, w_ref, y_ref, mod_ref, out_ref, x_scr):
    hc = o_ref[...] * h_ref[...]
    for hd in range(M_HEADS):
        sl = slice(hd * M_HEAD_DIM, (hd + 1) * M_HEAD_DIM)
        x_scr[:, sl] = _rms(hc[:, sl], ng_ref[:, sl]).astype(BF16)
    out = jnp.dot(x_scr[...], w_ref[...], preferred_element_type=F32)
    out_ref[...] = y_ref[...] + mod_ref[0][2:3] * out


def _mlstm_out(hsum, o, norm_g, w_out, y, mods, mrow):
    return pl.pallas_call(
        _mlstm_out_kernel,
        out_shape=jax.ShapeDtypeStruct((N_TOK, D), F32),
        grid_spec=pltpu.PrefetchScalarGridSpec(
            num_scalar_prefetch=1, grid=(NL,),
            in_specs=[_tok_spec(D, TL), _tok_spec(D, TL), _full_spec((1, D)), _full_spec((D, D)), _tok_spec(D, TL),
                      _mod_spec()],
            out_specs=_tok_spec(D, TL),
            scratch_shapes=[pltpu.VMEM((TL, D), BF16)]),
        compiler_params=_cparams(),
        name="mlstm_out",
    )(mrow, hsum, o, norm_g, w_out, y, mods)


ROUTE_OFF = N_GROUPS
SLAB = D // (2 * LANE)
V7X_SC_CORES = 2
V7X_SC_SUBCORES = 16
SC_WORKERS = V7X_SC_CORES * V7X_SC_SUBCORES
SC_WINDOW = 128
HI_MASK = -65536


def _bf16_bits(x):
    return lax.bitcast_convert_type(x.astype(BF16).astype(F32), I32)


def _store_slabs(ref, x):
    rows = x.shape[0]
    for c in range(SLAB):
        lo = lax.shift_right_logical(_bf16_bits(x[:, (2 * c) * LANE:(2 * c + 1) * LANE]), 16)
        hi = _bf16_bits(x[:, (2 * c + 1) * LANE:(2 * c + 2) * LANE]) & HI_MASK
        ref[pl.ds(c, rows, stride=SLAB), :] = lo | hi


def _load_slabs(ref, dst, rows, dtype):
    for c in range(SLAB):
        w = ref[pl.ds(c, rows, stride=SLAB), :]
        lo = lax.bitcast_convert_type(lax.shift_left(w, 16), F32)
        hi = lax.bitcast_convert_type(w & HI_MASK, F32)
        dst[:, (2 * c) * LANE:(2 * c + 1) * LANE] = lo.astype(dtype)
        dst[:, (2 * c + 1) * LANE:(2 * c + 2) * LANE] = hi.astype(dtype)


def _route_kernel(mr_ref, y_ref, mod_ref, g_ref, wr_ref, br_ref, tri_ref, x_ref, wt_ref, meta_ref, cnt_ref, cnt_scr):
    x = _norm_mod(y_ref[...], g_ref[...], mod_ref[0], 1)
    _store_slabs(x_ref, x)
    lg = jnp.dot(x.astype(BF16), wr_ref[...], preferred_element_type=F32) + br_ref[...]
    lane = lax.broadcasted_iota(I32, lg.shape, 1).astype(F32)
    ninf = -jnp.inf
    big = float(LANE)
    lgg = jnp.where(lane < N_GROUPS, lg, ninf)
    gmax = jnp.max(lgg, axis=-1, keepdims=True)
    g_idx = jnp.min(jnp.where(lgg == gmax, lane, big), axis=-1, keepdims=True)
    g_w = 1.0 / jnp.sum(jnp.exp(lgg - gmax), axis=-1, keepdims=True)
    lo = ROUTE_OFF + g_idx * EXPERTS_PER_GROUP
    le = jnp.where((lane >= lo) & (lane < lo + EXPERTS_PER_GROUP), lg, ninf)
    m1 = jnp.max(le, axis=-1, keepdims=True)
    i1 = jnp.min(jnp.where(le == m1, lane, big), axis=-1, keepdims=True)
    le2 = jnp.where(lane == i1, ninf, le)
    m2 = jnp.max(le2, axis=-1, keepdims=True)
    i2 = jnp.min(jnp.where(le2 == m2, lane, big), axis=-1, keepdims=True)
    r = jnp.exp(m2 - m1)
    p1 = 1.0 / (1.0 + r)
    p2 = r / (1.0 + r)
    two = lax.broadcasted_iota(I32, (x.shape[0], TOP_K), 1)
    wt_ref[...] = jnp.where(two == 0, g_w * p1, g_w * p2)
    @pl.when(pl.program_id(0) == 0)
    def _():
        cnt_scr[...] = jnp.zeros(cnt_scr.shape, F32)

    oh1 = (lane == i1).astype(F32)
    oh2 = (lane == i2).astype(F32)
    both = oh1 + oh2
    before = jnp.dot(tri_ref[...], both.astype(BF16), preferred_element_type=F32) + cnt_scr[...]
    rk1 = jnp.sum(oh1 * before, axis=-1, keepdims=True)
    rk2 = jnp.sum(oh2 * before, axis=-1, keepdims=True)
    cnt_scr[...] = cnt_scr[...] + jnp.sum(both, axis=0, keepdims=True)
    cnt_ref[...] = cnt_scr[...]
    cols = (i1 - ROUTE_OFF, i2 - ROUTE_OFF, rk1, rk2)
    packed = jnp.zeros(lg.shape, F32)
    for c, val in enumerate(cols):
        packed = jnp.where(lane == c, val, packed)
    meta_ref[...] = jnp.transpose(packed)[0:len(cols), :].astype(I32)


def _route(y, mods, mrow, g2, w_route, b_route):
    return pl.pallas_call(
        _route_kernel,
        out_shape=(jax.ShapeDtypeStruct((N_TOK * SLAB, LANE), I32), jax.ShapeDtypeStruct((N_TOK, TOP_K), F32),
                   jax.ShapeDtypeStruct((2 * TOP_K, N_TOK), I32), jax.ShapeDtypeStruct((1, LANE), F32)),
        grid_spec=pltpu.PrefetchScalarGridSpec(
            num_scalar_prefetch=1, grid=(NL,),
            in_specs=[_tok_spec(D, TL), _mod_spec(), _full_spec((1, D)), _full_spec((D, LANE)),
                      _full_spec((1, LANE)), _full_spec((TL, TL))],
            out_specs=(pl.BlockSpec((TL * SLAB, LANE), lambda j, *_: (j, 0)), _tok_spec(TOP_K, TL),
                       pl.BlockSpec((2 * TOP_K, TL), lambda j, *_: (0, j)), _full_spec((1, LANE))),
            scratch_shapes=[pltpu.VMEM((1, LANE), F32)]),
        compiler_params=_cparams(),
        name="moe_route",
    )(mrow, y, mods, g2, w_route, b_route, jnp.asarray(np.tril(np.ones((TL, TL), np.float32), -1), dtype=BF16))


SLOT_COLS = 2048


def _slot_kernel(meta_ref, ps_ref, o_ref):
    sub = lax.broadcasted_iota(I32, (N_EXPERTS, SLOT_COLS), 0)
    meta = meta_ref[...]
    table = ps_ref[...]
    for k in range(TOP_K):
        start = jnp.sum(jnp.where(sub == meta[k:k + 1, :], table, 0), axis=0, keepdims=True)
        o_ref[k:k + 1, :] = start + meta[TOP_K + k:TOP_K + k + 1, :]


def _slots(meta, pad_start):
    return pl.pallas_call(
        _slot_kernel,
        out_shape=jax.ShapeDtypeStruct((TOP_K, N_TOK), I32),
        grid=(N_TOK // SLOT_COLS,),
        in_specs=[pl.BlockSpec((2 * TOP_K, SLOT_COLS), lambda j: (0, j)), pl.BlockSpec((N_EXPERTS, 1), lambda j: (0, 0))],
        out_specs=pl.BlockSpec((TOP_K, SLOT_COLS), lambda j: (0, j)),
        compiler_params=_cparams(),
        name="moe_slots",
    )(meta, pad_start.astype(I32).reshape(N_EXPERTS, 1))


def _dispatch_tables(meta, lane_counts):
    counts = lane_counts[0, ROUTE_OFF:ROUTE_OFF + N_EXPERTS].astype(I32)
    padded = ((counts + EBLK - 1) // EBLK) * EBLK
    pad_end = jnp.cumsum(padded)
    pad_start = pad_end - padded
    dest = _slots(meta, pad_start)
    n_blk = (padded // EBLK).astype(I32)
    blk_start = (pad_start // EBLK).astype(I32)
    n_used = (pad_end[-1] // EBLK).astype(I32).reshape(1)
    return dest, blk_start, n_blk, n_used


def _sc_mesh():
    return plsc.VectorSubcoreMesh(core_axis_name="core", subcore_axis_name="subcore",
                                  num_cores=V7X_SC_CORES, num_subcores=V7X_SC_SUBCORES)


def _sc_worker():
    return lax.axis_index("core") * V7X_SC_SUBCORES + lax.axis_index("subcore")


def _sc_dispatch(x_slabs, d0, d1):
    per = N_TOK // SC_WORKERS

    @functools.partial(
        pl.kernel, out_type=jax.ShapeDtypeStruct((P_SLOTS, SLAB, LANE), I32), mesh=_sc_mesh(), name="moe_dispatch",
        scratch_types=[pltpu.VMEM((1, per), I32), pltpu.VMEM((1, per), I32), pltpu.VMEM((SC_WINDOW, SLAB, LANE), I32)])
    def run(x_hbm, d0_hbm, d1_hbm, o_hbm, i0_v, i1_v, buf):
        base = _sc_worker() * per
        pltpu.sync_copy(d0_hbm.at[:, pl.ds(base, per)], i0_v)
        pltpu.sync_copy(d1_hbm.at[:, pl.ds(base, per)], i1_v)

        @pl.loop(0, per // SC_WINDOW)
        def _(s):
            off = s * SC_WINDOW
            pltpu.sync_copy(x_hbm.at[pl.ds(base + off, SC_WINDOW)], buf)
            pltpu.sync_copy(buf, o_hbm.at[i0_v.at[0, pl.ds(off, SC_WINDOW)]])
            pltpu.sync_copy(buf, o_hbm.at[i1_v.at[0, pl.ds(off, SC_WINDOW)]])

    return run(x_slabs.reshape(N_TOK, SLAB, LANE), d0, d1)


def _sc_collect(y_slabs, dcat):
    per = N_ASSIGN // SC_WORKERS

    @functools.partial(
        pl.kernel, out_type=jax.ShapeDtypeStruct((N_ASSIGN, SLAB, LANE), I32), mesh=_sc_mesh(), name="moe_collect",
        scratch_types=[pltpu.VMEM((1, per), I32), pltpu.VMEM((SC_WINDOW, SLAB, LANE), I32)])
    def run(y_hbm, i_hbm, o_hbm, i_v, buf):
        base = _sc_worker() * per
        pltpu.sync_copy(i_hbm.at[:, pl.ds(base, per)], i_v)

        @pl.loop(0, per // SC_WINDOW)
        def _(s):
            off = s * SC_WINDOW
            pltpu.sync_copy(y_hbm.at[i_v.at[0, pl.ds(off, SC_WINDOW)]], buf)
            pltpu.sync_copy(buf, o_hbm.at[pl.ds(base + off, SC_WINDOW)])

    return run(y_slabs.reshape(P_SLOTS, SLAB, LANE), dcat)


EROWS = EBLK * SLAB


def _expert_kernel(bs_ref, nb_ref, nu_ref, wg_ref, wu_ref, wd_ref, x_hbm, y_hbm,
                   xbuf, ybuf, xs, wg_bf, wu_bf, wd_bf, isem, osem):
    e = pl.program_id(0)
    n_exp = pl.num_programs(0)
    n_used = nu_ref[0]
    b0 = bs_ref[e]
    nb = nb_ref[e]

    def in_copy(g, slot):
        return pltpu.make_async_copy(x_hbm.at[pl.ds(pl.multiple_of(g * EROWS, EROWS), EROWS)], xbuf.at[slot],
                                     isem.at[slot])

    def out_copy(g, slot):
        return pltpu.make_async_copy(ybuf.at[slot], y_hbm.at[pl.ds(pl.multiple_of(g * EROWS, EROWS), EROWS)],
                                     osem.at[slot])

    @pl.when(e == 0)
    def _():
        in_copy(0, 0).start()

    @pl.when(nb > 0)
    def _():
        wg_bf[...] = wg_ref[0, 0].astype(BF16)
        wu_bf[...] = wu_ref[0, 0].astype(BF16)
        wd_bf[...] = wd_ref[0, 0].astype(BF16)

    def block(k, carry):
        g = b0 + k
        slot = lax.rem(g, 2)
        in_copy(g, slot).wait()

        @pl.when(g + 1 < n_used)
        def _():
            in_copy(g + 1, 1 - slot).start()

        _load_slabs(xbuf.at[slot], xs, EBLK, BF16)
        xb = xs[...]
        gt = jnp.dot(xb, wg_bf[...], preferred_element_type=F32)
        up = jnp.dot(xb, wu_bf[...], preferred_element_type=F32)
        hmid = (gt * _sigmoid(gt) * up).astype(BF16)
        res = jnp.dot(hmid, wd_bf[...], preferred_element_type=F32)

        @pl.when(g >= 2)
        def _():
            out_copy(g - 2, slot).wait()

        _store_slabs(ybuf.at[slot], res)
        out_copy(g, slot).start()
        return carry

    lax.fori_loop(0, nb, block, 0)

    @pl.when(e == n_exp - 1)
    def _():
        last = n_used - 1
        out_copy(last, lax.rem(last, 2)).wait()

        @pl.when(n_used >= 2)
        def _():
            out_copy(last - 1, lax.rem(last - 1, 2)).wait()


def _experts(x_sorted, blk_start, n_blk, n_used, w_gate, w_up, w_down, layer):
    any_spec = pl.BlockSpec(memory_space=pl.ANY)
    wspec = lambda r, c: pl.BlockSpec((1, 1, r, c), lambda e, *_: (layer, e, 0, 0))
    return pl.pallas_call(
        _expert_kernel,
        out_shape=jax.ShapeDtypeStruct((P_SLOTS * SLAB, LANE), I32),
        grid_spec=pltpu.PrefetchScalarGridSpec(
            num_scalar_prefetch=3, grid=(N_EXPERTS,),
            in_specs=[wspec(D, D_EXPERT), wspec(D, D_EXPERT), wspec(D_EXPERT, D), any_spec],
            out_specs=any_spec,
            scratch_shapes=[
                pltpu.VMEM((2, EROWS, LANE), I32), pltpu.VMEM((2, EROWS, LANE), I32),
                pltpu.VMEM((EBLK, D), BF16),
                pltpu.VMEM((D, D_EXPERT), BF16), pltpu.VMEM((D, D_EXPERT), BF16), pltpu.VMEM((D_EXPERT, D), BF16),
                pltpu.SemaphoreType.DMA((2,)), pltpu.SemaphoreType.DMA((2,)),
            ]),
        compiler_params=_cparams(),
        name="moe_experts",
    )(blk_start, n_blk, n_used, w_gate, w_up, w_down, x_sorted.reshape(P_SLOTS * SLAB, LANE))


def _combine_kernel(final, mr_ref, e0_ref, e1_ref, wt_ref, y_ref, mod_ref, fg_ref, o_ref, a_scr, b_scr):
    _load_slabs(e0_ref, a_scr, TL, F32)
    _load_slabs(e1_ref, b_scr, TL, F32)
    wt = wt_ref[...]
    moe = wt[:, 0:1] * a_scr[...] + wt[:, 1:2] * b_scr[...]
    y_new = y_ref[...] + mod_ref[0][5:6] * moe
    o_ref[...] = _rms(y_new, fg_ref[...]) if final else y_new


def _combine(ym, wts, y, mods, mrow, final_g, blk0, nblk, final):
    tok = lambda width: pl.BlockSpec((TL, width), lambda j, *_: (blk0 + j, 0))
    slab0 = pl.BlockSpec((TL * SLAB, LANE), lambda j, *_: (blk0 + j, 0))
    slab1 = pl.BlockSpec((TL * SLAB, LANE), lambda j, *_: (NL + blk0 + j, 0))
    mod = pl.BlockSpec((1, 6, D), lambda j, mr: (mr[blk0 + j], 0, 0))
    return pl.pallas_call(
        functools.partial(_combine_kernel, final),
        out_shape=jax.ShapeDtypeStruct((nblk * TL, D), F32),
        grid_spec=pltpu.PrefetchScalarGridSpec(
            num_scalar_prefetch=1, grid=(nblk,),
            in_specs=[slab0, slab1, tok(TOP_K), tok(D), mod, _full_spec((1, D))],
            out_specs=pl.BlockSpec((TL, D), lambda j, *_: (j, 0)),
            scratch_shapes=[pltpu.VMEM((TL, D), F32), pltpu.VMEM((TL, D), F32)]),
        compiler_params=_cparams(),
        name="moe_combine",
    )(mrow, ym, ym, wts, y, mods, final_g)


def kernel(x_prompt, x_sample, cache_attn_k, cache_attn_v, state_mlstm_C, state_mlstm_n, state_mlstm_m, c, c_ctx, ada_w, ada_b, norm1_g, norm2_g, conv_w_in, conv_w_dw, conv_b_dw, conv_ln_g, conv_ln_b, conv_w_out, attn_w_qkv, attn_q_norm, attn_k_norm, attn_w_o, mlstm_w_in, mlstm_b_gate, mlstm_norm_g, mlstm_w_out, moe_w_group, moe_b_group, moe_w_router, moe_b_router, moe_w_gate, moe_w_up, moe_w_down, final_norm_g):
    y = jnp.concatenate([x_prompt.reshape(NP_TOK, D), x_sample.reshape(NS_TOK, D)], axis=0)
    cvec = jnp.concatenate([c_ctx[None, :], c, jnp.zeros((MOD_ROWS - 1 - DEC_BATCH, D), F32)], axis=0)
    mods = _ada_all(cvec, ada_w, ada_b)
    rope = _rope_blocks()
    new_k = new_v = new_c = new_n = new_m = None
    for i in range(DEPTH):
        kind, slot = i % 3, i // 3
        mrow = jnp.asarray(_MOD_ROW + i * MOD_ROWS)
        mrow_sb = jnp.asarray(_MOD_ROW_SB + i * MOD_ROWS)
        mrow_l = jnp.asarray(_MOD_ROW_L + i * MOD_ROWS)
        g1 = norm1_g[i].reshape(1, D)
        if kind == 0:
            u = _conv_in(y, mods, mrow_l, g1, conv_w_in[slot].astype(BF16))
            w_dw = jnp.concatenate([conv_w_dw[slot], jnp.zeros((1, D), F32)], axis=0)
            y = _conv_main(u, y, mods, mrow_sb, w_dw, conv_b_dw[slot].reshape(1, D), conv_ln_g[slot].reshape(1, D),
                           conv_ln_b[slot].reshape(1, D), conv_w_out[slot].astype(BF16))
        elif kind == 1:
            q, kb, vb, kf, vf = _attn_qkv(y, mods, mrow_sb, g1, attn_w_qkv[slot].astype(BF16),
                                          attn_q_norm[slot].reshape(1, HEAD_DIM), attn_k_norm[slot].reshape(1, HEAD_DIM),
                                          rope)
            new_k = kf[:NP_TOK].reshape(BATCH, 1, SEQ, N_KV_HEADS, HEAD_DIM)
            new_v = vf[:NP_TOK].reshape(BATCH, 1, SEQ, N_KV_HEADS, HEAD_DIM)
            ck = cache_attn_k[:, slot].reshape(DEC_BATCH, PAST_LEN, KV_DIM)
            cv = cache_attn_v[:, slot].reshape(DEC_BATCH, PAST_LEN, KV_DIM)
            y = _attention(q, kb, vb, ck, cv, attn_w_o[slot].astype(BF16), y, mods, i)
        else:
            w_in = mlstm_w_in[slot]
            w_gate = jnp.concatenate([w_in[:, 4 * D:], jnp.zeros((D, LANE - 4 * M_HEADS), F32)], axis=1)
            b_gate = jnp.concatenate([mlstm_b_gate[slot], jnp.zeros((LANE - 4 * M_HEADS,), F32)]).reshape(1, LANE)
            q, k, v, o, gates = _mlstm_in(y, mods, mrow, g1, mlstm_w_in, slot, w_gate.astype(BF16), b_gate)
            sc = state_mlstm_C[:, slot]
            sn = state_mlstm_n[:, slot].reshape(DEC_BATCH, 2, M_HEADS, 1, M_HEAD_DIM)
            sm = state_mlstm_m[:, slot].reshape(DEC_BATCH, 2, M_HEADS, 1, 1)
            hsum, nc_, nn_, nm_ = _mlstm_scan(q, k, v, gates, sc, sn, sm)
            new_c = nc_[:, None]
            new_n = nn_.reshape(BATCH, 1, 2, M_HEADS, M_HEAD_DIM)
            new_m = nm_[..., 0, 0].reshape(BATCH, 1, 2, M_HEADS)
            y = _mlstm_out(hsum, o, mlstm_norm_g[slot].reshape(1, D), mlstm_w_out[slot].astype(BF16), y, mods, mrow_l)
        w_route = jnp.concatenate([moe_w_group[i], moe_w_router[i],
                                   jnp.zeros((D, LANE - N_GROUPS - N_EXPERTS), F32)], axis=1)
        b_route = jnp.concatenate([moe_b_group[i], moe_b_router[i],
                                   jnp.zeros((LANE - N_GROUPS - N_EXPERTS,), F32)]).reshape(1, LANE)
        x2, ewt, meta, cnt = _route(y, mods, mrow_l, norm2_g[i].reshape(1, D), w_route.astype(BF16), b_route)
        dest, blk_start, n_blk, n_used = _dispatch_tables(meta, cnt)
        x_sorted = _sc_dispatch(x2, dest[0:1], dest[1:2])
        y_sorted = _experts(x_sorted, blk_start, n_blk, n_used, moe_w_gate, moe_w_up, moe_w_down, i)
        ym = _sc_collect(y_sorted, dest.reshape(1, N_ASSIGN))
        ym = ym.reshape(N_ASSIGN * SLAB, LANE)
        fg = final_norm_g.reshape(1, D)
        if i + 1 < DEPTH:
            y = _combine(ym, ewt, y, mods, mrow_l, fg, 0, NL, False)
        else:
            y_prompt = _combine(ym, ewt, y, mods, mrow_l, fg, 0, NLP, True).reshape(BATCH, SEQ, D)
            y_sample = _combine(ym, ewt, y, mods, mrow_l, fg, NLP, NL - NLP, True).reshape(DEC_BATCH, DEC_SEQ, D)
    return (y_prompt, y_sample, new_k, new_v, new_c, new_n, new_m)
```

```python
import functools

import jax
import jax.numpy as jnp
import numpy as np
from jax import lax
from jax.experimental import pallas as pl
from jax.experimental.pallas import tpu as pltpu
from jax.experimental.pallas import tpu_sc as plsc

F32 = jnp.float32
BF16 = jnp.bfloat16
I32 = jnp.int32

D = 1024
BATCH, SEQ = 16, 256
DEC_BATCH, DEC_SEQ = 8, 1024
PAST_LEN = 256
DEPTH = 4
GRID_W = 64
EPS = 1e-6
CONV_WIDTH = 31
CONV_PAD = CONV_WIDTH // 2
HEAD_DIM = 128
N_HEADS = 8
N_KV_HEADS = 2
GQA_GROUP = N_HEADS // N_KV_HEADS
Q_DIM = N_HEADS * HEAD_DIM
KV_DIM = N_KV_HEADS * HEAD_DIM
QKV_DIM = Q_DIM + 2 * KV_DIM
ROPE_THETA = 10000.0
M_HEADS = 4
M_HEAD_DIM = D // M_HEADS
M_CHUNK = 64
N_GROUPS = 4
EXPERTS_PER_GROUP = 8
N_EXPERTS = N_GROUPS * EXPERTS_PER_GROUP
TOP_K = 2
D_EXPERT = 512

NP_TOK = BATCH * SEQ
NS_TOK = DEC_BATCH * DEC_SEQ
N_TOK = NP_TOK + NS_TOK
TM = 512
NB = N_TOK // TM
NBP = NP_TOK // TM
BLK_PER_DEC = DEC_SEQ // TM
TL = 1024
NL = N_TOK // TL
NLP = NP_TOK // TL
SB = 256
NSB = N_TOK // SB
NSBP = NP_TOK // SB
SB_PER_DEC = DEC_SEQ // SB
MOD_ROWS = 16
HALO = 16
LANE = 128
SUBLANE = 8

N_ASSIGN = N_TOK * TOP_K
EBLK = 256
N_EBLK = N_ASSIGN // EBLK + N_EXPERTS
P_SLOTS = N_EBLK * EBLK
N_PAD_SLOTS = P_SLOTS - N_ASSIGN

VMEM_LIMIT = 56 * 1024 * 1024


def _block_tables(nb, nbp, per_dec):
    j = np.arange(nb)
    is_p = j < nbp
    mod_row = np.where(is_p, 0, 1 + (j - nbp) // per_dec)
    rope_idx = np.where(is_p, 0, 1 + (j - nbp) % per_dec)
    first = np.where(is_p, 1, ((j - nbp) % per_dec == 0).astype(np.int64))
    last = np.where(is_p, 1, ((j - nbp) % per_dec == per_dec - 1).astype(np.int64))
    return (mod_row.astype(np.int32), rope_idx.astype(np.int32), first.astype(np.int32), last.astype(np.int32))


_MOD_ROW, _, _, _ = _block_tables(NB, NBP, BLK_PER_DEC)
_MOD_ROW_L, _, _, _ = _block_tables(NL, NLP, DEC_SEQ // TL)
_MOD_ROW_SB, _ROPE_IDX_SB, _SEQ_FIRST, _SEQ_LAST = _block_tables(NSB, NSBP, SB_PER_DEC)


def _cparams(n_axes=1):
    return pltpu.CompilerParams(dimension_semantics=("arbitrary",) * n_axes, vmem_limit_bytes=VMEM_LIMIT)


def _sigmoid(x):
    return 1.0 / (1.0 + jnp.exp(-x))


def _rms(x, g):
    return x * lax.rsqrt(jnp.mean(x * x, axis=-1, keepdims=True) + EPS) * g


def _norm_mod(y, g, mod, which):
    shift = mod[3 * which:3 * which + 1]
    scale = mod[3 * which + 1:3 * which + 2]
    return _rms(y, g) * (1.0 + scale) + shift


def _ada_kernel(c_ref, w_ref, b_ref, o_ref):
    c = c_ref[...]
    s = c * _sigmoid(c)
    o_ref[0] = jnp.dot(s.astype(BF16), w_ref[0].astype(BF16), preferred_element_type=F32) + b_ref[0]


def _ada_all(cvec, ada_w, ada_b):
    tn = 1536
    out = pl.pallas_call(
        _ada_kernel,
        out_shape=jax.ShapeDtypeStruct((DEPTH, MOD_ROWS, 6 * D), F32),
        grid=(DEPTH, 6 * D // tn),
        in_specs=[
            pl.BlockSpec((MOD_ROWS, D), lambda l, n: (0, 0)),
            pl.BlockSpec((1, D, tn), lambda l, n: (l, 0, n)),
            pl.BlockSpec((1, 1, tn), lambda l, n: (l, 0, n)),
        ],
        out_specs=pl.BlockSpec((1, MOD_ROWS, tn), lambda l, n: (l, 0, n)),
        compiler_params=_cparams(2),
        name="ada_mod",
    )(cvec, ada_w, ada_b.reshape(DEPTH, 1, 6 * D))
    return out.reshape(DEPTH * MOD_ROWS, 6, D)


def _tok_spec(width, rows=TM):
    return pl.BlockSpec((rows, width), lambda j, *_: (j, 0))


def _mod_spec():
    return pl.BlockSpec((1, 6, D), lambda j, mr, *_: (mr[j], 0, 0))


def _full_spec(shape):
    nd = len(shape)
    return pl.BlockSpec(shape, lambda j, *_: (0,) * nd)


def _pair_specs(rows, nbp, split):
    s_off = nbp if split else 0
    return [pl.BlockSpec((rows, D), lambda j, *_: (jnp.minimum(j, nbp - 1), 0)),
            pl.BlockSpec((rows, D), lambda j, *_: (jnp.maximum(j, nbp) - s_off, 0))]


def _pair_block(nbp, yp_ref, ys_ref):
    return jnp.where(pl.program_id(0) < nbp, yp_ref[...], ys_ref[...])


def _conv_in_kernel(mr_ref, yp_ref, ys_ref, mod_ref, g_ref, w_ref, u_ref):
    h = _norm_mod(_pair_block(NLP, yp_ref, ys_ref), g_ref[...], mod_ref[0], 0)
    ag = jnp.dot(h.astype(BF16), w_ref[...], preferred_element_type=F32)
    u_ref[...] = ag[:, :D] * _sigmoid(ag[:, D:])


def _conv_in(yp, ys, split, mods, mrow, g1, w_in):
    return pl.pallas_call(
        _conv_in_kernel,
        out_shape=jax.ShapeDtypeStruct((N_TOK, D), F32),
        grid_spec=pltpu.PrefetchScalarGridSpec(
            num_scalar_prefetch=1, grid=(NL,),
            in_specs=_pair_specs(TL, NLP, split) + [_mod_spec(), _full_spec((1, D)), _full_spec((D, 2 * D))],
            out_specs=_tok_spec(D, TL)),
        compiler_params=_cparams(),
        name="conv_in",
    )(mrow, yp, ys, mods, g1, w_in)


def _conv_main_kernel(mr_ref, first_ref, last_ref, u_ref, up_ref, un_ref, wdw_ref, bdw_ref, lg_ref, lb_ref,
                      wout_ref, yp_ref, ys_ref, mod_ref, o_ref, ext_ref, acc_ref):
    j = pl.program_id(0)
    zero = jnp.zeros((HALO, D), F32)
    ext_ref[0:HALO, :] = jnp.where(first_ref[j] == 1, zero, up_ref[...])
    ext_ref[HALO:HALO + SB, :] = u_ref[...]
    ext_ref[HALO + SB:2 * HALO + SB, :] = jnp.where(last_ref[j] == 1, zero, un_ref[...])

    off0 = HALO - CONV_PAD
    n_a = (off0 + CONV_WIDTH - 1) // SUBLANE + 1
    n_chunks = SB // SUBLANE

    def strip(ci, carry):
        cs = pl.ds(pl.multiple_of(ci * LANE, LANE), LANE)
        wk = [jnp.broadcast_to(wdw_ref[k:k + 1, cs], (SUBLANE, LANE)) for k in range(CONV_WIDTH)]
        bias = jnp.broadcast_to(bdw_ref[:, cs], (SUBLANE, LANE))
        sub = lax.broadcasted_iota(I32, (SUBLANE, LANE), 0)
        prev_rot, prev_v0 = None, None
        for j in range(n_chunks + 1):
            tiles = [ext_ref[SUBLANE * (j + a):SUBLANE * (j + a + 1), cs] for a in range(n_a)]
            part = []
            for s in range(SUBLANE):
                acc = None
                for a in range(n_a):
                    k = SUBLANE * a + s - off0
                    if (0 <= k < CONV_WIDTH) and not (s == 0 and j == n_chunks):
                        term = tiles[a] * wk[k]
                        acc = term if acc is None else acc + term
                part.append(acc)
            rot = [None] + [pltpu.roll(part[s], SUBLANE - s, 0) for s in range(1, SUBLANE)]
            if j >= 1:
                out = prev_v0 + bias
                for s in range(1, SUBLANE):
                    out = out + jnp.where(sub < SUBLANE - s, prev_rot[s], rot[s])
                acc_ref[SUBLANE * (j - 1):SUBLANE * j, cs] = out
            prev_rot, prev_v0 = rot, part[0]
        return carry

    lax.fori_loop(0, D // LANE, strip, 0)

    c = acc_ref[...]
    mu = jnp.mean(c, axis=-1, keepdims=True)
    cc = c - mu
    var = jnp.mean(cc * cc, axis=-1, keepdims=True)
    z = cc * lax.rsqrt(var + EPS) * lg_ref[...] + lb_ref[...]
    z = z * _sigmoid(z)
    out = jnp.dot(z.astype(BF16), wout_ref[...], preferred_element_type=F32)
    o_ref[...] = _pair_block(NSBP, yp_ref, ys_ref) + mod_ref[0][2:3] * out


def _conv_main(u, yp, ys, split, mods, mrow, w_dw, b_dw, ln_g, ln_b, w_out):
    nh = N_TOK // HALO
    per = SB // HALO
    sb_spec = pl.BlockSpec((SB, D), lambda j, *_: (j, 0))
    return pl.pallas_call(
        _conv_main_kernel,
        out_shape=jax.ShapeDtypeStruct((N_TOK, D), F32),
        grid_spec=pltpu.PrefetchScalarGridSpec(
            num_scalar_prefetch=3, grid=(NSB,),
            in_specs=[
                sb_spec,
                pl.BlockSpec((HALO, D), lambda j, *_: (jnp.maximum(j * per - 1, 0), 0)),
                pl.BlockSpec((HALO, D), lambda j, *_: (jnp.minimum((j + 1) * per, nh - 1), 0)),
                _full_spec((CONV_WIDTH + 1, D)), _full_spec((1, D)), _full_spec((1, D)), _full_spec((1, D)),
                _full_spec((D, D)), *_pair_specs(SB, NSBP, split), _mod_spec(),
            ],
            out_specs=sb_spec,
            scratch_shapes=[pltpu.VMEM((SB + 2 * HALO, D), F32), pltpu.VMEM((SB, D), F32)]),
        compiler_params=_cparams(),
        name="conv_main",
    )(mrow, jnp.asarray(_SEQ_FIRST), jnp.asarray(_SEQ_LAST), u, u, u, w_dw, b_dw, ln_g, ln_b, w_out, yp, ys, mods)


def _rope_angles():
    rows = DEC_SEQ // GRID_W
    row = jnp.repeat(jnp.arange(rows, dtype=F32), GRID_W)
    col = jnp.tile(jnp.arange(GRID_W, dtype=F32), rows)
    axis_dim = HEAD_DIM // 2
    freqs = jnp.power(ROPE_THETA, -jnp.arange(axis_dim // 2, dtype=F32) * 2.0 / axis_dim)
    ang_r = row[:, None] * freqs[None, :]
    ang_c = col[:, None] * freqs[None, :]
    return jnp.concatenate([ang_r, ang_r, ang_c, ang_c], axis=-1)


def _rope_blocks():
    ang = _rope_angles()
    cos, sin = jnp.cos(ang), jnp.sin(ang)
    lane = np.arange(HEAD_DIM)
    lo = jnp.asarray(((lane % (HEAD_DIM // 2)) < HEAD_DIM // 4).astype(np.float32))
    sin_a = -sin * lo[None, :]
    sin_b = sin * (1.0 - lo)[None, :]
    nblk = DEC_SEQ // SB
    ident = jnp.ones((1, SB, HEAD_DIM), F32)
    zeros = jnp.zeros((1, SB, HEAD_DIM), F32)
    cos_t = jnp.concatenate([ident, cos.reshape(nblk, SB, HEAD_DIM)], axis=0)
    sa_t = jnp.concatenate([zeros, sin_a.reshape(nblk, SB, HEAD_DIM)], axis=0)
    sb_t = jnp.concatenate([zeros, sin_b.reshape(nblk, SB, HEAD_DIM)], axis=0)
    return cos_t, sa_t, sb_t


def _attn_qkv_kernel(mr_ref, ri_ref, y_ref, mod_ref, g_ref, w_ref, qg_ref, kg_ref, cos_ref, sa_ref, sb_ref,
                     q_ref, kb_ref, vb_ref, kf_ref, vf_ref):
    h = _norm_mod(y_ref[...], g_ref[...], mod_ref[0], 0)
    qkv = jnp.dot(h.astype(BF16), w_ref[...], preferred_element_type=F32)
    cos, sa, sb = cos_ref[0], sa_ref[0], sb_ref[0]
    quarter = HEAD_DIM // 4

    def head(x, g):
        xn = _rms(x, g)
        return xn * cos + pltpu.roll(xn, HEAD_DIM - quarter, 1) * sa + pltpu.roll(xn, quarter, 1) * sb

    scale = HEAD_DIM ** -0.5
    for hd in range(N_HEADS):
        sl = slice(hd * HEAD_DIM, (hd + 1) * HEAD_DIM)
        q_ref[:, sl] = (head(qkv[:, sl], qg_ref[...]) * scale).astype(BF16)
    for kv in range(N_KV_HEADS):
        sl = slice(kv * HEAD_DIM, (kv + 1) * HEAD_DIM)
        kr = head(qkv[:, Q_DIM + kv * HEAD_DIM:Q_DIM + (kv + 1) * HEAD_DIM], kg_ref[...])
        kf_ref[:, sl] = kr
        kb_ref[:, sl] = kr.astype(BF16)
    v = qkv[:, Q_DIM + KV_DIM:]
    vf_ref[...] = v
    vb_ref[...] = v.astype(BF16)


def _attn_qkv(y, mods, mrow, g1, w_qkv, q_g, k_g, rope):
    cos_t, sa_t, sb_t = rope
    rspec = pl.BlockSpec((1, SB, HEAD_DIM), lambda j, mr, ri: (ri[j], 0, 0))
    return pl.pallas_call(
        _attn_qkv_kernel,
        out_shape=(jax.ShapeDtypeStruct((N_TOK, Q_DIM), BF16), jax.ShapeDtypeStruct((N_TOK, KV_DIM), BF16),
                   jax.ShapeDtypeStruct((N_TOK, KV_DIM), BF16), jax.ShapeDtypeStruct((N_TOK, KV_DIM), F32),
                   jax.ShapeDtypeStruct((N_TOK, KV_DIM), F32)),
        grid_spec=pltpu.PrefetchScalarGridSpec(
            num_scalar_prefetch=2, grid=(NSB,),
            in_specs=[_tok_spec(D, SB), _mod_spec(), _full_spec((1, D)), _full_spec((D, QKV_DIM)),
                      _full_spec((1, HEAD_DIM)), _full_spec((1, HEAD_DIM)), rspec, rspec, rspec],
            out_specs=(_tok_spec(Q_DIM, SB), _tok_spec(KV_DIM, SB), _tok_spec(KV_DIM, SB), _tok_spec(KV_DIM, SB),
                       _tok_spec(KV_DIM, SB))),
        compiler_params=_cparams(),
        name="attn_qkv",
    )(mrow, jnp.asarray(_ROPE_IDX_SB), y, mods, g1, w_qkv, q_g, k_g, cos_t, sa_t, sb_t)


def _attn_heads(q, ks, vs, o_scr):
    nt = (((1,), (1,)), ((), ()))
    for hd in range(N_HEADS):
        g = hd // GQA_GROUP
        qh = q[:, hd * HEAD_DIM:(hd + 1) * HEAD_DIM]
        gs = slice(g * HEAD_DIM, (g + 1) * HEAD_DIM)
        ss = [lax.dot_general(qh, k[:, gs], nt, preferred_element_type=F32) for k in ks]
        m = functools.reduce(jnp.maximum, [jnp.max(s, axis=-1, keepdims=True) for s in ss])
        ps = [jnp.exp(s - m) for s in ss]
        l = functools.reduce(lambda a, b: a + b, [jnp.sum(p, axis=-1, keepdims=True) for p in ps])
        o = functools.reduce(lambda a, b: a + b,
                             [jnp.dot(p.astype(BF16), v[:, gs], preferred_element_type=F32) for p, v in zip(ps, vs)])
        o_scr[:, hd * HEAD_DIM:(hd + 1) * HEAD_DIM] = (o / l).astype(BF16)


def _attn_ctx_kernel(q_ref, k_ref, v_ref, wo_ref, y_ref, mod_ref, o_ref, o_scr):
    _attn_heads(q_ref[...], [k_ref[...]], [v_ref[...]], o_scr)
    out = jnp.dot(o_scr[...], wo_ref[...], preferred_element_type=F32)
    o_ref[...] = y_ref[...] + mod_ref[0][2:3] * out


def _attn_lat_kernel(q_ref, k_ref, v_ref, ck_ref, cv_ref, wo_ref, y_ref, mod_ref, ctx_out_ref, o_ref, o_scr):
    del ctx_out_ref
    _attn_heads(q_ref[...], [k_ref[...], ck_ref[0].astype(BF16)], [v_ref[...], cv_ref[0].astype(BF16)], o_scr)
    out = jnp.dot(o_scr[...], wo_ref[...], preferred_element_type=F32)
    o_ref[...] = y_ref[...] + mod_ref[0][2:3] * out


def _attention(q, kb, vb, cache_k, cache_v, w_o, y, mods, layer):
    y_ctx = pl.pallas_call(
        _attn_ctx_kernel,
        out_shape=jax.ShapeDtypeStruct((N_TOK, D), F32),
        grid=(BATCH,),
        in_specs=[
            pl.BlockSpec((SEQ, Q_DIM), lambda s: (s, 0)),
            pl.BlockSpec((SEQ, KV_DIM), lambda s: (s, 0)),
            pl.BlockSpec((SEQ, KV_DIM), lambda s: (s, 0)),
            pl.BlockSpec((Q_DIM, D), lambda s: (0, 0)),
            pl.BlockSpec((SEQ, D), lambda s: (s, 0)),
            pl.BlockSpec((1, 6, D), lambda s: (layer * MOD_ROWS, 0, 0)),
        ],
        out_specs=pl.BlockSpec((SEQ, D), lambda s: (s, 0)),
        scratch_shapes=[pltpu.VMEM((SEQ, Q_DIM), BF16)],
        compiler_params=_cparams(),
        name="attn_ctx",
    )(q, kb, vb, w_o, y, mods)
    pb = NP_TOK // DEC_SEQ
    return pl.pallas_call(
        _attn_lat_kernel,
        out_shape=jax.ShapeDtypeStruct((N_TOK, D), F32),
        input_output_aliases={8: 0},
        grid=(DEC_BATCH, SB_PER_DEC),
        in_specs=[
            pl.BlockSpec((SB, Q_DIM), lambda b, t: (NSBP + b * SB_PER_DEC + t, 0)),
            pl.BlockSpec((DEC_SEQ, KV_DIM), lambda b, t: (pb + b, 0)),
            pl.BlockSpec((DEC_SEQ, KV_DIM), lambda b, t: (pb + b, 0)),
            pl.BlockSpec((1, PAST_LEN, KV_DIM), lambda b, t: (b, 0, 0)),
            pl.BlockSpec((1, PAST_LEN, KV_DIM), lambda b, t: (b, 0, 0)),
            pl.BlockSpec((Q_DIM, D), lambda b, t: (0, 0)),
            pl.BlockSpec((SB, D), lambda b, t: (NSBP + b * SB_PER_DEC + t, 0)),
            pl.BlockSpec((1, 6, D), lambda b, t: (layer * MOD_ROWS + 1 + b, 0, 0)),
            pl.BlockSpec(memory_space=pl.ANY),
        ],
        out_specs=pl.BlockSpec((SB, D), lambda b, t: (NSBP + b * SB_PER_DEC + t, 0)),
        scratch_shapes=[pltpu.VMEM((SB, Q_DIM), BF16)],
        compiler_params=_cparams(2),
        name="attn_lat",
    )(q, kb, vb, cache_k, cache_v, w_o, y, mods, y_ctx)


def _log_sigmoid(x):
    return jnp.minimum(x, 0.0) - jnp.log(1.0 + jnp.exp(-jnp.abs(x)))


W_T = 256


def _mlstm_in_kernel(mr_ref, y_ref, mod_ref, g_ref, wt_ref, bg_ref, q_ref, k_ref, v_ref, o_ref, gt_ref, w_ref, wg_ref):
    @pl.when(pl.program_id(0) == 0)
    def _():
        for r in range(0, 4 * D, W_T):
            w_ref[:, r:r + W_T] = jnp.transpose(wt_ref[0, r:r + W_T, :]).astype(BF16)
        n_out = 4 * D + 4 * M_HEADS
        tail = jnp.transpose(wt_ref[0, n_out - LANE:n_out, :])
        lane = lax.broadcasted_iota(I32, tail.shape, 1)
        wg_ref[...] = jnp.where(lane < 4 * M_HEADS, pltpu.roll(tail, 4 * M_HEADS, axis=1), 0.0).astype(BF16)

    h = _norm_mod(y_ref[...], g_ref[...], mod_ref[0], 0)
    hb = h.astype(BF16)
    q_ref[...] = jnp.dot(hb, w_ref[:, 0:D], preferred_element_type=F32).astype(BF16)
    k_ref[...] = (jnp.dot(hb, w_ref[:, D:2 * D], preferred_element_type=F32) * (M_HEAD_DIM ** -0.5)).astype(BF16)
    v_ref[...] = jnp.dot(hb, w_ref[:, 2 * D:3 * D], preferred_element_type=F32).astype(BF16)
    o_ref[...] = _sigmoid(jnp.dot(hb, w_ref[:, 3 * D:4 * D], preferred_element_type=F32))
    gt = jnp.dot(hb, wg_ref[...], preferred_element_type=F32) + bg_ref[...]
    lane = lax.broadcasted_iota(I32, gt.shape, 1)
    is_f = ((lane >= M_HEADS) & (lane < 2 * M_HEADS)) | ((lane >= 3 * M_HEADS) & (lane < 4 * M_HEADS))
    gt_ref[...] = jnp.where(is_f, _log_sigmoid(gt), gt)


def _mlstm_in(y, mods, mrow, g1, w_in_all, slot, b_gate):
    w_t = jnp.swapaxes(w_in_all, 1, 2)
    w_spec = pl.BlockSpec((1,) + w_t.shape[1:], lambda j, *_: (slot, 0, 0), pipeline_mode=pl.Buffered(1))
    return pl.pallas_call(
        _mlstm_in_kernel,
        out_shape=(jax.ShapeDtypeStruct((N_TOK, D), BF16), jax.ShapeDtypeStruct((N_TOK, D), BF16),
                   jax.ShapeDtypeStruct((N_TOK, D), BF16), jax.ShapeDtypeStruct((N_TOK, D), F32),
                   jax.ShapeDtypeStruct((N_TOK, LANE), F32)),
        grid_spec=pltpu.PrefetchScalarGridSpec(
            num_scalar_prefetch=1, grid=(NB,),
            in_specs=[_tok_spec(D), _mod_spec(), _full_spec((1, D)), w_spec, _full_spec((1, LANE))],
            out_specs=(_tok_spec(D), _tok_spec(D), _tok_spec(D), _tok_spec(D), _tok_spec(LANE)),
            scratch_shapes=[pltpu.VMEM((D, 4 * D), BF16), pltpu.VMEM((D, LANE), BF16)]),
        compiler_params=_cparams(),
        name="mlstm_in",
    )(mrow, y, mods, g1, w_t, b_gate)


def _mlstm_load(hd, c, q_ref, k_ref, v_ref, gc_ref, gr_ref):
    r0 = pl.multiple_of(c * M_CHUNK, M_CHUNK)
    hs = slice(hd * M_HEAD_DIM, (hd + 1) * M_HEAD_DIM)
    rows = pl.ds(r0, M_CHUNK)
    return rows, hs, q_ref[rows, hs], k_ref[rows, hs], v_ref[rows, hs], gc_ref[hd, rows, :], gr_ref[hd, c]


def _mlstm_chunks(chains, ms, loaded, c_scr, n_scr):
    L = M_CHUNK
    n = range(len(chains))
    t_idx = lax.broadcasted_iota(I32, (L, L), 0)
    s_idx = lax.broadcasted_iota(I32, (L, L), 1)
    masks = {0: (s_idx <= t_idx, t_idx <= s_idx), 1: (s_idx >= t_idx, t_idx >= s_idx)}
    q = [ld[2] for ld in loaded]
    k = [ld[3] for ld in loaded]
    v = [ld[4] for ld in loaded]
    i_col = [ld[5][:, 2 * d:2 * d + 1] for (_, d), ld in zip(chains, loaded)]
    lf_col = [ld[5][:, 2 * d + 1:2 * d + 2] for (_, d), ld in zip(chains, loaded)]
    i_row = [ld[6][2 * d:2 * d + 1, :] for (_, d), ld in zip(chains, loaded)]
    lf_row = [ld[6][2 * d + 1:2 * d + 2, :] for (_, d), ld in zip(chains, loaded)]
    mask = [masks[d][0] for _, d in chains]
    mask_t = [masks[d][1] for _, d in chains]
    b_col = [jnp.sum(jnp.where(mask[i], lf_row[i], 0.0), axis=1, keepdims=True) for i in n]
    b_row = [jnp.sum(jnp.where(mask_t[i], lf_col[i], 0.0), axis=0, keepdims=True) for i in n]
    log_d = [jnp.where(mask[i], b_col[i] - b_row[i] + i_row[i], -jnp.inf) for i in n]
    li = [b_col[i] + ms[i] for i in n]
    m_r = [jnp.maximum(li[i], jnp.max(log_d[i], axis=1, keepdims=True)) for i in n]
    a_int = [jnp.exp(li[i] - m_r[i]) for i in n]
    dmat = [jnp.exp(log_d[i] - m_r[i]) for i in n]
    cmat = [c_scr[d, hd] for hd, d in chains]
    nvec = [n_scr[d, hd] for hd, d in chains]
    gram = [lax.dot_general(q[i], k[i], (((1,), (1,)), ((), ())), preferred_element_type=F32) for i in n]
    inter = [jnp.dot(q[i], cmat[i].astype(BF16), preferred_element_type=F32) for i in n]
    s = [gram[i] * dmat[i] for i in n]
    intra = [jnp.dot(s[i].astype(BF16), v[i], preferred_element_type=F32) for i in n]
    qn = [jnp.sum(q[i].astype(F32) * nvec[i], axis=1, keepdims=True) for i in n]
    den = [a_int[i] * qn[i] + jnp.sum(s[i], axis=1, keepdims=True) for i in n]
    hh = [(a_int[i] * inter[i] + intra[i]) / jnp.maximum(jnp.abs(den[i]), jnp.exp(-m_r[i])) for i in n]
    b_last = [b_row[i][:, L - 1:L] if chains[i][1] == 0 else b_row[i][:, 0:1] for i in n]
    log_w = [b_last[i] - b_col[i] + i_col[i] for i in n]
    m_new = [jnp.maximum(b_last[i] + ms[i], jnp.max(log_w[i], axis=0, keepdims=True)) for i in n]
    w = [jnp.exp(log_w[i] - m_new[i]) for i in n]
    decay = [jnp.exp(b_last[i] + ms[i] - m_new[i]) for i in n]
    kw = [k[i].astype(F32) * w[i] for i in n]
    kv = [lax.dot_general(kw[i].astype(BF16), v[i], (((0,), (0,)), ((), ())), preferred_element_type=F32) for i in n]
    for i, (hd, d) in enumerate(chains):
        c_scr[d, hd] = decay[i] * cmat[i] + kv[i]
        n_scr[d, hd] = decay[i] * nvec[i] + jnp.sum(kw[i], axis=0, keepdims=True)
    return hh, m_new


def _mlstm_scan_body(n_chunks, q_ref, k_ref, v_ref, gc_ref, gr_ref, h_ref, hb_scr, c_scr, n_scr, m0):
    chains = [(hd, d) for hd in range(M_HEADS) for d in range(2)]

    def body(c, ms):
        loaded = [_mlstm_load(hd, c if d == 0 else n_chunks - 1 - c, q_ref, k_ref, v_ref, gc_ref, gr_ref)
                  for hd, d in chains]
        hh, m_new = _mlstm_chunks(chains, ms, loaded, c_scr, n_scr)
        for (hd, d), ld, h in zip(chains, loaded, hh):
            dst = h_ref if d == 0 else hb_scr
            dst[ld[0], ld[1]] = h
        return tuple(m_new)

    ms = lax.fori_loop(0, n_chunks, body, tuple(m0))
    h_ref[...] += hb_scr[...]
    return ms


def _mlstm_scan_ctx_kernel(q_ref, k_ref, v_ref, gc_ref, gr_ref, h_ref, cn_ref, nn_ref, mn_ref, hb_scr, c_scr, n_scr):
    c_scr[...] = jnp.zeros(c_scr.shape, F32)
    n_scr[...] = jnp.zeros(n_scr.shape, F32)
    zero = jnp.zeros((1, 1), F32)
    ms = _mlstm_scan_body(SEQ // M_CHUNK, q_ref, k_ref, v_ref, gc_ref, gr_ref, h_ref, hb_scr, c_scr, n_scr,
                          [zero] * (2 * M_HEADS))
    cn_ref[0] = c_scr[...]
    nn_ref[0] = n_scr[...]
    for hd in range(M_HEADS):
        for d in range(2):
            mn_ref[0, d, hd] = jnp.broadcast_to(ms[2 * hd + d], (1, LANE))


def _mlstm_scan_lat_kernel(q_ref, k_ref, v_ref, gc_ref, gr_ref, c0_ref, n0_ref, m0_ref, ctx_out_ref, h_ref,
                           hb_scr, c_scr, n_scr):
    del ctx_out_ref
    c_scr[...] = c0_ref[0]
    n_scr[...] = n0_ref[0]
    m0 = [m0_ref[0, d, hd] for hd in range(M_HEADS) for d in range(2)]
    _mlstm_scan_body(DEC_SEQ // M_CHUNK, q_ref, k_ref, v_ref, gc_ref, gr_ref, h_ref, hb_scr, c_scr, n_scr, m0)


def _mlstm_scan(q, k, v, gates, state_c, state_n, state_m):
    g16 = gates[:, :4 * M_HEADS].reshape(N_TOK, 4, M_HEADS)
    gcol = jnp.transpose(g16, (2, 0, 1))
    grow = jnp.transpose(g16.reshape(N_TOK // M_CHUNK, M_CHUNK, 4, M_HEADS), (3, 0, 2, 1))
    hd = M_HEAD_DIM
    state_scratch = [pltpu.VMEM((2, M_HEADS, hd, hd), F32), pltpu.VMEM((2, M_HEADS, 1, hd), F32)]
    ncp = SEQ // M_CHUNK
    h_ctx, new_c, new_n, new_m = pl.pallas_call(
        _mlstm_scan_ctx_kernel,
        out_shape=(jax.ShapeDtypeStruct((N_TOK, D), F32),
                   jax.ShapeDtypeStruct((BATCH, 2, M_HEADS, hd, hd), F32),
                   jax.ShapeDtypeStruct((BATCH, 2, M_HEADS, 1, hd), F32),
                   jax.ShapeDtypeStruct((BATCH, 2, M_HEADS, 1, LANE), F32)),
        grid=(BATCH,),
        in_specs=[
            pl.BlockSpec((SEQ, D), lambda s: (s, 0)),
            pl.BlockSpec((SEQ, D), lambda s: (s, 0)),
            pl.BlockSpec((SEQ, D), lambda s: (s, 0)),
            pl.BlockSpec((M_HEADS, SEQ, 4), lambda s: (0, s, 0)),
            pl.BlockSpec((M_HEADS, ncp, 4, M_CHUNK), lambda s: (0, s, 0, 0)),
        ],
        out_specs=(
            pl.BlockSpec((SEQ, D), lambda s: (s, 0)),
            pl.BlockSpec((1, 2, M_HEADS, hd, hd), lambda s: (s, 0, 0, 0, 0)),
            pl.BlockSpec((1, 2, M_HEADS, 1, hd), lambda s: (s, 0, 0, 0, 0)),
            pl.BlockSpec((1, 2, M_HEADS, 1, LANE), lambda s: (s, 0, 0, 0, 0)),
        ),
        scratch_shapes=[pltpu.VMEM((SEQ, D), F32)] + state_scratch,
        compiler_params=_cparams(),
        name="mlstm_scan_ctx",
    )(q, k, v, gcol, grow)
    ncl = DEC_SEQ // M_CHUNK
    pb = NP_TOK // DEC_SEQ
    h_all = pl.pallas_call(
        _mlstm_scan_lat_kernel,
        out_shape=jax.ShapeDtypeStruct((N_TOK, D), F32),
        input_output_aliases={8: 0},
        grid=(DEC_BATCH,),
        in_specs=[
            pl.BlockSpec((DEC_SEQ, D), lambda b: (pb + b, 0)),
            pl.BlockSpec((DEC_SEQ, D), lambda b: (pb + b, 0)),
            pl.BlockSpec((DEC_SEQ, D), lambda b: (pb + b, 0)),
            pl.BlockSpec((M_HEADS, DEC_SEQ, 4), lambda b: (0, pb + b, 0)),
            pl.BlockSpec((M_HEADS, ncl, 4, M_CHUNK), lambda b: (0, pb + b, 0, 0)),
            pl.BlockSpec((1, 2, M_HEADS, hd, hd), lambda b: (b, 0, 0, 0, 0)),
            pl.BlockSpec((1, 2, M_HEADS, 1, hd), lambda b: (b, 0, 0, 0, 0)),
            pl.BlockSpec((1, 2, M_HEADS, 1, 1), lambda b: (b, 0, 0, 0, 0)),
            pl.BlockSpec(memory_space=pl.ANY),
        ],
        out_specs=pl.BlockSpec((DEC_SEQ, D), lambda b: (pb + b, 0)),
        scratch_shapes=[pltpu.VMEM((DEC_SEQ, D), F32)] + state_scratch,
        compiler_params=_cparams(),
        name="mlstm_scan_lat",
    )(q, k, v, gcol, grow, state_c, state_n, state_m, h_ctx)
    return h_all, new_c, new_n, new_m


def _mlstm_out_kernel(mr_ref, h_ref, o_ref, ng_ref, w_ref, y_ref, mod_ref, out_ref, x_scr):
    hc = o_ref[...] * h_ref[...]
    for hd in range(M_HEADS):
        sl = slice(hd * M_HEAD_DIM, (hd + 1) * M_HEAD_DIM)
        x_scr[:, sl] = _rms(hc[:, sl], ng_ref[:, sl]).astype(BF16)
    out = jnp.dot(x_scr[...], w_ref[...], preferred_element_type=F32)
    out_ref[...] = y_ref[...] + mod_ref[0][2:3] * out


def _mlstm_out(hsum, o, norm_g, w_out, y, mods, mrow):
    return pl.pallas_call(
        _mlstm_out_kernel,
        out_shape=jax.ShapeDtypeStruct((N_TOK, D), F32),
        grid_spec=pltpu.PrefetchScalarGridSpec(
            num_scalar_prefetch=1, grid=(NL,),
            in_specs=[_tok_spec(D, TL), _tok_spec(D, TL), _full_spec((1, D)), _full_spec((D, D)), _tok_spec(D, TL),
                      _mod_spec()],
            out_specs=_tok_spec(D, TL),
            scratch_shapes=[pltpu.VMEM((TL, D), BF16)]),
        compiler_params=_cparams(),
        name="mlstm_out",
    )(mrow, hsum, o, norm_g, w_out, y, mods)


ROUTE_OFF = N_GROUPS
SLAB = D // (2 * LANE)
V7X_SC_CORES = 2
V7X_SC_SUBCORES = 16
SC_WORKERS = V7X_SC_CORES * V7X_SC_SUBCORES
SC_WINDOW = 128
HI_MASK = -65536


def _bf16_bits(x):
    return lax.bitcast_convert_type(x.astype(BF16).astype(F32), I32)


def _store_slabs(ref, x):
    rows = x.shape[0]
    for c in range(SLAB):
        lo = lax.shift_right_logical(_bf16_bits(x[:, (2 * c) * LANE:(2 * c + 1) * LANE]), 16)
        hi = _bf16_bits(x[:, (2 * c + 1) * LANE:(2 * c + 2) * LANE]) & HI_MASK
        ref[pl.ds(c, rows, stride=SLAB), :] = lo | hi


def _load_slabs(ref, dst, rows, dtype):
    for c in range(SLAB):
        w = ref[pl.ds(c, rows, stride=SLAB), :]
        lo = lax.bitcast_convert_type(lax.shift_left(w, 16), F32)
        hi = lax.bitcast_convert_type(w & HI_MASK, F32)
        dst[:, (2 * c) * LANE:(2 * c + 1) * LANE] = lo.astype(dtype)
        dst[:, (2 * c + 1) * LANE:(2 * c + 2) * LANE] = hi.astype(dtype)


def _route_kernel(mr_ref, y_ref, mod_ref, g_ref, wr_ref, br_ref, tri_ref, x_ref, wt_ref, meta_ref, cnt_ref, cnt_scr):
    x = _norm_mod(y_ref[...], g_ref[...], mod_ref[0], 1)
    _store_slabs(x_ref, x)
    lg = jnp.dot(x.astype(BF16), wr_ref[...], preferred_element_type=F32) + br_ref[...]
    lane = lax.broadcasted_iota(I32, lg.shape, 1).astype(F32)
    ninf = -jnp.inf
    big = float(LANE)
    lgg = jnp.where(lane < N_GROUPS, lg, ninf)
    gmax = jnp.max(lgg, axis=-1, keepdims=True)
    g_idx = jnp.min(jnp.where(lgg == gmax, lane, big), axis=-1, keepdims=True)
    g_w = 1.0 / jnp.sum(jnp.exp(lgg - gmax), axis=-1, keepdims=True)
    lo = ROUTE_OFF + g_idx * EXPERTS_PER_GROUP
    le = jnp.where((lane >= lo) & (lane < lo + EXPERTS_PER_GROUP), lg, ninf)
    m1 = jnp.max(le, axis=-1, keepdims=True)
    i1 = jnp.min(jnp.where(le == m1, lane, big), axis=-1, keepdims=True)
    le2 = jnp.where(lane == i1, ninf, le)
    m2 = jnp.max(le2, axis=-1, keepdims=True)
    i2 = jnp.min(jnp.where(le2 == m2, lane, big), axis=-1, keepdims=True)
    r = jnp.exp(m2 - m1)
    p1 = 1.0 / (1.0 + r)
    p2 = r / (1.0 + r)
    two = lax.broadcasted_iota(I32, (x.shape[0], TOP_K), 1)
    wt_ref[...] = jnp.where(two == 0, g_w * p1, g_w * p2)
    @pl.when(pl.program_id(0) == 0)
    def _():
        cnt_scr[...] = jnp.zeros(cnt_scr.shape, F32)

    oh1 = (lane == i1).astype(F32)
    oh2 = (lane == i2).astype(F32)
    both = oh1 + oh2
    before = jnp.dot(tri_ref[...], both.astype(BF16), preferred_element_type=F32) + cnt_scr[...]
    rk1 = jnp.sum(oh1 * before, axis=-1, keepdims=True)
    rk2 = jnp.sum(oh2 * before, axis=-1, keepdims=True)
    cnt_scr[...] = cnt_scr[...] + jnp.sum(both, axis=0, keepdims=True)
    cnt_ref[...] = cnt_scr[...]
    cols = (i1 - ROUTE_OFF, i2 - ROUTE_OFF, rk1, rk2)
    packed = jnp.zeros(lg.shape, F32)
    for c, val in enumerate(cols):
        packed = jnp.where(lane == c, val, packed)
    meta_ref[...] = jnp.transpose(packed)[0:len(cols), :].astype(I32)


def _route(y, mods, mrow, g2, w_route, b_route):
    return pl.pallas_call(
        _route_kernel,
        out_shape=(jax.ShapeDtypeStruct((N_TOK * SLAB, LANE), I32), jax.ShapeDtypeStruct((N_TOK, TOP_K), F32),
                   jax.ShapeDtypeStruct((2 * TOP_K, N_TOK), I32), jax.ShapeDtypeStruct((1, LANE), F32)),
        grid_spec=pltpu.PrefetchScalarGridSpec(
            num_scalar_prefetch=1, grid=(NL,),
            in_specs=[_tok_spec(D, TL), _mod_spec(), _full_spec((1, D)), _full_spec((D, LANE)),
                      _full_spec((1, LANE)), _full_spec((TL, TL))],
            out_specs=(pl.BlockSpec((TL * SLAB, LANE), lambda j, *_: (j, 0)), _tok_spec(TOP_K, TL),
                       pl.BlockSpec((2 * TOP_K, TL), lambda j, *_: (0, j)), _full_spec((1, LANE))),
            scratch_shapes=[pltpu.VMEM((1, LANE), F32)]),
        compiler_params=_cparams(),
        name="moe_route",
    )(mrow, y, mods, g2, w_route, b_route, jnp.asarray(np.tril(np.ones((TL, TL), np.float32), -1), dtype=BF16))


SLOT_COLS = 2048


def _slot_kernel(meta_ref, ps_ref, o_ref):
    sub = lax.broadcasted_iota(I32, (N_EXPERTS, SLOT_COLS), 0)
    meta = meta_ref[...]
    table = ps_ref[...]
    for k in range(TOP_K):
        start = jnp.sum(jnp.where(sub == meta[k:k + 1, :], table, 0), axis=0, keepdims=True)
        o_ref[k:k + 1, :] = start + meta[TOP_K + k:TOP_K + k + 1, :]


def _slots(meta, pad_start):
    return pl.pallas_call(
        _slot_kernel,
        out_shape=jax.ShapeDtypeStruct((TOP_K, N_TOK), I32),
        grid=(N_TOK // SLOT_COLS,),
        in_specs=[pl.BlockSpec((2 * TOP_K, SLOT_COLS), lambda j: (0, j)), pl.BlockSpec((N_EXPERTS, 1), lambda j: (0, 0))],
        out_specs=pl.BlockSpec((TOP_K, SLOT_COLS), lambda j: (0, j)),
        compiler_params=_cparams(),
        name="moe_slots",
    )(meta, pad_start.astype(I32).reshape(N_EXPERTS, 1))


def _dispatch_tables(meta, lane_counts):
    counts = lane_counts[0, ROUTE_OFF:ROUTE_OFF + N_EXPERTS].astype(I32)
    padded = ((counts + EBLK - 1) // EBLK) * EBLK
    pad_end = jnp.cumsum(padded)
    pad_start = pad_end - padded
    dest = _slots(meta, pad_start)
    n_blk = (padded // EBLK).astype(I32)
    blk_start = (pad_start // EBLK).astype(I32)
    n_used = (pad_end[-1] // EBLK).astype(I32).reshape(1)
    return dest, blk_start, n_blk, n_used


def _sc_mesh():
    return plsc.VectorSubcoreMesh(core_axis_name="core", subcore_axis_name="subcore",
                                  num_cores=V7X_SC_CORES, num_subcores=V7X_SC_SUBCORES)


def _sc_worker():
    return lax.axis_index("core") * V7X_SC_SUBCORES + lax.axis_index("subcore")


def _sc_dispatch(x_slabs, d0, d1):
    per = N_TOK // SC_WORKERS

    @functools.partial(
        pl.kernel, out_type=jax.ShapeDtypeStruct((P_SLOTS, SLAB, LANE), I32), mesh=_sc_mesh(), name="moe_dispatch",
        scratch_types=[pltpu.VMEM((1, per), I32), pltpu.VMEM((1, per), I32), pltpu.VMEM((SC_WINDOW, SLAB, LANE), I32)])
    def run(x_hbm, d0_hbm, d1_hbm, o_hbm, i0_v, i1_v, buf):
        base = _sc_worker() * per
        pltpu.sync_copy(d0_hbm.at[:, pl.ds(base, per)], i0_v)
        pltpu.sync_copy(d1_hbm.at[:, pl.ds(base, per)], i1_v)

        @pl.loop(0, per // SC_WINDOW)
        def _(s):
            off = s * SC_WINDOW
            pltpu.sync_copy(x_hbm.at[pl.ds(base + off, SC_WINDOW)], buf)
            pltpu.sync_copy(buf, o_hbm.at[i0_v.at[0, pl.ds(off, SC_WINDOW)]])
            pltpu.sync_copy(buf, o_hbm.at[i1_v.at[0, pl.ds(off, SC_WINDOW)]])

    return run(x_slabs.reshape(N_TOK, SLAB, LANE), d0, d1)


def _sc_collect(y_slabs, dcat):
    per = N_ASSIGN // SC_WORKERS

    @functools.partial(
        pl.kernel, out_type=jax.ShapeDtypeStruct((N_ASSIGN, SLAB, LANE), I32), mesh=_sc_mesh(), name="moe_collect",
        scratch_types=[pltpu.VMEM((1, per), I32), pltpu.VMEM((SC_WINDOW, SLAB, LANE), I32)])
    def run(y_hbm, i_hbm, o_hbm, i_v, buf):
        base = _sc_worker() * per
        pltpu.sync_copy(i_hbm.at[:, pl.ds(base, per)], i_v)

        @pl.loop(0, per // SC_WINDOW)
        def _(s):
            off = s * SC_WINDOW
            pltpu.sync_copy(y_hbm.at[i_v.at[0, pl.ds(off, SC_WINDOW)]], buf)
            pltpu.sync_copy(buf, o_hbm.at[pl.ds(base + off, SC_WINDOW)])

    return run(y_slabs.reshape(P_SLOTS, SLAB, LANE), dcat)


EROWS = EBLK * SLAB


def _expert_kernel(bs_ref, nb_ref, nu_ref, wg_ref, wu_ref, wd_ref, x_hbm, y_hbm,
                   xbuf, ybuf, xs, wg_bf, wu_bf, wd_bf, isem, osem):
    e = pl.program_id(0)
    n_exp = pl.num_programs(0)
    n_used = nu_ref[0]
    b0 = bs_ref[e]
    nb = nb_ref[e]

    def in_copy(g, slot):
        return pltpu.make_async_copy(x_hbm.at[pl.ds(pl.multiple_of(g * EROWS, EROWS), EROWS)], xbuf.at[slot],
                                     isem.at[slot])

    def out_copy(g, slot):
        return pltpu.make_async_copy(ybuf.at[slot], y_hbm.at[pl.ds(pl.multiple_of(g * EROWS, EROWS), EROWS)],
                                     osem.at[slot])

    @pl.when(e == 0)
    def _():
        in_copy(0, 0).start()

    @pl.when(nb > 0)
    def _():
        wg_bf[...] = wg_ref[0, 0].astype(BF16)
        wu_bf[...] = wu_ref[0, 0].astype(BF16)
        wd_bf[...] = wd_ref[0, 0].astype(BF16)

    def block(k, carry):
        g = b0 + k
        slot = lax.rem(g, 2)
        in_copy(g, slot).wait()

        @pl.when(g + 1 < n_used)
        def _():
            in_copy(g + 1, 1 - slot).start()

        _load_slabs(xbuf.at[slot], xs, EBLK, BF16)
        xb = xs[...]
        gt = jnp.dot(xb, wg_bf[...], preferred_element_type=F32)
        up = jnp.dot(xb, wu_bf[...], preferred_element_type=F32)
        hmid = (gt * _sigmoid(gt) * up).astype(BF16)
        res = jnp.dot(hmid, wd_bf[...], preferred_element_type=F32)

        @pl.when(g >= 2)
        def _():
            out_copy(g - 2, slot).wait()

        _store_slabs(ybuf.at[slot], res)
        out_copy(g, slot).start()
        return carry

    lax.fori_loop(0, nb, block, 0)

    @pl.when(e == n_exp - 1)
    def _():
        last = n_used - 1
        out_copy(last, lax.rem(last, 2)).wait()

        @pl.when(n_used >= 2)
        def _():
            out_copy(last - 1, lax.rem(last - 1, 2)).wait()


def _experts(x_sorted, blk_start, n_blk, n_used, w_gate, w_up, w_down, layer):
    any_spec = pl.BlockSpec(memory_space=pl.ANY)
    wspec = lambda r, c: pl.BlockSpec((1, 1, r, c), lambda e, *_: (layer, e, 0, 0))
    return pl.pallas_call(
        _expert_kernel,
        out_shape=jax.ShapeDtypeStruct((P_SLOTS * SLAB, LANE), I32),
        grid_spec=pltpu.PrefetchScalarGridSpec(
            num_scalar_prefetch=3, grid=(N_EXPERTS,),
            in_specs=[wspec(D, D_EXPERT), wspec(D, D_EXPERT), wspec(D_EXPERT, D), any_spec],
            out_specs=any_spec,
            scratch_shapes=[
                pltpu.VMEM((2, EROWS, LANE), I32), pltpu.VMEM((2, EROWS, LANE), I32),
                pltpu.VMEM((EBLK, D), BF16),
                pltpu.VMEM((D, D_EXPERT), BF16), pltpu.VMEM((D, D_EXPERT), BF16), pltpu.VMEM((D_EXPERT, D), BF16),
                pltpu.SemaphoreType.DMA((2,)), pltpu.SemaphoreType.DMA((2,)),
            ]),
        compiler_params=_cparams(),
        name="moe_experts",
    )(blk_start, n_blk, n_used, w_gate, w_up, w_down, x_sorted.reshape(P_SLOTS * SLAB, LANE))


def _combine_kernel(final, mr_ref, e0_ref, e1_ref, wt_ref, y_ref, mod_ref, fg_ref, o_ref, a_scr, b_scr):
    _load_slabs(e0_ref, a_scr, TL, F32)
    _load_slabs(e1_ref, b_scr, TL, F32)
    wt = wt_ref[...]
    moe = wt[:, 0:1] * a_scr[...] + wt[:, 1:2] * b_scr[...]
    y_new = y_ref[...] + mod_ref[0][5:6] * moe
    o_ref[...] = _rms(y_new, fg_ref[...]) if final else y_new


def _combine(ym, wts, y, mods, mrow, final_g, blk0, nblk, final):
    tok = lambda width: pl.BlockSpec((TL, width), lambda j, *_: (blk0 + j, 0))
    slab0 = pl.BlockSpec((TL * SLAB, LANE), lambda j, *_: (blk0 + j, 0))
    slab1 = pl.BlockSpec((TL * SLAB, LANE), lambda j, *_: (NL + blk0 + j, 0))
    mod = pl.BlockSpec((1, 6, D), lambda j, mr: (mr[blk0 + j], 0, 0))
    return pl.pallas_call(
        functools.partial(_combine_kernel, final),
        out_shape=jax.ShapeDtypeStruct((nblk * TL, D), F32),
        grid_spec=pltpu.PrefetchScalarGridSpec(
            num_scalar_prefetch=1, grid=(nblk,),
            in_specs=[slab0, slab1, tok(TOP_K), tok(D), mod, _full_spec((1, D))],
            out_specs=pl.BlockSpec((TL, D), lambda j, *_: (j, 0)),
            scratch_shapes=[pltpu.VMEM((TL, D), F32), pltpu.VMEM((TL, D), F32)]),
        compiler_params=_cparams(),
        name="moe_combine",
    )(mrow, ym, ym, wts, y, mods, final_g)


def kernel(x_prompt, x_sample, cache_attn_k, cache_attn_v, state_mlstm_C, state_mlstm_n, state_mlstm_m, c, c_ctx, ada_w, ada_b, norm1_g, norm2_g, conv_w_in, conv_w_dw, conv_b_dw, conv_ln_g, conv_ln_b, conv_w_out, attn_w_qkv, attn_q_norm, attn_k_norm, attn_w_o, mlstm_w_in, mlstm_b_gate, mlstm_norm_g, mlstm_w_out, moe_w_group, moe_b_group, moe_w_router, moe_b_router, moe_w_gate, moe_w_up, moe_w_down, final_norm_g):
    y = None
    cvec = jnp.concatenate([c_ctx[None, :], c, jnp.zeros((MOD_ROWS - 1 - DEC_BATCH, D), F32)], axis=0)
    mods = _ada_all(cvec, ada_w, ada_b)
    rope = _rope_blocks()
    new_k = new_v = new_c = new_n = new_m = None
    for i in range(DEPTH):
        kind, slot = i % 3, i // 3
        mrow = jnp.asarray(_MOD_ROW + i * MOD_ROWS)
        mrow_sb = jnp.asarray(_MOD_ROW_SB + i * MOD_ROWS)
        mrow_l = jnp.asarray(_MOD_ROW_L + i * MOD_ROWS)
        g1 = norm1_g[i].reshape(1, D)
        if kind == 0:
            src = (x_prompt.reshape(NP_TOK, D), x_sample.reshape(NS_TOK, D), True) if i == 0 else (y, y, False)
            u = _conv_in(*src, mods, mrow_l, g1, conv_w_in[slot].astype(BF16))
            w_dw = jnp.concatenate([conv_w_dw[slot], jnp.zeros((1, D), F32)], axis=0)
            y = _conv_main(u, *src, mods, mrow_sb, w_dw, conv_b_dw[slot].reshape(1, D), conv_ln_g[slot].reshape(1, D),
                           conv_ln_b[slot].reshape(1, D), conv_w_out[slot].astype(BF16))
        elif kind == 1:
            q, kb, vb, kf, vf = _attn_qkv(y, mods, mrow_sb, g1, attn_w_qkv[slot].astype(BF16),
                                          attn_q_norm[slot].reshape(1, HEAD_DIM), attn_k_norm[slot].reshape(1, HEAD_DIM),
                                          rope)
            new_k = kf[:NP_TOK].reshape(BATCH, 1, SEQ, N_KV_HEADS, HEAD_DIM)
            new_v = vf[:NP_TOK].reshape(BATCH, 1, SEQ, N_KV_HEADS, HEAD_DIM)
            ck = cache_attn_k[:, slot].reshape(DEC_BATCH, PAST_LEN, KV_DIM)
            cv = cache_attn_v[:, slot].reshape(DEC_BATCH, PAST_LEN, KV_DIM)
            y = _attention(q, kb, vb, ck, cv, attn_w_o[slot].astype(BF16), y, mods, i)
        else:
            b_gate = jnp.concatenate([mlstm_b_gate[slot], jnp.zeros((LANE - 4 * M_HEADS,), F32)]).reshape(1, LANE)
            q, k, v, o, gates = _mlstm_in(y, mods, mrow, g1, mlstm_w_in, slot, b_gate)
            sc = state_mlstm_C[:, slot]
            sn = state_mlstm_n[:, slot].reshape(DEC_BATCH, 2, M_HEADS, 1, M_HEAD_DIM)
            sm = state_mlstm_m[:, slot].reshape(DEC_BATCH, 2, M_HEADS, 1, 1)
            hsum, nc_, nn_, nm_ = _mlstm_scan(q, k, v, gates, sc, sn, sm)
            new_c = nc_[:, None]
            new_n = nn_.reshape(BATCH, 1, 2, M_HEADS, M_HEAD_DIM)
            new_m = nm_[..., 0, 0].reshape(BATCH, 1, 2, M_HEADS)
            y = _mlstm_out(hsum, o, mlstm_norm_g[slot].reshape(1, D), mlstm_w_out[slot].astype(BF16), y, mods, mrow_l)
        w_route = jnp.concatenate([moe_w_group[i], moe_w_router[i],
                                   jnp.zeros((D, LANE - N_GROUPS - N_EXPERTS), F32)], axis=1)
        b_route = jnp.concatenate([moe_b_group[i], moe_b_router[i],
                                   jnp.zeros((LANE - N_GROUPS - N_EXPERTS,), F32)]).reshape(1, LANE)
        x2, ewt, meta, cnt = _route(y, mods, mrow_l, norm2_g[i].reshape(1, D), w_route.astype(BF16), b_route)
        dest, blk_start, n_blk, n_used = _dispatch_tables(meta, cnt)
        x_sorted = _sc_dispatch(x2, dest[0:1], dest[1:2])
        y_sorted = _experts(x_sorted, blk_start, n_blk, n_used, moe_w_gate, moe_w_up, moe_w_down, i)
        ym = _sc_collect(y_sorted, dest.reshape(1, N_ASSIGN))
        ym = ym.reshape(N_ASSIGN * SLAB, LANE)
        fg = final_norm_g.reshape(1, D)
        if i + 1 < DEPTH:
            y = _combine(ym, ewt, y, mods, mrow_l, fg, 0, NL, False)
        else:
            y_prompt = _combine(ym, ewt, y, mods, mrow_l, fg, 0, NLP, True).reshape(BATCH, SEQ, D)
            y_sample = _combine(ym, ewt, y, mods, mrow_l, fg, NLP, NL - NLP, True).reshape(DEC_BATCH, DEC_SEQ, D)
    return (y_prompt, y_sample, new_k, new_v, new_c, new_n, new_m)
```

```python
import functools

import jax
import jax.numpy as jnp
import numpy as np
from jax import lax
from jax.experimental import pallas as pl
from jax.experimental.pallas import tpu as pltpu
from jax.experimental.pallas import tpu_sc as plsc

F32 = jnp.float32
BF16 = jnp.bfloat16
I32 = jnp.int32

D = 1024
BATCH, SEQ = 16, 256
DEC_BATCH, DEC_SEQ = 8, 1024
PAST_LEN = 256
DEPTH = 4
GRID_W = 64
EPS = 1e-6
CONV_WIDTH = 31
CONV_PAD = CONV_WIDTH // 2
HEAD_DIM = 128
N_HEADS = 8
N_KV_HEADS = 2
GQA_GROUP = N_HEADS // N_KV_HEADS
Q_DIM = N_HEADS * HEAD_DIM
KV_DIM = N_KV_HEADS * HEAD_DIM
QKV_DIM = Q_DIM + 2 * KV_DIM
ROPE_THETA = 10000.0
M_HEADS = 4
M_HEAD_DIM = D // M_HEADS
M_CHUNK = 64
N_GROUPS = 4
EXPERTS_PER_GROUP = 8
N_EXPERTS = N_GROUPS * EXPERTS_PER_GROUP
TOP_K = 2
D_EXPERT = 512

NP_TOK = BATCH * SEQ
NS_TOK = DEC_BATCH * DEC_SEQ
N_TOK = NP_TOK + NS_TOK
TM = 512
NB = N_TOK // TM
NBP = NP_TOK // TM
BLK_PER_DEC = DEC_SEQ // TM
TL = 1024
NL = N_TOK // TL
NLP = NP_TOK // TL
SB = 256
NSB = N_TOK // SB
NSBP = NP_TOK // SB
SB_PER_DEC = DEC_SEQ // SB
MOD_ROWS = 16
HALO = 16
LANE = 128
SUBLANE = 8

N_ASSIGN = N_TOK * TOP_K
EBLK = 256
N_EBLK = N_ASSIGN // EBLK + N_EXPERTS
P_SLOTS = N_EBLK * EBLK
N_PAD_SLOTS = P_SLOTS - N_ASSIGN

VMEM_LIMIT = 56 * 1024 * 1024


def _block_tables(nb, nbp, per_dec):
    j = np.arange(nb)
    is_p = j < nbp
    mod_row = np.where(is_p, 0, 1 + (j - nbp) // per_dec)
    rope_idx = np.where(is_p, 0, 1 + (j - nbp) % per_dec)
    first = np.where(is_p, 1, ((j - nbp) % per_dec == 0).astype(np.int64))
    last = np.where(is_p, 1, ((j - nbp) % per_dec == per_dec - 1).astype(np.int64))
    return (mod_row.astype(np.int32), rope_idx.astype(np.int32), first.astype(np.int32), last.astype(np.int32))


_MOD_ROW, _, _, _ = _block_tables(NB, NBP, BLK_PER_DEC)
_MOD_ROW_L, _, _, _ = _block_tables(NL, NLP, DEC_SEQ // TL)
_MOD_ROW_SB, _ROPE_IDX_SB, _SEQ_FIRST, _SEQ_LAST = _block_tables(NSB, NSBP, SB_PER_DEC)


def _cparams(n_axes=1):
    return pltpu.CompilerParams(dimension_semantics=("arbitrary",) * n_axes, vmem_limit_bytes=VMEM_LIMIT)


def _sigmoid(x):
    return 1.0 / (1.0 + jnp.exp(-x))


def _rms(x, g):
    return x * lax.rsqrt(jnp.mean(x * x, axis=-1, keepdims=True) + EPS) * g


def _norm_mod(y, g, mod, which):
    shift = mod[3 * which:3 * which + 1]
    scale = mod[3 * which + 1:3 * which + 2]
    return _rms(y, g) * (1.0 + scale) + shift


def _ada_kernel(c_ref, w_ref, b_ref, o_ref):
    c = c_ref[...]
    s = c * _sigmoid(c)
    o_ref[0] = jnp.dot(s.astype(BF16), w_ref[0].astype(BF16), preferred_element_type=F32) + b_ref[0]


def _ada_layer(cvec, ada_w, ada_b, layer):
    tn = 1536
    out = pl.pallas_call(
        _ada_kernel,
        out_shape=jax.ShapeDtypeStruct((1, MOD_ROWS, 6 * D), F32),
        grid=(6 * D // tn,),
        in_specs=[
            pl.BlockSpec((MOD_ROWS, D), lambda n: (0, 0)),
            pl.BlockSpec((1, D, tn), lambda n: (layer, 0, n)),
            pl.BlockSpec((1, 1, tn), lambda n: (layer, 0, n)),
        ],
        out_specs=pl.BlockSpec((1, MOD_ROWS, tn), lambda n: (0, 0, n)),
        compiler_params=_cparams(1),
        name="ada_mod",
    )(cvec, ada_w, ada_b.reshape(DEPTH, 1, 6 * D))
    return out.reshape(MOD_ROWS, 6, D)


def _tok_spec(width, rows=TM):
    return pl.BlockSpec((rows, width), lambda j, *_: (j, 0))


def _mod_spec():
    return pl.BlockSpec((1, 6, D), lambda j, mr, *_: (mr[j], 0, 0))


def _full_spec(shape):
    nd = len(shape)
    return pl.BlockSpec(shape, lambda j, *_: (0,) * nd)


def _pair_specs(rows, nbp, split):
    s_off = nbp if split else 0
    return [pl.BlockSpec((rows, D), lambda j, *_: (jnp.minimum(j, nbp - 1), 0)),
            pl.BlockSpec((rows, D), lambda j, *_: (jnp.maximum(j, nbp) - s_off, 0))]


def _pair_block(nbp, yp_ref, ys_ref):
    return jnp.where(pl.program_id(0) < nbp, yp_ref[...], ys_ref[...])


def _conv_in_kernel(mr_ref, yp_ref, ys_ref, mod_ref, g_ref, w_ref, u_ref):
    h = _norm_mod(_pair_block(NLP, yp_ref, ys_ref), g_ref[...], mod_ref[0], 0)
    ag = jnp.dot(h.astype(BF16), w_ref[...], preferred_element_type=F32)
    u_ref[...] = ag[:, :D] * _sigmoid(ag[:, D:])


def _conv_in(yp, ys, split, mods, mrow, g1, w_in):
    return pl.pallas_call(
        _conv_in_kernel,
        out_shape=jax.ShapeDtypeStruct((N_TOK, D), F32),
        grid_spec=pltpu.PrefetchScalarGridSpec(
            num_scalar_prefetch=1, grid=(NL,),
            in_specs=_pair_specs(TL, NLP, split) + [_mod_spec(), _full_spec((1, D)), _full_spec((D, 2 * D))],
            out_specs=_tok_spec(D, TL)),
        compiler_params=_cparams(),
        name="conv_in",
    )(mrow, yp, ys, mods, g1, w_in)


def _conv_main_kernel(mr_ref, first_ref, last_ref, u_ref, up_ref, un_ref, wdw_ref, bdw_ref, lg_ref, lb_ref,
                      wout_ref, yp_ref, ys_ref, mod_ref, o_ref, ext_ref, acc_ref):
    j = pl.program_id(0)
    zero = jnp.zeros((HALO, D), F32)
    ext_ref[0:HALO, :] = jnp.where(first_ref[j] == 1, zero, up_ref[...])
    ext_ref[HALO:HALO + SB, :] = u_ref[...]
    ext_ref[HALO + SB:2 * HALO + SB, :] = jnp.where(last_ref[j] == 1, zero, un_ref[...])

    off0 = HALO - CONV_PAD
    n_a = (off0 + CONV_WIDTH - 1) // SUBLANE + 1
    n_chunks = SB // SUBLANE

    def strip(ci, carry):
        cs = pl.ds(pl.multiple_of(ci * LANE, LANE), LANE)
        wk = [jnp.broadcast_to(wdw_ref[k:k + 1, cs], (SUBLANE, LANE)) for k in range(CONV_WIDTH)]
        bias = jnp.broadcast_to(bdw_ref[:, cs], (SUBLANE, LANE))
        sub = lax.broadcasted_iota(I32, (SUBLANE, LANE), 0)
        prev_rot, prev_v0 = None, None
        for j in range(n_chunks + 1):
            tiles = [ext_ref[SUBLANE * (j + a):SUBLANE * (j + a + 1), cs] for a in range(n_a)]
            part = []
            for s in range(SUBLANE):
                acc = None
                for a in range(n_a):
                    k = SUBLANE * a + s - off0
                    if (0 <= k < CONV_WIDTH) and not (s == 0 and j == n_chunks):
                        term = tiles[a] * wk[k]
                        acc = term if acc is None else acc + term
                part.append(acc)
            rot = [None] + [pltpu.roll(part[s], SUBLANE - s, 0) for s in range(1, SUBLANE)]
            if j >= 1:
                out = prev_v0 + bias
                for s in range(1, SUBLANE):
                    out = out + jnp.where(sub < SUBLANE - s, prev_rot[s], rot[s])
                acc_ref[SUBLANE * (j - 1):SUBLANE * j, cs] = out
            prev_rot, prev_v0 = rot, part[0]
        return carry

    lax.fori_loop(0, D // LANE, strip, 0)

    c = acc_ref[...]
    mu = jnp.mean(c, axis=-1, keepdims=True)
    cc = c - mu
    var = jnp.mean(cc * cc, axis=-1, keepdims=True)
    z = cc * lax.rsqrt(var + EPS) * lg_ref[...] + lb_ref[...]
    z = z * _sigmoid(z)
    out = jnp.dot(z.astype(BF16), wout_ref[...], preferred_element_type=F32)
    o_ref[...] = _pair_block(NSBP, yp_ref, ys_ref) + mod_ref[0][2:3] * out


def _conv_main(u, yp, ys, split, mods, mrow, w_dw, b_dw, ln_g, ln_b, w_out):
    nh = N_TOK // HALO
    per = SB // HALO
    sb_spec = pl.BlockSpec((SB, D), lambda j, *_: (j, 0))
    return pl.pallas_call(
        _conv_main_kernel,
        out_shape=jax.ShapeDtypeStruct((N_TOK, D), F32),
        grid_spec=pltpu.PrefetchScalarGridSpec(
            num_scalar_prefetch=3, grid=(NSB,),
            in_specs=[
                sb_spec,
                pl.BlockSpec((HALO, D), lambda j, *_: (jnp.maximum(j * per - 1, 0), 0)),
                pl.BlockSpec((HALO, D), lambda j, *_: (jnp.minimum((j + 1) * per, nh - 1), 0)),
                _full_spec((CONV_WIDTH + 1, D)), _full_spec((1, D)), _full_spec((1, D)), _full_spec((1, D)),
                _full_spec((D, D)), *_pair_specs(SB, NSBP, split), _mod_spec(),
            ],
            out_specs=sb_spec,
            scratch_shapes=[pltpu.VMEM((SB + 2 * HALO, D), F32), pltpu.VMEM((SB, D), F32)]),
        compiler_params=_cparams(),
        name="conv_main",
    )(mrow, jnp.asarray(_SEQ_FIRST), jnp.asarray(_SEQ_LAST), u, u, u, w_dw, b_dw, ln_g, ln_b, w_out, yp, ys, mods)


def _rope_angles():
    rows = DEC_SEQ // GRID_W
    row = jnp.repeat(jnp.arange(rows, dtype=F32), GRID_W)
    col = jnp.tile(jnp.arange(GRID_W, dtype=F32), rows)
    axis_dim = HEAD_DIM // 2
    freqs = jnp.power(ROPE_THETA, -jnp.arange(axis_dim // 2, dtype=F32) * 2.0 / axis_dim)
    ang_r = row[:, None] * freqs[None, :]
    ang_c = col[:, None] * freqs[None, :]
    return jnp.concatenate([ang_r, ang_r, ang_c, ang_c], axis=-1)


def _rope_blocks():
    ang = _rope_angles()
    cos, sin = jnp.cos(ang), jnp.sin(ang)
    lane = np.arange(HEAD_DIM)
    lo = jnp.asarray(((lane % (HEAD_DIM // 2)) < HEAD_DIM // 4).astype(np.float32))
    sin_a = -sin * lo[None, :]
    sin_b = sin * (1.0 - lo)[None, :]
    nblk = DEC_SEQ // SB
    ident = jnp.ones((1, SB, HEAD_DIM), F32)
    zeros = jnp.zeros((1, SB, HEAD_DIM), F32)
    cos_t = jnp.concatenate([ident, cos.reshape(nblk, SB, HEAD_DIM)], axis=0)
    sa_t = jnp.concatenate([zeros, sin_a.reshape(nblk, SB, HEAD_DIM)], axis=0)
    sb_t = jnp.concatenate([zeros, sin_b.reshape(nblk, SB, HEAD_DIM)], axis=0)
    return cos_t, sa_t, sb_t


def _attn_qkv_kernel(mr_ref, ri_ref, y_ref, mod_ref, g_ref, w_ref, qg_ref, kg_ref, cos_ref, sa_ref, sb_ref,
                     q_ref, kb_ref, vb_ref, kf_ref, vf_ref):
    h = _norm_mod(y_ref[...], g_ref[...], mod_ref[0], 0)
    qkv = jnp.dot(h.astype(BF16), w_ref[...], preferred_element_type=F32)
    cos, sa, sb = cos_ref[0], sa_ref[0], sb_ref[0]
    quarter = HEAD_DIM // 4

    def head(x, g):
        xn = _rms(x, g)
        return xn * cos + pltpu.roll(xn, HEAD_DIM - quarter, 1) * sa + pltpu.roll(xn, quarter, 1) * sb

    scale = HEAD_DIM ** -0.5
    for hd in range(N_HEADS):
        sl = slice(hd * HEAD_DIM, (hd + 1) * HEAD_DIM)
        q_ref[:, sl] = (head(qkv[:, sl], qg_ref[...]) * scale).astype(BF16)
    for kv in range(N_KV_HEADS):
        sl = slice(kv * HEAD_DIM, (kv + 1) * HEAD_DIM)
        kr = head(qkv[:, Q_DIM + kv * HEAD_DIM:Q_DIM + (kv + 1) * HEAD_DIM], kg_ref[...])
        kf_ref[:, sl] = kr
        kb_ref[:, sl] = kr.astype(BF16)
    v = qkv[:, Q_DIM + KV_DIM:]
    vf_ref[...] = v
    vb_ref[...] = v.astype(BF16)


def _attn_qkv(y, mods, mrow, g1, w_qkv, q_g, k_g, rope):
    cos_t, sa_t, sb_t = rope
    rspec = pl.BlockSpec((1, SB, HEAD_DIM), lambda j, mr, ri: (ri[j], 0, 0))
    return pl.pallas_call(
        _attn_qkv_kernel,
        out_shape=(jax.ShapeDtypeStruct((N_TOK, Q_DIM), BF16), jax.ShapeDtypeStruct((N_TOK, KV_DIM), BF16),
                   jax.ShapeDtypeStruct((N_TOK, KV_DIM), BF16), jax.ShapeDtypeStruct((N_TOK, KV_DIM), F32),
                   jax.ShapeDtypeStruct((N_TOK, KV_DIM), F32)),
        grid_spec=pltpu.PrefetchScalarGridSpec(
            num_scalar_prefetch=2, grid=(NSB,),
            in_specs=[_tok_spec(D, SB), _mod_spec(), _full_spec((1, D)), _full_spec((D, QKV_DIM)),
                      _full_spec((1, HEAD_DIM)), _full_spec((1, HEAD_DIM)), rspec, rspec, rspec],
            out_specs=(_tok_spec(Q_DIM, SB), _tok_spec(KV_DIM, SB), _tok_spec(KV_DIM, SB), _tok_spec(KV_DIM, SB),
                       _tok_spec(KV_DIM, SB))),
        compiler_params=_cparams(),
        name="attn_qkv",
    )(mrow, jnp.asarray(_ROPE_IDX_SB), y, mods, g1, w_qkv, q_g, k_g, cos_t, sa_t, sb_t)


def _attn_heads(q, ks, vs, o_scr):
    nt = (((1,), (1,)), ((), ()))
    for hd in range(N_HEADS):
        g = hd // GQA_GROUP
        qh = q[:, hd * HEAD_DIM:(hd + 1) * HEAD_DIM]
        gs = slice(g * HEAD_DIM, (g + 1) * HEAD_DIM)
        ss = [lax.dot_general(qh, k[:, gs], nt, preferred_element_type=F32) for k in ks]
        m = functools.reduce(jnp.maximum, [jnp.max(s, axis=-1, keepdims=True) for s in ss])
        ps = [jnp.exp(s - m) for s in ss]
        l = functools.reduce(lambda a, b: a + b, [jnp.sum(p, axis=-1, keepdims=True) for p in ps])
        o = functools.reduce(lambda a, b: a + b,
                             [jnp.dot(p.astype(BF16), v[:, gs], preferred_element_type=F32) for p, v in zip(ps, vs)])
        o_scr[:, hd * HEAD_DIM:(hd + 1) * HEAD_DIM] = (o / l).astype(BF16)


def _attn_ctx_kernel(q_ref, k_ref, v_ref, wo_ref, y_ref, mod_ref, o_ref, o_scr):
    _attn_heads(q_ref[...], [k_ref[...]], [v_ref[...]], o_scr)
    out = jnp.dot(o_scr[...], wo_ref[...], preferred_element_type=F32)
    o_ref[...] = y_ref[...] + mod_ref[0][2:3] * out


def _attn_lat_kernel(q_ref, k_ref, v_ref, ck_ref, cv_ref, wo_ref, y_ref, mod_ref, ctx_out_ref, o_ref, o_scr):
    del ctx_out_ref
    _attn_heads(q_ref[...], [k_ref[...], ck_ref[0].astype(BF16)], [v_ref[...], cv_ref[0].astype(BF16)], o_scr)
    out = jnp.dot(o_scr[...], wo_ref[...], preferred_element_type=F32)
    o_ref[...] = y_ref[...] + mod_ref[0][2:3] * out


def _attention(q, kb, vb, cache_k, cache_v, w_o, y, mods):
    y_ctx = pl.pallas_call(
        _attn_ctx_kernel,
        out_shape=jax.ShapeDtypeStruct((N_TOK, D), F32),
        grid=(BATCH,),
        in_specs=[
            pl.BlockSpec((SEQ, Q_DIM), lambda s: (s, 0)),
            pl.BlockSpec((SEQ, KV_DIM), lambda s: (s, 0)),
            pl.BlockSpec((SEQ, KV_DIM), lambda s: (s, 0)),
            pl.BlockSpec((Q_DIM, D), lambda s: (0, 0)),
            pl.BlockSpec((SEQ, D), lambda s: (s, 0)),
            pl.BlockSpec((1, 6, D), lambda s: (0, 0, 0)),
        ],
        out_specs=pl.BlockSpec((SEQ, D), lambda s: (s, 0)),
        scratch_shapes=[pltpu.VMEM((SEQ, Q_DIM), BF16)],
        compiler_params=_cparams(),
        name="attn_ctx",
    )(q, kb, vb, w_o, y, mods)
    pb = NP_TOK // DEC_SEQ
    return pl.pallas_call(
        _attn_lat_kernel,
        out_shape=jax.ShapeDtypeStruct((N_TOK, D), F32),
        input_output_aliases={8: 0},
        grid=(DEC_BATCH, SB_PER_DEC),
        in_specs=[
            pl.BlockSpec((SB, Q_DIM), lambda b, t: (NSBP + b * SB_PER_DEC + t, 0)),
            pl.BlockSpec((DEC_SEQ, KV_DIM), lambda b, t: (pb + b, 0)),
            pl.BlockSpec((DEC_SEQ, KV_DIM), lambda b, t: (pb + b, 0)),
            pl.BlockSpec((1, PAST_LEN, KV_DIM), lambda b, t: (b, 0, 0)),
            pl.BlockSpec((1, PAST_LEN, KV_DIM), lambda b, t: (b, 0, 0)),
            pl.BlockSpec((Q_DIM, D), lambda b, t: (0, 0)),
            pl.BlockSpec((SB, D), lambda b, t: (NSBP + b * SB_PER_DEC + t, 0)),
            pl.BlockSpec((1, 6, D), lambda b, t: (1 + b, 0, 0)),
            pl.BlockSpec(memory_space=pl.ANY),
        ],
        out_specs=pl.BlockSpec((SB, D), lambda b, t: (NSBP + b * SB_PER_DEC + t, 0)),
        scratch_shapes=[pltpu.VMEM((SB, Q_DIM), BF16)],
        compiler_params=_cparams(2),
        name="attn_lat",
    )(q, kb, vb, cache_k, cache_v, w_o, y, mods, y_ctx)


def _log_sigmoid(x):
    return jnp.minimum(x, 0.0) - jnp.log(1.0 + jnp.exp(-jnp.abs(x)))


W_T = 256


def _mlstm_in_kernel(mr_ref, y_ref, mod_ref, g_ref, wt_ref, bg_ref, q_ref, k_ref, v_ref, o_ref, gt_ref, gr_ref,
                     w_ref, wg_ref):
    @pl.when(pl.program_id(0) == 0)
    def _():
        for r in range(0, 4 * D, W_T):
            w_ref[:, r:r + W_T] = jnp.transpose(wt_ref[0, r:r + W_T, :]).astype(BF16)
        n_out = 4 * D + 4 * M_HEADS
        tail = jnp.transpose(wt_ref[0, n_out - LANE:n_out, :])
        lane = lax.broadcasted_iota(I32, tail.shape, 1)
        wg_ref[...] = jnp.where(lane < 4 * M_HEADS, pltpu.roll(tail, 4 * M_HEADS, axis=1), 0.0).astype(BF16)

    h = _norm_mod(y_ref[...], g_ref[...], mod_ref[0], 0)
    hb = h.astype(BF16)
    q_ref[...] = jnp.dot(hb, w_ref[:, 0:D], preferred_element_type=F32).astype(BF16)
    k_ref[...] = (jnp.dot(hb, w_ref[:, D:2 * D], preferred_element_type=F32) * (M_HEAD_DIM ** -0.5)).astype(BF16)
    v_ref[...] = jnp.dot(hb, w_ref[:, 2 * D:3 * D], preferred_element_type=F32).astype(BF16)
    o_ref[...] = _sigmoid(jnp.dot(hb, w_ref[:, 3 * D:4 * D], preferred_element_type=F32))
    gt = jnp.dot(hb, wg_ref[...], preferred_element_type=F32) + bg_ref[...]
    lane = lax.broadcasted_iota(I32, gt.shape, 1)
    is_f = ((lane >= M_HEADS) & (lane < 2 * M_HEADS)) | ((lane >= 3 * M_HEADS) & (lane < 4 * M_HEADS))
    gt = jnp.where(is_f, _log_sigmoid(gt), gt)
    gt_ref[...] = gt
    per = LANE // M_CHUNK
    for p in range(TM // LANE):
        t = jnp.transpose(gt[p * LANE:(p + 1) * LANE, :])
        for f in range(per):
            gr_ref[p * per + f] = t[0:4 * M_HEADS, f * M_CHUNK:(f + 1) * M_CHUNK]


def _mlstm_in(y, mods, mrow, g1, w_in_all, slot, b_gate):
    w_t = jnp.swapaxes(w_in_all, 1, 2)
    w_spec = pl.BlockSpec((1,) + w_t.shape[1:], lambda j, *_: (slot, 0, 0), pipeline_mode=pl.Buffered(1))
    return pl.pallas_call(
        _mlstm_in_kernel,
        out_shape=(jax.ShapeDtypeStruct((N_TOK, D), BF16), jax.ShapeDtypeStruct((N_TOK, D), BF16),
                   jax.ShapeDtypeStruct((N_TOK, D), BF16), jax.ShapeDtypeStruct((N_TOK, D), F32),
                   jax.ShapeDtypeStruct((N_TOK, LANE), F32),
                   jax.ShapeDtypeStruct((N_TOK // M_CHUNK, 4 * M_HEADS, M_CHUNK), F32)),
        grid_spec=pltpu.PrefetchScalarGridSpec(
            num_scalar_prefetch=1, grid=(NB,),
            in_specs=[_tok_spec(D), _mod_spec(), _full_spec((1, D)), w_spec, _full_spec((1, LANE))],
            out_specs=(_tok_spec(D), _tok_spec(D), _tok_spec(D), _tok_spec(D), _tok_spec(LANE),
                       pl.BlockSpec((TM // M_CHUNK, 4 * M_HEADS, M_CHUNK), lambda j, *_: (j, 0, 0))),
            scratch_shapes=[pltpu.VMEM((D, 4 * D), BF16), pltpu.VMEM((D, LANE), BF16)]),
        compiler_params=_cparams(),
        name="mlstm_in",
    )(mrow, y, mods, g1, w_t, b_gate)


def _mlstm_load(hd, c, q_ref, k_ref, v_ref, gc_ref, gr_ref):
    r0 = pl.multiple_of(c * M_CHUNK, M_CHUNK)
    hs = slice(hd * M_HEAD_DIM, (hd + 1) * M_HEAD_DIM)
    rows = pl.ds(r0, M_CHUNK)
    return rows, hs, q_ref[rows, hs], k_ref[rows, hs], v_ref[rows, hs], gc_ref[rows, :], gr_ref[c]


def _mlstm_chunks(chains, ms, loaded, c_scr, n_scr):
    L = M_CHUNK
    n = range(len(chains))
    t_idx = lax.broadcasted_iota(I32, (L, L), 0)
    s_idx = lax.broadcasted_iota(I32, (L, L), 1)
    masks = {0: (s_idx <= t_idx, t_idx <= s_idx), 1: (s_idx >= t_idx, t_idx >= s_idx)}
    q = [ld[2] for ld in loaded]
    k = [ld[3] for ld in loaded]
    v = [ld[4] for ld in loaded]
    gi = [2 * d * M_HEADS + hd for hd, d in chains]
    gf = [(2 * d + 1) * M_HEADS + hd for hd, d in chains]
    i_col = [ld[5][:, gi[i]:gi[i] + 1] for i, ld in enumerate(loaded)]
    lf_col = [ld[5][:, gf[i]:gf[i] + 1] for i, ld in enumerate(loaded)]
    i_row = [ld[6][gi[i]:gi[i] + 1, :] for i, ld in enumerate(loaded)]
    lf_row = [ld[6][gf[i]:gf[i] + 1, :] for i, ld in enumerate(loaded)]
    mask = [masks[d][0] for _, d in chains]
    mask_t = [masks[d][1] for _, d in chains]
    b_col = [jnp.sum(jnp.where(mask[i], lf_row[i], 0.0), axis=1, keepdims=True) for i in n]
    b_row = [jnp.sum(jnp.where(mask_t[i], lf_col[i], 0.0), axis=0, keepdims=True) for i in n]
    log_d = [jnp.where(mask[i], b_col[i] - b_row[i] + i_row[i], -jnp.inf) for i in n]
    li = [b_col[i] + ms[i] for i in n]
    m_r = [jnp.maximum(li[i], jnp.max(log_d[i], axis=1, keepdims=True)) for i in n]
    a_int = [jnp.exp(li[i] - m_r[i]) for i in n]
    dmat = [jnp.exp(log_d[i] - m_r[i]) for i in n]
    cmat = [c_scr[d, hd] for hd, d in chains]
    nvec = [n_scr[d, hd] for hd, d in chains]
    gram = [lax.dot_general(q[i], k[i], (((1,), (1,)), ((), ())), preferred_element_type=F32) for i in n]
    inter = [jnp.dot(q[i], cmat[i].astype(BF16), preferred_element_type=F32) for i in n]
    s = [gram[i] * dmat[i] for i in n]
    intra = [jnp.dot(s[i].astype(BF16), v[i], preferred_element_type=F32) for i in n]
    qn = [jnp.sum(q[i].astype(F32) * nvec[i], axis=1, keepdims=True) for i in n]
    den = [a_int[i] * qn[i] + jnp.sum(s[i], axis=1, keepdims=True) for i in n]
    hh = [(a_int[i] * inter[i] + intra[i]) / jnp.maximum(jnp.abs(den[i]), jnp.exp(-m_r[i])) for i in n]
    b_last = [b_row[i][:, L - 1:L] if chains[i][1] == 0 else b_row[i][:, 0:1] for i in n]
    log_w = [b_last[i] - b_col[i] + i_col[i] for i in n]
    m_new = [jnp.maximum(b_last[i] + ms[i], jnp.max(log_w[i], axis=0, keepdims=True)) for i in n]
    w = [jnp.exp(log_w[i] - m_new[i]) for i in n]
    decay = [jnp.exp(b_last[i] + ms[i] - m_new[i]) for i in n]
    kw = [k[i].astype(F32) * w[i] for i in n]
    kv = [lax.dot_general(kw[i].astype(BF16), v[i], (((0,), (0,)), ((), ())), preferred_element_type=F32) for i in n]
    for i, (hd, d) in enumerate(chains):
        c_scr[d, hd] = decay[i] * cmat[i] + kv[i]
        n_scr[d, hd] = decay[i] * nvec[i] + jnp.sum(kw[i], axis=0, keepdims=True)
    return hh, m_new


def _mlstm_scan_body(n_chunks, q_ref, k_ref, v_ref, gc_ref, gr_ref, h_ref, hb_scr, c_scr, n_scr, m0):
    chains = [(hd, d) for hd in range(M_HEADS) for d in range(2)]

    def body(c, ms):
        loaded = [_mlstm_load(hd, c if d == 0 else n_chunks - 1 - c, q_ref, k_ref, v_ref, gc_ref, gr_ref)
                  for hd, d in chains]
        hh, m_new = _mlstm_chunks(chains, ms, loaded, c_scr, n_scr)
        for (hd, d), ld, h in zip(chains, loaded, hh):
            dst = h_ref if d == 0 else hb_scr
            dst[ld[0], ld[1]] = h
        return tuple(m_new)

    ms = lax.fori_loop(0, n_chunks, body, tuple(m0))
    h_ref[...] += hb_scr[...]
    return ms


def _mlstm_scan_ctx_kernel(q_ref, k_ref, v_ref, gc_ref, gr_ref, h_ref, cn_ref, nn_ref, mn_ref, hb_scr, c_scr, n_scr):
    c_scr[...] = jnp.zeros(c_scr.shape, F32)
    n_scr[...] = jnp.zeros(n_scr.shape, F32)
    zero = jnp.zeros((1, 1), F32)
    ms = _mlstm_scan_body(SEQ // M_CHUNK, q_ref, k_ref, v_ref, gc_ref, gr_ref, h_ref, hb_scr, c_scr, n_scr,
                          [zero] * (2 * M_HEADS))
    cn_ref[0] = c_scr[...]
    nn_ref[0] = n_scr[...]
    for hd in range(M_HEADS):
        for d in range(2):
            mn_ref[0, d, hd] = jnp.broadcast_to(ms[2 * hd + d], (1, LANE))


def _mlstm_scan_lat_kernel(q_ref, k_ref, v_ref, gc_ref, gr_ref, c0_ref, n0_ref, m0_ref, ctx_out_ref, h_ref,
                           hb_scr, c_scr, n_scr):
    del ctx_out_ref
    c_scr[...] = c0_ref[0]
    n_scr[...] = n0_ref[0]
    m0 = [m0_ref[0, d, hd] for hd in range(M_HEADS) for d in range(2)]
    _mlstm_scan_body(DEC_SEQ // M_CHUNK, q_ref, k_ref, v_ref, gc_ref, gr_ref, h_ref, hb_scr, c_scr, n_scr, m0)


def _mlstm_scan(q, k, v, gcol, grow, state_c, state_n, state_m):
    hd = M_HEAD_DIM
    ng = 4 * M_HEADS
    state_scratch = [pltpu.VMEM((2, M_HEADS, hd, hd), F32), pltpu.VMEM((2, M_HEADS, 1, hd), F32)]
    ncp = SEQ // M_CHUNK
    h_ctx, new_c, new_n, new_m = pl.pallas_call(
        _mlstm_scan_ctx_kernel,
        out_shape=(jax.ShapeDtypeStruct((N_TOK, D), F32),
                   jax.ShapeDtypeStruct((BATCH, 2, M_HEADS, hd, hd), F32),
                   jax.ShapeDtypeStruct((BATCH, 2, M_HEADS, 1, hd), F32),
                   jax.ShapeDtypeStruct((BATCH, 2, M_HEADS, 1, LANE), F32)),
        grid=(BATCH,),
        in_specs=[
            pl.BlockSpec((SEQ, D), lambda s: (s, 0)),
            pl.BlockSpec((SEQ, D), lambda s: (s, 0)),
            pl.BlockSpec((SEQ, D), lambda s: (s, 0)),
            pl.BlockSpec((SEQ, LANE), lambda s: (s, 0)),
            pl.BlockSpec((ncp, ng, M_CHUNK), lambda s: (s, 0, 0)),
        ],
        out_specs=(
            pl.BlockSpec((SEQ, D), lambda s: (s, 0)),
            pl.BlockSpec((1, 2, M_HEADS, hd, hd), lambda s: (s, 0, 0, 0, 0)),
            pl.BlockSpec((1, 2, M_HEADS, 1, hd), lambda s: (s, 0, 0, 0, 0)),
            pl.BlockSpec((1, 2, M_HEADS, 1, LANE), lambda s: (s, 0, 0, 0, 0)),
        ),
        scratch_shapes=[pltpu.VMEM((SEQ, D), F32)] + state_scratch,
        compiler_params=_cparams(),
        name="mlstm_scan_ctx",
    )(q, k, v, gcol, grow)
    ncl = DEC_SEQ // M_CHUNK
    pb = NP_TOK // DEC_SEQ
    h_all = pl.pallas_call(
        _mlstm_scan_lat_kernel,
        out_shape=jax.ShapeDtypeStruct((N_TOK, D), F32),
        input_output_aliases={8: 0},
        grid=(DEC_BATCH,),
        in_specs=[
            pl.BlockSpec((DEC_SEQ, D), lambda b: (pb + b, 0)),
            pl.BlockSpec((DEC_SEQ, D), lambda b: (pb + b, 0)),
            pl.BlockSpec((DEC_SEQ, D), lambda b: (pb + b, 0)),
            pl.BlockSpec((DEC_SEQ, LANE), lambda b: (pb + b, 0)),
            pl.BlockSpec((ncl, ng, M_CHUNK), lambda b: (pb + b, 0, 0)),
            pl.BlockSpec((1, 2, M_HEADS, hd, hd), lambda b: (b, 0, 0, 0, 0)),
            pl.BlockSpec((1, 2, M_HEADS, 1, hd), lambda b: (b, 0, 0, 0, 0)),
            pl.BlockSpec((1, 2, M_HEADS, 1, 1), lambda b: (b, 0, 0, 0, 0)),
            pl.BlockSpec(memory_space=pl.ANY),
        ],
        out_specs=pl.BlockSpec((DEC_SEQ, D), lambda b: (pb + b, 0)),
        scratch_shapes=[pltpu.VMEM((DEC_SEQ, D), F32)] + state_scratch,
        compiler_params=_cparams(),
        name="mlstm_scan_lat",
    )(q, k, v, gcol, grow, state_c, state_n, state_m, h_ctx)
    return h_all, new_c, new_n, new_m


def _mlstm_out_kernel(mr_ref, h_ref, o_ref, ng_ref, w_ref, y_ref, mod_ref, out_ref, x_scr):
    hc = o_ref[...] * h_ref[...]
    for hd in range(M_HEADS):
        sl = slice(hd * M_HEAD_DIM, (hd + 1) * M_HEAD_DIM)
        x_scr[:, sl] = _rms(hc[:, sl], ng_ref[:, sl]).astype(BF16)
    out = jnp.dot(x_scr[...], w_ref[...], preferred_element_type=F32)
    out_ref[...] = y_ref[...] + mod_ref[0][2:3] * out


def _mlstm_out(hsum, o, norm_g, w_out, y, mods, mrow):
    return pl.pallas_call(
        _mlstm_out_kernel,
        out_shape=jax.ShapeDtypeStruct((N_TOK, D), F32),
        grid_spec=pltpu.PrefetchScalarGridSpec(
            num_scalar_prefetch=1, grid=(NL,),
            in_specs=[_tok_spec(D, TL), _tok_spec(D, TL), _full_spec((1, D)), _full_spec((D, D)), _tok_spec(D, TL),
                      _mod_spec()],
            out_specs=_tok_spec(D, TL),
            scratch_shapes=[pltpu.VMEM((TL, D), BF16)]),
        compiler_params=_cparams(),
        name="mlstm_out",
    )(mrow, hsum, o, norm_g, w_out, y, mods)


ROUTE_OFF = N_GROUPS
SLAB = D // (2 * LANE)
V7X_SC_CORES = 2
V7X_SC_SUBCORES = 16
SC_WORKERS = V7X_SC_CORES * V7X_SC_SUBCORES
SC_WINDOW = 128
HI_MASK = -65536


def _bf16_bits(x):
    return lax.bitcast_convert_type(x.astype(BF16).astype(F32), I32)


def _store_slabs(ref, x):
    rows = x.shape[0]
    for c in range(SLAB):
        lo = lax.shift_right_logical(_bf16_bits(x[:, (2 * c) * LANE:(2 * c + 1) * LANE]), 16)
        hi = _bf16_bits(x[:, (2 * c + 1) * LANE:(2 * c + 2) * LANE]) & HI_MASK
        ref[pl.ds(c, rows, stride=SLAB), :] = lo | hi


def _load_slabs(ref, dst, rows, dtype):
    for c in range(SLAB):
        w = ref[pl.ds(c, rows, stride=SLAB), :]
        lo = lax.bitcast_convert_type(lax.shift_left(w, 16), F32)
        hi = lax.bitcast_convert_type(w & HI_MASK, F32)
        dst[:, (2 * c) * LANE:(2 * c + 1) * LANE] = lo.astype(dtype)
        dst[:, (2 * c + 1) * LANE:(2 * c + 2) * LANE] = hi.astype(dtype)


def _route_kernel(mr_ref, y_ref, mod_ref, g_ref, wr_ref, br_ref, tri_ref, x_ref, wt_ref, meta_ref, cnt_ref, cnt_scr):
    x = _norm_mod(y_ref[...], g_ref[...], mod_ref[0], 1)
    _store_slabs(x_ref, x)
    lg = jnp.dot(x.astype(BF16), wr_ref[...], preferred_element_type=F32) + br_ref[...]
    lane = lax.broadcasted_iota(I32, lg.shape, 1).astype(F32)
    ninf = -jnp.inf
    big = float(LANE)
    lgg = jnp.where(lane < N_GROUPS, lg, ninf)
    gmax = jnp.max(lgg, axis=-1, keepdims=True)
    g_idx = jnp.min(jnp.where(lgg == gmax, lane, big), axis=-1, keepdims=True)
    g_w = 1.0 / jnp.sum(jnp.exp(lgg - gmax), axis=-1, keepdims=True)
    lo = ROUTE_OFF + g_idx * EXPERTS_PER_GROUP
    le = jnp.where((lane >= lo) & (lane < lo + EXPERTS_PER_GROUP), lg, ninf)
    m1 = jnp.max(le, axis=-1, keepdims=True)
    i1 = jnp.min(jnp.where(le == m1, lane, big), axis=-1, keepdims=True)
    le2 = jnp.where(lane == i1, ninf, le)
    m2 = jnp.max(le2, axis=-1, keepdims=True)
    i2 = jnp.min(jnp.where(le2 == m2, lane, big), axis=-1, keepdims=True)
    r = jnp.exp(m2 - m1)
    p1 = 1.0 / (1.0 + r)
    p2 = r / (1.0 + r)
    two = lax.broadcasted_iota(I32, (x.shape[0], TOP_K), 1)
    wt_ref[...] = jnp.where(two == 0, g_w * p1, g_w * p2)
    @pl.when(pl.program_id(0) == 0)
    def _():
        cnt_scr[...] = jnp.zeros(cnt_scr.shape, F32)

    oh1 = (lane == i1).astype(F32)
    oh2 = (lane == i2).astype(F32)
    both = oh1 + oh2
    before = jnp.dot(tri_ref[...], both.astype(BF16), preferred_element_type=F32) + cnt_scr[...]
    rk1 = jnp.sum(oh1 * before, axis=-1, keepdims=True)
    rk2 = jnp.sum(oh2 * before, axis=-1, keepdims=True)
    cnt_scr[...] = cnt_scr[...] + jnp.sum(both, axis=0, keepdims=True)
    cnt_ref[...] = cnt_scr[...]
    cols = (i1 - ROUTE_OFF, i2 - ROUTE_OFF, rk1, rk2)
    packed = jnp.zeros(lg.shape, F32)
    for c, val in enumerate(cols):
        packed = jnp.where(lane == c, val, packed)
    meta_ref[...] = jnp.transpose(packed)[0:len(cols), :].astype(I32)


def _route(y, mods, mrow, g2, w_route, b_route):
    return pl.pallas_call(
        _route_kernel,
        out_shape=(jax.ShapeDtypeStruct((N_TOK * SLAB, LANE), I32), jax.ShapeDtypeStruct((N_TOK, TOP_K), F32),
                   jax.ShapeDtypeStruct((2 * TOP_K, N_TOK), I32), jax.ShapeDtypeStruct((1, LANE), F32)),
        grid_spec=pltpu.PrefetchScalarGridSpec(
            num_scalar_prefetch=1, grid=(NL,),
            in_specs=[_tok_spec(D, TL), _mod_spec(), _full_spec((1, D)), _full_spec((D, LANE)),
                      _full_spec((1, LANE)), _full_spec((TL, TL))],
            out_specs=(pl.BlockSpec((TL * SLAB, LANE), lambda j, *_: (j, 0)), _tok_spec(TOP_K, TL),
                       pl.BlockSpec((2 * TOP_K, TL), lambda j, *_: (0, j)), _full_spec((1, LANE))),
            scratch_shapes=[pltpu.VMEM((1, LANE), F32)]),
        compiler_params=_cparams(),
        name="moe_route",
    )(mrow, y, mods, g2, w_route, b_route, jnp.asarray(np.tril(np.ones((TL, TL), np.float32), -1), dtype=BF16))


SLOT_COLS = 2048


def _slot_kernel(meta_ref, ps_ref, o_ref):
    sub = lax.broadcasted_iota(I32, (N_EXPERTS, SLOT_COLS), 0)
    meta = meta_ref[...]
    table = ps_ref[...]
    for k in range(TOP_K):
        start = jnp.sum(jnp.where(sub == meta[k:k + 1, :], table, 0), axis=0, keepdims=True)
        o_ref[k:k + 1, :] = start + meta[TOP_K + k:TOP_K + k + 1, :]


def _slots(meta, pad_start):
    return pl.pallas_call(
        _slot_kernel,
        out_shape=jax.ShapeDtypeStruct((TOP_K, N_TOK), I32),
        grid=(N_TOK // SLOT_COLS,),
        in_specs=[pl.BlockSpec((2 * TOP_K, SLOT_COLS), lambda j: (0, j)), pl.BlockSpec((N_EXPERTS, 1), lambda j: (0, 0))],
        out_specs=pl.BlockSpec((TOP_K, SLOT_COLS), lambda j: (0, j)),
        compiler_params=_cparams(),
        name="moe_slots",
    )(meta, pad_start.astype(I32).reshape(N_EXPERTS, 1))


def _dispatch_tables(meta, lane_counts):
    counts = lane_counts[0, ROUTE_OFF:ROUTE_OFF + N_EXPERTS].astype(I32)
    padded = ((counts + EBLK - 1) // EBLK) * EBLK
    pad_end = jnp.cumsum(padded)
    pad_start = pad_end - padded
    dest = _slots(meta, pad_start)
    n_blk = (padded // EBLK).astype(I32)
    blk_start = (pad_start // EBLK).astype(I32)
    n_used = (pad_end[-1] // EBLK).astype(I32).reshape(1)
    return dest, blk_start, n_blk, n_used


def _sc_mesh():
    return plsc.VectorSubcoreMesh(core_axis_name="core", subcore_axis_name="subcore",
                                  num_cores=V7X_SC_CORES, num_subcores=V7X_SC_SUBCORES)


def _sc_worker():
    return lax.axis_index("core") * V7X_SC_SUBCORES + lax.axis_index("subcore")


def _sc_dispatch(x_slabs, d0, d1):
    per = N_TOK // SC_WORKERS

    @functools.partial(
        pl.kernel, out_type=jax.ShapeDtypeStruct((P_SLOTS, SLAB, LANE), I32), mesh=_sc_mesh(), name="moe_dispatch",
        scratch_types=[pltpu.VMEM((1, per), I32), pltpu.VMEM((1, per), I32), pltpu.VMEM((SC_WINDOW, SLAB, LANE), I32)])
    def run(x_hbm, d0_hbm, d1_hbm, o_hbm, i0_v, i1_v, buf):
        base = _sc_worker() * per
        pltpu.sync_copy(d0_hbm.at[:, pl.ds(base, per)], i0_v)
        pltpu.sync_copy(d1_hbm.at[:, pl.ds(base, per)], i1_v)

        @pl.loop(0, per // SC_WINDOW)
        def _(s):
            off = s * SC_WINDOW
            pltpu.sync_copy(x_hbm.at[pl.ds(base + off, SC_WINDOW)], buf)
            pltpu.sync_copy(buf, o_hbm.at[i0_v.at[0, pl.ds(off, SC_WINDOW)]])
            pltpu.sync_copy(buf, o_hbm.at[i1_v.at[0, pl.ds(off, SC_WINDOW)]])

    return run(x_slabs.reshape(N_TOK, SLAB, LANE), d0, d1)


def _sc_collect(y_slabs, dcat):
    per = N_ASSIGN // SC_WORKERS

    @functools.partial(
        pl.kernel, out_type=jax.ShapeDtypeStruct((N_ASSIGN, SLAB, LANE), I32), mesh=_sc_mesh(), name="moe_collect",
        scratch_types=[pltpu.VMEM((1, per), I32), pltpu.VMEM((SC_WINDOW, SLAB, LANE), I32)])
    def run(y_hbm, i_hbm, o_hbm, i_v, buf):
        base = _sc_worker() * per
        pltpu.sync_copy(i_hbm.at[:, pl.ds(base, per)], i_v)

        @pl.loop(0, per // SC_WINDOW)
        def _(s):
            off = s * SC_WINDOW
            pltpu.sync_copy(y_hbm.at[i_v.at[0, pl.ds(off, SC_WINDOW)]], buf)
            pltpu.sync_copy(buf, o_hbm.at[pl.ds(base + off, SC_WINDOW)])

    return run(y_slabs.reshape(P_SLOTS, SLAB, LANE), dcat)


EROWS = EBLK * SLAB


def _expert_kernel(bs_ref, nb_ref, nu_ref, wg_ref, wu_ref, wd_ref, x_hbm, y_hbm,
                   xbuf, ybuf, xs, wg_bf, wu_bf, wd_bf, isem, osem):
    e = pl.program_id(0)
    n_exp = pl.num_programs(0)
    n_used = nu_ref[0]
    b0 = bs_ref[e]
    nb = nb_ref[e]

    def in_copy(g, slot):
        return pltpu.make_async_copy(x_hbm.at[pl.ds(pl.multiple_of(g * EROWS, EROWS), EROWS)], xbuf.at[slot],
                                     isem.at[slot])

    def out_copy(g, slot):
        return pltpu.make_async_copy(ybuf.at[slot], y_hbm.at[pl.ds(pl.multiple_of(g * EROWS, EROWS), EROWS)],
                                     osem.at[slot])

    @pl.when(e == 0)
    def _():
        in_copy(0, 0).start()

    @pl.when(nb > 0)
    def _():
        wg_bf[...] = wg_ref[0, 0].astype(BF16)
        wu_bf[...] = wu_ref[0, 0].astype(BF16)
        wd_bf[...] = wd_ref[0, 0].astype(BF16)

    def block(k, carry):
        g = b0 + k
        slot = lax.rem(g, 2)
        in_copy(g, slot).wait()

        @pl.when(g + 1 < n_used)
        def _():
            in_copy(g + 1, 1 - slot).start()

        _load_slabs(xbuf.at[slot], xs, EBLK, BF16)
        xb = xs[...]
        gt = jnp.dot(xb, wg_bf[...], preferred_element_type=F32)
        up = jnp.dot(xb, wu_bf[...], preferred_element_type=F32)
        hmid = (gt * _sigmoid(gt) * up).astype(BF16)
        res = jnp.dot(hmid, wd_bf[...], preferred_element_type=F32)

        @pl.when(g >= 2)
        def _():
            out_copy(g - 2, slot).wait()

        _store_slabs(ybuf.at[slot], res)
        out_copy(g, slot).start()
        return carry

    lax.fori_loop(0, nb, block, 0)

    @pl.when(e == n_exp - 1)
    def _():
        last = n_used - 1
        out_copy(last, lax.rem(last, 2)).wait()

        @pl.when(n_used >= 2)
        def _():
            out_copy(last - 1, lax.rem(last - 1, 2)).wait()


def _experts(x_sorted, blk_start, n_blk, n_used, w_gate, w_up, w_down, layer):
    any_spec = pl.BlockSpec(memory_space=pl.ANY)
    wspec = lambda r, c: pl.BlockSpec((1, 1, r, c), lambda e, *_: (layer, e, 0, 0))
    return pl.pallas_call(
        _expert_kernel,
        out_shape=jax.ShapeDtypeStruct((P_SLOTS * SLAB, LANE), I32),
        grid_spec=pltpu.PrefetchScalarGridSpec(
            num_scalar_prefetch=3, grid=(N_EXPERTS,),
            in_specs=[wspec(D, D_EXPERT), wspec(D, D_EXPERT), wspec(D_EXPERT, D), any_spec],
            out_specs=any_spec,
            scratch_shapes=[
                pltpu.VMEM((2, EROWS, LANE), I32), pltpu.VMEM((2, EROWS, LANE), I32),
                pltpu.VMEM((EBLK, D), BF16),
                pltpu.VMEM((D, D_EXPERT), BF16), pltpu.VMEM((D, D_EXPERT), BF16), pltpu.VMEM((D_EXPERT, D), BF16),
                pltpu.SemaphoreType.DMA((2,)), pltpu.SemaphoreType.DMA((2,)),
            ]),
        compiler_params=_cparams(),
        name="moe_experts",
    )(blk_start, n_blk, n_used, w_gate, w_up, w_down, x_sorted.reshape(P_SLOTS * SLAB, LANE))


def _combine_kernel(final, mr_ref, e0_ref, e1_ref, wt_ref, y_ref, mod_ref, fg_ref, o_ref, a_scr, b_scr):
    _load_slabs(e0_ref, a_scr, TL, F32)
    _load_slabs(e1_ref, b_scr, TL, F32)
    wt = wt_ref[...]
    moe = wt[:, 0:1] * a_scr[...] + wt[:, 1:2] * b_scr[...]
    y_new = y_ref[...] + mod_ref[0][5:6] * moe
    o_ref[...] = _rms(y_new, fg_ref[...]) if final else y_new


def _combine(ym, wts, y, mods, mrow, final_g, blk0, nblk, final):
    tok = lambda width: pl.BlockSpec((TL, width), lambda j, *_: (blk0 + j, 0))
    slab0 = pl.BlockSpec((TL * SLAB, LANE), lambda j, *_: (blk0 + j, 0))
    slab1 = pl.BlockSpec((TL * SLAB, LANE), lambda j, *_: (NL + blk0 + j, 0))
    mod = pl.BlockSpec((1, 6, D), lambda j, mr: (mr[blk0 + j], 0, 0))
    return pl.pallas_call(
        functools.partial(_combine_kernel, final),
        out_shape=jax.ShapeDtypeStruct((nblk * TL, D), F32),
        grid_spec=pltpu.PrefetchScalarGridSpec(
            num_scalar_prefetch=1, grid=(nblk,),
            in_specs=[slab0, slab1, tok(TOP_K), tok(D), mod, _full_spec((1, D))],
            out_specs=pl.BlockSpec((TL, D), lambda j, *_: (j, 0)),
            scratch_shapes=[pltpu.VMEM((TL, D), F32), pltpu.VMEM((TL, D), F32)]),
        compiler_params=_cparams(),
        name="moe_combine",
    )(mrow, ym, ym, wts, y, mods, final_g)


def kernel(x_prompt, x_sample, cache_attn_k, cache_attn_v, state_mlstm_C, state_mlstm_n, state_mlstm_m, c, c_ctx, ada_w, ada_b, norm1_g, norm2_g, conv_w_in, conv_w_dw, conv_b_dw, conv_ln_g, conv_ln_b, conv_w_out, attn_w_qkv, attn_q_norm, attn_k_norm, attn_w_o, mlstm_w_in, mlstm_b_gate, mlstm_norm_g, mlstm_w_out, moe_w_group, moe_b_group, moe_w_router, moe_b_router, moe_w_gate, moe_w_up, moe_w_down, final_norm_g):
    y = None
    cvec = jnp.concatenate([c_ctx[None, :], c, jnp.zeros((MOD_ROWS - 1 - DEC_BATCH, D), F32)], axis=0)
    rope = _rope_blocks()
    mrow, mrow_sb, mrow_l = jnp.asarray(_MOD_ROW), jnp.asarray(_MOD_ROW_SB), jnp.asarray(_MOD_ROW_L)
    new_k = new_v = new_c = new_n = new_m = None
    for i in range(DEPTH):
        kind, slot = i % 3, i // 3
        mods = _ada_layer(cvec, ada_w, ada_b, i)
        g1 = norm1_g[i].reshape(1, D)
        if kind == 0:
            src = (x_prompt.reshape(NP_TOK, D), x_sample.reshape(NS_TOK, D), True) if i == 0 else (y, y, False)
            u = _conv_in(*src, mods, mrow_l, g1, conv_w_in[slot].astype(BF16))
            w_dw = jnp.concatenate([conv_w_dw[slot], jnp.zeros((1, D), F32)], axis=0)
            y = _conv_main(u, *src, mods, mrow_sb, w_dw, conv_b_dw[slot].reshape(1, D), conv_ln_g[slot].reshape(1, D),
                           conv_ln_b[slot].reshape(1, D), conv_w_out[slot].astype(BF16))
        elif kind == 1:
            q, kb, vb, kf, vf = _attn_qkv(y, mods, mrow_sb, g1, attn_w_qkv[slot].astype(BF16),
                                          attn_q_norm[slot].reshape(1, HEAD_DIM), attn_k_norm[slot].reshape(1, HEAD_DIM),
                                          rope)
            new_k = kf[:NP_TOK].reshape(BATCH, 1, SEQ, N_KV_HEADS, HEAD_DIM)
            new_v = vf[:NP_TOK].reshape(BATCH, 1, SEQ, N_KV_HEADS, HEAD_DIM)
            ck = cache_attn_k[:, slot].reshape(DEC_BATCH, PAST_LEN, KV_DIM)
            cv = cache_attn_v[:, slot].reshape(DEC_BATCH, PAST_LEN, KV_DIM)
            y = _attention(q, kb, vb, ck, cv, attn_w_o[slot].astype(BF16), y, mods)
        else:
            b_gate = jnp.concatenate([mlstm_b_gate[slot], jnp.zeros((LANE - 4 * M_HEADS,), F32)]).reshape(1, LANE)
            q, k, v, o, gates, grow = _mlstm_in(y, mods, mrow, g1, mlstm_w_in, slot, b_gate)
            sc = state_mlstm_C[:, slot]
            sn = state_mlstm_n[:, slot].reshape(DEC_BATCH, 2, M_HEADS, 1, M_HEAD_DIM)
            sm = state_mlstm_m[:, slot].reshape(DEC_BATCH, 2, M_HEADS, 1, 1)
            hsum, nc_, nn_, nm_ = _mlstm_scan(q, k, v, gates, grow, sc, sn, sm)
            new_c = nc_[:, None]
            new_n = nn_.reshape(BATCH, 1, 2, M_HEADS, M_HEAD_DIM)
            new_m = nm_[..., 0, 0].reshape(BATCH, 1, 2, M_HEADS)
            y = _mlstm_out(hsum, o, mlstm_norm_g[slot].reshape(1, D), mlstm_w_out[slot].astype(BF16), y, mods, mrow_l)
        w_route = jnp.concatenate([moe_w_group[i], moe_w_router[i],
                                   jnp.zeros((D, LANE - N_GROUPS - N_EXPERTS), F32)], axis=1)
        b_route = jnp.concatenate([moe_b_group[i], moe_b_router[i],
                                   jnp.zeros((LANE - N_GROUPS - N_EXPERTS,), F32)]).reshape(1, LANE)
        x2, ewt, meta, cnt = _route(y, mods, mrow_l, norm2_g[i].reshape(1, D), w_route.astype(BF16), b_route)
        dest, blk_start, n_blk, n_used = _dispatch_tables(meta, cnt)
        x_sorted = _sc_dispatch(x2, dest[0:1], dest[1:2])
        y_sorted = _experts(x_sorted, blk_start, n_blk, n_used, moe_w_gate, moe_w_up, moe_w_down, i)
        ym = _sc_collect(y_sorted, dest.reshape(1, N_ASSIGN))
        ym = ym.reshape(N_ASSIGN * SLAB, LANE)
        fg = final_norm_g.reshape(1, D)
        if i + 1 < DEPTH:
            y = _combine(ym, ewt, y, mods, mrow_l, fg, 0, NL, False)
        else:
            y_prompt = _combine(ym, ewt, y, mods, mrow_l, fg, 0, NLP, True).reshape(BATCH, SEQ, D)
            y_sample = _combine(ym, ewt, y, mods, mrow_l, fg, NLP, NL - NLP, True).reshape(DEC_BATCH, DEC_SEQ, D)
    return (y_prompt, y_sample, new_k, new_v, new_c, new_n, new_m)
```

```python
import functools

import jax
import jax.numpy as jnp
import numpy as np
from jax import lax
from jax.experimental import pallas as pl
from jax.experimental.pallas import tpu as pltpu
from jax.experimental.pallas import tpu_sc as plsc

F32 = jnp.float32
BF16 = jnp.bfloat16
I32 = jnp.int32

D = 1024
BATCH, SEQ = 16, 256
DEC_BATCH, DEC_SEQ = 8, 1024
PAST_LEN = 256
DEPTH = 4
GRID_W = 64
EPS = 1e-6
CONV_WIDTH = 31
CONV_PAD = CONV_WIDTH // 2
HEAD_DIM = 128
N_HEADS = 8
N_KV_HEADS = 2
GQA_GROUP = N_HEADS // N_KV_HEADS
Q_DIM = N_HEADS * HEAD_DIM
KV_DIM = N_KV_HEADS * HEAD_DIM
QKV_DIM = Q_DIM + 2 * KV_DIM
ROPE_THETA = 10000.0
M_HEADS = 4
M_HEAD_DIM = D // M_HEADS
M_CHUNK = 64
N_GROUPS = 4
EXPERTS_PER_GROUP = 8
N_EXPERTS = N_GROUPS * EXPERTS_PER_GROUP
TOP_K = 2
D_EXPERT = 512

NP_TOK = BATCH * SEQ
NS_TOK = DEC_BATCH * DEC_SEQ
N_TOK = NP_TOK + NS_TOK
TM = 512
NB = N_TOK // TM
NBP = NP_TOK // TM
BLK_PER_DEC = DEC_SEQ // TM
TL = 1024
NL = N_TOK // TL
NLP = NP_TOK // TL
SB = 256
NSB = N_TOK // SB
NSBP = NP_TOK // SB
SB_PER_DEC = DEC_SEQ // SB
MOD_ROWS = 16
HALO = 16
LANE = 128
SUBLANE = 8

N_ASSIGN = N_TOK * TOP_K
EBLK = 256
N_EBLK = N_ASSIGN // EBLK + N_EXPERTS
P_SLOTS = N_EBLK * EBLK
N_PAD_SLOTS = P_SLOTS - N_ASSIGN

VMEM_LIMIT = 56 * 1024 * 1024


def _block_tables(nb, nbp, per_dec):
    j = np.arange(nb)
    is_p = j < nbp
    mod_row = np.where(is_p, 0, 1 + (j - nbp) // per_dec)
    rope_idx = np.where(is_p, 0, 1 + (j - nbp) % per_dec)
    first = np.where(is_p, 1, ((j - nbp) % per_dec == 0).astype(np.int64))
    last = np.where(is_p, 1, ((j - nbp) % per_dec == per_dec - 1).astype(np.int64))
    return (mod_row.astype(np.int32), rope_idx.astype(np.int32), first.astype(np.int32), last.astype(np.int32))


_MOD_ROW, _, _, _ = _block_tables(NB, NBP, BLK_PER_DEC)
_MOD_ROW_L, _, _, _ = _block_tables(NL, NLP, DEC_SEQ // TL)
_MOD_ROW_SB, _ROPE_IDX_SB, _SEQ_FIRST, _SEQ_LAST = _block_tables(NSB, NSBP, SB_PER_DEC)


def _cparams(n_axes=1):
    return pltpu.CompilerParams(dimension_semantics=("arbitrary",) * n_axes, vmem_limit_bytes=VMEM_LIMIT)


def _sigmoid(x):
    return 1.0 / (1.0 + jnp.exp(-x))


def _rms(x, g):
    return x * lax.rsqrt(jnp.mean(x * x, axis=-1, keepdims=True) + EPS) * g


def _norm_mod(y, g, mod, which):
    shift = mod[3 * which:3 * which + 1]
    scale = mod[3 * which + 1:3 * which + 2]
    return _rms(y, g) * (1.0 + scale) + shift


def _ada_kernel(c_ref, w_ref, b_ref, o_ref):
    c = c_ref[...]
    s = c * _sigmoid(c)
    o_ref[0] = jnp.dot(s.astype(BF16), w_ref[0].astype(BF16), preferred_element_type=F32) + b_ref[0]


def _ada_layer(cvec, ada_w, ada_b, layer):
    tn = 1536
    out = pl.pallas_call(
        _ada_kernel,
        out_shape=jax.ShapeDtypeStruct((1, MOD_ROWS, 6 * D), F32),
        grid=(6 * D // tn,),
        in_specs=[
            pl.BlockSpec((MOD_ROWS, D), lambda n: (0, 0)),
            pl.BlockSpec((1, D, tn), lambda n: (layer, 0, n)),
            pl.BlockSpec((1, 1, tn), lambda n: (layer, 0, n)),
        ],
        out_specs=pl.BlockSpec((1, MOD_ROWS, tn), lambda n: (0, 0, n)),
        compiler_params=_cparams(1),
        name="ada_mod",
    )(cvec, ada_w, ada_b.reshape(DEPTH, 1, 6 * D))
    return out.reshape(MOD_ROWS, 6, D)


def _tok_spec(width, rows=TM):
    return pl.BlockSpec((rows, width), lambda j, *_: (j, 0))


def _mod_spec():
    return pl.BlockSpec((1, 6, D), lambda j, mr, *_: (mr[j], 0, 0))


def _full_spec(shape):
    nd = len(shape)
    return pl.BlockSpec(shape, lambda j, *_: (0,) * nd)


def _pair_specs(rows, nbp, split):
    s_off = nbp if split else 0
    return [pl.BlockSpec((rows, D), lambda j, *_: (jnp.minimum(j, nbp - 1), 0)),
            pl.BlockSpec((rows, D), lambda j, *_: (jnp.maximum(j, nbp) - s_off, 0))]


def _pair_block(nbp, yp_ref, ys_ref):
    return jnp.where(pl.program_id(0) < nbp, yp_ref[...], ys_ref[...])


def _conv_in_kernel(mr_ref, yp_ref, ys_ref, mod_ref, g_ref, w_ref, u_ref):
    h = _norm_mod(_pair_block(NLP, yp_ref, ys_ref), g_ref[...], mod_ref[0], 0)
    ag = jnp.dot(h.astype(BF16), w_ref[...], preferred_element_type=F32)
    u_ref[...] = ag[:, :D] * _sigmoid(ag[:, D:])


def _conv_in(yp, ys, split, mods, mrow, g1, w_in):
    return pl.pallas_call(
        _conv_in_kernel,
        out_shape=jax.ShapeDtypeStruct((N_TOK, D), F32),
        grid_spec=pltpu.PrefetchScalarGridSpec(
            num_scalar_prefetch=1, grid=(NL,),
            in_specs=_pair_specs(TL, NLP, split) + [_mod_spec(), _full_spec((1, D)), _full_spec((D, 2 * D))],
            out_specs=_tok_spec(D, TL)),
        compiler_params=_cparams(),
        name="conv_in",
    )(mrow, yp, ys, mods, g1, w_in)


def _conv_main_kernel(mr_ref, first_ref, last_ref, u_ref, up_ref, un_ref, wdw_ref, bdw_ref, lg_ref, lb_ref,
                      wout_ref, yp_ref, ys_ref, mod_ref, o_ref, ext_ref, acc_ref):
    j = pl.program_id(0)
    zero = jnp.zeros((HALO, D), F32)
    ext_ref[0:HALO, :] = jnp.where(first_ref[j] == 1, zero, up_ref[...])
    ext_ref[HALO:HALO + SB, :] = u_ref[...]
    ext_ref[HALO + SB:2 * HALO + SB, :] = jnp.where(last_ref[j] == 1, zero, un_ref[...])

    off0 = HALO - CONV_PAD
    n_a = (off0 + CONV_WIDTH - 1) // SUBLANE + 1
    n_chunks = SB // SUBLANE

    def strip(ci, carry):
        cs = pl.ds(pl.multiple_of(ci * LANE, LANE), LANE)
        wk = [jnp.broadcast_to(wdw_ref[k:k + 1, cs], (SUBLANE, LANE)) for k in range(CONV_WIDTH)]
        bias = jnp.broadcast_to(bdw_ref[:, cs], (SUBLANE, LANE))
        sub = lax.broadcasted_iota(I32, (SUBLANE, LANE), 0)
        prev_rot, prev_v0 = None, None
        for j in range(n_chunks + 1):
            tiles = [ext_ref[SUBLANE * (j + a):SUBLANE * (j + a + 1), cs] for a in range(n_a)]
            part = []
            for s in range(SUBLANE):
                acc = None
                for a in range(n_a):
                    k = SUBLANE * a + s - off0
                    if (0 <= k < CONV_WIDTH) and not (s == 0 and j == n_chunks):
                        term = tiles[a] * wk[k]
                        acc = term if acc is None else acc + term
                part.append(acc)
            rot = [None] + [pltpu.roll(part[s], SUBLANE - s, 0) for s in range(1, SUBLANE)]
            if j >= 1:
                out = prev_v0 + bias
                for s in range(1, SUBLANE):
                    out = out + jnp.where(sub < SUBLANE - s, prev_rot[s], rot[s])
                acc_ref[SUBLANE * (j - 1):SUBLANE * j, cs] = out
            prev_rot, prev_v0 = rot, part[0]
        return carry

    lax.fori_loop(0, D // LANE, strip, 0)

    c = acc_ref[...]
    mu = jnp.mean(c, axis=-1, keepdims=True)
    cc = c - mu
    var = jnp.mean(cc * cc, axis=-1, keepdims=True)
    z = cc * lax.rsqrt(var + EPS) * lg_ref[...] + lb_ref[...]
    z = z * _sigmoid(z)
    out = jnp.dot(z.astype(BF16), wout_ref[...], preferred_element_type=F32)
    o_ref[...] = _pair_block(NSBP, yp_ref, ys_ref) + mod_ref[0][2:3] * out


def _conv_main(u, yp, ys, split, mods, mrow, w_dw, b_dw, ln_g, ln_b, w_out):
    nh = N_TOK // HALO
    per = SB // HALO
    sb_spec = pl.BlockSpec((SB, D), lambda j, *_: (j, 0))
    return pl.pallas_call(
        _conv_main_kernel,
        out_shape=jax.ShapeDtypeStruct((N_TOK, D), F32),
        grid_spec=pltpu.PrefetchScalarGridSpec(
            num_scalar_prefetch=3, grid=(NSB,),
            in_specs=[
                sb_spec,
                pl.BlockSpec((HALO, D), lambda j, *_: (jnp.maximum(j * per - 1, 0), 0)),
                pl.BlockSpec((HALO, D), lambda j, *_: (jnp.minimum((j + 1) * per, nh - 1), 0)),
                _full_spec((CONV_WIDTH + 1, D)), _full_spec((1, D)), _full_spec((1, D)), _full_spec((1, D)),
                _full_spec((D, D)), *_pair_specs(SB, NSBP, split), _mod_spec(),
            ],
            out_specs=sb_spec,
            scratch_shapes=[pltpu.VMEM((SB + 2 * HALO, D), F32), pltpu.VMEM((SB, D), F32)]),
        compiler_params=_cparams(),
        name="conv_main",
    )(mrow, jnp.asarray(_SEQ_FIRST), jnp.asarray(_SEQ_LAST), u, u, u, w_dw, b_dw, ln_g, ln_b, w_out, yp, ys, mods)


def _rope_angles():
    rows = DEC_SEQ // GRID_W
    row = jnp.repeat(jnp.arange(rows, dtype=F32), GRID_W)
    col = jnp.tile(jnp.arange(GRID_W, dtype=F32), rows)
    axis_dim = HEAD_DIM // 2
    freqs = jnp.power(ROPE_THETA, -jnp.arange(axis_dim // 2, dtype=F32) * 2.0 / axis_dim)
    ang_r = row[:, None] * freqs[None, :]
    ang_c = col[:, None] * freqs[None, :]
    return jnp.concatenate([ang_r, ang_r, ang_c, ang_c], axis=-1)


def _rope_blocks():
    ang = _rope_angles()
    cos, sin = jnp.cos(ang), jnp.sin(ang)
    lane = np.arange(HEAD_DIM)
    lo = jnp.asarray(((lane % (HEAD_DIM // 2)) < HEAD_DIM // 4).astype(np.float32))
    sin_a = -sin * lo[None, :]
    sin_b = sin * (1.0 - lo)[None, :]
    nblk = DEC_SEQ // SB
    ident = jnp.ones((1, SB, HEAD_DIM), F32)
    zeros = jnp.zeros((1, SB, HEAD_DIM), F32)
    cos_t = jnp.concatenate([ident, cos.reshape(nblk, SB, HEAD_DIM)], axis=0)
    sa_t = jnp.concatenate([zeros, sin_a.reshape(nblk, SB, HEAD_DIM)], axis=0)
    sb_t = jnp.concatenate([zeros, sin_b.reshape(nblk, SB, HEAD_DIM)], axis=0)
    return cos_t, sa_t, sb_t


def _attn_qkv_kernel(mr_ref, ri_ref, y_ref, mod_ref, g_ref, w_ref, qg_ref, kg_ref, cos_ref, sa_ref, sb_ref,
                     q_ref, kb_ref, vb_ref, kf_ref, vf_ref):
    h = _norm_mod(y_ref[...], g_ref[...], mod_ref[0], 0)
    qkv = jnp.dot(h.astype(BF16), w_ref[...], preferred_element_type=F32)
    cos, sa, sb = cos_ref[0], sa_ref[0], sb_ref[0]
    quarter = HEAD_DIM // 4

    def head(x, g):
        xn = _rms(x, g)
        return xn * cos + pltpu.roll(xn, HEAD_DIM - quarter, 1) * sa + pltpu.roll(xn, quarter, 1) * sb

    scale = HEAD_DIM ** -0.5
    for hd in range(N_HEADS):
        sl = slice(hd * HEAD_DIM, (hd + 1) * HEAD_DIM)
        q_ref[:, sl] = (head(qkv[:, sl], qg_ref[...]) * scale).astype(BF16)
    for kv in range(N_KV_HEADS):
        sl = slice(kv * HEAD_DIM, (kv + 1) * HEAD_DIM)
        kr = head(qkv[:, Q_DIM + kv * HEAD_DIM:Q_DIM + (kv + 1) * HEAD_DIM], kg_ref[...])
        kf_ref[:, sl] = kr
        kb_ref[:, sl] = kr.astype(BF16)
    v = qkv[:, Q_DIM + KV_DIM:]
    vf_ref[...] = v
    vb_ref[...] = v.astype(BF16)


def _attn_qkv(y, mods, mrow, g1, w_qkv, q_g, k_g, rope):
    cos_t, sa_t, sb_t = rope
    rspec = pl.BlockSpec((1, SB, HEAD_DIM), lambda j, mr, ri: (ri[j], 0, 0))
    return pl.pallas_call(
        _attn_qkv_kernel,
        out_shape=(jax.ShapeDtypeStruct((N_TOK, Q_DIM), BF16), jax.ShapeDtypeStruct((N_TOK, KV_DIM), BF16),
                   jax.ShapeDtypeStruct((N_TOK, KV_DIM), BF16), jax.ShapeDtypeStruct((N_TOK, KV_DIM), F32),
                   jax.ShapeDtypeStruct((N_TOK, KV_DIM), F32)),
        grid_spec=pltpu.PrefetchScalarGridSpec(
            num_scalar_prefetch=2, grid=(NSB,),
            in_specs=[_tok_spec(D, SB), _mod_spec(), _full_spec((1, D)), _full_spec((D, QKV_DIM)),
                      _full_spec((1, HEAD_DIM)), _full_spec((1, HEAD_DIM)), rspec, rspec, rspec],
            out_specs=(_tok_spec(Q_DIM, SB), _tok_spec(KV_DIM, SB), _tok_spec(KV_DIM, SB), _tok_spec(KV_DIM, SB),
                       _tok_spec(KV_DIM, SB))),
        compiler_params=_cparams(),
        name="attn_qkv",
    )(mrow, jnp.asarray(_ROPE_IDX_SB), y, mods, g1, w_qkv, q_g, k_g, cos_t, sa_t, sb_t)


def _attn_heads(q, ks, vs, o_scr):
    nt = (((1,), (1,)), ((), ()))
    for hd in range(N_HEADS):
        g = hd // GQA_GROUP
        qh = q[:, hd * HEAD_DIM:(hd + 1) * HEAD_DIM]
        gs = slice(g * HEAD_DIM, (g + 1) * HEAD_DIM)
        ss = [lax.dot_general(qh, k[:, gs], nt, preferred_element_type=F32) for k in ks]
        m = functools.reduce(jnp.maximum, [jnp.max(s, axis=-1, keepdims=True) for s in ss])
        ps = [jnp.exp(s - m) for s in ss]
        l = functools.reduce(lambda a, b: a + b, [jnp.sum(p, axis=-1, keepdims=True) for p in ps])
        o = functools.reduce(lambda a, b: a + b,
                             [jnp.dot(p.astype(BF16), v[:, gs], preferred_element_type=F32) for p, v in zip(ps, vs)])
        o_scr[:, hd * HEAD_DIM:(hd + 1) * HEAD_DIM] = (o / l).astype(BF16)


def _attn_ctx_kernel(q_ref, k_ref, v_ref, wo_ref, y_ref, mod_ref, o_ref, o_scr):
    _attn_heads(q_ref[...], [k_ref[...]], [v_ref[...]], o_scr)
    out = jnp.dot(o_scr[...], wo_ref[...], preferred_element_type=F32)
    o_ref[...] = y_ref[...] + mod_ref[0][2:3] * out


def _attn_lat_kernel(q_ref, k_ref, v_ref, ck_ref, cv_ref, wo_ref, y_ref, mod_ref, ctx_out_ref, o_ref, o_scr):
    del ctx_out_ref
    _attn_heads(q_ref[...], [k_ref[...], ck_ref[0].astype(BF16)], [v_ref[...], cv_ref[0].astype(BF16)], o_scr)
    out = jnp.dot(o_scr[...], wo_ref[...], preferred_element_type=F32)
    o_ref[...] = y_ref[...] + mod_ref[0][2:3] * out


def _attention(q, kb, vb, cache_k, cache_v, w_o, y, mods):
    y_ctx = pl.pallas_call(
        _attn_ctx_kernel,
        out_shape=jax.ShapeDtypeStruct((N_TOK, D), F32),
        grid=(BATCH,),
        in_specs=[
            pl.BlockSpec((SEQ, Q_DIM), lambda s: (s, 0)),
            pl.BlockSpec((SEQ, KV_DIM), lambda s: (s, 0)),
            pl.BlockSpec((SEQ, KV_DIM), lambda s: (s, 0)),
            pl.BlockSpec((Q_DIM, D), lambda s: (0, 0)),
            pl.BlockSpec((SEQ, D), lambda s: (s, 0)),
            pl.BlockSpec((1, 6, D), lambda s: (0, 0, 0)),
        ],
        out_specs=pl.BlockSpec((SEQ, D), lambda s: (s, 0)),
        scratch_shapes=[pltpu.VMEM((SEQ, Q_DIM), BF16)],
        compiler_params=_cparams(),
        name="attn_ctx",
    )(q, kb, vb, w_o, y, mods)
    pb = NP_TOK // DEC_SEQ
    return pl.pallas_call(
        _attn_lat_kernel,
        out_shape=jax.ShapeDtypeStruct((N_TOK, D), F32),
        input_output_aliases={8: 0},
        grid=(DEC_BATCH, SB_PER_DEC),
        in_specs=[
            pl.BlockSpec((SB, Q_DIM), lambda b, t: (NSBP + b * SB_PER_DEC + t, 0)),
            pl.BlockSpec((DEC_SEQ, KV_DIM), lambda b, t: (pb + b, 0)),
            pl.BlockSpec((DEC_SEQ, KV_DIM), lambda b, t: (pb + b, 0)),
            pl.BlockSpec((1, PAST_LEN, KV_DIM), lambda b, t: (b, 0, 0)),
            pl.BlockSpec((1, PAST_LEN, KV_DIM), lambda b, t: (b, 0, 0)),
            pl.BlockSpec((Q_DIM, D), lambda b, t: (0, 0)),
            pl.BlockSpec((SB, D), lambda b, t: (NSBP + b * SB_PER_DEC + t, 0)),
            pl.BlockSpec((1, 6, D), lambda b, t: (1 + b, 0, 0)),
            pl.BlockSpec(memory_space=pl.ANY),
        ],
        out_specs=pl.BlockSpec((SB, D), lambda b, t: (NSBP + b * SB_PER_DEC + t, 0)),
        scratch_shapes=[pltpu.VMEM((SB, Q_DIM), BF16)],
        compiler_params=_cparams(2),
        name="attn_lat",
    )(q, kb, vb, cache_k, cache_v, w_o, y, mods, y_ctx)


def _log_sigmoid(x):
    return jnp.minimum(x, 0.0) - jnp.log(1.0 + jnp.exp(-jnp.abs(x)))


W_T = 256


def _mlstm_in_kernel(mr_ref, y_ref, mod_ref, g_ref, wt_ref, bg_ref, q_ref, k_ref, v_ref, o_ref, gt_ref, gr_ref,
                     w_ref, wg_ref):
    @pl.when(pl.program_id(0) == 0)
    def _():
        for r in range(0, 4 * D, W_T):
            w_ref[:, r:r + W_T] = jnp.transpose(wt_ref[0, r:r + W_T, :]).astype(BF16)
        n_out = 4 * D + 4 * M_HEADS
        tail = jnp.transpose(wt_ref[0, n_out - LANE:n_out, :])
        lane = lax.broadcasted_iota(I32, tail.shape, 1)
        wg_ref[...] = jnp.where(lane < 4 * M_HEADS, pltpu.roll(tail, 4 * M_HEADS, axis=1), 0.0).astype(BF16)

    h = _norm_mod(y_ref[...], g_ref[...], mod_ref[0], 0)
    hb = h.astype(BF16)
    q_ref[...] = jnp.dot(hb, w_ref[:, 0:D], preferred_element_type=F32).astype(BF16)
    k_ref[...] = (jnp.dot(hb, w_ref[:, D:2 * D], preferred_element_type=F32) * (M_HEAD_DIM ** -0.5)).astype(BF16)
    v_ref[...] = jnp.dot(hb, w_ref[:, 2 * D:3 * D], preferred_element_type=F32).astype(BF16)
    o_ref[...] = _sigmoid(jnp.dot(hb, w_ref[:, 3 * D:4 * D], preferred_element_type=F32))
    gt = jnp.dot(hb, wg_ref[...], preferred_element_type=F32) + bg_ref[...]
    lane = lax.broadcasted_iota(I32, gt.shape, 1)
    is_f = ((lane >= M_HEADS) & (lane < 2 * M_HEADS)) | ((lane >= 3 * M_HEADS) & (lane < 4 * M_HEADS))
    gt = jnp.where(is_f, _log_sigmoid(gt), gt)
    gt_ref[...] = gt
    per = LANE // M_CHUNK
    for p in range(TM // LANE):
        t = jnp.transpose(gt[p * LANE:(p + 1) * LANE, :])
        for f in range(per):
            gr_ref[p * per + f] = t[0:4 * M_HEADS, f * M_CHUNK:(f + 1) * M_CHUNK]


def _mlstm_in(y, mods, mrow, g1, w_in_all, slot, b_gate):
    w_t = jnp.swapaxes(w_in_all, 1, 2)
    w_spec = pl.BlockSpec((1,) + w_t.shape[1:], lambda j, *_: (slot, 0, 0), pipeline_mode=pl.Buffered(1))
    return pl.pallas_call(
        _mlstm_in_kernel,
        out_shape=(jax.ShapeDtypeStruct((N_TOK, D), BF16), jax.ShapeDtypeStruct((N_TOK, D), BF16),
                   jax.ShapeDtypeStruct((N_TOK, D), BF16), jax.ShapeDtypeStruct((N_TOK, D), F32),
                   jax.ShapeDtypeStruct((N_TOK, LANE), F32),
                   jax.ShapeDtypeStruct((N_TOK // M_CHUNK, 4 * M_HEADS, M_CHUNK), F32)),
        grid_spec=pltpu.PrefetchScalarGridSpec(
            num_scalar_prefetch=1, grid=(NB,),
            in_specs=[_tok_spec(D), _mod_spec(), _full_spec((1, D)), w_spec, _full_spec((1, LANE))],
            out_specs=(_tok_spec(D), _tok_spec(D), _tok_spec(D), _tok_spec(D), _tok_spec(LANE),
                       pl.BlockSpec((TM // M_CHUNK, 4 * M_HEADS, M_CHUNK), lambda j, *_: (j, 0, 0))),
            scratch_shapes=[pltpu.VMEM((D, 4 * D), BF16), pltpu.VMEM((D, LANE), BF16)]),
        compiler_params=_cparams(),
        name="mlstm_in",
    )(mrow, y, mods, g1, w_t, b_gate)


def _mlstm_load(hd, c, q_ref, k_ref, v_ref, gc_ref, gr_ref):
    r0 = pl.multiple_of(c * M_CHUNK, M_CHUNK)
    hs = slice(hd * M_HEAD_DIM, (hd + 1) * M_HEAD_DIM)
    rows = pl.ds(r0, M_CHUNK)
    return rows, hs, q_ref[rows, hs], k_ref[rows, hs], v_ref[rows, hs], gc_ref[rows, :], gr_ref[c]


def _mlstm_chunks(chains, ms, loaded, c_scr, n_scr):
    L = M_CHUNK
    n = range(len(chains))
    t_idx = lax.broadcasted_iota(I32, (L, L), 0)
    s_idx = lax.broadcasted_iota(I32, (L, L), 1)
    masks = {0: (s_idx <= t_idx, t_idx <= s_idx), 1: (s_idx >= t_idx, t_idx >= s_idx)}
    q = [ld[2] for ld in loaded]
    k = [ld[3] for ld in loaded]
    v = [ld[4] for ld in loaded]
    gi = [2 * d * M_HEADS + hd for hd, d in chains]
    gf = [(2 * d + 1) * M_HEADS + hd for hd, d in chains]
    i_col = [ld[5][:, gi[i]:gi[i] + 1] for i, ld in enumerate(loaded)]
    lf_col = [ld[5][:, gf[i]:gf[i] + 1] for i, ld in enumerate(loaded)]
    i_row = [ld[6][gi[i]:gi[i] + 1, :] for i, ld in enumerate(loaded)]
    lf_row = [ld[6][gf[i]:gf[i] + 1, :] for i, ld in enumerate(loaded)]
    mask = [masks[d][0] for _, d in chains]
    mask_t = [masks[d][1] for _, d in chains]
    b_col = [jnp.sum(jnp.where(mask[i], lf_row[i], 0.0), axis=1, keepdims=True) for i in n]
    b_row = [jnp.sum(jnp.where(mask_t[i], lf_col[i], 0.0), axis=0, keepdims=True) for i in n]
    log_d = [jnp.where(mask[i], b_col[i] - b_row[i] + i_row[i], -jnp.inf) for i in n]
    li = [b_col[i] + ms[i] for i in n]
    m_r = [jnp.maximum(li[i], jnp.max(log_d[i], axis=1, keepdims=True)) for i in n]
    a_int = [jnp.exp(li[i] - m_r[i]) for i in n]
    dmat = [jnp.exp(log_d[i] - m_r[i]) for i in n]
    cmat = [c_scr[d, hd] for hd, d in chains]
    nvec = [n_scr[d, hd] for hd, d in chains]
    gram = [lax.dot_general(q[i], k[i], (((1,), (1,)), ((), ())), preferred_element_type=F32) for i in n]
    inter = [jnp.dot(q[i], cmat[i].astype(BF16), preferred_element_type=F32) for i in n]
    s = [gram[i] * dmat[i] for i in n]
    intra = [jnp.dot(s[i].astype(BF16), v[i], preferred_element_type=F32) for i in n]
    qn = [jnp.sum(q[i].astype(F32) * nvec[i], axis=1, keepdims=True) for i in n]
    den = [a_int[i] * qn[i] + jnp.sum(s[i], axis=1, keepdims=True) for i in n]
    hh = [(a_int[i] * inter[i] + intra[i]) / jnp.maximum(jnp.abs(den[i]), jnp.exp(-m_r[i])) for i in n]
    b_last = [b_row[i][:, L - 1:L] if chains[i][1] == 0 else b_row[i][:, 0:1] for i in n]
    log_w = [b_last[i] - b_col[i] + i_col[i] for i in n]
    m_new = [jnp.maximum(b_last[i] + ms[i], jnp.max(log_w[i], axis=0, keepdims=True)) for i in n]
    w = [jnp.exp(log_w[i] - m_new[i]) for i in n]
    decay = [jnp.exp(b_last[i] + ms[i] - m_new[i]) for i in n]
    kw = [k[i].astype(F32) * w[i] for i in n]
    kv = [lax.dot_general(kw[i].astype(BF16), v[i], (((0,), (0,)), ((), ())), preferred_element_type=F32) for i in n]
    for i, (hd, d) in enumerate(chains):
        c_scr[d, hd] = decay[i] * cmat[i] + kv[i]
        n_scr[d, hd] = decay[i] * nvec[i] + jnp.sum(kw[i], axis=0, keepdims=True)
    return hh, m_new


SCAN_GROUP = 2 * M_HEADS


def _mlstm_scan_body(n_chunks, q_ref, k_ref, v_ref, gc_ref, gr_ref, h_ref, hb_scr, c_scr, n_scr, m0):
    chains = [(hd, d) for hd in range(M_HEADS) for d in range(2)]

    def body(c, ms):
        out = []
        for g0 in range(0, len(chains), SCAN_GROUP):
            grp = chains[g0:g0 + SCAN_GROUP]
            loaded = [_mlstm_load(hd, c if d == 0 else n_chunks - 1 - c, q_ref, k_ref, v_ref, gc_ref, gr_ref)
                      for hd, d in grp]
            hh, m_new = _mlstm_chunks(grp, ms[g0:g0 + SCAN_GROUP], loaded, c_scr, n_scr)
            for (hd, d), ld, h in zip(grp, loaded, hh):
                dst = h_ref if d == 0 else hb_scr
                dst[ld[0], ld[1]] = h
            out += m_new
        return tuple(out)

    ms = lax.fori_loop(0, n_chunks, body, tuple(m0))
    h_ref[...] += hb_scr[...]
    return ms


def _mlstm_scan_ctx_kernel(q_ref, k_ref, v_ref, gc_ref, gr_ref, h_ref, cn_ref, nn_ref, mn_ref, hb_scr, c_scr, n_scr):
    c_scr[...] = jnp.zeros(c_scr.shape, F32)
    n_scr[...] = jnp.zeros(n_scr.shape, F32)
    zero = jnp.zeros((1, 1), F32)
    ms = _mlstm_scan_body(SEQ // M_CHUNK, q_ref, k_ref, v_ref, gc_ref, gr_ref, h_ref, hb_scr, c_scr, n_scr,
                          [zero] * (2 * M_HEADS))
    cn_ref[0] = c_scr[...]
    nn_ref[0] = n_scr[...]
    for hd in range(M_HEADS):
        for d in range(2):
            mn_ref[0, d, hd] = jnp.broadcast_to(ms[2 * hd + d], (1, LANE))


def _mlstm_scan_lat_kernel(q_ref, k_ref, v_ref, gc_ref, gr_ref, c0_ref, n0_ref, m0_ref, ctx_out_ref, h_ref,
                           hb_scr, c_scr, n_scr):
    del ctx_out_ref
    c_scr[...] = c0_ref[0]
    n_scr[...] = n0_ref[0]
    m0 = [m0_ref[0, d, hd] for hd in range(M_HEADS) for d in range(2)]
    _mlstm_scan_body(DEC_SEQ // M_CHUNK, q_ref, k_ref, v_ref, gc_ref, gr_ref, h_ref, hb_scr, c_scr, n_scr, m0)


def _mlstm_scan(q, k, v, gcol, grow, state_c, state_n, state_m):
    hd = M_HEAD_DIM
    ng = 4 * M_HEADS
    state_scratch = [pltpu.VMEM((2, M_HEADS, hd, hd), F32), pltpu.VMEM((2, M_HEADS, 1, hd), F32)]
    ncp = SEQ // M_CHUNK
    h_ctx, new_c, new_n, new_m = pl.pallas_call(
        _mlstm_scan_ctx_kernel,
        out_shape=(jax.ShapeDtypeStruct((N_TOK, D), F32),
                   jax.ShapeDtypeStruct((BATCH, 2, M_HEADS, hd, hd), F32),
                   jax.ShapeDtypeStruct((BATCH, 2, M_HEADS, 1, hd), F32),
                   jax.ShapeDtypeStruct((BATCH, 2, M_HEADS, 1, LANE), F32)),
        grid=(BATCH,),
        in_specs=[
            pl.BlockSpec((SEQ, D), lambda s: (s, 0)),
            pl.BlockSpec((SEQ, D), lambda s: (s, 0)),
            pl.BlockSpec((SEQ, D), lambda s: (s, 0)),
            pl.BlockSpec((SEQ, LANE), lambda s: (s, 0)),
            pl.BlockSpec((ncp, ng, M_CHUNK), lambda s: (s, 0, 0)),
        ],
        out_specs=(
            pl.BlockSpec((SEQ, D), lambda s: (s, 0)),
            pl.BlockSpec((1, 2, M_HEADS, hd, hd), lambda s: (s, 0, 0, 0, 0)),
            pl.BlockSpec((1, 2, M_HEADS, 1, hd), lambda s: (s, 0, 0, 0, 0)),
            pl.BlockSpec((1, 2, M_HEADS, 1, LANE), lambda s: (s, 0, 0, 0, 0)),
        ),
        scratch_shapes=[pltpu.VMEM((SEQ, D), F32)] + state_scratch,
        compiler_params=_cparams(),
        name="mlstm_scan_ctx",
    )(q, k, v, gcol, grow)
    ncl = DEC_SEQ // M_CHUNK
    pb = NP_TOK // DEC_SEQ
    h_all = pl.pallas_call(
        _mlstm_scan_lat_kernel,
        out_shape=jax.ShapeDtypeStruct((N_TOK, D), F32),
        input_output_aliases={8: 0},
        grid=(DEC_BATCH,),
        in_specs=[
            pl.BlockSpec((DEC_SEQ, D), lambda b: (pb + b, 0)),
            pl.BlockSpec((DEC_SEQ, D), lambda b: (pb + b, 0)),
            pl.BlockSpec((DEC_SEQ, D), lambda b: (pb + b, 0)),
            pl.BlockSpec((DEC_SEQ, LANE), lambda b: (pb + b, 0)),
            pl.BlockSpec((ncl, ng, M_CHUNK), lambda b: (pb + b, 0, 0)),
            pl.BlockSpec((1, 2, M_HEADS, hd, hd), lambda b: (b, 0, 0, 0, 0)),
            pl.BlockSpec((1, 2, M_HEADS, 1, hd), lambda b: (b, 0, 0, 0, 0)),
            pl.BlockSpec((1, 2, M_HEADS, 1, 1), lambda b: (b, 0, 0, 0, 0)),
            pl.BlockSpec(memory_space=pl.ANY),
        ],
        out_specs=pl.BlockSpec((DEC_SEQ, D), lambda b: (pb + b, 0)),
        scratch_shapes=[pltpu.VMEM((DEC_SEQ, D), F32)] + state_scratch,
        compiler_params=_cparams(),
        name="mlstm_scan_lat",
    )(q, k, v, gcol, grow, state_c, state_n, state_m, h_ctx)
    return h_all, new_c, new_n, new_m


def _mlstm_out_kernel(mr_ref, h_ref, o_ref, ng_ref, w_ref, y_ref, mod_ref, out_ref, x_scr):
    hc = o_ref[...] * h_ref[...]
    for hd in range(M_HEADS):
        sl = slice(hd * M_HEAD_DIM, (hd + 1) * M_HEAD_DIM)
        x_scr[:, sl] = _rms(hc[:, sl], ng_ref[:, sl]).astype(BF16)
    out = jnp.dot(x_scr[...], w_ref[...], preferred_element_type=F32)
    out_ref[...] = y_ref[...] + mod_ref[0][2:3] * out


def _mlstm_out(hsum, o, norm_g, w_out, y, mods, mrow):
    return pl.pallas_call(
        _mlstm_out_kernel,
        out_shape=jax.ShapeDtypeStruct((N_TOK, D), F32),
        grid_spec=pltpu.PrefetchScalarGridSpec(
            num_scalar_prefetch=1, grid=(NL,),
            in_specs=[_tok_spec(D, TL), _tok_spec(D, TL), _full_spec((1, D)), _full_spec((D, D)), _tok_spec(D, TL),
                      _mod_spec()],
            out_specs=_tok_spec(D, TL),
            scratch_shapes=[pltpu.VMEM((TL, D), BF16)]),
        compiler_params=_cparams(),
        name="mlstm_out",
    )(mrow, hsum, o, norm_g, w_out, y, mods)


ROUTE_OFF = N_GROUPS
SLAB = D // (2 * LANE)
V7X_SC_CORES = 2
V7X_SC_SUBCORES = 16
SC_WORKERS = V7X_SC_CORES * V7X_SC_SUBCORES
SC_WINDOW = 128
SC_HALF = SC_WINDOW // 2
HI_MASK = -65536


def _bf16_bits(x):
    return lax.bitcast_convert_type(x.astype(BF16).astype(F32), I32)


def _store_slabs(ref, x):
    rows = x.shape[0]
    for c in range(SLAB):
        lo = lax.shift_right_logical(_bf16_bits(x[:, (2 * c) * LANE:(2 * c + 1) * LANE]), 16)
        hi = _bf16_bits(x[:, (2 * c + 1) * LANE:(2 * c + 2) * LANE]) & HI_MASK
        ref[pl.ds(c, rows, stride=SLAB), :] = lo | hi


def _load_slabs(ref, dst, rows, dtype):
    for c in range(SLAB):
        w = ref[pl.ds(c, rows, stride=SLAB), :]
        lo = lax.bitcast_convert_type(lax.shift_left(w, 16), F32)
        hi = lax.bitcast_convert_type(w & HI_MASK, F32)
        dst[:, (2 * c) * LANE:(2 * c + 1) * LANE] = lo.astype(dtype)
        dst[:, (2 * c + 1) * LANE:(2 * c + 2) * LANE] = hi.astype(dtype)


def _route_kernel(mr_ref, y_ref, mod_ref, g_ref, wr_ref, br_ref, tri_ref, x_ref, wt_ref, meta_ref, cnt_ref, cnt_scr):
    x = _norm_mod(y_ref[...], g_ref[...], mod_ref[0], 1)
    _store_slabs(x_ref, x)
    lg = jnp.dot(x.astype(BF16), wr_ref[...], preferred_element_type=F32) + br_ref[...]
    lane = lax.broadcasted_iota(I32, lg.shape, 1).astype(F32)
    ninf = -jnp.inf
    big = float(LANE)
    lgg = jnp.where(lane < N_GROUPS, lg, ninf)
    gmax = jnp.max(lgg, axis=-1, keepdims=True)
    g_idx = jnp.min(jnp.where(lgg == gmax, lane, big), axis=-1, keepdims=True)
    g_w = 1.0 / jnp.sum(jnp.exp(lgg - gmax), axis=-1, keepdims=True)
    lo = ROUTE_OFF + g_idx * EXPERTS_PER_GROUP
    le = jnp.where((lane >= lo) & (lane < lo + EXPERTS_PER_GROUP), lg, ninf)
    m1 = jnp.max(le, axis=-1, keepdims=True)
    i1 = jnp.min(jnp.where(le == m1, lane, big), axis=-1, keepdims=True)
    le2 = jnp.where(lane == i1, ninf, le)
    m2 = jnp.max(le2, axis=-1, keepdims=True)
    i2 = jnp.min(jnp.where(le2 == m2, lane, big), axis=-1, keepdims=True)
    r = jnp.exp(m2 - m1)
    p1 = 1.0 / (1.0 + r)
    p2 = r / (1.0 + r)
    two = lax.broadcasted_iota(I32, (x.shape[0], TOP_K), 1)
    wt_ref[...] = jnp.where(two == 0, g_w * p1, g_w * p2)
    @pl.when(pl.program_id(0) == 0)
    def _():
        cnt_scr[...] = jnp.zeros(cnt_scr.shape, F32)

    oh1 = (lane == i1).astype(F32)
    oh2 = (lane == i2).astype(F32)
    both = oh1 + oh2
    before = jnp.dot(tri_ref[...], both.astype(BF16), preferred_element_type=F32) + cnt_scr[...]
    rk1 = jnp.sum(oh1 * before, axis=-1, keepdims=True)
    rk2 = jnp.sum(oh2 * before, axis=-1, keepdims=True)
    cnt_scr[...] = cnt_scr[...] + jnp.sum(both, axis=0, keepdims=True)
    cnt_ref[...] = cnt_scr[...]
    cols = (i1 - ROUTE_OFF, i2 - ROUTE_OFF, rk1, rk2)
    packed = jnp.zeros(lg.shape, F32)
    for c, val in enumerate(cols):
        packed = jnp.where(lane == c, val, packed)
    meta_ref[...] = jnp.transpose(packed)[0:len(cols), :].astype(I32)


def _route(y, mods, mrow, g2, w_route, b_route):
    return pl.pallas_call(
        _route_kernel,
        out_shape=(jax.ShapeDtypeStruct((N_TOK * SLAB, LANE), I32), jax.ShapeDtypeStruct((N_TOK, TOP_K), F32),
                   jax.ShapeDtypeStruct((2 * TOP_K, N_TOK), I32), jax.ShapeDtypeStruct((1, LANE), F32)),
        grid_spec=pltpu.PrefetchScalarGridSpec(
            num_scalar_prefetch=1, grid=(NL,),
            in_specs=[_tok_spec(D, TL), _mod_spec(), _full_spec((1, D)), _full_spec((D, LANE)),
                      _full_spec((1, LANE)), _full_spec((TL, TL))],
            out_specs=(pl.BlockSpec((TL * SLAB, LANE), lambda j, *_: (j, 0)), _tok_spec(TOP_K, TL),
                       pl.BlockSpec((2 * TOP_K, TL), lambda j, *_: (0, j)), _full_spec((1, LANE))),
            scratch_shapes=[pltpu.VMEM((1, LANE), F32)]),
        compiler_params=_cparams(),
        name="moe_route",
    )(mrow, y, mods, g2, w_route, b_route, jnp.asarray(np.tril(np.ones((TL, TL), np.float32), -1), dtype=BF16))


SLOT_COLS = 2048


def _slot_kernel(meta_ref, ps_ref, o_ref):
    sub = lax.broadcasted_iota(I32, (N_EXPERTS, SLOT_COLS), 0)
    meta = meta_ref[...]
    table = ps_ref[...]
    for k in range(TOP_K):
        start = jnp.sum(jnp.where(sub == meta[k:k + 1, :], table, 0), axis=0, keepdims=True)
        o_ref[k:k + 1, :] = start + meta[TOP_K + k:TOP_K + k + 1, :]


def _slots(meta, pad_start):
    return pl.pallas_call(
        _slot_kernel,
        out_shape=jax.ShapeDtypeStruct((TOP_K, N_TOK), I32),
        grid=(N_TOK // SLOT_COLS,),
        in_specs=[pl.BlockSpec((2 * TOP_K, SLOT_COLS), lambda j: (0, j)), pl.BlockSpec((N_EXPERTS, 1), lambda j: (0, 0))],
        out_specs=pl.BlockSpec((TOP_K, SLOT_COLS), lambda j: (0, j)),
        compiler_params=_cparams(),
        name="moe_slots",
    )(meta, pad_start.astype(I32).reshape(N_EXPERTS, 1))


def _dispatch_tables(meta, lane_counts):
    counts = lane_counts[0, ROUTE_OFF:ROUTE_OFF + N_EXPERTS].astype(I32)
    padded = ((counts + EBLK - 1) // EBLK) * EBLK
    pad_end = jnp.cumsum(padded)
    pad_start = pad_end - padded
    dest = _slots(meta, pad_start)
    n_blk = (padded // EBLK).astype(I32)
    blk_start = (pad_start // EBLK).astype(I32)
    n_used = (pad_end[-1] // EBLK).astype(I32).reshape(1)
    return dest, blk_start, n_blk, n_used


def _sc_mesh():
    return plsc.VectorSubcoreMesh(core_axis_name="core", subcore_axis_name="subcore",
                                  num_cores=V7X_SC_CORES, num_subcores=V7X_SC_SUBCORES)


def _sc_worker():
    return lax.axis_index("core") * V7X_SC_SUBCORES + lax.axis_index("subcore")


def _sc_dispatch(x_slabs, d0, d1):
    per = N_TOK // SC_WORKERS
    n_win = per // SC_HALF

    @functools.partial(
        pl.kernel, out_type=jax.ShapeDtypeStruct((P_SLOTS, SLAB, LANE), I32), mesh=_sc_mesh(), name="moe_dispatch",
        scratch_types=[pltpu.VMEM((1, per), I32), pltpu.VMEM((1, per), I32), pltpu.VMEM((2, SC_HALF, SLAB, LANE), I32),
                       pltpu.SemaphoreType.DMA((2,)), pltpu.SemaphoreType.DMA((2,))])
    def run(x_hbm, d0_hbm, d1_hbm, o_hbm, i0_v, i1_v, buf, lsem, ssem):
        base = _sc_worker() * per
        pltpu.sync_copy(d0_hbm.at[:, pl.ds(base, per)], i0_v)
        pltpu.sync_copy(d1_hbm.at[:, pl.ds(base, per)], i1_v)
        loads = [pltpu.make_async_copy(x_hbm.at[pl.ds(base + s * SC_HALF, SC_HALF)], buf.at[s % 2], lsem.at[s % 2])
                 for s in range(n_win)]
        loads[0].start()
        for s in range(n_win):
            loads[s].wait()
            if s + 1 < n_win:
                loads[s + 1].start()
            win = pl.ds(s * SC_HALF, SC_HALF)
            outs = [pltpu.make_async_copy(buf.at[s % 2], o_hbm.at[iv.at[0, win]], ssem.at[a])
                    for a, iv in enumerate((i0_v, i1_v))]
            for cp in outs:
                cp.start()
            for cp in outs:
                cp.wait()

    return run(x_slabs.reshape(N_TOK, SLAB, LANE), d0, d1)


def _sc_collect(y_slabs, dcat):
    per = N_ASSIGN // SC_WORKERS
    n_win = per // SC_HALF

    @functools.partial(
        pl.kernel, out_type=jax.ShapeDtypeStruct((N_ASSIGN, SLAB, LANE), I32), mesh=_sc_mesh(), name="moe_collect",
        scratch_types=[pltpu.VMEM((1, per), I32), pltpu.VMEM((2, SC_HALF, SLAB, LANE), I32),
                       pltpu.SemaphoreType.DMA((2,)), pltpu.SemaphoreType.DMA((2,))])
    def run(y_hbm, i_hbm, o_hbm, i_v, buf, gsem, wsem):
        base = _sc_worker() * per
        pltpu.sync_copy(i_hbm.at[:, pl.ds(base, per)], i_v)
        gathers = [pltpu.make_async_copy(y_hbm.at[i_v.at[0, pl.ds(s * SC_HALF, SC_HALF)]], buf.at[s % 2],
                                         gsem.at[s % 2]) for s in range(n_win)]
        writes = [pltpu.make_async_copy(buf.at[s % 2], o_hbm.at[pl.ds(base + s * SC_HALF, SC_HALF)], wsem.at[s % 2])
                  for s in range(n_win)]
        gathers[0].start()
        for s in range(n_win):
            gathers[s].wait()
            if s >= 1:
                writes[s - 1].wait()
            if s + 1 < n_win:
                gathers[s + 1].start()
            writes[s].start()
        writes[n_win - 1].wait()

    return run(y_slabs.reshape(P_SLOTS, SLAB, LANE), dcat)


EROWS = EBLK * SLAB


def _expert_kernel(bs_ref, nb_ref, nu_ref, wg_ref, wu_ref, wd_ref, x_hbm, y_hbm,
                   xbuf, ybuf, xs, wg_bf, wu_bf, wd_bf, isem, osem):
    e = pl.program_id(0)
    n_exp = pl.num_programs(0)
    n_used = nu_ref[0]
    b0 = bs_ref[e]
    nb = nb_ref[e]

    def in_copy(g, slot):
        return pltpu.make_async_copy(x_hbm.at[pl.ds(pl.multiple_of(g * EROWS, EROWS), EROWS)], xbuf.at[slot],
                                     isem.at[slot])

    def out_copy(g, slot):
        return pltpu.make_async_copy(ybuf.at[slot], y_hbm.at[pl.ds(pl.multiple_of(g * EROWS, EROWS), EROWS)],
                                     osem.at[slot])

    @pl.when(e == 0)
    def _():
        in_copy(0, 0).start()

    @pl.when(nb > 0)
    def _():
        wg_bf[...] = wg_ref[0, 0].astype(BF16)
        wu_bf[...] = wu_ref[0, 0].astype(BF16)
        wd_bf[...] = wd_ref[0, 0].astype(BF16)

    def block(k, carry):
        g = b0 + k
        slot = lax.rem(g, 2)
        in_copy(g, slot).wait()

        @pl.when(g + 1 < n_used)
        def _():
            in_copy(g + 1, 1 - slot).start()

        _load_slabs(xbuf.at[slot], xs, EBLK, BF16)
        xb = xs[...]
        gt = jnp.dot(xb, wg_bf[...], preferred_element_type=F32)
        up = jnp.dot(xb, wu_bf[...], preferred_element_type=F32)
        hmid = (gt * _sigmoid(gt) * up).astype(BF16)
        res = jnp.dot(hmid, wd_bf[...], preferred_element_type=F32)

        @pl.when(g >= 2)
        def _():
            out_copy(g - 2, slot).wait()

        _store_slabs(ybuf.at[slot], res)
        out_copy(g, slot).start()
        return carry

    lax.fori_loop(0, nb, block, 0)

    @pl.when(e == n_exp - 1)
    def _():
        last = n_used - 1
        out_copy(last, lax.rem(last, 2)).wait()

        @pl.when(n_used >= 2)
        def _():
            out_copy(last - 1, lax.rem(last - 1, 2)).wait()


def _experts(x_sorted, blk_start, n_blk, n_used, w_gate, w_up, w_down, layer):
    any_spec = pl.BlockSpec(memory_space=pl.ANY)
    wspec = lambda r, c: pl.BlockSpec((1, 1, r, c), lambda e, *_: (layer, e, 0, 0))
    return pl.pallas_call(
        _expert_kernel,
        out_shape=jax.ShapeDtypeStruct((P_SLOTS * SLAB, LANE), I32),
        grid_spec=pltpu.PrefetchScalarGridSpec(
            num_scalar_prefetch=3, grid=(N_EXPERTS,),
            in_specs=[wspec(D, D_EXPERT), wspec(D, D_EXPERT), wspec(D_EXPERT, D), any_spec],
            out_specs=any_spec,
            scratch_shapes=[
                pltpu.VMEM((2, EROWS, LANE), I32), pltpu.VMEM((2, EROWS, LANE), I32),
                pltpu.VMEM((EBLK, D), BF16),
                pltpu.VMEM((D, D_EXPERT), BF16), pltpu.VMEM((D, D_EXPERT), BF16), pltpu.VMEM((D_EXPERT, D), BF16),
                pltpu.SemaphoreType.DMA((2,)), pltpu.SemaphoreType.DMA((2,)),
            ]),
        compiler_params=_cparams(),
        name="moe_experts",
    )(blk_start, n_blk, n_used, w_gate, w_up, w_down, x_sorted.reshape(P_SLOTS * SLAB, LANE))


def _combine_kernel(final, mr_ref, e0_ref, e1_ref, wt_ref, y_ref, mod_ref, fg_ref, o_ref, a_scr, b_scr):
    _load_slabs(e0_ref, a_scr, TL, F32)
    _load_slabs(e1_ref, b_scr, TL, F32)
    wt = wt_ref[...]
    moe = wt[:, 0:1] * a_scr[...] + wt[:, 1:2] * b_scr[...]
    y_new = y_ref[...] + mod_ref[0][5:6] * moe
    o_ref[...] = _rms(y_new, fg_ref[...]) if final else y_new


def _combine(ym, wts, y, mods, mrow, final_g, blk0, nblk, final):
    tok = lambda width: pl.BlockSpec((TL, width), lambda j, *_: (blk0 + j, 0))
    slab0 = pl.BlockSpec((TL * SLAB, LANE), lambda j, *_: (blk0 + j, 0))
    slab1 = pl.BlockSpec((TL * SLAB, LANE), lambda j, *_: (NL + blk0 + j, 0))
    mod = pl.BlockSpec((1, 6, D), lambda j, mr: (mr[blk0 + j], 0, 0))
    return pl.pallas_call(
        functools.partial(_combine_kernel, final),
        out_shape=jax.ShapeDtypeStruct((nblk * TL, D), F32),
        grid_spec=pltpu.PrefetchScalarGridSpec(
            num_scalar_prefetch=1, grid=(nblk,),
            in_specs=[slab0, slab1, tok(TOP_K), tok(D), mod, _full_spec((1, D))],
            out_specs=pl.BlockSpec((TL, D), lambda j, *_: (j, 0)),
            scratch_shapes=[pltpu.VMEM((TL, D), F32), pltpu.VMEM((TL, D), F32)]),
        compiler_params=_cparams(),
        name="moe_combine",
    )(mrow, ym, ym, wts, y, mods, final_g)


def kernel(x_prompt, x_sample, cache_attn_k, cache_attn_v, state_mlstm_C, state_mlstm_n, state_mlstm_m, c, c_ctx, ada_w, ada_b, norm1_g, norm2_g, conv_w_in, conv_w_dw, conv_b_dw, conv_ln_g, conv_ln_b, conv_w_out, attn_w_qkv, attn_q_norm, attn_k_norm, attn_w_o, mlstm_w_in, mlstm_b_gate, mlstm_norm_g, mlstm_w_out, moe_w_group, moe_b_group, moe_w_router, moe_b_router, moe_w_gate, moe_w_up, moe_w_down, final_norm_g):
    y = None
    cvec = jnp.concatenate([c_ctx[None, :], c, jnp.zeros((MOD_ROWS - 1 - DEC_BATCH, D), F32)], axis=0)
    rope = _rope_blocks()
    mrow, mrow_sb, mrow_l = jnp.asarray(_MOD_ROW), jnp.asarray(_MOD_ROW_SB), jnp.asarray(_MOD_ROW_L)
    new_k = new_v = new_c = new_n = new_m = None
    for i in range(DEPTH):
        kind, slot = i % 3, i // 3
        mods = _ada_layer(cvec, ada_w, ada_b, i)
        g1 = norm1_g[i].reshape(1, D)
        if kind == 0:
            src = (x_prompt.reshape(NP_TOK, D), x_sample.reshape(NS_TOK, D), True) if i == 0 else (y, y, False)
            u = _conv_in(*src, mods, mrow_l, g1, conv_w_in[slot].astype(BF16))
            w_dw = jnp.concatenate([conv_w_dw[slot], jnp.zeros((1, D), F32)], axis=0)
            y = _conv_main(u, *src, mods, mrow_sb, w_dw, conv_b_dw[slot].reshape(1, D), conv_ln_g[slot].reshape(1, D),
                           conv_ln_b[slot].reshape(1, D), conv_w_out[slot].astype(BF16))
        elif kind == 1:
            q, kb, vb, kf, vf = _attn_qkv(y, mods, mrow_sb, g1, attn_w_qkv[slot].astype(BF16),
                                          attn_q_norm[slot].reshape(1, HEAD_DIM), attn_k_norm[slot].reshape(1, HEAD_DIM),
                                          rope)
            new_k = kf[:NP_TOK].reshape(BATCH, 1, SEQ, N_KV_HEADS, HEAD_DIM)
            new_v = vf[:NP_TOK].reshape(BATCH, 1, SEQ, N_KV_HEADS, HEAD_DIM)
            ck = cache_attn_k[:, slot].reshape(DEC_BATCH, PAST_LEN, KV_DIM)
            cv = cache_attn_v[:, slot].reshape(DEC_BATCH, PAST_LEN, KV_DIM)
            y = _attention(q, kb, vb, ck, cv, attn_w_o[slot].astype(BF16), y, mods)
        else:
            b_gate = jnp.concatenate([mlstm_b_gate[slot], jnp.zeros((LANE - 4 * M_HEADS,), F32)]).reshape(1, LANE)
            q, k, v, o, gates, grow = _mlstm_in(y, mods, mrow, g1, mlstm_w_in, slot, b_gate)
            sc = state_mlstm_C[:, slot]
            sn = state_mlstm_n[:, slot].reshape(DEC_BATCH, 2, M_HEADS, 1, M_HEAD_DIM)
            sm = state_mlstm_m[:, slot].reshape(DEC_BATCH, 2, M_HEADS, 1, 1)
            hsum, nc_, nn_, nm_ = _mlstm_scan(q, k, v, gates, grow, sc, sn, sm)
            new_c = nc_[:, None]
            new_n = nn_.reshape(BATCH, 1, 2, M_HEADS, M_HEAD_DIM)
            new_m = nm_[..., 0, 0].reshape(BATCH, 1, 2, M_HEADS)
            y = _mlstm_out(hsum, o, mlstm_norm_g[slot].reshape(1, D), mlstm_w_out[slot].astype(BF16), y, mods, mrow_l)
        w_route = jnp.concatenate([moe_w_group[i], moe_w_router[i],
                                   jnp.zeros((D, LANE - N_GROUPS - N_EXPERTS), F32)], axis=1)
        b_route = jnp.concatenate([moe_b_group[i], moe_b_router[i],
                                   jnp.zeros((LANE - N_GROUPS - N_EXPERTS,), F32)]).reshape(1, LANE)
        x2, ewt, meta, cnt = _route(y, mods, mrow_l, norm2_g[i].reshape(1, D), w_route.astype(BF16), b_route)
        dest, blk_start, n_blk, n_used = _dispatch_tables(meta, cnt)
        x_sorted = _sc_dispatch(x2, dest[0:1], dest[1:2])
        y_sorted = _experts(x_sorted, blk_start, n_blk, n_used, moe_w_gate, moe_w_up, moe_w_down, i)
        ym = _sc_collect(y_sorted, dest.reshape(1, N_ASSIGN))
        ym = ym.reshape(N_ASSIGN * SLAB, LANE)
        fg = final_norm_g.reshape(1, D)
        if i + 1 < DEPTH:
            y = _combine(ym, ewt, y, mods, mrow_l, fg, 0, NL, False)
        else:
            y_prompt = _combine(ym, ewt, y, mods, mrow_l, fg, 0, NLP, True).reshape(BATCH, SEQ, D)
            y_sample = _combine(ym, ewt, y, mods, mrow_l, fg, NLP, NL - NLP, True).reshape(DEC_BATCH, DEC_SEQ, D)
    return (y_prompt, y_sample, new_k, new_v, new_c, new_n, new_m)
```

```python
import functools

import jax
import jax.numpy as jnp
import numpy as np
from jax import lax
from jax.experimental import pallas as pl
from jax.experimental.pallas import tpu as pltpu
from jax.experimental.pallas import tpu_sc as plsc

F32 = jnp.float32
BF16 = jnp.bfloat16
I32 = jnp.int32

D = 1024
BATCH, SEQ = 16, 256
DEC_BATCH, DEC_SEQ = 8, 1024
PAST_LEN = 256
DEPTH = 4
GRID_W = 64
EPS = 1e-6
CONV_WIDTH = 31
CONV_PAD = CONV_WIDTH // 2
HEAD_DIM = 128
N_HEADS = 8
N_KV_HEADS = 2
GQA_GROUP = N_HEADS // N_KV_HEADS
Q_DIM = N_HEADS * HEAD_DIM
KV_DIM = N_KV_HEADS * HEAD_DIM
QKV_DIM = Q_DIM + 2 * KV_DIM
ROPE_THETA = 10000.0
M_HEADS = 4
M_HEAD_DIM = D // M_HEADS
M_CHUNK = 64
N_GROUPS = 4
EXPERTS_PER_GROUP = 8
N_EXPERTS = N_GROUPS * EXPERTS_PER_GROUP
TOP_K = 2
D_EXPERT = 512

NP_TOK = BATCH * SEQ
NS_TOK = DEC_BATCH * DEC_SEQ
N_TOK = NP_TOK + NS_TOK
TM = 512
NB = N_TOK // TM
NBP = NP_TOK // TM
BLK_PER_DEC = DEC_SEQ // TM
TL = 1024
NL = N_TOK // TL
NLP = NP_TOK // TL
SB = 256
NSB = N_TOK // SB
NSBP = NP_TOK // SB
SB_PER_DEC = DEC_SEQ // SB
MOD_ROWS = 16
HALO = 16
LANE = 128
SUBLANE = 8

N_ASSIGN = N_TOK * TOP_K
EBLK = 256
N_EBLK = N_ASSIGN // EBLK + N_EXPERTS
P_SLOTS = N_EBLK * EBLK
N_PAD_SLOTS = P_SLOTS - N_ASSIGN

VMEM_LIMIT = 56 * 1024 * 1024


def _block_tables(nb, nbp, per_dec):
    j = np.arange(nb)
    is_p = j < nbp
    mod_row = np.where(is_p, 0, 1 + (j - nbp) // per_dec)
    rope_idx = np.where(is_p, 0, 1 + (j - nbp) % per_dec)
    first = np.where(is_p, 1, ((j - nbp) % per_dec == 0).astype(np.int64))
    last = np.where(is_p, 1, ((j - nbp) % per_dec == per_dec - 1).astype(np.int64))
    return (mod_row.astype(np.int32), rope_idx.astype(np.int32), first.astype(np.int32), last.astype(np.int32))


_MOD_ROW, _, _, _ = _block_tables(NB, NBP, BLK_PER_DEC)
_MOD_ROW_L, _, _, _ = _block_tables(NL, NLP, DEC_SEQ // TL)
_MOD_ROW_SB, _ROPE_IDX_SB, _SEQ_FIRST, _SEQ_LAST = _block_tables(NSB, NSBP, SB_PER_DEC)


def _cparams(n_axes=1):
    return pltpu.CompilerParams(dimension_semantics=("arbitrary",) * n_axes, vmem_limit_bytes=VMEM_LIMIT)


def _sigmoid(x):
    return 1.0 / (1.0 + jnp.exp(-x))


def _rms(x, g):
    return x * lax.rsqrt(jnp.mean(x * x, axis=-1, keepdims=True) + EPS) * g


def _mod_vec(mod_ref, k):
    return mod_ref[0, :, k * D:(k + 1) * D]


def _norm_mod(y, g, mod_ref, which):
    return _rms(y, g) * (1.0 + _mod_vec(mod_ref, 3 * which + 1)) + _mod_vec(mod_ref, 3 * which)


def _ada_kernel(c_ref, w_ref, b_ref, o_ref):
    c = c_ref[...]
    s = c * _sigmoid(c)
    res = jnp.dot(s.astype(BF16), w_ref[0].astype(BF16), preferred_element_type=F32) + b_ref[0]
    for r in range(MOD_ROWS):
        o_ref[r] = res[r:r + 1, :]


def _ada_layer(cvec, ada_w, ada_b, layer):
    tn = 1536
    return pl.pallas_call(
        _ada_kernel,
        out_shape=jax.ShapeDtypeStruct((MOD_ROWS, 1, 6 * D), F32),
        grid=(6 * D // tn,),
        in_specs=[
            pl.BlockSpec((MOD_ROWS, D), lambda n: (0, 0)),
            pl.BlockSpec((1, D, tn), lambda n: (layer, 0, n)),
            pl.BlockSpec((1, 1, tn), lambda n: (layer, 0, n)),
        ],
        out_specs=pl.BlockSpec((MOD_ROWS, 1, tn), lambda n: (0, 0, n)),
        compiler_params=_cparams(1),
        name="ada_mod",
    )(cvec, ada_w, ada_b.reshape(DEPTH, 1, 6 * D))


def _tok_spec(width, rows=TM):
    return pl.BlockSpec((rows, width), lambda j, *_: (j, 0))


def _mod_spec():
    return pl.BlockSpec((1, 1, 6 * D), lambda j, mr, *_: (mr[j], 0, 0))


def _full_spec(shape):
    nd = len(shape)
    return pl.BlockSpec(shape, lambda j, *_: (0,) * nd)


def _pair_specs(rows, nbp, split):
    s_off = nbp if split else 0
    return [pl.BlockSpec((rows, D), lambda j, *_: (jnp.minimum(j, nbp - 1), 0)),
            pl.BlockSpec((rows, D), lambda j, *_: (jnp.maximum(j, nbp) - s_off, 0))]


def _pair_block(nbp, yp_ref, ys_ref):
    return jnp.where(pl.program_id(0) < nbp, yp_ref[...], ys_ref[...])


def _resident_weight_spec(w_all, slot):
    return pl.BlockSpec((1,) + w_all.shape[1:], lambda j, *_: (slot, 0, 0), pipeline_mode=pl.Buffered(1))


def _conv_in_kernel(mr_ref, yp_ref, ys_ref, mod_ref, g_ref, wf_ref, u_ref, w_ref):
    @pl.when(pl.program_id(0) == 0)
    def _():
        w_ref[...] = wf_ref[0].astype(BF16)

    h = _norm_mod(_pair_block(NLP, yp_ref, ys_ref), g_ref[...], mod_ref, 0)
    ag = jnp.dot(h.astype(BF16), w_ref[...], preferred_element_type=F32)
    u_ref[...] = ag[:, :D] * _sigmoid(ag[:, D:])


def _conv_in(yp, ys, split, mods, mrow, g1, w_in_all, slot):
    return pl.pallas_call(
        _conv_in_kernel,
        out_shape=jax.ShapeDtypeStruct((N_TOK, D), F32),
        grid_spec=pltpu.PrefetchScalarGridSpec(
            num_scalar_prefetch=1, grid=(NL,),
            in_specs=_pair_specs(TL, NLP, split) + [_mod_spec(), _full_spec((1, D)),
                                                    _resident_weight_spec(w_in_all, slot)],
            out_specs=_tok_spec(D, TL),
            scratch_shapes=[pltpu.VMEM((D, 2 * D), BF16)]),
        compiler_params=_cparams(),
        name="conv_in",
    )(mrow, yp, ys, mods, g1, w_in_all)


def _conv_main_kernel(mr_ref, first_ref, last_ref, u_ref, up_ref, un_ref, wdw_ref, bdw_ref, lg_ref, lb_ref,
                      woutf_ref, yp_ref, ys_ref, mod_ref, o_ref, ext_ref, acc_ref, wout_ref):
    j = pl.program_id(0)

    @pl.when(j == 0)
    def _():
        wout_ref[...] = woutf_ref[0].astype(BF16)

    zero = jnp.zeros((HALO, D), F32)
    ext_ref[0:HALO, :] = jnp.where(first_ref[j] == 1, zero, up_ref[...])
    ext_ref[HALO:HALO + SB, :] = u_ref[...]
    ext_ref[HALO + SB:2 * HALO + SB, :] = jnp.where(last_ref[j] == 1, zero, un_ref[...])

    off0 = HALO - CONV_PAD
    n_a = (off0 + CONV_WIDTH - 1) // SUBLANE + 1
    n_chunks = SB // SUBLANE

    def strip(ci, carry):
        cs = pl.ds(pl.multiple_of(ci * LANE, LANE), LANE)
        wk = [jnp.broadcast_to(wdw_ref[k:k + 1, cs], (SUBLANE, LANE)) for k in range(CONV_WIDTH)]
        bias = jnp.broadcast_to(bdw_ref[:, cs], (SUBLANE, LANE))
        sub = lax.broadcasted_iota(I32, (SUBLANE, LANE), 0)
        prev_rot, prev_v0 = None, None
        for j in range(n_chunks + 1):
            tiles = [ext_ref[SUBLANE * (j + a):SUBLANE * (j + a + 1), cs] for a in range(n_a)]
            part = []
            for s in range(SUBLANE):
                acc = None
                for a in range(n_a):
                    k = SUBLANE * a + s - off0
                    if (0 <= k < CONV_WIDTH) and not (s == 0 and j == n_chunks):
                        term = tiles[a] * wk[k]
                        acc = term if acc is None else acc + term
                part.append(acc)
            rot = [None] + [pltpu.roll(part[s], SUBLANE - s, 0) for s in range(1, SUBLANE)]
            if j >= 1:
                out = prev_v0 + bias
                for s in range(1, SUBLANE):
                    out = out + jnp.where(sub < SUBLANE - s, prev_rot[s], rot[s])
                acc_ref[SUBLANE * (j - 1):SUBLANE * j, cs] = out
            prev_rot, prev_v0 = rot, part[0]
        return carry

    lax.fori_loop(0, D // LANE, strip, 0)

    c = acc_ref[...]
    mu = jnp.mean(c, axis=-1, keepdims=True)
    cc = c - mu
    var = jnp.mean(cc * cc, axis=-1, keepdims=True)
    z = cc * lax.rsqrt(var + EPS) * lg_ref[...] + lb_ref[...]
    z = z * _sigmoid(z)
    out = jnp.dot(z.astype(BF16), wout_ref[...], preferred_element_type=F32)
    o_ref[...] = _pair_block(NSBP, yp_ref, ys_ref) + _mod_vec(mod_ref, 2) * out


def _conv_main(u, yp, ys, split, mods, mrow, w_dw, b_dw, ln_g, ln_b, w_out_all, slot):
    nh = N_TOK // HALO
    per = SB // HALO
    sb_spec = pl.BlockSpec((SB, D), lambda j, *_: (j, 0))
    return pl.pallas_call(
        _conv_main_kernel,
        out_shape=jax.ShapeDtypeStruct((N_TOK, D), F32),
        grid_spec=pltpu.PrefetchScalarGridSpec(
            num_scalar_prefetch=3, grid=(NSB,),
            in_specs=[
                sb_spec,
                pl.BlockSpec((HALO, D), lambda j, *_: (jnp.maximum(j * per - 1, 0), 0)),
                pl.BlockSpec((HALO, D), lambda j, *_: (jnp.minimum((j + 1) * per, nh - 1), 0)),
                _full_spec((CONV_WIDTH + 1, D)), _full_spec((1, D)), _full_spec((1, D)), _full_spec((1, D)),
                _resident_weight_spec(w_out_all, slot), *_pair_specs(SB, NSBP, split), _mod_spec(),
            ],
            out_specs=sb_spec,
            scratch_shapes=[pltpu.VMEM((SB + 2 * HALO, D), F32), pltpu.VMEM((SB, D), F32), pltpu.VMEM((D, D), BF16)]),
        compiler_params=_cparams(),
        name="conv_main",
    )(mrow, jnp.asarray(_SEQ_FIRST), jnp.asarray(_SEQ_LAST), u, u, u, w_dw, b_dw, ln_g, ln_b, w_out_all, yp, ys, mods)


def _rope_angles():
    rows = DEC_SEQ // GRID_W
    row = jnp.repeat(jnp.arange(rows, dtype=F32), GRID_W)
    col = jnp.tile(jnp.arange(GRID_W, dtype=F32), rows)
    axis_dim = HEAD_DIM // 2
    freqs = jnp.power(ROPE_THETA, -jnp.arange(axis_dim // 2, dtype=F32) * 2.0 / axis_dim)
    ang_r = row[:, None] * freqs[None, :]
    ang_c = col[:, None] * freqs[None, :]
    return jnp.concatenate([ang_r, ang_r, ang_c, ang_c], axis=-1)


def _rope_blocks():
    ang = _rope_angles()
    cos, sin = jnp.cos(ang), jnp.sin(ang)
    lane = np.arange(HEAD_DIM)
    lo = jnp.asarray(((lane % (HEAD_DIM // 2)) < HEAD_DIM // 4).astype(np.float32))
    sin_a = -sin * lo[None, :]
    sin_b = sin * (1.0 - lo)[None, :]
    nblk = DEC_SEQ // SB
    ident = jnp.ones((1, SB, HEAD_DIM), F32)
    zeros = jnp.zeros((1, SB, HEAD_DIM), F32)
    cos_t = jnp.concatenate([ident, cos.reshape(nblk, SB, HEAD_DIM)], axis=0)
    sa_t = jnp.concatenate([zeros, sin_a.reshape(nblk, SB, HEAD_DIM)], axis=0)
    sb_t = jnp.concatenate([zeros, sin_b.reshape(nblk, SB, HEAD_DIM)], axis=0)
    return cos_t, sa_t, sb_t


def _attn_qkv_kernel(mr_ref, ri_ref, y_ref, mod_ref, g_ref, w_ref, qg_ref, kg_ref, cos_ref, sa_ref, sb_ref,
                     q_ref, kb_ref, vb_ref, kf_ref, vf_ref):
    h = _norm_mod(y_ref[...], g_ref[...], mod_ref, 0)
    qkv = jnp.dot(h.astype(BF16), w_ref[...], preferred_element_type=F32)
    cos, sa, sb = cos_ref[0], sa_ref[0], sb_ref[0]
    quarter = HEAD_DIM // 4

    def head(x, g):
        xn = _rms(x, g)
        return xn * cos + pltpu.roll(xn, HEAD_DIM - quarter, 1) * sa + pltpu.roll(xn, quarter, 1) * sb

    scale = HEAD_DIM ** -0.5
    for hd in range(N_HEADS):
        sl = slice(hd * HEAD_DIM, (hd + 1) * HEAD_DIM)
        q_ref[:, sl] = (head(qkv[:, sl], qg_ref[...]) * scale).astype(BF16)
    for kv in range(N_KV_HEADS):
        sl = slice(kv * HEAD_DIM, (kv + 1) * HEAD_DIM)
        kr = head(qkv[:, Q_DIM + kv * HEAD_DIM:Q_DIM + (kv + 1) * HEAD_DIM], kg_ref[...])
        kf_ref[:, sl] = kr
        kb_ref[:, sl] = kr.astype(BF16)
    v = qkv[:, Q_DIM + KV_DIM:]
    vf_ref[...] = v
    vb_ref[...] = v.astype(BF16)


def _attn_qkv(y, mods, mrow, g1, w_qkv, q_g, k_g, rope):
    cos_t, sa_t, sb_t = rope
    rspec = pl.BlockSpec((1, SB, HEAD_DIM), lambda j, mr, ri: (ri[j], 0, 0))
    return pl.pallas_call(
        _attn_qkv_kernel,
        out_shape=(jax.ShapeDtypeStruct((N_TOK, Q_DIM), BF16), jax.ShapeDtypeStruct((N_TOK, KV_DIM), BF16),
                   jax.ShapeDtypeStruct((N_TOK, KV_DIM), BF16), jax.ShapeDtypeStruct((N_TOK, KV_DIM), F32),
                   jax.ShapeDtypeStruct((N_TOK, KV_DIM), F32)),
        grid_spec=pltpu.PrefetchScalarGridSpec(
            num_scalar_prefetch=2, grid=(NSB,),
            in_specs=[_tok_spec(D, SB), _mod_spec(), _full_spec((1, D)), _full_spec((D, QKV_DIM)),
                      _full_spec((1, HEAD_DIM)), _full_spec((1, HEAD_DIM)), rspec, rspec, rspec],
            out_specs=(_tok_spec(Q_DIM, SB), _tok_spec(KV_DIM, SB), _tok_spec(KV_DIM, SB), _tok_spec(KV_DIM, SB),
                       _tok_spec(KV_DIM, SB))),
        compiler_params=_cparams(),
        name="attn_qkv",
    )(mrow, jnp.asarray(_ROPE_IDX_SB), y, mods, g1, w_qkv, q_g, k_g, cos_t, sa_t, sb_t)


def _attn_heads(q, ks, vs, o_scr):
    nt = (((1,), (1,)), ((), ()))
    for hd in range(N_HEADS):
        g = hd // GQA_GROUP
        qh = q[:, hd * HEAD_DIM:(hd + 1) * HEAD_DIM]
        gs = slice(g * HEAD_DIM, (g + 1) * HEAD_DIM)
        ss = [lax.dot_general(qh, k[:, gs], nt, preferred_element_type=F32) for k in ks]
        m = functools.reduce(jnp.maximum, [jnp.max(s, axis=-1, keepdims=True) for s in ss])
        ps = [jnp.exp(s - m) for s in ss]
        l = functools.reduce(lambda a, b: a + b, [jnp.sum(p, axis=-1, keepdims=True) for p in ps])
        o = functools.reduce(lambda a, b: a + b,
                             [jnp.dot(p.astype(BF16), v[:, gs], preferred_element_type=F32) for p, v in zip(ps, vs)])
        o_scr[:, hd * HEAD_DIM:(hd + 1) * HEAD_DIM] = (o / l).astype(BF16)


def _attn_ctx_kernel(q_ref, k_ref, v_ref, wo_ref, y_ref, mod_ref, o_ref, o_scr):
    _attn_heads(q_ref[...], [k_ref[...]], [v_ref[...]], o_scr)
    out = jnp.dot(o_scr[...], wo_ref[...], preferred_element_type=F32)
    o_ref[...] = y_ref[...] + _mod_vec(mod_ref, 2) * out


def _attn_lat_kernel(q_ref, k_ref, v_ref, ck_ref, cv_ref, wo_ref, y_ref, mod_ref, ctx_out_ref, o_ref, o_scr):
    del ctx_out_ref
    _attn_heads(q_ref[...], [k_ref[...], ck_ref[0].astype(BF16)], [v_ref[...], cv_ref[0].astype(BF16)], o_scr)
    out = jnp.dot(o_scr[...], wo_ref[...], preferred_element_type=F32)
    o_ref[...] = y_ref[...] + _mod_vec(mod_ref, 2) * out


def _attention(q, kb, vb, cache_k, cache_v, w_o, y, mods):
    y_ctx = pl.pallas_call(
        _attn_ctx_kernel,
        out_shape=jax.ShapeDtypeStruct((N_TOK, D), F32),
        grid=(BATCH,),
        in_specs=[
            pl.BlockSpec((SEQ, Q_DIM), lambda s: (s, 0)),
            pl.BlockSpec((SEQ, KV_DIM), lambda s: (s, 0)),
            pl.BlockSpec((SEQ, KV_DIM), lambda s: (s, 0)),
            pl.BlockSpec((Q_DIM, D), lambda s: (0, 0)),
            pl.BlockSpec((SEQ, D), lambda s: (s, 0)),
            pl.BlockSpec((1, 1, 6 * D), lambda s: (0, 0, 0)),
        ],
        out_specs=pl.BlockSpec((SEQ, D), lambda s: (s, 0)),
        scratch_shapes=[pltpu.VMEM((SEQ, Q_DIM), BF16)],
        compiler_params=_cparams(),
        name="attn_ctx",
    )(q, kb, vb, w_o, y, mods)
    pb = NP_TOK // DEC_SEQ
    return pl.pallas_call(
        _attn_lat_kernel,
        out_shape=jax.ShapeDtypeStruct((N_TOK, D), F32),
        input_output_aliases={8: 0},
        grid=(DEC_BATCH, SB_PER_DEC),
        in_specs=[
            pl.BlockSpec((SB, Q_DIM), lambda b, t: (NSBP + b * SB_PER_DEC + t, 0)),
            pl.BlockSpec((DEC_SEQ, KV_DIM), lambda b, t: (pb + b, 0)),
            pl.BlockSpec((DEC_SEQ, KV_DIM), lambda b, t: (pb + b, 0)),
            pl.BlockSpec((1, PAST_LEN, KV_DIM), lambda b, t: (b, 0, 0)),
            pl.BlockSpec((1, PAST_LEN, KV_DIM), lambda b, t: (b, 0, 0)),
            pl.BlockSpec((Q_DIM, D), lambda b, t: (0, 0)),
            pl.BlockSpec((SB, D), lambda b, t: (NSBP + b * SB_PER_DEC + t, 0)),
            pl.BlockSpec((1, 1, 6 * D), lambda b, t: (1 + b, 0, 0)),
            pl.BlockSpec(memory_space=pl.ANY),
        ],
        out_specs=pl.BlockSpec((SB, D), lambda b, t: (NSBP + b * SB_PER_DEC + t, 0)),
        scratch_shapes=[pltpu.VMEM((SB, Q_DIM), BF16)],
        compiler_params=_cparams(2),
        name="attn_lat",
    )(q, kb, vb, cache_k, cache_v, w_o, y, mods, y_ctx)


def _log_sigmoid(x):
    return jnp.minimum(x, 0.0) - jnp.log(1.0 + jnp.exp(-jnp.abs(x)))


W_T = 256


def _mlstm_in_kernel(mr_ref, y_ref, mod_ref, g_ref, wt_ref, bg_ref, q_ref, k_ref, v_ref, o_ref, gt_ref, gr_ref,
                     w_ref, wg_ref):
    @pl.when(pl.program_id(0) == 0)
    def _():
        for r in range(0, 4 * D, W_T):
            w_ref[:, r:r + W_T] = jnp.transpose(wt_ref[0, r:r + W_T, :]).astype(BF16)
        n_out = 4 * D + 4 * M_HEADS
        tail = jnp.transpose(wt_ref[0, n_out - LANE:n_out, :])
        lane = lax.broadcasted_iota(I32, tail.shape, 1)
        wg_ref[...] = jnp.where(lane < 4 * M_HEADS, pltpu.roll(tail, 4 * M_HEADS, axis=1), 0.0).astype(BF16)

    h = _norm_mod(y_ref[...], g_ref[...], mod_ref, 0)
    hb = h.astype(BF16)
    q_ref[...] = jnp.dot(hb, w_ref[:, 0:D], preferred_element_type=F32).astype(BF16)
    k_ref[...] = (jnp.dot(hb, w_ref[:, D:2 * D], preferred_element_type=F32) * (M_HEAD_DIM ** -0.5)).astype(BF16)
    v_ref[...] = jnp.dot(hb, w_ref[:, 2 * D:3 * D], preferred_element_type=F32).astype(BF16)
    o_ref[...] = _sigmoid(jnp.dot(hb, w_ref[:, 3 * D:4 * D], preferred_element_type=F32))
    gt = jnp.dot(hb, wg_ref[...], preferred_element_type=F32) + bg_ref[...]
    lane = lax.broadcasted_iota(I32, gt.shape, 1)
    is_f = ((lane >= M_HEADS) & (lane < 2 * M_HEADS)) | ((lane >= 3 * M_HEADS) & (lane < 4 * M_HEADS))
    gt = jnp.where(is_f, _log_sigmoid(gt), gt)
    gt_ref[...] = gt
    per = LANE // M_CHUNK
    for p in range(TM // LANE):
        t = jnp.transpose(gt[p * LANE:(p + 1) * LANE, :])
        for f in range(per):
            gr_ref[p * per + f] = t[0:4 * M_HEADS, f * M_CHUNK:(f + 1) * M_CHUNK]


def _mlstm_in(y, mods, mrow, g1, w_in_all, slot, b_gate):
    w_t = jnp.swapaxes(w_in_all, 1, 2)
    w_spec = pl.BlockSpec((1,) + w_t.shape[1:], lambda j, *_: (slot, 0, 0), pipeline_mode=pl.Buffered(1))
    return pl.pallas_call(
        _mlstm_in_kernel,
        out_shape=(jax.ShapeDtypeStruct((N_TOK, D), BF16), jax.ShapeDtypeStruct((N_TOK, D), BF16),
                   jax.ShapeDtypeStruct((N_TOK, D), BF16), jax.ShapeDtypeStruct((N_TOK, D), F32),
                   jax.ShapeDtypeStruct((N_TOK, LANE), F32),
                   jax.ShapeDtypeStruct((N_TOK // M_CHUNK, 4 * M_HEADS, M_CHUNK), F32)),
        grid_spec=pltpu.PrefetchScalarGridSpec(
            num_scalar_prefetch=1, grid=(NB,),
            in_specs=[_tok_spec(D), _mod_spec(), _full_spec((1, D)), w_spec, _full_spec((1, LANE))],
            out_specs=(_tok_spec(D), _tok_spec(D), _tok_spec(D), _tok_spec(D), _tok_spec(LANE),
                       pl.BlockSpec((TM // M_CHUNK, 4 * M_HEADS, M_CHUNK), lambda j, *_: (j, 0, 0))),
            scratch_shapes=[pltpu.VMEM((D, 4 * D), BF16), pltpu.VMEM((D, LANE), BF16)]),
        compiler_params=_cparams(),
        name="mlstm_in",
    )(mrow, y, mods, g1, w_t, b_gate)


def _mlstm_load(hd, c, q_ref, k_ref, v_ref, gc_ref, gr_ref):
    r0 = pl.multiple_of(c * M_CHUNK, M_CHUNK)
    hs = slice(hd * M_HEAD_DIM, (hd + 1) * M_HEAD_DIM)
    rows = pl.ds(r0, M_CHUNK)
    return rows, hs, q_ref[rows, hs], k_ref[rows, hs], v_ref[rows, hs], gc_ref[rows, :], gr_ref[c]


def _mlstm_chunks(chains, ms, loaded, c_scr, n_scr):
    L = M_CHUNK
    n = range(len(chains))
    t_idx = lax.broadcasted_iota(I32, (L, L), 0)
    s_idx = lax.broadcasted_iota(I32, (L, L), 1)
    masks = {0: (s_idx <= t_idx, t_idx <= s_idx), 1: (s_idx >= t_idx, t_idx >= s_idx)}
    q = [ld[2] for ld in loaded]
    k = [ld[3] for ld in loaded]
    v = [ld[4] for ld in loaded]
    gi = [2 * d * M_HEADS + hd for hd, d in chains]
    gf = [(2 * d + 1) * M_HEADS + hd for hd, d in chains]
    i_col = [ld[5][:, gi[i]:gi[i] + 1] for i, ld in enumerate(loaded)]
    lf_col = [ld[5][:, gf[i]:gf[i] + 1] for i, ld in enumerate(loaded)]
    i_row = [ld[6][gi[i]:gi[i] + 1, :] for i, ld in enumerate(loaded)]
    lf_row = [ld[6][gf[i]:gf[i] + 1, :] for i, ld in enumerate(loaded)]
    mask = [masks[d][0] for _, d in chains]
    mask_t = [masks[d][1] for _, d in chains]
    b_col = [jnp.sum(jnp.where(mask[i], lf_row[i], 0.0), axis=1, keepdims=True) for i in n]
    b_row = [jnp.sum(jnp.where(mask_t[i], lf_col[i], 0.0), axis=0, keepdims=True) for i in n]
    log_d = [jnp.where(mask[i], b_col[i] - b_row[i] + i_row[i], -jnp.inf) for i in n]
    li = [b_col[i] + ms[i] for i in n]
    m_r = [jnp.maximum(li[i], jnp.max(log_d[i], axis=1, keepdims=True)) for i in n]
    a_int = [jnp.exp(li[i] - m_r[i]) for i in n]
    dmat = [jnp.exp(log_d[i] - m_r[i]) for i in n]
    cmat = [c_scr[d, hd] for hd, d in chains]
    nvec = [n_scr[d, hd] for hd, d in chains]
    gram = [lax.dot_general(q[i], k[i], (((1,), (1,)), ((), ())), preferred_element_type=F32) for i in n]
    inter = [jnp.dot(q[i], cmat[i].astype(BF16), preferred_element_type=F32) for i in n]
    s = [gram[i] * dmat[i] for i in n]
    intra = [jnp.dot(s[i].astype(BF16), v[i], preferred_element_type=F32) for i in n]
    qn = [jnp.sum(q[i].astype(F32) * nvec[i], axis=1, keepdims=True) for i in n]
    den = [a_int[i] * qn[i] + jnp.sum(s[i], axis=1, keepdims=True) for i in n]
    hh = [(a_int[i] * inter[i] + intra[i]) / jnp.maximum(jnp.abs(den[i]), jnp.exp(-m_r[i])) for i in n]
    b_last = [b_row[i][:, L - 1:L] if chains[i][1] == 0 else b_row[i][:, 0:1] for i in n]
    log_w = [b_last[i] - b_col[i] + i_col[i] for i in n]
    m_new = [jnp.maximum(b_last[i] + ms[i], jnp.max(log_w[i], axis=0, keepdims=True)) for i in n]
    w = [jnp.exp(log_w[i] - m_new[i]) for i in n]
    decay = [jnp.exp(b_last[i] + ms[i] - m_new[i]) for i in n]
    kw = [k[i].astype(F32) * w[i] for i in n]
    kv = [lax.dot_general(kw[i].astype(BF16), v[i], (((0,), (0,)), ((), ())), preferred_element_type=F32) for i in n]
    for i, (hd, d) in enumerate(chains):
        c_scr[d, hd] = decay[i] * cmat[i] + kv[i]
        n_scr[d, hd] = decay[i] * nvec[i] + jnp.sum(kw[i], axis=0, keepdims=True)
    return hh, m_new


SCAN_GROUP = 2 * M_HEADS


def _mlstm_scan_body(n_chunks, q_ref, k_ref, v_ref, gc_ref, gr_ref, h_ref, hb_scr, c_scr, n_scr, m0):
    chains = [(hd, d) for hd in range(M_HEADS) for d in range(2)]

    def body(c, ms):
        out = []
        for g0 in range(0, len(chains), SCAN_GROUP):
            grp = chains[g0:g0 + SCAN_GROUP]
            loaded = [_mlstm_load(hd, c if d == 0 else n_chunks - 1 - c, q_ref, k_ref, v_ref, gc_ref, gr_ref)
                      for hd, d in grp]
            hh, m_new = _mlstm_chunks(grp, ms[g0:g0 + SCAN_GROUP], loaded, c_scr, n_scr)
            for (hd, d), ld, h in zip(grp, loaded, hh):
                dst = h_ref if d == 0 else hb_scr
                dst[ld[0], ld[1]] = h
            out += m_new
        return tuple(out)

    ms = lax.fori_loop(0, n_chunks, body, tuple(m0))
    h_ref[...] += hb_scr[...]
    return ms


def _mlstm_scan_ctx_kernel(q_ref, k_ref, v_ref, gc_ref, gr_ref, h_ref, cn_ref, nn_ref, mn_ref, hb_scr, c_scr, n_scr):
    c_scr[...] = jnp.zeros(c_scr.shape, F32)
    n_scr[...] = jnp.zeros(n_scr.shape, F32)
    zero = jnp.zeros((1, 1), F32)
    ms = _mlstm_scan_body(SEQ // M_CHUNK, q_ref, k_ref, v_ref, gc_ref, gr_ref, h_ref, hb_scr, c_scr, n_scr,
                          [zero] * (2 * M_HEADS))
    cn_ref[0] = c_scr[...]
    nn_ref[0] = n_scr[...]
    for hd in range(M_HEADS):
        for d in range(2):
            mn_ref[0, d, hd] = jnp.broadcast_to(ms[2 * hd + d], (1, LANE))


def _mlstm_scan_lat_kernel(q_ref, k_ref, v_ref, gc_ref, gr_ref, c0_ref, n0_ref, m0_ref, ctx_out_ref, h_ref,
                           hb_scr, c_scr, n_scr):
    del ctx_out_ref
    c_scr[...] = c0_ref[0]
    n_scr[...] = n0_ref[0]
    m0 = [m0_ref[0, d, hd] for hd in range(M_HEADS) for d in range(2)]
    _mlstm_scan_body(DEC_SEQ // M_CHUNK, q_ref, k_ref, v_ref, gc_ref, gr_ref, h_ref, hb_scr, c_scr, n_scr, m0)


def _mlstm_scan(q, k, v, gcol, grow, state_c, state_n, state_m):
    hd = M_HEAD_DIM
    ng = 4 * M_HEADS
    state_scratch = [pltpu.VMEM((2, M_HEADS, hd, hd), F32), pltpu.VMEM((2, M_HEADS, 1, hd), F32)]
    ncp = SEQ // M_CHUNK
    h_ctx, new_c, new_n, new_m = pl.pallas_call(
        _mlstm_scan_ctx_kernel,
        out_shape=(jax.ShapeDtypeStruct((N_TOK, D), F32),
                   jax.ShapeDtypeStruct((BATCH, 2, M_HEADS, hd, hd), F32),
                   jax.ShapeDtypeStruct((BATCH, 2, M_HEADS, 1, hd), F32),
                   jax.ShapeDtypeStruct((BATCH, 2, M_HEADS, 1, LANE), F32)),
        grid=(BATCH,),
        in_specs=[
            pl.BlockSpec((SEQ, D), lambda s: (s, 0)),
            pl.BlockSpec((SEQ, D), lambda s: (s, 0)),
            pl.BlockSpec((SEQ, D), lambda s: (s, 0)),
            pl.BlockSpec((SEQ, LANE), lambda s: (s, 0)),
            pl.BlockSpec((ncp, ng, M_CHUNK), lambda s: (s, 0, 0)),
        ],
        out_specs=(
            pl.BlockSpec((SEQ, D), lambda s: (s, 0)),
            pl.BlockSpec((1, 2, M_HEADS, hd, hd), lambda s: (s, 0, 0, 0, 0)),
            pl.BlockSpec((1, 2, M_HEADS, 1, hd), lambda s: (s, 0, 0, 0, 0)),
            pl.BlockSpec((1, 2, M_HEADS, 1, LANE), lambda s: (s, 0, 0, 0, 0)),
        ),
        scratch_shapes=[pltpu.VMEM((SEQ, D), F32)] + state_scratch,
        compiler_params=_cparams(),
        name="mlstm_scan_ctx",
    )(q, k, v, gcol, grow)
    ncl = DEC_SEQ // M_CHUNK
    pb = NP_TOK // DEC_SEQ
    h_all = pl.pallas_call(
        _mlstm_scan_lat_kernel,
        out_shape=jax.ShapeDtypeStruct((N_TOK, D), F32),
        input_output_aliases={8: 0},
        grid=(DEC_BATCH,),
        in_specs=[
            pl.BlockSpec((DEC_SEQ, D), lambda b: (pb + b, 0)),
            pl.BlockSpec((DEC_SEQ, D), lambda b: (pb + b, 0)),
            pl.BlockSpec((DEC_SEQ, D), lambda b: (pb + b, 0)),
            pl.BlockSpec((DEC_SEQ, LANE), lambda b: (pb + b, 0)),
            pl.BlockSpec((ncl, ng, M_CHUNK), lambda b: (pb + b, 0, 0)),
            pl.BlockSpec((1, 2, M_HEADS, hd, hd), lambda b: (b, 0, 0, 0, 0)),
            pl.BlockSpec((1, 2, M_HEADS, 1, hd), lambda b: (b, 0, 0, 0, 0)),
            pl.BlockSpec((1, 2, M_HEADS, 1, 1), lambda b: (b, 0, 0, 0, 0)),
            pl.BlockSpec(memory_space=pl.ANY),
        ],
        out_specs=pl.BlockSpec((DEC_SEQ, D), lambda b: (pb + b, 0)),
        scratch_shapes=[pltpu.VMEM((DEC_SEQ, D), F32)] + state_scratch,
        compiler_params=_cparams(),
        name="mlstm_scan_lat",
    )(q, k, v, gcol, grow, state_c, state_n, state_m, h_ctx)
    return h_all, new_c, new_n, new_m


def _mlstm_out_kernel(mr_ref, h_ref, o_ref, ng_ref, w_ref, y_ref, mod_ref, out_ref, x_scr):
    hc = o_ref[...] * h_ref[...]
    for hd in range(M_HEADS):
        sl = slice(hd * M_HEAD_DIM, (hd + 1) * M_HEAD_DIM)
        x_scr[:, sl] = _rms(hc[:, sl], ng_ref[:, sl]).astype(BF16)
    out = jnp.dot(x_scr[...], w_ref[...], preferred_element_type=F32)
    out_ref[...] = y_ref[...] + _mod_vec(mod_ref, 2) * out


def _mlstm_out(hsum, o, norm_g, w_out, y, mods, mrow):
    return pl.pallas_call(
        _mlstm_out_kernel,
        out_shape=jax.ShapeDtypeStruct((N_TOK, D), F32),
        grid_spec=pltpu.PrefetchScalarGridSpec(
            num_scalar_prefetch=1, grid=(NL,),
            in_specs=[_tok_spec(D, TL), _tok_spec(D, TL), _full_spec((1, D)), _full_spec((D, D)), _tok_spec(D, TL),
                      _mod_spec()],
            out_specs=_tok_spec(D, TL),
            scratch_shapes=[pltpu.VMEM((TL, D), BF16)]),
        compiler_params=_cparams(),
        name="mlstm_out",
    )(mrow, hsum, o, norm_g, w_out, y, mods)


ROUTE_OFF = N_GROUPS
SLAB = D // (2 * LANE)
V7X_SC_CORES = 2
V7X_SC_SUBCORES = 16
SC_WORKERS = V7X_SC_CORES * V7X_SC_SUBCORES
SC_WINDOW = 128
SC_HALF = SC_WINDOW // 2
HI_MASK = -65536


def _bf16_bits(x):
    return lax.bitcast_convert_type(x.astype(BF16).astype(F32), I32)


def _store_slabs(ref, x):
    rows = x.shape[0]
    for c in range(SLAB):
        lo = lax.shift_right_logical(_bf16_bits(x[:, (2 * c) * LANE:(2 * c + 1) * LANE]), 16)
        hi = _bf16_bits(x[:, (2 * c + 1) * LANE:(2 * c + 2) * LANE]) & HI_MASK
        ref[pl.ds(c, rows, stride=SLAB), :] = lo | hi


def _load_slabs(ref, dst, rows, dtype):
    for c in range(SLAB):
        w = ref[pl.ds(c, rows, stride=SLAB), :]
        lo = lax.bitcast_convert_type(lax.shift_left(w, 16), F32)
        hi = lax.bitcast_convert_type(w & HI_MASK, F32)
        dst[:, (2 * c) * LANE:(2 * c + 1) * LANE] = lo.astype(dtype)
        dst[:, (2 * c + 1) * LANE:(2 * c + 2) * LANE] = hi.astype(dtype)


def _route_kernel(mr_ref, y_ref, mod_ref, g_ref, wr_ref, br_ref, tri_ref, x_ref, wt_ref, meta_ref, cnt_ref, cnt_scr):
    x = _norm_mod(y_ref[...], g_ref[...], mod_ref, 1)
    _store_slabs(x_ref, x)
    lg = jnp.dot(x.astype(BF16), wr_ref[...], preferred_element_type=F32) + br_ref[...]
    lane = lax.broadcasted_iota(I32, lg.shape, 1).astype(F32)
    ninf = -jnp.inf
    big = float(LANE)
    lgg = jnp.where(lane < N_GROUPS, lg, ninf)
    gmax = jnp.max(lgg, axis=-1, keepdims=True)
    g_idx = jnp.min(jnp.where(lgg == gmax, lane, big), axis=-1, keepdims=True)
    g_w = 1.0 / jnp.sum(jnp.exp(lgg - gmax), axis=-1, keepdims=True)
    lo = ROUTE_OFF + g_idx * EXPERTS_PER_GROUP
    le = jnp.where((lane >= lo) & (lane < lo + EXPERTS_PER_GROUP), lg, ninf)
    m1 = jnp.max(le, axis=-1, keepdims=True)
    i1 = jnp.min(jnp.where(le == m1, lane, big), axis=-1, keepdims=True)
    le2 = jnp.where(lane == i1, ninf, le)
    m2 = jnp.max(le2, axis=-1, keepdims=True)
    i2 = jnp.min(jnp.where(le2 == m2, lane, big), axis=-1, keepdims=True)
    r = jnp.exp(m2 - m1)
    p1 = 1.0 / (1.0 + r)
    p2 = r / (1.0 + r)
    two = lax.broadcasted_iota(I32, (x.shape[0], TOP_K), 1)
    wt_ref[...] = jnp.where(two == 0, g_w * p1, g_w * p2)
    @pl.when(pl.program_id(0) == 0)
    def _():
        cnt_scr[...] = jnp.zeros(cnt_scr.shape, F32)

    oh1 = (lane == i1).astype(F32)
    oh2 = (lane == i2).astype(F32)
    both = oh1 + oh2
    before = jnp.dot(tri_ref[...], both.astype(BF16), preferred_element_type=F32) + cnt_scr[...]
    rk1 = jnp.sum(oh1 * before, axis=-1, keepdims=True)
    rk2 = jnp.sum(oh2 * before, axis=-1, keepdims=True)
    cnt_scr[...] = cnt_scr[...] + jnp.sum(both, axis=0, keepdims=True)
    cnt_ref[...] = cnt_scr[...]
    cols = (i1 - ROUTE_OFF, i2 - ROUTE_OFF, rk1, rk2)
    packed = jnp.zeros(lg.shape, F32)
    for c, val in enumerate(cols):
        packed = jnp.where(lane == c, val, packed)
    meta_ref[...] = jnp.transpose(packed)[0:len(cols), :].astype(I32)


def _route(y, mods, mrow, g2, w_route, b_route):
    return pl.pallas_call(
        _route_kernel,
        out_shape=(jax.ShapeDtypeStruct((N_TOK * SLAB, LANE), I32), jax.ShapeDtypeStruct((N_TOK, TOP_K), F32),
                   jax.ShapeDtypeStruct((2 * TOP_K, N_TOK), I32), jax.ShapeDtypeStruct((1, LANE), F32)),
        grid_spec=pltpu.PrefetchScalarGridSpec(
            num_scalar_prefetch=1, grid=(NL,),
            in_specs=[_tok_spec(D, TL), _mod_spec(), _full_spec((1, D)), _full_spec((D, LANE)),
                      _full_spec((1, LANE)), _full_spec((TL, TL))],
            out_specs=(pl.BlockSpec((TL * SLAB, LANE), lambda j, *_: (j, 0)), _tok_spec(TOP_K, TL),
                       pl.BlockSpec((2 * TOP_K, TL), lambda j, *_: (0, j)), _full_spec((1, LANE))),
            scratch_shapes=[pltpu.VMEM((1, LANE), F32)]),
        compiler_params=_cparams(),
        name="moe_route",
    )(mrow, y, mods, g2, w_route, b_route, jnp.asarray(np.tril(np.ones((TL, TL), np.float32), -1), dtype=BF16))


SLOT_COLS = 2048


def _slot_kernel(meta_ref, ps_ref, o_ref):
    sub = lax.broadcasted_iota(I32, (N_EXPERTS, SLOT_COLS), 0)
    meta = meta_ref[...]
    table = ps_ref[...]
    for k in range(TOP_K):
        start = jnp.sum(jnp.where(sub == meta[k:k + 1, :], table, 0), axis=0, keepdims=True)
        o_ref[k:k + 1, :] = start + meta[TOP_K + k:TOP_K + k + 1, :]


def _slots(meta, pad_start):
    return pl.pallas_call(
        _slot_kernel,
        out_shape=jax.ShapeDtypeStruct((TOP_K, N_TOK), I32),
        grid=(N_TOK // SLOT_COLS,),
        in_specs=[pl.BlockSpec((2 * TOP_K, SLOT_COLS), lambda j: (0, j)), pl.BlockSpec((N_EXPERTS, 1), lambda j: (0, 0))],
        out_specs=pl.BlockSpec((TOP_K, SLOT_COLS), lambda j: (0, j)),
        compiler_params=_cparams(),
        name="moe_slots",
    )(meta, pad_start.astype(I32).reshape(N_EXPERTS, 1))


def _dispatch_tables(meta, lane_counts):
    counts = lane_counts[0, ROUTE_OFF:ROUTE_OFF + N_EXPERTS].astype(I32)
    padded = ((counts + EBLK - 1) // EBLK) * EBLK
    pad_end = jnp.cumsum(padded)
    pad_start = pad_end - padded
    dest = _slots(meta, pad_start)
    n_blk = (padded // EBLK).astype(I32)
    blk_start = (pad_start // EBLK).astype(I32)
    n_used = (pad_end[-1] // EBLK).astype(I32).reshape(1)
    return dest, blk_start, n_blk, n_used


def _sc_mesh():
    return plsc.VectorSubcoreMesh(core_axis_name="core", subcore_axis_name="subcore",
                                  num_cores=V7X_SC_CORES, num_subcores=V7X_SC_SUBCORES)


def _sc_worker():
    return lax.axis_index("core") * V7X_SC_SUBCORES + lax.axis_index("subcore")


def _sc_dispatch(x_slabs, d0, d1):
    per = N_TOK // SC_WORKERS
    n_win = per // SC_HALF

    @functools.partial(
        pl.kernel, out_type=jax.ShapeDtypeStruct((P_SLOTS, SLAB, LANE), I32), mesh=_sc_mesh(), name="moe_dispatch",
        scratch_types=[pltpu.VMEM((1, per), I32), pltpu.VMEM((1, per), I32), pltpu.VMEM((2, SC_HALF, SLAB, LANE), I32),
                       pltpu.SemaphoreType.DMA((2,)), pltpu.SemaphoreType.DMA((2,))])
    def run(x_hbm, d0_hbm, d1_hbm, o_hbm, i0_v, i1_v, buf, lsem, ssem):
        base = _sc_worker() * per
        pltpu.sync_copy(d0_hbm.at[:, pl.ds(base, per)], i0_v)
        pltpu.sync_copy(d1_hbm.at[:, pl.ds(base, per)], i1_v)
        loads = [pltpu.make_async_copy(x_hbm.at[pl.ds(base + s * SC_HALF, SC_HALF)], buf.at[s % 2], lsem.at[s % 2])
                 for s in range(n_win)]
        loads[0].start()
        for s in range(n_win):
            loads[s].wait()
            if s + 1 < n_win:
                loads[s + 1].start()
            win = pl.ds(s * SC_HALF, SC_HALF)
            outs = [pltpu.make_async_copy(buf.at[s % 2], o_hbm.at[iv.at[0, win]], ssem.at[a])
                    for a, iv in enumerate((i0_v, i1_v))]
            for cp in outs:
                cp.start()
            for cp in outs:
                cp.wait()

    return run(x_slabs.reshape(N_TOK, SLAB, LANE), d0, d1)


def _sc_collect(y_slabs, dcat):
    per = N_ASSIGN // SC_WORKERS
    n_win = per // SC_HALF

    @functools.partial(
        pl.kernel, out_type=jax.ShapeDtypeStruct((N_ASSIGN, SLAB, LANE), I32), mesh=_sc_mesh(), name="moe_collect",
        scratch_types=[pltpu.VMEM((1, per), I32), pltpu.VMEM((2, SC_HALF, SLAB, LANE), I32),
                       pltpu.SemaphoreType.DMA((2,)), pltpu.SemaphoreType.DMA((2,))])
    def run(y_hbm, i_hbm, o_hbm, i_v, buf, gsem, wsem):
        base = _sc_worker() * per
        pltpu.sync_copy(i_hbm.at[:, pl.ds(base, per)], i_v)
        gathers = [pltpu.make_async_copy(y_hbm.at[i_v.at[0, pl.ds(s * SC_HALF, SC_HALF)]], buf.at[s % 2],
                                         gsem.at[s % 2]) for s in range(n_win)]
        writes = [pltpu.make_async_copy(buf.at[s % 2], o_hbm.at[pl.ds(base + s * SC_HALF, SC_HALF)], wsem.at[s % 2])
                  for s in range(n_win)]
        gathers[0].start()
        for s in range(n_win):
            gathers[s].wait()
            if s >= 1:
                writes[s - 1].wait()
            if s + 1 < n_win:
                gathers[s + 1].start()
            writes[s].start()
        writes[n_win - 1].wait()

    return run(y_slabs.reshape(P_SLOTS, SLAB, LANE), dcat)


EROWS = EBLK * SLAB


def _expert_kernel(bs_ref, nb_ref, nu_ref, wg_ref, wu_ref, wd_ref, x_hbm, y_hbm,
                   xbuf, ybuf, xs, wg_bf, wu_bf, wd_bf, isem, osem):
    e = pl.program_id(0)
    n_exp = pl.num_programs(0)
    n_used = nu_ref[0]
    b0 = bs_ref[e]
    nb = nb_ref[e]

    def in_copy(g, slot):
        return pltpu.make_async_copy(x_hbm.at[pl.ds(pl.multiple_of(g * EROWS, EROWS), EROWS)], xbuf.at[slot],
                                     isem.at[slot])

    def out_copy(g, slot):
        return pltpu.make_async_copy(ybuf.at[slot], y_hbm.at[pl.ds(pl.multiple_of(g * EROWS, EROWS), EROWS)],
                                     osem.at[slot])

    @pl.when(e == 0)
    def _():
        in_copy(0, 0).start()

    @pl.when(nb > 0)
    def _():
        wg_bf[...] = wg_ref[0, 0].astype(BF16)
        wu_bf[...] = wu_ref[0, 0].astype(BF16)
        wd_bf[...] = wd_ref[0, 0].astype(BF16)

    def block(k, carry):
        g = b0 + k
        slot = lax.rem(g, 2)
        in_copy(g, slot).wait()

        @pl.when(g + 1 < n_used)
        def _():
            in_copy(g + 1, 1 - slot).start()

        _load_slabs(xbuf.at[slot], xs, EBLK, BF16)
        xb = xs[...]
        gt = jnp.dot(xb, wg_bf[...], preferred_element_type=F32)
        up = jnp.dot(xb, wu_bf[...], preferred_element_type=F32)
        hmid = (gt * _sigmoid(gt) * up).astype(BF16)
        res = jnp.dot(hmid, wd_bf[...], preferred_element_type=F32)

        @pl.when(g >= 2)
        def _():
            out_copy(g - 2, slot).wait()

        _store_slabs(ybuf.at[slot], res)
        out_copy(g, slot).start()
        return carry

    lax.fori_loop(0, nb, block, 0)

    @pl.when(e == n_exp - 1)
    def _():
        last = n_used - 1
        out_copy(last, lax.rem(last, 2)).wait()

        @pl.when(n_used >= 2)
        def _():
            out_copy(last - 1, lax.rem(last - 1, 2)).wait()


def _experts(x_sorted, blk_start, n_blk, n_used, w_gate, w_up, w_down, layer):
    any_spec = pl.BlockSpec(memory_space=pl.ANY)
    wspec = lambda r, c: pl.BlockSpec((1, 1, r, c), lambda e, *_: (layer, e, 0, 0))
    return pl.pallas_call(
        _expert_kernel,
        out_shape=jax.ShapeDtypeStruct((P_SLOTS * SLAB, LANE), I32),
        grid_spec=pltpu.PrefetchScalarGridSpec(
            num_scalar_prefetch=3, grid=(N_EXPERTS,),
            in_specs=[wspec(D, D_EXPERT), wspec(D, D_EXPERT), wspec(D_EXPERT, D), any_spec],
            out_specs=any_spec,
            scratch_shapes=[
                pltpu.VMEM((2, EROWS, LANE), I32), pltpu.VMEM((2, EROWS, LANE), I32),
                pltpu.VMEM((EBLK, D), BF16),
                pltpu.VMEM((D, D_EXPERT), BF16), pltpu.VMEM((D, D_EXPERT), BF16), pltpu.VMEM((D_EXPERT, D), BF16),
                pltpu.SemaphoreType.DMA((2,)), pltpu.SemaphoreType.DMA((2,)),
            ]),
        compiler_params=_cparams(),
        name="moe_experts",
    )(blk_start, n_blk, n_used, w_gate, w_up, w_down, x_sorted.reshape(P_SLOTS * SLAB, LANE))


def _combine_kernel(final, mr_ref, e0_ref, e1_ref, wt_ref, y_ref, mod_ref, fg_ref, o_ref, a_scr, b_scr):
    _load_slabs(e0_ref, a_scr, TL, F32)
    _load_slabs(e1_ref, b_scr, TL, F32)
    wt = wt_ref[...]
    moe = wt[:, 0:1] * a_scr[...] + wt[:, 1:2] * b_scr[...]
    y_new = y_ref[...] + _mod_vec(mod_ref, 5) * moe
    o_ref[...] = _rms(y_new, fg_ref[...]) if final else y_new


def _combine(ym, wts, y, mods, mrow, final_g, blk0, nblk, final):
    tok = lambda width: pl.BlockSpec((TL, width), lambda j, *_: (blk0 + j, 0))
    slab0 = pl.BlockSpec((TL * SLAB, LANE), lambda j, *_: (blk0 + j, 0))
    slab1 = pl.BlockSpec((TL * SLAB, LANE), lambda j, *_: (NL + blk0 + j, 0))
    mod = pl.BlockSpec((1, 1, 6 * D), lambda j, mr: (mr[blk0 + j], 0, 0))
    return pl.pallas_call(
        functools.partial(_combine_kernel, final),
        out_shape=jax.ShapeDtypeStruct((nblk * TL, D), F32),
        grid_spec=pltpu.PrefetchScalarGridSpec(
            num_scalar_prefetch=1, grid=(nblk,),
            in_specs=[slab0, slab1, tok(TOP_K), tok(D), mod, _full_spec((1, D))],
            out_specs=pl.BlockSpec((TL, D), lambda j, *_: (j, 0)),
            scratch_shapes=[pltpu.VMEM((TL, D), F32), pltpu.VMEM((TL, D), F32)]),
        compiler_params=_cparams(),
        name="moe_combine",
    )(mrow, ym, ym, wts, y, mods, final_g)


def kernel(x_prompt, x_sample, cache_attn_k, cache_attn_v, state_mlstm_C, state_mlstm_n, state_mlstm_m, c, c_ctx, ada_w, ada_b, norm1_g, norm2_g, conv_w_in, conv_w_dw, conv_b_dw, conv_ln_g, conv_ln_b, conv_w_out, attn_w_qkv, attn_q_norm, attn_k_norm, attn_w_o, mlstm_w_in, mlstm_b_gate, mlstm_norm_g, mlstm_w_out, moe_w_group, moe_b_group, moe_w_router, moe_b_router, moe_w_gate, moe_w_up, moe_w_down, final_norm_g):
    y = None
    cvec = jnp.concatenate([c_ctx[None, :], c, jnp.zeros((MOD_ROWS - 1 - DEC_BATCH, D), F32)], axis=0)
    rope = _rope_blocks()
    mrow, mrow_sb, mrow_l = jnp.asarray(_MOD_ROW), jnp.asarray(_MOD_ROW_SB), jnp.asarray(_MOD_ROW_L)
    new_k = new_v = new_c = new_n = new_m = None
    for i in range(DEPTH):
        kind, slot = i % 3, i // 3
        mods = _ada_layer(cvec, ada_w, ada_b, i)
        g1 = norm1_g[i].reshape(1, D)
        if kind == 0:
            src = (x_prompt.reshape(NP_TOK, D), x_sample.reshape(NS_TOK, D), True) if i == 0 else (y, y, False)
            u = _conv_in(*src, mods, mrow_l, g1, conv_w_in, slot)
            w_dw = jnp.concatenate([conv_w_dw[slot], jnp.zeros((1, D), F32)], axis=0)
            y = _conv_main(u, *src, mods, mrow_sb, w_dw, conv_b_dw[slot].reshape(1, D), conv_ln_g[slot].reshape(1, D),
                           conv_ln_b[slot].reshape(1, D), conv_w_out, slot)
        elif kind == 1:
            q, kb, vb, kf, vf = _attn_qkv(y, mods, mrow_sb, g1, attn_w_qkv[slot].astype(BF16),
                                          attn_q_norm[slot].reshape(1, HEAD_DIM), attn_k_norm[slot].reshape(1, HEAD_DIM),
                                          rope)
            new_k = kf[:NP_TOK].reshape(BATCH, 1, SEQ, N_KV_HEADS, HEAD_DIM)
            new_v = vf[:NP_TOK].reshape(BATCH, 1, SEQ, N_KV_HEADS, HEAD_DIM)
            ck = cache_attn_k[:, slot].reshape(DEC_BATCH, PAST_LEN, KV_DIM)
            cv = cache_attn_v[:, slot].reshape(DEC_BATCH, PAST_LEN, KV_DIM)
            y = _attention(q, kb, vb, ck, cv, attn_w_o[slot].astype(BF16), y, mods)
        else:
            b_gate = jnp.concatenate([mlstm_b_gate[slot], jnp.zeros((LANE - 4 * M_HEADS,), F32)]).reshape(1, LANE)
            q, k, v, o, gates, grow = _mlstm_in(y, mods, mrow, g1, mlstm_w_in, slot, b_gate)
            sc = state_mlstm_C[:, slot]
            sn = state_mlstm_n[:, slot].reshape(DEC_BATCH, 2, M_HEADS, 1, M_HEAD_DIM)
            sm = state_mlstm_m[:, slot].reshape(DEC_BATCH, 2, M_HEADS, 1, 1)
            hsum, nc_, nn_, nm_ = _mlstm_scan(q, k, v, gates, grow, sc, sn, sm)
            new_c = nc_[:, None]
            new_n = nn_.reshape(BATCH, 1, 2, M_HEADS, M_HEAD_DIM)
            new_m = nm_[..., 0, 0].reshape(BATCH, 1, 2, M_HEADS)
            y = _mlstm_out(hsum, o, mlstm_norm_g[slot].reshape(1, D), mlstm_w_out[slot].astype(BF16), y, mods, mrow_l)
        w_route = jnp.concatenate([moe_w_group[i], moe_w_router[i],
                                   jnp.zeros((D, LANE - N_GROUPS - N_EXPERTS), F32)], axis=1)
        b_route = jnp.concatenate([moe_b_group[i], moe_b_router[i],
                                   jnp.zeros((LANE - N_GROUPS - N_EXPERTS,), F32)]).reshape(1, LANE)
        x2, ewt, meta, cnt = _route(y, mods, mrow_l, norm2_g[i].reshape(1, D), w_route.astype(BF16), b_route)
        dest, blk_start, n_blk, n_used = _dispatch_tables(meta, cnt)
        x_sorted = _sc_dispatch(x2, dest[0:1], dest[1:2])
        y_sorted = _experts(x_sorted, blk_start, n_blk, n_used, moe_w_gate, moe_w_up, moe_w_down, i)
        ym = _sc_collect(y_sorted, dest.reshape(1, N_ASSIGN))
        ym = ym.reshape(N_ASSIGN * SLAB, LANE)
        fg = final_norm_g.reshape(1, D)
        if i + 1 < DEPTH:
            y = _combine(ym, ewt, y, mods, mrow_l, fg, 0, NL, False)
        else:
            y_prompt = _combine(ym, ewt, y, mods, mrow_l, fg, 0, NLP, True).reshape(BATCH, SEQ, D)
            y_sample = _combine(ym, ewt, y, mods, mrow_l, fg, NLP, NL - NLP, True).reshape(DEC_BATCH, DEC_SEQ, D)
    return (y_prompt, y_sample, new_k, new_v, new_c, new_n, new_m)
```

```python
import functools

import jax
import jax.numpy as jnp
import numpy as np
from jax import lax
from jax.experimental import pallas as pl
from jax.experimental.pallas import tpu as pltpu
from jax.experimental.pallas import tpu_sc as plsc

F32 = jnp.float32
BF16 = jnp.bfloat16
I32 = jnp.int32

D = 1024
BATCH, SEQ = 16, 256
DEC_BATCH, DEC_SEQ = 8, 1024
PAST_LEN = 256
DEPTH = 4
GRID_W = 64
EPS = 1e-6
CONV_WIDTH = 31
CONV_PAD = CONV_WIDTH // 2
HEAD_DIM = 128
N_HEADS = 8
N_KV_HEADS = 2
GQA_GROUP = N_HEADS // N_KV_HEADS
Q_DIM = N_HEADS * HEAD_DIM
KV_DIM = N_KV_HEADS * HEAD_DIM
QKV_DIM = Q_DIM + 2 * KV_DIM
ROPE_THETA = 10000.0
M_HEADS = 4
M_HEAD_DIM = D // M_HEADS
M_CHUNK = 64
N_GROUPS = 4
EXPERTS_PER_GROUP = 8
N_EXPERTS = N_GROUPS * EXPERTS_PER_GROUP
TOP_K = 2
D_EXPERT = 512

NP_TOK = BATCH * SEQ
NS_TOK = DEC_BATCH * DEC_SEQ
N_TOK = NP_TOK + NS_TOK
TM = 512
NB = N_TOK // TM
NBP = NP_TOK // TM
BLK_PER_DEC = DEC_SEQ // TM
TL = 1024
NL = N_TOK // TL
NLP = NP_TOK // TL
SB = 256
NSB = N_TOK // SB
NSBP = NP_TOK // SB
SB_PER_DEC = DEC_SEQ // SB
MOD_ROWS = 16
HALO = 16
LANE = 128
SUBLANE = 8

N_ASSIGN = N_TOK * TOP_K
EBLK = 256
N_EBLK = N_ASSIGN // EBLK + N_EXPERTS
P_SLOTS = N_EBLK * EBLK
N_PAD_SLOTS = P_SLOTS - N_ASSIGN

VMEM_LIMIT = 56 * 1024 * 1024


def _block_tables(nb, nbp, per_dec):
    j = np.arange(nb)
    is_p = j < nbp
    mod_row = np.where(is_p, 0, 1 + (j - nbp) // per_dec)
    rope_idx = np.where(is_p, 0, 1 + (j - nbp) % per_dec)
    first = np.where(is_p, 1, ((j - nbp) % per_dec == 0).astype(np.int64))
    last = np.where(is_p, 1, ((j - nbp) % per_dec == per_dec - 1).astype(np.int64))
    return (mod_row.astype(np.int32), rope_idx.astype(np.int32), first.astype(np.int32), last.astype(np.int32))


_MOD_ROW, _, _, _ = _block_tables(NB, NBP, BLK_PER_DEC)
_MOD_ROW_L, _, _, _ = _block_tables(NL, NLP, DEC_SEQ // TL)
_MOD_ROW_SB, _ROPE_IDX_SB, _SEQ_FIRST, _SEQ_LAST = _block_tables(NSB, NSBP, SB_PER_DEC)


def _cparams(n_axes=1):
    return pltpu.CompilerParams(dimension_semantics=("arbitrary",) * n_axes, vmem_limit_bytes=VMEM_LIMIT)


def _sigmoid(x):
    return 1.0 / (1.0 + jnp.exp(-x))


def _rms(x, g):
    return x * lax.rsqrt(jnp.mean(x * x, axis=-1, keepdims=True) + EPS) * g


def _mod_vec(mod_ref, k):
    return mod_ref[0, :, k * D:(k + 1) * D]


def _norm_mod(y, g, mod_ref, which):
    return _rms(y, g) * (1.0 + _mod_vec(mod_ref, 3 * which + 1)) + _mod_vec(mod_ref, 3 * which)


def _ada_kernel(c_ref, w_ref, b_ref, o_ref):
    c = c_ref[...]
    s = c * _sigmoid(c)
    res = jnp.dot(s.astype(BF16), w_ref[0].astype(BF16), preferred_element_type=F32) + b_ref[0]
    for r in range(MOD_ROWS):
        o_ref[r] = res[r:r + 1, :]


def _ada_layer(cvec, ada_w, ada_b, layer):
    tn = 1536
    return pl.pallas_call(
        _ada_kernel,
        out_shape=jax.ShapeDtypeStruct((MOD_ROWS, 1, 6 * D), F32),
        grid=(6 * D // tn,),
        in_specs=[
            pl.BlockSpec((MOD_ROWS, D), lambda n: (0, 0)),
            pl.BlockSpec((1, D, tn), lambda n: (layer, 0, n)),
            pl.BlockSpec((1, 1, tn), lambda n: (layer, 0, n)),
        ],
        out_specs=pl.BlockSpec((MOD_ROWS, 1, tn), lambda n: (0, 0, n)),
        compiler_params=_cparams(1),
        name="ada_mod",
    )(cvec, ada_w, ada_b.reshape(DEPTH, 1, 6 * D))


def _tok_spec(width, rows=TM):
    return pl.BlockSpec((rows, width), lambda j, *_: (j, 0))


def _mod_spec():
    return pl.BlockSpec((1, 1, 6 * D), lambda j, mr, *_: (mr[j], 0, 0))


def _full_spec(shape):
    nd = len(shape)
    return pl.BlockSpec(shape, lambda j, *_: (0,) * nd)


def _pair_specs(rows, nbp, split):
    s_off = nbp if split else 0
    return [pl.BlockSpec((rows, D), lambda j, *_: (jnp.minimum(j, nbp - 1), 0)),
            pl.BlockSpec((rows, D), lambda j, *_: (jnp.maximum(j, nbp) - s_off, 0))]


def _pair_block(nbp, yp_ref, ys_ref):
    return jnp.where(pl.program_id(0) < nbp, yp_ref[...], ys_ref[...])


def _resident_weight_spec(w_all, slot):
    return pl.BlockSpec((1,) + w_all.shape[1:], lambda j, *_: (slot, 0, 0), pipeline_mode=pl.Buffered(1))


def _conv_in_kernel(mr_ref, yp_ref, ys_ref, mod_ref, g_ref, wf_ref, u_ref, w_ref):
    @pl.when(pl.program_id(0) == 0)
    def _():
        w_ref[...] = wf_ref[0].astype(BF16)

    h = _norm_mod(_pair_block(NLP, yp_ref, ys_ref), g_ref[...], mod_ref, 0)
    ag = jnp.dot(h.astype(BF16), w_ref[...], preferred_element_type=F32)
    u_ref[...] = ag[:, :D] * _sigmoid(ag[:, D:])


def _conv_in(yp, ys, split, mods, mrow, g1, w_in_all, slot):
    return pl.pallas_call(
        _conv_in_kernel,
        out_shape=jax.ShapeDtypeStruct((N_TOK, D), F32),
        grid_spec=pltpu.PrefetchScalarGridSpec(
            num_scalar_prefetch=1, grid=(NL,),
            in_specs=_pair_specs(TL, NLP, split) + [_mod_spec(), _full_spec((1, D)),
                                                    _resident_weight_spec(w_in_all, slot)],
            out_specs=_tok_spec(D, TL),
            scratch_shapes=[pltpu.VMEM((D, 2 * D), BF16)]),
        compiler_params=_cparams(),
        name="conv_in",
    )(mrow, yp, ys, mods, g1, w_in_all)


def _conv_main_kernel(mr_ref, first_ref, last_ref, u_ref, up_ref, un_ref, wdw_ref, bdw_ref, lg_ref, lb_ref,
                      woutf_ref, yp_ref, ys_ref, mod_ref, o_ref, ext_ref, acc_ref, wout_ref):
    j = pl.program_id(0)

    @pl.when(j == 0)
    def _():
        wout_ref[...] = woutf_ref[0].astype(BF16)

    zero = jnp.zeros((HALO, D), F32)
    ext_ref[0:HALO, :] = jnp.where(first_ref[j] == 1, zero, up_ref[...])
    ext_ref[HALO:HALO + SB, :] = u_ref[...]
    ext_ref[HALO + SB:2 * HALO + SB, :] = jnp.where(last_ref[j] == 1, zero, un_ref[...])

    off0 = HALO - CONV_PAD
    n_a = (off0 + CONV_WIDTH - 1) // SUBLANE + 1
    n_chunks = SB // SUBLANE

    def strip(ci, carry):
        cs = pl.ds(pl.multiple_of(ci * LANE, LANE), LANE)
        wk = [jnp.broadcast_to(wdw_ref[k:k + 1, cs], (SUBLANE, LANE)) for k in range(CONV_WIDTH)]
        bias = jnp.broadcast_to(bdw_ref[:, cs], (SUBLANE, LANE))
        sub = lax.broadcasted_iota(I32, (SUBLANE, LANE), 0)
        prev_rot, prev_v0 = None, None
        for j in range(n_chunks + 1):
            tiles = [ext_ref[SUBLANE * (j + a):SUBLANE * (j + a + 1), cs] for a in range(n_a)]
            part = []
            for s in range(SUBLANE):
                acc = None
                for a in range(n_a):
                    k = SUBLANE * a + s - off0
                    if (0 <= k < CONV_WIDTH) and not (s == 0 and j == n_chunks):
                        term = tiles[a] * wk[k]
                        acc = term if acc is None else acc + term
                part.append(acc)
            rot = [None] + [pltpu.roll(part[s], SUBLANE - s, 0) for s in range(1, SUBLANE)]
            if j >= 1:
                out = prev_v0 + bias
                for s in range(1, SUBLANE):
                    out = out + jnp.where(sub < SUBLANE - s, prev_rot[s], rot[s])
                acc_ref[SUBLANE * (j - 1):SUBLANE * j, cs] = out
            prev_rot, prev_v0 = rot, part[0]
        return carry

    lax.fori_loop(0, D // LANE, strip, 0)

    c = acc_ref[...]
    mu = jnp.mean(c, axis=-1, keepdims=True)
    cc = c - mu
    var = jnp.mean(cc * cc, axis=-1, keepdims=True)
    z = cc * lax.rsqrt(var + EPS) * lg_ref[...] + lb_ref[...]
    z = z * _sigmoid(z)
    out = jnp.dot(z.astype(BF16), wout_ref[...], preferred_element_type=F32)
    o_ref[...] = _pair_block(NSBP, yp_ref, ys_ref) + _mod_vec(mod_ref, 2) * out


def _conv_main(u, yp, ys, split, mods, mrow, w_dw, b_dw, ln_g, ln_b, w_out_all, slot):
    nh = N_TOK // HALO
    per = SB // HALO
    sb_spec = pl.BlockSpec((SB, D), lambda j, *_: (j, 0))
    return pl.pallas_call(
        _conv_main_kernel,
        out_shape=jax.ShapeDtypeStruct((N_TOK, D), F32),
        grid_spec=pltpu.PrefetchScalarGridSpec(
            num_scalar_prefetch=3, grid=(NSB,),
            in_specs=[
                sb_spec,
                pl.BlockSpec((HALO, D), lambda j, *_: (jnp.maximum(j * per - 1, 0), 0)),
                pl.BlockSpec((HALO, D), lambda j, *_: (jnp.minimum((j + 1) * per, nh - 1), 0)),
                _full_spec((CONV_WIDTH + 1, D)), _full_spec((1, D)), _full_spec((1, D)), _full_spec((1, D)),
                _resident_weight_spec(w_out_all, slot), *_pair_specs(SB, NSBP, split), _mod_spec(),
            ],
            out_specs=sb_spec,
            scratch_shapes=[pltpu.VMEM((SB + 2 * HALO, D), F32), pltpu.VMEM((SB, D), F32), pltpu.VMEM((D, D), BF16)]),
        compiler_params=_cparams(),
        name="conv_main",
    )(mrow, jnp.asarray(_SEQ_FIRST), jnp.asarray(_SEQ_LAST), u, u, u, w_dw, b_dw, ln_g, ln_b, w_out_all, yp, ys, mods)


def _rope_angles():
    rows = DEC_SEQ // GRID_W
    row = jnp.repeat(jnp.arange(rows, dtype=F32), GRID_W)
    col = jnp.tile(jnp.arange(GRID_W, dtype=F32), rows)
    axis_dim = HEAD_DIM // 2
    freqs = jnp.power(ROPE_THETA, -jnp.arange(axis_dim // 2, dtype=F32) * 2.0 / axis_dim)
    ang_r = row[:, None] * freqs[None, :]
    ang_c = col[:, None] * freqs[None, :]
    return jnp.concatenate([ang_r, ang_r, ang_c, ang_c], axis=-1)


def _rope_blocks():
    ang = _rope_angles()
    cos, sin = jnp.cos(ang), jnp.sin(ang)
    lane = np.arange(HEAD_DIM)
    lo = jnp.asarray(((lane % (HEAD_DIM // 2)) < HEAD_DIM // 4).astype(np.float32))
    sin_a = -sin * lo[None, :]
    sin_b = sin * (1.0 - lo)[None, :]
    nblk = DEC_SEQ // SB
    ident = jnp.ones((1, SB, HEAD_DIM), F32)
    zeros = jnp.zeros((1, SB, HEAD_DIM), F32)
    cos_t = jnp.concatenate([ident, cos.reshape(nblk, SB, HEAD_DIM)], axis=0)
    sa_t = jnp.concatenate([zeros, sin_a.reshape(nblk, SB, HEAD_DIM)], axis=0)
    sb_t = jnp.concatenate([zeros, sin_b.reshape(nblk, SB, HEAD_DIM)], axis=0)
    return cos_t, sa_t, sb_t


def _attn_qkv_kernel(mr_ref, ri_ref, y_ref, mod_ref, g_ref, w_ref, qg_ref, kg_ref, cos_ref, sa_ref, sb_ref,
                     q_ref, kb_ref, vb_ref, kf_ref, vf_ref):
    h = _norm_mod(y_ref[...], g_ref[...], mod_ref, 0)
    qkv = jnp.dot(h.astype(BF16), w_ref[...], preferred_element_type=F32)
    cos, sa, sb = cos_ref[0], sa_ref[0], sb_ref[0]
    quarter = HEAD_DIM // 4

    def head(x, g):
        xn = _rms(x, g)
        return xn * cos + pltpu.roll(xn, HEAD_DIM - quarter, 1) * sa + pltpu.roll(xn, quarter, 1) * sb

    scale = HEAD_DIM ** -0.5
    for hd in range(N_HEADS):
        sl = slice(hd * HEAD_DIM, (hd + 1) * HEAD_DIM)
        q_ref[:, sl] = (head(qkv[:, sl], qg_ref[...]) * scale).astype(BF16)
    for kv in range(N_KV_HEADS):
        sl = slice(kv * HEAD_DIM, (kv + 1) * HEAD_DIM)
        kr = head(qkv[:, Q_DIM + kv * HEAD_DIM:Q_DIM + (kv + 1) * HEAD_DIM], kg_ref[...])
        kf_ref[:, sl] = kr
        kb_ref[:, sl] = kr.astype(BF16)
    v = qkv[:, Q_DIM + KV_DIM:]
    vf_ref[...] = v
    vb_ref[...] = v.astype(BF16)


def _attn_qkv(y, mods, mrow, g1, w_qkv, q_g, k_g, rope):
    cos_t, sa_t, sb_t = rope
    rspec = pl.BlockSpec((1, SB, HEAD_DIM), lambda j, mr, ri: (ri[j], 0, 0))
    return pl.pallas_call(
        _attn_qkv_kernel,
        out_shape=(jax.ShapeDtypeStruct((N_TOK, Q_DIM), BF16), jax.ShapeDtypeStruct((N_TOK, KV_DIM), BF16),
                   jax.ShapeDtypeStruct((N_TOK, KV_DIM), BF16), jax.ShapeDtypeStruct((N_TOK, KV_DIM), F32),
                   jax.ShapeDtypeStruct((N_TOK, KV_DIM), F32)),
        grid_spec=pltpu.PrefetchScalarGridSpec(
            num_scalar_prefetch=2, grid=(NSB,),
            in_specs=[_tok_spec(D, SB), _mod_spec(), _full_spec((1, D)), _full_spec((D, QKV_DIM)),
                      _full_spec((1, HEAD_DIM)), _full_spec((1, HEAD_DIM)), rspec, rspec, rspec],
            out_specs=(_tok_spec(Q_DIM, SB), _tok_spec(KV_DIM, SB), _tok_spec(KV_DIM, SB), _tok_spec(KV_DIM, SB),
                       _tok_spec(KV_DIM, SB))),
        compiler_params=_cparams(),
        name="attn_qkv",
    )(mrow, jnp.asarray(_ROPE_IDX_SB), y, mods, g1, w_qkv, q_g, k_g, cos_t, sa_t, sb_t)


def _attn_heads(q, ks, vs, o_scr):
    nt = (((1,), (1,)), ((), ()))
    for hd in range(N_HEADS):
        g = hd // GQA_GROUP
        qh = q[:, hd * HEAD_DIM:(hd + 1) * HEAD_DIM]
        gs = slice(g * HEAD_DIM, (g + 1) * HEAD_DIM)
        ss = [lax.dot_general(qh, k[:, gs], nt, preferred_element_type=F32) for k in ks]
        m = functools.reduce(jnp.maximum, [jnp.max(s, axis=-1, keepdims=True) for s in ss])
        ps = [jnp.exp(s - m) for s in ss]
        l = functools.reduce(lambda a, b: a + b, [jnp.sum(p, axis=-1, keepdims=True) for p in ps])
        o = functools.reduce(lambda a, b: a + b,
                             [jnp.dot(p.astype(BF16), v[:, gs], preferred_element_type=F32) for p, v in zip(ps, vs)])
        o_scr[:, hd * HEAD_DIM:(hd + 1) * HEAD_DIM] = (o / l).astype(BF16)


def _attn_ctx_kernel(q_ref, k_ref, v_ref, wo_ref, y_ref, mod_ref, o_ref, o_scr):
    _attn_heads(q_ref[...], [k_ref[...]], [v_ref[...]], o_scr)
    out = jnp.dot(o_scr[...], wo_ref[...], preferred_element_type=F32)
    o_ref[...] = y_ref[...] + _mod_vec(mod_ref, 2) * out


def _attn_lat_kernel(q_ref, k_ref, v_ref, ck_ref, cv_ref, wo_ref, y_ref, mod_ref, ctx_out_ref, o_ref, o_scr):
    del ctx_out_ref
    _attn_heads(q_ref[...], [k_ref[...], ck_ref[0].astype(BF16)], [v_ref[...], cv_ref[0].astype(BF16)], o_scr)
    out = jnp.dot(o_scr[...], wo_ref[...], preferred_element_type=F32)
    o_ref[...] = y_ref[...] + _mod_vec(mod_ref, 2) * out


def _attention(q, kb, vb, cache_k, cache_v, w_o, y, mods):
    y_ctx = pl.pallas_call(
        _attn_ctx_kernel,
        out_shape=jax.ShapeDtypeStruct((N_TOK, D), F32),
        grid=(BATCH,),
        in_specs=[
            pl.BlockSpec((SEQ, Q_DIM), lambda s: (s, 0)),
            pl.BlockSpec((SEQ, KV_DIM), lambda s: (s, 0)),
            pl.BlockSpec((SEQ, KV_DIM), lambda s: (s, 0)),
            pl.BlockSpec((Q_DIM, D), lambda s: (0, 0)),
            pl.BlockSpec((SEQ, D), lambda s: (s, 0)),
            pl.BlockSpec((1, 1, 6 * D), lambda s: (0, 0, 0)),
        ],
        out_specs=pl.BlockSpec((SEQ, D), lambda s: (s, 0)),
        scratch_shapes=[pltpu.VMEM((SEQ, Q_DIM), BF16)],
        compiler_params=_cparams(),
        name="attn_ctx",
    )(q, kb, vb, w_o, y, mods)
    pb = NP_TOK // DEC_SEQ
    return pl.pallas_call(
        _attn_lat_kernel,
        out_shape=jax.ShapeDtypeStruct((N_TOK, D), F32),
        input_output_aliases={8: 0},
        grid=(DEC_BATCH, SB_PER_DEC),
        in_specs=[
            pl.BlockSpec((SB, Q_DIM), lambda b, t: (NSBP + b * SB_PER_DEC + t, 0)),
            pl.BlockSpec((DEC_SEQ, KV_DIM), lambda b, t: (pb + b, 0)),
            pl.BlockSpec((DEC_SEQ, KV_DIM), lambda b, t: (pb + b, 0)),
            pl.BlockSpec((1, PAST_LEN, KV_DIM), lambda b, t: (b, 0, 0)),
            pl.BlockSpec((1, PAST_LEN, KV_DIM), lambda b, t: (b, 0, 0)),
            pl.BlockSpec((Q_DIM, D), lambda b, t: (0, 0)),
            pl.BlockSpec((SB, D), lambda b, t: (NSBP + b * SB_PER_DEC + t, 0)),
            pl.BlockSpec((1, 1, 6 * D), lambda b, t: (1 + b, 0, 0)),
            pl.BlockSpec(memory_space=pl.ANY),
        ],
        out_specs=pl.BlockSpec((SB, D), lambda b, t: (NSBP + b * SB_PER_DEC + t, 0)),
        scratch_shapes=[pltpu.VMEM((SB, Q_DIM), BF16)],
        compiler_params=_cparams(2),
        name="attn_lat",
    )(q, kb, vb, cache_k, cache_v, w_o, y, mods, y_ctx)


def _log_sigmoid(x):
    return jnp.minimum(x, 0.0) - jnp.log(1.0 + jnp.exp(-jnp.abs(x)))


W_T = 256


def _mlstm_in_kernel(mr_ref, y_ref, mod_ref, g_ref, wt_ref, bg_ref, q_ref, k_ref, v_ref, o_ref, gt_ref, gr_ref,
                     w_ref, wg_ref):
    @pl.when(pl.program_id(0) == 0)
    def _():
        for r in range(0, 4 * D, W_T):
            w_ref[:, r:r + W_T] = jnp.transpose(wt_ref[0, r:r + W_T, :]).astype(BF16)
        n_out = 4 * D + 4 * M_HEADS
        tail = jnp.transpose(wt_ref[0, n_out - LANE:n_out, :])
        lane = lax.broadcasted_iota(I32, tail.shape, 1)
        wg_ref[...] = jnp.where(lane < 4 * M_HEADS, pltpu.roll(tail, 4 * M_HEADS, axis=1), 0.0).astype(BF16)

    h = _norm_mod(y_ref[...], g_ref[...], mod_ref, 0)
    hb = h.astype(BF16)
    q_ref[...] = jnp.dot(hb, w_ref[:, 0:D], preferred_element_type=F32).astype(BF16)
    k_ref[...] = (jnp.dot(hb, w_ref[:, D:2 * D], preferred_element_type=F32) * (M_HEAD_DIM ** -0.5)).astype(BF16)
    v_ref[...] = jnp.dot(hb, w_ref[:, 2 * D:3 * D], preferred_element_type=F32).astype(BF16)
    o_ref[...] = _sigmoid(jnp.dot(hb, w_ref[:, 3 * D:4 * D], preferred_element_type=F32))
    gt = jnp.dot(hb, wg_ref[...], preferred_element_type=F32) + bg_ref[...]
    lane = lax.broadcasted_iota(I32, gt.shape, 1)
    is_f = ((lane >= M_HEADS) & (lane < 2 * M_HEADS)) | ((lane >= 3 * M_HEADS) & (lane < 4 * M_HEADS))
    gt = jnp.where(is_f, _log_sigmoid(gt), gt)
    gt_ref[...] = gt
    per = LANE // M_CHUNK
    for p in range(TM // LANE):
        t = jnp.transpose(gt[p * LANE:(p + 1) * LANE, :])
        for f in range(per):
            gr_ref[p * per + f] = t[0:4 * M_HEADS, f * M_CHUNK:(f + 1) * M_CHUNK]


def _mlstm_in(y, mods, mrow, g1, w_in_all, slot, b_gate):
    w_t = jnp.swapaxes(w_in_all, 1, 2)
    w_spec = pl.BlockSpec((1,) + w_t.shape[1:], lambda j, *_: (slot, 0, 0), pipeline_mode=pl.Buffered(1))
    return pl.pallas_call(
        _mlstm_in_kernel,
        out_shape=(jax.ShapeDtypeStruct((N_TOK, D), BF16), jax.ShapeDtypeStruct((N_TOK, D), BF16),
                   jax.ShapeDtypeStruct((N_TOK, D), BF16), jax.ShapeDtypeStruct((N_TOK, D), F32),
                   jax.ShapeDtypeStruct((N_TOK, LANE), F32),
                   jax.ShapeDtypeStruct((N_TOK // M_CHUNK, 4 * M_HEADS, M_CHUNK), F32)),
        grid_spec=pltpu.PrefetchScalarGridSpec(
            num_scalar_prefetch=1, grid=(NB,),
            in_specs=[_tok_spec(D), _mod_spec(), _full_spec((1, D)), w_spec, _full_spec((1, LANE))],
            out_specs=(_tok_spec(D), _tok_spec(D), _tok_spec(D), _tok_spec(D), _tok_spec(LANE),
                       pl.BlockSpec((TM // M_CHUNK, 4 * M_HEADS, M_CHUNK), lambda j, *_: (j, 0, 0))),
            scratch_shapes=[pltpu.VMEM((D, 4 * D), BF16), pltpu.VMEM((D, LANE), BF16)]),
        compiler_params=_cparams(),
        name="mlstm_in",
    )(mrow, y, mods, g1, w_t, b_gate)


def _mlstm_load(hd, c, q_ref, k_ref, v_ref, gc_ref, gr_ref):
    r0 = pl.multiple_of(c * M_CHUNK, M_CHUNK)
    hs = slice(hd * M_HEAD_DIM, (hd + 1) * M_HEAD_DIM)
    rows = pl.ds(r0, M_CHUNK)
    return rows, hs, q_ref[rows, hs], k_ref[rows, hs], v_ref[rows, hs], gc_ref[rows, :], gr_ref[c]


def _mlstm_chunks(chains, ms, loaded, c_scr, n_scr):
    L = M_CHUNK
    n = range(len(chains))
    t_idx = lax.broadcasted_iota(I32, (L, L), 0)
    s_idx = lax.broadcasted_iota(I32, (L, L), 1)
    masks = {0: (s_idx <= t_idx, t_idx <= s_idx), 1: (s_idx >= t_idx, t_idx >= s_idx)}
    q = [ld[2] for ld in loaded]
    k = [ld[3] for ld in loaded]
    v = [ld[4] for ld in loaded]
    gi = [2 * d * M_HEADS + hd for hd, d in chains]
    gf = [(2 * d + 1) * M_HEADS + hd for hd, d in chains]
    i_col = [ld[5][:, gi[i]:gi[i] + 1] for i, ld in enumerate(loaded)]
    lf_col = [ld[5][:, gf[i]:gf[i] + 1] for i, ld in enumerate(loaded)]
    i_row = [ld[6][gi[i]:gi[i] + 1, :] for i, ld in enumerate(loaded)]
    lf_row = [ld[6][gf[i]:gf[i] + 1, :] for i, ld in enumerate(loaded)]
    mask = [masks[d][0] for _, d in chains]
    mask_t = [masks[d][1] for _, d in chains]
    b_col = [jnp.sum(jnp.where(mask[i], lf_row[i], 0.0), axis=1, keepdims=True) for i in n]
    b_row = [jnp.sum(jnp.where(mask_t[i], lf_col[i], 0.0), axis=0, keepdims=True) for i in n]
    log_d = [jnp.where(mask[i], b_col[i] - b_row[i] + i_row[i], -jnp.inf) for i in n]
    li = [b_col[i] + ms[i] for i in n]
    m_r = [jnp.maximum(li[i], jnp.max(log_d[i], axis=1, keepdims=True)) for i in n]
    a_int = [jnp.exp(li[i] - m_r[i]) for i in n]
    dmat = [jnp.exp(log_d[i] - m_r[i]) for i in n]
    cmat = [c_scr[d, hd] for hd, d in chains]
    nvec = [n_scr[d, hd] for hd, d in chains]
    gram = [lax.dot_general(q[i], k[i], (((1,), (1,)), ((), ())), preferred_element_type=F32) for i in n]
    inter = [jnp.dot(q[i], cmat[i].astype(BF16), preferred_element_type=F32) for i in n]
    s = [gram[i] * dmat[i] for i in n]
    intra = [jnp.dot(s[i].astype(BF16), v[i], preferred_element_type=F32) for i in n]
    qn = [jnp.sum(q[i].astype(F32) * nvec[i], axis=1, keepdims=True) for i in n]
    den = [a_int[i] * qn[i] + jnp.sum(s[i], axis=1, keepdims=True) for i in n]
    hh = [(a_int[i] * inter[i] + intra[i]) / jnp.maximum(jnp.abs(den[i]), jnp.exp(-m_r[i])) for i in n]
    b_last = [b_row[i][:, L - 1:L] if chains[i][1] == 0 else b_row[i][:, 0:1] for i in n]
    log_w = [b_last[i] - b_col[i] + i_col[i] for i in n]
    m_new = [jnp.maximum(b_last[i] + ms[i], jnp.max(log_w[i], axis=0, keepdims=True)) for i in n]
    w = [jnp.exp(log_w[i] - m_new[i]) for i in n]
    decay = [jnp.exp(b_last[i] + ms[i] - m_new[i]) for i in n]
    kw = [k[i].astype(F32) * w[i] for i in n]
    kv = [lax.dot_general(kw[i].astype(BF16), v[i], (((0,), (0,)), ((), ())), preferred_element_type=F32) for i in n]
    for i, (hd, d) in enumerate(chains):
        c_scr[d, hd] = decay[i] * cmat[i] + kv[i]
        n_scr[d, hd] = decay[i] * nvec[i] + jnp.sum(kw[i], axis=0, keepdims=True)
    return hh, m_new


SCAN_GROUP = 2 * M_HEADS


def _mlstm_scan_body(n_chunks, q_ref, k_ref, v_ref, gc_ref, gr_ref, h_ref, hb_scr, c_scr, n_scr, m0):
    chains = [(hd, d) for hd in range(M_HEADS) for d in range(2)]

    def body(c, ms):
        out = []
        for g0 in range(0, len(chains), SCAN_GROUP):
            grp = chains[g0:g0 + SCAN_GROUP]
            loaded = [_mlstm_load(hd, c if d == 0 else n_chunks - 1 - c, q_ref, k_ref, v_ref, gc_ref, gr_ref)
                      for hd, d in grp]
            hh, m_new = _mlstm_chunks(grp, ms[g0:g0 + SCAN_GROUP], loaded, c_scr, n_scr)
            for (hd, d), ld, h in zip(grp, loaded, hh):
                dst = h_ref if d == 0 else hb_scr
                dst[ld[0], ld[1]] = h
            out += m_new
        return tuple(out)

    ms = lax.fori_loop(0, n_chunks, body, tuple(m0))
    h_ref[...] += hb_scr[...]
    return ms


def _mlstm_scan_ctx_kernel(q_ref, k_ref, v_ref, gc_ref, gr_ref, h_ref, cn_ref, nn_ref, mn_ref, hb_scr, c_scr, n_scr):
    c_scr[...] = jnp.zeros(c_scr.shape, F32)
    n_scr[...] = jnp.zeros(n_scr.shape, F32)
    zero = jnp.zeros((1, 1), F32)
    ms = _mlstm_scan_body(SEQ // M_CHUNK, q_ref, k_ref, v_ref, gc_ref, gr_ref, h_ref, hb_scr, c_scr, n_scr,
                          [zero] * (2 * M_HEADS))
    cn_ref[0] = c_scr[...]
    nn_ref[0] = n_scr[...]
    for hd in range(M_HEADS):
        for d in range(2):
            mn_ref[0, d, hd] = jnp.broadcast_to(ms[2 * hd + d], (1, LANE))


def _mlstm_scan_lat_kernel(q_ref, k_ref, v_ref, gc_ref, gr_ref, c0_ref, n0_ref, m0_ref, ctx_out_ref, h_ref,
                           hb_scr, c_scr, n_scr):
    del ctx_out_ref
    c_scr[...] = c0_ref[0]
    n_scr[...] = n0_ref[0]
    m0 = [m0_ref[0, d, hd] for hd in range(M_HEADS) for d in range(2)]
    _mlstm_scan_body(DEC_SEQ // M_CHUNK, q_ref, k_ref, v_ref, gc_ref, gr_ref, h_ref, hb_scr, c_scr, n_scr, m0)


def _mlstm_scan(q, k, v, gcol, grow, state_c, state_n, state_m):
    hd = M_HEAD_DIM
    ng = 4 * M_HEADS
    state_scratch = [pltpu.VMEM((2, M_HEADS, hd, hd), F32), pltpu.VMEM((2, M_HEADS, 1, hd), F32)]
    ncp = SEQ // M_CHUNK
    h_ctx, new_c, new_n, new_m = pl.pallas_call(
        _mlstm_scan_ctx_kernel,
        out_shape=(jax.ShapeDtypeStruct((N_TOK, D), F32),
                   jax.ShapeDtypeStruct((BATCH, 2, M_HEADS, hd, hd), F32),
                   jax.ShapeDtypeStruct((BATCH, 2, M_HEADS, 1, hd), F32),
                   jax.ShapeDtypeStruct((BATCH, 2, M_HEADS, 1, LANE), F32)),
        grid=(BATCH,),
        in_specs=[
            pl.BlockSpec((SEQ, D), lambda s: (s, 0)),
            pl.BlockSpec((SEQ, D), lambda s: (s, 0)),
            pl.BlockSpec((SEQ, D), lambda s: (s, 0)),
            pl.BlockSpec((SEQ, LANE), lambda s: (s, 0)),
            pl.BlockSpec((ncp, ng, M_CHUNK), lambda s: (s, 0, 0)),
        ],
        out_specs=(
            pl.BlockSpec((SEQ, D), lambda s: (s, 0)),
            pl.BlockSpec((1, 2, M_HEADS, hd, hd), lambda s: (s, 0, 0, 0, 0)),
            pl.BlockSpec((1, 2, M_HEADS, 1, hd), lambda s: (s, 0, 0, 0, 0)),
            pl.BlockSpec((1, 2, M_HEADS, 1, LANE), lambda s: (s, 0, 0, 0, 0)),
        ),
        scratch_shapes=[pltpu.VMEM((SEQ, D), F32)] + state_scratch,
        compiler_params=_cparams(),
        name="mlstm_scan_ctx",
    )(q, k, v, gcol, grow)
    ncl = DEC_SEQ // M_CHUNK
    pb = NP_TOK // DEC_SEQ
    h_all = pl.pallas_call(
        _mlstm_scan_lat_kernel,
        out_shape=jax.ShapeDtypeStruct((N_TOK, D), F32),
        input_output_aliases={8: 0},
        grid=(DEC_BATCH,),
        in_specs=[
            pl.BlockSpec((DEC_SEQ, D), lambda b: (pb + b, 0)),
            pl.BlockSpec((DEC_SEQ, D), lambda b: (pb + b, 0)),
            pl.BlockSpec((DEC_SEQ, D), lambda b: (pb + b, 0)),
            pl.BlockSpec((DEC_SEQ, LANE), lambda b: (pb + b, 0)),
            pl.BlockSpec((ncl, ng, M_CHUNK), lambda b: (pb + b, 0, 0)),
            pl.BlockSpec((1, 2, M_HEADS, hd, hd), lambda b: (b, 0, 0, 0, 0)),
            pl.BlockSpec((1, 2, M_HEADS, 1, hd), lambda b: (b, 0, 0, 0, 0)),
            pl.BlockSpec((1, 2, M_HEADS, 1, 1), lambda b: (b, 0, 0, 0, 0)),
            pl.BlockSpec(memory_space=pl.ANY),
        ],
        out_specs=pl.BlockSpec((DEC_SEQ, D), lambda b: (pb + b, 0)),
        scratch_shapes=[pltpu.VMEM((DEC_SEQ, D), F32)] + state_scratch,
        compiler_params=_cparams(),
        name="mlstm_scan_lat",
    )(q, k, v, gcol, grow, state_c, state_n, state_m, h_ctx)
    return h_all, new_c, new_n, new_m


def _mlstm_out_kernel(mr_ref, h_ref, o_ref, ng_ref, w_ref, y_ref, mod_ref, out_ref, x_scr):
    hc = o_ref[...] * h_ref[...]
    for hd in range(M_HEADS):
        sl = slice(hd * M_HEAD_DIM, (hd + 1) * M_HEAD_DIM)
        x_scr[:, sl] = _rms(hc[:, sl], ng_ref[:, sl]).astype(BF16)
    out = jnp.dot(x_scr[...], w_ref[...], preferred_element_type=F32)
    out_ref[...] = y_ref[...] + _mod_vec(mod_ref, 2) * out


def _mlstm_out(hsum, o, norm_g, w_out, y, mods, mrow):
    return pl.pallas_call(
        _mlstm_out_kernel,
        out_shape=jax.ShapeDtypeStruct((N_TOK, D), F32),
        grid_spec=pltpu.PrefetchScalarGridSpec(
            num_scalar_prefetch=1, grid=(NL,),
            in_specs=[_tok_spec(D, TL), _tok_spec(D, TL), _full_spec((1, D)), _full_spec((D, D)), _tok_spec(D, TL),
                      _mod_spec()],
            out_specs=_tok_spec(D, TL),
            scratch_shapes=[pltpu.VMEM((TL, D), BF16)]),
        compiler_params=_cparams(),
        name="mlstm_out",
    )(mrow, hsum, o, norm_g, w_out, y, mods)


ROUTE_OFF = N_GROUPS
SLAB = D // (2 * LANE)
V7X_SC_CORES = 2
V7X_SC_SUBCORES = 16
SC_WORKERS = V7X_SC_CORES * V7X_SC_SUBCORES
SC_WINDOW = 128
SC_HALF = SC_WINDOW // 2
HI_MASK = -65536


def _bf16_bits(x):
    return lax.bitcast_convert_type(x.astype(BF16).astype(F32), I32)


def _store_slabs(ref, x):
    rows = x.shape[0]
    for c in range(SLAB):
        lo = lax.shift_right_logical(_bf16_bits(x[:, (2 * c) * LANE:(2 * c + 1) * LANE]), 16)
        hi = _bf16_bits(x[:, (2 * c + 1) * LANE:(2 * c + 2) * LANE]) & HI_MASK
        ref[pl.ds(c, rows, stride=SLAB), :] = lo | hi


def _load_slabs(ref, dst, rows, dtype):
    for c in range(SLAB):
        w = ref[pl.ds(c, rows, stride=SLAB), :]
        lo = lax.bitcast_convert_type(lax.shift_left(w, 16), F32)
        hi = lax.bitcast_convert_type(w & HI_MASK, F32)
        dst[:, (2 * c) * LANE:(2 * c + 1) * LANE] = lo.astype(dtype)
        dst[:, (2 * c + 1) * LANE:(2 * c + 2) * LANE] = hi.astype(dtype)


def _route_kernel(mr_ref, y_ref, mod_ref, g_ref, wr_ref, br_ref, tri_ref, x_ref, wt_ref, dest_ref, cnt_ref,
                  cnt_scr, meta_scr):
    x = _norm_mod(y_ref[...], g_ref[...], mod_ref, 1)
    _store_slabs(x_ref, x)
    lg = jnp.dot(x.astype(BF16), wr_ref[...], preferred_element_type=F32) + br_ref[...]
    lane = lax.broadcasted_iota(I32, lg.shape, 1).astype(F32)
    ninf = -jnp.inf
    big = float(LANE)
    lgg = jnp.where(lane < N_GROUPS, lg, ninf)
    gmax = jnp.max(lgg, axis=-1, keepdims=True)
    g_idx = jnp.min(jnp.where(lgg == gmax, lane, big), axis=-1, keepdims=True)
    g_w = 1.0 / jnp.sum(jnp.exp(lgg - gmax), axis=-1, keepdims=True)
    lo = ROUTE_OFF + g_idx * EXPERTS_PER_GROUP
    le = jnp.where((lane >= lo) & (lane < lo + EXPERTS_PER_GROUP), lg, ninf)
    m1 = jnp.max(le, axis=-1, keepdims=True)
    i1 = jnp.min(jnp.where(le == m1, lane, big), axis=-1, keepdims=True)
    le2 = jnp.where(lane == i1, ninf, le)
    m2 = jnp.max(le2, axis=-1, keepdims=True)
    i2 = jnp.min(jnp.where(le2 == m2, lane, big), axis=-1, keepdims=True)
    r = jnp.exp(m2 - m1)
    p1 = 1.0 / (1.0 + r)
    p2 = r / (1.0 + r)
    two = lax.broadcasted_iota(I32, (x.shape[0], TOP_K), 1)
    wt_ref[...] = jnp.where(two == 0, g_w * p1, g_w * p2)
    @pl.when(pl.program_id(0) == 0)
    def _():
        cnt_scr[...] = jnp.zeros(cnt_scr.shape, F32)

    oh1 = (lane == i1).astype(F32)
    oh2 = (lane == i2).astype(F32)
    both = oh1 + oh2
    before = jnp.dot(tri_ref[...], both.astype(BF16), preferred_element_type=F32) + cnt_scr[...]
    rk1 = jnp.sum(oh1 * before, axis=-1, keepdims=True)
    rk2 = jnp.sum(oh2 * before, axis=-1, keepdims=True)
    cnt_scr[...] = cnt_scr[...] + jnp.sum(both, axis=0, keepdims=True)
    cnt_ref[...] = cnt_scr[...]
    cols = (i1 - ROUTE_OFF, i2 - ROUTE_OFF, rk1, rk2)
    packed = jnp.zeros(lg.shape, F32)
    for c, val in enumerate(cols):
        packed = jnp.where(lane == c, val, packed)
    j = pl.program_id(0)
    meta_scr[:, pl.ds(pl.multiple_of(j * TL, TL), TL)] = jnp.transpose(packed)[0:len(cols), :].astype(I32)

    @pl.when(j == NL - 1)
    def _():
        sub8 = (SUBLANE, LANE)
        ln = lax.broadcasted_iota(I32, sub8, 1)
        counts = jnp.broadcast_to(cnt_scr[...], sub8)
        is_e = (ln >= ROUTE_OFF) & (ln < ROUTE_OFF + N_EXPERTS)
        padded = jnp.where(is_e, jnp.floor((counts + (EBLK - 1)) * (1.0 / EBLK)) * EBLK, 0.0)
        inc = padded
        sh = 1
        while sh < LANE:
            inc = inc + jnp.where(ln >= sh, pltpu.roll(inc, sh, 1), 0.0)
            sh *= 2
        start = pltpu.roll(inc - padded, LANE - ROUTE_OFF, 1)
        table = jnp.transpose(jnp.broadcast_to(start[0:1, :], (LANE, LANE)))[0:N_EXPERTS, 0:1].astype(I32)
        sub = lax.broadcasted_iota(I32, (N_EXPERTS, SLOT_COLS), 0)
        for c0 in range(0, N_TOK, SLOT_COLS):
            meta = meta_scr[:, c0:c0 + SLOT_COLS]
            for k in range(TOP_K):
                first = jnp.sum(jnp.where(sub == meta[k:k + 1, :], table, 0), axis=0, keepdims=True)
                dest_ref[k:k + 1, c0:c0 + SLOT_COLS] = first + meta[TOP_K + k:TOP_K + k + 1, :]


SLOT_COLS = 2048


def _route(y, mods, mrow, g2, w_route, b_route):
    return pl.pallas_call(
        _route_kernel,
        out_shape=(jax.ShapeDtypeStruct((N_TOK * SLAB, LANE), I32), jax.ShapeDtypeStruct((N_TOK, TOP_K), F32),
                   jax.ShapeDtypeStruct((TOP_K, N_TOK), I32), jax.ShapeDtypeStruct((1, LANE), F32)),
        grid_spec=pltpu.PrefetchScalarGridSpec(
            num_scalar_prefetch=1, grid=(NL,),
            in_specs=[_tok_spec(D, TL), _mod_spec(), _full_spec((1, D)), _full_spec((D, LANE)),
                      _full_spec((1, LANE)), _full_spec((TL, TL))],
            out_specs=(pl.BlockSpec((TL * SLAB, LANE), lambda j, *_: (j, 0)), _tok_spec(TOP_K, TL),
                       _full_spec((TOP_K, N_TOK)), _full_spec((1, LANE))),
            scratch_shapes=[pltpu.VMEM((1, LANE), F32), pltpu.VMEM((2 * TOP_K, N_TOK), I32)]),
        compiler_params=_cparams(),
        name="moe_route",
    )(mrow, y, mods, g2, w_route, b_route, jnp.asarray(np.tril(np.ones((TL, TL), np.float32), -1), dtype=BF16))


def _block_tables_of(lane_counts):
    counts = lane_counts[0, ROUTE_OFF:ROUTE_OFF + N_EXPERTS].astype(I32)
    padded = ((counts + EBLK - 1) // EBLK) * EBLK
    pad_end = jnp.cumsum(padded)
    pad_start = pad_end - padded
    n_blk = (padded // EBLK).astype(I32)
    blk_start = (pad_start // EBLK).astype(I32)
    n_used = (pad_end[-1] // EBLK).astype(I32).reshape(1)
    return blk_start, n_blk, n_used


def _sc_mesh():
    return plsc.VectorSubcoreMesh(core_axis_name="core", subcore_axis_name="subcore",
                                  num_cores=V7X_SC_CORES, num_subcores=V7X_SC_SUBCORES)


def _sc_worker():
    return lax.axis_index("core") * V7X_SC_SUBCORES + lax.axis_index("subcore")


def _sc_dispatch(x_slabs, dest):
    per = N_TOK // SC_WORKERS
    n_win = per // SC_HALF

    @functools.partial(
        pl.kernel, out_type=jax.ShapeDtypeStruct((P_SLOTS, SLAB, LANE), I32), mesh=_sc_mesh(), name="moe_dispatch",
        scratch_types=[pltpu.VMEM((1, per), I32), pltpu.VMEM((1, per), I32), pltpu.VMEM((2, SC_HALF, SLAB, LANE), I32),
                       pltpu.SemaphoreType.DMA((2,)), pltpu.SemaphoreType.DMA((2,))])
    def run(x_hbm, d_hbm, o_hbm, i0_v, i1_v, buf, lsem, ssem):
        base = _sc_worker() * per
        pltpu.sync_copy(d_hbm.at[pl.ds(0, 1), pl.ds(base, per)], i0_v)
        pltpu.sync_copy(d_hbm.at[pl.ds(1, 1), pl.ds(base, per)], i1_v)
        loads = [pltpu.make_async_copy(x_hbm.at[pl.ds(base + s * SC_HALF, SC_HALF)], buf.at[s % 2], lsem.at[s % 2])
                 for s in range(n_win)]
        loads[0].start()
        for s in range(n_win):
            loads[s].wait()
            if s + 1 < n_win:
                loads[s + 1].start()
            win = pl.ds(s * SC_HALF, SC_HALF)
            outs = [pltpu.make_async_copy(buf.at[s % 2], o_hbm.at[iv.at[0, win]], ssem.at[a])
                    for a, iv in enumerate((i0_v, i1_v))]
            for cp in outs:
                cp.start()
            for cp in outs:
                cp.wait()

    return run(x_slabs.reshape(N_TOK, SLAB, LANE), dest)


def _sc_collect(y_slabs, dest):
    per = N_ASSIGN // SC_WORKERS
    n_win = per // SC_HALF

    @functools.partial(
        pl.kernel, out_type=jax.ShapeDtypeStruct((N_ASSIGN, SLAB, LANE), I32), mesh=_sc_mesh(), name="moe_collect",
        scratch_types=[pltpu.VMEM((1, per), I32), pltpu.VMEM((2, SC_HALF, SLAB, LANE), I32),
                       pltpu.SemaphoreType.DMA((2,)), pltpu.SemaphoreType.DMA((2,))])
    def run(y_hbm, i_hbm, o_hbm, i_v, buf, gsem, wsem):
        w = _sc_worker()
        base = w * per
        per_a = SC_WORKERS // TOP_K
        pltpu.sync_copy(i_hbm.at[pl.ds(w // per_a, 1), pl.ds((w % per_a) * per, per)], i_v)
        gathers = [pltpu.make_async_copy(y_hbm.at[i_v.at[0, pl.ds(s * SC_HALF, SC_HALF)]], buf.at[s % 2],
                                         gsem.at[s % 2]) for s in range(n_win)]
        writes = [pltpu.make_async_copy(buf.at[s % 2], o_hbm.at[pl.ds(base + s * SC_HALF, SC_HALF)], wsem.at[s % 2])
                  for s in range(n_win)]
        gathers[0].start()
        for s in range(n_win):
            gathers[s].wait()
            if s >= 1:
                writes[s - 1].wait()
            if s + 1 < n_win:
                gathers[s + 1].start()
            writes[s].start()
        writes[n_win - 1].wait()

    return run(y_slabs.reshape(P_SLOTS, SLAB, LANE), dest)


EROWS = EBLK * SLAB


def _expert_kernel(bs_ref, nb_ref, nu_ref, wg_ref, wu_ref, wd_ref, x_hbm, y_hbm,
                   xbuf, ybuf, xs, wg_bf, wu_bf, wd_bf, isem, osem):
    e = pl.program_id(0)
    n_exp = pl.num_programs(0)
    n_used = nu_ref[0]
    b0 = bs_ref[e]
    nb = nb_ref[e]

    def in_copy(g, slot):
        return pltpu.make_async_copy(x_hbm.at[pl.ds(pl.multiple_of(g * EROWS, EROWS), EROWS)], xbuf.at[slot],
                                     isem.at[slot])

    def out_copy(g, slot):
        return pltpu.make_async_copy(ybuf.at[slot], y_hbm.at[pl.ds(pl.multiple_of(g * EROWS, EROWS), EROWS)],
                                     osem.at[slot])

    @pl.when(e == 0)
    def _():
        in_copy(0, 0).start()

    @pl.when(nb > 0)
    def _():
        wg_bf[...] = wg_ref[0, 0].astype(BF16)
        wu_bf[...] = wu_ref[0, 0].astype(BF16)
        wd_bf[...] = wd_ref[0, 0].astype(BF16)

    def block(k, carry):
        g = b0 + k
        slot = lax.rem(g, 2)
        in_copy(g, slot).wait()

        @pl.when(g + 1 < n_used)
        def _():
            in_copy(g + 1, 1 - slot).start()

        _load_slabs(xbuf.at[slot], xs, EBLK, BF16)
        xb = xs[...]
        gt = jnp.dot(xb, wg_bf[...], preferred_element_type=F32)
        up = jnp.dot(xb, wu_bf[...], preferred_element_type=F32)
        hmid = (gt * _sigmoid(gt) * up).astype(BF16)
        res = jnp.dot(hmid, wd_bf[...], preferred_element_type=F32)

        @pl.when(g >= 2)
        def _():
            out_copy(g - 2, slot).wait()

        _store_slabs(ybuf.at[slot], res)
        out_copy(g, slot).start()
        return carry

    lax.fori_loop(0, nb, block, 0)

    @pl.when(e == n_exp - 1)
    def _():
        last = n_used - 1
        out_copy(last, lax.rem(last, 2)).wait()

        @pl.when(n_used >= 2)
        def _():
            out_copy(last - 1, lax.rem(last - 1, 2)).wait()


def _experts(x_sorted, blk_start, n_blk, n_used, w_gate, w_up, w_down, layer):
    any_spec = pl.BlockSpec(memory_space=pl.ANY)
    wspec = lambda r, c: pl.BlockSpec((1, 1, r, c), lambda e, *_: (layer, e, 0, 0))
    return pl.pallas_call(
        _expert_kernel,
        out_shape=jax.ShapeDtypeStruct((P_SLOTS * SLAB, LANE), I32),
        grid_spec=pltpu.PrefetchScalarGridSpec(
            num_scalar_prefetch=3, grid=(N_EXPERTS,),
            in_specs=[wspec(D, D_EXPERT), wspec(D, D_EXPERT), wspec(D_EXPERT, D), any_spec],
            out_specs=any_spec,
            scratch_shapes=[
                pltpu.VMEM((2, EROWS, LANE), I32), pltpu.VMEM((2, EROWS, LANE), I32),
                pltpu.VMEM((EBLK, D), BF16),
                pltpu.VMEM((D, D_EXPERT), BF16), pltpu.VMEM((D, D_EXPERT), BF16), pltpu.VMEM((D_EXPERT, D), BF16),
                pltpu.SemaphoreType.DMA((2,)), pltpu.SemaphoreType.DMA((2,)),
            ]),
        compiler_params=_cparams(),
        name="moe_experts",
    )(blk_start, n_blk, n_used, w_gate, w_up, w_down, x_sorted.reshape(P_SLOTS * SLAB, LANE))


def _combine_kernel(final, mr_ref, e0_ref, e1_ref, wt_ref, y_ref, mod_ref, fg_ref, o_ref, a_scr, b_scr):
    _load_slabs(e0_ref, a_scr, TL, F32)
    _load_slabs(e1_ref, b_scr, TL, F32)
    wt = wt_ref[...]
    moe = wt[:, 0:1] * a_scr[...] + wt[:, 1:2] * b_scr[...]
    y_new = y_ref[...] + _mod_vec(mod_ref, 5) * moe
    o_ref[...] = _rms(y_new, fg_ref[...]) if final else y_new


def _combine(ym, wts, y, mods, mrow, final_g, blk0, nblk, final):
    tok = lambda width: pl.BlockSpec((TL, width), lambda j, *_: (blk0 + j, 0))
    slab0 = pl.BlockSpec((TL * SLAB, LANE), lambda j, *_: (blk0 + j, 0))
    slab1 = pl.BlockSpec((TL * SLAB, LANE), lambda j, *_: (NL + blk0 + j, 0))
    mod = pl.BlockSpec((1, 1, 6 * D), lambda j, mr: (mr[blk0 + j], 0, 0))
    return pl.pallas_call(
        functools.partial(_combine_kernel, final),
        out_shape=jax.ShapeDtypeStruct((nblk * TL, D), F32),
        grid_spec=pltpu.PrefetchScalarGridSpec(
            num_scalar_prefetch=1, grid=(nblk,),
            in_specs=[slab0, slab1, tok(TOP_K), tok(D), mod, _full_spec((1, D))],
            out_specs=pl.BlockSpec((TL, D), lambda j, *_: (j, 0)),
            scratch_shapes=[pltpu.VMEM((TL, D), F32), pltpu.VMEM((TL, D), F32)]),
        compiler_params=_cparams(),
        name="moe_combine",
    )(mrow, ym, ym, wts, y, mods, final_g)


def kernel(x_prompt, x_sample, cache_attn_k, cache_attn_v, state_mlstm_C, state_mlstm_n, state_mlstm_m, c, c_ctx, ada_w, ada_b, norm1_g, norm2_g, conv_w_in, conv_w_dw, conv_b_dw, conv_ln_g, conv_ln_b, conv_w_out, attn_w_qkv, attn_q_norm, attn_k_norm, attn_w_o, mlstm_w_in, mlstm_b_gate, mlstm_norm_g, mlstm_w_out, moe_w_group, moe_b_group, moe_w_router, moe_b_router, moe_w_gate, moe_w_up, moe_w_down, final_norm_g):
    y = None
    cvec = jnp.concatenate([c_ctx[None, :], c, jnp.zeros((MOD_ROWS - 1 - DEC_BATCH, D), F32)], axis=0)
    rope = _rope_blocks()
    mrow, mrow_sb, mrow_l = jnp.asarray(_MOD_ROW), jnp.asarray(_MOD_ROW_SB), jnp.asarray(_MOD_ROW_L)
    new_k = new_v = new_c = new_n = new_m = None
    for i in range(DEPTH):
        kind, slot = i % 3, i // 3
        mods = _ada_layer(cvec, ada_w, ada_b, i)
        g1 = norm1_g[i].reshape(1, D)
        if kind == 0:
            src = (x_prompt.reshape(NP_TOK, D), x_sample.reshape(NS_TOK, D), True) if i == 0 else (y, y, False)
            u = _conv_in(*src, mods, mrow_l, g1, conv_w_in, slot)
            w_dw = jnp.concatenate([conv_w_dw[slot], jnp.zeros((1, D), F32)], axis=0)
            y = _conv_main(u, *src, mods, mrow_sb, w_dw, conv_b_dw[slot].reshape(1, D), conv_ln_g[slot].reshape(1, D),
                           conv_ln_b[slot].reshape(1, D), conv_w_out, slot)
        elif kind == 1:
            q, kb, vb, kf, vf = _attn_qkv(y, mods, mrow_sb, g1, attn_w_qkv[slot].astype(BF16),
                                          attn_q_norm[slot].reshape(1, HEAD_DIM), attn_k_norm[slot].reshape(1, HEAD_DIM),
                                          rope)
            new_k = kf[:NP_TOK].reshape(BATCH, 1, SEQ, N_KV_HEADS, HEAD_DIM)
            new_v = vf[:NP_TOK].reshape(BATCH, 1, SEQ, N_KV_HEADS, HEAD_DIM)
            ck = cache_attn_k[:, slot].reshape(DEC_BATCH, PAST_LEN, KV_DIM)
            cv = cache_attn_v[:, slot].reshape(DEC_BATCH, PAST_LEN, KV_DIM)
            y = _attention(q, kb, vb, ck, cv, attn_w_o[slot].astype(BF16), y, mods)
        else:
            b_gate = jnp.concatenate([mlstm_b_gate[slot], jnp.zeros((LANE - 4 * M_HEADS,), F32)]).reshape(1, LANE)
            q, k, v, o, gates, grow = _mlstm_in(y, mods, mrow, g1, mlstm_w_in, slot, b_gate)
            sc = state_mlstm_C[:, slot]
            sn = state_mlstm_n[:, slot].reshape(DEC_BATCH, 2, M_HEADS, 1, M_HEAD_DIM)
            sm = state_mlstm_m[:, slot].reshape(DEC_BATCH, 2, M_HEADS, 1, 1)
            hsum, nc_, nn_, nm_ = _mlstm_scan(q, k, v, gates, grow, sc, sn, sm)
            new_c = nc_[:, None]
            new_n = nn_.reshape(BATCH, 1, 2, M_HEADS, M_HEAD_DIM)
            new_m = nm_[..., 0, 0].reshape(BATCH, 1, 2, M_HEADS)
            y = _mlstm_out(hsum, o, mlstm_norm_g[slot].reshape(1, D), mlstm_w_out[slot].astype(BF16), y, mods, mrow_l)
        w_route = jnp.concatenate([moe_w_group[i], moe_w_router[i],
                                   jnp.zeros((D, LANE - N_GROUPS - N_EXPERTS), F32)], axis=1)
        b_route = jnp.concatenate([moe_b_group[i], moe_b_router[i],
                                   jnp.zeros((LANE - N_GROUPS - N_EXPERTS,), F32)]).reshape(1, LANE)
        x2, ewt, dest, cnt = _route(y, mods, mrow_l, norm2_g[i].reshape(1, D), w_route.astype(BF16), b_route)
        blk_start, n_blk, n_used = _block_tables_of(cnt)
        x_sorted = _sc_dispatch(x2, dest)
        y_sorted = _experts(x_sorted, blk_start, n_blk, n_used, moe_w_gate, moe_w_up, moe_w_down, i)
        ym = _sc_collect(y_sorted, dest)
        ym = ym.reshape(N_ASSIGN * SLAB, LANE)
        fg = final_norm_g.reshape(1, D)
        if i + 1 < DEPTH:
            y = _combine(ym, ewt, y, mods, mrow_l, fg, 0, NL, False)
        else:
            y_prompt = _combine(ym, ewt, y, mods, mrow_l, fg, 0, NLP, True).reshape(BATCH, SEQ, D)
            y_sample = _combine(ym, ewt, y, mods, mrow_l, fg, NLP, NL - NLP, True).reshape(DEC_BATCH, DEC_SEQ, D)
    return (y_prompt, y_sample, new_k, new_v, new_c, new_n, new_m)
```

```python
import functools

import jax
import jax.numpy as jnp
import numpy as np
from jax import lax
from jax.experimental import pallas as pl
from jax.experimental.pallas import tpu as pltpu
from jax.experimental.pallas import tpu_sc as plsc

F32 = jnp.float32
BF16 = jnp.bfloat16
I32 = jnp.int32

D = 1024
BATCH, SEQ = 16, 256
DEC_BATCH, DEC_SEQ = 8, 1024
PAST_LEN = 256
DEPTH = 4
GRID_W = 64
EPS = 1e-6
CONV_WIDTH = 31
CONV_PAD = CONV_WIDTH // 2
HEAD_DIM = 128
N_HEADS = 8
N_KV_HEADS = 2
GQA_GROUP = N_HEADS // N_KV_HEADS
Q_DIM = N_HEADS * HEAD_DIM
KV_DIM = N_KV_HEADS * HEAD_DIM
QKV_DIM = Q_DIM + 2 * KV_DIM
ROPE_THETA = 10000.0
M_HEADS = 4
M_HEAD_DIM = D // M_HEADS
M_CHUNK = 64
N_GROUPS = 4
EXPERTS_PER_GROUP = 8
N_EXPERTS = N_GROUPS * EXPERTS_PER_GROUP
TOP_K = 2
D_EXPERT = 512

NP_TOK = BATCH * SEQ
NS_TOK = DEC_BATCH * DEC_SEQ
N_TOK = NP_TOK + NS_TOK
TM = 512
NB = N_TOK // TM
NBP = NP_TOK // TM
BLK_PER_DEC = DEC_SEQ // TM
TL = 1024
NL = N_TOK // TL
NLP = NP_TOK // TL
SB = 256
NSB = N_TOK // SB
NSBP = NP_TOK // SB
SB_PER_DEC = DEC_SEQ // SB
MOD_ROWS = 16
HALO = 16
LANE = 128
SUBLANE = 8

N_ASSIGN = N_TOK * TOP_K
EBLK = 256
N_EBLK = N_ASSIGN // EBLK + N_EXPERTS
P_SLOTS = N_EBLK * EBLK
N_PAD_SLOTS = P_SLOTS - N_ASSIGN

VMEM_LIMIT = 56 * 1024 * 1024


def _block_tables(nb, nbp, per_dec):
    j = np.arange(nb)
    is_p = j < nbp
    mod_row = np.where(is_p, 0, 1 + (j - nbp) // per_dec)
    rope_idx = np.where(is_p, 0, 1 + (j - nbp) % per_dec)
    first = np.where(is_p, 1, ((j - nbp) % per_dec == 0).astype(np.int64))
    last = np.where(is_p, 1, ((j - nbp) % per_dec == per_dec - 1).astype(np.int64))
    return (mod_row.astype(np.int32), rope_idx.astype(np.int32), first.astype(np.int32), last.astype(np.int32))


_MOD_ROW, _, _, _ = _block_tables(NB, NBP, BLK_PER_DEC)
_MOD_ROW_L, _, _, _ = _block_tables(NL, NLP, DEC_SEQ // TL)
_MOD_ROW_SB, _ROPE_IDX_SB, _SEQ_FIRST, _SEQ_LAST = _block_tables(NSB, NSBP, SB_PER_DEC)


def _cparams(n_axes=1):
    return pltpu.CompilerParams(dimension_semantics=("arbitrary",) * n_axes, vmem_limit_bytes=VMEM_LIMIT)


def _sigmoid(x):
    return 1.0 / (1.0 + jnp.exp(-x))


def _rms(x, g):
    return x * lax.rsqrt(jnp.mean(x * x, axis=-1, keepdims=True) + EPS) * g


def _mod_vec(mod_ref, k):
    return mod_ref[0, :, k * D:(k + 1) * D]


def _norm_mod(y, g, mod_ref, which):
    return _rms(y, g) * (1.0 + _mod_vec(mod_ref, 3 * which + 1)) + _mod_vec(mod_ref, 3 * which)


def _ada_kernel(c_ref, w_ref, b_ref, o_ref):
    c = c_ref[...]
    s = c * _sigmoid(c)
    res = jnp.dot(s.astype(BF16), w_ref[0].astype(BF16), preferred_element_type=F32) + b_ref[0]
    for r in range(MOD_ROWS):
        o_ref[r] = res[r:r + 1, :]


def _ada_layer(cvec, ada_w, ada_b, layer):
    tn = 1536
    return pl.pallas_call(
        _ada_kernel,
        out_shape=jax.ShapeDtypeStruct((MOD_ROWS, 1, 6 * D), F32),
        grid=(6 * D // tn,),
        in_specs=[
            pl.BlockSpec((MOD_ROWS, D), lambda n: (0, 0)),
            pl.BlockSpec((1, D, tn), lambda n: (layer, 0, n)),
            pl.BlockSpec((1, 1, tn), lambda n: (layer, 0, n)),
        ],
        out_specs=pl.BlockSpec((MOD_ROWS, 1, tn), lambda n: (0, 0, n)),
        compiler_params=_cparams(1),
        name="ada_mod",
    )(cvec, ada_w, ada_b.reshape(DEPTH, 1, 6 * D))


def _tok_spec(width, rows=TM):
    return pl.BlockSpec((rows, width), lambda j, *_: (j, 0))


def _mod_spec():
    return pl.BlockSpec((1, 1, 6 * D), lambda j, mr, *_: (mr[j], 0, 0))


def _full_spec(shape):
    nd = len(shape)
    return pl.BlockSpec(shape, lambda j, *_: (0,) * nd)


def _pair_specs(rows, nbp, split):
    s_off = nbp if split else 0
    return [pl.BlockSpec((rows, D), lambda j, *_: (jnp.minimum(j, nbp - 1), 0)),
            pl.BlockSpec((rows, D), lambda j, *_: (jnp.maximum(j, nbp) - s_off, 0))]


def _pair_block(nbp, yp_ref, ys_ref):
    return jnp.where(pl.program_id(0) < nbp, yp_ref[...], ys_ref[...])


def _resident_weight_spec(w_all, slot):
    return pl.BlockSpec((1,) + w_all.shape[1:], lambda j, *_: (slot, 0, 0), pipeline_mode=pl.Buffered(1))


def _conv_in_kernel(mr_ref, yp_ref, ys_ref, mod_ref, g_ref, wf_ref, u_ref, w_ref):
    @pl.when(pl.program_id(0) == 0)
    def _():
        w_ref[...] = wf_ref[0].astype(BF16)

    h = _norm_mod(_pair_block(NLP, yp_ref, ys_ref), g_ref[...], mod_ref, 0)
    ag = jnp.dot(h.astype(BF16), w_ref[...], preferred_element_type=F32)
    u_ref[...] = ag[:, :D] * _sigmoid(ag[:, D:])


def _conv_in(yp, ys, split, mods, mrow, g1, w_in_all, slot):
    return pl.pallas_call(
        _conv_in_kernel,
        out_shape=jax.ShapeDtypeStruct((N_TOK, D), F32),
        grid_spec=pltpu.PrefetchScalarGridSpec(
            num_scalar_prefetch=1, grid=(NL,),
            in_specs=_pair_specs(TL, NLP, split) + [_mod_spec(), _full_spec((1, D)),
                                                    _resident_weight_spec(w_in_all, slot)],
            out_specs=_tok_spec(D, TL),
            scratch_shapes=[pltpu.VMEM((D, 2 * D), BF16)]),
        compiler_params=_cparams(),
        name="conv_in",
    )(mrow, yp, ys, mods, g1, w_in_all)


def _conv_main_kernel(mr_ref, first_ref, last_ref, u_ref, up_ref, un_ref, wdw_ref, bdw_ref, lg_ref, lb_ref,
                      woutf_ref, yp_ref, ys_ref, mod_ref, o_ref, ext_ref, acc_ref, wout_ref):
    j = pl.program_id(0)

    @pl.when(j == 0)
    def _():
        wout_ref[...] = woutf_ref[0].astype(BF16)

    zero = jnp.zeros((HALO, D), F32)
    ext_ref[0:HALO, :] = jnp.where(first_ref[j] == 1, zero, up_ref[...])
    ext_ref[HALO:HALO + SB, :] = u_ref[...]
    ext_ref[HALO + SB:2 * HALO + SB, :] = jnp.where(last_ref[j] == 1, zero, un_ref[...])

    off0 = HALO - CONV_PAD
    n_a = (off0 + CONV_WIDTH - 1) // SUBLANE + 1
    n_chunks = SB // SUBLANE

    def strip(ci, carry):
        cs = pl.ds(pl.multiple_of(ci * LANE, LANE), LANE)
        wk = [jnp.broadcast_to(wdw_ref[k:k + 1, cs], (SUBLANE, LANE)) for k in range(CONV_WIDTH)]
        bias = jnp.broadcast_to(bdw_ref[:, cs], (SUBLANE, LANE))
        sub = lax.broadcasted_iota(I32, (SUBLANE, LANE), 0)
        prev_rot, prev_v0 = None, None
        for j in range(n_chunks + 1):
            tiles = [ext_ref[SUBLANE * (j + a):SUBLANE * (j + a + 1), cs] for a in range(n_a)]
            part = []
            for s in range(SUBLANE):
                acc = None
                for a in range(n_a):
                    k = SUBLANE * a + s - off0
                    if (0 <= k < CONV_WIDTH) and not (s == 0 and j == n_chunks):
                        term = tiles[a] * wk[k]
                        acc = term if acc is None else acc + term
                part.append(acc)
            rot = [None] + [pltpu.roll(part[s], SUBLANE - s, 0) for s in range(1, SUBLANE)]
            if j >= 1:
                out = prev_v0 + bias
                for s in range(1, SUBLANE):
                    out = out + jnp.where(sub < SUBLANE - s, prev_rot[s], rot[s])
                acc_ref[SUBLANE * (j - 1):SUBLANE * j, cs] = out
            prev_rot, prev_v0 = rot, part[0]
        return carry

    lax.fori_loop(0, D // LANE, strip, 0)

    c = acc_ref[...]
    mu = jnp.mean(c, axis=-1, keepdims=True)
    cc = c - mu
    var = jnp.mean(cc * cc, axis=-1, keepdims=True)
    z = cc * lax.rsqrt(var + EPS) * lg_ref[...] + lb_ref[...]
    z = z * _sigmoid(z)
    out = jnp.dot(z.astype(BF16), wout_ref[...], preferred_element_type=F32)
    o_ref[...] = _pair_block(NSBP, yp_ref, ys_ref) + _mod_vec(mod_ref, 2) * out


def _conv_main(u, yp, ys, split, mods, mrow, w_dw, b_dw, ln_g, ln_b, w_out_all, slot):
    nh = N_TOK // HALO
    per = SB // HALO
    sb_spec = pl.BlockSpec((SB, D), lambda j, *_: (j, 0))
    return pl.pallas_call(
        _conv_main_kernel,
        out_shape=jax.ShapeDtypeStruct((N_TOK, D), F32),
        grid_spec=pltpu.PrefetchScalarGridSpec(
            num_scalar_prefetch=3, grid=(NSB,),
            in_specs=[
                sb_spec,
                pl.BlockSpec((HALO, D), lambda j, *_: (jnp.maximum(j * per - 1, 0), 0)),
                pl.BlockSpec((HALO, D), lambda j, *_: (jnp.minimum((j + 1) * per, nh - 1), 0)),
                _full_spec((CONV_WIDTH + 1, D)), _full_spec((1, D)), _full_spec((1, D)), _full_spec((1, D)),
                _resident_weight_spec(w_out_all, slot), *_pair_specs(SB, NSBP, split), _mod_spec(),
            ],
            out_specs=sb_spec,
            scratch_shapes=[pltpu.VMEM((SB + 2 * HALO, D), F32), pltpu.VMEM((SB, D), F32), pltpu.VMEM((D, D), BF16)]),
        compiler_params=_cparams(),
        name="conv_main",
    )(mrow, jnp.asarray(_SEQ_FIRST), jnp.asarray(_SEQ_LAST), u, u, u, w_dw, b_dw, ln_g, ln_b, w_out_all, yp, ys, mods)


def _rope_angles():
    rows = DEC_SEQ // GRID_W
    row = jnp.repeat(jnp.arange(rows, dtype=F32), GRID_W)
    col = jnp.tile(jnp.arange(GRID_W, dtype=F32), rows)
    axis_dim = HEAD_DIM // 2
    freqs = jnp.power(ROPE_THETA, -jnp.arange(axis_dim // 2, dtype=F32) * 2.0 / axis_dim)
    ang_r = row[:, None] * freqs[None, :]
    ang_c = col[:, None] * freqs[None, :]
    return jnp.concatenate([ang_r, ang_r, ang_c, ang_c], axis=-1)


def _rope_blocks():
    ang = _rope_angles()
    cos, sin = jnp.cos(ang), jnp.sin(ang)
    lane = np.arange(HEAD_DIM)
    lo = jnp.asarray(((lane % (HEAD_DIM // 2)) < HEAD_DIM // 4).astype(np.float32))
    sin_a = -sin * lo[None, :]
    sin_b = sin * (1.0 - lo)[None, :]
    nblk = DEC_SEQ // SB
    ident = jnp.ones((1, SB, HEAD_DIM), F32)
    zeros = jnp.zeros((1, SB, HEAD_DIM), F32)
    cos_t = jnp.concatenate([ident, cos.reshape(nblk, SB, HEAD_DIM)], axis=0)
    sa_t = jnp.concatenate([zeros, sin_a.reshape(nblk, SB, HEAD_DIM)], axis=0)
    sb_t = jnp.concatenate([zeros, sin_b.reshape(nblk, SB, HEAD_DIM)], axis=0)
    return cos_t, sa_t, sb_t


def _attn_qkv_kernel(mr_ref, ri_ref, y_ref, mod_ref, g_ref, wf_ref, qg_ref, kg_ref, cos_ref, sa_ref, sb_ref,
                     q_ref, kb_ref, vb_ref, kf_ref, vf_ref, w_ref):
    @pl.when(pl.program_id(0) == 0)
    def _():
        w_ref[...] = wf_ref[0].astype(BF16)

    h = _norm_mod(y_ref[...], g_ref[...], mod_ref, 0)
    qkv = jnp.dot(h.astype(BF16), w_ref[...], preferred_element_type=F32)
    cos, sa, sb = cos_ref[0], sa_ref[0], sb_ref[0]
    quarter = HEAD_DIM // 4

    def head(x, g):
        xn = _rms(x, g)
        return xn * cos + pltpu.roll(xn, HEAD_DIM - quarter, 1) * sa + pltpu.roll(xn, quarter, 1) * sb

    scale = HEAD_DIM ** -0.5
    for hd in range(N_HEADS):
        sl = slice(hd * HEAD_DIM, (hd + 1) * HEAD_DIM)
        q_ref[:, sl] = (head(qkv[:, sl], qg_ref[...]) * scale).astype(BF16)
    for kv in range(N_KV_HEADS):
        sl = slice(kv * HEAD_DIM, (kv + 1) * HEAD_DIM)
        kr = head(qkv[:, Q_DIM + kv * HEAD_DIM:Q_DIM + (kv + 1) * HEAD_DIM], kg_ref[...])
        kf_ref[:, sl] = kr
        kb_ref[:, sl] = kr.astype(BF16)
    v = qkv[:, Q_DIM + KV_DIM:]
    vf_ref[...] = v
    vb_ref[...] = v.astype(BF16)


def _attn_qkv(y, mods, mrow, g1, w_qkv_all, slot, q_g, k_g, rope):
    cos_t, sa_t, sb_t = rope
    rspec = pl.BlockSpec((1, SB, HEAD_DIM), lambda j, mr, ri: (ri[j], 0, 0))
    return pl.pallas_call(
        _attn_qkv_kernel,
        out_shape=(jax.ShapeDtypeStruct((N_TOK, Q_DIM), BF16), jax.ShapeDtypeStruct((N_TOK, KV_DIM), BF16),
                   jax.ShapeDtypeStruct((N_TOK, KV_DIM), BF16), jax.ShapeDtypeStruct((N_TOK, KV_DIM), F32),
                   jax.ShapeDtypeStruct((N_TOK, KV_DIM), F32)),
        grid_spec=pltpu.PrefetchScalarGridSpec(
            num_scalar_prefetch=2, grid=(NSB,),
            in_specs=[_tok_spec(D, SB), _mod_spec(), _full_spec((1, D)), _resident_weight_spec(w_qkv_all, slot),
                      _full_spec((1, HEAD_DIM)), _full_spec((1, HEAD_DIM)), rspec, rspec, rspec],
            out_specs=(_tok_spec(Q_DIM, SB), _tok_spec(KV_DIM, SB), _tok_spec(KV_DIM, SB), _tok_spec(KV_DIM, SB),
                       _tok_spec(KV_DIM, SB)),
            scratch_shapes=[pltpu.VMEM((D, QKV_DIM), BF16)]),
        compiler_params=_cparams(),
        name="attn_qkv",
    )(mrow, jnp.asarray(_ROPE_IDX_SB), y, mods, g1, w_qkv_all, q_g, k_g, cos_t, sa_t, sb_t)


def _attn_heads(q, ks, vs, o_scr):
    nt = (((1,), (1,)), ((), ()))
    for hd in range(N_HEADS):
        g = hd // GQA_GROUP
        qh = q[:, hd * HEAD_DIM:(hd + 1) * HEAD_DIM]
        gs = slice(g * HEAD_DIM, (g + 1) * HEAD_DIM)
        ss = [lax.dot_general(qh, k[:, gs], nt, preferred_element_type=F32) for k in ks]
        m = functools.reduce(jnp.maximum, [jnp.max(s, axis=-1, keepdims=True) for s in ss])
        ps = [jnp.exp(s - m) for s in ss]
        l = functools.reduce(lambda a, b: a + b, [jnp.sum(p, axis=-1, keepdims=True) for p in ps])
        o = functools.reduce(lambda a, b: a + b,
                             [jnp.dot(p.astype(BF16), v[:, gs], preferred_element_type=F32) for p, v in zip(ps, vs)])
        o_scr[:, hd * HEAD_DIM:(hd + 1) * HEAD_DIM] = (o / l).astype(BF16)


def _attn_ctx_kernel(q_ref, k_ref, v_ref, wo_ref, y_ref, mod_ref, o_ref, o_scr):
    _attn_heads(q_ref[...], [k_ref[...]], [v_ref[...]], o_scr)
    out = jnp.dot(o_scr[...], wo_ref[...], preferred_element_type=F32)
    o_ref[...] = y_ref[...] + _mod_vec(mod_ref, 2) * out


def _attn_lat_kernel(q_ref, k_ref, v_ref, ck_ref, cv_ref, wo_ref, y_ref, mod_ref, ctx_out_ref, o_ref, o_scr):
    del ctx_out_ref
    _attn_heads(q_ref[...], [k_ref[...], ck_ref[0].astype(BF16)], [v_ref[...], cv_ref[0].astype(BF16)], o_scr)
    out = jnp.dot(o_scr[...], wo_ref[...], preferred_element_type=F32)
    o_ref[...] = y_ref[...] + _mod_vec(mod_ref, 2) * out


def _attention(q, kb, vb, cache_k, cache_v, w_o, y, mods):
    y_ctx = pl.pallas_call(
        _attn_ctx_kernel,
        out_shape=jax.ShapeDtypeStruct((N_TOK, D), F32),
        grid=(BATCH,),
        in_specs=[
            pl.BlockSpec((SEQ, Q_DIM), lambda s: (s, 0)),
            pl.BlockSpec((SEQ, KV_DIM), lambda s: (s, 0)),
            pl.BlockSpec((SEQ, KV_DIM), lambda s: (s, 0)),
            pl.BlockSpec((Q_DIM, D), lambda s: (0, 0)),
            pl.BlockSpec((SEQ, D), lambda s: (s, 0)),
            pl.BlockSpec((1, 1, 6 * D), lambda s: (0, 0, 0)),
        ],
        out_specs=pl.BlockSpec((SEQ, D), lambda s: (s, 0)),
        scratch_shapes=[pltpu.VMEM((SEQ, Q_DIM), BF16)],
        compiler_params=_cparams(),
        name="attn_ctx",
    )(q, kb, vb, w_o, y, mods)
    pb = NP_TOK // DEC_SEQ
    return pl.pallas_call(
        _attn_lat_kernel,
        out_shape=jax.ShapeDtypeStruct((N_TOK, D), F32),
        input_output_aliases={8: 0},
        grid=(DEC_BATCH, SB_PER_DEC),
        in_specs=[
            pl.BlockSpec((SB, Q_DIM), lambda b, t: (NSBP + b * SB_PER_DEC + t, 0)),
            pl.BlockSpec((DEC_SEQ, KV_DIM), lambda b, t: (pb + b, 0)),
            pl.BlockSpec((DEC_SEQ, KV_DIM), lambda b, t: (pb + b, 0)),
            pl.BlockSpec((1, PAST_LEN, KV_DIM), lambda b, t: (b, 0, 0)),
            pl.BlockSpec((1, PAST_LEN, KV_DIM), lambda b, t: (b, 0, 0)),
            pl.BlockSpec((Q_DIM, D), lambda b, t: (0, 0)),
            pl.BlockSpec((SB, D), lambda b, t: (NSBP + b * SB_PER_DEC + t, 0)),
            pl.BlockSpec((1, 1, 6 * D), lambda b, t: (1 + b, 0, 0)),
            pl.BlockSpec(memory_space=pl.ANY),
        ],
        out_specs=pl.BlockSpec((SB, D), lambda b, t: (NSBP + b * SB_PER_DEC + t, 0)),
        scratch_shapes=[pltpu.VMEM((SB, Q_DIM), BF16)],
        compiler_params=_cparams(2),
        name="attn_lat",
    )(q, kb, vb, cache_k, cache_v, w_o, y, mods, y_ctx)


def _log_sigmoid(x):
    return jnp.minimum(x, 0.0) - jnp.log(1.0 + jnp.exp(-jnp.abs(x)))


W_T = 256


def _mlstm_in_kernel(mr_ref, y_ref, mod_ref, g_ref, wt_ref, bg_ref, q_ref, k_ref, v_ref, o_ref, gt_ref, gr_ref,
                     w_ref, wg_ref):
    @pl.when(pl.program_id(0) == 0)
    def _():
        for r in range(0, 4 * D, W_T):
            w_ref[:, r:r + W_T] = jnp.transpose(wt_ref[0, r:r + W_T, :]).astype(BF16)
        n_out = 4 * D + 4 * M_HEADS
        tail = jnp.transpose(wt_ref[0, n_out - LANE:n_out, :])
        lane = lax.broadcasted_iota(I32, tail.shape, 1)
        wg_ref[...] = jnp.where(lane < 4 * M_HEADS, pltpu.roll(tail, 4 * M_HEADS, axis=1), 0.0).astype(BF16)

    h = _norm_mod(y_ref[...], g_ref[...], mod_ref, 0)
    hb = h.astype(BF16)
    q_ref[...] = jnp.dot(hb, w_ref[:, 0:D], preferred_element_type=F32).astype(BF16)
    k_ref[...] = (jnp.dot(hb, w_ref[:, D:2 * D], preferred_element_type=F32) * (M_HEAD_DIM ** -0.5)).astype(BF16)
    v_ref[...] = jnp.dot(hb, w_ref[:, 2 * D:3 * D], preferred_element_type=F32).astype(BF16)
    o_ref[...] = _sigmoid(jnp.dot(hb, w_ref[:, 3 * D:4 * D], preferred_element_type=F32))
    gt = jnp.dot(hb, wg_ref[...], preferred_element_type=F32) + bg_ref[...]
    lane = lax.broadcasted_iota(I32, gt.shape, 1)
    is_f = ((lane >= M_HEADS) & (lane < 2 * M_HEADS)) | ((lane >= 3 * M_HEADS) & (lane < 4 * M_HEADS))
    gt = jnp.where(is_f, _log_sigmoid(gt), gt)
    gt_ref[...] = gt
    per = LANE // M_CHUNK
    for p in range(TM // LANE):
        t = jnp.transpose(gt[p * LANE:(p + 1) * LANE, :])
        for f in range(per):
            gr_ref[p * per + f] = t[0:4 * M_HEADS, f * M_CHUNK:(f + 1) * M_CHUNK]


def _mlstm_in(y, mods, mrow, g1, w_in_all, slot, b_gate):
    w_t = jnp.swapaxes(w_in_all, 1, 2)
    w_spec = pl.BlockSpec((1,) + w_t.shape[1:], lambda j, *_: (slot, 0, 0), pipeline_mode=pl.Buffered(1))
    return pl.pallas_call(
        _mlstm_in_kernel,
        out_shape=(jax.ShapeDtypeStruct((N_TOK, D), BF16), jax.ShapeDtypeStruct((N_TOK, D), BF16),
                   jax.ShapeDtypeStruct((N_TOK, D), BF16), jax.ShapeDtypeStruct((N_TOK, D), F32),
                   jax.ShapeDtypeStruct((N_TOK, LANE), F32),
                   jax.ShapeDtypeStruct((N_TOK // M_CHUNK, 4 * M_HEADS, M_CHUNK), F32)),
        grid_spec=pltpu.PrefetchScalarGridSpec(
            num_scalar_prefetch=1, grid=(NB,),
            in_specs=[_tok_spec(D), _mod_spec(), _full_spec((1, D)), w_spec, _full_spec((1, LANE))],
            out_specs=(_tok_spec(D), _tok_spec(D), _tok_spec(D), _tok_spec(D), _tok_spec(LANE),
                       pl.BlockSpec((TM // M_CHUNK, 4 * M_HEADS, M_CHUNK), lambda j, *_: (j, 0, 0))),
            scratch_shapes=[pltpu.VMEM((D, 4 * D), BF16), pltpu.VMEM((D, LANE), BF16)]),
        compiler_params=_cparams(),
        name="mlstm_in",
    )(mrow, y, mods, g1, w_t, b_gate)


def _mlstm_load(hd, c, q_ref, k_ref, v_ref, gc_ref, gr_ref):
    r0 = pl.multiple_of(c * M_CHUNK, M_CHUNK)
    hs = slice(hd * M_HEAD_DIM, (hd + 1) * M_HEAD_DIM)
    rows = pl.ds(r0, M_CHUNK)
    return rows, hs, q_ref[rows, hs], k_ref[rows, hs], v_ref[rows, hs], gc_ref[rows, :], gr_ref[c]


def _mlstm_chunks(chains, ms, loaded, c_scr, n_scr):
    L = M_CHUNK
    n = range(len(chains))
    t_idx = lax.broadcasted_iota(I32, (L, L), 0)
    s_idx = lax.broadcasted_iota(I32, (L, L), 1)
    masks = {0: (s_idx <= t_idx, t_idx <= s_idx), 1: (s_idx >= t_idx, t_idx >= s_idx)}
    q = [ld[2] for ld in loaded]
    k = [ld[3] for ld in loaded]
    v = [ld[4] for ld in loaded]
    gi = [2 * d * M_HEADS + hd for hd, d in chains]
    gf = [(2 * d + 1) * M_HEADS + hd for hd, d in chains]
    i_col = [ld[5][:, gi[i]:gi[i] + 1] for i, ld in enumerate(loaded)]
    lf_col = [ld[5][:, gf[i]:gf[i] + 1] for i, ld in enumerate(loaded)]
    i_row = [ld[6][gi[i]:gi[i] + 1, :] for i, ld in enumerate(loaded)]
    lf_row = [ld[6][gf[i]:gf[i] + 1, :] for i, ld in enumerate(loaded)]
    mask = [masks[d][0] for _, d in chains]
    mask_t = [masks[d][1] for _, d in chains]
    b_col = [jnp.sum(jnp.where(mask[i], lf_row[i], 0.0), axis=1, keepdims=True) for i in n]
    b_row = [jnp.sum(jnp.where(mask_t[i], lf_col[i], 0.0), axis=0, keepdims=True) for i in n]
    log_d = [jnp.where(mask[i], b_col[i] - b_row[i] + i_row[i], -jnp.inf) for i in n]
    li = [b_col[i] + ms[i] for i in n]
    m_r = [jnp.maximum(li[i], jnp.max(log_d[i], axis=1, keepdims=True)) for i in n]
    a_int = [jnp.exp(li[i] - m_r[i]) for i in n]
    dmat = [jnp.exp(log_d[i] - m_r[i]) for i in n]
    cmat = [c_scr[d, hd] for hd, d in chains]
    nvec = [n_scr[d, hd] for hd, d in chains]
    gram = [lax.dot_general(q[i], k[i], (((1,), (1,)), ((), ())), preferred_element_type=F32) for i in n]
    inter = [jnp.dot(q[i], cmat[i].astype(BF16), preferred_element_type=F32) for i in n]
    s = [gram[i] * dmat[i] for i in n]
    intra = [jnp.dot(s[i].astype(BF16), v[i], preferred_element_type=F32) for i in n]
    qn = [jnp.sum(q[i].astype(F32) * nvec[i], axis=1, keepdims=True) for i in n]
    den = [a_int[i] * qn[i] + jnp.sum(s[i], axis=1, keepdims=True) for i in n]
    hh = [(a_int[i] * inter[i] + intra[i]) / jnp.maximum(jnp.abs(den[i]), jnp.exp(-m_r[i])) for i in n]
    b_last = [b_row[i][:, L - 1:L] if chains[i][1] == 0 else b_row[i][:, 0:1] for i in n]
    log_w = [b_last[i] - b_col[i] + i_col[i] for i in n]
    m_new = [jnp.maximum(b_last[i] + ms[i], jnp.max(log_w[i], axis=0, keepdims=True)) for i in n]
    w = [jnp.exp(log_w[i] - m_new[i]) for i in n]
    decay = [jnp.exp(b_last[i] + ms[i] - m_new[i]) for i in n]
    kw = [k[i].astype(F32) * w[i] for i in n]
    kv = [lax.dot_general(kw[i].astype(BF16), v[i], (((0,), (0,)), ((), ())), preferred_element_type=F32) for i in n]
    for i, (hd, d) in enumerate(chains):
        c_scr[d, hd] = decay[i] * cmat[i] + kv[i]
        n_scr[d, hd] = decay[i] * nvec[i] + jnp.sum(kw[i], axis=0, keepdims=True)
    return hh, m_new


SCAN_GROUP = 2 * M_HEADS


def _mlstm_scan_body(n_chunks, q_ref, k_ref, v_ref, gc_ref, gr_ref, h_ref, hb_scr, c_scr, n_scr, m0):
    chains = [(hd, d) for hd in range(M_HEADS) for d in range(2)]

    def body(c, ms):
        out = []
        for g0 in range(0, len(chains), SCAN_GROUP):
            grp = chains[g0:g0 + SCAN_GROUP]
            loaded = [_mlstm_load(hd, c if d == 0 else n_chunks - 1 - c, q_ref, k_ref, v_ref, gc_ref, gr_ref)
                      for hd, d in grp]
            hh, m_new = _mlstm_chunks(grp, ms[g0:g0 + SCAN_GROUP], loaded, c_scr, n_scr)
            for (hd, d), ld, h in zip(grp, loaded, hh):
                dst = h_ref if d == 0 else hb_scr
                dst[ld[0], ld[1]] = h
            out += m_new
        return tuple(out)

    ms = lax.fori_loop(0, n_chunks, body, tuple(m0))
    h_ref[...] += hb_scr[...]
    return ms


def _mlstm_scan_ctx_kernel(q_ref, k_ref, v_ref, gc_ref, gr_ref, h_ref, cn_ref, nn_ref, mn_ref, hb_scr, c_scr, n_scr):
    c_scr[...] = jnp.zeros(c_scr.shape, F32)
    n_scr[...] = jnp.zeros(n_scr.shape, F32)
    zero = jnp.zeros((1, 1), F32)
    ms = _mlstm_scan_body(SEQ // M_CHUNK, q_ref, k_ref, v_ref, gc_ref, gr_ref, h_ref, hb_scr, c_scr, n_scr,
                          [zero] * (2 * M_HEADS))
    cn_ref[0] = c_scr[...]
    nn_ref[0] = n_scr[...]
    for hd in range(M_HEADS):
        for d in range(2):
            mn_ref[0, d, hd] = jnp.broadcast_to(ms[2 * hd + d], (1, LANE))


def _mlstm_scan_lat_kernel(q_ref, k_ref, v_ref, gc_ref, gr_ref, c0_ref, n0_ref, m0_ref, ctx_out_ref, h_ref,
                           hb_scr, c_scr, n_scr):
    del ctx_out_ref
    c_scr[...] = c0_ref[0]
    n_scr[...] = n0_ref[0]
    m0 = [m0_ref[0, d, hd] for hd in range(M_HEADS) for d in range(2)]
    _mlstm_scan_body(DEC_SEQ // M_CHUNK, q_ref, k_ref, v_ref, gc_ref, gr_ref, h_ref, hb_scr, c_scr, n_scr, m0)


def _mlstm_scan(q, k, v, gcol, grow, state_c, state_n, state_m):
    hd = M_HEAD_DIM
    ng = 4 * M_HEADS
    state_scratch = [pltpu.VMEM((2, M_HEADS, hd, hd), F32), pltpu.VMEM((2, M_HEADS, 1, hd), F32)]
    ncp = SEQ // M_CHUNK
    h_ctx, new_c, new_n, new_m = pl.pallas_call(
        _mlstm_scan_ctx_kernel,
        out_shape=(jax.ShapeDtypeStruct((N_TOK, D), F32),
                   jax.ShapeDtypeStruct((BATCH, 2, M_HEADS, hd, hd), F32),
                   jax.ShapeDtypeStruct((BATCH, 2, M_HEADS, 1, hd), F32),
                   jax.ShapeDtypeStruct((BATCH, 2, M_HEADS, 1, LANE), F32)),
        grid=(BATCH,),
        in_specs=[
            pl.BlockSpec((SEQ, D), lambda s: (s, 0)),
            pl.BlockSpec((SEQ, D), lambda s: (s, 0)),
            pl.BlockSpec((SEQ, D), lambda s: (s, 0)),
            pl.BlockSpec((SEQ, LANE), lambda s: (s, 0)),
            pl.BlockSpec((ncp, ng, M_CHUNK), lambda s: (s, 0, 0)),
        ],
        out_specs=(
            pl.BlockSpec((SEQ, D), lambda s: (s, 0)),
            pl.BlockSpec((1, 2, M_HEADS, hd, hd), lambda s: (s, 0, 0, 0, 0)),
            pl.BlockSpec((1, 2, M_HEADS, 1, hd), lambda s: (s, 0, 0, 0, 0)),
            pl.BlockSpec((1, 2, M_HEADS, 1, LANE), lambda s: (s, 0, 0, 0, 0)),
        ),
        scratch_shapes=[pltpu.VMEM((SEQ, D), F32)] + state_scratch,
        compiler_params=_cparams(),
        name="mlstm_scan_ctx",
    )(q, k, v, gcol, grow)
    ncl = DEC_SEQ // M_CHUNK
    pb = NP_TOK // DEC_SEQ
    h_all = pl.pallas_call(
        _mlstm_scan_lat_kernel,
        out_shape=jax.ShapeDtypeStruct((N_TOK, D), F32),
        input_output_aliases={8: 0},
        grid=(DEC_BATCH,),
        in_specs=[
            pl.BlockSpec((DEC_SEQ, D), lambda b: (pb + b, 0)),
            pl.BlockSpec((DEC_SEQ, D), lambda b: (pb + b, 0)),
            pl.BlockSpec((DEC_SEQ, D), lambda b: (pb + b, 0)),
            pl.BlockSpec((DEC_SEQ, LANE), lambda b: (pb + b, 0)),
            pl.BlockSpec((ncl, ng, M_CHUNK), lambda b: (pb + b, 0, 0)),
            pl.BlockSpec((1, 2, M_HEADS, hd, hd), lambda b: (b, 0, 0, 0, 0)),
            pl.BlockSpec((1, 2, M_HEADS, 1, hd), lambda b: (b, 0, 0, 0, 0)),
            pl.BlockSpec((1, 2, M_HEADS, 1, 1), lambda b: (b, 0, 0, 0, 0)),
            pl.BlockSpec(memory_space=pl.ANY),
        ],
        out_specs=pl.BlockSpec((DEC_SEQ, D), lambda b: (pb + b, 0)),
        scratch_shapes=[pltpu.VMEM((DEC_SEQ, D), F32)] + state_scratch,
        compiler_params=_cparams(),
        name="mlstm_scan_lat",
    )(q, k, v, gcol, grow, state_c, state_n, state_m, h_ctx)
    return h_all, new_c, new_n, new_m


def _mlstm_out_kernel(mr_ref, h_ref, o_ref, ng_ref, wf_ref, y_ref, mod_ref, out_ref, x_scr, w_ref):
    @pl.when(pl.program_id(0) == 0)
    def _():
        w_ref[...] = wf_ref[0].astype(BF16)

    hc = o_ref[...] * h_ref[...]
    for hd in range(M_HEADS):
        sl = slice(hd * M_HEAD_DIM, (hd + 1) * M_HEAD_DIM)
        x_scr[:, sl] = _rms(hc[:, sl], ng_ref[:, sl]).astype(BF16)
    out = jnp.dot(x_scr[...], w_ref[...], preferred_element_type=F32)
    out_ref[...] = y_ref[...] + _mod_vec(mod_ref, 2) * out


def _mlstm_out(hsum, o, norm_g, w_out_all, slot, y, mods, mrow):
    return pl.pallas_call(
        _mlstm_out_kernel,
        out_shape=jax.ShapeDtypeStruct((N_TOK, D), F32),
        grid_spec=pltpu.PrefetchScalarGridSpec(
            num_scalar_prefetch=1, grid=(NL,),
            in_specs=[_tok_spec(D, TL), _tok_spec(D, TL), _full_spec((1, D)), _resident_weight_spec(w_out_all, slot),
                      _tok_spec(D, TL),
                      _mod_spec()],
            out_specs=_tok_spec(D, TL),
            scratch_shapes=[pltpu.VMEM((TL, D), BF16), pltpu.VMEM((D, D), BF16)]),
        compiler_params=_cparams(),
        name="mlstm_out",
    )(mrow, hsum, o, norm_g, w_out_all, y, mods)


ROUTE_OFF = N_GROUPS
SLAB = D // (2 * LANE)
V7X_SC_CORES = 2
V7X_SC_SUBCORES = 16
SC_WORKERS = V7X_SC_CORES * V7X_SC_SUBCORES
SC_WINDOW = 128
SC_HALF = SC_WINDOW // 2
HI_MASK = -65536


def _bf16_bits(x):
    return lax.bitcast_convert_type(x.astype(BF16).astype(F32), I32)


def _store_slabs(ref, x):
    rows = x.shape[0]
    for c in range(SLAB):
        lo = lax.shift_right_logical(_bf16_bits(x[:, (2 * c) * LANE:(2 * c + 1) * LANE]), 16)
        hi = _bf16_bits(x[:, (2 * c + 1) * LANE:(2 * c + 2) * LANE]) & HI_MASK
        ref[pl.ds(c, rows, stride=SLAB), :] = lo | hi


def _load_slabs(ref, dst, rows, dtype):
    for c in range(SLAB):
        w = ref[pl.ds(c, rows, stride=SLAB), :]
        lo = lax.bitcast_convert_type(lax.shift_left(w, 16), F32)
        hi = lax.bitcast_convert_type(w & HI_MASK, F32)
        dst[:, (2 * c) * LANE:(2 * c + 1) * LANE] = lo.astype(dtype)
        dst[:, (2 * c + 1) * LANE:(2 * c + 2) * LANE] = hi.astype(dtype)


def _route_kernel(mr_ref, y_ref, mod_ref, g_ref, wr_ref, br_ref, tri_ref, x_ref, wt_ref, dest_ref, cnt_ref,
                  cnt_scr, meta_scr):
    x = _norm_mod(y_ref[...], g_ref[...], mod_ref, 1)
    _store_slabs(x_ref, x)
    lg = jnp.dot(x.astype(BF16), wr_ref[...], preferred_element_type=F32) + br_ref[...]
    lane = lax.broadcasted_iota(I32, lg.shape, 1).astype(F32)
    ninf = -jnp.inf
    big = float(LANE)
    lgg = jnp.where(lane < N_GROUPS, lg, ninf)
    gmax = jnp.max(lgg, axis=-1, keepdims=True)
    g_idx = jnp.min(jnp.where(lgg == gmax, lane, big), axis=-1, keepdims=True)
    g_w = 1.0 / jnp.sum(jnp.exp(lgg - gmax), axis=-1, keepdims=True)
    lo = ROUTE_OFF + g_idx * EXPERTS_PER_GROUP
    le = jnp.where((lane >= lo) & (lane < lo + EXPERTS_PER_GROUP), lg, ninf)
    m1 = jnp.max(le, axis=-1, keepdims=True)
    i1 = jnp.min(jnp.where(le == m1, lane, big), axis=-1, keepdims=True)
    le2 = jnp.where(lane == i1, ninf, le)
    m2 = jnp.max(le2, axis=-1, keepdims=True)
    i2 = jnp.min(jnp.where(le2 == m2, lane, big), axis=-1, keepdims=True)
    r = jnp.exp(m2 - m1)
    p1 = 1.0 / (1.0 + r)
    p2 = r / (1.0 + r)
    two = lax.broadcasted_iota(I32, (x.shape[0], TOP_K), 1)
    wt_ref[...] = jnp.where(two == 0, g_w * p1, g_w * p2)
    @pl.when(pl.program_id(0) == 0)
    def _():
        cnt_scr[...] = jnp.zeros(cnt_scr.shape, F32)

    oh1 = (lane == i1).astype(F32)
    oh2 = (lane == i2).astype(F32)
    both = oh1 + oh2
    before = jnp.dot(tri_ref[...], both.astype(BF16), preferred_element_type=F32) + cnt_scr[...]
    rk1 = jnp.sum(oh1 * before, axis=-1, keepdims=True)
    rk2 = jnp.sum(oh2 * before, axis=-1, keepdims=True)
    cnt_scr[...] = cnt_scr[...] + jnp.sum(both, axis=0, keepdims=True)
    cnt_ref[...] = cnt_scr[...]
    cols = (i1 - ROUTE_OFF, i2 - ROUTE_OFF, rk1, rk2)
    packed = jnp.zeros(lg.shape, F32)
    for c, val in enumerate(cols):
        packed = jnp.where(lane == c, val, packed)
    j = pl.program_id(0)
    meta_scr[:, pl.ds(pl.multiple_of(j * TL, TL), TL)] = jnp.transpose(packed)[0:len(cols), :].astype(I32)

    @pl.when(j == NL - 1)
    def _():
        sub8 = (SUBLANE, LANE)
        ln = lax.broadcasted_iota(I32, sub8, 1)
        counts = jnp.broadcast_to(cnt_scr[...], sub8)
        is_e = (ln >= ROUTE_OFF) & (ln < ROUTE_OFF + N_EXPERTS)
        padded = jnp.where(is_e, jnp.floor((counts + (EBLK - 1)) * (1.0 / EBLK)) * EBLK, 0.0)
        inc = padded
        sh = 1
        while sh < LANE:
            inc = inc + jnp.where(ln >= sh, pltpu.roll(inc, sh, 1), 0.0)
            sh *= 2
        start = pltpu.roll(inc - padded, LANE - ROUTE_OFF, 1)
        table = jnp.transpose(jnp.broadcast_to(start[0:1, :], (LANE, LANE)))[0:N_EXPERTS, 0:1].astype(I32)
        sub = lax.broadcasted_iota(I32, (N_EXPERTS, SLOT_COLS), 0)
        for c0 in range(0, N_TOK, SLOT_COLS):
            meta = meta_scr[:, c0:c0 + SLOT_COLS]
            for k in range(TOP_K):
                first = jnp.sum(jnp.where(sub == meta[k:k + 1, :], table, 0), axis=0, keepdims=True)
                dest_ref[k:k + 1, c0:c0 + SLOT_COLS] = first + meta[TOP_K + k:TOP_K + k + 1, :]


SLOT_COLS = 2048


def _route(y, mods, mrow, g2, w_route, b_route):
    return pl.pallas_call(
        _route_kernel,
        out_shape=(jax.ShapeDtypeStruct((N_TOK * SLAB, LANE), I32), jax.ShapeDtypeStruct((N_TOK, TOP_K), F32),
                   jax.ShapeDtypeStruct((TOP_K, N_TOK), I32), jax.ShapeDtypeStruct((1, LANE), F32)),
        grid_spec=pltpu.PrefetchScalarGridSpec(
            num_scalar_prefetch=1, grid=(NL,),
            in_specs=[_tok_spec(D, TL), _mod_spec(), _full_spec((1, D)), _full_spec((D, LANE)),
                      _full_spec((1, LANE)), _full_spec((TL, TL))],
            out_specs=(pl.BlockSpec((TL * SLAB, LANE), lambda j, *_: (j, 0)), _tok_spec(TOP_K, TL),
                       _full_spec((TOP_K, N_TOK)), _full_spec((1, LANE))),
            scratch_shapes=[pltpu.VMEM((1, LANE), F32), pltpu.VMEM((2 * TOP_K, N_TOK), I32)]),
        compiler_params=_cparams(),
        name="moe_route",
    )(mrow, y, mods, g2, w_route, b_route, jnp.asarray(np.tril(np.ones((TL, TL), np.float32), -1), dtype=BF16))


def _block_tables_of(lane_counts):
    counts = lane_counts[0, ROUTE_OFF:ROUTE_OFF + N_EXPERTS].astype(I32)
    padded = ((counts + EBLK - 1) // EBLK) * EBLK
    pad_end = jnp.cumsum(padded)
    pad_start = pad_end - padded
    n_blk = (padded // EBLK).astype(I32)
    blk_start = (pad_start // EBLK).astype(I32)
    n_used = (pad_end[-1] // EBLK).astype(I32).reshape(1)
    return blk_start, n_blk, n_used


def _sc_mesh():
    return plsc.VectorSubcoreMesh(core_axis_name="core", subcore_axis_name="subcore",
                                  num_cores=V7X_SC_CORES, num_subcores=V7X_SC_SUBCORES)


def _sc_worker():
    return lax.axis_index("core") * V7X_SC_SUBCORES + lax.axis_index("subcore")


def _sc_dispatch(x_slabs, dest):
    per = N_TOK // SC_WORKERS
    n_win = per // SC_HALF

    @functools.partial(
        pl.kernel, out_type=jax.ShapeDtypeStruct((P_SLOTS, SLAB, LANE), I32), mesh=_sc_mesh(), name="moe_dispatch",
        scratch_types=[pltpu.VMEM((1, per), I32), pltpu.VMEM((1, per), I32), pltpu.VMEM((2, SC_HALF, SLAB, LANE), I32),
                       pltpu.SemaphoreType.DMA((2,)), pltpu.SemaphoreType.DMA((2,))])
    def run(x_hbm, d_hbm, o_hbm, i0_v, i1_v, buf, lsem, ssem):
        base = _sc_worker() * per
        pltpu.sync_copy(d_hbm.at[pl.ds(0, 1), pl.ds(base, per)], i0_v)
        pltpu.sync_copy(d_hbm.at[pl.ds(1, 1), pl.ds(base, per)], i1_v)
        loads = [pltpu.make_async_copy(x_hbm.at[pl.ds(base + s * SC_HALF, SC_HALF)], buf.at[s % 2], lsem.at[s % 2])
                 for s in range(n_win)]
        loads[0].start()
        for s in range(n_win):
            loads[s].wait()
            if s + 1 < n_win:
                loads[s + 1].start()
            win = pl.ds(s * SC_HALF, SC_HALF)
            outs = [pltpu.make_async_copy(buf.at[s % 2], o_hbm.at[iv.at[0, win]], ssem.at[a])
                    for a, iv in enumerate((i0_v, i1_v))]
            for cp in outs:
                cp.start()
            for cp in outs:
                cp.wait()

    return run(x_slabs.reshape(N_TOK, SLAB, LANE), dest)


def _sc_collect(y_slabs, dest):
    per = N_ASSIGN // SC_WORKERS
    n_win = per // SC_HALF

    @functools.partial(
        pl.kernel, out_type=jax.ShapeDtypeStruct((N_ASSIGN, SLAB, LANE), I32), mesh=_sc_mesh(), name="moe_collect",
        scratch_types=[pltpu.VMEM((1, per), I32), pltpu.VMEM((2, SC_HALF, SLAB, LANE), I32),
                       pltpu.SemaphoreType.DMA((2,)), pltpu.SemaphoreType.DMA((2,))])
    def run(y_hbm, i_hbm, o_hbm, i_v, buf, gsem, wsem):
        w = _sc_worker()
        base = w * per
        per_a = SC_WORKERS // TOP_K
        pltpu.sync_copy(i_hbm.at[pl.ds(w // per_a, 1), pl.ds((w % per_a) * per, per)], i_v)
        gathers = [pltpu.make_async_copy(y_hbm.at[i_v.at[0, pl.ds(s * SC_HALF, SC_HALF)]], buf.at[s % 2],
                                         gsem.at[s % 2]) for s in range(n_win)]
        writes = [pltpu.make_async_copy(buf.at[s % 2], o_hbm.at[pl.ds(base + s * SC_HALF, SC_HALF)], wsem.at[s % 2])
                  for s in range(n_win)]
        gathers[0].start()
        for s in range(n_win):
            gathers[s].wait()
            if s >= 1:
                writes[s - 1].wait()
            if s + 1 < n_win:
                gathers[s + 1].start()
            writes[s].start()
        writes[n_win - 1].wait()

    return run(y_slabs.reshape(P_SLOTS, SLAB, LANE), dest)


EROWS = EBLK * SLAB


def _expert_kernel(bs_ref, nb_ref, nu_ref, wg_ref, wu_ref, wd_ref, x_hbm, y_hbm,
                   xbuf, ybuf, xs, wg_bf, wu_bf, wd_bf, isem, osem):
    e = pl.program_id(0)
    n_exp = pl.num_programs(0)
    n_used = nu_ref[0]
    b0 = bs_ref[e]
    nb = nb_ref[e]

    def in_copy(g, slot):
        return pltpu.make_async_copy(x_hbm.at[pl.ds(pl.multiple_of(g * EROWS, EROWS), EROWS)], xbuf.at[slot],
                                     isem.at[slot])

    def out_copy(g, slot):
        return pltpu.make_async_copy(ybuf.at[slot], y_hbm.at[pl.ds(pl.multiple_of(g * EROWS, EROWS), EROWS)],
                                     osem.at[slot])

    @pl.when(e == 0)
    def _():
        in_copy(0, 0).start()

    @pl.when(nb > 0)
    def _():
        wg_bf[...] = wg_ref[0, 0].astype(BF16)
        wu_bf[...] = wu_ref[0, 0].astype(BF16)
        wd_bf[...] = wd_ref[0, 0].astype(BF16)

    def block(k, carry):
        g = b0 + k
        slot = lax.rem(g, 2)
        in_copy(g, slot).wait()

        @pl.when(g + 1 < n_used)
        def _():
            in_copy(g + 1, 1 - slot).start()

        _load_slabs(xbuf.at[slot], xs, EBLK, BF16)
        xb = xs[...]
        gt = jnp.dot(xb, wg_bf[...], preferred_element_type=F32)
        up = jnp.dot(xb, wu_bf[...], preferred_element_type=F32)
        hmid = (gt * _sigmoid(gt) * up).astype(BF16)
        res = jnp.dot(hmid, wd_bf[...], preferred_element_type=F32)

        @pl.when(g >= 2)
        def _():
            out_copy(g - 2, slot).wait()

        _store_slabs(ybuf.at[slot], res)
        out_copy(g, slot).start()
        return carry

    lax.fori_loop(0, nb, block, 0)

    @pl.when(e == n_exp - 1)
    def _():
        last = n_used - 1
        out_copy(last, lax.rem(last, 2)).wait()

        @pl.when(n_used >= 2)
        def _():
            out_copy(last - 1, lax.rem(last - 1, 2)).wait()


def _experts(x_sorted, blk_start, n_blk, n_used, w_gate, w_up, w_down, layer):
    any_spec = pl.BlockSpec(memory_space=pl.ANY)
    wspec = lambda r, c: pl.BlockSpec((1, 1, r, c), lambda e, *_: (layer, e, 0, 0))
    return pl.pallas_call(
        _expert_kernel,
        out_shape=jax.ShapeDtypeStruct((P_SLOTS * SLAB, LANE), I32),
        grid_spec=pltpu.PrefetchScalarGridSpec(
            num_scalar_prefetch=3, grid=(N_EXPERTS,),
            in_specs=[wspec(D, D_EXPERT), wspec(D, D_EXPERT), wspec(D_EXPERT, D), any_spec],
            out_specs=any_spec,
            scratch_shapes=[
                pltpu.VMEM((2, EROWS, LANE), I32), pltpu.VMEM((2, EROWS, LANE), I32),
                pltpu.VMEM((EBLK, D), BF16),
                pltpu.VMEM((D, D_EXPERT), BF16), pltpu.VMEM((D, D_EXPERT), BF16), pltpu.VMEM((D_EXPERT, D), BF16),
                pltpu.SemaphoreType.DMA((2,)), pltpu.SemaphoreType.DMA((2,)),
            ]),
        compiler_params=_cparams(),
        name="moe_experts",
    )(blk_start, n_blk, n_used, w_gate, w_up, w_down, x_sorted.reshape(P_SLOTS * SLAB, LANE))


def _combine_kernel(final, mr_ref, e0_ref, e1_ref, wt_ref, y_ref, mod_ref, fg_ref, o_ref, a_scr, b_scr):
    _load_slabs(e0_ref, a_scr, TL, F32)
    _load_slabs(e1_ref, b_scr, TL, F32)
    wt = wt_ref[...]
    moe = wt[:, 0:1] * a_scr[...] + wt[:, 1:2] * b_scr[...]
    y_new = y_ref[...] + _mod_vec(mod_ref, 5) * moe
    o_ref[...] = _rms(y_new, fg_ref[...]) if final else y_new


def _combine(ym, wts, y, mods, mrow, final_g, blk0, nblk, final):
    tok = lambda width: pl.BlockSpec((TL, width), lambda j, *_: (blk0 + j, 0))
    slab0 = pl.BlockSpec((TL * SLAB, LANE), lambda j, *_: (blk0 + j, 0))
    slab1 = pl.BlockSpec((TL * SLAB, LANE), lambda j, *_: (NL + blk0 + j, 0))
    mod = pl.BlockSpec((1, 1, 6 * D), lambda j, mr: (mr[blk0 + j], 0, 0))
    return pl.pallas_call(
        functools.partial(_combine_kernel, final),
        out_shape=jax.ShapeDtypeStruct((nblk * TL, D), F32),
        grid_spec=pltpu.PrefetchScalarGridSpec(
            num_scalar_prefetch=1, grid=(nblk,),
            in_specs=[slab0, slab1, tok(TOP_K), tok(D), mod, _full_spec((1, D))],
            out_specs=pl.BlockSpec((TL, D), lambda j, *_: (j, 0)),
            scratch_shapes=[pltpu.VMEM((TL, D), F32), pltpu.VMEM((TL, D), F32)]),
        compiler_params=_cparams(),
        name="moe_combine",
    )(mrow, ym, ym, wts, y, mods, final_g)


def kernel(x_prompt, x_sample, cache_attn_k, cache_attn_v, state_mlstm_C, state_mlstm_n, state_mlstm_m, c, c_ctx, ada_w, ada_b, norm1_g, norm2_g, conv_w_in, conv_w_dw, conv_b_dw, conv_ln_g, conv_ln_b, conv_w_out, attn_w_qkv, attn_q_norm, attn_k_norm, attn_w_o, mlstm_w_in, mlstm_b_gate, mlstm_norm_g, mlstm_w_out, moe_w_group, moe_b_group, moe_w_router, moe_b_router, moe_w_gate, moe_w_up, moe_w_down, final_norm_g):
    y = None
    cvec = jnp.concatenate([c_ctx[None, :], c, jnp.zeros((MOD_ROWS - 1 - DEC_BATCH, D), F32)], axis=0)
    rope = _rope_blocks()
    mrow, mrow_sb, mrow_l = jnp.asarray(_MOD_ROW), jnp.asarray(_MOD_ROW_SB), jnp.asarray(_MOD_ROW_L)
    new_k = new_v = new_c = new_n = new_m = None
    for i in range(DEPTH):
        kind, slot = i % 3, i // 3
        mods = _ada_layer(cvec, ada_w, ada_b, i)
        g1 = norm1_g[i].reshape(1, D)
        if kind == 0:
            src = (x_prompt.reshape(NP_TOK, D), x_sample.reshape(NS_TOK, D), True) if i == 0 else (y, y, False)
            u = _conv_in(*src, mods, mrow_l, g1, conv_w_in, slot)
            w_dw = jnp.concatenate([conv_w_dw[slot], jnp.zeros((1, D), F32)], axis=0)
            y = _conv_main(u, *src, mods, mrow_sb, w_dw, conv_b_dw[slot].reshape(1, D), conv_ln_g[slot].reshape(1, D),
                           conv_ln_b[slot].reshape(1, D), conv_w_out, slot)
        elif kind == 1:
            q, kb, vb, kf, vf = _attn_qkv(y, mods, mrow_sb, g1, attn_w_qkv, slot,
                                          attn_q_norm[slot].reshape(1, HEAD_DIM), attn_k_norm[slot].reshape(1, HEAD_DIM),
                                          rope)
            new_k = kf[:NP_TOK].reshape(BATCH, 1, SEQ, N_KV_HEADS, HEAD_DIM)
            new_v = vf[:NP_TOK].reshape(BATCH, 1, SEQ, N_KV_HEADS, HEAD_DIM)
            ck = cache_attn_k[:, slot].reshape(DEC_BATCH, PAST_LEN, KV_DIM)
            cv = cache_attn_v[:, slot].reshape(DEC_BATCH, PAST_LEN, KV_DIM)
            y = _attention(q, kb, vb, ck, cv, attn_w_o[slot].astype(BF16), y, mods)
        else:
            b_gate = jnp.concatenate([mlstm_b_gate[slot], jnp.zeros((LANE - 4 * M_HEADS,), F32)]).reshape(1, LANE)
            q, k, v, o, gates, grow = _mlstm_in(y, mods, mrow, g1, mlstm_w_in, slot, b_gate)
            sc = state_mlstm_C[:, slot]
            sn = state_mlstm_n[:, slot].reshape(DEC_BATCH, 2, M_HEADS, 1, M_HEAD_DIM)
            sm = state_mlstm_m[:, slot].reshape(DEC_BATCH, 2, M_HEADS, 1, 1)
            hsum, nc_, nn_, nm_ = _mlstm_scan(q, k, v, gates, grow, sc, sn, sm)
            new_c = nc_[:, None]
            new_n = nn_.reshape(BATCH, 1, 2, M_HEADS, M_HEAD_DIM)
            new_m = nm_[..., 0, 0].reshape(BATCH, 1, 2, M_HEADS)
            y = _mlstm_out(hsum, o, mlstm_norm_g[slot].reshape(1, D), mlstm_w_out, slot, y, mods, mrow_l)
        w_route = jnp.concatenate([moe_w_group[i], moe_w_router[i],
                                   jnp.zeros((D, LANE - N_GROUPS - N_EXPERTS), F32)], axis=1)
        b_route = jnp.concatenate([moe_b_group[i], moe_b_router[i],
                                   jnp.zeros((LANE - N_GROUPS - N_EXPERTS,), F32)]).reshape(1, LANE)
        x2, ewt, dest, cnt = _route(y, mods, mrow_l, norm2_g[i].reshape(1, D), w_route.astype(BF16), b_route)
        blk_start, n_blk, n_used = _block_tables_of(cnt)
        x_sorted = _sc_dispatch(x2, dest)
        y_sorted = _experts(x_sorted, blk_start, n_blk, n_used, moe_w_gate, moe_w_up, moe_w_down, i)
        ym = _sc_collect(y_sorted, dest)
        ym = ym.reshape(N_ASSIGN * SLAB, LANE)
        fg = final_norm_g.reshape(1, D)
        if i + 1 < DEPTH:
            y = _combine(ym, ewt, y, mods, mrow_l, fg, 0, NL, False)
        else:
            y_prompt = _combine(ym, ewt, y, mods, mrow_l, fg, 0, NLP, True).reshape(BATCH, SEQ, D)
            y_sample = _combine(ym, ewt, y, mods, mrow_l, fg, NLP, NL - NLP, True).reshape(DEC_BATCH, DEC_SEQ, D)
    return (y_prompt, y_sample, new_k, new_v, new_c, new_n, new_m)
```

```python
import functools

import jax
import jax.numpy as jnp
import numpy as np
from jax import lax
from jax.experimental import pallas as pl
from jax.experimental.pallas import tpu as pltpu
from jax.experimental.pallas import tpu_sc as plsc

F32 = jnp.float32
BF16 = jnp.bfloat16
I32 = jnp.int32

D = 1024
BATCH, SEQ = 16, 256
DEC_BATCH, DEC_SEQ = 8, 1024
PAST_LEN = 256
DEPTH = 4
GRID_W = 64
EPS = 1e-6
CONV_WIDTH = 31
CONV_PAD = CONV_WIDTH // 2
HEAD_DIM = 128
N_HEADS = 8
N_KV_HEADS = 2
GQA_GROUP = N_HEADS // N_KV_HEADS
Q_DIM = N_HEADS * HEAD_DIM
KV_DIM = N_KV_HEADS * HEAD_DIM
QKV_DIM = Q_DIM + 2 * KV_DIM
ROPE_THETA = 10000.0
M_HEADS = 4
M_HEAD_DIM = D // M_HEADS
M_CHUNK = 64
N_GROUPS = 4
EXPERTS_PER_GROUP = 8
N_EXPERTS = N_GROUPS * EXPERTS_PER_GROUP
TOP_K = 2
D_EXPERT = 512

NP_TOK = BATCH * SEQ
NS_TOK = DEC_BATCH * DEC_SEQ
N_TOK = NP_TOK + NS_TOK
TM = 512
NB = N_TOK // TM
NBP = NP_TOK // TM
BLK_PER_DEC = DEC_SEQ // TM
TL = 1024
NL = N_TOK // TL
NLP = NP_TOK // TL
SB = 256
NSB = N_TOK // SB
NSBP = NP_TOK // SB
SB_PER_DEC = DEC_SEQ // SB
MOD_ROWS = 16
HALO = 16
LANE = 128
SUBLANE = 8

N_ASSIGN = N_TOK * TOP_K
EBLK = 1024
N_EBLK = N_ASSIGN // EBLK + N_EXPERTS
P_SLOTS = N_EBLK * EBLK
N_PAD_SLOTS = P_SLOTS - N_ASSIGN

VMEM_LIMIT = 56 * 1024 * 1024


def _block_tables(nb, nbp, per_dec):
    j = np.arange(nb)
    is_p = j < nbp
    mod_row = np.where(is_p, 0, 1 + (j - nbp) // per_dec)
    rope_idx = np.where(is_p, 0, 1 + (j - nbp) % per_dec)
    first = np.where(is_p, 1, ((j - nbp) % per_dec == 0).astype(np.int64))
    last = np.where(is_p, 1, ((j - nbp) % per_dec == per_dec - 1).astype(np.int64))
    return (mod_row.astype(np.int32), rope_idx.astype(np.int32), first.astype(np.int32), last.astype(np.int32))


_MOD_ROW, _, _, _ = _block_tables(NB, NBP, BLK_PER_DEC)
_MOD_ROW_L, _, _, _ = _block_tables(NL, NLP, DEC_SEQ // TL)
_MOD_ROW_SB, _ROPE_IDX_SB, _SEQ_FIRST, _SEQ_LAST = _block_tables(NSB, NSBP, SB_PER_DEC)


def _cparams(n_axes=1):
    return pltpu.CompilerParams(dimension_semantics=("arbitrary",) * n_axes, vmem_limit_bytes=VMEM_LIMIT)


def _sigmoid(x):
    return 1.0 / (1.0 + jnp.exp(-x))


def _rms(x, g):
    return x * lax.rsqrt(jnp.mean(x * x, axis=-1, keepdims=True) + EPS) * g


def _mod_vec(mod_ref, k):
    return mod_ref[0, :, k * D:(k + 1) * D]


def _norm_mod(y, g, mod_ref, which):
    return _rms(y, g) * (1.0 + _mod_vec(mod_ref, 3 * which + 1)) + _mod_vec(mod_ref, 3 * which)


def _ada_kernel(c_ref, w_ref, b_ref, o_ref):
    c = c_ref[...]
    s = c * _sigmoid(c)
    res = jnp.dot(s.astype(BF16), w_ref[0].astype(BF16), preferred_element_type=F32) + b_ref[0]
    for r in range(MOD_ROWS):
        o_ref[r] = res[r:r + 1, :]


def _ada_layer(cvec, ada_w, ada_b, layer):
    tn = 1536
    return pl.pallas_call(
        _ada_kernel,
        out_shape=jax.ShapeDtypeStruct((MOD_ROWS, 1, 6 * D), F32),
        grid=(6 * D // tn,),
        in_specs=[
            pl.BlockSpec((MOD_ROWS, D), lambda n: (0, 0)),
            pl.BlockSpec((1, D, tn), lambda n: (layer, 0, n)),
            pl.BlockSpec((1, 1, tn), lambda n: (layer, 0, n)),
        ],
        out_specs=pl.BlockSpec((MOD_ROWS, 1, tn), lambda n: (0, 0, n)),
        compiler_params=_cparams(1),
        name="ada_mod",
    )(cvec, ada_w, ada_b.reshape(DEPTH, 1, 6 * D))


def _tok_spec(width, rows=TM):
    return pl.BlockSpec((rows, width), lambda j, *_: (j, 0))


def _mod_spec():
    return pl.BlockSpec((1, 1, 6 * D), lambda j, mr, *_: (mr[j], 0, 0))


def _full_spec(shape):
    nd = len(shape)
    return pl.BlockSpec(shape, lambda j, *_: (0,) * nd)


def _pair_specs(rows, nbp, split):
    s_off = nbp if split else 0
    return [pl.BlockSpec((rows, D), lambda j, *_: (jnp.minimum(j, nbp - 1), 0)),
            pl.BlockSpec((rows, D), lambda j, *_: (jnp.maximum(j, nbp) - s_off, 0))]


def _pair_block(nbp, yp_ref, ys_ref):
    return jnp.where(pl.program_id(0) < nbp, yp_ref[...], ys_ref[...])


def _resident_weight_spec(w_all, slot):
    return pl.BlockSpec((1,) + w_all.shape[1:], lambda j, *_: (slot, 0, 0), pipeline_mode=pl.Buffered(1))


def _conv_in_kernel(mr_ref, yp_ref, ys_ref, mod_ref, g_ref, wf_ref, u_ref, w_ref):
    @pl.when(pl.program_id(0) == 0)
    def _():
        w_ref[...] = wf_ref[0].astype(BF16)

    h = _norm_mod(_pair_block(NLP, yp_ref, ys_ref), g_ref[...], mod_ref, 0)
    ag = jnp.dot(h.astype(BF16), w_ref[...], preferred_element_type=F32)
    u_ref[...] = ag[:, :D] * _sigmoid(ag[:, D:])


def _conv_in(yp, ys, split, mods, mrow, g1, w_in_all, slot):
    return pl.pallas_call(
        _conv_in_kernel,
        out_shape=jax.ShapeDtypeStruct((N_TOK, D), F32),
        grid_spec=pltpu.PrefetchScalarGridSpec(
            num_scalar_prefetch=1, grid=(NL,),
            in_specs=_pair_specs(TL, NLP, split) + [_mod_spec(), _full_spec((1, D)),
                                                    _resident_weight_spec(w_in_all, slot)],
            out_specs=_tok_spec(D, TL),
            scratch_shapes=[pltpu.VMEM((D, 2 * D), BF16)]),
        compiler_params=_cparams(),
        name="conv_in",
    )(mrow, yp, ys, mods, g1, w_in_all)


def _conv_main_kernel(mr_ref, first_ref, last_ref, u_ref, up_ref, un_ref, wdw_ref, bdw_ref, lg_ref, lb_ref,
                      woutf_ref, yp_ref, ys_ref, mod_ref, o_ref, ext_ref, acc_ref, wout_ref):
    j = pl.program_id(0)

    @pl.when(j == 0)
    def _():
        wout_ref[...] = woutf_ref[0].astype(BF16)

    zero = jnp.zeros((HALO, D), F32)
    ext_ref[0:HALO, :] = jnp.where(first_ref[j] == 1, zero, up_ref[...])
    ext_ref[HALO:HALO + SB, :] = u_ref[...]
    ext_ref[HALO + SB:2 * HALO + SB, :] = jnp.where(last_ref[j] == 1, zero, un_ref[...])

    off0 = HALO - CONV_PAD
    n_a = (off0 + CONV_WIDTH - 1) // SUBLANE + 1
    n_chunks = SB // SUBLANE

    def strip(ci, carry):
        cs = pl.ds(pl.multiple_of(ci * LANE, LANE), LANE)
        wk = [jnp.broadcast_to(wdw_ref[k:k + 1, cs], (SUBLANE, LANE)) for k in range(CONV_WIDTH)]
        bias = jnp.broadcast_to(bdw_ref[:, cs], (SUBLANE, LANE))
        sub = lax.broadcasted_iota(I32, (SUBLANE, LANE), 0)
        prev_rot, prev_v0 = None, None
        for j in range(n_chunks + 1):
            tiles = [ext_ref[SUBLANE * (j + a):SUBLANE * (j + a + 1), cs] for a in range(n_a)]
            part = []
            for s in range(SUBLANE):
                acc = None
                for a in range(n_a):
                    k = SUBLANE * a + s - off0
                    if (0 <= k < CONV_WIDTH) and not (s == 0 and j == n_chunks):
                        term = tiles[a] * wk[k]
                        acc = term if acc is None else acc + term
                part.append(acc)
            rot = [None] + [pltpu.roll(part[s], SUBLANE - s, 0) for s in range(1, SUBLANE)]
            if j >= 1:
                out = prev_v0 + bias
                for s in range(1, SUBLANE):
                    out = out + jnp.where(sub < SUBLANE - s, prev_rot[s], rot[s])
                acc_ref[SUBLANE * (j - 1):SUBLANE * j, cs] = out
            prev_rot, prev_v0 = rot, part[0]
        return carry

    lax.fori_loop(0, D // LANE, strip, 0)

    c = acc_ref[...]
    mu = jnp.mean(c, axis=-1, keepdims=True)
    cc = c - mu
    var = jnp.mean(cc * cc, axis=-1, keepdims=True)
    z = cc * lax.rsqrt(var + EPS) * lg_ref[...] + lb_ref[...]
    z = z * _sigmoid(z)
    out = jnp.dot(z.astype(BF16), wout_ref[...], preferred_element_type=F32)
    o_ref[...] = _pair_block(NSBP, yp_ref, ys_ref) + _mod_vec(mod_ref, 2) * out


def _conv_main(u, yp, ys, split, mods, mrow, w_dw, b_dw, ln_g, ln_b, w_out_all, slot):
    nh = N_TOK // HALO
    per = SB // HALO
    sb_spec = pl.BlockSpec((SB, D), lambda j, *_: (j, 0))
    return pl.pallas_call(
        _conv_main_kernel,
        out_shape=jax.ShapeDtypeStruct((N_TOK, D), F32),
        grid_spec=pltpu.PrefetchScalarGridSpec(
            num_scalar_prefetch=3, grid=(NSB,),
            in_specs=[
                sb_spec,
                pl.BlockSpec((HALO, D), lambda j, *_: (jnp.maximum(j * per - 1, 0), 0)),
                pl.BlockSpec((HALO, D), lambda j, *_: (jnp.minimum((j + 1) * per, nh - 1), 0)),
                _full_spec((CONV_WIDTH + 1, D)), _full_spec((1, D)), _full_spec((1, D)), _full_spec((1, D)),
                _resident_weight_spec(w_out_all, slot), *_pair_specs(SB, NSBP, split), _mod_spec(),
            ],
            out_specs=sb_spec,
            scratch_shapes=[pltpu.VMEM((SB + 2 * HALO, D), F32), pltpu.VMEM((SB, D), F32), pltpu.VMEM((D, D), BF16)]),
        compiler_params=_cparams(),
        name="conv_main",
    )(mrow, jnp.asarray(_SEQ_FIRST), jnp.asarray(_SEQ_LAST), u, u, u, w_dw, b_dw, ln_g, ln_b, w_out_all, yp, ys, mods)


def _rope_angles():
    rows = DEC_SEQ // GRID_W
    row = jnp.repeat(jnp.arange(rows, dtype=F32), GRID_W)
    col = jnp.tile(jnp.arange(GRID_W, dtype=F32), rows)
    axis_dim = HEAD_DIM // 2
    freqs = jnp.power(ROPE_THETA, -jnp.arange(axis_dim // 2, dtype=F32) * 2.0 / axis_dim)
    ang_r = row[:, None] * freqs[None, :]
    ang_c = col[:, None] * freqs[None, :]
    return jnp.concatenate([ang_r, ang_r, ang_c, ang_c], axis=-1)


def _rope_blocks():
    ang = _rope_angles()
    cos, sin = jnp.cos(ang), jnp.sin(ang)
    lane = np.arange(HEAD_DIM)
    lo = jnp.asarray(((lane % (HEAD_DIM // 2)) < HEAD_DIM // 4).astype(np.float32))
    sin_a = -sin * lo[None, :]
    sin_b = sin * (1.0 - lo)[None, :]
    nblk = DEC_SEQ // SB
    ident = jnp.ones((1, SB, HEAD_DIM), F32)
    zeros = jnp.zeros((1, SB, HEAD_DIM), F32)
    cos_t = jnp.concatenate([ident, cos.reshape(nblk, SB, HEAD_DIM)], axis=0)
    sa_t = jnp.concatenate([zeros, sin_a.reshape(nblk, SB, HEAD_DIM)], axis=0)
    sb_t = jnp.concatenate([zeros, sin_b.reshape(nblk, SB, HEAD_DIM)], axis=0)
    return cos_t, sa_t, sb_t


def _attn_qkv_kernel(mr_ref, ri_ref, y_ref, mod_ref, g_ref, wf_ref, qg_ref, kg_ref, cos_ref, sa_ref, sb_ref,
                     q_ref, kb_ref, vb_ref, kf_ref, vf_ref, w_ref):
    @pl.when(pl.program_id(0) == 0)
    def _():
        w_ref[...] = wf_ref[0].astype(BF16)

    h = _norm_mod(y_ref[...], g_ref[...], mod_ref, 0)
    qkv = jnp.dot(h.astype(BF16), w_ref[...], preferred_element_type=F32)
    cos, sa, sb = cos_ref[0], sa_ref[0], sb_ref[0]
    quarter = HEAD_DIM // 4

    def head(x, g):
        xn = _rms(x, g)
        return xn * cos + pltpu.roll(xn, HEAD_DIM - quarter, 1) * sa + pltpu.roll(xn, quarter, 1) * sb

    scale = HEAD_DIM ** -0.5
    for hd in range(N_HEADS):
        sl = slice(hd * HEAD_DIM, (hd + 1) * HEAD_DIM)
        q_ref[:, sl] = (head(qkv[:, sl], qg_ref[...]) * scale).astype(BF16)
    for kv in range(N_KV_HEADS):
        sl = slice(kv * HEAD_DIM, (kv + 1) * HEAD_DIM)
        kr = head(qkv[:, Q_DIM + kv * HEAD_DIM:Q_DIM + (kv + 1) * HEAD_DIM], kg_ref[...])
        kf_ref[:, sl] = kr
        kb_ref[:, sl] = kr.astype(BF16)
    v = qkv[:, Q_DIM + KV_DIM:]
    vf_ref[...] = v
    vb_ref[...] = v.astype(BF16)


def _attn_qkv(y, mods, mrow, g1, w_qkv_all, slot, q_g, k_g, rope):
    cos_t, sa_t, sb_t = rope
    rspec = pl.BlockSpec((1, SB, HEAD_DIM), lambda j, mr, ri: (ri[j], 0, 0))
    return pl.pallas_call(
        _attn_qkv_kernel,
        out_shape=(jax.ShapeDtypeStruct((N_TOK, Q_DIM), BF16), jax.ShapeDtypeStruct((N_TOK, KV_DIM), BF16),
                   jax.ShapeDtypeStruct((N_TOK, KV_DIM), BF16), jax.ShapeDtypeStruct((N_TOK, KV_DIM), F32),
                   jax.ShapeDtypeStruct((N_TOK, KV_DIM), F32)),
        grid_spec=pltpu.PrefetchScalarGridSpec(
            num_scalar_prefetch=2, grid=(NSB,),
            in_specs=[_tok_spec(D, SB), _mod_spec(), _full_spec((1, D)), _resident_weight_spec(w_qkv_all, slot),
                      _full_spec((1, HEAD_DIM)), _full_spec((1, HEAD_DIM)), rspec, rspec, rspec],
            out_specs=(_tok_spec(Q_DIM, SB), _tok_spec(KV_DIM, SB), _tok_spec(KV_DIM, SB), _tok_spec(KV_DIM, SB),
                       _tok_spec(KV_DIM, SB)),
            scratch_shapes=[pltpu.VMEM((D, QKV_DIM), BF16)]),
        compiler_params=_cparams(),
        name="attn_qkv",
    )(mrow, jnp.asarray(_ROPE_IDX_SB), y, mods, g1, w_qkv_all, q_g, k_g, cos_t, sa_t, sb_t)


def _attn_heads(q, ks, vs, o_scr):
    nt = (((1,), (1,)), ((), ()))
    for hd in range(N_HEADS):
        g = hd // GQA_GROUP
        qh = q[:, hd * HEAD_DIM:(hd + 1) * HEAD_DIM]
        gs = slice(g * HEAD_DIM, (g + 1) * HEAD_DIM)
        ss = [lax.dot_general(qh, k[:, gs], nt, preferred_element_type=F32) for k in ks]
        m = functools.reduce(jnp.maximum, [jnp.max(s, axis=-1, keepdims=True) for s in ss])
        ps = [jnp.exp(s - m) for s in ss]
        l = functools.reduce(lambda a, b: a + b, [jnp.sum(p, axis=-1, keepdims=True) for p in ps])
        o = functools.reduce(lambda a, b: a + b,
                             [jnp.dot(p.astype(BF16), v[:, gs], preferred_element_type=F32) for p, v in zip(ps, vs)])
        o_scr[:, hd * HEAD_DIM:(hd + 1) * HEAD_DIM] = (o / l).astype(BF16)


def _attn_ctx_kernel(q_ref, k_ref, v_ref, wo_ref, y_ref, mod_ref, o_ref, o_scr):
    _attn_heads(q_ref[...], [k_ref[...]], [v_ref[...]], o_scr)
    out = jnp.dot(o_scr[...], wo_ref[...], preferred_element_type=F32)
    o_ref[...] = y_ref[...] + _mod_vec(mod_ref, 2) * out


def _attn_lat_kernel(q_ref, k_ref, v_ref, ck_ref, cv_ref, wo_ref, y_ref, mod_ref, ctx_out_ref, o_ref, o_scr):
    del ctx_out_ref
    _attn_heads(q_ref[...], [k_ref[...], ck_ref[0].astype(BF16)], [v_ref[...], cv_ref[0].astype(BF16)], o_scr)
    out = jnp.dot(o_scr[...], wo_ref[...], preferred_element_type=F32)
    o_ref[...] = y_ref[...] + _mod_vec(mod_ref, 2) * out


def _attention(q, kb, vb, cache_k, cache_v, w_o, y, mods):
    y_ctx = pl.pallas_call(
        _attn_ctx_kernel,
        out_shape=jax.ShapeDtypeStruct((N_TOK, D), F32),
        grid=(BATCH,),
        in_specs=[
            pl.BlockSpec((SEQ, Q_DIM), lambda s: (s, 0)),
            pl.BlockSpec((SEQ, KV_DIM), lambda s: (s, 0)),
            pl.BlockSpec((SEQ, KV_DIM), lambda s: (s, 0)),
            pl.BlockSpec((Q_DIM, D), lambda s: (0, 0)),
            pl.BlockSpec((SEQ, D), lambda s: (s, 0)),
            pl.BlockSpec((1, 1, 6 * D), lambda s: (0, 0, 0)),
        ],
        out_specs=pl.BlockSpec((SEQ, D), lambda s: (s, 0)),
        scratch_shapes=[pltpu.VMEM((SEQ, Q_DIM), BF16)],
        compiler_params=_cparams(),
        name="attn_ctx",
    )(q, kb, vb, w_o, y, mods)
    pb = NP_TOK // DEC_SEQ
    return pl.pallas_call(
        _attn_lat_kernel,
        out_shape=jax.ShapeDtypeStruct((N_TOK, D), F32),
        input_output_aliases={8: 0},
        grid=(DEC_BATCH, SB_PER_DEC),
        in_specs=[
            pl.BlockSpec((SB, Q_DIM), lambda b, t: (NSBP + b * SB_PER_DEC + t, 0)),
            pl.BlockSpec((DEC_SEQ, KV_DIM), lambda b, t: (pb + b, 0)),
            pl.BlockSpec((DEC_SEQ, KV_DIM), lambda b, t: (pb + b, 0)),
            pl.BlockSpec((1, PAST_LEN, KV_DIM), lambda b, t: (b, 0, 0)),
            pl.BlockSpec((1, PAST_LEN, KV_DIM), lambda b, t: (b, 0, 0)),
            pl.BlockSpec((Q_DIM, D), lambda b, t: (0, 0)),
            pl.BlockSpec((SB, D), lambda b, t: (NSBP + b * SB_PER_DEC + t, 0)),
            pl.BlockSpec((1, 1, 6 * D), lambda b, t: (1 + b, 0, 0)),
            pl.BlockSpec(memory_space=pl.ANY),
        ],
        out_specs=pl.BlockSpec((SB, D), lambda b, t: (NSBP + b * SB_PER_DEC + t, 0)),
        scratch_shapes=[pltpu.VMEM((SB, Q_DIM), BF16)],
        compiler_params=_cparams(2),
        name="attn_lat",
    )(q, kb, vb, cache_k, cache_v, w_o, y, mods, y_ctx)


def _log_sigmoid(x):
    return jnp.minimum(x, 0.0) - jnp.log(1.0 + jnp.exp(-jnp.abs(x)))


W_T = 256


def _mlstm_in_kernel(mr_ref, y_ref, mod_ref, g_ref, wt_ref, bg_ref, q_ref, k_ref, v_ref, o_ref, gt_ref, gr_ref,
                     w_ref, wg_ref):
    @pl.when(pl.program_id(0) == 0)
    def _():
        for r in range(0, 4 * D, W_T):
            w_ref[:, r:r + W_T] = jnp.transpose(wt_ref[0, r:r + W_T, :]).astype(BF16)
        n_out = 4 * D + 4 * M_HEADS
        tail = jnp.transpose(wt_ref[0, n_out - LANE:n_out, :])
        lane = lax.broadcasted_iota(I32, tail.shape, 1)
        wg_ref[...] = jnp.where(lane < 4 * M_HEADS, pltpu.roll(tail, 4 * M_HEADS, axis=1), 0.0).astype(BF16)

    h = _norm_mod(y_ref[...], g_ref[...], mod_ref, 0)
    hb = h.astype(BF16)
    q_ref[...] = jnp.dot(hb, w_ref[:, 0:D], preferred_element_type=F32).astype(BF16)
    k_ref[...] = (jnp.dot(hb, w_ref[:, D:2 * D], preferred_element_type=F32) * (M_HEAD_DIM ** -0.5)).astype(BF16)
    v_ref[...] = jnp.dot(hb, w_ref[:, 2 * D:3 * D], preferred_element_type=F32).astype(BF16)
    o_ref[...] = _sigmoid(jnp.dot(hb, w_ref[:, 3 * D:4 * D], preferred_element_type=F32))
    gt = jnp.dot(hb, wg_ref[...], preferred_element_type=F32) + bg_ref[...]
    lane = lax.broadcasted_iota(I32, gt.shape, 1)
    is_f = ((lane >= M_HEADS) & (lane < 2 * M_HEADS)) | ((lane >= 3 * M_HEADS) & (lane < 4 * M_HEADS))
    gt = jnp.where(is_f, _log_sigmoid(gt), gt)
    gt_ref[...] = gt
    per = LANE // M_CHUNK
    for p in range(TM // LANE):
        t = jnp.transpose(gt[p * LANE:(p + 1) * LANE, :])
        for f in range(per):
            gr_ref[p * per + f] = t[0:4 * M_HEADS, f * M_CHUNK:(f + 1) * M_CHUNK]


def _mlstm_in(y, mods, mrow, g1, w_in_all, slot, b_gate):
    w_t = jnp.swapaxes(w_in_all, 1, 2)
    w_spec = pl.BlockSpec((1,) + w_t.shape[1:], lambda j, *_: (slot, 0, 0), pipeline_mode=pl.Buffered(1))
    return pl.pallas_call(
        _mlstm_in_kernel,
        out_shape=(jax.ShapeDtypeStruct((N_TOK, D), BF16), jax.ShapeDtypeStruct((N_TOK, D), BF16),
                   jax.ShapeDtypeStruct((N_TOK, D), BF16), jax.ShapeDtypeStruct((N_TOK, D), F32),
                   jax.ShapeDtypeStruct((N_TOK, LANE), F32),
                   jax.ShapeDtypeStruct((N_TOK // M_CHUNK, 4 * M_HEADS, M_CHUNK), F32)),
        grid_spec=pltpu.PrefetchScalarGridSpec(
            num_scalar_prefetch=1, grid=(NB,),
            in_specs=[_tok_spec(D), _mod_spec(), _full_spec((1, D)), w_spec, _full_spec((1, LANE))],
            out_specs=(_tok_spec(D), _tok_spec(D), _tok_spec(D), _tok_spec(D), _tok_spec(LANE),
                       pl.BlockSpec((TM // M_CHUNK, 4 * M_HEADS, M_CHUNK), lambda j, *_: (j, 0, 0))),
            scratch_shapes=[pltpu.VMEM((D, 4 * D), BF16), pltpu.VMEM((D, LANE), BF16)]),
        compiler_params=_cparams(),
        name="mlstm_in",
    )(mrow, y, mods, g1, w_t, b_gate)


def _mlstm_load(hd, c, q_ref, k_ref, v_ref, gc_ref, gr_ref):
    r0 = pl.multiple_of(c * M_CHUNK, M_CHUNK)
    hs = slice(hd * M_HEAD_DIM, (hd + 1) * M_HEAD_DIM)
    rows = pl.ds(r0, M_CHUNK)
    return rows, hs, q_ref[rows, hs], k_ref[rows, hs], v_ref[rows, hs], gc_ref[rows, :], gr_ref[c]


def _mlstm_chunks(chains, ms, loaded, c_scr, n_scr):
    L = M_CHUNK
    n = range(len(chains))
    t_idx = lax.broadcasted_iota(I32, (L, L), 0)
    s_idx = lax.broadcasted_iota(I32, (L, L), 1)
    masks = {0: (s_idx <= t_idx, t_idx <= s_idx), 1: (s_idx >= t_idx, t_idx >= s_idx)}
    q = [ld[2] for ld in loaded]
    k = [ld[3] for ld in loaded]
    v = [ld[4] for ld in loaded]
    gi = [2 * d * M_HEADS + hd for hd, d in chains]
    gf = [(2 * d + 1) * M_HEADS + hd for hd, d in chains]
    i_col = [ld[5][:, gi[i]:gi[i] + 1] for i, ld in enumerate(loaded)]
    lf_col = [ld[5][:, gf[i]:gf[i] + 1] for i, ld in enumerate(loaded)]
    i_row = [ld[6][gi[i]:gi[i] + 1, :] for i, ld in enumerate(loaded)]
    lf_row = [ld[6][gf[i]:gf[i] + 1, :] for i, ld in enumerate(loaded)]
    mask = [masks[d][0] for _, d in chains]
    mask_t = [masks[d][1] for _, d in chains]
    b_col = [jnp.sum(jnp.where(mask[i], lf_row[i], 0.0), axis=1, keepdims=True) for i in n]
    b_row = [jnp.sum(jnp.where(mask_t[i], lf_col[i], 0.0), axis=0, keepdims=True) for i in n]
    log_d = [jnp.where(mask[i], b_col[i] - b_row[i] + i_row[i], -jnp.inf) for i in n]
    li = [b_col[i] + ms[i] for i in n]
    m_r = [jnp.maximum(li[i], jnp.max(log_d[i], axis=1, keepdims=True)) for i in n]
    a_int = [jnp.exp(li[i] - m_r[i]) for i in n]
    dmat = [jnp.exp(log_d[i] - m_r[i]) for i in n]
    cmat = [c_scr[d, hd] for hd, d in chains]
    nvec = [n_scr[d, hd] for hd, d in chains]
    gram = [lax.dot_general(q[i], k[i], (((1,), (1,)), ((), ())), preferred_element_type=F32) for i in n]
    inter = [jnp.dot(q[i], cmat[i].astype(BF16), preferred_element_type=F32) for i in n]
    s = [gram[i] * dmat[i] for i in n]
    intra = [jnp.dot(s[i].astype(BF16), v[i], preferred_element_type=F32) for i in n]
    qn = [jnp.sum(q[i].astype(F32) * nvec[i], axis=1, keepdims=True) for i in n]
    den = [a_int[i] * qn[i] + jnp.sum(s[i], axis=1, keepdims=True) for i in n]
    hh = [(a_int[i] * inter[i] + intra[i]) / jnp.maximum(jnp.abs(den[i]), jnp.exp(-m_r[i])) for i in n]
    b_last = [b_row[i][:, L - 1:L] if chains[i][1] == 0 else b_row[i][:, 0:1] for i in n]
    log_w = [b_last[i] - b_col[i] + i_col[i] for i in n]
    m_new = [jnp.maximum(b_last[i] + ms[i], jnp.max(log_w[i], axis=0, keepdims=True)) for i in n]
    w = [jnp.exp(log_w[i] - m_new[i]) for i in n]
    decay = [jnp.exp(b_last[i] + ms[i] - m_new[i]) for i in n]
    kw = [k[i].astype(F32) * w[i] for i in n]
    kv = [lax.dot_general(kw[i].astype(BF16), v[i], (((0,), (0,)), ((), ())), preferred_element_type=F32) for i in n]
    for i, (hd, d) in enumerate(chains):
        c_scr[d, hd] = decay[i] * cmat[i] + kv[i]
        n_scr[d, hd] = decay[i] * nvec[i] + jnp.sum(kw[i], axis=0, keepdims=True)
    return hh, m_new


SCAN_GROUP = 2 * M_HEADS


def _mlstm_scan_body(n_chunks, q_ref, k_ref, v_ref, gc_ref, gr_ref, h_ref, hb_scr, c_scr, n_scr, m0):
    chains = [(hd, d) for hd in range(M_HEADS) for d in range(2)]

    def body(c, ms):
        out = []
        for g0 in range(0, len(chains), SCAN_GROUP):
            grp = chains[g0:g0 + SCAN_GROUP]
            loaded = [_mlstm_load(hd, c if d == 0 else n_chunks - 1 - c, q_ref, k_ref, v_ref, gc_ref, gr_ref)
                      for hd, d in grp]
            hh, m_new = _mlstm_chunks(grp, ms[g0:g0 + SCAN_GROUP], loaded, c_scr, n_scr)
            for (hd, d), ld, h in zip(grp, loaded, hh):
                dst = h_ref if d == 0 else hb_scr
                dst[ld[0], ld[1]] = h
            out += m_new
        return tuple(out)

    ms = lax.fori_loop(0, n_chunks, body, tuple(m0))
    h_ref[...] += hb_scr[...]
    return ms


def _mlstm_scan_ctx_kernel(q_ref, k_ref, v_ref, gc_ref, gr_ref, h_ref, cn_ref, nn_ref, mn_ref, hb_scr, c_scr, n_scr):
    c_scr[...] = jnp.zeros(c_scr.shape, F32)
    n_scr[...] = jnp.zeros(n_scr.shape, F32)
    zero = jnp.zeros((1, 1), F32)
    ms = _mlstm_scan_body(SEQ // M_CHUNK, q_ref, k_ref, v_ref, gc_ref, gr_ref, h_ref, hb_scr, c_scr, n_scr,
                          [zero] * (2 * M_HEADS))
    cn_ref[0] = c_scr[...]
    nn_ref[0] = n_scr[...]
    for hd in range(M_HEADS):
        for d in range(2):
            mn_ref[0, d, hd] = jnp.broadcast_to(ms[2 * hd + d], (1, LANE))


def _mlstm_scan_lat_kernel(q_ref, k_ref, v_ref, gc_ref, gr_ref, c0_ref, n0_ref, m0_ref, ctx_out_ref, h_ref,
                           hb_scr, c_scr, n_scr):
    del ctx_out_ref
    c_scr[...] = c0_ref[0]
    n_scr[...] = n0_ref[0]
    m0 = [m0_ref[0, d, hd] for hd in range(M_HEADS) for d in range(2)]
    _mlstm_scan_body(DEC_SEQ // M_CHUNK, q_ref, k_ref, v_ref, gc_ref, gr_ref, h_ref, hb_scr, c_scr, n_scr, m0)


def _mlstm_scan(q, k, v, gcol, grow, state_c, state_n, state_m):
    hd = M_HEAD_DIM
    ng = 4 * M_HEADS
    state_scratch = [pltpu.VMEM((2, M_HEADS, hd, hd), F32), pltpu.VMEM((2, M_HEADS, 1, hd), F32)]
    ncp = SEQ // M_CHUNK
    h_ctx, new_c, new_n, new_m = pl.pallas_call(
        _mlstm_scan_ctx_kernel,
        out_shape=(jax.ShapeDtypeStruct((N_TOK, D), F32),
                   jax.ShapeDtypeStruct((BATCH, 2, M_HEADS, hd, hd), F32),
                   jax.ShapeDtypeStruct((BATCH, 2, M_HEADS, 1, hd), F32),
                   jax.ShapeDtypeStruct((BATCH, 2, M_HEADS, 1, LANE), F32)),
        grid=(BATCH,),
        in_specs=[
            pl.BlockSpec((SEQ, D), lambda s: (s, 0)),
            pl.BlockSpec((SEQ, D), lambda s: (s, 0)),
            pl.BlockSpec((SEQ, D), lambda s: (s, 0)),
            pl.BlockSpec((SEQ, LANE), lambda s: (s, 0)),
            pl.BlockSpec((ncp, ng, M_CHUNK), lambda s: (s, 0, 0)),
        ],
        out_specs=(
            pl.BlockSpec((SEQ, D), lambda s: (s, 0)),
            pl.BlockSpec((1, 2, M_HEADS, hd, hd), lambda s: (s, 0, 0, 0, 0)),
            pl.BlockSpec((1, 2, M_HEADS, 1, hd), lambda s: (s, 0, 0, 0, 0)),
            pl.BlockSpec((1, 2, M_HEADS, 1, LANE), lambda s: (s, 0, 0, 0, 0)),
        ),
        scratch_shapes=[pltpu.VMEM((SEQ, D), F32)] + state_scratch,
        compiler_params=_cparams(),
        name="mlstm_scan_ctx",
    )(q, k, v, gcol, grow)
    ncl = DEC_SEQ // M_CHUNK
    pb = NP_TOK // DEC_SEQ
    h_all = pl.pallas_call(
        _mlstm_scan_lat_kernel,
        out_shape=jax.ShapeDtypeStruct((N_TOK, D), F32),
        input_output_aliases={8: 0},
        grid=(DEC_BATCH,),
        in_specs=[
            pl.BlockSpec((DEC_SEQ, D), lambda b: (pb + b, 0)),
            pl.BlockSpec((DEC_SEQ, D), lambda b: (pb + b, 0)),
            pl.BlockSpec((DEC_SEQ, D), lambda b: (pb + b, 0)),
            pl.BlockSpec((DEC_SEQ, LANE), lambda b: (pb + b, 0)),
            pl.BlockSpec((ncl, ng, M_CHUNK), lambda b: (pb + b, 0, 0)),
            pl.BlockSpec((1, 2, M_HEADS, hd, hd), lambda b: (b, 0, 0, 0, 0)),
            pl.BlockSpec((1, 2, M_HEADS, 1, hd), lambda b: (b, 0, 0, 0, 0)),
            pl.BlockSpec((1, 2, M_HEADS, 1, 1), lambda b: (b, 0, 0, 0, 0)),
            pl.BlockSpec(memory_space=pl.ANY),
        ],
        out_specs=pl.BlockSpec((DEC_SEQ, D), lambda b: (pb + b, 0)),
        scratch_shapes=[pltpu.VMEM((DEC_SEQ, D), F32)] + state_scratch,
        compiler_params=_cparams(),
        name="mlstm_scan_lat",
    )(q, k, v, gcol, grow, state_c, state_n, state_m, h_ctx)
    return h_all, new_c, new_n, new_m


def _mlstm_out_kernel(mr_ref, h_ref, o_ref, ng_ref, wf_ref, y_ref, mod_ref, out_ref, x_scr, w_ref):
    @pl.when(pl.program_id(0) == 0)
    def _():
        w_ref[...] = wf_ref[0].astype(BF16)

    hc = o_ref[...] * h_ref[...]
    for hd in range(M_HEADS):
        sl = slice(hd * M_HEAD_DIM, (hd + 1) * M_HEAD_DIM)
        x_scr[:, sl] = _rms(hc[:, sl], ng_ref[:, sl]).astype(BF16)
    out = jnp.dot(x_scr[...], w_ref[...], preferred_element_type=F32)
    out_ref[...] = y_ref[...] + _mod_vec(mod_ref, 2) * out


def _mlstm_out(hsum, o, norm_g, w_out_all, slot, y, mods, mrow):
    return pl.pallas_call(
        _mlstm_out_kernel,
        out_shape=jax.ShapeDtypeStruct((N_TOK, D), F32),
        grid_spec=pltpu.PrefetchScalarGridSpec(
            num_scalar_prefetch=1, grid=(NL,),
            in_specs=[_tok_spec(D, TL), _tok_spec(D, TL), _full_spec((1, D)), _resident_weight_spec(w_out_all, slot),
                      _tok_spec(D, TL),
                      _mod_spec()],
            out_specs=_tok_spec(D, TL),
            scratch_shapes=[pltpu.VMEM((TL, D), BF16), pltpu.VMEM((D, D), BF16)]),
        compiler_params=_cparams(),
        name="mlstm_out",
    )(mrow, hsum, o, norm_g, w_out_all, y, mods)


ROUTE_OFF = N_GROUPS
SLAB = D // (2 * LANE)
V7X_SC_CORES = 2
V7X_SC_SUBCORES = 16
SC_WORKERS = V7X_SC_CORES * V7X_SC_SUBCORES
SC_WINDOW = 128
SC_HALF = SC_WINDOW // 2
HI_MASK = -65536


def _bf16_bits(x):
    return lax.bitcast_convert_type(x.astype(BF16).astype(F32), I32)


def _store_slabs(ref, x):
    rows = x.shape[0]
    for c in range(SLAB):
        lo = lax.shift_right_logical(_bf16_bits(x[:, (2 * c) * LANE:(2 * c + 1) * LANE]), 16)
        hi = _bf16_bits(x[:, (2 * c + 1) * LANE:(2 * c + 2) * LANE]) & HI_MASK
        ref[pl.ds(c, rows, stride=SLAB), :] = lo | hi


def _load_slabs(ref, dst, rows, dtype):
    for c in range(SLAB):
        w = ref[pl.ds(c, rows, stride=SLAB), :]
        lo = lax.bitcast_convert_type(lax.shift_left(w, 16), F32)
        hi = lax.bitcast_convert_type(w & HI_MASK, F32)
        dst[:, (2 * c) * LANE:(2 * c + 1) * LANE] = lo.astype(dtype)
        dst[:, (2 * c + 1) * LANE:(2 * c + 2) * LANE] = hi.astype(dtype)


def _route_kernel(mr_ref, y_ref, mod_ref, g_ref, wr_ref, br_ref, tri_ref, x_ref, wt_ref, dest_ref, cnt_ref,
                  cnt_scr, meta_scr):
    x = _norm_mod(y_ref[...], g_ref[...], mod_ref, 1)
    _store_slabs(x_ref, x)
    lg = jnp.dot(x.astype(BF16), wr_ref[...], preferred_element_type=F32) + br_ref[...]
    lane = lax.broadcasted_iota(I32, lg.shape, 1).astype(F32)
    ninf = -jnp.inf
    big = float(LANE)
    lgg = jnp.where(lane < N_GROUPS, lg, ninf)
    gmax = jnp.max(lgg, axis=-1, keepdims=True)
    g_idx = jnp.min(jnp.where(lgg == gmax, lane, big), axis=-1, keepdims=True)
    g_w = 1.0 / jnp.sum(jnp.exp(lgg - gmax), axis=-1, keepdims=True)
    lo = ROUTE_OFF + g_idx * EXPERTS_PER_GROUP
    le = jnp.where((lane >= lo) & (lane < lo + EXPERTS_PER_GROUP), lg, ninf)
    m1 = jnp.max(le, axis=-1, keepdims=True)
    i1 = jnp.min(jnp.where(le == m1, lane, big), axis=-1, keepdims=True)
    le2 = jnp.where(lane == i1, ninf, le)
    m2 = jnp.max(le2, axis=-1, keepdims=True)
    i2 = jnp.min(jnp.where(le2 == m2, lane, big), axis=-1, keepdims=True)
    r = jnp.exp(m2 - m1)
    p1 = 1.0 / (1.0 + r)
    p2 = r / (1.0 + r)
    two = lax.broadcasted_iota(I32, (x.shape[0], TOP_K), 1)
    wt_ref[...] = jnp.where(two == 0, g_w * p1, g_w * p2)
    @pl.when(pl.program_id(0) == 0)
    def _():
        cnt_scr[...] = jnp.zeros(cnt_scr.shape, F32)

    oh1 = (lane == i1).astype(F32)
    oh2 = (lane == i2).astype(F32)
    both = oh1 + oh2
    before = jnp.dot(tri_ref[...], both.astype(BF16), preferred_element_type=F32) + cnt_scr[...]
    rk1 = jnp.sum(oh1 * before, axis=-1, keepdims=True)
    rk2 = jnp.sum(oh2 * before, axis=-1, keepdims=True)
    cnt_scr[...] = cnt_scr[...] + jnp.sum(both, axis=0, keepdims=True)
    cnt_ref[...] = cnt_scr[...]
    cols = (i1 - ROUTE_OFF, i2 - ROUTE_OFF, rk1, rk2)
    packed = jnp.zeros(lg.shape, F32)
    for c, val in enumerate(cols):
        packed = jnp.where(lane == c, val, packed)
    j = pl.program_id(0)
    meta_scr[:, pl.ds(pl.multiple_of(j * TL, TL), TL)] = jnp.transpose(packed)[0:len(cols), :].astype(I32)

    @pl.when(j == NL - 1)
    def _():
        sub8 = (SUBLANE, LANE)
        ln = lax.broadcasted_iota(I32, sub8, 1)
        counts = jnp.broadcast_to(cnt_scr[...], sub8)
        is_e = (ln >= ROUTE_OFF) & (ln < ROUTE_OFF + N_EXPERTS)
        padded = jnp.where(is_e, jnp.floor((counts + (EBLK - 1)) * (1.0 / EBLK)) * EBLK, 0.0)
        inc = padded
        sh = 1
        while sh < LANE:
            inc = inc + jnp.where(ln >= sh, pltpu.roll(inc, sh, 1), 0.0)
            sh *= 2
        start = pltpu.roll(inc - padded, LANE - ROUTE_OFF, 1)
        table = jnp.transpose(jnp.broadcast_to(start[0:1, :], (LANE, LANE)))[0:N_EXPERTS, 0:1].astype(I32)
        sub = lax.broadcasted_iota(I32, (N_EXPERTS, SLOT_COLS), 0)
        for c0 in range(0, N_TOK, SLOT_COLS):
            meta = meta_scr[:, c0:c0 + SLOT_COLS]
            for k in range(TOP_K):
                first = jnp.sum(jnp.where(sub == meta[k:k + 1, :], table, 0), axis=0, keepdims=True)
                dest_ref[k:k + 1, c0:c0 + SLOT_COLS] = first + meta[TOP_K + k:TOP_K + k + 1, :]


SLOT_COLS = 2048


def _route(y, mods, mrow, g2, w_route, b_route):
    return pl.pallas_call(
        _route_kernel,
        out_shape=(jax.ShapeDtypeStruct((N_TOK * SLAB, LANE), I32), jax.ShapeDtypeStruct((N_TOK, TOP_K), F32),
                   jax.ShapeDtypeStruct((TOP_K, N_TOK), I32), jax.ShapeDtypeStruct((1, LANE), F32)),
        grid_spec=pltpu.PrefetchScalarGridSpec(
            num_scalar_prefetch=1, grid=(NL,),
            in_specs=[_tok_spec(D, TL), _mod_spec(), _full_spec((1, D)), _full_spec((D, LANE)),
                      _full_spec((1, LANE)), _full_spec((TL, TL))],
            out_specs=(pl.BlockSpec((TL * SLAB, LANE), lambda j, *_: (j, 0)), _tok_spec(TOP_K, TL),
                       _full_spec((TOP_K, N_TOK)), _full_spec((1, LANE))),
            scratch_shapes=[pltpu.VMEM((1, LANE), F32), pltpu.VMEM((2 * TOP_K, N_TOK), I32)]),
        compiler_params=_cparams(),
        name="moe_route",
    )(mrow, y, mods, g2, w_route, b_route, jnp.asarray(np.tril(np.ones((TL, TL), np.float32), -1), dtype=BF16))


def _block_tables_of(lane_counts):
    counts = lane_counts[0, ROUTE_OFF:ROUTE_OFF + N_EXPERTS].astype(I32)
    padded = ((counts + EBLK - 1) // EBLK) * EBLK
    pad_end = jnp.cumsum(padded)
    pad_start = pad_end - padded
    n_blk = (padded // EBLK).astype(I32)
    blk_start = (pad_start // EBLK).astype(I32)
    n_used = (pad_end[-1] // EBLK).astype(I32).reshape(1)
    return blk_start, n_blk, n_used


def _sc_mesh():
    return plsc.VectorSubcoreMesh(core_axis_name="core", subcore_axis_name="subcore",
                                  num_cores=V7X_SC_CORES, num_subcores=V7X_SC_SUBCORES)


def _sc_worker():
    return lax.axis_index("core") * V7X_SC_SUBCORES + lax.axis_index("subcore")


def _sc_dispatch(x_slabs, dest):
    per = N_TOK // SC_WORKERS
    n_win = per // SC_HALF

    @functools.partial(
        pl.kernel, out_type=jax.ShapeDtypeStruct((P_SLOTS, SLAB, LANE), I32), mesh=_sc_mesh(), name="moe_dispatch",
        scratch_types=[pltpu.VMEM((1, per), I32), pltpu.VMEM((1, per), I32), pltpu.VMEM((2, SC_HALF, SLAB, LANE), I32),
                       pltpu.SemaphoreType.DMA((2,)), pltpu.SemaphoreType.DMA((2,))])
    def run(x_hbm, d_hbm, o_hbm, i0_v, i1_v, buf, lsem, ssem):
        base = _sc_worker() * per
        pltpu.sync_copy(d_hbm.at[pl.ds(0, 1), pl.ds(base, per)], i0_v)
        pltpu.sync_copy(d_hbm.at[pl.ds(1, 1), pl.ds(base, per)], i1_v)
        loads = [pltpu.make_async_copy(x_hbm.at[pl.ds(base + s * SC_HALF, SC_HALF)], buf.at[s % 2], lsem.at[s % 2])
                 for s in range(n_win)]
        loads[0].start()
        for s in range(n_win):
            loads[s].wait()
            if s + 1 < n_win:
                loads[s + 1].start()
            win = pl.ds(s * SC_HALF, SC_HALF)
            outs = [pltpu.make_async_copy(buf.at[s % 2], o_hbm.at[iv.at[0, win]], ssem.at[a])
                    for a, iv in enumerate((i0_v, i1_v))]
            for cp in outs:
                cp.start()
            for cp in outs:
                cp.wait()

    return run(x_slabs.reshape(N_TOK, SLAB, LANE), dest)


def _sc_collect(y_slabs, dest):
    per = N_ASSIGN // SC_WORKERS
    n_win = per // SC_HALF

    @functools.partial(
        pl.kernel, out_type=jax.ShapeDtypeStruct((N_ASSIGN, SLAB, LANE), I32), mesh=_sc_mesh(), name="moe_collect",
        scratch_types=[pltpu.VMEM((1, per), I32), pltpu.VMEM((2, SC_HALF, SLAB, LANE), I32),
                       pltpu.SemaphoreType.DMA((2,)), pltpu.SemaphoreType.DMA((2,))])
    def run(y_hbm, i_hbm, o_hbm, i_v, buf, gsem, wsem):
        w = _sc_worker()
        base = w * per
        per_a = SC_WORKERS // TOP_K
        pltpu.sync_copy(i_hbm.at[pl.ds(w // per_a, 1), pl.ds((w % per_a) * per, per)], i_v)
        gathers = [pltpu.make_async_copy(y_hbm.at[i_v.at[0, pl.ds(s * SC_HALF, SC_HALF)]], buf.at[s % 2],
                                         gsem.at[s % 2]) for s in range(n_win)]
        writes = [pltpu.make_async_copy(buf.at[s % 2], o_hbm.at[pl.ds(base + s * SC_HALF, SC_HALF)], wsem.at[s % 2])
                  for s in range(n_win)]
        gathers[0].start()
        for s in range(n_win):
            gathers[s].wait()
            if s >= 1:
                writes[s - 1].wait()
            if s + 1 < n_win:
                gathers[s + 1].start()
            writes[s].start()
        writes[n_win - 1].wait()

    return run(y_slabs.reshape(P_SLOTS, SLAB, LANE), dest)


EROWS = EBLK * SLAB


def _expert_kernel(bs_ref, nb_ref, nu_ref, wg_ref, wu_ref, wd_ref, x_hbm, y_hbm,
                   xbuf, ybuf, xs, wg_bf, wu_bf, wd_bf, isem, osem):
    e = pl.program_id(0)
    n_exp = pl.num_programs(0)
    n_used = nu_ref[0]
    b0 = bs_ref[e]
    nb = nb_ref[e]

    def in_copy(g, slot):
        return pltpu.make_async_copy(x_hbm.at[pl.ds(pl.multiple_of(g * EROWS, EROWS), EROWS)], xbuf.at[slot],
                                     isem.at[slot])

    def out_copy(g, slot):
        return pltpu.make_async_copy(ybuf.at[slot], y_hbm.at[pl.ds(pl.multiple_of(g * EROWS, EROWS), EROWS)],
                                     osem.at[slot])

    @pl.when(e == 0)
    def _():
        in_copy(0, 0).start()

    @pl.when(nb > 0)
    def _():
        wg_bf[...] = wg_ref[0, 0].astype(BF16)
        wu_bf[...] = wu_ref[0, 0].astype(BF16)
        wd_bf[...] = wd_ref[0, 0].astype(BF16)

    def block(k, carry):
        g = b0 + k
        slot = lax.rem(g, 2)
        in_copy(g, slot).wait()

        @pl.when(g + 1 < n_used)
        def _():
            in_copy(g + 1, 1 - slot).start()

        _load_slabs(xbuf.at[slot], xs, EBLK, BF16)
        xb = xs[...]
        gt = jnp.dot(xb, wg_bf[...], preferred_element_type=F32)
        up = jnp.dot(xb, wu_bf[...], preferred_element_type=F32)
        hmid = (gt * _sigmoid(gt) * up).astype(BF16)
        res = jnp.dot(hmid, wd_bf[...], preferred_element_type=F32)

        @pl.when(g >= 2)
        def _():
            out_copy(g - 2, slot).wait()

        _store_slabs(ybuf.at[slot], res)
        out_copy(g, slot).start()
        return carry

    lax.fori_loop(0, nb, block, 0)

    @pl.when(e == n_exp - 1)
    def _():
        last = n_used - 1
        out_copy(last, lax.rem(last, 2)).wait()

        @pl.when(n_used >= 2)
        def _():
            out_copy(last - 1, lax.rem(last - 1, 2)).wait()


def _experts(x_sorted, blk_start, n_blk, n_used, w_gate, w_up, w_down, layer):
    any_spec = pl.BlockSpec(memory_space=pl.ANY)
    wspec = lambda r, c: pl.BlockSpec((1, 1, r, c), lambda e, *_: (layer, e, 0, 0))
    return pl.pallas_call(
        _expert_kernel,
        out_shape=jax.ShapeDtypeStruct((P_SLOTS * SLAB, LANE), I32),
        grid_spec=pltpu.PrefetchScalarGridSpec(
            num_scalar_prefetch=3, grid=(N_EXPERTS,),
            in_specs=[wspec(D, D_EXPERT), wspec(D, D_EXPERT), wspec(D_EXPERT, D), any_spec],
            out_specs=any_spec,
            scratch_shapes=[
                pltpu.VMEM((2, EROWS, LANE), I32), pltpu.VMEM((2, EROWS, LANE), I32),
                pltpu.VMEM((EBLK, D), BF16),
                pltpu.VMEM((D, D_EXPERT), BF16), pltpu.VMEM((D, D_EXPERT), BF16), pltpu.VMEM((D_EXPERT, D), BF16),
                pltpu.SemaphoreType.DMA((2,)), pltpu.SemaphoreType.DMA((2,)),
            ]),
        compiler_params=_cparams(),
        name="moe_experts",
    )(blk_start, n_blk, n_used, w_gate, w_up, w_down, x_sorted.reshape(P_SLOTS * SLAB, LANE))


def _combine_kernel(final, mr_ref, e0_ref, e1_ref, wt_ref, y_ref, mod_ref, fg_ref, o_ref, a_scr, b_scr):
    _load_slabs(e0_ref, a_scr, TL, F32)
    _load_slabs(e1_ref, b_scr, TL, F32)
    wt = wt_ref[...]
    moe = wt[:, 0:1] * a_scr[...] + wt[:, 1:2] * b_scr[...]
    y_new = y_ref[...] + _mod_vec(mod_ref, 5) * moe
    o_ref[...] = _rms(y_new, fg_ref[...]) if final else y_new


def _combine(ym, wts, y, mods, mrow, final_g, blk0, nblk, final):
    tok = lambda width: pl.BlockSpec((TL, width), lambda j, *_: (blk0 + j, 0))
    slab0 = pl.BlockSpec((TL * SLAB, LANE), lambda j, *_: (blk0 + j, 0))
    slab1 = pl.BlockSpec((TL * SLAB, LANE), lambda j, *_: (NL + blk0 + j, 0))
    mod = pl.BlockSpec((1, 1, 6 * D), lambda j, mr: (mr[blk0 + j], 0, 0))
    return pl.pallas_call(
        functools.partial(_combine_kernel, final),
        out_shape=jax.ShapeDtypeStruct((nblk * TL, D), F32),
        grid_spec=pltpu.PrefetchScalarGridSpec(
            num_scalar_prefetch=1, grid=(nblk,),
            in_specs=[slab0, slab1, tok(TOP_K), tok(D), mod, _full_spec((1, D))],
            out_specs=pl.BlockSpec((TL, D), lambda j, *_: (j, 0)),
            scratch_shapes=[pltpu.VMEM((TL, D), F32), pltpu.VMEM((TL, D), F32)]),
        compiler_params=_cparams(),
        name="moe_combine",
    )(mrow, ym, ym, wts, y, mods, final_g)


def kernel(x_prompt, x_sample, cache_attn_k, cache_attn_v, state_mlstm_C, state_mlstm_n, state_mlstm_m, c, c_ctx, ada_w, ada_b, norm1_g, norm2_g, conv_w_in, conv_w_dw, conv_b_dw, conv_ln_g, conv_ln_b, conv_w_out, attn_w_qkv, attn_q_norm, attn_k_norm, attn_w_o, mlstm_w_in, mlstm_b_gate, mlstm_norm_g, mlstm_w_out, moe_w_group, moe_b_group, moe_w_router, moe_b_router, moe_w_gate, moe_w_up, moe_w_down, final_norm_g):
    y = None
    cvec = jnp.concatenate([c_ctx[None, :], c, jnp.zeros((MOD_ROWS - 1 - DEC_BATCH, D), F32)], axis=0)
    rope = _rope_blocks()
    mrow, mrow_sb, mrow_l = jnp.asarray(_MOD_ROW), jnp.asarray(_MOD_ROW_SB), jnp.asarray(_MOD_ROW_L)
    new_k = new_v = new_c = new_n = new_m = None
    for i in range(DEPTH):
        kind, slot = i % 3, i // 3
        mods = _ada_layer(cvec, ada_w, ada_b, i)
        g1 = norm1_g[i].reshape(1, D)
        if kind == 0:
            src = (x_prompt.reshape(NP_TOK, D), x_sample.reshape(NS_TOK, D), True) if i == 0 else (y, y, False)
            u = _conv_in(*src, mods, mrow_l, g1, conv_w_in, slot)
            w_dw = jnp.concatenate([conv_w_dw[slot], jnp.zeros((1, D), F32)], axis=0)
            y = _conv_main(u, *src, mods, mrow_sb, w_dw, conv_b_dw[slot].reshape(1, D), conv_ln_g[slot].reshape(1, D),
                           conv_ln_b[slot].reshape(1, D), conv_w_out, slot)
        elif kind == 1:
            q, kb, vb, kf, vf = _attn_qkv(y, mods, mrow_sb, g1, attn_w_qkv, slot,
                                          attn_q_norm[slot].reshape(1, HEAD_DIM), attn_k_norm[slot].reshape(1, HEAD_DIM),
                                          rope)
            new_k = kf[:NP_TOK].reshape(BATCH, 1, SEQ, N_KV_HEADS, HEAD_DIM)
            new_v = vf[:NP_TOK].reshape(BATCH, 1, SEQ, N_KV_HEADS, HEAD_DIM)
            ck = cache_attn_k[:, slot].reshape(DEC_BATCH, PAST_LEN, KV_DIM)
            cv = cache_attn_v[:, slot].reshape(DEC_BATCH, PAST_LEN, KV_DIM)
            y = _attention(q, kb, vb, ck, cv, attn_w_o[slot].astype(BF16), y, mods)
        else:
            b_gate = jnp.concatenate([mlstm_b_gate[slot], jnp.zeros((LANE - 4 * M_HEADS,), F32)]).reshape(1, LANE)
            q, k, v, o, gates, grow = _mlstm_in(y, mods, mrow, g1, mlstm_w_in, slot, b_gate)
            sc = state_mlstm_C[:, slot]
            sn = state_mlstm_n[:, slot].reshape(DEC_BATCH, 2, M_HEADS, 1, M_HEAD_DIM)
            sm = state_mlstm_m[:, slot].reshape(DEC_BATCH, 2, M_HEADS, 1, 1)
            hsum, nc_, nn_, nm_ = _mlstm_scan(q, k, v, gates, grow, sc, sn, sm)
            new_c = nc_[:, None]
            new_n = nn_.reshape(BATCH, 1, 2, M_HEADS, M_HEAD_DIM)
            new_m = nm_[..., 0, 0].reshape(BATCH, 1, 2, M_HEADS)
            y = _mlstm_out(hsum, o, mlstm_norm_g[slot].reshape(1, D), mlstm_w_out, slot, y, mods, mrow_l)
        w_route = jnp.concatenate([moe_w_group[i], moe_w_router[i],
                                   jnp.zeros((D, LANE - N_GROUPS - N_EXPERTS), F32)], axis=1)
        b_route = jnp.concatenate([moe_b_group[i], moe_b_router[i],
                                   jnp.zeros((LANE - N_GROUPS - N_EXPERTS,), F32)]).reshape(1, LANE)
        x2, ewt, dest, cnt = _route(y, mods, mrow_l, norm2_g[i].reshape(1, D), w_route.astype(BF16), b_route)
        blk_start, n_blk, n_used = _block_tables_of(cnt)
        x_sorted = _sc_dispatch(x2, dest)
        y_sorted = _experts(x_sorted, blk_start, n_blk, n_used, moe_w_gate, moe_w_up, moe_w_down, i)
        ym = _sc_collect(y_sorted, dest)
        ym = ym.reshape(N_ASSIGN * SLAB, LANE)
        fg = final_norm_g.reshape(1, D)
        if i + 1 < DEPTH:
            y = _combine(ym, ewt, y, mods, mrow_l, fg, 0, NL, False)
        else:
            y_prompt = _combine(ym, ewt, y, mods, mrow_l, fg, 0, NLP, True).reshape(BATCH, SEQ, D)
            y_sample = _combine(ym, ewt, y, mods, mrow_l, fg, NLP, NL - NLP, True).reshape(DEC_BATCH, DEC_SEQ, D)
    return (y_prompt, y_sample, new_k, new_v, new_c, new_n, new_m)
```

```python
import functools

import jax
import jax.numpy as jnp
import numpy as np
from jax import lax
from jax.experimental import pallas as pl
from jax.experimental.pallas import tpu as pltpu
from jax.experimental.pallas import tpu_sc as plsc

F32 = jnp.float32
BF16 = jnp.bfloat16
I32 = jnp.int32

D = 1024
BATCH, SEQ = 16, 256
DEC_BATCH, DEC_SEQ = 8, 1024
PAST_LEN = 256
DEPTH = 4
GRID_W = 64
EPS = 1e-6
CONV_WIDTH = 31
CONV_PAD = CONV_WIDTH // 2
HEAD_DIM = 128
N_HEADS = 8
N_KV_HEADS = 2
GQA_GROUP = N_HEADS // N_KV_HEADS
Q_DIM = N_HEADS * HEAD_DIM
KV_DIM = N_KV_HEADS * HEAD_DIM
QKV_DIM = Q_DIM + 2 * KV_DIM
ROPE_THETA = 10000.0
M_HEADS = 4
M_HEAD_DIM = D // M_HEADS
M_CHUNK = 64
N_GROUPS = 4
EXPERTS_PER_GROUP = 8
N_EXPERTS = N_GROUPS * EXPERTS_PER_GROUP
TOP_K = 2
D_EXPERT = 512

NP_TOK = BATCH * SEQ
NS_TOK = DEC_BATCH * DEC_SEQ
N_TOK = NP_TOK + NS_TOK
TM = 512
NB = N_TOK // TM
NBP = NP_TOK // TM
BLK_PER_DEC = DEC_SEQ // TM
TL = 1024
NL = N_TOK // TL
NLP = NP_TOK // TL
SB = 256
NSB = N_TOK // SB
NSBP = NP_TOK // SB
SB_PER_DEC = DEC_SEQ // SB
MOD_ROWS = 16
HALO = 16
LANE = 128
SUBLANE = 8

N_ASSIGN = N_TOK * TOP_K
EBLK = 1024
N_EBLK = N_ASSIGN // EBLK + N_EXPERTS
P_SLOTS = N_EBLK * EBLK
N_PAD_SLOTS = P_SLOTS - N_ASSIGN

VMEM_LIMIT = 56 * 1024 * 1024


def _block_tables(nb, nbp, per_dec):
    j = np.arange(nb)
    is_p = j < nbp
    mod_row = np.where(is_p, 0, 1 + (j - nbp) // per_dec)
    rope_idx = np.where(is_p, 0, 1 + (j - nbp) % per_dec)
    first = np.where(is_p, 1, ((j - nbp) % per_dec == 0).astype(np.int64))
    last = np.where(is_p, 1, ((j - nbp) % per_dec == per_dec - 1).astype(np.int64))
    return (mod_row.astype(np.int32), rope_idx.astype(np.int32), first.astype(np.int32), last.astype(np.int32))


_MOD_ROW, _, _, _ = _block_tables(NB, NBP, BLK_PER_DEC)
_MOD_ROW_L, _, _, _ = _block_tables(NL, NLP, DEC_SEQ // TL)
_MOD_ROW_SB, _ROPE_IDX_SB, _SEQ_FIRST, _SEQ_LAST = _block_tables(NSB, NSBP, SB_PER_DEC)


def _cparams(n_axes=1):
    return pltpu.CompilerParams(dimension_semantics=("arbitrary",) * n_axes, vmem_limit_bytes=VMEM_LIMIT)


def _sigmoid(x):
    return 1.0 / (1.0 + jnp.exp(-x))


def _rms(x, g):
    return x * lax.rsqrt(jnp.mean(x * x, axis=-1, keepdims=True) + EPS) * g


def _mod_vec(mod_ref, k):
    return mod_ref[0, :, k * D:(k + 1) * D]


def _norm_mod(y, g, mod_ref, which):
    return _rms(y, g) * (1.0 + _mod_vec(mod_ref, 3 * which + 1)) + _mod_vec(mod_ref, 3 * which)


def _ada_kernel(c_ref, w_ref, b_ref, o_ref):
    c = c_ref[...]
    s = c * _sigmoid(c)
    res = jnp.dot(s.astype(BF16), w_ref[0].astype(BF16), preferred_element_type=F32) + b_ref[0]
    for r in range(MOD_ROWS):
        o_ref[r] = res[r:r + 1, :]


def _ada_layer(cvec, ada_w, ada_b, layer):
    tn = 1536
    return pl.pallas_call(
        _ada_kernel,
        out_shape=jax.ShapeDtypeStruct((MOD_ROWS, 1, 6 * D), F32),
        grid=(6 * D // tn,),
        in_specs=[
            pl.BlockSpec((MOD_ROWS, D), lambda n: (0, 0)),
            pl.BlockSpec((1, D, tn), lambda n: (layer, 0, n)),
            pl.BlockSpec((1, 1, tn), lambda n: (layer, 0, n)),
        ],
        out_specs=pl.BlockSpec((MOD_ROWS, 1, tn), lambda n: (0, 0, n)),
        compiler_params=_cparams(1),
        name="ada_mod",
    )(cvec, ada_w, ada_b.reshape(DEPTH, 1, 6 * D))


def _tok_spec(width, rows=TM):
    return pl.BlockSpec((rows, width), lambda j, *_: (j, 0))


def _mod_spec():
    return pl.BlockSpec((1, 1, 6 * D), lambda j, mr, *_: (mr[j], 0, 0))


def _full_spec(shape):
    nd = len(shape)
    return pl.BlockSpec(shape, lambda j, *_: (0,) * nd)


def _pair_specs(rows, nbp, split):
    s_off = nbp if split else 0
    return [pl.BlockSpec((rows, D), lambda j, *_: (jnp.minimum(j, nbp - 1), 0)),
            pl.BlockSpec((rows, D), lambda j, *_: (jnp.maximum(j, nbp) - s_off, 0))]


def _pair_block(nbp, yp_ref, ys_ref):
    return jnp.where(pl.program_id(0) < nbp, yp_ref[...], ys_ref[...])


def _resident_weight_spec(w_all, slot):
    return pl.BlockSpec((1,) + w_all.shape[1:], lambda j, *_: (slot, 0, 0), pipeline_mode=pl.Buffered(1))


def _conv_in_kernel(mr_ref, yp_ref, ys_ref, mod_ref, g_ref, wf_ref, u_ref, w_ref):
    @pl.when(pl.program_id(0) == 0)
    def _():
        w_ref[...] = wf_ref[0].astype(BF16)

    h = _norm_mod(_pair_block(NLP, yp_ref, ys_ref), g_ref[...], mod_ref, 0)
    ag = jnp.dot(h.astype(BF16), w_ref[...], preferred_element_type=F32)
    u_ref[...] = ag[:, :D] * _sigmoid(ag[:, D:])


def _conv_in(yp, ys, split, mods, mrow, g1, w_in_all, slot):
    return pl.pallas_call(
        _conv_in_kernel,
        out_shape=jax.ShapeDtypeStruct((N_TOK, D), F32),
        grid_spec=pltpu.PrefetchScalarGridSpec(
            num_scalar_prefetch=1, grid=(NL,),
            in_specs=_pair_specs(TL, NLP, split) + [_mod_spec(), _full_spec((1, D)),
                                                    _resident_weight_spec(w_in_all, slot)],
            out_specs=_tok_spec(D, TL),
            scratch_shapes=[pltpu.VMEM((D, 2 * D), BF16)]),
        compiler_params=_cparams(),
        name="conv_in",
    )(mrow, yp, ys, mods, g1, w_in_all)


def _conv_main_kernel(mr_ref, first_ref, last_ref, u_ref, up_ref, un_ref, wdw_ref, bdw_ref, lg_ref, lb_ref,
                      woutf_ref, yp_ref, ys_ref, mod_ref, o_ref, ext_ref, acc_ref, wout_ref):
    j = pl.program_id(0)

    @pl.when(j == 0)
    def _():
        wout_ref[...] = woutf_ref[0].astype(BF16)

    zero = jnp.zeros((HALO, D), F32)
    ext_ref[0:HALO, :] = jnp.where(first_ref[j] == 1, zero, up_ref[...])
    ext_ref[HALO:HALO + SB, :] = u_ref[...]
    ext_ref[HALO + SB:2 * HALO + SB, :] = jnp.where(last_ref[j] == 1, zero, un_ref[...])

    off0 = HALO - CONV_PAD
    n_a = (off0 + CONV_WIDTH - 1) // SUBLANE + 1
    n_chunks = SB // SUBLANE

    def strip(ci, carry):
        cs = pl.ds(pl.multiple_of(ci * LANE, LANE), LANE)
        wk = [jnp.broadcast_to(wdw_ref[k:k + 1, cs], (SUBLANE, LANE)) for k in range(CONV_WIDTH)]
        bias = jnp.broadcast_to(bdw_ref[:, cs], (SUBLANE, LANE))
        sub = lax.broadcasted_iota(I32, (SUBLANE, LANE), 0)
        prev_rot, prev_v0 = None, None
        for j in range(n_chunks + 1):
            tiles = [ext_ref[SUBLANE * (j + a):SUBLANE * (j + a + 1), cs] for a in range(n_a)]
            part = []
            for s in range(SUBLANE):
                acc = None
                for a in range(n_a):
                    k = SUBLANE * a + s - off0
                    if (0 <= k < CONV_WIDTH) and not (s == 0 and j == n_chunks):
                        term = tiles[a] * wk[k]
                        acc = term if acc is None else acc + term
                part.append(acc)
            rot = [None] + [pltpu.roll(part[s], SUBLANE - s, 0) for s in range(1, SUBLANE)]
            if j >= 1:
                out = prev_v0 + bias
                for s in range(1, SUBLANE):
                    out = out + jnp.where(sub < SUBLANE - s, prev_rot[s], rot[s])
                acc_ref[SUBLANE * (j - 1):SUBLANE * j, cs] = out
            prev_rot, prev_v0 = rot, part[0]
        return carry

    lax.fori_loop(0, D // LANE, strip, 0)

    c = acc_ref[...]
    mu = jnp.mean(c, axis=-1, keepdims=True)
    cc = c - mu
    var = jnp.mean(cc * cc, axis=-1, keepdims=True)
    z = cc * lax.rsqrt(var + EPS) * lg_ref[...] + lb_ref[...]
    z = z * _sigmoid(z)
    out = jnp.dot(z.astype(BF16), wout_ref[...], preferred_element_type=F32)
    o_ref[...] = _pair_block(NSBP, yp_ref, ys_ref) + _mod_vec(mod_ref, 2) * out


def _conv_main(u, yp, ys, split, mods, mrow, w_dw, b_dw, ln_g, ln_b, w_out_all, slot):
    nh = N_TOK // HALO
    per = SB // HALO
    sb_spec = pl.BlockSpec((SB, D), lambda j, *_: (j, 0))
    return pl.pallas_call(
        _conv_main_kernel,
        out_shape=jax.ShapeDtypeStruct((N_TOK, D), F32),
        grid_spec=pltpu.PrefetchScalarGridSpec(
            num_scalar_prefetch=3, grid=(NSB,),
            in_specs=[
                sb_spec,
                pl.BlockSpec((HALO, D), lambda j, *_: (jnp.maximum(j * per - 1, 0), 0)),
                pl.BlockSpec((HALO, D), lambda j, *_: (jnp.minimum((j + 1) * per, nh - 1), 0)),
                _full_spec((CONV_WIDTH + 1, D)), _full_spec((1, D)), _full_spec((1, D)), _full_spec((1, D)),
                _resident_weight_spec(w_out_all, slot), *_pair_specs(SB, NSBP, split), _mod_spec(),
            ],
            out_specs=sb_spec,
            scratch_shapes=[pltpu.VMEM((SB + 2 * HALO, D), F32), pltpu.VMEM((SB, D), F32), pltpu.VMEM((D, D), BF16)]),
        compiler_params=_cparams(),
        name="conv_main",
    )(mrow, jnp.asarray(_SEQ_FIRST), jnp.asarray(_SEQ_LAST), u, u, u, w_dw, b_dw, ln_g, ln_b, w_out_all, yp, ys, mods)


def _rope_angles():
    rows = DEC_SEQ // GRID_W
    row = jnp.repeat(jnp.arange(rows, dtype=F32), GRID_W)
    col = jnp.tile(jnp.arange(GRID_W, dtype=F32), rows)
    axis_dim = HEAD_DIM // 2
    freqs = jnp.power(ROPE_THETA, -jnp.arange(axis_dim // 2, dtype=F32) * 2.0 / axis_dim)
    ang_r = row[:, None] * freqs[None, :]
    ang_c = col[:, None] * freqs[None, :]
    return jnp.concatenate([ang_r, ang_r, ang_c, ang_c], axis=-1)


def _rope_blocks():
    ang = _rope_angles()
    cos, sin = jnp.cos(ang), jnp.sin(ang)
    lane = np.arange(HEAD_DIM)
    lo = jnp.asarray(((lane % (HEAD_DIM // 2)) < HEAD_DIM // 4).astype(np.float32))
    sin_a = -sin * lo[None, :]
    sin_b = sin * (1.0 - lo)[None, :]
    nblk = DEC_SEQ // SB
    ident = jnp.ones((1, SB, HEAD_DIM), F32)
    zeros = jnp.zeros((1, SB, HEAD_DIM), F32)
    cos_t = jnp.concatenate([ident, cos.reshape(nblk, SB, HEAD_DIM)], axis=0)
    sa_t = jnp.concatenate([zeros, sin_a.reshape(nblk, SB, HEAD_DIM)], axis=0)
    sb_t = jnp.concatenate([zeros, sin_b.reshape(nblk, SB, HEAD_DIM)], axis=0)
    return cos_t, sa_t, sb_t


def _attn_qkv_kernel(mr_ref, ri_ref, y_ref, mod_ref, g_ref, wf_ref, qg_ref, kg_ref, cos_ref, sa_ref, sb_ref,
                     q_ref, kb_ref, vb_ref, kf_ref, vf_ref, w_ref):
    @pl.when(pl.program_id(0) == 0)
    def _():
        w_ref[...] = wf_ref[0].astype(BF16)

    h = _norm_mod(y_ref[...], g_ref[...], mod_ref, 0)
    qkv = jnp.dot(h.astype(BF16), w_ref[...], preferred_element_type=F32)
    cos, sa, sb = cos_ref[0], sa_ref[0], sb_ref[0]
    quarter = HEAD_DIM // 4

    def head(x, g):
        xn = _rms(x, g)
        return xn * cos + pltpu.roll(xn, HEAD_DIM - quarter, 1) * sa + pltpu.roll(xn, quarter, 1) * sb

    scale = HEAD_DIM ** -0.5
    for hd in range(N_HEADS):
        sl = slice(hd * HEAD_DIM, (hd + 1) * HEAD_DIM)
        q_ref[:, sl] = (head(qkv[:, sl], qg_ref[...]) * scale).astype(BF16)
    for kv in range(N_KV_HEADS):
        sl = slice(kv * HEAD_DIM, (kv + 1) * HEAD_DIM)
        kr = head(qkv[:, Q_DIM + kv * HEAD_DIM:Q_DIM + (kv + 1) * HEAD_DIM], kg_ref[...])
        kf_ref[:, sl] = kr
        kb_ref[:, sl] = kr.astype(BF16)
    v = qkv[:, Q_DIM + KV_DIM:]
    vf_ref[...] = v
    vb_ref[...] = v.astype(BF16)


def _attn_qkv(y, mods, mrow, g1, w_qkv_all, slot, q_g, k_g, rope):
    cos_t, sa_t, sb_t = rope
    rspec = pl.BlockSpec((1, SB, HEAD_DIM), lambda j, mr, ri: (ri[j], 0, 0))
    return pl.pallas_call(
        _attn_qkv_kernel,
        out_shape=(jax.ShapeDtypeStruct((N_TOK, Q_DIM), BF16), jax.ShapeDtypeStruct((N_TOK, KV_DIM), BF16),
                   jax.ShapeDtypeStruct((N_TOK, KV_DIM), BF16), jax.ShapeDtypeStruct((N_TOK, KV_DIM), F32),
                   jax.ShapeDtypeStruct((N_TOK, KV_DIM), F32)),
        grid_spec=pltpu.PrefetchScalarGridSpec(
            num_scalar_prefetch=2, grid=(NSB,),
            in_specs=[_tok_spec(D, SB), _mod_spec(), _full_spec((1, D)), _resident_weight_spec(w_qkv_all, slot),
                      _full_spec((1, HEAD_DIM)), _full_spec((1, HEAD_DIM)), rspec, rspec, rspec],
            out_specs=(_tok_spec(Q_DIM, SB), _tok_spec(KV_DIM, SB), _tok_spec(KV_DIM, SB), _tok_spec(KV_DIM, SB),
                       _tok_spec(KV_DIM, SB)),
            scratch_shapes=[pltpu.VMEM((D, QKV_DIM), BF16)]),
        compiler_params=_cparams(),
        name="attn_qkv",
    )(mrow, jnp.asarray(_ROPE_IDX_SB), y, mods, g1, w_qkv_all, q_g, k_g, cos_t, sa_t, sb_t)


def _attn_heads(q, ks, vs, o_scr):
    nt = (((1,), (1,)), ((), ()))
    for hd in range(N_HEADS):
        g = hd // GQA_GROUP
        qh = q[:, hd * HEAD_DIM:(hd + 1) * HEAD_DIM]
        gs = slice(g * HEAD_DIM, (g + 1) * HEAD_DIM)
        ss = [lax.dot_general(qh, k[:, gs], nt, preferred_element_type=F32) for k in ks]
        m = functools.reduce(jnp.maximum, [jnp.max(s, axis=-1, keepdims=True) for s in ss])
        ps = [jnp.exp(s - m) for s in ss]
        l = functools.reduce(lambda a, b: a + b, [jnp.sum(p, axis=-1, keepdims=True) for p in ps])
        o = functools.reduce(lambda a, b: a + b,
                             [jnp.dot(p.astype(BF16), v[:, gs], preferred_element_type=F32) for p, v in zip(ps, vs)])
        o_scr[:, hd * HEAD_DIM:(hd + 1) * HEAD_DIM] = (o / l).astype(BF16)


def _attn_ctx_kernel(q_ref, k_ref, v_ref, wo_ref, y_ref, mod_ref, o_ref, o_scr):
    _attn_heads(q_ref[...], [k_ref[...]], [v_ref[...]], o_scr)
    out = jnp.dot(o_scr[...], wo_ref[...], preferred_element_type=F32)
    o_ref[...] = y_ref[...] + _mod_vec(mod_ref, 2) * out


def _attn_lat_kernel(q_ref, k_ref, v_ref, ck_ref, cv_ref, wo_ref, y_ref, mod_ref, ctx_out_ref, o_ref, o_scr):
    del ctx_out_ref
    _attn_heads(q_ref[...], [k_ref[...], ck_ref[0].astype(BF16)], [v_ref[...], cv_ref[0].astype(BF16)], o_scr)
    out = jnp.dot(o_scr[...], wo_ref[...], preferred_element_type=F32)
    o_ref[...] = y_ref[...] + _mod_vec(mod_ref, 2) * out


def _attention(q, kb, vb, cache_k, cache_v, w_o, y, mods):
    y_ctx = pl.pallas_call(
        _attn_ctx_kernel,
        out_shape=jax.ShapeDtypeStruct((N_TOK, D), F32),
        grid=(BATCH,),
        in_specs=[
            pl.BlockSpec((SEQ, Q_DIM), lambda s: (s, 0)),
            pl.BlockSpec((SEQ, KV_DIM), lambda s: (s, 0)),
            pl.BlockSpec((SEQ, KV_DIM), lambda s: (s, 0)),
            pl.BlockSpec((Q_DIM, D), lambda s: (0, 0)),
            pl.BlockSpec((SEQ, D), lambda s: (s, 0)),
            pl.BlockSpec((1, 1, 6 * D), lambda s: (0, 0, 0)),
        ],
        out_specs=pl.BlockSpec((SEQ, D), lambda s: (s, 0)),
        scratch_shapes=[pltpu.VMEM((SEQ, Q_DIM), BF16)],
        compiler_params=_cparams(),
        name="attn_ctx",
    )(q, kb, vb, w_o, y, mods)
    pb = NP_TOK // DEC_SEQ
    return pl.pallas_call(
        _attn_lat_kernel,
        out_shape=jax.ShapeDtypeStruct((N_TOK, D), F32),
        input_output_aliases={8: 0},
        grid=(DEC_BATCH, SB_PER_DEC),
        in_specs=[
            pl.BlockSpec((SB, Q_DIM), lambda b, t: (NSBP + b * SB_PER_DEC + t, 0)),
            pl.BlockSpec((DEC_SEQ, KV_DIM), lambda b, t: (pb + b, 0)),
            pl.BlockSpec((DEC_SEQ, KV_DIM), lambda b, t: (pb + b, 0)),
            pl.BlockSpec((1, PAST_LEN, KV_DIM), lambda b, t: (b, 0, 0)),
            pl.BlockSpec((1, PAST_LEN, KV_DIM), lambda b, t: (b, 0, 0)),
            pl.BlockSpec((Q_DIM, D), lambda b, t: (0, 0)),
            pl.BlockSpec((SB, D), lambda b, t: (NSBP + b * SB_PER_DEC + t, 0)),
            pl.BlockSpec((1, 1, 6 * D), lambda b, t: (1 + b, 0, 0)),
            pl.BlockSpec(memory_space=pl.ANY),
        ],
        out_specs=pl.BlockSpec((SB, D), lambda b, t: (NSBP + b * SB_PER_DEC + t, 0)),
        scratch_shapes=[pltpu.VMEM((SB, Q_DIM), BF16)],
        compiler_params=_cparams(2),
        name="attn_lat",
    )(q, kb, vb, cache_k, cache_v, w_o, y, mods, y_ctx)


def _log_sigmoid(x):
    return jnp.minimum(x, 0.0) - jnp.log(1.0 + jnp.exp(-jnp.abs(x)))


W_T = 256


def _mlstm_in_kernel(mr_ref, y_ref, mod_ref, g_ref, wt_ref, bg_ref, q_ref, k_ref, v_ref, o_ref, gt_ref, gr_ref,
                     w_ref, wg_ref):
    @pl.when(pl.program_id(0) == 0)
    def _():
        for r in range(0, 4 * D, W_T):
            w_ref[:, r:r + W_T] = jnp.transpose(wt_ref[0, r:r + W_T, :]).astype(BF16)
        n_out = 4 * D + 4 * M_HEADS
        tail = jnp.transpose(wt_ref[0, n_out - LANE:n_out, :])
        lane = lax.broadcasted_iota(I32, tail.shape, 1)
        wg_ref[...] = jnp.where(lane < 4 * M_HEADS, pltpu.roll(tail, 4 * M_HEADS, axis=1), 0.0).astype(BF16)

    h = _norm_mod(y_ref[...], g_ref[...], mod_ref, 0)
    hb = h.astype(BF16)
    q_ref[...] = jnp.dot(hb, w_ref[:, 0:D], preferred_element_type=F32).astype(BF16)
    k_ref[...] = (jnp.dot(hb, w_ref[:, D:2 * D], preferred_element_type=F32) * (M_HEAD_DIM ** -0.5)).astype(BF16)
    v_ref[...] = jnp.dot(hb, w_ref[:, 2 * D:3 * D], preferred_element_type=F32).astype(BF16)
    o_ref[...] = _sigmoid(jnp.dot(hb, w_ref[:, 3 * D:4 * D], preferred_element_type=F32))
    gt = jnp.dot(hb, wg_ref[...], preferred_element_type=F32) + bg_ref[...]
    lane = lax.broadcasted_iota(I32, gt.shape, 1)
    is_f = ((lane >= M_HEADS) & (lane < 2 * M_HEADS)) | ((lane >= 3 * M_HEADS) & (lane < 4 * M_HEADS))
    gt = jnp.where(is_f, _log_sigmoid(gt), gt)
    gt_ref[...] = gt
    per = LANE // M_CHUNK
    for p in range(TM // LANE):
        t = jnp.transpose(gt[p * LANE:(p + 1) * LANE, :])
        for f in range(per):
            gr_ref[p * per + f] = t[0:4 * M_HEADS, f * M_CHUNK:(f + 1) * M_CHUNK]


def _mlstm_in(y, mods, mrow, g1, w_in_all, slot, b_gate):
    w_t = jnp.swapaxes(w_in_all, 1, 2)
    w_spec = pl.BlockSpec((1,) + w_t.shape[1:], lambda j, *_: (slot, 0, 0), pipeline_mode=pl.Buffered(1))
    return pl.pallas_call(
        _mlstm_in_kernel,
        out_shape=(jax.ShapeDtypeStruct((N_TOK, D), BF16), jax.ShapeDtypeStruct((N_TOK, D), BF16),
                   jax.ShapeDtypeStruct((N_TOK, D), BF16), jax.ShapeDtypeStruct((N_TOK, D), F32),
                   jax.ShapeDtypeStruct((N_TOK, LANE), F32),
                   jax.ShapeDtypeStruct((N_TOK // M_CHUNK, 4 * M_HEADS, M_CHUNK), F32)),
        grid_spec=pltpu.PrefetchScalarGridSpec(
            num_scalar_prefetch=1, grid=(NB,),
            in_specs=[_tok_spec(D), _mod_spec(), _full_spec((1, D)), w_spec, _full_spec((1, LANE))],
            out_specs=(_tok_spec(D), _tok_spec(D), _tok_spec(D), _tok_spec(D), _tok_spec(LANE),
                       pl.BlockSpec((TM // M_CHUNK, 4 * M_HEADS, M_CHUNK), lambda j, *_: (j, 0, 0))),
            scratch_shapes=[pltpu.VMEM((D, 4 * D), BF16), pltpu.VMEM((D, LANE), BF16)]),
        compiler_params=_cparams(),
        name="mlstm_in",
    )(mrow, y, mods, g1, w_t, b_gate)


def _mlstm_load(hd, c, q_ref, k_ref, v_ref, gc_ref, gr_ref):
    r0 = pl.multiple_of(c * M_CHUNK, M_CHUNK)
    hs = slice(hd * M_HEAD_DIM, (hd + 1) * M_HEAD_DIM)
    rows = pl.ds(r0, M_CHUNK)
    return rows, hs, q_ref[rows, hs], k_ref[rows, hs], v_ref[rows, hs], gc_ref[rows, :], gr_ref[c]


def _mlstm_chunks(chains, ms, loaded, c_scr, n_scr):
    L = M_CHUNK
    n = range(len(chains))
    t_idx = lax.broadcasted_iota(I32, (L, L), 0)
    s_idx = lax.broadcasted_iota(I32, (L, L), 1)
    masks = {0: (s_idx <= t_idx, t_idx <= s_idx), 1: (s_idx >= t_idx, t_idx >= s_idx)}
    q = [ld[2] for ld in loaded]
    k = [ld[3] for ld in loaded]
    v = [ld[4] for ld in loaded]
    gi = [2 * d * M_HEADS + hd for hd, d in chains]
    gf = [(2 * d + 1) * M_HEADS + hd for hd, d in chains]
    i_col = [ld[5][:, gi[i]:gi[i] + 1] for i, ld in enumerate(loaded)]
    lf_col = [ld[5][:, gf[i]:gf[i] + 1] for i, ld in enumerate(loaded)]
    i_row = [ld[6][gi[i]:gi[i] + 1, :] for i, ld in enumerate(loaded)]
    lf_row = [ld[6][gf[i]:gf[i] + 1, :] for i, ld in enumerate(loaded)]
    mask = [masks[d][0] for _, d in chains]
    mask_t = [masks[d][1] for _, d in chains]
    b_col = [jnp.sum(jnp.where(mask[i], lf_row[i], 0.0), axis=1, keepdims=True) for i in n]
    b_row = [jnp.sum(jnp.where(mask_t[i], lf_col[i], 0.0), axis=0, keepdims=True) for i in n]
    log_d = [jnp.where(mask[i], b_col[i] - b_row[i] + i_row[i], -jnp.inf) for i in n]
    li = [b_col[i] + ms[i] for i in n]
    m_r = [jnp.maximum(li[i], jnp.max(log_d[i], axis=1, keepdims=True)) for i in n]
    a_int = [jnp.exp(li[i] - m_r[i]) for i in n]
    dmat = [jnp.exp(log_d[i] - m_r[i]) for i in n]
    cmat = [c_scr[d, hd] for hd, d in chains]
    nvec = [n_scr[d, hd] for hd, d in chains]
    gram = [lax.dot_general(q[i], k[i], (((1,), (1,)), ((), ())), preferred_element_type=F32) for i in n]
    inter = [jnp.dot(q[i], cmat[i].astype(BF16), preferred_element_type=F32) for i in n]
    s = [gram[i] * dmat[i] for i in n]
    intra = [jnp.dot(s[i].astype(BF16), v[i], preferred_element_type=F32) for i in n]
    qn = [jnp.sum(q[i].astype(F32) * nvec[i], axis=1, keepdims=True) for i in n]
    den = [a_int[i] * qn[i] + jnp.sum(s[i], axis=1, keepdims=True) for i in n]
    hh = [(a_int[i] * inter[i] + intra[i]) / jnp.maximum(jnp.abs(den[i]), jnp.exp(-m_r[i])) for i in n]
    b_last = [b_row[i][:, L - 1:L] if chains[i][1] == 0 else b_row[i][:, 0:1] for i in n]
    log_w = [b_last[i] - b_col[i] + i_col[i] for i in n]
    m_new = [jnp.maximum(b_last[i] + ms[i], jnp.max(log_w[i], axis=0, keepdims=True)) for i in n]
    w = [jnp.exp(log_w[i] - m_new[i]) for i in n]
    decay = [jnp.exp(b_last[i] + ms[i] - m_new[i]) for i in n]
    kw = [k[i].astype(F32) * w[i] for i in n]
    kv = [lax.dot_general(kw[i].astype(BF16), v[i], (((0,), (0,)), ((), ())), preferred_element_type=F32) for i in n]
    for i, (hd, d) in enumerate(chains):
        c_scr[d, hd] = decay[i] * cmat[i] + kv[i]
        n_scr[d, hd] = decay[i] * nvec[i] + jnp.sum(kw[i], axis=0, keepdims=True)
    return hh, m_new


SCAN_GROUP = 2 * M_HEADS


def _mlstm_scan_body(n_chunks, q_ref, k_ref, v_ref, gc_ref, gr_ref, h_ref, hb_scr, c_scr, n_scr, m0):
    chains = [(hd, d) for hd in range(M_HEADS) for d in range(2)]

    def body(c, ms):
        out = []
        for g0 in range(0, len(chains), SCAN_GROUP):
            grp = chains[g0:g0 + SCAN_GROUP]
            loaded = [_mlstm_load(hd, c if d == 0 else n_chunks - 1 - c, q_ref, k_ref, v_ref, gc_ref, gr_ref)
                      for hd, d in grp]
            hh, m_new = _mlstm_chunks(grp, ms[g0:g0 + SCAN_GROUP], loaded, c_scr, n_scr)
            for (hd, d), ld, h in zip(grp, loaded, hh):
                dst = h_ref if d == 0 else hb_scr
                dst[ld[0], ld[1]] = h
            out += m_new
        return tuple(out)

    ms = lax.fori_loop(0, n_chunks, body, tuple(m0))
    h_ref[...] += hb_scr[...]
    return ms


def _mlstm_scan_ctx_kernel(q_ref, k_ref, v_ref, gc_ref, gr_ref, h_ref, cn_ref, nn_ref, mn_ref, hb_scr, c_scr, n_scr):
    c_scr[...] = jnp.zeros(c_scr.shape, F32)
    n_scr[...] = jnp.zeros(n_scr.shape, F32)
    zero = jnp.zeros((1, 1), F32)
    ms = _mlstm_scan_body(SEQ // M_CHUNK, q_ref, k_ref, v_ref, gc_ref, gr_ref, h_ref, hb_scr, c_scr, n_scr,
                          [zero] * (2 * M_HEADS))
    cn_ref[0] = c_scr[...]
    nn_ref[0] = n_scr[...]
    for hd in range(M_HEADS):
        for d in range(2):
            mn_ref[0, d, hd] = jnp.broadcast_to(ms[2 * hd + d], (1, LANE))


def _mlstm_scan_lat_kernel(q_ref, k_ref, v_ref, gc_ref, gr_ref, c0_ref, n0_ref, m0_ref, ctx_out_ref, h_ref,
                           hb_scr, c_scr, n_scr):
    del ctx_out_ref
    c_scr[...] = c0_ref[0]
    n_scr[...] = n0_ref[0]
    m0 = [m0_ref[0, d, hd] for hd in range(M_HEADS) for d in range(2)]
    _mlstm_scan_body(DEC_SEQ // M_CHUNK, q_ref, k_ref, v_ref, gc_ref, gr_ref, h_ref, hb_scr, c_scr, n_scr, m0)


def _mlstm_scan(q, k, v, gcol, grow, state_c, state_n, state_m):
    hd = M_HEAD_DIM
    ng = 4 * M_HEADS
    state_scratch = [pltpu.VMEM((2, M_HEADS, hd, hd), F32), pltpu.VMEM((2, M_HEADS, 1, hd), F32)]
    ncp = SEQ // M_CHUNK
    h_ctx, new_c, new_n, new_m = pl.pallas_call(
        _mlstm_scan_ctx_kernel,
        out_shape=(jax.ShapeDtypeStruct((N_TOK, D), F32),
                   jax.ShapeDtypeStruct((BATCH, 2, M_HEADS, hd, hd), F32),
                   jax.ShapeDtypeStruct((BATCH, 2, M_HEADS, 1, hd), F32),
                   jax.ShapeDtypeStruct((BATCH, 2, M_HEADS, 1, LANE), F32)),
        grid=(BATCH,),
        in_specs=[
            pl.BlockSpec((SEQ, D), lambda s: (s, 0)),
            pl.BlockSpec((SEQ, D), lambda s: (s, 0)),
            pl.BlockSpec((SEQ, D), lambda s: (s, 0)),
            pl.BlockSpec((SEQ, LANE), lambda s: (s, 0)),
            pl.BlockSpec((ncp, ng, M_CHUNK), lambda s: (s, 0, 0)),
        ],
        out_specs=(
            pl.BlockSpec((SEQ, D), lambda s: (s, 0)),
            pl.BlockSpec((1, 2, M_HEADS, hd, hd), lambda s: (s, 0, 0, 0, 0)),
            pl.BlockSpec((1, 2, M_HEADS, 1, hd), lambda s: (s, 0, 0, 0, 0)),
            pl.BlockSpec((1, 2, M_HEADS, 1, LANE), lambda s: (s, 0, 0, 0, 0)),
        ),
        scratch_shapes=[pltpu.VMEM((SEQ, D), F32)] + state_scratch,
        compiler_params=_cparams(),
        name="mlstm_scan_ctx",
    )(q, k, v, gcol, grow)
    ncl = DEC_SEQ // M_CHUNK
    pb = NP_TOK // DEC_SEQ
    h_all = pl.pallas_call(
        _mlstm_scan_lat_kernel,
        out_shape=jax.ShapeDtypeStruct((N_TOK, D), F32),
        input_output_aliases={8: 0},
        grid=(DEC_BATCH,),
        in_specs=[
            pl.BlockSpec((DEC_SEQ, D), lambda b: (pb + b, 0)),
            pl.BlockSpec((DEC_SEQ, D), lambda b: (pb + b, 0)),
            pl.BlockSpec((DEC_SEQ, D), lambda b: (pb + b, 0)),
            pl.BlockSpec((DEC_SEQ, LANE), lambda b: (pb + b, 0)),
            pl.BlockSpec((ncl, ng, M_CHUNK), lambda b: (pb + b, 0, 0)),
            pl.BlockSpec((1, 2, M_HEADS, hd, hd), lambda b: (b, 0, 0, 0, 0)),
            pl.BlockSpec((1, 2, M_HEADS, 1, hd), lambda b: (b, 0, 0, 0, 0)),
            pl.BlockSpec((1, 2, M_HEADS, 1, 1), lambda b: (b, 0, 0, 0, 0)),
            pl.BlockSpec(memory_space=pl.ANY),
        ],
        out_specs=pl.BlockSpec((DEC_SEQ, D), lambda b: (pb + b, 0)),
        scratch_shapes=[pltpu.VMEM((DEC_SEQ, D), F32)] + state_scratch,
        compiler_params=_cparams(),
        name="mlstm_scan_lat",
    )(q, k, v, gcol, grow, state_c, state_n, state_m, h_ctx)
    return h_all, new_c, new_n, new_m


def _mlstm_out_kernel(mr_ref, h_ref, o_ref, ng_ref, wf_ref, y_ref, mod_ref, out_ref, x_scr, w_ref):
    @pl.when(pl.program_id(0) == 0)
    def _():
        w_ref[...] = wf_ref[0].astype(BF16)

    hc = o_ref[...] * h_ref[...]
    for hd in range(M_HEADS):
        sl = slice(hd * M_HEAD_DIM, (hd + 1) * M_HEAD_DIM)
        x_scr[:, sl] = _rms(hc[:, sl], ng_ref[:, sl]).astype(BF16)
    out = jnp.dot(x_scr[...], w_ref[...], preferred_element_type=F32)
    out_ref[...] = y_ref[...] + _mod_vec(mod_ref, 2) * out


def _mlstm_out(hsum, o, norm_g, w_out_all, slot, y, mods, mrow):
    return pl.pallas_call(
        _mlstm_out_kernel,
        out_shape=jax.ShapeDtypeStruct((N_TOK, D), F32),
        grid_spec=pltpu.PrefetchScalarGridSpec(
            num_scalar_prefetch=1, grid=(NL,),
            in_specs=[_tok_spec(D, TL), _tok_spec(D, TL), _full_spec((1, D)), _resident_weight_spec(w_out_all, slot),
                      _tok_spec(D, TL),
                      _mod_spec()],
            out_specs=_tok_spec(D, TL),
            scratch_shapes=[pltpu.VMEM((TL, D), BF16), pltpu.VMEM((D, D), BF16)]),
        compiler_params=_cparams(),
        name="mlstm_out",
    )(mrow, hsum, o, norm_g, w_out_all, y, mods)


ROUTE_OFF = N_GROUPS
SLAB = D // (2 * LANE)
V7X_SC_CORES = 2
V7X_SC_SUBCORES = 16
SC_WORKERS = V7X_SC_CORES * V7X_SC_SUBCORES
SC_WINDOW = 128
SC_HALF = SC_WINDOW // 2
HI_MASK = -65536


def _bf16_bits(x):
    return lax.bitcast_convert_type(x.astype(BF16).astype(F32), I32)


def _store_slabs(ref, x):
    rows = x.shape[0]
    for c in range(SLAB):
        lo = lax.shift_right_logical(_bf16_bits(x[:, (2 * c) * LANE:(2 * c + 1) * LANE]), 16)
        hi = _bf16_bits(x[:, (2 * c + 1) * LANE:(2 * c + 2) * LANE]) & HI_MASK
        ref[pl.ds(c, rows, stride=SLAB), :] = lo | hi


def _load_slabs(ref, dst, rows, dtype):
    for c in range(SLAB):
        w = ref[pl.ds(c, rows, stride=SLAB), :]
        lo = lax.bitcast_convert_type(lax.shift_left(w, 16), F32)
        hi = lax.bitcast_convert_type(w & HI_MASK, F32)
        dst[:, (2 * c) * LANE:(2 * c + 1) * LANE] = lo.astype(dtype)
        dst[:, (2 * c + 1) * LANE:(2 * c + 2) * LANE] = hi.astype(dtype)


def _route_kernel(mr_ref, y_ref, mod_ref, g_ref, wr_ref, br_ref, tri_ref, x_ref, wt_ref, dest_ref, cnt_ref,
                  cnt_scr, meta_scr):
    x = _norm_mod(y_ref[...], g_ref[...], mod_ref, 1)
    _store_slabs(x_ref, x)
    lg = jnp.dot(x.astype(BF16), wr_ref[...], preferred_element_type=F32) + br_ref[...]
    lane = lax.broadcasted_iota(I32, lg.shape, 1).astype(F32)
    ninf = -jnp.inf
    big = float(LANE)
    lgg = jnp.where(lane < N_GROUPS, lg, ninf)
    gmax = jnp.max(lgg, axis=-1, keepdims=True)
    g_idx = jnp.min(jnp.where(lgg == gmax, lane, big), axis=-1, keepdims=True)
    g_w = 1.0 / jnp.sum(jnp.exp(lgg - gmax), axis=-1, keepdims=True)
    lo = ROUTE_OFF + g_idx * EXPERTS_PER_GROUP
    le = jnp.where((lane >= lo) & (lane < lo + EXPERTS_PER_GROUP), lg, ninf)
    m1 = jnp.max(le, axis=-1, keepdims=True)
    i1 = jnp.min(jnp.where(le == m1, lane, big), axis=-1, keepdims=True)
    le2 = jnp.where(lane == i1, ninf, le)
    m2 = jnp.max(le2, axis=-1, keepdims=True)
    i2 = jnp.min(jnp.where(le2 == m2, lane, big), axis=-1, keepdims=True)
    r = jnp.exp(m2 - m1)
    p1 = 1.0 / (1.0 + r)
    p2 = r / (1.0 + r)
    two = lax.broadcasted_iota(I32, (x.shape[0], TOP_K), 1)
    wt_ref[...] = jnp.where(two == 0, g_w * p1, g_w * p2)
    @pl.when(pl.program_id(0) == 0)
    def _():
        cnt_scr[...] = jnp.zeros(cnt_scr.shape, F32)

    oh1 = (lane == i1).astype(F32)
    oh2 = (lane == i2).astype(F32)
    both = oh1 + oh2
    before = jnp.dot(tri_ref[...], both.astype(BF16), preferred_element_type=F32) + cnt_scr[...]
    rk1 = jnp.sum(oh1 * before, axis=-1, keepdims=True)
    rk2 = jnp.sum(oh2 * before, axis=-1, keepdims=True)
    cnt_scr[...] = cnt_scr[...] + jnp.sum(both, axis=0, keepdims=True)
    cnt_ref[...] = cnt_scr[...]
    cols = (i1 - ROUTE_OFF, i2 - ROUTE_OFF, rk1, rk2)
    packed = jnp.zeros(lg.shape, F32)
    for c, val in enumerate(cols):
        packed = jnp.where(lane == c, val, packed)
    j = pl.program_id(0)
    meta_scr[:, pl.ds(pl.multiple_of(j * TL, TL), TL)] = jnp.transpose(packed)[0:len(cols), :].astype(I32)

    @pl.when(j == NL - 1)
    def _():
        sub8 = (SUBLANE, LANE)
        ln = lax.broadcasted_iota(I32, sub8, 1)
        counts = jnp.broadcast_to(cnt_scr[...], sub8)
        is_e = (ln >= ROUTE_OFF) & (ln < ROUTE_OFF + N_EXPERTS)
        padded = jnp.where(is_e, jnp.floor((counts + (EBLK - 1)) * (1.0 / EBLK)) * EBLK, 0.0)
        inc = padded
        sh = 1
        while sh < LANE:
            inc = inc + jnp.where(ln >= sh, pltpu.roll(inc, sh, 1), 0.0)
            sh *= 2
        start = pltpu.roll(inc - padded, LANE - ROUTE_OFF, 1)
        table = jnp.transpose(jnp.broadcast_to(start[0:1, :], (LANE, LANE)))[0:N_EXPERTS, 0:1].astype(I32)
        sub = lax.broadcasted_iota(I32, (N_EXPERTS, SLOT_COLS), 0)
        for c0 in range(0, N_TOK, SLOT_COLS):
            meta = meta_scr[:, c0:c0 + SLOT_COLS]
            for k in range(TOP_K):
                first = jnp.sum(jnp.where(sub == meta[k:k + 1, :], table, 0), axis=0, keepdims=True)
                dest_ref[k:k + 1, c0:c0 + SLOT_COLS] = first + meta[TOP_K + k:TOP_K + k + 1, :]


SLOT_COLS = 2048


def _route(y, mods, mrow, g2, w_route, b_route):
    return pl.pallas_call(
        _route_kernel,
        out_shape=(jax.ShapeDtypeStruct((N_TOK * SLAB, LANE), I32), jax.ShapeDtypeStruct((N_TOK, TOP_K), F32),
                   jax.ShapeDtypeStruct((TOP_K, N_TOK), I32), jax.ShapeDtypeStruct((1, LANE), F32)),
        grid_spec=pltpu.PrefetchScalarGridSpec(
            num_scalar_prefetch=1, grid=(NL,),
            in_specs=[_tok_spec(D, TL), _mod_spec(), _full_spec((1, D)), _full_spec((D, LANE)),
                      _full_spec((1, LANE)), _full_spec((TL, TL))],
            out_specs=(pl.BlockSpec((TL * SLAB, LANE), lambda j, *_: (j, 0)), _tok_spec(TOP_K, TL),
                       _full_spec((TOP_K, N_TOK)), _full_spec((1, LANE))),
            scratch_shapes=[pltpu.VMEM((1, LANE), F32), pltpu.VMEM((2 * TOP_K, N_TOK), I32)]),
        compiler_params=_cparams(),
        name="moe_route",
    )(mrow, y, mods, g2, w_route, b_route, jnp.asarray(np.tril(np.ones((TL, TL), np.float32), -1), dtype=BF16))


def _block_tables_of(lane_counts):
    counts = lane_counts[0, ROUTE_OFF:ROUTE_OFF + N_EXPERTS].astype(I32)
    padded = ((counts + EBLK - 1) // EBLK) * EBLK
    pad_end = jnp.cumsum(padded)
    pad_start = pad_end - padded
    n_blk = (padded // EBLK).astype(I32)
    blk_start = (pad_start // EBLK).astype(I32)
    n_used = (pad_end[-1] // EBLK).astype(I32).reshape(1)
    return blk_start, n_blk, n_used


def _sc_mesh():
    return plsc.VectorSubcoreMesh(core_axis_name="core", subcore_axis_name="subcore",
                                  num_cores=V7X_SC_CORES, num_subcores=V7X_SC_SUBCORES)


def _sc_worker():
    return lax.axis_index("core") * V7X_SC_SUBCORES + lax.axis_index("subcore")


def _sc_dispatch(x_slabs, dest):
    per = N_TOK // SC_WORKERS
    n_win = per // SC_HALF

    @functools.partial(
        pl.kernel, out_type=jax.ShapeDtypeStruct((P_SLOTS, SLAB, LANE), I32), mesh=_sc_mesh(), name="moe_dispatch",
        scratch_types=[pltpu.VMEM((1, per), I32), pltpu.VMEM((1, per), I32), pltpu.VMEM((2, SC_HALF, SLAB, LANE), I32),
                       pltpu.SemaphoreType.DMA((2,)), pltpu.SemaphoreType.DMA((2,))])
    def run(x_hbm, d_hbm, o_hbm, i0_v, i1_v, buf, lsem, ssem):
        base = _sc_worker() * per
        pltpu.sync_copy(d_hbm.at[pl.ds(0, 1), pl.ds(base, per)], i0_v)
        pltpu.sync_copy(d_hbm.at[pl.ds(1, 1), pl.ds(base, per)], i1_v)
        loads = [pltpu.make_async_copy(x_hbm.at[pl.ds(base + s * SC_HALF, SC_HALF)], buf.at[s % 2], lsem.at[s % 2])
                 for s in range(n_win)]
        loads[0].start()
        for s in range(n_win):
            loads[s].wait()
            if s + 1 < n_win:
                loads[s + 1].start()
            win = pl.ds(s * SC_HALF, SC_HALF)
            outs = [pltpu.make_async_copy(buf.at[s % 2], o_hbm.at[iv.at[0, win]], ssem.at[a])
                    for a, iv in enumerate((i0_v, i1_v))]
            for cp in outs:
                cp.start()
            for cp in outs:
                cp.wait()

    return run(x_slabs.reshape(N_TOK, SLAB, LANE), dest)


def _sc_collect(y_slabs, dest):
    per = N_ASSIGN // SC_WORKERS
    n_win = per // SC_HALF

    @functools.partial(
        pl.kernel, out_type=jax.ShapeDtypeStruct((N_ASSIGN, SLAB, LANE), I32), mesh=_sc_mesh(), name="moe_collect",
        scratch_types=[pltpu.VMEM((1, per), I32), pltpu.VMEM((2, SC_HALF, SLAB, LANE), I32),
                       pltpu.SemaphoreType.DMA((2,)), pltpu.SemaphoreType.DMA((2,))])
    def run(y_hbm, i_hbm, o_hbm, i_v, buf, gsem, wsem):
        w = _sc_worker()
        base = w * per
        per_a = SC_WORKERS // TOP_K
        pltpu.sync_copy(i_hbm.at[pl.ds(w // per_a, 1), pl.ds((w % per_a) * per, per)], i_v)
        gathers = [pltpu.make_async_copy(y_hbm.at[i_v.at[0, pl.ds(s * SC_HALF, SC_HALF)]], buf.at[s % 2],
                                         gsem.at[s % 2]) for s in range(n_win)]
        writes = [pltpu.make_async_copy(buf.at[s % 2], o_hbm.at[pl.ds(base + s * SC_HALF, SC_HALF)], wsem.at[s % 2])
                  for s in range(n_win)]
        gathers[0].start()
        for s in range(n_win):
            gathers[s].wait()
            if s >= 1:
                writes[s - 1].wait()
            if s + 1 < n_win:
                gathers[s + 1].start()
            writes[s].start()
        writes[n_win - 1].wait()

    return run(y_slabs.reshape(P_SLOTS, SLAB, LANE), dest)


EROWS = EBLK * SLAB


def _expert_kernel(bs_ref, nb_ref, nu_ref, wg_ref, wu_ref, wd_ref, x_hbm, y_hbm,
                   xbuf, ybuf, xs, wg_bf, wu_bf, wd_bf, isem, osem):
    e = pl.program_id(0)
    n_exp = pl.num_programs(0)
    n_used = nu_ref[0]
    b0 = bs_ref[e]
    nb = nb_ref[e]

    def in_copy(g, slot):
        return pltpu.make_async_copy(x_hbm.at[pl.ds(pl.multiple_of(g * EROWS, EROWS), EROWS)], xbuf.at[slot],
                                     isem.at[slot])

    def out_copy(g, slot):
        return pltpu.make_async_copy(ybuf.at[slot], y_hbm.at[pl.ds(pl.multiple_of(g * EROWS, EROWS), EROWS)],
                                     osem.at[slot])

    @pl.when(e == 0)
    def _():
        in_copy(0, 0).start()

    @pl.when(nb > 0)
    def _():
        wg_bf[...] = wg_ref[0, 0].astype(BF16)
        wu_bf[...] = wu_ref[0, 0].astype(BF16)
        wd_bf[...] = wd_ref[0, 0].astype(BF16)

    def block(k, carry):
        g = b0 + k
        slot = lax.rem(g, 2)
        in_copy(g, slot).wait()

        @pl.when(g + 1 < n_used)
        def _():
            in_copy(g + 1, 1 - slot).start()

        _load_slabs(xbuf.at[slot], xs, EBLK, BF16)
        xb = xs[...]
        gt = jnp.dot(xb, wg_bf[...], preferred_element_type=F32)
        up = jnp.dot(xb, wu_bf[...], preferred_element_type=F32)
        hmid = (gt * _sigmoid(gt) * up).astype(BF16)
        res = jnp.dot(hmid, wd_bf[...], preferred_element_type=F32)

        @pl.when(g >= 2)
        def _():
            out_copy(g - 2, slot).wait()

        _store_slabs(ybuf.at[slot], res)
        out_copy(g, slot).start()
        return carry

    lax.fori_loop(0, nb, block, 0)

    @pl.when(e == n_exp - 1)
    def _():
        last = n_used - 1
        out_copy(last, lax.rem(last, 2)).wait()

        @pl.when(n_used >= 2)
        def _():
            out_copy(last - 1, lax.rem(last - 1, 2)).wait()


def _experts(x_sorted, blk_start, n_blk, n_used, w_gate, w_up, w_down, layer):
    any_spec = pl.BlockSpec(memory_space=pl.ANY)
    wspec = lambda r, c: pl.BlockSpec((1, 1, r, c), lambda e, *_: (layer, e, 0, 0))
    return pl.pallas_call(
        _expert_kernel,
        out_shape=jax.ShapeDtypeStruct((P_SLOTS * SLAB, LANE), I32),
        grid_spec=pltpu.PrefetchScalarGridSpec(
            num_scalar_prefetch=3, grid=(N_EXPERTS,),
            in_specs=[wspec(D, D_EXPERT), wspec(D, D_EXPERT), wspec(D_EXPERT, D), any_spec],
            out_specs=any_spec,
            scratch_shapes=[
                pltpu.VMEM((2, EROWS, LANE), I32), pltpu.VMEM((2, EROWS, LANE), I32),
                pltpu.VMEM((EBLK, D), BF16),
                pltpu.VMEM((D, D_EXPERT), BF16), pltpu.VMEM((D, D_EXPERT), BF16), pltpu.VMEM((D_EXPERT, D), BF16),
                pltpu.SemaphoreType.DMA((2,)), pltpu.SemaphoreType.DMA((2,)),
            ]),
        compiler_params=_cparams(),
        name="moe_experts",
    )(blk_start, n_blk, n_used, w_gate, w_up, w_down, x_sorted.reshape(P_SLOTS * SLAB, LANE))


def _combine_kernel(final, mr_ref, e0_ref, e1_ref, wt_ref, y_ref, mod_ref, fg_ref, o_ref, a_scr, b_scr):
    _load_slabs(e0_ref, a_scr, TL, F32)
    _load_slabs(e1_ref, b_scr, TL, F32)
    wt = wt_ref[...]
    moe = wt[:, 0:1] * a_scr[...] + wt[:, 1:2] * b_scr[...]
    y_new = y_ref[...] + _mod_vec(mod_ref, 5) * moe
    o_ref[...] = _rms(y_new, fg_ref[...]) if final else y_new


def _combine(ym, wts, y, mods, mrow, final_g, blk0, nblk, final):
    tok = lambda width: pl.BlockSpec((TL, width), lambda j, *_: (blk0 + j, 0))
    slab0 = pl.BlockSpec((TL * SLAB, LANE), lambda j, *_: (blk0 + j, 0))
    slab1 = pl.BlockSpec((TL * SLAB, LANE), lambda j, *_: (NL + blk0 + j, 0))
    mod = pl.BlockSpec((1, 1, 6 * D), lambda j, mr: (mr[blk0 + j], 0, 0))
    return pl.pallas_call(
        functools.partial(_combine_kernel, final),
        out_shape=jax.ShapeDtypeStruct((nblk * TL, D), F32),
        grid_spec=pltpu.PrefetchScalarGridSpec(
            num_scalar_prefetch=1, grid=(nblk,),
            in_specs=[slab0, slab1, tok(TOP_K), tok(D), mod, _full_spec((1, D))],
            out_specs=pl.BlockSpec((TL, D), lambda j, *_: (j, 0)),
            scratch_shapes=[pltpu.VMEM((TL, D), F32), pltpu.VMEM((TL, D), F32)]),
        compiler_params=_cparams(),
        name="moe_combine",
    )(mrow, ym, ym, wts, y, mods, final_g)


def _combine_conv_in_kernel(mr_ref, e0_ref, e1_ref, wt_ref, y_ref, mod_ref, modn_ref, g_ref, wf_ref, o_ref, u_ref,
                            a_scr, b_scr, w_ref):
    @pl.when(pl.program_id(0) == 0)
    def _():
        w_ref[...] = wf_ref[0].astype(BF16)

    _load_slabs(e0_ref, a_scr, TM, F32)
    _load_slabs(e1_ref, b_scr, TM, F32)
    wt = wt_ref[...]
    moe = wt[:, 0:1] * a_scr[...] + wt[:, 1:2] * b_scr[...]
    y_new = y_ref[...] + _mod_vec(mod_ref, 5) * moe
    o_ref[...] = y_new
    h = _norm_mod(y_new, g_ref[...], modn_ref, 0)
    ag = jnp.dot(h.astype(BF16), w_ref[...], preferred_element_type=F32)
    u_ref[...] = ag[:, :D] * _sigmoid(ag[:, D:])


def _combine_conv_in(ym, wts, y, mods, mods_next, mrow, g1_next, w_in_all, slot):
    slab0 = pl.BlockSpec((TM * SLAB, LANE), lambda j, *_: (j, 0))
    slab1 = pl.BlockSpec((TM * SLAB, LANE), lambda j, *_: (NB + j, 0))
    return pl.pallas_call(
        _combine_conv_in_kernel,
        out_shape=(jax.ShapeDtypeStruct((N_TOK, D), F32), jax.ShapeDtypeStruct((N_TOK, D), F32)),
        grid_spec=pltpu.PrefetchScalarGridSpec(
            num_scalar_prefetch=1, grid=(NB,),
            in_specs=[slab0, slab1, _tok_spec(TOP_K), _tok_spec(D), _mod_spec(), _mod_spec(), _full_spec((1, D)),
                      _resident_weight_spec(w_in_all, slot)],
            out_specs=(_tok_spec(D), _tok_spec(D)),
            scratch_shapes=[pltpu.VMEM((TM, D), F32), pltpu.VMEM((TM, D), F32), pltpu.VMEM((D, 2 * D), BF16)]),
        compiler_params=_cparams(),
        name="moe_combine_conv_in",
    )(mrow, ym, ym, wts, y, mods, mods_next, g1_next, w_in_all)


def kernel(x_prompt, x_sample, cache_attn_k, cache_attn_v, state_mlstm_C, state_mlstm_n, state_mlstm_m, c, c_ctx, ada_w, ada_b, norm1_g, norm2_g, conv_w_in, conv_w_dw, conv_b_dw, conv_ln_g, conv_ln_b, conv_w_out, attn_w_qkv, attn_q_norm, attn_k_norm, attn_w_o, mlstm_w_in, mlstm_b_gate, mlstm_norm_g, mlstm_w_out, moe_w_group, moe_b_group, moe_w_router, moe_b_router, moe_w_gate, moe_w_up, moe_w_down, final_norm_g):
    y = None
    cvec = jnp.concatenate([c_ctx[None, :], c, jnp.zeros((MOD_ROWS - 1 - DEC_BATCH, D), F32)], axis=0)
    rope = _rope_blocks()
    mrow, mrow_sb, mrow_l = jnp.asarray(_MOD_ROW), jnp.asarray(_MOD_ROW_SB), jnp.asarray(_MOD_ROW_L)
    new_k = new_v = new_c = new_n = new_m = None
    mods_all = [_ada_layer(cvec, ada_w, ada_b, i) for i in range(DEPTH)]
    u = None
    for i in range(DEPTH):
        kind, slot = i % 3, i // 3
        mods = mods_all[i]
        g1 = norm1_g[i].reshape(1, D)
        if kind == 0:
            src = (x_prompt.reshape(NP_TOK, D), x_sample.reshape(NS_TOK, D), True) if i == 0 else (y, y, False)
            if i == 0:
                u = _conv_in(*src, mods, mrow_l, g1, conv_w_in, slot)
            w_dw = jnp.concatenate([conv_w_dw[slot], jnp.zeros((1, D), F32)], axis=0)
            y = _conv_main(u, *src, mods, mrow_sb, w_dw, conv_b_dw[slot].reshape(1, D), conv_ln_g[slot].reshape(1, D),
                           conv_ln_b[slot].reshape(1, D), conv_w_out, slot)
        elif kind == 1:
            q, kb, vb, kf, vf = _attn_qkv(y, mods, mrow_sb, g1, attn_w_qkv, slot,
                                          attn_q_norm[slot].reshape(1, HEAD_DIM), attn_k_norm[slot].reshape(1, HEAD_DIM),
                                          rope)
            new_k = kf[:NP_TOK].reshape(BATCH, 1, SEQ, N_KV_HEADS, HEAD_DIM)
            new_v = vf[:NP_TOK].reshape(BATCH, 1, SEQ, N_KV_HEADS, HEAD_DIM)
            ck = cache_attn_k[:, slot].reshape(DEC_BATCH, PAST_LEN, KV_DIM)
            cv = cache_attn_v[:, slot].reshape(DEC_BATCH, PAST_LEN, KV_DIM)
            y = _attention(q, kb, vb, ck, cv, attn_w_o[slot].astype(BF16), y, mods)
        else:
            b_gate = jnp.concatenate([mlstm_b_gate[slot], jnp.zeros((LANE - 4 * M_HEADS,), F32)]).reshape(1, LANE)
            q, k, v, o, gates, grow = _mlstm_in(y, mods, mrow, g1, mlstm_w_in, slot, b_gate)
            sc = state_mlstm_C[:, slot]
            sn = state_mlstm_n[:, slot].reshape(DEC_BATCH, 2, M_HEADS, 1, M_HEAD_DIM)
            sm = state_mlstm_m[:, slot].reshape(DEC_BATCH, 2, M_HEADS, 1, 1)
            hsum, nc_, nn_, nm_ = _mlstm_scan(q, k, v, gates, grow, sc, sn, sm)
            new_c = nc_[:, None]
            new_n = nn_.reshape(BATCH, 1, 2, M_HEADS, M_HEAD_DIM)
            new_m = nm_[..., 0, 0].reshape(BATCH, 1, 2, M_HEADS)
            y = _mlstm_out(hsum, o, mlstm_norm_g[slot].reshape(1, D), mlstm_w_out, slot, y, mods, mrow_l)
        w_route = jnp.concatenate([moe_w_group[i], moe_w_router[i],
                                   jnp.zeros((D, LANE - N_GROUPS - N_EXPERTS), F32)], axis=1)
        b_route = jnp.concatenate([moe_b_group[i], moe_b_router[i],
                                   jnp.zeros((LANE - N_GROUPS - N_EXPERTS,), F32)]).reshape(1, LANE)
        x2, ewt, dest, cnt = _route(y, mods, mrow_l, norm2_g[i].reshape(1, D), w_route.astype(BF16), b_route)
        blk_start, n_blk, n_used = _block_tables_of(cnt)
        x_sorted = _sc_dispatch(x2, dest)
        y_sorted = _experts(x_sorted, blk_start, n_blk, n_used, moe_w_gate, moe_w_up, moe_w_down, i)
        ym = _sc_collect(y_sorted, dest)
        ym = ym.reshape(N_ASSIGN * SLAB, LANE)
        fg = final_norm_g.reshape(1, D)
        if i + 1 < DEPTH and (i + 1) % 3 == 0:
            y, u = _combine_conv_in(ym, ewt, y, mods, mods_all[i + 1], mrow, norm1_g[i + 1].reshape(1, D),
                                    conv_w_in, (i + 1) // 3)
        elif i + 1 < DEPTH:
            y = _combine(ym, ewt, y, mods, mrow_l, fg, 0, NL, False)
        else:
            y_prompt = _combine(ym, ewt, y, mods, mrow_l, fg, 0, NLP, True).reshape(BATCH, SEQ, D)
            y_sample = _combine(ym, ewt, y, mods, mrow_l, fg, NLP, NL - NLP, True).reshape(DEC_BATCH, DEC_SEQ, D)
    return (y_prompt, y_sample, new_k, new_v, new_c, new_n, new_m)
```

```python
import functools

import jax
import jax.numpy as jnp
import numpy as np
from jax import lax
from jax.experimental import pallas as pl
from jax.experimental.pallas import tpu as pltpu
from jax.experimental.pallas import tpu_sc as plsc

F32 = jnp.float32
BF16 = jnp.bfloat16
I32 = jnp.int32

D = 1024
BATCH, SEQ = 16, 256
DEC_BATCH, DEC_SEQ = 8, 1024
PAST_LEN = 256
DEPTH = 4
GRID_W = 64
EPS = 1e-6
CONV_WIDTH = 31
CONV_PAD = CONV_WIDTH // 2
HEAD_DIM = 128
N_HEADS = 8
N_KV_HEADS = 2
GQA_GROUP = N_HEADS // N_KV_HEADS
Q_DIM = N_HEADS * HEAD_DIM
KV_DIM = N_KV_HEADS * HEAD_DIM
QKV_DIM = Q_DIM + 2 * KV_DIM
ROPE_THETA = 10000.0
M_HEADS = 4
M_HEAD_DIM = D // M_HEADS
M_CHUNK = 64
N_GROUPS = 4
EXPERTS_PER_GROUP = 8
N_EXPERTS = N_GROUPS * EXPERTS_PER_GROUP
TOP_K = 2
D_EXPERT = 512

NP_TOK = BATCH * SEQ
NS_TOK = DEC_BATCH * DEC_SEQ
N_TOK = NP_TOK + NS_TOK
TM = 512
NB = N_TOK // TM
NBP = NP_TOK // TM
BLK_PER_DEC = DEC_SEQ // TM
TL = 1024
NL = N_TOK // TL
NLP = NP_TOK // TL
SB = 256
NSB = N_TOK // SB
NSBP = NP_TOK // SB
SB_PER_DEC = DEC_SEQ // SB
MOD_ROWS = 16
HALO = 16
LANE = 128
SUBLANE = 8

N_ASSIGN = N_TOK * TOP_K
EBLK = 1024
N_EBLK = N_ASSIGN // EBLK + N_EXPERTS
P_SLOTS = N_EBLK * EBLK
N_PAD_SLOTS = P_SLOTS - N_ASSIGN

VMEM_LIMIT = 56 * 1024 * 1024


def _block_tables(nb, nbp, per_dec):
    j = np.arange(nb)
    is_p = j < nbp
    mod_row = np.where(is_p, 0, 1 + (j - nbp) // per_dec)
    rope_idx = np.where(is_p, 0, 1 + (j - nbp) % per_dec)
    first = np.where(is_p, 1, ((j - nbp) % per_dec == 0).astype(np.int64))
    last = np.where(is_p, 1, ((j - nbp) % per_dec == per_dec - 1).astype(np.int64))
    return (mod_row.astype(np.int32), rope_idx.astype(np.int32), first.astype(np.int32), last.astype(np.int32))


_MOD_ROW, _, _, _ = _block_tables(NB, NBP, BLK_PER_DEC)
_MOD_ROW_L, _, _, _ = _block_tables(NL, NLP, DEC_SEQ // TL)
_MOD_ROW_SB, _ROPE_IDX_SB, _SEQ_FIRST, _SEQ_LAST = _block_tables(NSB, NSBP, SB_PER_DEC)


def _cparams(n_axes=1):
    return pltpu.CompilerParams(dimension_semantics=("arbitrary",) * n_axes, vmem_limit_bytes=VMEM_LIMIT)


def _sigmoid(x):
    return 1.0 / (1.0 + jnp.exp(-x))


def _rms(x, g):
    return x * lax.rsqrt(jnp.mean(x * x, axis=-1, keepdims=True) + EPS) * g


def _mod_vec(mod_ref, k):
    return mod_ref[0, :, k * D:(k + 1) * D]


def _norm_mod(y, g, mod_ref, which):
    return _rms(y, g) * (1.0 + _mod_vec(mod_ref, 3 * which + 1)) + _mod_vec(mod_ref, 3 * which)


def _ada_kernel(c_ref, w_ref, b_ref, o_ref):
    c = c_ref[...]
    s = c * _sigmoid(c)
    res = jnp.dot(s.astype(BF16), w_ref[0].astype(BF16), preferred_element_type=F32) + b_ref[0]
    for r in range(MOD_ROWS):
        o_ref[r] = res[r:r + 1, :]


def _ada_layer(cvec, ada_w, ada_b, layer):
    tn = 1536
    return pl.pallas_call(
        _ada_kernel,
        out_shape=jax.ShapeDtypeStruct((MOD_ROWS, 1, 6 * D), F32),
        grid=(6 * D // tn,),
        in_specs=[
            pl.BlockSpec((MOD_ROWS, D), lambda n: (0, 0)),
            pl.BlockSpec((1, D, tn), lambda n: (layer, 0, n)),
            pl.BlockSpec((1, 1, tn), lambda n: (layer, 0, n)),
        ],
        out_specs=pl.BlockSpec((MOD_ROWS, 1, tn), lambda n: (0, 0, n)),
        compiler_params=_cparams(1),
        name="ada_mod",
    )(cvec, ada_w, ada_b.reshape(DEPTH, 1, 6 * D))


def _tok_spec(width, rows=TM):
    return pl.BlockSpec((rows, width), lambda j, *_: (j, 0))


def _mod_spec():
    return pl.BlockSpec((1, 1, 6 * D), lambda j, mr, *_: (mr[j], 0, 0))


def _full_spec(shape):
    nd = len(shape)
    return pl.BlockSpec(shape, lambda j, *_: (0,) * nd)


def _pair_specs(rows, nbp, split):
    s_off = nbp if split else 0
    return [pl.BlockSpec((rows, D), lambda j, *_: (jnp.minimum(j, nbp - 1), 0)),
            pl.BlockSpec((rows, D), lambda j, *_: (jnp.maximum(j, nbp) - s_off, 0))]


def _pair_block(nbp, yp_ref, ys_ref):
    return jnp.where(pl.program_id(0) < nbp, yp_ref[...], ys_ref[...])


def _resident_weight_spec(w_all, slot):
    return pl.BlockSpec((1,) + w_all.shape[1:], lambda j, *_: (slot, 0, 0), pipeline_mode=pl.Buffered(1))


def _conv_in_kernel(mr_ref, yp_ref, ys_ref, mod_ref, g_ref, wf_ref, u_ref, w_ref):
    @pl.when(pl.program_id(0) == 0)
    def _():
        w_ref[...] = wf_ref[0].astype(BF16)

    h = _norm_mod(_pair_block(NLP, yp_ref, ys_ref), g_ref[...], mod_ref, 0)
    ag = jnp.dot(h.astype(BF16), w_ref[...], preferred_element_type=F32)
    u_ref[...] = ag[:, :D] * _sigmoid(ag[:, D:])


def _conv_in(yp, ys, split, mods, mrow, g1, w_in_all, slot):
    return pl.pallas_call(
        _conv_in_kernel,
        out_shape=jax.ShapeDtypeStruct((N_TOK, D), F32),
        grid_spec=pltpu.PrefetchScalarGridSpec(
            num_scalar_prefetch=1, grid=(NL,),
            in_specs=_pair_specs(TL, NLP, split) + [_mod_spec(), _full_spec((1, D)),
                                                    _resident_weight_spec(w_in_all, slot)],
            out_specs=_tok_spec(D, TL),
            scratch_shapes=[pltpu.VMEM((D, 2 * D), BF16)]),
        compiler_params=_cparams(),
        name="conv_in",
    )(mrow, yp, ys, mods, g1, w_in_all)


def _conv_main_kernel(mr_ref, first_ref, last_ref, u_ref, up_ref, un_ref, wdw_ref, bdw_ref, lg_ref, lb_ref,
                      woutf_ref, yp_ref, ys_ref, mod_ref, o_ref, ext_ref, acc_ref, wout_ref):
    j = pl.program_id(0)

    @pl.when(j == 0)
    def _():
        wout_ref[...] = woutf_ref[0].astype(BF16)

    zero = jnp.zeros((HALO, D), F32)
    ext_ref[0:HALO, :] = jnp.where(first_ref[j] == 1, zero, up_ref[...])
    ext_ref[HALO:HALO + SB, :] = u_ref[...]
    ext_ref[HALO + SB:2 * HALO + SB, :] = jnp.where(last_ref[j] == 1, zero, un_ref[...])

    off0 = HALO - CONV_PAD
    n_a = (off0 + CONV_WIDTH - 1) // SUBLANE + 1
    n_chunks = SB // SUBLANE

    def strip(ci, carry):
        cs = pl.ds(pl.multiple_of(ci * LANE, LANE), LANE)
        wk = [jnp.broadcast_to(wdw_ref[k:k + 1, cs], (SUBLANE, LANE)) for k in range(CONV_WIDTH)]
        bias = jnp.broadcast_to(bdw_ref[:, cs], (SUBLANE, LANE))
        sub = lax.broadcasted_iota(I32, (SUBLANE, LANE), 0)
        prev_rot, prev_v0 = None, None
        for j in range(n_chunks + 1):
            tiles = [ext_ref[SUBLANE * (j + a):SUBLANE * (j + a + 1), cs] for a in range(n_a)]
            part = []
            for s in range(SUBLANE):
                acc = None
                for a in range(n_a):
                    k = SUBLANE * a + s - off0
                    if (0 <= k < CONV_WIDTH) and not (s == 0 and j == n_chunks):
                        term = tiles[a] * wk[k]
                        acc = term if acc is None else acc + term
                part.append(acc)
            rot = [None] + [pltpu.roll(part[s], SUBLANE - s, 0) for s in range(1, SUBLANE)]
            if j >= 1:
                out = prev_v0 + bias
                for s in range(1, SUBLANE):
                    out = out + jnp.where(sub < SUBLANE - s, prev_rot[s], rot[s])
                acc_ref[SUBLANE * (j - 1):SUBLANE * j, cs] = out
            prev_rot, prev_v0 = rot, part[0]
        return carry

    lax.fori_loop(0, D // LANE, strip, 0)

    c = acc_ref[...]
    mu = jnp.mean(c, axis=-1, keepdims=True)
    cc = c - mu
    var = jnp.mean(cc * cc, axis=-1, keepdims=True)
    z = cc * lax.rsqrt(var + EPS) * lg_ref[...] + lb_ref[...]
    z = z * _sigmoid(z)
    out = jnp.dot(z.astype(BF16), wout_ref[...], preferred_element_type=F32)
    o_ref[...] = _pair_block(NSBP, yp_ref, ys_ref) + _mod_vec(mod_ref, 2) * out


def _conv_main(u, yp, ys, split, mods, mrow, w_dw, b_dw, ln_g, ln_b, w_out_all, slot):
    nh = N_TOK // HALO
    per = SB // HALO
    sb_spec = pl.BlockSpec((SB, D), lambda j, *_: (j, 0))
    return pl.pallas_call(
        _conv_main_kernel,
        out_shape=jax.ShapeDtypeStruct((N_TOK, D), F32),
        grid_spec=pltpu.PrefetchScalarGridSpec(
            num_scalar_prefetch=3, grid=(NSB,),
            in_specs=[
                sb_spec,
                pl.BlockSpec((HALO, D), lambda j, *_: (jnp.maximum(j * per - 1, 0), 0)),
                pl.BlockSpec((HALO, D), lambda j, *_: (jnp.minimum((j + 1) * per, nh - 1), 0)),
                _full_spec((CONV_WIDTH + 1, D)), _full_spec((1, D)), _full_spec((1, D)), _full_spec((1, D)),
                _resident_weight_spec(w_out_all, slot), *_pair_specs(SB, NSBP, split), _mod_spec(),
            ],
            out_specs=sb_spec,
            scratch_shapes=[pltpu.VMEM((SB + 2 * HALO, D), F32), pltpu.VMEM((SB, D), F32), pltpu.VMEM((D, D), BF16)]),
        compiler_params=_cparams(),
        name="conv_main",
    )(mrow, jnp.asarray(_SEQ_FIRST), jnp.asarray(_SEQ_LAST), u, u, u, w_dw, b_dw, ln_g, ln_b, w_out_all, yp, ys, mods)


def _rope_angles():
    rows = DEC_SEQ // GRID_W
    row = jnp.repeat(jnp.arange(rows, dtype=F32), GRID_W)
    col = jnp.tile(jnp.arange(GRID_W, dtype=F32), rows)
    axis_dim = HEAD_DIM // 2
    freqs = jnp.power(ROPE_THETA, -jnp.arange(axis_dim // 2, dtype=F32) * 2.0 / axis_dim)
    ang_r = row[:, None] * freqs[None, :]
    ang_c = col[:, None] * freqs[None, :]
    return jnp.concatenate([ang_r, ang_r, ang_c, ang_c], axis=-1)


def _rope_blocks():
    ang = _rope_angles()
    cos, sin = jnp.cos(ang), jnp.sin(ang)
    lane = np.arange(HEAD_DIM)
    lo = jnp.asarray(((lane % (HEAD_DIM // 2)) < HEAD_DIM // 4).astype(np.float32))
    sin_a = -sin * lo[None, :]
    sin_b = sin * (1.0 - lo)[None, :]
    nblk = DEC_SEQ // SB
    ident = jnp.ones((1, SB, HEAD_DIM), F32)
    zeros = jnp.zeros((1, SB, HEAD_DIM), F32)
    cos_t = jnp.concatenate([ident, cos.reshape(nblk, SB, HEAD_DIM)], axis=0)
    sa_t = jnp.concatenate([zeros, sin_a.reshape(nblk, SB, HEAD_DIM)], axis=0)
    sb_t = jnp.concatenate([zeros, sin_b.reshape(nblk, SB, HEAD_DIM)], axis=0)
    return cos_t, sa_t, sb_t


def _attn_qkv_kernel(mr_ref, ri_ref, y_ref, mod_ref, g_ref, wf_ref, qg_ref, kg_ref, cos_ref, sa_ref, sb_ref,
                     q_ref, kb_ref, vb_ref, kf_ref, vf_ref, w_ref):
    @pl.when(pl.program_id(0) == 0)
    def _():
        w_ref[...] = wf_ref[0].astype(BF16)

    h = _norm_mod(y_ref[...], g_ref[...], mod_ref, 0)
    qkv = jnp.dot(h.astype(BF16), w_ref[...], preferred_element_type=F32)
    cos, sa, sb = cos_ref[0], sa_ref[0], sb_ref[0]
    quarter = HEAD_DIM // 4

    def head(x, g):
        xn = _rms(x, g)
        return xn * cos + pltpu.roll(xn, HEAD_DIM - quarter, 1) * sa + pltpu.roll(xn, quarter, 1) * sb

    scale = HEAD_DIM ** -0.5
    for hd in range(N_HEADS):
        sl = slice(hd * HEAD_DIM, (hd + 1) * HEAD_DIM)
        q_ref[:, sl] = (head(qkv[:, sl], qg_ref[...]) * scale).astype(BF16)
    for kv in range(N_KV_HEADS):
        sl = slice(kv * HEAD_DIM, (kv + 1) * HEAD_DIM)
        kr = head(qkv[:, Q_DIM + kv * HEAD_DIM:Q_DIM + (kv + 1) * HEAD_DIM], kg_ref[...])
        kf_ref[:, sl] = kr
        kb_ref[:, sl] = kr.astype(BF16)
    v = qkv[:, Q_DIM + KV_DIM:]
    vf_ref[...] = v
    vb_ref[...] = v.astype(BF16)


def _attn_qkv(y, mods, mrow, g1, w_qkv_all, slot, q_g, k_g, rope):
    cos_t, sa_t, sb_t = rope
    rspec = pl.BlockSpec((1, SB, HEAD_DIM), lambda j, mr, ri: (ri[j], 0, 0))
    return pl.pallas_call(
        _attn_qkv_kernel,
        out_shape=(jax.ShapeDtypeStruct((N_TOK, Q_DIM), BF16), jax.ShapeDtypeStruct((N_TOK, KV_DIM), BF16),
                   jax.ShapeDtypeStruct((N_TOK, KV_DIM), BF16), jax.ShapeDtypeStruct((N_TOK, KV_DIM), F32),
                   jax.ShapeDtypeStruct((N_TOK, KV_DIM), F32)),
        grid_spec=pltpu.PrefetchScalarGridSpec(
            num_scalar_prefetch=2, grid=(NSB,),
            in_specs=[_tok_spec(D, SB), _mod_spec(), _full_spec((1, D)), _resident_weight_spec(w_qkv_all, slot),
                      _full_spec((1, HEAD_DIM)), _full_spec((1, HEAD_DIM)), rspec, rspec, rspec],
            out_specs=(_tok_spec(Q_DIM, SB), _tok_spec(KV_DIM, SB), _tok_spec(KV_DIM, SB), _tok_spec(KV_DIM, SB),
                       _tok_spec(KV_DIM, SB)),
            scratch_shapes=[pltpu.VMEM((D, QKV_DIM), BF16)]),
        compiler_params=_cparams(),
        name="attn_qkv",
    )(mrow, jnp.asarray(_ROPE_IDX_SB), y, mods, g1, w_qkv_all, q_g, k_g, cos_t, sa_t, sb_t)


def _attn_heads(q, ks, vs, o_scr):
    nt = (((1,), (1,)), ((), ()))
    for hd in range(N_HEADS):
        g = hd // GQA_GROUP
        qh = q[:, hd * HEAD_DIM:(hd + 1) * HEAD_DIM]
        gs = slice(g * HEAD_DIM, (g + 1) * HEAD_DIM)
        ss = [lax.dot_general(qh, k[:, gs], nt, preferred_element_type=F32) for k in ks]
        m = functools.reduce(jnp.maximum, [jnp.max(s, axis=-1, keepdims=True) for s in ss])
        ps = [jnp.exp(s - m) for s in ss]
        l = functools.reduce(lambda a, b: a + b, [jnp.sum(p, axis=-1, keepdims=True) for p in ps])
        o = functools.reduce(lambda a, b: a + b,
                             [jnp.dot(p.astype(BF16), v[:, gs], preferred_element_type=F32) for p, v in zip(ps, vs)])
        o_scr[:, hd * HEAD_DIM:(hd + 1) * HEAD_DIM] = (o / l).astype(BF16)


def _attn_ctx_kernel(q_ref, k_ref, v_ref, wo_ref, y_ref, mod_ref, o_ref, o_scr):
    _attn_heads(q_ref[...], [k_ref[...]], [v_ref[...]], o_scr)
    out = jnp.dot(o_scr[...], wo_ref[...], preferred_element_type=F32)
    o_ref[...] = y_ref[...] + _mod_vec(mod_ref, 2) * out


def _attn_lat_kernel(q_ref, k_ref, v_ref, ck_ref, cv_ref, wo_ref, y_ref, mod_ref, ctx_out_ref, o_ref, o_scr):
    del ctx_out_ref
    _attn_heads(q_ref[...], [k_ref[...], ck_ref[0].astype(BF16)], [v_ref[...], cv_ref[0].astype(BF16)], o_scr)
    out = jnp.dot(o_scr[...], wo_ref[...], preferred_element_type=F32)
    o_ref[...] = y_ref[...] + _mod_vec(mod_ref, 2) * out


def _attention(q, kb, vb, cache_k, cache_v, w_o, y, mods):
    y_ctx = pl.pallas_call(
        _attn_ctx_kernel,
        out_shape=jax.ShapeDtypeStruct((N_TOK, D), F32),
        grid=(BATCH,),
        in_specs=[
            pl.BlockSpec((SEQ, Q_DIM), lambda s: (s, 0)),
            pl.BlockSpec((SEQ, KV_DIM), lambda s: (s, 0)),
            pl.BlockSpec((SEQ, KV_DIM), lambda s: (s, 0)),
            pl.BlockSpec((Q_DIM, D), lambda s: (0, 0)),
            pl.BlockSpec((SEQ, D), lambda s: (s, 0)),
            pl.BlockSpec((1, 1, 6 * D), lambda s: (0, 0, 0)),
        ],
        out_specs=pl.BlockSpec((SEQ, D), lambda s: (s, 0)),
        scratch_shapes=[pltpu.VMEM((SEQ, Q_DIM), BF16)],
        compiler_params=_cparams(),
        name="attn_ctx",
    )(q, kb, vb, w_o, y, mods)
    pb = NP_TOK // DEC_SEQ
    return pl.pallas_call(
        _attn_lat_kernel,
        out_shape=jax.ShapeDtypeStruct((N_TOK, D), F32),
        input_output_aliases={8: 0},
        grid=(DEC_BATCH, SB_PER_DEC),
        in_specs=[
            pl.BlockSpec((SB, Q_DIM), lambda b, t: (NSBP + b * SB_PER_DEC + t, 0)),
            pl.BlockSpec((DEC_SEQ, KV_DIM), lambda b, t: (pb + b, 0)),
            pl.BlockSpec((DEC_SEQ, KV_DIM), lambda b, t: (pb + b, 0)),
            pl.BlockSpec((1, PAST_LEN, KV_DIM), lambda b, t: (b, 0, 0)),
            pl.BlockSpec((1, PAST_LEN, KV_DIM), lambda b, t: (b, 0, 0)),
            pl.BlockSpec((Q_DIM, D), lambda b, t: (0, 0)),
            pl.BlockSpec((SB, D), lambda b, t: (NSBP + b * SB_PER_DEC + t, 0)),
            pl.BlockSpec((1, 1, 6 * D), lambda b, t: (1 + b, 0, 0)),
            pl.BlockSpec(memory_space=pl.ANY),
        ],
        out_specs=pl.BlockSpec((SB, D), lambda b, t: (NSBP + b * SB_PER_DEC + t, 0)),
        scratch_shapes=[pltpu.VMEM((SB, Q_DIM), BF16)],
        compiler_params=_cparams(2),
        name="attn_lat",
    )(q, kb, vb, cache_k, cache_v, w_o, y, mods, y_ctx)


def _log_sigmoid(x):
    return jnp.minimum(x, 0.0) - jnp.log(1.0 + jnp.exp(-jnp.abs(x)))


W_T = 256


def _mlstm_in_kernel(mr_ref, y_ref, *refs):
    _mlstm_in_core(y_ref[...], *refs)


def _combine_mlstm_in_kernel(mr_ref, e0_ref, e1_ref, wt_ref, y_ref, modc_ref, *refs):
    yo_ref, a_scr, b_scr = refs[4], refs[-2], refs[-1]
    _load_slabs(e0_ref, a_scr, TM, F32)
    _load_slabs(e1_ref, b_scr, TM, F32)
    wt = wt_ref[...]
    moe = wt[:, 0:1] * a_scr[...] + wt[:, 1:2] * b_scr[...]
    y_new = y_ref[...] + _mod_vec(modc_ref, 5) * moe
    yo_ref[...] = y_new
    _mlstm_in_core(y_new, *refs[:4], *refs[5:-2])


def _mlstm_in_core(y, mod_ref, g_ref, wt_ref, bg_ref, q_ref, k_ref, v_ref, o_ref, gt_ref, gr_ref, w_ref, wg_ref):
    @pl.when(pl.program_id(0) == 0)
    def _():
        for r in range(0, 4 * D, W_T):
            w_ref[:, r:r + W_T] = jnp.transpose(wt_ref[0, r:r + W_T, :]).astype(BF16)
        n_out = 4 * D + 4 * M_HEADS
        tail = jnp.transpose(wt_ref[0, n_out - LANE:n_out, :])
        lane = lax.broadcasted_iota(I32, tail.shape, 1)
        wg_ref[...] = jnp.where(lane < 4 * M_HEADS, pltpu.roll(tail, 4 * M_HEADS, axis=1), 0.0).astype(BF16)

    h = _norm_mod(y, g_ref[...], mod_ref, 0)
    hb = h.astype(BF16)
    q_ref[...] = jnp.dot(hb, w_ref[:, 0:D], preferred_element_type=F32).astype(BF16)
    k_ref[...] = (jnp.dot(hb, w_ref[:, D:2 * D], preferred_element_type=F32) * (M_HEAD_DIM ** -0.5)).astype(BF16)
    v_ref[...] = jnp.dot(hb, w_ref[:, 2 * D:3 * D], preferred_element_type=F32).astype(BF16)
    o_ref[...] = _sigmoid(jnp.dot(hb, w_ref[:, 3 * D:4 * D], preferred_element_type=F32))
    gt = jnp.dot(hb, wg_ref[...], preferred_element_type=F32) + bg_ref[...]
    lane = lax.broadcasted_iota(I32, gt.shape, 1)
    is_f = ((lane >= M_HEADS) & (lane < 2 * M_HEADS)) | ((lane >= 3 * M_HEADS) & (lane < 4 * M_HEADS))
    gt = jnp.where(is_f, _log_sigmoid(gt), gt)
    gt_ref[...] = gt
    per = LANE // M_CHUNK
    for p in range(TM // LANE):
        t = jnp.transpose(gt[p * LANE:(p + 1) * LANE, :])
        for f in range(per):
            gr_ref[p * per + f] = t[0:4 * M_HEADS, f * M_CHUNK:(f + 1) * M_CHUNK]


def _mlstm_in(y, mods, mrow, g1, w_in_all, slot, b_gate, combine=None):
    w_t = jnp.swapaxes(w_in_all, 1, 2)
    w_spec = pl.BlockSpec((1,) + w_t.shape[1:], lambda j, *_: (slot, 0, 0), pipeline_mode=pl.Buffered(1))
    out_shape = (jax.ShapeDtypeStruct((N_TOK, D), BF16), jax.ShapeDtypeStruct((N_TOK, D), BF16),
                 jax.ShapeDtypeStruct((N_TOK, D), BF16), jax.ShapeDtypeStruct((N_TOK, D), F32),
                 jax.ShapeDtypeStruct((N_TOK, LANE), F32),
                 jax.ShapeDtypeStruct((N_TOK // M_CHUNK, 4 * M_HEADS, M_CHUNK), F32))
    in_specs = [_tok_spec(D), _mod_spec(), _full_spec((1, D)), w_spec, _full_spec((1, LANE))]
    out_specs = (_tok_spec(D), _tok_spec(D), _tok_spec(D), _tok_spec(D), _tok_spec(LANE),
                 pl.BlockSpec((TM // M_CHUNK, 4 * M_HEADS, M_CHUNK), lambda j, *_: (j, 0, 0)))
    scratch = [pltpu.VMEM((D, 4 * D), BF16), pltpu.VMEM((D, LANE), BF16)]
    body, args, name = _mlstm_in_kernel, (mrow, y, mods, g1, w_t, b_gate), "mlstm_in"
    if combine is not None:
        ym, wts, mods_prev = combine
        slab0 = pl.BlockSpec((TM * SLAB, LANE), lambda j, *_: (j, 0))
        slab1 = pl.BlockSpec((TM * SLAB, LANE), lambda j, *_: (NB + j, 0))
        in_specs = [slab0, slab1, _tok_spec(TOP_K), _tok_spec(D), _mod_spec()] + in_specs[1:]
        out_shape = (jax.ShapeDtypeStruct((N_TOK, D), F32),) + out_shape
        out_specs = (_tok_spec(D),) + out_specs
        scratch = scratch + [pltpu.VMEM((TM, D), F32), pltpu.VMEM((TM, D), F32)]
        body, args, name = _combine_mlstm_in_kernel, (mrow, ym, ym, wts, y, mods_prev, mods, g1, w_t, b_gate), \
            "moe_combine_mlstm_in"
    return pl.pallas_call(
        body, out_shape=out_shape,
        grid_spec=pltpu.PrefetchScalarGridSpec(num_scalar_prefetch=1, grid=(NB,), in_specs=in_specs,
                                               out_specs=out_specs, scratch_shapes=scratch),
        compiler_params=_cparams(),
        name=name,
    )(*args)


def _mlstm_load(hd, c, q_ref, k_ref, v_ref, gc_ref, gr_ref):
    r0 = pl.multiple_of(c * M_CHUNK, M_CHUNK)
    hs = slice(hd * M_HEAD_DIM, (hd + 1) * M_HEAD_DIM)
    rows = pl.ds(r0, M_CHUNK)
    return rows, hs, q_ref[rows, hs], k_ref[rows, hs], v_ref[rows, hs], gc_ref[rows, :], gr_ref[c]


def _mlstm_chunks(chains, ms, loaded, c_scr, n_scr):
    L = M_CHUNK
    n = range(len(chains))
    t_idx = lax.broadcasted_iota(I32, (L, L), 0)
    s_idx = lax.broadcasted_iota(I32, (L, L), 1)
    masks = {0: (s_idx <= t_idx, t_idx <= s_idx), 1: (s_idx >= t_idx, t_idx >= s_idx)}
    q = [ld[2] for ld in loaded]
    k = [ld[3] for ld in loaded]
    v = [ld[4] for ld in loaded]
    gi = [2 * d * M_HEADS + hd for hd, d in chains]
    gf = [(2 * d + 1) * M_HEADS + hd for hd, d in chains]
    i_col = [ld[5][:, gi[i]:gi[i] + 1] for i, ld in enumerate(loaded)]
    lf_col = [ld[5][:, gf[i]:gf[i] + 1] for i, ld in enumerate(loaded)]
    i_row = [ld[6][gi[i]:gi[i] + 1, :] for i, ld in enumerate(loaded)]
    lf_row = [ld[6][gf[i]:gf[i] + 1, :] for i, ld in enumerate(loaded)]
    mask = [masks[d][0] for _, d in chains]
    mask_t = [masks[d][1] for _, d in chains]
    b_col = [jnp.sum(jnp.where(mask[i], lf_row[i], 0.0), axis=1, keepdims=True) for i in n]
    b_row = [jnp.sum(jnp.where(mask_t[i], lf_col[i], 0.0), axis=0, keepdims=True) for i in n]
    log_d = [jnp.where(mask[i], b_col[i] - b_row[i] + i_row[i], -jnp.inf) for i in n]
    li = [b_col[i] + ms[i] for i in n]
    m_r = [jnp.maximum(li[i], jnp.max(log_d[i], axis=1, keepdims=True)) for i in n]
    a_int = [jnp.exp(li[i] - m_r[i]) for i in n]
    dmat = [jnp.exp(log_d[i] - m_r[i]) for i in n]
    cmat = [c_scr[d, hd] for hd, d in chains]
    nvec = [n_scr[d, hd] for hd, d in chains]
    gram = [lax.dot_general(q[i], k[i], (((1,), (1,)), ((), ())), preferred_element_type=F32) for i in n]
    inter = [jnp.dot(q[i], cmat[i].astype(BF16), preferred_element_type=F32) for i in n]
    s = [gram[i] * dmat[i] for i in n]
    intra = [jnp.dot(s[i].astype(BF16), v[i], preferred_element_type=F32) for i in n]
    qn = [jnp.sum(q[i].astype(F32) * nvec[i], axis=1, keepdims=True) for i in n]
    den = [a_int[i] * qn[i] + jnp.sum(s[i], axis=1, keepdims=True) for i in n]
    hh = [(a_int[i] * inter[i] + intra[i]) / jnp.maximum(jnp.abs(den[i]), jnp.exp(-m_r[i])) for i in n]
    b_last = [b_row[i][:, L - 1:L] if chains[i][1] == 0 else b_row[i][:, 0:1] for i in n]
    log_w = [b_last[i] - b_col[i] + i_col[i] for i in n]
    m_new = [jnp.maximum(b_last[i] + ms[i], jnp.max(log_w[i], axis=0, keepdims=True)) for i in n]
    w = [jnp.exp(log_w[i] - m_new[i]) for i in n]
    decay = [jnp.exp(b_last[i] + ms[i] - m_new[i]) for i in n]
    kw = [k[i].astype(F32) * w[i] for i in n]
    kv = [lax.dot_general(kw[i].astype(BF16), v[i], (((0,), (0,)), ((), ())), preferred_element_type=F32) for i in n]
    for i, (hd, d) in enumerate(chains):
        c_scr[d, hd] = decay[i] * cmat[i] + kv[i]
        n_scr[d, hd] = decay[i] * nvec[i] + jnp.sum(kw[i], axis=0, keepdims=True)
    return hh, m_new


SCAN_GROUP = 2 * M_HEADS


def _mlstm_scan_body(n_chunks, q_ref, k_ref, v_ref, gc_ref, gr_ref, h_ref, hb_scr, c_scr, n_scr, m0):
    chains = [(hd, d) for hd in range(M_HEADS) for d in range(2)]

    def body(c, ms):
        out = []
        for g0 in range(0, len(chains), SCAN_GROUP):
            grp = chains[g0:g0 + SCAN_GROUP]
            loaded = [_mlstm_load(hd, c if d == 0 else n_chunks - 1 - c, q_ref, k_ref, v_ref, gc_ref, gr_ref)
                      for hd, d in grp]
            hh, m_new = _mlstm_chunks(grp, ms[g0:g0 + SCAN_GROUP], loaded, c_scr, n_scr)
            for (hd, d), ld, h in zip(grp, loaded, hh):
                dst = h_ref if d == 0 else hb_scr
                dst[ld[0], ld[1]] = h
            out += m_new
        return tuple(out)

    ms = lax.fori_loop(0, n_chunks, body, tuple(m0))
    h_ref[...] += hb_scr[...]
    return ms


def _mlstm_scan_ctx_kernel(q_ref, k_ref, v_ref, gc_ref, gr_ref, h_ref, cn_ref, nn_ref, mn_ref, hb_scr, c_scr, n_scr):
    c_scr[...] = jnp.zeros(c_scr.shape, F32)
    n_scr[...] = jnp.zeros(n_scr.shape, F32)
    zero = jnp.zeros((1, 1), F32)
    ms = _mlstm_scan_body(SEQ // M_CHUNK, q_ref, k_ref, v_ref, gc_ref, gr_ref, h_ref, hb_scr, c_scr, n_scr,
                          [zero] * (2 * M_HEADS))
    cn_ref[0] = c_scr[...]
    nn_ref[0] = n_scr[...]
    for hd in range(M_HEADS):
        for d in range(2):
            mn_ref[0, d, hd] = jnp.broadcast_to(ms[2 * hd + d], (1, LANE))


def _mlstm_scan_lat_kernel(q_ref, k_ref, v_ref, gc_ref, gr_ref, c0_ref, n0_ref, m0_ref, ctx_out_ref, h_ref,
                           hb_scr, c_scr, n_scr):
    del ctx_out_ref
    c_scr[...] = c0_ref[0]
    n_scr[...] = n0_ref[0]
    m0 = [m0_ref[0, d, hd] for hd in range(M_HEADS) for d in range(2)]
    _mlstm_scan_body(DEC_SEQ // M_CHUNK, q_ref, k_ref, v_ref, gc_ref, gr_ref, h_ref, hb_scr, c_scr, n_scr, m0)


def _mlstm_scan(q, k, v, gcol, grow, state_c, state_n, state_m):
    hd = M_HEAD_DIM
    ng = 4 * M_HEADS
    state_scratch = [pltpu.VMEM((2, M_HEADS, hd, hd), F32), pltpu.VMEM((2, M_HEADS, 1, hd), F32)]
    ncp = SEQ // M_CHUNK
    h_ctx, new_c, new_n, new_m = pl.pallas_call(
        _mlstm_scan_ctx_kernel,
        out_shape=(jax.ShapeDtypeStruct((N_TOK, D), F32),
                   jax.ShapeDtypeStruct((BATCH, 2, M_HEADS, hd, hd), F32),
                   jax.ShapeDtypeStruct((BATCH, 2, M_HEADS, 1, hd), F32),
                   jax.ShapeDtypeStruct((BATCH, 2, M_HEADS, 1, LANE), F32)),
        grid=(BATCH,),
        in_specs=[
            pl.BlockSpec((SEQ, D), lambda s: (s, 0)),
            pl.BlockSpec((SEQ, D), lambda s: (s, 0)),
            pl.BlockSpec((SEQ, D), lambda s: (s, 0)),
            pl.BlockSpec((SEQ, LANE), lambda s: (s, 0)),
            pl.BlockSpec((ncp, ng, M_CHUNK), lambda s: (s, 0, 0)),
        ],
        out_specs=(
            pl.BlockSpec((SEQ, D), lambda s: (s, 0)),
            pl.BlockSpec((1, 2, M_HEADS, hd, hd), lambda s: (s, 0, 0, 0, 0)),
            pl.BlockSpec((1, 2, M_HEADS, 1, hd), lambda s: (s, 0, 0, 0, 0)),
            pl.BlockSpec((1, 2, M_HEADS, 1, LANE), lambda s: (s, 0, 0, 0, 0)),
        ),
        scratch_shapes=[pltpu.VMEM((SEQ, D), F32)] + state_scratch,
        compiler_params=_cparams(),
        name="mlstm_scan_ctx",
    )(q, k, v, gcol, grow)
    ncl = DEC_SEQ // M_CHUNK
    pb = NP_TOK // DEC_SEQ
    h_all = pl.pallas_call(
        _mlstm_scan_lat_kernel,
        out_shape=jax.ShapeDtypeStruct((N_TOK, D), F32),
        input_output_aliases={8: 0},
        grid=(DEC_BATCH,),
        in_specs=[
            pl.BlockSpec((DEC_SEQ, D), lambda b: (pb + b, 0)),
            pl.BlockSpec((DEC_SEQ, D), lambda b: (pb + b, 0)),
            pl.BlockSpec((DEC_SEQ, D), lambda b: (pb + b, 0)),
            pl.BlockSpec((DEC_SEQ, LANE), lambda b: (pb + b, 0)),
            pl.BlockSpec((ncl, ng, M_CHUNK), lambda b: (pb + b, 0, 0)),
            pl.BlockSpec((1, 2, M_HEADS, hd, hd), lambda b: (b, 0, 0, 0, 0)),
            pl.BlockSpec((1, 2, M_HEADS, 1, hd), lambda b: (b, 0, 0, 0, 0)),
            pl.BlockSpec((1, 2, M_HEADS, 1, 1), lambda b: (b, 0, 0, 0, 0)),
            pl.BlockSpec(memory_space=pl.ANY),
        ],
        out_specs=pl.BlockSpec((DEC_SEQ, D), lambda b: (pb + b, 0)),
        scratch_shapes=[pltpu.VMEM((DEC_SEQ, D), F32)] + state_scratch,
        compiler_params=_cparams(),
        name="mlstm_scan_lat",
    )(q, k, v, gcol, grow, state_c, state_n, state_m, h_ctx)
    return h_all, new_c, new_n, new_m


def _mlstm_out_kernel(mr_ref, h_ref, o_ref, ng_ref, wf_ref, y_ref, mod_ref, out_ref, x_scr, w_ref):
    @pl.when(pl.program_id(0) == 0)
    def _():
        w_ref[...] = wf_ref[0].astype(BF16)

    hc = o_ref[...] * h_ref[...]
    for hd in range(M_HEADS):
        sl = slice(hd * M_HEAD_DIM, (hd + 1) * M_HEAD_DIM)
        x_scr[:, sl] = _rms(hc[:, sl], ng_ref[:, sl]).astype(BF16)
    out = jnp.dot(x_scr[...], w_ref[...], preferred_element_type=F32)
    out_ref[...] = y_ref[...] + _mod_vec(mod_ref, 2) * out


def _mlstm_out(hsum, o, norm_g, w_out_all, slot, y, mods, mrow):
    return pl.pallas_call(
        _mlstm_out_kernel,
        out_shape=jax.ShapeDtypeStruct((N_TOK, D), F32),
        grid_spec=pltpu.PrefetchScalarGridSpec(
            num_scalar_prefetch=1, grid=(NL,),
            in_specs=[_tok_spec(D, TL), _tok_spec(D, TL), _full_spec((1, D)), _resident_weight_spec(w_out_all, slot),
                      _tok_spec(D, TL),
                      _mod_spec()],
            out_specs=_tok_spec(D, TL),
            scratch_shapes=[pltpu.VMEM((TL, D), BF16), pltpu.VMEM((D, D), BF16)]),
        compiler_params=_cparams(),
        name="mlstm_out",
    )(mrow, hsum, o, norm_g, w_out_all, y, mods)


ROUTE_OFF = N_GROUPS
SLAB = D // (2 * LANE)
V7X_SC_CORES = 2
V7X_SC_SUBCORES = 16
SC_WORKERS = V7X_SC_CORES * V7X_SC_SUBCORES
SC_WINDOW = 128
SC_HALF = SC_WINDOW // 2
HI_MASK = -65536


def _bf16_bits(x):
    return lax.bitcast_convert_type(x.astype(BF16).astype(F32), I32)


def _store_slabs(ref, x):
    rows = x.shape[0]
    for c in range(SLAB):
        lo = lax.shift_right_logical(_bf16_bits(x[:, (2 * c) * LANE:(2 * c + 1) * LANE]), 16)
        hi = _bf16_bits(x[:, (2 * c + 1) * LANE:(2 * c + 2) * LANE]) & HI_MASK
        ref[pl.ds(c, rows, stride=SLAB), :] = lo | hi


def _load_slabs(ref, dst, rows, dtype):
    for c in range(SLAB):
        w = ref[pl.ds(c, rows, stride=SLAB), :]
        lo = lax.bitcast_convert_type(lax.shift_left(w, 16), F32)
        hi = lax.bitcast_convert_type(w & HI_MASK, F32)
        dst[:, (2 * c) * LANE:(2 * c + 1) * LANE] = lo.astype(dtype)
        dst[:, (2 * c + 1) * LANE:(2 * c + 2) * LANE] = hi.astype(dtype)


def _route_kernel(mr_ref, y_ref, mod_ref, g_ref, wr_ref, br_ref, tri_ref, x_ref, wt_ref, dest_ref, cnt_ref,
                  cnt_scr, meta_scr):
    x = _norm_mod(y_ref[...], g_ref[...], mod_ref, 1)
    _store_slabs(x_ref, x)
    lg = jnp.dot(x.astype(BF16), wr_ref[...], preferred_element_type=F32) + br_ref[...]
    lane = lax.broadcasted_iota(I32, lg.shape, 1).astype(F32)
    ninf = -jnp.inf
    big = float(LANE)
    lgg = jnp.where(lane < N_GROUPS, lg, ninf)
    gmax = jnp.max(lgg, axis=-1, keepdims=True)
    g_idx = jnp.min(jnp.where(lgg == gmax, lane, big), axis=-1, keepdims=True)
    g_w = 1.0 / jnp.sum(jnp.exp(lgg - gmax), axis=-1, keepdims=True)
    lo = ROUTE_OFF + g_idx * EXPERTS_PER_GROUP
    le = jnp.where((lane >= lo) & (lane < lo + EXPERTS_PER_GROUP), lg, ninf)
    m1 = jnp.max(le, axis=-1, keepdims=True)
    i1 = jnp.min(jnp.where(le == m1, lane, big), axis=-1, keepdims=True)
    le2 = jnp.where(lane == i1, ninf, le)
    m2 = jnp.max(le2, axis=-1, keepdims=True)
    i2 = jnp.min(jnp.where(le2 == m2, lane, big), axis=-1, keepdims=True)
    r = jnp.exp(m2 - m1)
    p1 = 1.0 / (1.0 + r)
    p2 = r / (1.0 + r)
    two = lax.broadcasted_iota(I32, (x.shape[0], TOP_K), 1)
    wt_ref[...] = jnp.where(two == 0, g_w * p1, g_w * p2)
    @pl.when(pl.program_id(0) == 0)
    def _():
        cnt_scr[...] = jnp.zeros(cnt_scr.shape, F32)

    oh1 = (lane == i1).astype(F32)
    oh2 = (lane == i2).astype(F32)
    both = oh1 + oh2
    before = jnp.dot(tri_ref[...], both.astype(BF16), preferred_element_type=F32) + cnt_scr[...]
    rk1 = jnp.sum(oh1 * before, axis=-1, keepdims=True)
    rk2 = jnp.sum(oh2 * before, axis=-1, keepdims=True)
    cnt_scr[...] = cnt_scr[...] + jnp.sum(both, axis=0, keepdims=True)
    cnt_ref[...] = cnt_scr[...]
    cols = (i1 - ROUTE_OFF, i2 - ROUTE_OFF, rk1, rk2)
    packed = jnp.zeros(lg.shape, F32)
    for c, val in enumerate(cols):
        packed = jnp.where(lane == c, val, packed)
    j = pl.program_id(0)
    meta_scr[:, pl.ds(pl.multiple_of(j * TL, TL), TL)] = jnp.transpose(packed)[0:len(cols), :].astype(I32)

    @pl.when(j == NL - 1)
    def _():
        sub8 = (SUBLANE, LANE)
        ln = lax.broadcasted_iota(I32, sub8, 1)
        counts = jnp.broadcast_to(cnt_scr[...], sub8)
        is_e = (ln >= ROUTE_OFF) & (ln < ROUTE_OFF + N_EXPERTS)
        padded = jnp.where(is_e, jnp.floor((counts + (EBLK - 1)) * (1.0 / EBLK)) * EBLK, 0.0)
        inc = padded
        sh = 1
        while sh < LANE:
            inc = inc + jnp.where(ln >= sh, pltpu.roll(inc, sh, 1), 0.0)
            sh *= 2
        start = pltpu.roll(inc - padded, LANE - ROUTE_OFF, 1)
        table = jnp.transpose(jnp.broadcast_to(start[0:1, :], (LANE, LANE)))[0:N_EXPERTS, 0:1].astype(I32)
        sub = lax.broadcasted_iota(I32, (N_EXPERTS, SLOT_COLS), 0)
        for c0 in range(0, N_TOK, SLOT_COLS):
            meta = meta_scr[:, c0:c0 + SLOT_COLS]
            for k in range(TOP_K):
                first = jnp.sum(jnp.where(sub == meta[k:k + 1, :], table, 0), axis=0, keepdims=True)
                dest_ref[k:k + 1, c0:c0 + SLOT_COLS] = first + meta[TOP_K + k:TOP_K + k + 1, :]


SLOT_COLS = 2048


def _route(y, mods, mrow, g2, w_route, b_route):
    return pl.pallas_call(
        _route_kernel,
        out_shape=(jax.ShapeDtypeStruct((N_TOK * SLAB, LANE), I32), jax.ShapeDtypeStruct((N_TOK, TOP_K), F32),
                   jax.ShapeDtypeStruct((TOP_K, N_TOK), I32), jax.ShapeDtypeStruct((1, LANE), F32)),
        grid_spec=pltpu.PrefetchScalarGridSpec(
            num_scalar_prefetch=1, grid=(NL,),
            in_specs=[_tok_spec(D, TL), _mod_spec(), _full_spec((1, D)), _full_spec((D, LANE)),
                      _full_spec((1, LANE)), _full_spec((TL, TL))],
            out_specs=(pl.BlockSpec((TL * SLAB, LANE), lambda j, *_: (j, 0)), _tok_spec(TOP_K, TL),
                       _full_spec((TOP_K, N_TOK)), _full_spec((1, LANE))),
            scratch_shapes=[pltpu.VMEM((1, LANE), F32), pltpu.VMEM((2 * TOP_K, N_TOK), I32)]),
        compiler_params=_cparams(),
        name="moe_route",
    )(mrow, y, mods, g2, w_route, b_route, jnp.asarray(np.tril(np.ones((TL, TL), np.float32), -1), dtype=BF16))


def _block_tables_of(lane_counts):
    counts = lane_counts[0, ROUTE_OFF:ROUTE_OFF + N_EXPERTS].astype(I32)
    padded = ((counts + EBLK - 1) // EBLK) * EBLK
    pad_end = jnp.cumsum(padded)
    pad_start = pad_end - padded
    n_blk = (padded // EBLK).astype(I32)
    blk_start = (pad_start // EBLK).astype(I32)
    n_used = (pad_end[-1] // EBLK).astype(I32).reshape(1)
    return blk_start, n_blk, n_used


def _sc_mesh():
    return plsc.VectorSubcoreMesh(core_axis_name="core", subcore_axis_name="subcore",
                                  num_cores=V7X_SC_CORES, num_subcores=V7X_SC_SUBCORES)


def _sc_worker():
    return lax.axis_index("core") * V7X_SC_SUBCORES + lax.axis_index("subcore")


def _sc_dispatch(x_slabs, dest):
    per = N_TOK // SC_WORKERS
    n_win = per // SC_HALF

    @functools.partial(
        pl.kernel, out_type=jax.ShapeDtypeStruct((P_SLOTS, SLAB, LANE), I32), mesh=_sc_mesh(), name="moe_dispatch",
        scratch_types=[pltpu.VMEM((1, per), I32), pltpu.VMEM((1, per), I32), pltpu.VMEM((2, SC_HALF, SLAB, LANE), I32),
                       pltpu.SemaphoreType.DMA((2,)), pltpu.SemaphoreType.DMA((2,))])
    def run(x_hbm, d_hbm, o_hbm, i0_v, i1_v, buf, lsem, ssem):
        base = _sc_worker() * per
        pltpu.sync_copy(d_hbm.at[pl.ds(0, 1), pl.ds(base, per)], i0_v)
        pltpu.sync_copy(d_hbm.at[pl.ds(1, 1), pl.ds(base, per)], i1_v)
        loads = [pltpu.make_async_copy(x_hbm.at[pl.ds(base + s * SC_HALF, SC_HALF)], buf.at[s % 2], lsem.at[s % 2])
                 for s in range(n_win)]
        loads[0].start()
        for s in range(n_win):
            loads[s].wait()
            if s + 1 < n_win:
                loads[s + 1].start()
            win = pl.ds(s * SC_HALF, SC_HALF)
            outs = [pltpu.make_async_copy(buf.at[s % 2], o_hbm.at[iv.at[0, win]], ssem.at[a])
                    for a, iv in enumerate((i0_v, i1_v))]
            for cp in outs:
                cp.start()
            for cp in outs:
                cp.wait()

    return run(x_slabs.reshape(N_TOK, SLAB, LANE), dest)


def _sc_collect(y_slabs, dest):
    per = N_ASSIGN // SC_WORKERS
    n_win = per // SC_HALF

    @functools.partial(
        pl.kernel, out_type=jax.ShapeDtypeStruct((N_ASSIGN, SLAB, LANE), I32), mesh=_sc_mesh(), name="moe_collect",
        scratch_types=[pltpu.VMEM((1, per), I32), pltpu.VMEM((2, SC_HALF, SLAB, LANE), I32),
                       pltpu.SemaphoreType.DMA((2,)), pltpu.SemaphoreType.DMA((2,))])
    def run(y_hbm, i_hbm, o_hbm, i_v, buf, gsem, wsem):
        w = _sc_worker()
        base = w * per
        per_a = SC_WORKERS // TOP_K
        pltpu.sync_copy(i_hbm.at[pl.ds(w // per_a, 1), pl.ds((w % per_a) * per, per)], i_v)
        gathers = [pltpu.make_async_copy(y_hbm.at[i_v.at[0, pl.ds(s * SC_HALF, SC_HALF)]], buf.at[s % 2],
                                         gsem.at[s % 2]) for s in range(n_win)]
        writes = [pltpu.make_async_copy(buf.at[s % 2], o_hbm.at[pl.ds(base + s * SC_HALF, SC_HALF)], wsem.at[s % 2])
                  for s in range(n_win)]
        gathers[0].start()
        for s in range(n_win):
            gathers[s].wait()
            if s >= 1:
                writes[s - 1].wait()
            if s + 1 < n_win:
                gathers[s + 1].start()
            writes[s].start()
        writes[n_win - 1].wait()

    return run(y_slabs.reshape(P_SLOTS, SLAB, LANE), dest)


EROWS = EBLK * SLAB


def _expert_kernel(bs_ref, nb_ref, nu_ref, wg_ref, wu_ref, wd_ref, x_hbm, y_hbm,
                   xbuf, ybuf, xs, wg_bf, wu_bf, wd_bf, isem, osem):
    e = pl.program_id(0)
    n_exp = pl.num_programs(0)
    n_used = nu_ref[0]
    b0 = bs_ref[e]
    nb = nb_ref[e]

    def in_copy(g, slot):
        return pltpu.make_async_copy(x_hbm.at[pl.ds(pl.multiple_of(g * EROWS, EROWS), EROWS)], xbuf.at[slot],
                                     isem.at[slot])

    def out_copy(g, slot):
        return pltpu.make_async_copy(ybuf.at[slot], y_hbm.at[pl.ds(pl.multiple_of(g * EROWS, EROWS), EROWS)],
                                     osem.at[slot])

    @pl.when(e == 0)
    def _():
        in_copy(0, 0).start()

    @pl.when(nb > 0)
    def _():
        wg_bf[...] = wg_ref[0, 0].astype(BF16)
        wu_bf[...] = wu_ref[0, 0].astype(BF16)
        wd_bf[...] = wd_ref[0, 0].astype(BF16)

    def block(k, carry):
        g = b0 + k
        slot = lax.rem(g, 2)
        in_copy(g, slot).wait()

        @pl.when(g + 1 < n_used)
        def _():
            in_copy(g + 1, 1 - slot).start()

        _load_slabs(xbuf.at[slot], xs, EBLK, BF16)
        xb = xs[...]
        gt = jnp.dot(xb, wg_bf[...], preferred_element_type=F32)
        up = jnp.dot(xb, wu_bf[...], preferred_element_type=F32)
        hmid = (gt * _sigmoid(gt) * up).astype(BF16)
        res = jnp.dot(hmid, wd_bf[...], preferred_element_type=F32)

        @pl.when(g >= 2)
        def _():
            out_copy(g - 2, slot).wait()

        _store_slabs(ybuf.at[slot], res)
        out_copy(g, slot).start()
        return carry

    lax.fori_loop(0, nb, block, 0)

    @pl.when(e == n_exp - 1)
    def _():
        last = n_used - 1
        out_copy(last, lax.rem(last, 2)).wait()

        @pl.when(n_used >= 2)
        def _():
            out_copy(last - 1, lax.rem(last - 1, 2)).wait()


def _experts(x_sorted, blk_start, n_blk, n_used, w_gate, w_up, w_down, layer):
    any_spec = pl.BlockSpec(memory_space=pl.ANY)
    wspec = lambda r, c: pl.BlockSpec((1, 1, r, c), lambda e, *_: (layer, e, 0, 0))
    return pl.pallas_call(
        _expert_kernel,
        out_shape=jax.ShapeDtypeStruct((P_SLOTS * SLAB, LANE), I32),
        grid_spec=pltpu.PrefetchScalarGridSpec(
            num_scalar_prefetch=3, grid=(N_EXPERTS,),
            in_specs=[wspec(D, D_EXPERT), wspec(D, D_EXPERT), wspec(D_EXPERT, D), any_spec],
            out_specs=any_spec,
            scratch_shapes=[
                pltpu.VMEM((2, EROWS, LANE), I32), pltpu.VMEM((2, EROWS, LANE), I32),
                pltpu.VMEM((EBLK, D), BF16),
                pltpu.VMEM((D, D_EXPERT), BF16), pltpu.VMEM((D, D_EXPERT), BF16), pltpu.VMEM((D_EXPERT, D), BF16),
                pltpu.SemaphoreType.DMA((2,)), pltpu.SemaphoreType.DMA((2,)),
            ]),
        compiler_params=_cparams(),
        name="moe_experts",
    )(blk_start, n_blk, n_used, w_gate, w_up, w_down, x_sorted.reshape(P_SLOTS * SLAB, LANE))


def _combine_kernel(final, mr_ref, e0_ref, e1_ref, wt_ref, y_ref, mod_ref, fg_ref, o_ref, a_scr, b_scr):
    _load_slabs(e0_ref, a_scr, TL, F32)
    _load_slabs(e1_ref, b_scr, TL, F32)
    wt = wt_ref[...]
    moe = wt[:, 0:1] * a_scr[...] + wt[:, 1:2] * b_scr[...]
    y_new = y_ref[...] + _mod_vec(mod_ref, 5) * moe
    o_ref[...] = _rms(y_new, fg_ref[...]) if final else y_new


def _combine(ym, wts, y, mods, mrow, final_g, blk0, nblk, final):
    tok = lambda width: pl.BlockSpec((TL, width), lambda j, *_: (blk0 + j, 0))
    slab0 = pl.BlockSpec((TL * SLAB, LANE), lambda j, *_: (blk0 + j, 0))
    slab1 = pl.BlockSpec((TL * SLAB, LANE), lambda j, *_: (NL + blk0 + j, 0))
    mod = pl.BlockSpec((1, 1, 6 * D), lambda j, mr: (mr[blk0 + j], 0, 0))
    return pl.pallas_call(
        functools.partial(_combine_kernel, final),
        out_shape=jax.ShapeDtypeStruct((nblk * TL, D), F32),
        grid_spec=pltpu.PrefetchScalarGridSpec(
            num_scalar_prefetch=1, grid=(nblk,),
            in_specs=[slab0, slab1, tok(TOP_K), tok(D), mod, _full_spec((1, D))],
            out_specs=pl.BlockSpec((TL, D), lambda j, *_: (j, 0)),
            scratch_shapes=[pltpu.VMEM((TL, D), F32), pltpu.VMEM((TL, D), F32)]),
        compiler_params=_cparams(),
        name="moe_combine",
    )(mrow, ym, ym, wts, y, mods, final_g)


def _combine_conv_in_kernel(mr_ref, e0_ref, e1_ref, wt_ref, y_ref, mod_ref, modn_ref, g_ref, wf_ref, o_ref, u_ref,
                            a_scr, b_scr, w_ref):
    @pl.when(pl.program_id(0) == 0)
    def _():
        w_ref[...] = wf_ref[0].astype(BF16)

    _load_slabs(e0_ref, a_scr, TM, F32)
    _load_slabs(e1_ref, b_scr, TM, F32)
    wt = wt_ref[...]
    moe = wt[:, 0:1] * a_scr[...] + wt[:, 1:2] * b_scr[...]
    y_new = y_ref[...] + _mod_vec(mod_ref, 5) * moe
    o_ref[...] = y_new
    h = _norm_mod(y_new, g_ref[...], modn_ref, 0)
    ag = jnp.dot(h.astype(BF16), w_ref[...], preferred_element_type=F32)
    u_ref[...] = ag[:, :D] * _sigmoid(ag[:, D:])


def _combine_conv_in(ym, wts, y, mods, mods_next, mrow, g1_next, w_in_all, slot):
    slab0 = pl.BlockSpec((TM * SLAB, LANE), lambda j, *_: (j, 0))
    slab1 = pl.BlockSpec((TM * SLAB, LANE), lambda j, *_: (NB + j, 0))
    return pl.pallas_call(
        _combine_conv_in_kernel,
        out_shape=(jax.ShapeDtypeStruct((N_TOK, D), F32), jax.ShapeDtypeStruct((N_TOK, D), F32)),
        grid_spec=pltpu.PrefetchScalarGridSpec(
            num_scalar_prefetch=1, grid=(NB,),
            in_specs=[slab0, slab1, _tok_spec(TOP_K), _tok_spec(D), _mod_spec(), _mod_spec(), _full_spec((1, D)),
                      _resident_weight_spec(w_in_all, slot)],
            out_specs=(_tok_spec(D), _tok_spec(D)),
            scratch_shapes=[pltpu.VMEM((TM, D), F32), pltpu.VMEM((TM, D), F32), pltpu.VMEM((D, 2 * D), BF16)]),
        compiler_params=_cparams(),
        name="moe_combine_conv_in",
    )(mrow, ym, ym, wts, y, mods, mods_next, g1_next, w_in_all)


def kernel(x_prompt, x_sample, cache_attn_k, cache_attn_v, state_mlstm_C, state_mlstm_n, state_mlstm_m, c, c_ctx, ada_w, ada_b, norm1_g, norm2_g, conv_w_in, conv_w_dw, conv_b_dw, conv_ln_g, conv_ln_b, conv_w_out, attn_w_qkv, attn_q_norm, attn_k_norm, attn_w_o, mlstm_w_in, mlstm_b_gate, mlstm_norm_g, mlstm_w_out, moe_w_group, moe_b_group, moe_w_router, moe_b_router, moe_w_gate, moe_w_up, moe_w_down, final_norm_g):
    y = None
    cvec = jnp.concatenate([c_ctx[None, :], c, jnp.zeros((MOD_ROWS - 1 - DEC_BATCH, D), F32)], axis=0)
    rope = _rope_blocks()
    mrow, mrow_sb, mrow_l = jnp.asarray(_MOD_ROW), jnp.asarray(_MOD_ROW_SB), jnp.asarray(_MOD_ROW_L)
    new_k = new_v = new_c = new_n = new_m = None
    mods_all = [_ada_layer(cvec, ada_w, ada_b, i) for i in range(DEPTH)]
    u = None
    for i in range(DEPTH):
        kind, slot = i % 3, i // 3
        mods = mods_all[i]
        g1 = norm1_g[i].reshape(1, D)
        if kind == 0:
            src = (x_prompt.reshape(NP_TOK, D), x_sample.reshape(NS_TOK, D), True) if i == 0 else (y, y, False)
            if i == 0:
                u = _conv_in(*src, mods, mrow_l, g1, conv_w_in, slot)
            w_dw = jnp.concatenate([conv_w_dw[slot], jnp.zeros((1, D), F32)], axis=0)
            y = _conv_main(u, *src, mods, mrow_sb, w_dw, conv_b_dw[slot].reshape(1, D), conv_ln_g[slot].reshape(1, D),
                           conv_ln_b[slot].reshape(1, D), conv_w_out, slot)
        elif kind == 1:
            q, kb, vb, kf, vf = _attn_qkv(y, mods, mrow_sb, g1, attn_w_qkv, slot,
                                          attn_q_norm[slot].reshape(1, HEAD_DIM), attn_k_norm[slot].reshape(1, HEAD_DIM),
                                          rope)
            new_k = kf[:NP_TOK].reshape(BATCH, 1, SEQ, N_KV_HEADS, HEAD_DIM)
            new_v = vf[:NP_TOK].reshape(BATCH, 1, SEQ, N_KV_HEADS, HEAD_DIM)
            ck = cache_attn_k[:, slot].reshape(DEC_BATCH, PAST_LEN, KV_DIM)
            cv = cache_attn_v[:, slot].reshape(DEC_BATCH, PAST_LEN, KV_DIM)
            y = _attention(q, kb, vb, ck, cv, attn_w_o[slot].astype(BF16), y, mods)
        else:
            b_gate = jnp.concatenate([mlstm_b_gate[slot], jnp.zeros((LANE - 4 * M_HEADS,), F32)]).reshape(1, LANE)
            y, q, k, v, o, gates, grow = _mlstm_in(y, mods, mrow, g1, mlstm_w_in, slot, b_gate, combine=pending)
            sc = state_mlstm_C[:, slot]
            sn = state_mlstm_n[:, slot].reshape(DEC_BATCH, 2, M_HEADS, 1, M_HEAD_DIM)
            sm = state_mlstm_m[:, slot].reshape(DEC_BATCH, 2, M_HEADS, 1, 1)
            hsum, nc_, nn_, nm_ = _mlstm_scan(q, k, v, gates, grow, sc, sn, sm)
            new_c = nc_[:, None]
            new_n = nn_.reshape(BATCH, 1, 2, M_HEADS, M_HEAD_DIM)
            new_m = nm_[..., 0, 0].reshape(BATCH, 1, 2, M_HEADS)
            y = _mlstm_out(hsum, o, mlstm_norm_g[slot].reshape(1, D), mlstm_w_out, slot, y, mods, mrow_l)
        w_route = jnp.concatenate([moe_w_group[i], moe_w_router[i],
                                   jnp.zeros((D, LANE - N_GROUPS - N_EXPERTS), F32)], axis=1)
        b_route = jnp.concatenate([moe_b_group[i], moe_b_router[i],
                                   jnp.zeros((LANE - N_GROUPS - N_EXPERTS,), F32)]).reshape(1, LANE)
        x2, ewt, dest, cnt = _route(y, mods, mrow_l, norm2_g[i].reshape(1, D), w_route.astype(BF16), b_route)
        blk_start, n_blk, n_used = _block_tables_of(cnt)
        x_sorted = _sc_dispatch(x2, dest)
        y_sorted = _experts(x_sorted, blk_start, n_blk, n_used, moe_w_gate, moe_w_up, moe_w_down, i)
        ym = _sc_collect(y_sorted, dest)
        ym = ym.reshape(N_ASSIGN * SLAB, LANE)
        fg = final_norm_g.reshape(1, D)
        if i + 1 < DEPTH and (i + 1) % 3 == 0:
            y, u = _combine_conv_in(ym, ewt, y, mods, mods_all[i + 1], mrow, norm1_g[i + 1].reshape(1, D),
                                    conv_w_in, (i + 1) // 3)
        elif i + 1 < DEPTH and (i + 1) % 3 == 2:
            pending = (ym, ewt, mods)
        elif i + 1 < DEPTH:
            y = _combine(ym, ewt, y, mods, mrow_l, fg, 0, NL, False)
        else:
            y_prompt = _combine(ym, ewt, y, mods, mrow_l, fg, 0, NLP, True).reshape(BATCH, SEQ, D)
            y_sample = _combine(ym, ewt, y, mods, mrow_l, fg, NLP, NL - NLP, True).reshape(DEC_BATCH, DEC_SEQ, D)
    return (y_prompt, y_sample, new_k, new_v, new_c, new_n, new_m)
```
